```python
import math
import jax
import jax.numpy as jnp
from jax import lax
import numpy as np

D_MODEL = 2048
BATCH = 8
SEQ = 8192
DEPTH = 4

N_MIXERS = 3
MIX_WIDTH = D_MODEL
HEAD_DIM = 128
MEM_LEN = 256
MEM_HEADS = 4
MEM_WIDTH = MEM_HEADS * HEAD_DIM
MIXER_WIDTH = MIX_WIDTH - MEM_WIDTH
D_FF = 4 * D_MODEL
EPS = 1e-6
S5_GROUP = 16
S5_GROUPS = MIXER_WIDTH // S5_GROUP
S5_STATE = 64
S5_CHUNK = 128
GDN_HEADS = MIXER_WIDTH // HEAD_DIM
GDN_CONV = 4
GDN_CHUNK = 64
FOX_HEADS = MIXER_WIDTH // HEAD_DIM
FOX_BLOCK = 128
N_S5 = (DEPTH + 2) // N_MIXERS
N_GDN = (DEPTH + 1) // N_MIXERS
N_FOX = DEPTH // N_MIXERS
S5_IN = MIXER_WIDTH + MEM_WIDTH
GDN_IN = 4 * MIXER_WIDTH + 2 * GDN_HEADS + MEM_WIDTH
FOX_IN = 3 * MIXER_WIDTH + FOX_HEADS + MEM_WIDTH

kernel_name = 'hybrid_s5_gdn_fox_memory_trunk'


def _rmsnorm(x, gain):
    x32 = x.astype(jnp.float32)
    y = x32 * lax.rsqrt(jnp.mean(x32 * x32, axis=-1, keepdims=True) + EPS)
    return (y * gain.astype(jnp.float32)).astype(x.dtype)


def _l2norm(x):
    return x * lax.rsqrt(jnp.sum(x * x, axis=-1, keepdims=True) + EPS)


def _complex_affine_combine(e1, e2):
    a1r, a1i, b1r, b1i = e1
    a2r, a2i, b2r, b2i = e2
    return (a1r * a2r - a1i * a2i,
            a1r * a2i + a1i * a2r,
            a2r * b1r - a2i * b1i + b2r,
            a2r * b1i + a2i * b1r + b2i)


def _s5_mixer(u, lam_re, lam_im, log_dt, b_re, b_im, c_re, c_im, d_skip, w_glu, b_glu):
    bsz, seq, _ = u.shape
    f32 = jnp.float32
    n_chunks = seq // S5_CHUNK
    l_re, l_im = lam_re.astype(f32), lam_im.astype(f32)
    dt = jnp.exp(log_dt.astype(f32))[:, None]
    mag = jnp.exp(l_re * dt)
    a_re, a_im = mag * jnp.cos(l_im * dt), mag * jnp.sin(l_im * dt)
    den = l_re * l_re + l_im * l_im
    z_re = ((a_re - 1.0) * l_re + a_im * l_im) / den
    z_im = (a_im * l_re - (a_re - 1.0) * l_im) / den
    br, bi = b_re.astype(f32), b_im.astype(f32)
    bb_re = z_re[..., None] * br - z_im[..., None] * bi
    bb_im = z_re[..., None] * bi + z_im[..., None] * br
    cr, ci = c_re.astype(f32), c_im.astype(f32)
    blk = (bsz, S5_CHUNK, S5_GROUPS, S5_STATE)
    a_blk_re, a_blk_im = jnp.broadcast_to(a_re, blk), jnp.broadcast_to(a_im, blk)
    u32 = u.astype(f32)
    u_chunks = u32.reshape(bsz, n_chunks, S5_CHUNK, S5_GROUPS, S5_GROUP).transpose(1, 0, 2, 3, 4)

    def step(carry, uc):
        s_re, s_im = carry
        bu_re = jnp.einsum('blgc,gpc->blgp', uc, bb_re)
        bu_im = jnp.einsum('blgc,gpc->blgp', uc, bb_im)
        p_re, p_im, h_re, h_im = lax.associative_scan(
            _complex_affine_combine, (a_blk_re, a_blk_im, bu_re, bu_im), axis=1)
        h_re = h_re + p_re * s_re[:, None] - p_im * s_im[:, None]
        h_im = h_im + p_re * s_im[:, None] + p_im * s_re[:, None]
        y = jnp.einsum('blgp,gcp->blgc', h_re, cr) - jnp.einsum('blgp,gcp->blgc', h_im, ci)
        return (h_re[:, -1], h_im[:, -1]), y

    zeros = jnp.zeros((bsz, S5_GROUPS, S5_STATE), f32)
    _, y = lax.scan(step, (zeros, zeros), u_chunks)
    y = y.transpose(1, 0, 2, 3, 4).reshape(bsz, seq, MIXER_WIDTH)
    y = jax.nn.gelu(y + d_skip.astype(f32) * u32).astype(u.dtype)
    return y * jax.nn.sigmoid(y @ w_glu + b_glu)


def _causal_depthwise_conv(x, w):
    k = w.shape[0]
    return lax.conv_general_dilated(
        x, w[:, None, :].astype(x.dtype), window_strides=(1,), padding=[(k - 1, 0)],
        dimension_numbers=('NWC', 'WIO', 'NWC'), feature_group_count=x.shape[-1])


def _gdn_mixer(proj, conv_w, a_log, dt_bias, o_norm):
    bsz, seq, _ = proj.shape
    f32 = jnp.float32
    wd, nh, hd, cl = MIXER_WIDTH, GDN_HEADS, HEAD_DIM, GDN_CHUNK
    nc = seq // cl
    qkv = jax.nn.silu(_causal_depthwise_conv(proj[..., :3 * wd], conv_w)).astype(f32)
    gate = proj[..., 3 * wd:4 * wd].astype(f32).reshape(bsz, seq, nh, hd)
    a_in = proj[..., 4 * wd:4 * wd + nh].astype(f32)
    b_in = proj[..., 4 * wd + nh:4 * wd + 2 * nh].astype(f32)
    q = _l2norm(qkv[..., :wd].reshape(bsz, seq, nh, hd)) * hd ** -0.5
    k = _l2norm(qkv[..., wd:2 * wd].reshape(bsz, seq, nh, hd))
    v = qkv[..., 2 * wd:].reshape(bsz, seq, nh, hd)
    beta = jax.nn.sigmoid(b_in)
    g = -jnp.exp(a_log.astype(f32)) * jax.nn.softplus(a_in + dt_bias.astype(f32))

    def chunks(t):
        return t.reshape(bsz, nc, cl, nh, -1).transpose(0, 3, 1, 2, 4)

    q, k, v = chunks(q), chunks(k), chunks(v)
    beta = chunks(beta[..., None])
    gc = jnp.cumsum(chunks(g[..., None])[..., 0], axis=-1)
    idx = jnp.arange(cl)
    lower = idx[:, None] >= idx[None, :]
    strict = idx[:, None] > idx[None, :]
    decay = jnp.exp(jnp.where(lower, gc[..., :, None] - gc[..., None, :], -jnp.inf))
    kb, vb = k * beta, v * beta
    lmat = jnp.where(strict, jnp.einsum('bhncd,bhnsd->bhncs', kb, k) * decay, 0.0)
    rhs = jnp.concatenate([vb, kb * jnp.exp(gc)[..., None]], axis=-1)
    sol = lax.linalg.triangular_solve(lmat + jnp.eye(cl, dtype=f32), rhs, left_side=True,
                                      lower=True, unit_diagonal=True)
    u_c, w_c = sol[..., :hd], sol[..., hd:]
    attn_in = jnp.where(lower, jnp.einsum('bhncd,bhnsd->bhncs', q, k) * decay, 0.0)
    q_dec = q * jnp.exp(gc)[..., None]
    k_dec = k * jnp.exp(gc[..., -1:] - gc)[..., None]
    g_last = jnp.exp(gc[..., -1])
    xs = tuple(jnp.moveaxis(t, 2, 0) for t in (u_c, w_c, attn_in, q_dec, k_dec, g_last))

    def step(state, inp):
        u_i, w_i, a_i, qd_i, kd_i, gl_i = inp
        v_new = u_i - jnp.einsum('bhck,bhkv->bhcv', w_i, state)
        out = jnp.einsum('bhck,bhkv->bhcv', qd_i, state) + jnp.einsum('bhcs,bhsv->bhcv', a_i, v_new)
        state = state * gl_i[..., None, None] + jnp.einsum('bhck,bhcv->bhkv', kd_i, v_new)
        return state, out

    _, o = lax.scan(step, jnp.zeros((bsz, nh, hd, hd), f32), xs)
    o = o.transpose(1, 0, 3, 2, 4).reshape(bsz, seq, nh, hd)
    o = _rmsnorm(o, o_norm) * jax.nn.silu(gate)
    return o.reshape(bsz, seq, wd).astype(proj.dtype)


def _fox_mixer(proj, b_f):
    bsz, seq, _ = proj.shape
    f32 = jnp.float32
    wd, nh, hd = MIXER_WIDTH, FOX_HEADS, HEAD_DIM

    def heads(t):
        return t.reshape(bsz, seq, nh, hd).transpose(0, 2, 1, 3)

    q, k, v = heads(proj[..., :wd]), heads(proj[..., wd:2 * wd]), heads(proj[..., 2 * wd:3 * wd])
    f_logit = proj[..., 3 * wd:3 * wd + nh].astype(f32) + b_f.astype(f32)
    cum_f = jnp.cumsum(jax.nn.log_sigmoid(f_logit), axis=1).transpose(0, 2, 1)
    scale = hd ** -0.5
    q_off = jnp.arange(FOX_BLOCK)
    outs = []
    for blk in range(seq // FOX_BLOCK):
        q0, q1 = blk * FOX_BLOCK, (blk + 1) * FOX_BLOCK
        logits = (jnp.einsum('bhqd,bhkd->bhqk', q[:, :, q0:q1], k[:, :, :q1]).astype(f32) * scale
                  + cum_f[:, :, q0:q1, None] - cum_f[:, :, None, :q1])
        causal = (q0 + q_off)[:, None] >= jnp.arange(q1)[None, :]
        p = jax.nn.softmax(jnp.where(causal, logits, -jnp.inf), axis=-1).astype(v.dtype)
        outs.append(jnp.einsum('bhqk,bhkd->bhqd', p, v[:, :, :q1]))
    o = jnp.concatenate(outs, axis=2)
    return o.transpose(0, 2, 1, 3).reshape(bsz, seq, wd)


def _memory_attention(q_proj, mem_k, mem_v):
    bsz, seq, _ = q_proj.shape
    q = q_proj.reshape(bsz, seq, MEM_HEADS, HEAD_DIM)
    logits = jnp.einsum('bshd,bmhd->bhsm', q, mem_k).astype(jnp.float32) * HEAD_DIM ** -0.5
    p = jax.nn.softmax(logits, axis=-1).astype(mem_v.dtype)
    return jnp.einsum('bhsm,bmhd->bshd', p, mem_v).reshape(bsz, seq, MEM_WIDTH)


def _fwd_setup_inputs(seed: int = 0) -> dict:
    key = jax.random.key(seed)
    keys = iter(jax.random.split(key, 32))
    f32 = jnp.float32

    def normal(shape, scale):
        return scale * jax.random.normal(next(keys), shape, f32)

    def gain(shape):
        return 1.0 + normal(shape, 0.05)

    def uniform(shape, lo, hi):
        return jax.random.uniform(next(keys), shape, f32, lo, hi)

    x = normal((BATCH, SEQ, D_MODEL), 1.0)
    mem = normal((BATCH, MEM_LEN, D_MODEL), 1.0)
    mem_norm = gain((D_MODEL,))
    w_mem_kv = normal((D_MODEL, 2 * MEM_WIDTH), D_MODEL ** -0.5)
    norm1 = gain((DEPTH, D_MODEL))
    w_out = normal((DEPTH, MIX_WIDTH, D_MODEL), MIX_WIDTH ** -0.5)
    norm2 = gain((DEPTH, D_MODEL))
    w_up = normal((DEPTH, D_MODEL, D_FF), D_MODEL ** -0.5)
    w_down = normal((DEPTH, D_FF, D_MODEL), D_FF ** -0.5)
    norm_f = gain((D_MODEL,))
    s5_w_in = normal((N_S5, D_MODEL, S5_IN), D_MODEL ** -0.5)
    s5_lam_re = -0.5 * jnp.exp(normal((N_S5, S5_GROUPS, S5_STATE), 0.05))
    s5_lam_im = math.pi * jnp.arange(S5_STATE, dtype=f32) + normal((N_S5, S5_GROUPS, S5_STATE), 0.01)
    s5_log_dt = uniform((N_S5, S5_GROUPS), math.log(1e-3), math.log(1e-1))
    s5_b_re = normal((N_S5, S5_GROUPS, S5_STATE, S5_GROUP), (2 * S5_GROUP) ** -0.5)
    s5_b_im = normal((N_S5, S5_GROUPS, S5_STATE, S5_GROUP), (2 * S5_GROUP) ** -0.5)
    s5_c_re = normal((N_S5, S5_GROUPS, S5_GROUP, S5_STATE), S5_STATE ** -0.5)
    s5_c_im = normal((N_S5, S5_GROUPS, S5_GROUP, S5_STATE), S5_STATE ** -0.5)
    s5_d_skip = normal((N_S5, MIXER_WIDTH), 1.0)
    s5_w_glu = normal((N_S5, MIXER_WIDTH, MIXER_WIDTH), MIXER_WIDTH ** -0.5)
    s5_b_glu = normal((N_S5, MIXER_WIDTH), 0.01)
    gdn_w_in = normal((N_GDN, D_MODEL, GDN_IN), D_MODEL ** -0.5)
    gdn_conv_w = normal((N_GDN, GDN_CONV, 3 * MIXER_WIDTH), GDN_CONV ** -0.5)
    gdn_a_log = jnp.log(uniform((N_GDN, GDN_HEADS), 1.0, 16.0))
    dt = jnp.exp(uniform((N_GDN, GDN_HEADS), math.log(1e-3), math.log(1e-1)))
    gdn_dt_bias = dt + jnp.log(-jnp.expm1(-dt))
    gdn_o_norm = gain((N_GDN, HEAD_DIM))
    fox_w_in = normal((N_FOX, D_MODEL, FOX_IN), D_MODEL ** -0.5)
    fox_b_f = uniform((N_FOX, FOX_HEADS), 1.0, 6.0)
    return {'x': x, 'mem': mem, 'mem_norm': mem_norm, 'w_mem_kv': w_mem_kv,
            'norm1': norm1, 'w_out': w_out, 'norm2': norm2, 'w_up': w_up, 'w_down': w_down,
            'norm_f': norm_f,
            's5_w_in': s5_w_in, 's5_lam_re': s5_lam_re, 's5_lam_im': s5_lam_im,
            's5_log_dt': s5_log_dt, 's5_b_re': s5_b_re, 's5_b_im': s5_b_im,
            's5_c_re': s5_c_re, 's5_c_im': s5_c_im, 's5_d_skip': s5_d_skip,
            's5_w_glu': s5_w_glu, 's5_b_glu': s5_b_glu,
            'gdn_w_in': gdn_w_in, 'gdn_conv_w': gdn_conv_w, 'gdn_a_log': gdn_a_log,
            'gdn_dt_bias': gdn_dt_bias, 'gdn_o_norm': gdn_o_norm,
            'fox_w_in': fox_w_in, 'fox_b_f': fox_b_f}


def _fwd_reference(x, mem, mem_norm, w_mem_kv, norm1, w_out, norm2, w_up, w_down, norm_f,
              s5_w_in, s5_lam_re, s5_lam_im, s5_log_dt, s5_b_re, s5_b_im, s5_c_re, s5_c_im,
              s5_d_skip, s5_w_glu, s5_b_glu,
              gdn_w_in, gdn_conv_w, gdn_a_log, gdn_dt_bias, gdn_o_norm,
              fox_w_in, fox_b_f):
    bsz = x.shape[0]
    mkv = _rmsnorm(mem, mem_norm) @ w_mem_kv
    mem_k = mkv[..., :MEM_WIDTH].reshape(bsz, MEM_LEN, MEM_HEADS, HEAD_DIM)
    mem_v = mkv[..., MEM_WIDTH:].reshape(bsz, MEM_LEN, MEM_HEADS, HEAD_DIM)
    h = x
    for i in range(DEPTH):
        kind, j = i % N_MIXERS, i // N_MIXERS
        a = _rmsnorm(h, norm1[i])
        if kind == 0:
            proj = a @ s5_w_in[j]
            mix = _s5_mixer(proj[..., :-MEM_WIDTH], s5_lam_re[j], s5_lam_im[j], s5_log_dt[j],
                            s5_b_re[j], s5_b_im[j], s5_c_re[j], s5_c_im[j], s5_d_skip[j],
                            s5_w_glu[j], s5_b_glu[j])
        elif kind == 1:
            proj = a @ gdn_w_in[j]
            mix = _gdn_mixer(proj[..., :-MEM_WIDTH], gdn_conv_w[j], gdn_a_log[j],
                             gdn_dt_bias[j], gdn_o_norm[j])
        else:
            proj = a @ fox_w_in[j]
            mix = _fox_mixer(proj[..., :-MEM_WIDTH], fox_b_f[j])
        read = _memory_attention(proj[..., -MEM_WIDTH:], mem_k, mem_v)
        h = h + jnp.concatenate([mix, read], axis=-1) @ w_out[i]
        a = _rmsnorm(h, norm2[i])
        h = h + jnp.square(jax.nn.relu(a @ w_up[i])) @ w_down[i]
    return _rmsnorm(h, norm_f)


import jax as _jax
import jax.numpy as _jnp

TWIN_FORMAT = 'train_step'
FWD_PARAMS = ['x', 'mem', 'mem_norm', 'w_mem_kv', 'norm1', 'w_out', 'norm2', 'w_up', 'w_down', 'norm_f', 's5_w_in', 's5_lam_re', 's5_lam_im', 's5_log_dt', 's5_b_re', 's5_b_im', 's5_c_re', 's5_c_im', 's5_d_skip', 's5_w_glu', 's5_b_glu', 'gdn_w_in', 'gdn_conv_w', 'gdn_a_log', 'gdn_dt_bias', 'gdn_o_norm', 'fox_w_in', 'fox_b_f']
TWIN_WEIGHTS = ['mem_norm', 'w_mem_kv', 'norm1', 'w_out', 'norm2', 'w_up', 'w_down', 'norm_f', 's5_w_in', 's5_lam_re', 's5_lam_im', 's5_log_dt', 's5_b_re', 's5_b_im', 's5_c_re', 's5_c_im', 's5_d_skip', 's5_w_glu', 's5_b_glu', 'gdn_w_in', 'gdn_conv_w', 'gdn_a_log', 'gdn_dt_bias', 'gdn_o_norm', 'fox_w_in', 'fox_b_f']
TWIN_DIFF_INPUT = 'x'
TWIN_INPUTS = ['x', 'mem', 'mem_norm', 'w_mem_kv', 'norm1', 'w_out', 'norm2', 'w_up', 'w_down', 'norm_f', 's5_w_in', 's5_lam_re', 's5_lam_im', 's5_log_dt', 's5_b_re', 's5_b_im', 's5_c_re', 's5_c_im', 's5_d_skip', 's5_w_glu', 's5_b_glu', 'gdn_w_in', 'gdn_conv_w', 'gdn_a_log', 'gdn_dt_bias', 'gdn_o_norm', 'fox_w_in', 'fox_b_f', 'loss_target', 'm_mem_norm', 'm_w_mem_kv', 'm_norm1', 'm_w_out', 'm_norm2', 'm_w_up', 'm_w_down', 'm_norm_f', 'm_s5_w_in', 'm_s5_lam_re', 'm_s5_lam_im', 'm_s5_log_dt', 'm_s5_b_re', 'm_s5_b_im', 'm_s5_c_re', 'm_s5_c_im', 'm_s5_d_skip', 'm_s5_w_glu', 'm_s5_b_glu', 'm_gdn_w_in', 'm_gdn_conv_w', 'm_gdn_a_log', 'm_gdn_dt_bias', 'm_gdn_o_norm', 'm_fox_w_in', 'm_fox_b_f', 'v_mem_norm', 'v_w_mem_kv', 'v_norm1', 'v_w_out', 'v_norm2', 'v_w_up', 'v_w_down', 'v_norm_f', 'v_s5_w_in', 'v_s5_lam_re', 'v_s5_lam_im', 'v_s5_log_dt', 'v_s5_b_re', 'v_s5_b_im', 'v_s5_c_re', 'v_s5_c_im', 'v_s5_d_skip', 'v_s5_w_glu', 'v_s5_b_glu', 'v_gdn_w_in', 'v_gdn_conv_w', 'v_gdn_a_log', 'v_gdn_dt_bias', 'v_gdn_o_norm', 'v_fox_w_in', 'v_fox_b_f']
TWIN_OUTPUTS = ['loss', 'grad_x', 'grad_mem_norm', 'grad_w_mem_kv', 'grad_norm1', 'grad_w_out', 'grad_norm2', 'grad_w_up', 'grad_w_down', 'grad_norm_f', 'grad_s5_w_in', 'grad_s5_lam_re', 'grad_s5_lam_im', 'grad_s5_log_dt', 'grad_s5_b_re', 'grad_s5_b_im', 'grad_s5_c_re', 'grad_s5_c_im', 'grad_s5_d_skip', 'grad_s5_w_glu', 'grad_s5_b_glu', 'grad_gdn_w_in', 'grad_gdn_conv_w', 'grad_gdn_a_log', 'grad_gdn_dt_bias', 'grad_gdn_o_norm', 'grad_fox_w_in', 'grad_fox_b_f', 'delta_mem_norm', 'delta_w_mem_kv', 'delta_norm1', 'delta_w_out', 'delta_norm2', 'delta_w_up', 'delta_w_down', 'delta_norm_f', 'delta_s5_w_in', 'delta_s5_lam_re', 'delta_s5_lam_im', 'delta_s5_log_dt', 'delta_s5_b_re', 'delta_s5_b_im', 'delta_s5_c_re', 'delta_s5_c_im', 'delta_s5_d_skip', 'delta_s5_w_glu', 'delta_s5_b_glu', 'delta_gdn_w_in', 'delta_gdn_conv_w', 'delta_gdn_a_log', 'delta_gdn_dt_bias', 'delta_gdn_o_norm', 'delta_fox_w_in', 'delta_fox_b_f', 'new_m_mem_norm', 'new_m_w_mem_kv', 'new_m_norm1', 'new_m_w_out', 'new_m_norm2', 'new_m_w_up', 'new_m_w_down', 'new_m_norm_f', 'new_m_s5_w_in', 'new_m_s5_lam_re', 'new_m_s5_lam_im', 'new_m_s5_log_dt', 'new_m_s5_b_re', 'new_m_s5_b_im', 'new_m_s5_c_re', 'new_m_s5_c_im', 'new_m_s5_d_skip', 'new_m_s5_w_glu', 'new_m_s5_b_glu', 'new_m_gdn_w_in', 'new_m_gdn_conv_w', 'new_m_gdn_a_log', 'new_m_gdn_dt_bias', 'new_m_gdn_o_norm', 'new_m_fox_w_in', 'new_m_fox_b_f', 'new_v_mem_norm', 'new_v_w_mem_kv', 'new_v_norm1', 'new_v_w_out', 'new_v_norm2', 'new_v_w_up', 'new_v_w_down', 'new_v_norm_f', 'new_v_s5_w_in', 'new_v_s5_lam_re', 'new_v_s5_lam_im', 'new_v_s5_log_dt', 'new_v_s5_b_re', 'new_v_s5_b_im', 'new_v_s5_c_re', 'new_v_s5_c_im', 'new_v_s5_d_skip', 'new_v_s5_w_glu', 'new_v_s5_b_glu', 'new_v_gdn_w_in', 'new_v_gdn_conv_w', 'new_v_gdn_a_log', 'new_v_gdn_dt_bias', 'new_v_gdn_o_norm', 'new_v_fox_w_in', 'new_v_fox_b_f']
TWIN_LEAF_KINDS = {'loss': 'loss', 'grad_x': 'grad_x', 'grad_mem_norm': 'grad_w', 'grad_w_mem_kv': 'grad_w', 'grad_norm1': 'grad_w', 'grad_w_out': 'grad_w', 'grad_norm2': 'grad_w', 'grad_w_up': 'grad_w', 'grad_w_down': 'grad_w', 'grad_norm_f': 'grad_w', 'grad_s5_w_in': 'grad_w', 'grad_s5_lam_re': 'grad_w', 'grad_s5_lam_im': 'grad_w', 'grad_s5_log_dt': 'grad_w', 'grad_s5_b_re': 'grad_w', 'grad_s5_b_im': 'grad_w', 'grad_s5_c_re': 'grad_w', 'grad_s5_c_im': 'grad_w', 'grad_s5_d_skip': 'grad_w', 'grad_s5_w_glu': 'grad_w', 'grad_s5_b_glu': 'grad_w', 'grad_gdn_w_in': 'grad_w', 'grad_gdn_conv_w': 'grad_w', 'grad_gdn_a_log': 'grad_w', 'grad_gdn_dt_bias': 'grad_w', 'grad_gdn_o_norm': 'grad_w', 'grad_fox_w_in': 'grad_w', 'grad_fox_b_f': 'grad_w', 'delta_mem_norm': 'delta_w', 'delta_w_mem_kv': 'delta_w', 'delta_norm1': 'delta_w', 'delta_w_out': 'delta_w', 'delta_norm2': 'delta_w', 'delta_w_up': 'delta_w', 'delta_w_down': 'delta_w', 'delta_norm_f': 'delta_w', 'delta_s5_w_in': 'delta_w', 'delta_s5_lam_re': 'delta_w', 'delta_s5_lam_im': 'delta_w', 'delta_s5_log_dt': 'delta_w', 'delta_s5_b_re': 'delta_w', 'delta_s5_b_im': 'delta_w', 'delta_s5_c_re': 'delta_w', 'delta_s5_c_im': 'delta_w', 'delta_s5_d_skip': 'delta_w', 'delta_s5_w_glu': 'delta_w', 'delta_s5_b_glu': 'delta_w', 'delta_gdn_w_in': 'delta_w', 'delta_gdn_conv_w': 'delta_w', 'delta_gdn_a_log': 'delta_w', 'delta_gdn_dt_bias': 'delta_w', 'delta_gdn_o_norm': 'delta_w', 'delta_fox_w_in': 'delta_w', 'delta_fox_b_f': 'delta_w', 'new_m_mem_norm': 'new_m', 'new_m_w_mem_kv': 'new_m', 'new_m_norm1': 'new_m', 'new_m_w_out': 'new_m', 'new_m_norm2': 'new_m', 'new_m_w_up': 'new_m', 'new_m_w_down': 'new_m', 'new_m_norm_f': 'new_m', 'new_m_s5_w_in': 'new_m', 'new_m_s5_lam_re': 'new_m', 'new_m_s5_lam_im': 'new_m', 'new_m_s5_log_dt': 'new_m', 'new_m_s5_b_re': 'new_m', 'new_m_s5_b_im': 'new_m', 'new_m_s5_c_re': 'new_m', 'new_m_s5_c_im': 'new_m', 'new_m_s5_d_skip': 'new_m', 'new_m_s5_w_glu': 'new_m', 'new_m_s5_b_glu': 'new_m', 'new_m_gdn_w_in': 'new_m', 'new_m_gdn_conv_w': 'new_m', 'new_m_gdn_a_log': 'new_m', 'new_m_gdn_dt_bias': 'new_m', 'new_m_gdn_o_norm': 'new_m', 'new_m_fox_w_in': 'new_m', 'new_m_fox_b_f': 'new_m', 'new_v_mem_norm': 'new_v', 'new_v_w_mem_kv': 'new_v', 'new_v_norm1': 'new_v', 'new_v_w_out': 'new_v', 'new_v_norm2': 'new_v', 'new_v_w_up': 'new_v', 'new_v_w_down': 'new_v', 'new_v_norm_f': 'new_v', 'new_v_s5_w_in': 'new_v', 'new_v_s5_lam_re': 'new_v', 'new_v_s5_lam_im': 'new_v', 'new_v_s5_log_dt': 'new_v', 'new_v_s5_b_re': 'new_v', 'new_v_s5_b_im': 'new_v', 'new_v_s5_c_re': 'new_v', 'new_v_s5_c_im': 'new_v', 'new_v_s5_d_skip': 'new_v', 'new_v_s5_w_glu': 'new_v', 'new_v_s5_b_glu': 'new_v', 'new_v_gdn_w_in': 'new_v', 'new_v_gdn_conv_w': 'new_v', 'new_v_gdn_a_log': 'new_v', 'new_v_gdn_dt_bias': 'new_v', 'new_v_gdn_o_norm': 'new_v', 'new_v_fox_w_in': 'new_v', 'new_v_fox_b_f': 'new_v'}


def _forward(args):
    return _fwd_reference(*[args[k] for k in FWD_PARAMS])


def _output_shape():
    def fwd():
        inp = _fwd_setup_inputs(0)
        return _fwd_reference(*[inp[k] for k in FWD_PARAMS])
    out = _jax.eval_shape(fwd)
    return out.shape, out.dtype

N_MICROBATCH = 1
ADAM_LR = 0.001
ADAM_B1 = 0.9
ADAM_B2 = 0.999
ADAM_EPS = 1e-08
ADAM_WD = 0.01
ADAM_STEP = 10
PER_EXAMPLE_BATCH_AXIS = {'x': 0, 'mem': 0, 'loss_target': 0}
SHARED_INPUTS = []
_WEIGHT_DTYPES = {'mem_norm': _jnp.float32, 'w_mem_kv': _jnp.float32, 'norm1': _jnp.float32, 'w_out': _jnp.float32, 'norm2': _jnp.float32, 'w_up': _jnp.float32, 'w_down': _jnp.float32, 'norm_f': _jnp.float32, 's5_w_in': _jnp.float32, 's5_lam_re': _jnp.float32, 's5_lam_im': _jnp.float32, 's5_log_dt': _jnp.float32, 's5_b_re': _jnp.float32, 's5_b_im': _jnp.float32, 's5_c_re': _jnp.float32, 's5_c_im': _jnp.float32, 's5_d_skip': _jnp.float32, 's5_w_glu': _jnp.float32, 's5_b_glu': _jnp.float32, 'gdn_w_in': _jnp.float32, 'gdn_conv_w': _jnp.float32, 'gdn_a_log': _jnp.float32, 'gdn_dt_bias': _jnp.float32, 'gdn_o_norm': _jnp.float32, 'fox_w_in': _jnp.float32, 'fox_b_f': _jnp.float32}
MOMENT_SCALE = {'mem_norm': 1.883634e-02, 'w_mem_kv': 2.502990e-02, 'norm1': 7.946222e-02, 'w_out': 1.014365e-01, 'norm2': 1.336735e-01, 'w_up': 6.854095e-02, 'w_down': 2.600304e-01, 'norm_f': 3.303907e+01, 's5_w_in': 4.621950e-02, 's5_lam_re': 8.037939e-03, 's5_lam_im': 9.526045e-03, 's5_log_dt': 2.944412e+00, 's5_b_re': 3.820988e-03, 's5_b_im': 4.084142e-03, 's5_c_re': 5.321417e-03, 's5_c_im': 6.234661e-03, 's5_d_skip': 9.750077e-02, 's5_w_glu': 1.752623e-02, 's5_b_glu': 4.157290e-02, 'gdn_w_in': 5.057723e-02, 'gdn_conv_w': 5.356643e-02, 'gdn_a_log': 1.649724e-01, 'gdn_dt_bias': 1.630909e-01, 'gdn_o_norm': 2.977397e-01, 'fox_w_in': 9.586634e-02, 'fox_b_f': 1.789626e-01}


def _to_microbatches(a, axis):
    t = _jnp.moveaxis(a, axis, 0)
    t = t.reshape((N_MICROBATCH, t.shape[0] // N_MICROBATCH) + t.shape[1:])
    return _jnp.moveaxis(t, 1, axis + 1)


def setup_inputs(seed: int = 0) -> dict:
    inp = _fwd_setup_inputs(seed)
    key = _jax.random.fold_in(_jax.random.key(seed), 7919)
    shape, _ = _output_shape()
    out = dict(inp)
    out["loss_target"] = _jax.random.normal(_jax.random.fold_in(key, 0), shape, _jnp.float32)
    for i, name in enumerate(TWIN_WEIGHTS):
        w = inp[name].astype(_jnp.float32)
        if MOMENT_SCALE is None:
            s = _jnp.sqrt(_jnp.mean(_jnp.square(w)) + 1e-30)
        else:
            s = MOMENT_SCALE[name]
        km, kv = _jax.random.split(_jax.random.fold_in(key, i + 1))
        out[name] = w
        out["m_" + name] = s * _jax.random.normal(km, w.shape, _jnp.float32)
        out["v_" + name] = (s * s) * _jax.random.uniform(kv, w.shape, _jnp.float32, 0.5, 1.5)
    if N_MICROBATCH > 1:
        for name, axis in PER_EXAMPLE_BATCH_AXIS.items():
            out[name] = _to_microbatches(out[name], axis)
    return {'x': out['x'], 'mem': out['mem'], 'mem_norm': out['mem_norm'], 'w_mem_kv': out['w_mem_kv'], 'norm1': out['norm1'], 'w_out': out['w_out'], 'norm2': out['norm2'], 'w_up': out['w_up'], 'w_down': out['w_down'], 'norm_f': out['norm_f'], 's5_w_in': out['s5_w_in'], 's5_lam_re': out['s5_lam_re'], 's5_lam_im': out['s5_lam_im'], 's5_log_dt': out['s5_log_dt'], 's5_b_re': out['s5_b_re'], 's5_b_im': out['s5_b_im'], 's5_c_re': out['s5_c_re'], 's5_c_im': out['s5_c_im'], 's5_d_skip': out['s5_d_skip'], 's5_w_glu': out['s5_w_glu'], 's5_b_glu': out['s5_b_glu'], 'gdn_w_in': out['gdn_w_in'], 'gdn_conv_w': out['gdn_conv_w'], 'gdn_a_log': out['gdn_a_log'], 'gdn_dt_bias': out['gdn_dt_bias'], 'gdn_o_norm': out['gdn_o_norm'], 'fox_w_in': out['fox_w_in'], 'fox_b_f': out['fox_b_f'], 'loss_target': out['loss_target'], 'm_mem_norm': out['m_mem_norm'], 'm_w_mem_kv': out['m_w_mem_kv'], 'm_norm1': out['m_norm1'], 'm_w_out': out['m_w_out'], 'm_norm2': out['m_norm2'], 'm_w_up': out['m_w_up'], 'm_w_down': out['m_w_down'], 'm_norm_f': out['m_norm_f'], 'm_s5_w_in': out['m_s5_w_in'], 'm_s5_lam_re': out['m_s5_lam_re'], 'm_s5_lam_im': out['m_s5_lam_im'], 'm_s5_log_dt': out['m_s5_log_dt'], 'm_s5_b_re': out['m_s5_b_re'], 'm_s5_b_im': out['m_s5_b_im'], 'm_s5_c_re': out['m_s5_c_re'], 'm_s5_c_im': out['m_s5_c_im'], 'm_s5_d_skip': out['m_s5_d_skip'], 'm_s5_w_glu': out['m_s5_w_glu'], 'm_s5_b_glu': out['m_s5_b_glu'], 'm_gdn_w_in': out['m_gdn_w_in'], 'm_gdn_conv_w': out['m_gdn_conv_w'], 'm_gdn_a_log': out['m_gdn_a_log'], 'm_gdn_dt_bias': out['m_gdn_dt_bias'], 'm_gdn_o_norm': out['m_gdn_o_norm'], 'm_fox_w_in': out['m_fox_w_in'], 'm_fox_b_f': out['m_fox_b_f'], 'v_mem_norm': out['v_mem_norm'], 'v_w_mem_kv': out['v_w_mem_kv'], 'v_norm1': out['v_norm1'], 'v_w_out': out['v_w_out'], 'v_norm2': out['v_norm2'], 'v_w_up': out['v_w_up'], 'v_w_down': out['v_w_down'], 'v_norm_f': out['v_norm_f'], 'v_s5_w_in': out['v_s5_w_in'], 'v_s5_lam_re': out['v_s5_lam_re'], 'v_s5_lam_im': out['v_s5_lam_im'], 'v_s5_log_dt': out['v_s5_log_dt'], 'v_s5_b_re': out['v_s5_b_re'], 'v_s5_b_im': out['v_s5_b_im'], 'v_s5_c_re': out['v_s5_c_re'], 'v_s5_c_im': out['v_s5_c_im'], 'v_s5_d_skip': out['v_s5_d_skip'], 'v_s5_w_glu': out['v_s5_w_glu'], 'v_s5_b_glu': out['v_s5_b_glu'], 'v_gdn_w_in': out['v_gdn_w_in'], 'v_gdn_conv_w': out['v_gdn_conv_w'], 'v_gdn_a_log': out['v_gdn_a_log'], 'v_gdn_dt_bias': out['v_gdn_dt_bias'], 'v_gdn_o_norm': out['v_gdn_o_norm'], 'v_fox_w_in': out['v_fox_w_in'], 'v_fox_b_f': out['v_fox_b_f']}


def _loss(weights, diff, rest, loss_target):
    with _jax.named_scope("forward"):
        args = {**rest, TWIN_DIFF_INPUT: diff, **{k: w.astype(_WEIGHT_DTYPES[k]) for k, w in weights.items()}}
        y = _forward(args)
    with _jax.named_scope("loss_head"):
        err = _jnp.square(y.astype(_jnp.float32) - loss_target)
        return 0.5 * _jnp.sum(_jnp.mean(err, axis=-1)) if err.ndim else 0.5 * err


def _adamw(w, g, m, v):
    m = ADAM_B1 * m + (1.0 - ADAM_B1) * g
    v = ADAM_B2 * v + (1.0 - ADAM_B2) * _jnp.square(g)
    m_hat = m / (1.0 - ADAM_B1 ** ADAM_STEP)
    v_hat = v / (1.0 - ADAM_B2 ** ADAM_STEP)
    delta = -ADAM_LR * (m_hat / (_jnp.sqrt(v_hat) + ADAM_EPS) + ADAM_WD * w)
    return delta, m, v


def reference(x, mem, mem_norm, w_mem_kv, norm1, w_out, norm2, w_up, w_down, norm_f, s5_w_in, s5_lam_re, s5_lam_im, s5_log_dt, s5_b_re, s5_b_im, s5_c_re, s5_c_im, s5_d_skip, s5_w_glu, s5_b_glu, gdn_w_in, gdn_conv_w, gdn_a_log, gdn_dt_bias, gdn_o_norm, fox_w_in, fox_b_f, loss_target, m_mem_norm, m_w_mem_kv, m_norm1, m_w_out, m_norm2, m_w_up, m_w_down, m_norm_f, m_s5_w_in, m_s5_lam_re, m_s5_lam_im, m_s5_log_dt, m_s5_b_re, m_s5_b_im, m_s5_c_re, m_s5_c_im, m_s5_d_skip, m_s5_w_glu, m_s5_b_glu, m_gdn_w_in, m_gdn_conv_w, m_gdn_a_log, m_gdn_dt_bias, m_gdn_o_norm, m_fox_w_in, m_fox_b_f, v_mem_norm, v_w_mem_kv, v_norm1, v_w_out, v_norm2, v_w_up, v_w_down, v_norm_f, v_s5_w_in, v_s5_lam_re, v_s5_lam_im, v_s5_log_dt, v_s5_b_re, v_s5_b_im, v_s5_c_re, v_s5_c_im, v_s5_d_skip, v_s5_w_glu, v_s5_b_glu, v_gdn_w_in, v_gdn_conv_w, v_gdn_a_log, v_gdn_dt_bias, v_gdn_o_norm, v_fox_w_in, v_fox_b_f):
    given = dict(x=x, mem=mem, mem_norm=mem_norm, w_mem_kv=w_mem_kv, norm1=norm1, w_out=w_out, norm2=norm2, w_up=w_up, w_down=w_down, norm_f=norm_f, s5_w_in=s5_w_in, s5_lam_re=s5_lam_re, s5_lam_im=s5_lam_im, s5_log_dt=s5_log_dt, s5_b_re=s5_b_re, s5_b_im=s5_b_im, s5_c_re=s5_c_re, s5_c_im=s5_c_im, s5_d_skip=s5_d_skip, s5_w_glu=s5_w_glu, s5_b_glu=s5_b_glu, gdn_w_in=gdn_w_in, gdn_conv_w=gdn_conv_w, gdn_a_log=gdn_a_log, gdn_dt_bias=gdn_dt_bias, gdn_o_norm=gdn_o_norm, fox_w_in=fox_w_in, fox_b_f=fox_b_f, loss_target=loss_target, m_mem_norm=m_mem_norm, m_w_mem_kv=m_w_mem_kv, m_norm1=m_norm1, m_w_out=m_w_out, m_norm2=m_norm2, m_w_up=m_w_up, m_w_down=m_w_down, m_norm_f=m_norm_f, m_s5_w_in=m_s5_w_in, m_s5_lam_re=m_s5_lam_re, m_s5_lam_im=m_s5_lam_im, m_s5_log_dt=m_s5_log_dt, m_s5_b_re=m_s5_b_re, m_s5_b_im=m_s5_b_im, m_s5_c_re=m_s5_c_re, m_s5_c_im=m_s5_c_im, m_s5_d_skip=m_s5_d_skip, m_s5_w_glu=m_s5_w_glu, m_s5_b_glu=m_s5_b_glu, m_gdn_w_in=m_gdn_w_in, m_gdn_conv_w=m_gdn_conv_w, m_gdn_a_log=m_gdn_a_log, m_gdn_dt_bias=m_gdn_dt_bias, m_gdn_o_norm=m_gdn_o_norm, m_fox_w_in=m_fox_w_in, m_fox_b_f=m_fox_b_f, v_mem_norm=v_mem_norm, v_w_mem_kv=v_w_mem_kv, v_norm1=v_norm1, v_w_out=v_w_out, v_norm2=v_norm2, v_w_up=v_w_up, v_w_down=v_w_down, v_norm_f=v_norm_f, v_s5_w_in=v_s5_w_in, v_s5_lam_re=v_s5_lam_re, v_s5_lam_im=v_s5_lam_im, v_s5_log_dt=v_s5_log_dt, v_s5_b_re=v_s5_b_re, v_s5_b_im=v_s5_b_im, v_s5_c_re=v_s5_c_re, v_s5_c_im=v_s5_c_im, v_s5_d_skip=v_s5_d_skip, v_s5_w_glu=v_s5_w_glu, v_s5_b_glu=v_s5_b_glu, v_gdn_w_in=v_gdn_w_in, v_gdn_conv_w=v_gdn_conv_w, v_gdn_a_log=v_gdn_a_log, v_gdn_dt_bias=v_gdn_dt_bias, v_gdn_o_norm=v_gdn_o_norm, v_fox_w_in=v_fox_w_in, v_fox_b_f=v_fox_b_f)
    weights = {n: given[n] for n in TWIN_WEIGHTS}
    shared = {n: given[n] for n in SHARED_INPUTS}
    per_example = {n: given[n] for n in ['x', 'mem']}
    grad_fn = _jax.value_and_grad(_loss, argnums=(0, 1))

    def one_microbatch(ex, loss_target):
        ex = dict(ex)
        diff = ex.pop(TWIN_DIFF_INPUT)
        return grad_fn(weights, diff, {**shared, **ex}, loss_target)

    if N_MICROBATCH == 1:
        loss, (grad_w, grad_x) = one_microbatch(per_example, given["loss_target"])
    else:
        def body(carry, xs):
            loss_sum, grad_sum = carry
            l_k, (gw_k, gx_k) = one_microbatch(xs[0], xs[1])
            with _jax.named_scope("update"):
                return (loss_sum + l_k, _jax.tree.map(_jnp.add, grad_sum, gw_k)), gx_k

        init = (_jnp.zeros((), _jnp.float32), _jax.tree.map(_jnp.zeros_like, weights))
        (loss, grad_w), grad_x = _jax.lax.scan(body, init, (per_example, given["loss_target"]))
    with _jax.named_scope("update"):
        delta_w, new_m, new_v = {}, {}, {}
        for n in TWIN_WEIGHTS:
            delta_w[n], new_m[n], new_v[n] = _adamw(weights[n], grad_w[n], given["m_" + n], given["v_" + n])
    return (loss, grad_x, *[grad_w[n] for n in TWIN_WEIGHTS], *[delta_w[n] for n in TWIN_WEIGHTS],
            *[new_m[n] for n in TWIN_WEIGHTS], *[new_v[n] for n in TWIN_WEIGHTS])
```

```python
import functools
import math

import jax
import jax.numpy as jnp
from jax import lax
from jax.experimental import pallas as pl
from jax.experimental.pallas import tpu as pltpu

F32 = jnp.float32
MXU_DTYPE = jnp.bfloat16
EPS = 1e-6
HEAD = 128
S5_GROUP = 16
S5_STATE = 64
S5_SLAB = 256
S5_CHUNK = 128
GDN_CHUNK = 64
GDN_CONV = 4
LANES = 1024
VMEM_LIMIT_BYTES = 56 * 1024 * 1024
MESH = pl.DeviceIdType.MESH

ADAM_LR, ADAM_B1, ADAM_B2, ADAM_EPS, ADAM_WD, ADAM_STEP = 0.001, 0.9, 0.999, 1e-08, 0.01, 10

MM_TM, MM_TN, MM_TK = 1024, 1024, 512
ROW_TILE = 256
FOX_TILE = 512
MEM_TILE = 512
CONV_TILE = 1024
AG_ROWS = 8192
RS_ROWS = 8192

NN = (((1,), (0,)), ((), ()))
NT = (((1,), (1,)), ((), ()))
TN = (((0,), (0,)), ((), ()))

WEIGHTS = ['mem_norm', 'w_mem_kv', 'norm1', 'w_out', 'norm2', 'w_up', 'w_down', 'norm_f', 's5_w_in',
           's5_lam_re', 's5_lam_im', 's5_log_dt', 's5_b_re', 's5_b_im', 's5_c_re', 's5_c_im', 's5_d_skip',
           's5_w_glu', 's5_b_glu', 'gdn_w_in', 'gdn_conv_w', 'gdn_a_log', 'gdn_dt_bias', 'gdn_o_norm',
           'fox_w_in', 'fox_b_f']
SHARD_AXIS = {'w_mem_kv': 0, 'w_out': 1, 'w_up': 2, 'w_down': 1, 's5_w_in': 1, 's5_d_skip': 1,
              's5_w_glu': 1, 's5_b_glu': 1, 'gdn_w_in': 2, 'gdn_conv_w': 2, 'fox_w_in': 2}
MATMUL_WEIGHTS = ['w_mem_kv', 'w_out', 'w_up', 'w_down', 's5_w_in', 's5_w_glu', 'gdn_w_in', 'fox_w_in']
VECTOR_SHARDED = ['s5_d_skip', 's5_b_glu', 'gdn_conv_w']
REPLICATED = [n for n in WEIGHTS if n not in SHARD_AXIS]


def _tile(dim, target, align=128):
    if dim <= target:
        return dim
    t = (target // align) * align
    while t >= align:
        if dim % t == 0:
            return t
        t -= align
    return dim


def _cp(sem=None, **kw):
    return pltpu.CompilerParams(dimension_semantics=sem, vmem_limit_bytes=VMEM_LIMIT_BYTES, **kw)


def _dot(a, b, dims):
    return lax.dot_general(a.astype(MXU_DTYPE), b.astype(MXU_DTYPE), dims, preferred_element_type=F32)


def _dotf(a, b, dims):
    return lax.dot_general(a, b, dims, precision=lax.Precision.HIGHEST, preferred_element_type=F32)


def _sigmoid(x):
    return 1.0 / (1.0 + jnp.exp(-x))


def _softplus(x):
    return jnp.maximum(x, 0.0) + jnp.log(1.0 + jnp.exp(-jnp.abs(x)))


def _relu2(x):
    r = jnp.maximum(x, 0.0)
    return r * r


_GELU_C = math.sqrt(2.0 / math.pi)


def _gelu(x):
    return 0.5 * x * (1.0 + jnp.tanh(_GELU_C * (x + 0.044715 * x * x * x)))


def _gelu_grad(x):
    t = jnp.tanh(_GELU_C * (x + 0.044715 * x * x * x))
    return 0.5 * (1.0 + t) + 0.5 * x * (1.0 - t * t) * _GELU_C * (1.0 + 3.0 * 0.044715 * x * x)


def _silu_grad(x):
    s = _sigmoid(x)
    return s + x * s * (1.0 - s)


def _mm(a, b, *, name, ta=False, tb=False, a_pro=None, extras=(), epi=None, out_dtypes=(F32,),
        a_cols=None, tm=None, tn=None, tk=None):
    a_off = 0
    if a_cols is not None:
        a_off, K = a_cols
        M = a.shape[0]
    elif ta:
        K, M = a.shape
    else:
        M, K = a.shape
    N = b.shape[0] if tb else b.shape[1]
    assert (b.shape[1] if tb else b.shape[0]) == K, (a.shape, b.shape, ta, tb)
    tm = _tile(M, tm or MM_TM)
    tn = _tile(N, tn or MM_TN)
    tk = _tile(K, tk or MM_TK)
    assert a_off % tk == 0
    a_off_blocks = a_off // tk
    nk = K // tk
    n_ex, n_out = len(extras), len(out_dtypes)
    dims = TN if ta else (NT if tb else NN)

    def body(*refs):
        a_ref, b_ref = refs[0], refs[1]
        ex = refs[2:2 + n_ex]
        outs = refs[2 + n_ex:2 + n_ex + n_out]
        acc = refs[-1]
        k = pl.program_id(2)

        @pl.when(k == 0)
        def _():
            acc[...] = jnp.zeros_like(acc)

        at = a_ref[...]
        if a_pro is not None:
            at = a_pro(at)
        acc[...] += _dot(at, b_ref[...], dims)

        @pl.when(k == nk - 1)
        def _():
            res = acc[...]
            vals = epi(res, *[e[...] for e in ex]) if epi is not None else (res,)
            for o, v in zip(outs, vals):
                o[...] = v.astype(o.dtype)

    if ta:
        a_spec = pl.BlockSpec((tk, tm), lambda i, j, k: (k, i))
    else:
        a_spec = pl.BlockSpec((tm, tk), lambda i, j, k: (i, k + a_off_blocks))
    if tb:
        b_spec = pl.BlockSpec((tn, tk), lambda i, j, k: (j, k))
    else:
        b_spec = pl.BlockSpec((tk, tn), lambda i, j, k: (k, j))
    ex_specs, ex_arrays = [], []
    for arr, kind in extras:
        if kind == 'ij':
            ex_specs.append(pl.BlockSpec((tm, tn), lambda i, j, k: (i, j)))
            ex_arrays.append(arr)
        else:
            ex_specs.append(pl.BlockSpec((1, tn), lambda i, j, k: (0, j)))
            ex_arrays.append(arr.reshape(1, N))
    outs = pl.pallas_call(
        body, name=name, grid=(M // tm, N // tn, nk),
        in_specs=[a_spec, b_spec] + ex_specs,
        out_specs=[pl.BlockSpec((tm, tn), lambda i, j, k: (i, j)) for _ in out_dtypes],
        out_shape=[jax.ShapeDtypeStruct((M, N), dt) for dt in out_dtypes],
        scratch_shapes=[pltpu.VMEM((tm, tn), F32)],
        compiler_params=_cp(("parallel", "parallel", "arbitrary")),
    )(a, b, *ex_arrays)
    return outs[0] if n_out == 1 else tuple(outs)


def _rms_fwd(x, g, out_dtype, name):
    S, D = x.shape
    tr = _tile(S, ROW_TILE, 8)

    def body(x_ref, g_ref, o_ref):
        xv = x_ref[...]
        r = lax.rsqrt(jnp.mean(xv * xv, axis=-1, keepdims=True) + EPS)
        o_ref[...] = (xv * r * g_ref[...]).astype(o_ref.dtype)

    return pl.pallas_call(
        body, name=name, grid=(S // tr,),
        in_specs=[pl.BlockSpec((tr, D), lambda i: (i, 0)), pl.BlockSpec((1, D), lambda i: (0, 0))],
        out_specs=pl.BlockSpec((tr, D), lambda i: (i, 0)),
        out_shape=jax.ShapeDtypeStruct((S, D), out_dtype),
        compiler_params=_cp(("parallel",)),
    )(x, g.reshape(1, D))


def _rms_bwd(x, g, dy, res, name):
    S, D = x.shape
    tr = _tile(S, ROW_TILE, 8)
    has_res = res is not None

    def body(*refs):
        if has_res:
            x_ref, g_ref, dy_ref, res_ref, dx_ref, dg_ref = refs
        else:
            x_ref, g_ref, dy_ref, dx_ref, dg_ref = refs
        i = pl.program_id(0)

        @pl.when(i == 0)
        def _():
            dg_ref[...] = jnp.zeros_like(dg_ref)

        xv, d = x_ref[...], dy_ref[...].astype(F32)
        r = lax.rsqrt(jnp.mean(xv * xv, axis=-1, keepdims=True) + EPS)
        xh = xv * r
        t = d * g_ref[...]
        dx = r * (t - xh * jnp.mean(t * xh, axis=-1, keepdims=True))
        if has_res:
            dx = dx + res_ref[...]
        dx_ref[...] = dx
        dg_ref[...] += jnp.sum(d * xh, axis=0, keepdims=True)

    row = pl.BlockSpec((tr, D), lambda i: (i, 0))
    vec = pl.BlockSpec((1, D), lambda i: (0, 0))
    ins = [x, g.reshape(1, D), dy] + ([res] if has_res else [])
    return pl.pallas_call(
        body, name=name, grid=(S // tr,),
        in_specs=[row, vec, row] + ([row] if has_res else []),
        out_specs=[row, vec],
        out_shape=[jax.ShapeDtypeStruct((S, D), F32), jax.ShapeDtypeStruct((1, D), F32)],
        compiler_params=_cp(("arbitrary",)),
    )(*ins)


def _loss_head(h, g, target):
    S, D = h.shape
    tr = _tile(S, ROW_TILE, 8)

    def body(h_ref, g_ref, t_ref, loss_ref, dh_ref, dg_ref):
        i = pl.program_id(0)

        @pl.when(i == 0)
        def _():
            loss_ref[...] = jnp.zeros_like(loss_ref)
            dg_ref[...] = jnp.zeros_like(dg_ref)

        xv = h_ref[...]
        gv = g_ref[...]
        r = lax.rsqrt(jnp.mean(xv * xv, axis=-1, keepdims=True) + EPS)
        xh = xv * r
        err = xh * gv - t_ref[...]
        part = 0.5 * jnp.sum(jnp.mean(err * err, axis=-1, keepdims=True), axis=0, keepdims=True)
        loss_ref[...] += jnp.broadcast_to(part, loss_ref.shape)
        d = err * (1.0 / D)
        t = d * gv
        dh_ref[...] = r * (t - xh * jnp.mean(t * xh, axis=-1, keepdims=True))
        dg_ref[...] += jnp.sum(d * xh, axis=0, keepdims=True)

    row = pl.BlockSpec((tr, D), lambda i: (i, 0))
    vec = pl.BlockSpec((1, D), lambda i: (0, 0))
    return pl.pallas_call(
        body, name="loss_head", grid=(S // tr,),
        in_specs=[row, vec, row],
        out_specs=[pl.BlockSpec((8, 128), lambda i: (0, 0)), row, vec],
        out_shape=[jax.ShapeDtypeStruct((8, 128), F32), jax.ShapeDtypeStruct((S, D), F32),
                   jax.ShapeDtypeStruct((1, D), F32)],
        compiler_params=_cp(("arbitrary",)),
    )(h, g.reshape(1, D), target)


def _adamw(w, g, m, v, name):
    R, C = w.shape
    tr = _tile(R, max(8, (1 << 19) // max(C, 1) // 8 * 8), 8)
    c1 = 1.0 / (1.0 - ADAM_B1 ** ADAM_STEP)
    c2 = 1.0 / (1.0 - ADAM_B2 ** ADAM_STEP)

    def body(w_ref, g_ref, m_ref, v_ref, d_ref, nm_ref, nv_ref):
        gv = g_ref[...]
        nm = ADAM_B1 * m_ref[...] + (1.0 - ADAM_B1) * gv
        nv = ADAM_B2 * v_ref[...] + (1.0 - ADAM_B2) * (gv * gv)
        d_ref[...] = -ADAM_LR * ((nm * c1) / (jnp.sqrt(nv * c2) + ADAM_EPS) + ADAM_WD * w_ref[...])
        nm_ref[...] = nm
        nv_ref[...] = nv

    blk = pl.BlockSpec((tr, C), lambda i: (i, 0))
    return pl.pallas_call(
        body, name=name, grid=(R // tr,),
        in_specs=[blk] * 4, out_specs=[blk] * 3,
        out_shape=[jax.ShapeDtypeStruct((R, C), F32)] * 3,
        compiler_params=_cp(("parallel",)),
    )(w, g, m, v)


def _place():
    x, y, c = lax.axis_index("x"), lax.axis_index("y"), lax.axis_index("c")
    chips = [(1 - x, y), (x, 1 - y), (1 - x, 1 - y)]
    return x, y, c, chips


def _all_gather_chips(xs, name):
    r, n = xs.shape
    half = r // 2

    def body(x_ref, out_ref, send_sems, recv_sems, local_sem):
        x, y, c, chips = _place()
        me = 2 * x + y
        sibling = (x, y, 1 - c)

        def piece(chip, hc):
            return out_ref.at[chip, pl.ds(hc * half, half), :]

        def copy(k, src, dst, to):
            return pltpu.make_async_remote_copy(src_ref=src, dst_ref=dst, send_sem=send_sems.at[k],
                                                recv_sem=recv_sems.at[k], device_id=to, device_id_type=MESH)

        mine = pltpu.make_async_copy(x_ref, out_ref.at[me], local_sem)
        mine.start()
        src = x_ref.at[pl.ds(c * half, half), :]
        first = [copy(j, src, piece(me, c), (cx, cy, c)) for j, (cx, cy) in enumerate(chips)]
        for cp in first:
            cp.start()
        passed = []
        for j, (cx, cy) in enumerate(chips):
            got = piece(2 * cx + cy, c)
            copy(j, got, got, (cx, cy, c)).wait_recv()
            fwd = copy(3 + j, got, got, sibling)
            fwd.start()
            passed.append(fwd)
        for j, (cx, cy) in enumerate(chips):
            got = piece(2 * cx + cy, 1 - c)
            copy(3 + j, got, got, sibling).wait_recv()
        for cp in first + passed:
            cp.wait_send()
        mine.wait()

    return pl.pallas_call(
        body, name=name,
        in_specs=[pl.BlockSpec(memory_space=pl.ANY)],
        out_specs=pl.BlockSpec(memory_space=pl.ANY),
        out_shape=jax.ShapeDtypeStruct((4, r, n), xs.dtype),
        scratch_shapes=[pltpu.SemaphoreType.DMA((6,)), pltpu.SemaphoreType.DMA((6,)), pltpu.SemaphoreType.DMA],
    )(xs)


def _rs_swap_halves(g, name):
    _, r, n = g.shape
    half = r // 2

    def body(g_ref, out_ref, send_sem, recv_sem):
        x, y, c, _ = _place()
        cp = pltpu.make_async_remote_copy(
            src_ref=g_ref.at[:, pl.ds((1 - c) * half, half), :], dst_ref=out_ref,
            send_sem=send_sem, recv_sem=recv_sem, device_id=(x, y, 1 - c), device_id_type=MESH)
        cp.start()
        cp.wait()

    return pl.pallas_call(
        body, name=name,
        in_specs=[pl.BlockSpec(memory_space=pl.ANY)],
        out_specs=pl.BlockSpec(memory_space=pl.ANY),
        out_shape=jax.ShapeDtypeStruct((4, half, n), g.dtype),
        scratch_shapes=[pltpu.SemaphoreType.DMA, pltpu.SemaphoreType.DMA],
    )(g)


def _rs_add_halves(g, got, c_idx, name):
    _, r, n = g.shape
    half = r // 2
    tr = _tile(half, 512, 8)
    nb = half // tr

    def body(c_ref, g_ref, o_ref, out_ref):
        out_ref[...] = g_ref[...] + o_ref[...]

    return pl.pallas_call(
        body, name=name,
        grid_spec=pltpu.PrefetchScalarGridSpec(
            num_scalar_prefetch=1, grid=(4, nb),
            in_specs=[pl.BlockSpec((None, tr, n), lambda s, i, c: (s, c[0] * nb + i, 0)),
                      pl.BlockSpec((None, tr, n), lambda s, i, c: (s, i, 0))],
            out_specs=pl.BlockSpec((None, tr, n), lambda s, i, c: (s, i, 0))),
        out_shape=jax.ShapeDtypeStruct((4, half, n), F32),
        compiler_params=_cp(("parallel", "parallel")),
    )(c_idx, g, got)


def _rs_exchange_chips(p, name):
    _, h, n = p.shape

    def body(p_ref, out_ref, send_sems, recv_sems):
        x, y, c, chips = _place()
        copies = [pltpu.make_async_remote_copy(
            src_ref=p_ref.at[2 * cx + cy], dst_ref=out_ref.at[j], send_sem=send_sems.at[j],
            recv_sem=recv_sems.at[j], device_id=(cx, cy, c), device_id_type=MESH)
            for j, (cx, cy) in enumerate(chips)]
        for cp in copies:
            cp.start()
        for cp in copies:
            cp.wait()

    return pl.pallas_call(
        body, name=name,
        in_specs=[pl.BlockSpec(memory_space=pl.ANY)],
        out_specs=pl.BlockSpec(memory_space=pl.ANY),
        out_shape=jax.ShapeDtypeStruct((3, h, n), p.dtype),
        scratch_shapes=[pltpu.SemaphoreType.DMA((3,)), pltpu.SemaphoreType.DMA((3,))],
    )(p)


def _rs_add_chips(p, got, me_idx, name):
    _, h, n = p.shape
    tr = _tile(h, 512, 8)

    def body(me_ref, p_ref, a_ref, b_ref, c_ref, out_ref):
        out_ref[...] = ((p_ref[...] + a_ref[...]) + b_ref[...]) + c_ref[...]

    def got_spec(j):
        return pl.BlockSpec((None, tr, n), lambda i, me: (j, i, 0))

    return pl.pallas_call(
        body, name=name,
        grid_spec=pltpu.PrefetchScalarGridSpec(
            num_scalar_prefetch=1, grid=(h // tr,),
            in_specs=[pl.BlockSpec((None, tr, n), lambda i, me: (me[0], i, 0)),
                      got_spec(0), got_spec(1), got_spec(2)],
            out_specs=pl.BlockSpec((tr, n), lambda i, me: (i, 0))),
        out_shape=jax.ShapeDtypeStruct((h, n), F32),
        compiler_params=_cp(("parallel",)),
    )(me_idx, p, got, got, got)


def _rs_share_halves(q, name):
    h, n = q.shape

    def body(q_ref, out_ref, send_sem, recv_sem, local_sem):
        x, y, c, _ = _place()
        mine = pltpu.make_async_copy(q_ref, out_ref.at[pl.ds(c * h, h), :], local_sem)
        mine.start()
        cp = pltpu.make_async_remote_copy(
            src_ref=q_ref, dst_ref=out_ref.at[pl.ds(c * h, h), :], send_sem=send_sem, recv_sem=recv_sem,
            device_id=(x, y, 1 - c), device_id_type=MESH)
        cp.start()
        pltpu.make_async_remote_copy(
            src_ref=q_ref, dst_ref=out_ref.at[pl.ds((1 - c) * h, h), :], send_sem=send_sem, recv_sem=recv_sem,
            device_id=(x, y, 1 - c), device_id_type=MESH).wait_recv()
        cp.wait_send()
        mine.wait()

    return pl.pallas_call(
        body, name=name,
        in_specs=[pl.BlockSpec(memory_space=pl.ANY)],
        out_specs=pl.BlockSpec(memory_space=pl.ANY),
        out_shape=jax.ShapeDtypeStruct((2 * h, n), q.dtype),
        scratch_shapes=[pltpu.SemaphoreType.DMA, pltpu.SemaphoreType.DMA, pltpu.SemaphoreType.DMA],
    )(q)


def _reduce_scatter(g, c_idx, me_idx):
    got = _rs_swap_halves(g, "rs_swap_halves")
    p = _rs_add_halves(g, got, c_idx, "rs_add_halves")
    got = _rs_exchange_chips(p, "rs_exchange_chips")
    q = _rs_add_chips(p, got, me_idx, "rs_add_chips")
    return _rs_share_halves(q, "rs_share_halves")


def _all_reduce_small(v, name):
    R, n = v.shape

    def body(v_ref, out_ref, buf, send_sems, recv_sems):
        x, y, c, _ = _place()
        me = 4 * x + 2 * y + c
        buf[me] = v_ref[...]
        copies = []
        for d in range(1, 8):
            dx, dy, dc = (d >> 2) & 1, (d >> 1) & 1, d & 1
            px = x if dx == 0 else 1 - x
            py = y if dy == 0 else 1 - y
            pc = c if dc == 0 else 1 - c
            copies.append(pltpu.make_async_remote_copy(
                src_ref=v_ref, dst_ref=buf.at[me], send_sem=send_sems.at[d - 1], recv_sem=recv_sems.at[d - 1],
                device_id=(px, py, pc), device_id_type=MESH))
        for cp in copies:
            cp.start()
        for d in range(1, 8):
            dx, dy, dc = (d >> 2) & 1, (d >> 1) & 1, d & 1
            px = x if dx == 0 else 1 - x
            py = y if dy == 0 else 1 - y
            pc = c if dc == 0 else 1 - c
            pltpu.make_async_remote_copy(
                src_ref=v_ref, dst_ref=buf.at[4 * px + 2 * py + pc], send_sem=send_sems.at[d - 1],
                recv_sem=recv_sems.at[d - 1], device_id=(px, py, pc), device_id_type=MESH).wait_recv()
        for cp in copies:
            cp.wait_send()
        acc = buf[0]
        for k in range(1, 8):
            acc = acc + buf[k]
        out_ref[...] = acc

    return pl.pallas_call(
        body, name=name,
        in_specs=[pl.BlockSpec(memory_space=pltpu.VMEM)],
        out_specs=pl.BlockSpec(memory_space=pltpu.VMEM),
        out_shape=jax.ShapeDtypeStruct((R, n), F32),
        scratch_shapes=[pltpu.VMEM((8, R, n), F32), pltpu.SemaphoreType.DMA((7,)), pltpu.SemaphoreType.DMA((7,))],
        compiler_params=pltpu.CompilerParams(vmem_limit_bytes=VMEM_LIMIT_BYTES),
    )(v)


def _pack_rows(flat, rows_per_call):
    n = flat.shape[-1]
    per = rows_per_call * LANES
    calls = -(-n // per)
    pad = calls * per - n
    if pad:
        flat = jnp.pad(flat, [(0, 0)] * (flat.ndim - 1) + [(0, pad)])
    return flat.reshape(flat.shape[:-1] + (calls, rows_per_call, LANES)), n


def _mem_fwd(proj, q_blk, mkv, heads):
    S = proj.shape[0]
    ML = mkv.shape[0]
    t = _tile(S, MEM_TILE, 8)
    scale = HEAD ** -0.5

    def body(q_ref, k_ref, v_ref, o_ref):
        s = _dot(q_ref[...], k_ref[...], NT) * scale
        m = jnp.max(s, axis=-1, keepdims=True)
        e = jnp.exp(s - m)
        p = e / jnp.sum(e, axis=-1, keepdims=True)
        o_ref[...] = _dot(p, v_ref[...], NN)

    return pl.pallas_call(
        body, name="mem_fwd", grid=(S // t, heads),
        in_specs=[pl.BlockSpec((t, HEAD), lambda i, h: (i, q_blk + h)),
                  pl.BlockSpec((ML, HEAD), lambda i, h: (0, h)),
                  pl.BlockSpec((ML, HEAD), lambda i, h: (0, heads + h))],
        out_specs=pl.BlockSpec((t, HEAD), lambda i, h: (i, h)),
        out_shape=jax.ShapeDtypeStruct((S, heads * HEAD), F32),
        compiler_params=_cp(("parallel", "parallel")),
    )(proj, mkv, mkv)


def _mem_bwd(proj, q_blk, mkv, dcat, d_blk, heads):
    S = proj.shape[0]
    ML = mkv.shape[0]
    t = _tile(S, MEM_TILE, 8)
    scale = HEAD ** -0.5

    def body(q_ref, k_ref, v_ref, do_ref, dq_ref, dk_ref, dv_ref):
        i = pl.program_id(1)

        @pl.when(i == 0)
        def _():
            dk_ref[...] = jnp.zeros_like(dk_ref)
            dv_ref[...] = jnp.zeros_like(dv_ref)

        q, k, v, do = q_ref[...], k_ref[...], v_ref[...], do_ref[...]
        s = _dot(q, k, NT) * scale
        m = jnp.max(s, axis=-1, keepdims=True)
        e = jnp.exp(s - m)
        p = e / jnp.sum(e, axis=-1, keepdims=True)
        dp = _dot(do, v, NT)
        ds = p * (dp - jnp.sum(p * dp, axis=-1, keepdims=True))
        dq_ref[...] = _dot(ds, k, NN) * scale
        dk_ref[...] += _dot(ds, q, TN) * scale
        dv_ref[...] += _dot(p, do, TN)

    dq, dk, dv = pl.pallas_call(
        body, name="mem_bwd", grid=(heads, S // t),
        in_specs=[pl.BlockSpec((t, HEAD), lambda h, i: (i, q_blk + h)),
                  pl.BlockSpec((ML, HEAD), lambda h, i: (0, h)),
                  pl.BlockSpec((ML, HEAD), lambda h, i: (0, heads + h)),
                  pl.BlockSpec((t, HEAD), lambda h, i: (i, d_blk + h))],
        out_specs=[pl.BlockSpec((t, HEAD), lambda h, i: (i, h)),
                   pl.BlockSpec((ML, HEAD), lambda h, i: (0, h)),
                   pl.BlockSpec((ML, HEAD), lambda h, i: (0, h))],
        out_shape=[jax.ShapeDtypeStruct((S, heads * HEAD), F32),
                   jax.ShapeDtypeStruct((ML, heads * HEAD), F32),
                   jax.ShapeDtypeStruct((ML, heads * HEAD), F32)],
        compiler_params=_cp(("parallel", "arbitrary")),
    )(proj, mkv, mkv, dcat)
    return dq, jnp.concatenate([dk, dv], axis=1)


def _fox_gates(gl, bf):
    S = gl.shape[0]

    def body(g_ref, b_ref, o_ref):
        xv = g_ref[...] + b_ref[...]
        c = jnp.minimum(xv, 0.0) - jnp.log(1.0 + jnp.exp(-jnp.abs(xv)))
        row = lax.broadcasted_iota(jnp.int32, c.shape, 0)
        d = 1
        while d < S:
            c = c + jnp.where(row >= d, pltpu.roll(c, d, 0), 0.0)
            d *= 2
        o_ref[...] = c

    return pl.pallas_call(
        body, name="fox_gates", out_shape=jax.ShapeDtypeStruct((S, 128), F32),
        in_specs=[pl.BlockSpec(memory_space=pltpu.VMEM)] * 2,
        out_specs=pl.BlockSpec(memory_space=pltpu.VMEM),
        compiler_params=_cp(),
    )(gl, bf)


def _fox_gates_bwd(gl, bf, dcf):
    S = gl.shape[0]

    def body(g_ref, b_ref, d_ref, dg_ref, db_ref):
        c = d_ref[...]
        row = lax.broadcasted_iota(jnp.int32, c.shape, 0)
        d = 1
        while d < S:
            c = c + jnp.where(row < S - d, pltpu.roll(c, S - d, 0), 0.0)
            d *= 2
        dx = c * _sigmoid(-(g_ref[...] + b_ref[...]))
        dg_ref[...] = dx
        db_ref[...] = jnp.sum(dx, axis=0, keepdims=True)

    return pl.pallas_call(
        body, name="fox_gates_bwd",
        out_shape=[jax.ShapeDtypeStruct((S, 128), F32), jax.ShapeDtypeStruct((1, 128), F32)],
        in_specs=[pl.BlockSpec(memory_space=pltpu.VMEM)] * 3,
        out_specs=[pl.BlockSpec(memory_space=pltpu.VMEM)] * 2,
        compiler_params=_cp(),
    )(gl, bf, dcf)


def _fox_mask(s, qi, ki, t):
    row = qi * t + lax.broadcasted_iota(jnp.int32, (t, t), 0)
    col = ki * t + lax.broadcasted_iota(jnp.int32, (t, t), 1)
    return jnp.where(row >= col, s, -jnp.inf)


def _fox_fwd(proj, cfq, cfk, H):
    S = proj.shape[0]
    t = _tile(S, FOX_TILE)
    nq = S // t
    scale = HEAD ** -0.5

    def body(q_ref, k_ref, v_ref, cq_ref, ck_ref, o_ref, lse_ref, m_s, l_s, acc_s):
        qi, ki = pl.program_id(1), pl.program_id(2)

        @pl.when(ki == 0)
        def _():
            m_s[...] = jnp.full_like(m_s, -jnp.inf)
            l_s[...] = jnp.zeros_like(l_s)
            acc_s[...] = jnp.zeros_like(acc_s)

        @pl.when(ki <= qi)
        def _():
            s = _dot(q_ref[...], k_ref[...], NT) * scale + cq_ref[...] - ck_ref[...]
            s = _fox_mask(s, qi, ki, t)
            m_new = jnp.maximum(m_s[...], jnp.max(s, axis=-1, keepdims=True))
            alpha = jnp.exp(m_s[...] - m_new)
            p = jnp.exp(s - m_new)
            l_s[...] = alpha * l_s[...] + jnp.sum(p, axis=-1, keepdims=True)
            acc_s[...] = alpha * acc_s[...] + _dot(p, v_ref[...], NN)
            m_s[...] = m_new

        @pl.when(ki == nq - 1)
        def _():
            o_ref[...] = acc_s[...] / l_s[...]
            lse_ref[...] = m_s[...] + jnp.log(l_s[...])

    return pl.pallas_call(
        body, name="fox_fwd", grid=(H, nq, nq),
        in_specs=[pl.BlockSpec((t, HEAD), lambda h, qi, ki: (qi, h)),
                  pl.BlockSpec((t, HEAD), lambda h, qi, ki: (jnp.minimum(ki, qi), H + h)),
                  pl.BlockSpec((t, HEAD), lambda h, qi, ki: (jnp.minimum(ki, qi), 2 * H + h)),
                  pl.BlockSpec((None, t, 1), lambda h, qi, ki: (h, qi, 0)),
                  pl.BlockSpec((None, 1, t), lambda h, qi, ki: (h, 0, jnp.minimum(ki, qi)))],
        out_specs=[pl.BlockSpec((t, HEAD), lambda h, qi, ki: (qi, h)),
                   pl.BlockSpec((None, t, 1), lambda h, qi, ki: (h, qi, 0))],
        out_shape=[jax.ShapeDtypeStruct((S, H * HEAD), F32), jax.ShapeDtypeStruct((H, S, 1), F32)],
        scratch_shapes=[pltpu.VMEM((t, 1), F32), pltpu.VMEM((t, 1), F32), pltpu.VMEM((t, HEAD), F32)],
        compiler_params=_cp(("parallel", "parallel", "arbitrary")),
    )(proj, proj, proj, cfq, cfk)


def _fox_bwd_rowdot(proj, cfq, cfk, lse, dcat, H):
    S = proj.shape[0]
    t = _tile(S, FOX_TILE)
    nq = S // t
    scale = HEAD ** -0.5

    def body(q_ref, k_ref, v_ref, do_ref, lse_ref, cq_ref, ck_ref, d_ref):
        qi, ki = pl.program_id(1), pl.program_id(2)

        @pl.when(ki == 0)
        def _():
            d_ref[...] = jnp.zeros_like(d_ref)

        @pl.when(ki <= qi)
        def _():
            s = _dot(q_ref[...], k_ref[...], NT) * scale + cq_ref[...] - ck_ref[...]
            p = jnp.exp(_fox_mask(s, qi, ki, t) - lse_ref[...])
            dp = _dot(do_ref[...], v_ref[...], NT)
            d_ref[...] += jnp.sum(p * dp, axis=-1, keepdims=True)

    def kcol(h, qi, ki):
        return jnp.minimum(ki, qi)

    return pl.pallas_call(
        body, name="fox_bwd_rowdot", grid=(H, nq, nq),
        in_specs=[pl.BlockSpec((t, HEAD), lambda h, qi, ki: (qi, h)),
                  pl.BlockSpec((t, HEAD), lambda h, qi, ki: (kcol(h, qi, ki), H + h)),
                  pl.BlockSpec((t, HEAD), lambda h, qi, ki: (kcol(h, qi, ki), 2 * H + h)),
                  pl.BlockSpec((t, HEAD), lambda h, qi, ki: (qi, h)),
                  pl.BlockSpec((None, t, 1), lambda h, qi, ki: (h, qi, 0)),
                  pl.BlockSpec((None, t, 1), lambda h, qi, ki: (h, qi, 0)),
                  pl.BlockSpec((None, 1, t), lambda h, qi, ki: (h, 0, kcol(h, qi, ki)))],
        out_specs=pl.BlockSpec((None, t, 1), lambda h, qi, ki: (h, qi, 0)),
        out_shape=jax.ShapeDtypeStruct((H, S, 1), F32),
        compiler_params=_cp(("parallel", "parallel", "arbitrary")),
    )(proj, proj, proj, dcat, lse, cfq, cfk)


def _fox_bwd(proj, cfq, cfk, rowdot, lse, dcat, H):
    S = proj.shape[0]
    t = _tile(S, FOX_TILE)
    nq = S // t
    scale = HEAD ** -0.5

    def body(q_ref, k_ref, v_ref, dd_ref, do_ref, lse_ref, cq_ref, ck_ref, dq_ref, dk_ref, dv_ref, dck_ref):
        j, i = pl.program_id(1), pl.program_id(2)

        @pl.when((j == 0) & (i == 0))
        def _():
            dq_ref[...] = jnp.zeros_like(dq_ref)

        @pl.when(i == 0)
        def _():
            dk_ref[...] = jnp.zeros_like(dk_ref)
            dv_ref[...] = jnp.zeros_like(dv_ref)
            dck_ref[...] = jnp.zeros_like(dck_ref)

        @pl.when(i >= j)
        def _():
            q, k, v, do = q_ref[...], k_ref[...], v_ref[...], do_ref[...]
            s = _dot(q, k, NT) * scale + cq_ref[...] - ck_ref[...]
            s = _fox_mask(s, i, j, t)
            p = jnp.exp(s - lse_ref[...])
            dv_ref[...] += _dot(p, do, TN)
            dp = _dot(do, v, NT)
            ds = p * (dp - dd_ref[...])
            dk_ref[...] += _dot(ds, q, TN) * scale
            rows = pl.ds(pl.multiple_of(i * t, t), t)
            dq_ref[rows, :] += _dot(ds, k, NN) * scale
            dck_ref[...] -= jnp.sum(ds, axis=0, keepdims=True)

    def qrow(h, j, i):
        return jnp.maximum(i, j)

    return pl.pallas_call(
        body, name="fox_bwd", grid=(H, nq, nq),
        in_specs=[pl.BlockSpec((t, HEAD), lambda h, j, i: (qrow(h, j, i), h)),
                  pl.BlockSpec((t, HEAD), lambda h, j, i: (j, H + h)),
                  pl.BlockSpec((t, HEAD), lambda h, j, i: (j, 2 * H + h)),
                  pl.BlockSpec((None, t, 1), lambda h, j, i: (h, qrow(h, j, i), 0)),
                  pl.BlockSpec((t, HEAD), lambda h, j, i: (qrow(h, j, i), h)),
                  pl.BlockSpec((None, t, 1), lambda h, j, i: (h, qrow(h, j, i), 0)),
                  pl.BlockSpec((None, t, 1), lambda h, j, i: (h, qrow(h, j, i), 0)),
                  pl.BlockSpec((None, 1, t), lambda h, j, i: (h, 0, j))],
        out_specs=[pl.BlockSpec((S, HEAD), lambda h, j, i: (0, h)),
                   pl.BlockSpec((t, HEAD), lambda h, j, i: (j, h)),
                   pl.BlockSpec((t, HEAD), lambda h, j, i: (j, h)),
                   pl.BlockSpec((None, 1, t), lambda h, j, i: (h, 0, j))],
        out_shape=[jax.ShapeDtypeStruct((S, H * HEAD), F32)] * 3 + [jax.ShapeDtypeStruct((H, 1, S), F32)],
        compiler_params=_cp(("parallel", "arbitrary", "arbitrary")),
    )(proj, proj, proj, rowdot, dcat, lse, cfq, cfk)


def _s5_prep(lam_re, lam_im, log_dt, b_re, b_im, c_re, c_im):
    G, P = lam_re.shape
    ns = G // 16
    dt = jnp.exp(log_dt)[:, None]
    mag = jnp.exp(lam_re * dt)
    a_re, a_im = mag * jnp.cos(lam_im * dt), mag * jnp.sin(lam_im * dt)
    den = lam_re * lam_re + lam_im * lam_im
    z_re = ((a_re - 1.0) * lam_re + a_im * lam_im) / den
    z_im = (a_im * lam_re - (a_re - 1.0) * lam_im) / den
    bb_re = z_re[..., None] * b_re - z_im[..., None] * b_im
    bb_im = z_re[..., None] * b_im + z_im[..., None] * b_re
    eye = jnp.eye(16, dtype=F32)
    bb = jnp.stack([bb_re, bb_im]).reshape(2, ns, 16, P, S5_GROUP)
    wb = jnp.einsum('asgpc,gh->sgcahp', bb, eye).reshape(ns, S5_SLAB, 2 * 16 * P)
    cc = jnp.stack([c_re, -c_im]).reshape(2, ns, 16, S5_GROUP, P)
    wc = jnp.einsum('asgcp,gh->sagphc', cc, eye).reshape(ns, 2 * 16 * P, S5_SLAB)
    a = jnp.concatenate([a_re.reshape(ns, 1, 16 * P), a_im.reshape(ns, 1, 16 * P)], axis=-1)
    return wb, wc, a


def _s5_tables(lam_re, lam_im, log_dt):
    G, P = lam_re.shape
    ns = G // 16
    dt = jnp.exp(log_dt)[:, None]
    tt = jnp.arange(1, S5_CHUNK + 1, dtype=F32)[:, None, None]
    mag = jnp.exp(lam_re * dt * tt)
    ang = lam_im * dt * tt
    pr = (mag * jnp.cos(ang)).reshape(S5_CHUNK, ns, 16 * P).transpose(1, 0, 2)
    pi = (mag * jnp.sin(ang)).reshape(S5_CHUNK, ns, 16 * P).transpose(1, 0, 2)
    return pr, pi, pr[:, ::-1], pi[:, ::-1]


def _s5_scan_fwd(proj, wb, wc, pr, pi, dskip):
    S = proj.shape[0]
    ns = wb.shape[0]
    W = wb.shape[2]
    hw = W // 2
    T = S5_CHUNK
    nc = S // T
    mix = ns * S5_SLAB

    def body(u_ref, wb_ref, wc_ref, pr_ref, pi_ref, d_ref, v_ref, yg_ref, h_ref, cin_ref, carry):
        c = pl.program_id(1)

        @pl.when(c == 0)
        def _():
            carry[...] = jnp.zeros_like(carry)

        u = u_ref[...]
        bu = _dot(u, wb_ref[...], NN)
        xr, xi = bu[:, :hw], bu[:, hw:]
        row = lax.broadcasted_iota(jnp.int32, (T, hw), 0)
        d = 1
        while d < T:
            ar, ai = pr_ref[pl.ds(d - 1, 1), :], pi_ref[pl.ds(d - 1, 1), :]
            sr = jnp.where(row >= d, pltpu.roll(xr, d, 0), 0.0)
            si = jnp.where(row >= d, pltpu.roll(xi, d, 0), 0.0)
            xr, xi = xr + ar * sr - ai * si, xi + ar * si + ai * sr
            d *= 2
        cin_ref[...] = carry[...]
        cr, ci = carry[:, :hw], carry[:, hw:]
        pwr, pwi = pr_ref[...], pi_ref[...]
        hr = xr + pwr * cr - pwi * ci
        hi = xi + pwr * ci + pwi * cr
        h_ref[:, :hw] = hr
        h_ref[:, hw:] = hi
        carry[:, :hw] = hr[T - 1:T, :]
        carry[:, hw:] = hi[T - 1:T, :]
        y = _dot(h_ref[...], wc_ref[...], NN)
        v = y + d_ref[...] * u
        v_ref[...] = v
        yg_ref[...] = _gelu(v)

    return pl.pallas_call(
        body, name="s5_scan_fwd", grid=(ns, nc),
        in_specs=[pl.BlockSpec((T, S5_SLAB), lambda s, c: (c, s)),
                  pl.BlockSpec((None, S5_SLAB, W), lambda s, c: (s, 0, 0)),
                  pl.BlockSpec((None, W, S5_SLAB), lambda s, c: (s, 0, 0)),
                  pl.BlockSpec((None, T, hw), lambda s, c: (s, 0, 0)),
                  pl.BlockSpec((None, T, hw), lambda s, c: (s, 0, 0)),
                  pl.BlockSpec((1, S5_SLAB), lambda s, c: (0, s))],
        out_specs=[pl.BlockSpec((T, S5_SLAB), lambda s, c: (c, s)),
                   pl.BlockSpec((T, S5_SLAB), lambda s, c: (c, s)),
                   pl.BlockSpec((T, W), lambda s, c: (c, s)),
                   pl.BlockSpec((None, 1, W), lambda s, c: (c, 0, s))],
        out_shape=[jax.ShapeDtypeStruct((S, mix), F32), jax.ShapeDtypeStruct((S, mix), F32),
                   jax.ShapeDtypeStruct((S, ns * W), F32), jax.ShapeDtypeStruct((nc, 1, ns * W), F32)],
        scratch_shapes=[pltpu.VMEM((1, W), F32)],
        compiler_params=_cp(("parallel", "arbitrary")),
    )(proj, wb, wc, pr, pi, dskip)


def _s5_scan_bwd(dv, proj, hs, cin, wb, wc, pr, pi, prr, pir, dskip):
    S = proj.shape[0]
    ns = wb.shape[0]
    W = wb.shape[2]
    hw = W // 2
    T = S5_CHUNK
    nc = S // T
    mix = ns * S5_SLAB

    def body(dv_ref, u_ref, h_ref, cin_ref, wb_ref, wc_ref, pr_ref, pi_ref, prr_ref, pir_ref, d_ref,
             du_ref, dwb_ref, dwc_ref, da_ref, dd_ref, lam_s, carry):
        c = pl.program_id(1)

        @pl.when(c == 0)
        def _():
            carry[...] = jnp.zeros_like(carry)
            dwb_ref[...] = jnp.zeros_like(dwb_ref)
            dwc_ref[...] = jnp.zeros_like(dwc_ref)
            da_ref[...] = jnp.zeros_like(da_ref)
            dd_ref[...] = jnp.zeros_like(dd_ref)

        dy, u = dv_ref[...], u_ref[...]
        dh = _dot(dy, wc_ref[...], NT)
        gr, gi = dh[:, :hw], dh[:, hw:]
        row = lax.broadcasted_iota(jnp.int32, (T, hw), 0)
        d = 1
        while d < T:
            ar, ai = pr_ref[pl.ds(d - 1, 1), :], -pi_ref[pl.ds(d - 1, 1), :]
            sr = jnp.where(row < T - d, pltpu.roll(gr, T - d, 0), 0.0)
            si = jnp.where(row < T - d, pltpu.roll(gi, T - d, 0), 0.0)
            gr, gi = gr + ar * sr - ai * si, gi + ar * si + ai * sr
            d *= 2
        lr, li = carry[:, :hw], carry[:, hw:]
        pwr, pwi = prr_ref[...], -pir_ref[...]
        gr = gr + pwr * lr - pwi * li
        gi = gi + pwr * li + pwi * lr
        carry[:, :hw] = gr[0:1, :]
        carry[:, hw:] = gi[0:1, :]
        hr, hi = h_ref[:, :hw], h_ref[:, hw:]
        hpr = jnp.where(row >= 1, pltpu.roll(hr, 1, 0), cin_ref[:, :hw])
        hpi = jnp.where(row >= 1, pltpu.roll(hi, 1, 0), cin_ref[:, hw:])
        da_ref[:, :hw] += jnp.sum(hpr * gr + hpi * gi, axis=0, keepdims=True)
        da_ref[:, hw:] += jnp.sum(hpr * gi - hpi * gr, axis=0, keepdims=True)
        lam_s[:, :hw] = gr
        lam_s[:, hw:] = gi
        lam = lam_s[...]
        du_ref[...] = _dot(lam, wb_ref[...], NT) + dy * d_ref[...]
        dwb_ref[...] += _dot(u, lam, TN)
        dwc_ref[...] += _dot(h_ref[...], dy, TN)
        dd_ref[...] += jnp.sum(dy * u, axis=0, keepdims=True)

    def rc(c):
        return nc - 1 - c

    return pl.pallas_call(
        body, name="s5_scan_bwd", grid=(ns, nc),
        in_specs=[pl.BlockSpec((T, S5_SLAB), lambda s, c: (rc(c), s)),
                  pl.BlockSpec((T, S5_SLAB), lambda s, c: (rc(c), s)),
                  pl.BlockSpec((T, W), lambda s, c: (rc(c), s)),
                  pl.BlockSpec((None, 1, W), lambda s, c: (rc(c), 0, s)),
                  pl.BlockSpec((None, S5_SLAB, W), lambda s, c: (s, 0, 0)),
                  pl.BlockSpec((None, W, S5_SLAB), lambda s, c: (s, 0, 0)),
                  pl.BlockSpec((None, T, hw), lambda s, c: (s, 0, 0)),
                  pl.BlockSpec((None, T, hw), lambda s, c: (s, 0, 0)),
                  pl.BlockSpec((None, T, hw), lambda s, c: (s, 0, 0)),
                  pl.BlockSpec((None, T, hw), lambda s, c: (s, 0, 0)),
                  pl.BlockSpec((1, S5_SLAB), lambda s, c: (0, s))],
        out_specs=[pl.BlockSpec((T, S5_SLAB), lambda s, c: (rc(c), s)),
                   pl.BlockSpec((None, S5_SLAB, W), lambda s, c: (s, 0, 0)),
                   pl.BlockSpec((None, W, S5_SLAB), lambda s, c: (s, 0, 0)),
                   pl.BlockSpec((None, 1, W), lambda s, c: (s, 0, 0)),
                   pl.BlockSpec((1, S5_SLAB), lambda s, c: (0, s))],
        out_shape=[jax.ShapeDtypeStruct((S, mix), F32), jax.ShapeDtypeStruct(wb.shape, F32),
                   jax.ShapeDtypeStruct(wc.shape, F32), jax.ShapeDtypeStruct((ns, 1, W), F32),
                   jax.ShapeDtypeStruct((1, mix), F32)],
        scratch_shapes=[pltpu.VMEM((T, W), F32), pltpu.VMEM((1, W), F32)],
        compiler_params=_cp(("parallel", "arbitrary")),
    )(dv, proj, hs, cin, wb, wc, pr, pi, prr, pir, dskip)


def _s5_glu_bwd(dcat, yg, z):
    S, mix = yg.shape
    tr = _tile(S, ROW_TILE, 8)

    def body(do_ref, yg_ref, z_ref, dz_ref, dy_ref, db_ref):
        i = pl.program_id(0)

        @pl.when(i == 0)
        def _():
            db_ref[...] = jnp.zeros_like(db_ref)

        do, yg_, sz = do_ref[...], yg_ref[...], _sigmoid(z_ref[...])
        dz = do * yg_ * sz * (1.0 - sz)
        dz_ref[...] = dz
        dy_ref[...] = do * sz
        db_ref[...] += jnp.sum(dz, axis=0, keepdims=True)

    blk = pl.BlockSpec((tr, mix), lambda i: (i, 0))
    return pl.pallas_call(
        body, name="s5_glu_bwd", grid=(S // tr,),
        in_specs=[blk, blk, blk], out_specs=[blk, blk, pl.BlockSpec((1, mix), lambda i: (0, 0))],
        out_shape=[jax.ShapeDtypeStruct((S, mix), F32), jax.ShapeDtypeStruct((S, mix), F32),
                   jax.ShapeDtypeStruct((1, mix), F32)],
        compiler_params=_cp(("arbitrary",)),
    )(dcat, yg, z)


def _rows_down(x, j):
    return x if j == 0 else pltpu.roll(x, j, 0)


def _conv_rows(xe, w_ref, n):
    c = None
    for j in range(GDN_CONV):
        term = w_ref[pl.ds(GDN_CONV - 1 - j, 1), :] * _rows_down(xe, j)[8:8 + n, :]
        c = term if c is None else c + term
    return c


def _gdn_prep(proj, blk0, nblk, convw, norm, scale, name):
    S = proj.shape[0]
    tr = _tile(S, CONV_TILE, 8)
    nb8 = tr // 8

    def body(x_ref, xb_ref, w_ref, o_ref):
        i = pl.program_id(1)
        xe = jnp.concatenate([jnp.where(i == 0, 0.0, xb_ref[...]), x_ref[...]], axis=0)
        c = _conv_rows(xe, w_ref, tr)
        s = c * _sigmoid(c)
        if norm:
            s = s * lax.rsqrt(jnp.sum(s * s, axis=-1, keepdims=True) + EPS) * scale
        o_ref[...] = s

    return pl.pallas_call(
        body, name=name, grid=(nblk, S // tr),
        in_specs=[pl.BlockSpec((tr, HEAD), lambda j, i: (i, blk0 + j)),
                  pl.BlockSpec((8, HEAD), lambda j, i: (jnp.maximum(i * nb8 - 1, 0), blk0 + j)),
                  pl.BlockSpec((GDN_CONV, HEAD), lambda j, i: (0, j))],
        out_specs=pl.BlockSpec((tr, HEAD), lambda j, i: (i, j)),
        out_shape=jax.ShapeDtypeStruct((S, nblk * HEAD), F32),
        compiler_params=_cp(("parallel", "parallel")),
    )(proj, proj, convw)


def _gdn_prep_bwd(proj, blk0, nblk, convw, dout, norm, scale, name):
    S = proj.shape[0]
    tr = _tile(S, CONV_TILE, 8)
    nb8 = tr // 8
    last8 = S // 8 - 1
    nrow = S // tr

    def body(x_ref, xb_ref, xa_ref, w_ref, d_ref, da_ref, dx_ref, dw_ref):
        i = pl.program_id(1)

        @pl.when(i == 0)
        def _():
            dw_ref[...] = jnp.zeros_like(dw_ref)

        xe = jnp.concatenate([jnp.where(i == 0, 0.0, xb_ref[...]), x_ref[...], xa_ref[...]], axis=0)
        de = jnp.concatenate([d_ref[...], da_ref[...]], axis=0)
        n = tr + 8
        c = _conv_rows(xe, w_ref, n)
        sg = _sigmoid(c)
        s = c * sg
        if norm:
            r = lax.rsqrt(jnp.sum(s * s, axis=-1, keepdims=True) + EPS)
            ds = scale * r * (de - s * (r * r) * jnp.sum(de * s, axis=-1, keepdims=True))
        else:
            ds = de
        dc = ds * (sg + c * sg * (1.0 - sg))
        rowi = lax.broadcasted_iota(jnp.int32, (n, HEAD), 0)
        dc = jnp.where((i == nrow - 1) & (rowi >= tr), 0.0, dc)
        dct = dc[:tr, :]
        dx = None
        for j in range(GDN_CONV):
            tap = pl.ds(GDN_CONV - 1 - j, 1)
            up = dct if j == 0 else pltpu.roll(dc, n - j, 0)[:tr, :]
            term = w_ref[tap, :] * up
            dx = term if dx is None else dx + term
            dw_ref[tap, :] += jnp.sum(dct * _rows_down(xe, j)[8:8 + tr, :], axis=0, keepdims=True)
        dx_ref[...] = dx

    return pl.pallas_call(
        body, name=name, grid=(nblk, nrow),
        in_specs=[pl.BlockSpec((tr, HEAD), lambda j, i: (i, blk0 + j)),
                  pl.BlockSpec((8, HEAD), lambda j, i: (jnp.maximum(i * nb8 - 1, 0), blk0 + j)),
                  pl.BlockSpec((8, HEAD), lambda j, i: (jnp.minimum((i + 1) * nb8, last8), blk0 + j)),
                  pl.BlockSpec((GDN_CONV, HEAD), lambda j, i: (0, j)),
                  pl.BlockSpec((tr, HEAD), lambda j, i: (i, j)),
                  pl.BlockSpec((8, HEAD), lambda j, i: (jnp.minimum((i + 1) * nb8, last8), j))],
        out_specs=[pl.BlockSpec((tr, HEAD), lambda j, i: (i, j)),
                   pl.BlockSpec((GDN_CONV, HEAD), lambda j, i: (0, j))],
        out_shape=[jax.ShapeDtypeStruct((S, nblk * HEAD), F32),
                   jax.ShapeDtypeStruct((GDN_CONV, nblk * HEAD), F32)],
        compiler_params=_cp(("parallel", "arbitrary")),
    )(proj, proj, proj, convw, dout, dout)


def _gdn_gates(pg, alog, dtb):
    S = pg.shape[0]

    def body(a_ref, b_ref, al_ref, dt_ref, gc_ref, be_ref):
        g = -jnp.exp(al_ref[...]) * _softplus(a_ref[...] + dt_ref[...])
        rowm = lax.broadcasted_iota(jnp.int32, g.shape, 0) & (GDN_CHUNK - 1)
        c = g
        d = 1
        while d < GDN_CHUNK:
            c = c + jnp.where(rowm >= d, pltpu.roll(c, d, 0), 0.0)
            d *= 2
        gc_ref[...] = c
        be_ref[...] = _sigmoid(b_ref[...])

    blk = pl.BlockSpec((S, 128), lambda i: (0, 0))
    vec = pl.BlockSpec((1, 128), lambda i: (0, 0))
    return pl.pallas_call(
        body, name="gdn_gates", grid=(1,),
        in_specs=[blk, pl.BlockSpec((S, 128), lambda i: (0, 1)), vec, vec],
        out_specs=[blk, blk],
        out_shape=[jax.ShapeDtypeStruct((S, 128), F32)] * 2,
        compiler_params=_cp(("arbitrary",)),
    )(pg, pg, alog, dtb)


def _gdn_gates_bwd(pg, alog, dtb, dgc, dbeta):
    S = pg.shape[0]

    def body(a_ref, b_ref, al_ref, dt_ref, dgc_ref, dbe_ref, dpa_ref, dpb_ref, dal_ref, ddt_ref):
        rowm = lax.broadcasted_iota(jnp.int32, (S, 128), 0) & (GDN_CHUNK - 1)
        c = dgc_ref[...]
        d = 1
        while d < GDN_CHUNK:
            c = c + jnp.where(rowm < GDN_CHUNK - d, pltpu.roll(c, S - d, 0), 0.0)
            d *= 2
        xv = a_ref[...] + dt_ref[...]
        ea = jnp.exp(al_ref[...])
        g = -ea * _softplus(xv)
        dx = c * (-ea) * _sigmoid(xv)
        dpa_ref[...] = dx
        dal_ref[...] = jnp.sum(c * g, axis=0, keepdims=True)
        ddt_ref[...] = jnp.sum(dx, axis=0, keepdims=True)
        be = _sigmoid(b_ref[...])
        dpb_ref[...] = dbe_ref[...] * be * (1.0 - be)

    blk = pl.BlockSpec((S, 128), lambda i: (0, 0))
    blk1 = pl.BlockSpec((S, 128), lambda i: (0, 1))
    vec = pl.BlockSpec((1, 128), lambda i: (0, 0))
    dpa, dpb, dal, ddt = pl.pallas_call(
        body, name="gdn_gates_bwd", grid=(1,),
        in_specs=[blk, blk1, vec, vec, blk, blk],
        out_specs=[blk, blk, vec, vec],
        out_shape=[jax.ShapeDtypeStruct((S, 128), F32)] * 2 + [jax.ShapeDtypeStruct((1, 128), F32)] * 2,
        compiler_params=_cp(("arbitrary",)),
    )(pg, pg, alog, dtb, dgc, dbeta)
    return jnp.concatenate([dpa, dpb], axis=1), dal, ddt


def _gdn_pre(q, k, v, gc, gr, beta):
    C = GDN_CHUNK
    r = lax.broadcasted_iota(jnp.int32, (C, C), 0)
    c_ = lax.broadcasted_iota(jnp.int32, (C, C), 1)
    lower, strict = r >= c_, r > c_
    dec = jnp.exp(jnp.where(lower, gc - gr, -jnp.inf))
    kb, vb = k * beta, v * beta
    lmat = jnp.where(strict, _dot(kb, k, NT) * dec, 0.0)
    pk = -lmat
    tinv = jnp.where(r == c_, 1.0, 0.0) + pk
    for _ in range(5):
        pk = _dotf(pk, pk, NN)
        tinv = tinv + _dotf(tinv, pk, NN)
    e = jnp.exp(gc)
    glast = gc[C - 1:C, :]
    f = jnp.exp(glast - gc)
    gl = jnp.exp(glast)
    u = _dotf(tinv, vb, NN)
    w = _dotf(tinv, kb * e, NN)
    amat = jnp.where(lower, _dot(q, k, NT) * dec, 0.0)
    return dict(lower=lower, strict=strict, dec=dec, kb=kb, vb=vb, lmat=lmat, tinv=tinv, e=e, f=f, gl=gl,
                u=u, w=w, amat=amat, qd=q * e, kd=k * f)


def _gdn_chunk_fwd(q, k, v, gcol, grow, bcol):
    S = q.shape[0]
    H, NC = gcol.shape[0], gcol.shape[1]
    C = GDN_CHUNK

    def body(q_ref, k_ref, v_ref, gc_ref, gr_ref, b_ref, o_ref, st_ref, state):
        n = pl.program_id(1)

        @pl.when(n == 0)
        def _():
            state[...] = jnp.zeros_like(state)

        p = _gdn_pre(q_ref[...], k_ref[...], v_ref[...], gc_ref[...], gr_ref[...], b_ref[...])
        s0 = state[...]
        st_ref[...] = s0
        vn = p['u'] - _dot(p['w'], s0, NN)
        o_ref[...] = _dot(p['qd'], s0, NN) + _dot(p['amat'], vn, NN)
        state[...] = s0 * p['gl'] + _dot(p['kd'], vn, TN)

    tok = pl.BlockSpec((C, HEAD), lambda h, n: (n, h))
    col = pl.BlockSpec((None, None, C, 1), lambda h, n: (h, n, 0, 0))
    rowb = pl.BlockSpec((None, None, 1, C), lambda h, n: (h, n, 0, 0))
    return pl.pallas_call(
        body, name="gdn_chunk_fwd", grid=(H, NC),
        in_specs=[tok, tok, tok, col, rowb, col],
        out_specs=[tok, pl.BlockSpec((None, None, HEAD, HEAD), lambda h, n: (h, n, 0, 0))],
        out_shape=[jax.ShapeDtypeStruct((S, H * HEAD), F32), jax.ShapeDtypeStruct((H, NC, HEAD, HEAD), F32)],
        scratch_shapes=[pltpu.VMEM((HEAD, HEAD), F32)],
        compiler_params=_cp(("parallel", "arbitrary")),
    )(q, k, v, gcol, grow, bcol)


def _gdn_chunk_bwd(q, k, v, gcol, grow, bcol, st, do):
    S = q.shape[0]
    H, NC = gcol.shape[0], gcol.shape[1]
    C = GDN_CHUNK

    def body(q_ref, k_ref, v_ref, gc_ref, gr_ref, b_ref, st_ref, do_ref,
             dq_ref, dk_ref, dv_ref, dgc_ref, dbe_ref, dstate):
        n = pl.program_id(1)

        @pl.when(n == 0)
        def _():
            dstate[...] = jnp.zeros_like(dstate)

        q, k, v, beta = q_ref[...], k_ref[...], v_ref[...], b_ref[...]
        p = _gdn_pre(q, k, v, gc_ref[...], gr_ref[...], beta)
        lower, strict, dec = p['lower'], p['strict'], p['dec']
        s0, do_, ds1 = st_ref[...], do_ref[...], dstate[...]
        vn = p['u'] - _dot(p['w'], s0, NN)
        dvn = _dot(p['amat'], do_, TN) + _dot(p['kd'], ds1, NN)
        damat = jnp.where(lower, _dot(do_, vn, NT), 0.0)
        dqd = _dot(do_, s0, NT)
        dkd = _dot(vn, ds1, NT)
        dgl = jnp.sum(s0 * ds1, keepdims=True)
        dstate[...] = p['gl'] * ds1 + _dot(p['qd'], do_, TN) - _dot(p['w'], dvn, TN)
        dw = -_dot(dvn, s0, NT)
        dvb = _dotf(p['tinv'], dvn, TN)
        dkg = _dotf(p['tinv'], dw, TN)
        dl = -jnp.where(strict, _dotf(dvb, p['u'], NT) + _dotf(dkg, p['w'], NT), 0.0)
        dkk = dl * dec
        dqk = damat * dec
        m = dl * p['lmat'] + damat * p['amat']
        dkb = _dot(dkk, k, NN) + dkg * p['e']
        dk = _dot(dkk, p['kb'], TN) + _dot(dqk, q, TN) + dkd * p['f'] + dkb * beta
        dq = _dot(dqk, k, NN) + dqd * p['e']
        de = jnp.sum(dkg * p['kb'], axis=-1, keepdims=True) + jnp.sum(dqd * q, axis=-1, keepdims=True)
        df = jnp.sum(dkd * k, axis=-1, keepdims=True)
        dbe_ref[...] = jnp.sum(dkb * k, axis=-1, keepdims=True) + jnp.sum(dvb * v, axis=-1, keepdims=True)
        colsum = _dotf(m, jnp.ones((C, HEAD), F32), TN)[:, 0:1]
        dgc = jnp.sum(m, axis=-1, keepdims=True) - colsum + de * p['e'] - df * p['f']
        dlast = jnp.sum(df * p['f'], keepdims=True) + dgl * p['gl']
        rowi = lax.broadcasted_iota(jnp.int32, (C, 1), 0)
        dgc_ref[...] = dgc + jnp.where(rowi == C - 1, dlast, 0.0)
        dq_ref[...] = dq
        dk_ref[...] = dk
        dv_ref[...] = dvb * beta

    def rn(n):
        return NC - 1 - n

    tok = pl.BlockSpec((C, HEAD), lambda h, n: (rn(n), h))
    col = pl.BlockSpec((None, None, C, 1), lambda h, n: (h, rn(n), 0, 0))
    rowb = pl.BlockSpec((None, None, 1, C), lambda h, n: (h, rn(n), 0, 0))
    return pl.pallas_call(
        body, name="gdn_chunk_bwd", grid=(H, NC),
        in_specs=[tok, tok, tok, col, rowb, col,
                  pl.BlockSpec((None, None, HEAD, HEAD), lambda h, n: (h, rn(n), 0, 0)), tok],
        out_specs=[tok, tok, tok, col, col],
        out_shape=[jax.ShapeDtypeStruct((S, H * HEAD), F32)] * 3
        + [jax.ShapeDtypeStruct((H, NC, C, 1), F32)] * 2,
        scratch_shapes=[pltpu.VMEM((HEAD, HEAD), F32)],
        compiler_params=_cp(("parallel", "arbitrary")),
    )(q, k, v, gcol, grow, bcol, st, do)


def _gdn_onorm(o, proj, gate_blk, w, H):
    S = o.shape[0]
    tr = _tile(S, CONV_TILE, 8)

    def body(o_ref, g_ref, w_ref, out_ref):
        ov, gv = o_ref[...], g_ref[...]
        r = lax.rsqrt(jnp.mean(ov * ov, axis=-1, keepdims=True) + EPS)
        out_ref[...] = (ov * r * w_ref[...]) * (gv * _sigmoid(gv))

    return pl.pallas_call(
        body, name="gdn_onorm", grid=(S // tr, H),
        in_specs=[pl.BlockSpec((tr, HEAD), lambda i, h: (i, h)),
                  pl.BlockSpec((tr, HEAD), lambda i, h: (i, gate_blk + h)),
                  pl.BlockSpec((1, HEAD), lambda i, h: (0, 0))],
        out_specs=pl.BlockSpec((tr, HEAD), lambda i, h: (i, h)),
        out_shape=jax.ShapeDtypeStruct((S, H * HEAD), F32),
        compiler_params=_cp(("parallel", "parallel")),
    )(o, proj, w)


def _gdn_onorm_bwd(dcat, o, proj, gate_blk, w, H):
    S = o.shape[0]
    tr = _tile(S, CONV_TILE, 8)

    def body(d_ref, o_ref, g_ref, w_ref, do_ref, dg_ref, dw_ref):
        i, h = pl.program_id(0), pl.program_id(1)

        @pl.when((i == 0) & (h == 0))
        def _():
            dw_ref[...] = jnp.zeros_like(dw_ref)

        dm, ov, gv, wv = d_ref[...], o_ref[...], g_ref[...], w_ref[...]
        r = lax.rsqrt(jnp.mean(ov * ov, axis=-1, keepdims=True) + EPS)
        oh = ov * r
        sg = gv * _sigmoid(gv)
        dy = dm * sg
        t = dy * wv
        do_ref[...] = r * (t - oh * jnp.mean(t * oh, axis=-1, keepdims=True))
        dg_ref[...] = dm * (oh * wv) * _silu_grad(gv)
        dw_ref[...] += jnp.sum(dy * oh, axis=0, keepdims=True)

    tok = pl.BlockSpec((tr, HEAD), lambda i, h: (i, h))
    vec = pl.BlockSpec((1, HEAD), lambda i, h: (0, 0))
    return pl.pallas_call(
        body, name="gdn_onorm_bwd", grid=(S // tr, H),
        in_specs=[tok, tok, pl.BlockSpec((tr, HEAD), lambda i, h: (i, gate_blk + h)), vec],
        out_specs=[tok, tok, vec],
        out_shape=[jax.ShapeDtypeStruct((S, H * HEAD), F32)] * 2 + [jax.ShapeDtypeStruct((1, HEAD), F32)],
        compiler_params=_cp(("arbitrary", "arbitrary")),
    )(dcat, o, proj, w)


def _lanes_to_heads(a, H):
    return a[:, :H].T


def _heads_to_lanes(a):
    H = a.shape[0]
    return jnp.pad(a.T, ((0, 0), (0, 128 - H)))


def _pad_lanes(v):
    return jnp.pad(v.reshape(1, -1), ((0, 0), (0, 128 - v.shape[-1])))


def _s5_layer_fwd(a, w, cfg):
    proj = _mm(a, w['w_in'], name="s5_in")
    wb, wc, _ = w['prep']
    pr, pi, prr, pir = w['tables']
    v, yg, hs, cin = _s5_scan_fwd(proj, wb, wc, pr, pi, w['d_skip'])
    z, mix = _mm(yg, w['w_glu'], name="s5_glu", extras=[(yg, 'ij'), (w['b_glu'], 'j')],
                 epi=lambda acc, y, b: (acc + b, y * _sigmoid(acc + b)), out_dtypes=(F32, F32))
    return proj, mix, dict(v=v, yg=yg, hs=hs, cin=cin, z=z)


def _s5_layer_bwd(a, w, proj, sv, dcat, dmemq, cfg):
    wb, wc, _ = w['prep']
    pr, pi, prr, pir = w['tables']
    dz, dyg1, db_glu = _s5_glu_bwd(dcat, sv['yg'], sv['z'])
    dw_glu = _mm(sv['yg'], dz, name="s5_dwglu", ta=True)
    dv = _mm(dz, w['w_glu'], name="s5_dyg", tb=True, extras=[(dyg1, 'ij'), (sv['v'], 'ij')],
             epi=lambda acc, d1, vv: ((acc + d1) * _gelu_grad(vv),))
    du, dwb, dwc, da, dd = _s5_scan_bwd(dv, proj, sv['hs'], sv['cin'], wb, wc, pr, pi, prr, pir, w['d_skip'])
    dproj = jnp.concatenate([du, dmemq], axis=1)
    dw_in = _mm(a, dproj, name="s5_dwin", ta=True)
    da_in = _mm(dproj, w['w_in'], name="s5_da", tb=True)
    dlre, dlim, dldt, dbre, dbim, dcre, dcim = w['prep_vjp']((dwb, dwc, da))
    grads = dict(w_in=dw_in, w_glu=dw_glu, b_glu=db_glu[0], d_skip=dd[0], lam_re=dlre, lam_im=dlim,
                 log_dt=dldt, b_re=dbre, b_im=dbim, c_re=dcre, c_im=dcim)
    return da_in, grads


def _gdn_relayout(a, H, NC):
    t = _lanes_to_heads(a, H).reshape(H, NC, GDN_CHUNK)
    return t[..., None], t[:, :, None, :]


def _gdn_layer_fwd(a, w, cfg):
    H, MIX, S = cfg['H'], cfg['MIX'], a.shape[0]
    NC = S // GDN_CHUNK
    proj = _mm(a, w['w_main'], name="gdn_in")
    pg = _mm(a, w['w_gate'], name="gdn_in_gates")
    cw = w['conv_w']
    q = _gdn_prep(proj, 0, H, cw[:, :MIX], True, HEAD ** -0.5, "gdn_prep_q")
    k = _gdn_prep(proj, H, H, cw[:, MIX:2 * MIX], True, 1.0, "gdn_prep_k")
    v = _gdn_prep(proj, 2 * H, H, cw[:, 2 * MIX:], False, 1.0, "gdn_prep_v")
    gc, beta = _gdn_gates(pg, w['a_log'], w['dt_bias'])
    gcol, grow = _gdn_relayout(gc, H, NC)
    bcol, _ = _gdn_relayout(beta, H, NC)
    o, st = _gdn_chunk_fwd(q, k, v, gcol, grow, bcol)
    mix = _gdn_onorm(o, proj, 3 * H, w['o_norm'], H)
    return proj, mix, dict(pg=pg, q=q, k=k, v=v, gcol=gcol, grow=grow, bcol=bcol, o=o, st=st)


def _gdn_layer_bwd(a, w, proj, sv, dcat, dmemq, cfg):
    H, MIX, S = cfg['H'], cfg['MIX'], a.shape[0]
    cw = w['conv_w']
    do, dgate, donorm = _gdn_onorm_bwd(dcat, sv['o'], proj, 3 * H, w['o_norm'], H)
    dq, dk, dv, dgcol, dbcol = _gdn_chunk_bwd(sv['q'], sv['k'], sv['v'], sv['gcol'], sv['grow'], sv['bcol'],
                                              sv['st'], do)
    dgc = _heads_to_lanes(dgcol.reshape(H, S))
    dbeta = _heads_to_lanes(dbcol.reshape(H, S))
    dpg, dalog, ddtb = _gdn_gates_bwd(sv['pg'], w['a_log'], w['dt_bias'], dgc, dbeta)
    dxq, dwq = _gdn_prep_bwd(proj, 0, H, cw[:, :MIX], dq, True, HEAD ** -0.5, "gdn_prep_bwd_q")
    dxk, dwk = _gdn_prep_bwd(proj, H, H, cw[:, MIX:2 * MIX], dk, True, 1.0, "gdn_prep_bwd_k")
    dxv, dwv = _gdn_prep_bwd(proj, 2 * H, H, cw[:, 2 * MIX:], dv, False, 1.0, "gdn_prep_bwd_v")
    dproj = jnp.concatenate([dxq, dxk, dxv, dgate, dmemq], axis=1)
    dw_main = _mm(a, dproj, name="gdn_dwmain", ta=True)
    dw_gate = _mm(a, dpg, name="gdn_dwgate", ta=True)
    da1 = _mm(dpg, w['w_gate'], name="gdn_da_gates", tb=True)
    da_in = _mm(dproj, w['w_main'], name="gdn_da", tb=True, extras=[(da1, 'ij')], epi=lambda acc, e: (acc + e,))
    grads = dict(w_main=dw_main, w_gate=dw_gate, conv_w=jnp.concatenate([dwq, dwk, dwv], axis=1),
                 a_log=dalog[0, :H], dt_bias=ddtb[0, :H], o_norm=donorm[0])
    return da_in, grads


def _fox_layer_fwd(a, w, cfg):
    H = cfg['H']
    proj = _mm(a, w['w_main'], name="fox_in")
    pg = _mm(a, w['w_gate'], name="fox_in_gates")
    cf = _fox_gates(pg, w['b_f'])
    cfh = _lanes_to_heads(cf, H)
    cfq, cfk = cfh[:, :, None], cfh[:, None, :]
    o, lse = _fox_fwd(proj, cfq, cfk, H)
    return proj, o, dict(pg=pg, cfq=cfq, cfk=cfk, lse=lse)


def _fox_layer_bwd(a, w, proj, sv, dcat, dmemq, cfg):
    H = cfg['H']
    rowdot = _fox_bwd_rowdot(proj, sv['cfq'], sv['cfk'], sv['lse'], dcat, H)
    dq, dk, dv, dck = _fox_bwd(proj, sv['cfq'], sv['cfk'], rowdot, sv['lse'], dcat, H)
    dpg, dbf = _fox_gates_bwd(sv['pg'], w['b_f'], _heads_to_lanes(dck[:, 0, :]))
    dproj = jnp.concatenate([dq, dk, dv, dmemq], axis=1)
    dw_main = _mm(a, dproj, name="fox_dwmain", ta=True)
    dw_gate = _mm(a, dpg, name="fox_dwgate", ta=True)
    da1 = _mm(dpg, w['w_gate'], name="fox_da_gates", tb=True)
    da_in = _mm(dproj, w['w_main'], name="fox_da", tb=True, extras=[(da1, 'ij')], epi=lambda acc, e: (acc + e,))
    grads = dict(w_main=dw_main, w_gate=dw_gate, b_f=dbf[0, :H])
    return da_in, grads


_LAYER_FWD = (_s5_layer_fwd, _gdn_layer_fwd, _fox_layer_fwd)
_LAYER_BWD = (_s5_layer_bwd, _gdn_layer_bwd, _fox_layer_bwd)


def _mixer_weights(kind, j, fw, p, cfg):
    H, MIX, MW = cfg['H'], cfg['MIX'], cfg['MW']
    if kind == 0:
        params = tuple(p[n][j] for n in ('s5_lam_re', 's5_lam_im', 's5_log_dt', 's5_b_re', 's5_b_im',
                                         's5_c_re', 's5_c_im'))
        prep, prep_vjp = jax.vjp(_s5_prep, *params)
        prep = (prep[0].astype(MXU_DTYPE), prep[1].astype(MXU_DTYPE), prep[2])
        tables = _s5_tables(*params[:3])
        return dict(w_in=fw['s5_w_in'][j], w_glu=fw['s5_w_glu'][j], b_glu=fw['s5_b_glu'][j],
                    d_skip=fw['s5_d_skip'][j].reshape(1, MIX), prep=prep, prep_vjp=prep_vjp, tables=tables)
    if kind == 1:
        wi = fw['gdn_w_in'][j]
        c0 = 4 * MIX
        w_main = jnp.concatenate([wi[:, :c0], wi[:, c0 + 2 * H:]], axis=1)
        w_gate = jnp.concatenate([jnp.pad(wi[:, c0:c0 + H], ((0, 0), (0, 128 - H))),
                                  jnp.pad(wi[:, c0 + H:c0 + 2 * H], ((0, 0), (0, 128 - H)))], axis=1)
        return dict(w_main=w_main, w_gate=w_gate, conv_w=fw['gdn_conv_w'][j],
                    a_log=_pad_lanes(p['gdn_a_log'][j]), dt_bias=_pad_lanes(p['gdn_dt_bias'][j]),
                    o_norm=p['gdn_o_norm'][j].reshape(1, HEAD))
    wi = fw['fox_w_in'][j]
    c0 = 3 * MIX
    w_main = jnp.concatenate([wi[:, :c0], wi[:, c0 + H:]], axis=1)
    w_gate = jnp.pad(wi[:, c0:c0 + H], ((0, 0), (0, 128 - H)))
    return dict(w_main=w_main, w_gate=w_gate, b_f=_pad_lanes(p['fox_b_f'][j]))


def _local_step(p, fw, cfg):
    H, MIX, MW, MH, depth = cfg['H'], cfg['MIX'], cfg['MW'], cfg['MH'], cfg['depth']
    x, mem, target = p['x'], p['mem'], p['loss_target']
    q_blk = {0: MIX // HEAD, 1: 4 * MIX // HEAD, 2: 3 * MIX // HEAD}

    mem_n = _rms_fwd(mem, p['mem_norm'], MXU_DTYPE, "mem_rms")
    mkv = _mm(mem_n, fw['w_mem_kv'], name="mem_kv")

    h = x
    saved = []
    for i in range(depth):
        kind, j = i % 3, i // 3
        w = _mixer_weights(kind, j, fw, p, cfg)
        a = _rms_fwd(h, p['norm1'][i], MXU_DTYPE, "rms1")
        proj, mix, sv = _LAYER_FWD[kind](a, w, cfg)
        read = _mem_fwd(proj, q_blk[kind], mkv, MH)
        cat = jnp.concatenate([mix, read], axis=1)
        h1 = _mm(cat, fw['w_out'][i], name="out_proj", extras=[(h, 'ij')], epi=lambda acc, r: (acc + r,))
        a2 = _rms_fwd(h1, p['norm2'][i], MXU_DTYPE, "rms2")
        u = _mm(a2, fw['w_up'][i], name="mlp_up")
        h2 = _mm(u, fw['w_down'][i], name="mlp_down", a_pro=_relu2, extras=[(h1, 'ij')],
                 epi=lambda acc, r: (acc + r,))
        saved.append(dict(w=w, h=h, a=a, proj=proj, sv=sv, cat=cat, h1=h1, a2=a2, u=u))
        h = h2

    loss, dh, dnorm_f = _loss_head(h, p['norm_f'], target)

    g = {n: None for n in WEIGHTS}
    g['norm_f'] = dnorm_f[0]
    per_layer = {n: [None] * depth for n in ('norm1', 'norm2', 'w_out', 'w_up', 'w_down')}
    mix_grads = {0: {}, 1: {}, 2: {}}
    dmkv = None
    for i in reversed(range(depth)):
        kind, j = i % 3, i // 3
        s = saved[i]
        w = s['w']
        du = _mm(dh, fw['w_down'][i], name="mlp_ddown", tb=True, extras=[(s['u'], 'ij')],
                 epi=lambda acc, uu: (acc * (2.0 * jnp.maximum(uu, 0.0)),))
        per_layer['w_down'][i] = _mm(s['u'], dh, name="mlp_dwdown", ta=True, a_pro=_relu2)
        per_layer['w_up'][i] = _mm(s['a2'], du, name="mlp_dwup", ta=True)
        da2 = _mm(du, fw['w_up'][i], name="mlp_dup", tb=True)
        dh1, dn2 = _rms_bwd(s['h1'], p['norm2'][i], da2, dh, "rms2_bwd")
        per_layer['norm2'][i] = dn2[0]
        dcat = _mm(dh1, fw['w_out'][i], name="out_dproj", tb=True)
        per_layer['w_out'][i] = _mm(s['cat'], dh1, name="out_dw", ta=True)
        dmemq, dmkv_i = _mem_bwd(s['proj'], q_blk[kind], mkv, dcat, MIX // HEAD, MH)
        dmkv = dmkv_i if dmkv is None else dmkv + dmkv_i
        da, mg = _LAYER_BWD[kind](s['a'], w, s['proj'], s['sv'], dcat, dmemq, cfg)
        mix_grads[kind][j] = mg
        dh, dn1 = _rms_bwd(s['h'], p['norm1'][i], da, dh1, "rms1_bwd")
        per_layer['norm1'][i] = dn1[0]
    for n, v in per_layer.items():
        g[n] = jnp.stack(v)

    g['w_mem_kv'] = _mm(mem_n, dmkv, name="mem_dwkv", ta=True)
    dmem_n = _mm(dmkv, fw['w_mem_kv'], name="mem_dn", tb=True)
    _, dmn = _rms_bwd(mem, p['mem_norm'], dmem_n, None, "mem_rms_bwd")
    g['mem_norm'] = dmn[0]

    def stack(kind, key):
        return jnp.stack([mix_grads[kind][j][key] for j in sorted(mix_grads[kind])])

    g['s5_w_in'] = stack(0, 'w_in')
    g['s5_w_glu'] = stack(0, 'w_glu')
    g['s5_b_glu'] = stack(0, 'b_glu')
    g['s5_d_skip'] = stack(0, 'd_skip')
    for n in ('lam_re', 'lam_im', 'log_dt', 'b_re', 'b_im', 'c_re', 'c_im'):
        g['s5_' + n] = stack(0, n)
    wm, wg = stack(1, 'w_main'), stack(1, 'w_gate')
    c0 = 4 * MIX
    g['gdn_w_in'] = jnp.concatenate([wm[..., :c0], wg[..., :H], wg[..., 128:128 + H], wm[..., c0:]], axis=-1)
    g['gdn_conv_w'] = stack(1, 'conv_w')
    g['gdn_a_log'] = stack(1, 'a_log')
    g['gdn_dt_bias'] = stack(1, 'dt_bias')
    g['gdn_o_norm'] = stack(1, 'o_norm')
    wm, wg = stack(2, 'w_main'), stack(2, 'w_gate')
    c0 = 3 * MIX
    g['fox_w_in'] = jnp.concatenate([wm[..., :c0], wg[..., :H], wm[..., c0:]], axis=-1)
    g['fox_b_f'] = stack(2, 'b_f')
    return loss, dh, g


def _unshard(stacked, axis):
    t = jnp.moveaxis(stacked, 0, axis)
    shp = list(t.shape)
    return t.reshape(shp[:axis] + [shp[axis] * shp[axis + 1]] + shp[axis + 2:])


def _to_shards(full, axis):
    shp = list(full.shape)
    t = full.reshape(shp[:axis] + [4, shp[axis] // 4] + shp[axis + 1:])
    return jnp.moveaxis(t, axis, 0).reshape(4, -1)


def _gather_weights(p, names, dtype, rows_per_call, name):
    flat = jnp.concatenate([p[n].astype(dtype).reshape(-1) for n in names])
    packed, n_el = _pack_rows(flat, rows_per_call)
    got = jnp.stack([_all_gather_chips(packed[k], name) for k in range(packed.shape[0])], axis=1)
    got = got.reshape(4, -1)
    out, off = {}, 0
    for n in names:
        sz = p[n].size
        out[n] = _unshard(got[:, off:off + sz].reshape((4,) + p[n].shape), SHARD_AXIS[n])
        off += sz
    return out


def kernel(x, mem, mem_norm, w_mem_kv, norm1, w_out, norm2, w_up, w_down, norm_f, s5_w_in, s5_lam_re, s5_lam_im, s5_log_dt, s5_b_re, s5_b_im, s5_c_re, s5_c_im, s5_d_skip, s5_w_glu, s5_b_glu, gdn_w_in, gdn_conv_w, gdn_a_log, gdn_dt_bias, gdn_o_norm, fox_w_in, fox_b_f, loss_target, m_mem_norm, m_w_mem_kv, m_norm1, m_w_out, m_norm2, m_w_up, m_w_down, m_norm_f, m_s5_w_in, m_s5_lam_re, m_s5_lam_im, m_s5_log_dt, m_s5_b_re, m_s5_b_im, m_s5_c_re, m_s5_c_im, m_s5_d_skip, m_s5_w_glu, m_s5_b_glu, m_gdn_w_in, m_gdn_conv_w, m_gdn_a_log, m_gdn_dt_bias, m_gdn_o_norm, m_fox_w_in, m_fox_b_f, v_mem_norm, v_w_mem_kv, v_norm1, v_w_out, v_norm2, v_w_up, v_w_down, v_norm_f, v_s5_w_in, v_s5_lam_re, v_s5_lam_im, v_s5_log_dt, v_s5_b_re, v_s5_b_im, v_s5_c_re, v_s5_c_im, v_s5_d_skip, v_s5_w_glu, v_s5_b_glu, v_gdn_w_in, v_gdn_conv_w, v_gdn_a_log, v_gdn_dt_bias, v_gdn_o_norm, v_fox_w_in, v_fox_b_f):
    args = locals()
    p = {n: args[n] for n in WEIGHTS}
    mom = {n: args['m_' + n] for n in WEIGHTS}
    var = {n: args['v_' + n] for n in WEIGHTS}
    S, D = x.shape[1], x.shape[2]
    MW = w_mem_kv.shape[1] // 2
    MIX = D - MW
    cfg = dict(H=MIX // HEAD, MIX=MIX, MW=MW, MH=MW // HEAD, depth=norm1.shape[0])
    p.update(x=x.reshape(S, D), mem=mem.reshape(mem.shape[1], D), loss_target=loss_target.reshape(S, D))

    fw = _gather_weights(p, MATMUL_WEIGHTS, MXU_DTYPE, AG_ROWS, "all_gather_weights")
    fw.update(_gather_weights(p, VECTOR_SHARDED, F32, 16, "all_gather_vectors"))

    loss, dx, g = _local_step(p, fw, cfg)

    c_idx = lax.axis_index("c").astype(jnp.int32).reshape(1)
    me_idx = (2 * lax.axis_index("x") + lax.axis_index("y")).astype(jnp.int32).reshape(1)
    sharded = list(SHARD_AXIS)
    flat = jnp.concatenate([_to_shards(g[n], SHARD_AXIS[n]) for n in sharded], axis=1)
    packed, _ = _pack_rows(flat, RS_ROWS)
    red = jnp.concatenate([_reduce_scatter(packed[:, k], c_idx, me_idx).reshape(-1)
                           for k in range(packed.shape[1])])
    grads, off = {}, 0
    for n in sharded:
        grads[n] = red[off:off + p[n].size].reshape(p[n].shape)
        off += p[n].size

    small = jnp.concatenate([g[n].reshape(-1) for n in REPLICATED])
    n_small = small.shape[0]
    rows = -(-n_small // LANES // 8) * 8
    small = jnp.pad(small, (0, rows * LANES - n_small)).reshape(rows, LANES)
    small = _all_reduce_small(small, "all_reduce_small").reshape(-1)
    off = 0
    for n in REPLICATED:
        grads[n] = small[off:off + p[n].size].reshape(p[n].shape)
        off += p[n].size

    delta, new_m, new_v = {}, {}, {}
    for n in sharded:
        shp = p[n].shape
        two_d = (-1, shp[-1])
        d, nm, nv = _adamw(p[n].reshape(two_d), grads[n].reshape(two_d), mom[n].reshape(two_d),
                           var[n].reshape(two_d), "adamw_" + n)
        delta[n], new_m[n], new_v[n] = d.reshape(shp), nm.reshape(shp), nv.reshape(shp)

    def pack_small(src):
        f = jnp.concatenate([src[n].reshape(-1) for n in REPLICATED])
        return jnp.pad(f, (0, rows * LANES - n_small)).reshape(rows, LANES)

    d, nm, nv = _adamw(pack_small(p), pack_small(grads), pack_small(mom), pack_small(var), "adamw_small")
    d, nm, nv = d.reshape(-1), nm.reshape(-1), nv.reshape(-1)
    off = 0
    for n in REPLICATED:
        sz, shp = p[n].size, p[n].shape
        delta[n], new_m[n], new_v[n] = (d[off:off + sz].reshape(shp), nm[off:off + sz].reshape(shp),
                                        nv[off:off + sz].reshape(shp))
        off += sz

    total = lax.psum(loss[0, 0], ("x", "y", "c"))
    return (total, dx.reshape(x.shape), *[grads[n] for n in WEIGHTS], *[delta[n] for n in WEIGHTS],
            *[new_m[n] for n in WEIGHTS], *[new_v[n] for n in WEIGHTS])
```

```python
import math

import jax
import jax.numpy as jnp
import numpy as np
from jax import lax
from jax.experimental import pallas as pl
from jax.experimental.pallas import tpu as pltpu

F32 = jnp.float32
MXU_DTYPE = jnp.bfloat16
EPS = 1e-6
HEAD = 128
S5_GROUP = 16
S5_STATE = 64
S5_SLAB = 256
S5_CHUNK = 128
GDN_CHUNK = 64
GDN_CONV = 4
LANES = 1024
VMEM_LIMIT_BYTES = 56 * 1024 * 1024
MESH = pl.DeviceIdType.MESH
RS_PAYLOAD = jnp.bfloat16

ADAM_LR, ADAM_B1, ADAM_B2, ADAM_EPS, ADAM_WD, ADAM_STEP = 0.001, 0.9, 0.999, 1e-08, 0.01, 10

MM_TM, MM_TN, MM_TK = 1024, 1024, 1024
ROW_TILE = 256
FOX_TILE = 512
MEM_TILE = 512
CONV_TILE = 1024

NN = (((1,), (0,)), ((), ()))
NT = (((1,), (1,)), ((), ()))
TN = (((0,), (0,)), ((), ()))

WEIGHTS = ['mem_norm', 'w_mem_kv', 'norm1', 'w_out', 'norm2', 'w_up', 'w_down', 'norm_f', 's5_w_in',
           's5_lam_re', 's5_lam_im', 's5_log_dt', 's5_b_re', 's5_b_im', 's5_c_re', 's5_c_im', 's5_d_skip',
           's5_w_glu', 's5_b_glu', 'gdn_w_in', 'gdn_conv_w', 'gdn_a_log', 'gdn_dt_bias', 'gdn_o_norm',
           'fox_w_in', 'fox_b_f']
SHARD_AXIS = {'w_mem_kv': 0, 'w_out': 1, 'w_up': 2, 'w_down': 1, 's5_w_in': 1, 's5_d_skip': 1,
              's5_w_glu': 1, 's5_b_glu': 1, 'gdn_w_in': 2, 'gdn_conv_w': 2, 'fox_w_in': 2}
MATMUL_WEIGHTS = ['w_mem_kv', 'w_out', 'w_up', 'w_down', 's5_w_in', 's5_w_glu', 'gdn_w_in', 'fox_w_in']
VECTOR_SHARDED = ['s5_d_skip', 's5_b_glu', 'gdn_conv_w']
REPLICATED = [n for n in WEIGHTS if n not in SHARD_AXIS]


def _tile(dim, target, align=128):
    if dim <= target:
        return dim
    t = (target // align) * align
    while t >= align:
        if dim % t == 0:
            return t
        t -= align
    return dim


def _cp(sem=None, **kw):
    return pltpu.CompilerParams(dimension_semantics=sem, vmem_limit_bytes=VMEM_LIMIT_BYTES, **kw)


def _dot(a, b, dims):
    return lax.dot_general(a.astype(MXU_DTYPE), b.astype(MXU_DTYPE), dims, preferred_element_type=F32)


def _dotf(a, b, dims):
    return lax.dot_general(a, b, dims, precision=lax.Precision.HIGHEST, preferred_element_type=F32)


def _sigmoid(x):
    return 1.0 / (1.0 + jnp.exp(-x))


def _softplus(x):
    return jnp.maximum(x, 0.0) + jnp.log(1.0 + jnp.exp(-jnp.abs(x)))


def _relu2(x):
    r = jnp.maximum(x, 0.0)
    return r * r


_GELU_C = math.sqrt(2.0 / math.pi)


def _gelu(x):
    return 0.5 * x * (1.0 + jnp.tanh(_GELU_C * (x + 0.044715 * x * x * x)))


def _gelu_grad(x):
    t = jnp.tanh(_GELU_C * (x + 0.044715 * x * x * x))
    return 0.5 * (1.0 + t) + 0.5 * x * (1.0 - t * t) * _GELU_C * (1.0 + 3.0 * 0.044715 * x * x)


def _silu_grad(x):
    s = _sigmoid(x)
    return s + x * s * (1.0 - s)


def _mm(a, b, *, name, ta=False, tb=False, a_pro=None, extras=(), epi=None, out_dtypes=(F32,)):
    K, M = a.shape if ta else a.shape[::-1]
    N = b.shape[0] if tb else b.shape[1]
    assert (b.shape[1] if tb else b.shape[0]) == K, (a.shape, b.shape, ta, tb)
    tm, tn, tk = _tile(M, MM_TM), _tile(N, MM_TN), _tile(K, MM_TK)
    nk = K // tk
    n_ex, n_out = len(extras), len(out_dtypes)
    dims = TN if ta else (NT if tb else NN)

    def body(*refs):
        a_ref, b_ref = refs[0], refs[1]
        ex = refs[2:2 + n_ex]
        outs = refs[2 + n_ex:2 + n_ex + n_out]
        acc = refs[-1]
        k = pl.program_id(2)

        @pl.when(k == 0)
        def _():
            acc[...] = jnp.zeros_like(acc)

        at = a_ref[...]
        if a_pro is not None:
            at = a_pro(at)
        acc[...] += _dot(at, b_ref[...], dims)

        @pl.when(k == nk - 1)
        def _():
            res = acc[...]
            vals = epi(res, *[e[...] for e in ex]) if epi is not None else (res,)
            for o, v in zip(outs, vals):
                o[...] = v.astype(o.dtype)

    if ta:
        a_spec = pl.BlockSpec((tk, tm), lambda i, j, k: (k, i))
    else:
        a_spec = pl.BlockSpec((tm, tk), lambda i, j, k: (i, k))
    if tb:
        b_spec = pl.BlockSpec((tn, tk), lambda i, j, k: (j, k))
    else:
        b_spec = pl.BlockSpec((tk, tn), lambda i, j, k: (k, j))
    ex_specs, ex_arrays = [], []
    for arr, kind in extras:
        if kind == 'ij':
            ex_specs.append(pl.BlockSpec((tm, tn), lambda i, j, k: (i, j)))
            ex_arrays.append(arr)
        else:
            ex_specs.append(pl.BlockSpec((1, tn), lambda i, j, k: (0, j)))
            ex_arrays.append(arr.reshape(1, N))
    outs = pl.pallas_call(
        body, name=name, grid=(M // tm, N // tn, nk),
        in_specs=[a_spec, b_spec] + ex_specs,
        out_specs=[pl.BlockSpec((tm, tn), lambda i, j, k: (i, j)) for _ in out_dtypes],
        out_shape=[jax.ShapeDtypeStruct((M, N), dt) for dt in out_dtypes],
        scratch_shapes=[pltpu.VMEM((tm, tn), F32)],
        compiler_params=_cp(("parallel", "parallel", "arbitrary")),
    )(a, b, *ex_arrays)
    return outs[0] if n_out == 1 else tuple(outs)


def _rms_fwd(x, g, out_dtype, name):
    S, D = x.shape
    tr = _tile(S, ROW_TILE, 8)

    def body(x_ref, g_ref, o_ref):
        xv = x_ref[...]
        r = lax.rsqrt(jnp.mean(xv * xv, axis=-1, keepdims=True) + EPS)
        o_ref[...] = (xv * r * g_ref[...]).astype(o_ref.dtype)

    return pl.pallas_call(
        body, name=name, grid=(S // tr,),
        in_specs=[pl.BlockSpec((tr, D), lambda i: (i, 0)), pl.BlockSpec((1, D), lambda i: (0, 0))],
        out_specs=pl.BlockSpec((tr, D), lambda i: (i, 0)),
        out_shape=jax.ShapeDtypeStruct((S, D), out_dtype),
        compiler_params=_cp(("parallel",)),
    )(x, g.reshape(1, D))


def _rms_bwd(x, g, dy, res, name):
    S, D = x.shape
    tr = _tile(S, ROW_TILE, 8)
    has_res = res is not None

    def body(*refs):
        if has_res:
            x_ref, g_ref, dy_ref, res_ref, dx_ref, dg_ref = refs
        else:
            x_ref, g_ref, dy_ref, dx_ref, dg_ref = refs
        i = pl.program_id(0)

        @pl.when(i == 0)
        def _():
            dg_ref[...] = jnp.zeros_like(dg_ref)

        xv, d = x_ref[...], dy_ref[...].astype(F32)
        r = lax.rsqrt(jnp.mean(xv * xv, axis=-1, keepdims=True) + EPS)
        xh = xv * r
        t = d * g_ref[...]
        dx = r * (t - xh * jnp.mean(t * xh, axis=-1, keepdims=True))
        if has_res:
            dx = dx + res_ref[...]
        dx_ref[...] = dx
        dg_ref[...] += jnp.sum(d * xh, axis=0, keepdims=True)

    row = pl.BlockSpec((tr, D), lambda i: (i, 0))
    vec = pl.BlockSpec((1, D), lambda i: (0, 0))
    ins = [x, g.reshape(1, D), dy] + ([res] if has_res else [])
    return pl.pallas_call(
        body, name=name, grid=(S // tr,),
        in_specs=[row, vec, row] + ([row] if has_res else []),
        out_specs=[row, vec],
        out_shape=[jax.ShapeDtypeStruct((S, D), F32), jax.ShapeDtypeStruct((1, D), F32)],
        compiler_params=_cp(("arbitrary",)),
    )(*ins)


def _loss_head(h, g, target):
    S, D = h.shape
    tr = _tile(S, ROW_TILE, 8)

    def body(h_ref, g_ref, t_ref, loss_ref, dh_ref, dg_ref):
        i = pl.program_id(0)

        @pl.when(i == 0)
        def _():
            loss_ref[...] = jnp.zeros_like(loss_ref)
            dg_ref[...] = jnp.zeros_like(dg_ref)

        xv = h_ref[...]
        gv = g_ref[...]
        r = lax.rsqrt(jnp.mean(xv * xv, axis=-1, keepdims=True) + EPS)
        xh = xv * r
        err = xh * gv - t_ref[...]
        part = 0.5 * jnp.sum(jnp.mean(err * err, axis=-1, keepdims=True), axis=0, keepdims=True)
        loss_ref[...] += jnp.broadcast_to(part, loss_ref.shape)
        d = err * (1.0 / D)
        t = d * gv
        dh_ref[...] = r * (t - xh * jnp.mean(t * xh, axis=-1, keepdims=True))
        dg_ref[...] += jnp.sum(d * xh, axis=0, keepdims=True)

    row = pl.BlockSpec((tr, D), lambda i: (i, 0))
    vec = pl.BlockSpec((1, D), lambda i: (0, 0))
    return pl.pallas_call(
        body, name="loss_head", grid=(S // tr,),
        in_specs=[row, vec, row],
        out_specs=[pl.BlockSpec((8, 128), lambda i: (0, 0)), row, vec],
        out_shape=[jax.ShapeDtypeStruct((8, 128), F32), jax.ShapeDtypeStruct((S, D), F32),
                   jax.ShapeDtypeStruct((1, D), F32)],
        compiler_params=_cp(("arbitrary",)),
    )(h, g.reshape(1, D), target)


def _adamw(w, g, m, v, name):
    R, C = w.shape
    tr = _tile(R, max(8, (1 << 19) // max(C, 1) // 8 * 8), 8)
    c1 = 1.0 / (1.0 - ADAM_B1 ** ADAM_STEP)
    c2 = 1.0 / (1.0 - ADAM_B2 ** ADAM_STEP)

    def body(w_ref, g_ref, m_ref, v_ref, d_ref, nm_ref, nv_ref):
        gv = g_ref[...]
        nm = ADAM_B1 * m_ref[...] + (1.0 - ADAM_B1) * gv
        nv = ADAM_B2 * v_ref[...] + (1.0 - ADAM_B2) * (gv * gv)
        d_ref[...] = -ADAM_LR * ((nm * c1) / (jnp.sqrt(nv * c2) + ADAM_EPS) + ADAM_WD * w_ref[...])
        nm_ref[...] = nm
        nv_ref[...] = nv

    blk = pl.BlockSpec((tr, C), lambda i: (i, 0))
    return pl.pallas_call(
        body, name=name, grid=(R // tr,),
        in_specs=[blk] * 4, out_specs=[blk] * 3,
        out_shape=[jax.ShapeDtypeStruct((R, C), F32)] * 3,
        compiler_params=_cp(("parallel",)),
    )(w, g, m, v)


def _place():
    x, y, c = lax.axis_index("x"), lax.axis_index("y"), lax.axis_index("c")
    chips = [(1 - x, y), (x, 1 - y), (1 - x, 1 - y)]
    return x, y, c, chips


def _all_gather_chips(xs, name):
    r, n = xs.shape
    half = r // 2

    def body(x_ref, out_ref, send_sems, recv_sems):
        x, y, c, chips = _place()
        me = 2 * x + y
        sibling = (x, y, 1 - c)

        def piece(chip, hc):
            return out_ref.at[chip, pl.ds(hc * half, half), :]

        def copy(k, src, dst, to):
            return pltpu.make_async_remote_copy(src_ref=src, dst_ref=dst, send_sem=send_sems.at[k],
                                                recv_sem=recv_sems.at[k], device_id=to, device_id_type=MESH)

        src = x_ref.at[pl.ds(c * half, half), :]
        first = [copy(j, src, piece(me, c), (cx, cy, c)) for j, (cx, cy) in enumerate(chips)]
        for cp in first:
            cp.start()
        passed = []
        for j, (cx, cy) in enumerate(chips):
            got = piece(2 * cx + cy, c)
            copy(j, got, got, (cx, cy, c)).wait_recv()
            fwd = copy(3 + j, got, got, sibling)
            fwd.start()
            passed.append(fwd)
        for j, (cx, cy) in enumerate(chips):
            got = piece(2 * cx + cy, 1 - c)
            copy(3 + j, got, got, sibling).wait_recv()
        for cp in first + passed:
            cp.wait_send()

    return pl.pallas_call(
        body, name=name,
        in_specs=[pl.BlockSpec(memory_space=pl.ANY)],
        out_specs=pl.BlockSpec(memory_space=pl.ANY),
        out_shape=jax.ShapeDtypeStruct((4, r, n), xs.dtype),
        scratch_shapes=[pltpu.SemaphoreType.DMA((6,)), pltpu.SemaphoreType.DMA((6,))],
    )(xs)


def _gather(xs, me, name):
    return lax.dynamic_update_slice(_all_gather_chips(xs, name), xs[None], (me, 0, 0))


def _rs_swap_halves(g, name):
    _, r, n = g.shape
    half = r // 2

    def body(g_ref, out_ref, send_sem, recv_sem):
        x, y, c, _ = _place()
        cp = pltpu.make_async_remote_copy(
            src_ref=g_ref.at[:, pl.ds((1 - c) * half, half), :], dst_ref=out_ref,
            send_sem=send_sem, recv_sem=recv_sem, device_id=(x, y, 1 - c), device_id_type=MESH)
        cp.start()
        cp.wait()

    return pl.pallas_call(
        body, name=name,
        in_specs=[pl.BlockSpec(memory_space=pl.ANY)],
        out_specs=pl.BlockSpec(memory_space=pl.ANY),
        out_shape=jax.ShapeDtypeStruct((4, half, n), g.dtype),
        scratch_shapes=[pltpu.SemaphoreType.DMA, pltpu.SemaphoreType.DMA],
    )(g)


def _rs_add_halves(g, got, c_idx, name):
    _, r, n = g.shape
    half = r // 2
    tr = _tile(half, max(16, (1 << 19) // n // 16 * 16), 16)
    nb = half // tr

    def body(c_ref, g_ref, o_ref, out_ref, out16_ref):
        sm = g_ref[...] + o_ref[...]
        out_ref[...] = sm
        out16_ref[...] = sm.astype(out16_ref.dtype)

    blk = pl.BlockSpec((None, tr, n), lambda s, i, c: (s, i, 0))
    return pl.pallas_call(
        body, name=name,
        grid_spec=pltpu.PrefetchScalarGridSpec(
            num_scalar_prefetch=1, grid=(4, nb),
            in_specs=[pl.BlockSpec((None, tr, n), lambda s, i, c: (s, c[0] * nb + i, 0)), blk],
            out_specs=[blk, blk]),
        out_shape=[jax.ShapeDtypeStruct((4, half, n), F32), jax.ShapeDtypeStruct((4, half, n), RS_PAYLOAD)],
        compiler_params=_cp(("parallel", "parallel")),
    )(c_idx, g, got)


def _rs_exchange_chips(p, name):
    _, h, n = p.shape

    def body(p_ref, out_ref, send_sems, recv_sems):
        x, y, c, chips = _place()
        copies = [pltpu.make_async_remote_copy(
            src_ref=p_ref.at[2 * cx + cy], dst_ref=out_ref.at[j], send_sem=send_sems.at[j],
            recv_sem=recv_sems.at[j], device_id=(cx, cy, c), device_id_type=MESH)
            for j, (cx, cy) in enumerate(chips)]
        for cp in copies:
            cp.start()
        for cp in copies:
            cp.wait()

    return pl.pallas_call(
        body, name=name,
        in_specs=[pl.BlockSpec(memory_space=pl.ANY)],
        out_specs=pl.BlockSpec(memory_space=pl.ANY),
        out_shape=jax.ShapeDtypeStruct((3, h, n), p.dtype),
        scratch_shapes=[pltpu.SemaphoreType.DMA((3,)), pltpu.SemaphoreType.DMA((3,))],
    )(p)


def _rs_add_chips(p, got, me_idx, name):
    _, h, n = p.shape
    tr = _tile(h, max(16, (1 << 19) // n // 16 * 16), 16)

    def body(me_ref, p_ref, a_ref, b_ref, c_ref, out_ref):
        out_ref[...] = ((p_ref[...] + a_ref[...].astype(F32)) + b_ref[...].astype(F32)) + c_ref[...].astype(F32)

    def got_spec(j):
        return pl.BlockSpec((None, tr, n), lambda i, me: (j, i, 0))

    return pl.pallas_call(
        body, name=name,
        grid_spec=pltpu.PrefetchScalarGridSpec(
            num_scalar_prefetch=1, grid=(h // tr,),
            in_specs=[pl.BlockSpec((None, tr, n), lambda i, me: (me[0], i, 0)),
                      got_spec(0), got_spec(1), got_spec(2)],
            out_specs=pl.BlockSpec((tr, n), lambda i, me: (i, 0))),
        out_shape=jax.ShapeDtypeStruct((h, n), F32),
        compiler_params=_cp(("parallel",)),
    )(me_idx, p, got, got, got)


def _rs_share_halves(q, name):
    h, n = q.shape

    def body(q_ref, out_ref, send_sem, recv_sem):
        x, y, c, _ = _place()
        cp = pltpu.make_async_remote_copy(
            src_ref=q_ref, dst_ref=out_ref.at[pl.ds(c * h, h), :], send_sem=send_sem, recv_sem=recv_sem,
            device_id=(x, y, 1 - c), device_id_type=MESH)
        cp.start()
        pltpu.make_async_remote_copy(
            src_ref=q_ref, dst_ref=out_ref.at[pl.ds((1 - c) * h, h), :], send_sem=send_sem, recv_sem=recv_sem,
            device_id=(x, y, 1 - c), device_id_type=MESH).wait_recv()
        cp.wait_send()

    return pl.pallas_call(
        body, name=name,
        in_specs=[pl.BlockSpec(memory_space=pl.ANY)],
        out_specs=pl.BlockSpec(memory_space=pl.ANY),
        out_shape=jax.ShapeDtypeStruct((2 * h, n), q.dtype),
        scratch_shapes=[pltpu.SemaphoreType.DMA, pltpu.SemaphoreType.DMA],
    )(q)


def _reduce_scatter(g, place):
    c, c_idx, me_idx = place['c'], place['c_idx'], place['me_idx']
    got = _rs_swap_halves(g, "rs_swap_halves")
    p, p16 = _rs_add_halves(g, got, c_idx, "rs_add_halves")
    got = _rs_exchange_chips(p16, "rs_exchange_chips")
    q = _rs_add_chips(p, got, me_idx, "rs_add_chips")
    out = _rs_share_halves(q, "rs_share_halves")
    return lax.dynamic_update_slice(out, q, (c * q.shape[0], 0))


def _all_reduce_small(v, name):
    R, n = v.shape

    def body(v_ref, out_ref, buf, send_sems, recv_sems):
        x, y, c, _ = _place()
        me = 4 * x + 2 * y + c
        buf[me] = v_ref[...]
        copies = []
        for d in range(1, 8):
            dx, dy, dc = (d >> 2) & 1, (d >> 1) & 1, d & 1
            px = x if dx == 0 else 1 - x
            py = y if dy == 0 else 1 - y
            pc = c if dc == 0 else 1 - c
            copies.append(pltpu.make_async_remote_copy(
                src_ref=v_ref, dst_ref=buf.at[me], send_sem=send_sems.at[d - 1], recv_sem=recv_sems.at[d - 1],
                device_id=(px, py, pc), device_id_type=MESH))
        for cp in copies:
            cp.start()
        for d in range(1, 8):
            dx, dy, dc = (d >> 2) & 1, (d >> 1) & 1, d & 1
            px = x if dx == 0 else 1 - x
            py = y if dy == 0 else 1 - y
            pc = c if dc == 0 else 1 - c
            pltpu.make_async_remote_copy(
                src_ref=v_ref, dst_ref=buf.at[4 * px + 2 * py + pc], send_sem=send_sems.at[d - 1],
                recv_sem=recv_sems.at[d - 1], device_id=(px, py, pc), device_id_type=MESH).wait_recv()
        for cp in copies:
            cp.wait_send()
        acc = buf[0]
        for k in range(1, 8):
            acc = acc + buf[k]
        out_ref[...] = acc

    return pl.pallas_call(
        body, name=name,
        in_specs=[pl.BlockSpec(memory_space=pltpu.VMEM)],
        out_specs=pl.BlockSpec(memory_space=pltpu.VMEM),
        out_shape=jax.ShapeDtypeStruct((R, n), F32),
        scratch_shapes=[pltpu.VMEM((8, R, n), F32), pltpu.SemaphoreType.DMA((7,)), pltpu.SemaphoreType.DMA((7,))],
        compiler_params=pltpu.CompilerParams(vmem_limit_bytes=VMEM_LIMIT_BYTES),
    )(v)


def _pack_small(parts, rows):
    flat = jnp.concatenate([a.reshape(-1) for a in parts])
    return jnp.pad(flat, (0, rows * LANES - flat.shape[0])).reshape(rows, LANES)


def _mem_fwd(proj, q_blk, mkv, heads):
    S = proj.shape[0]
    ML = mkv.shape[0]
    t = _tile(S, MEM_TILE, 8)
    scale = HEAD ** -0.5

    def body(q_ref, k_ref, v_ref, o_ref):
        s = _dot(q_ref[...], k_ref[...], NT) * scale
        m = jnp.max(s, axis=-1, keepdims=True)
        e = jnp.exp(s - m)
        p = e / jnp.sum(e, axis=-1, keepdims=True)
        o_ref[...] = _dot(p, v_ref[...], NN)

    return pl.pallas_call(
        body, name="mem_fwd", grid=(S // t, heads),
        in_specs=[pl.BlockSpec((t, HEAD), lambda i, h: (i, q_blk + h)),
                  pl.BlockSpec((ML, HEAD), lambda i, h: (0, h)),
                  pl.BlockSpec((ML, HEAD), lambda i, h: (0, heads + h))],
        out_specs=pl.BlockSpec((t, HEAD), lambda i, h: (i, h)),
        out_shape=jax.ShapeDtypeStruct((S, heads * HEAD), F32),
        compiler_params=_cp(("parallel", "parallel")),
    )(proj, mkv, mkv)


def _mem_bwd(proj, q_blk, mkv, dcat, d_blk, heads):
    S = proj.shape[0]
    ML = mkv.shape[0]
    t = _tile(S, MEM_TILE, 8)
    scale = HEAD ** -0.5

    def body(q_ref, k_ref, v_ref, do_ref, dq_ref, dk_ref, dv_ref):
        i = pl.program_id(1)

        @pl.when(i == 0)
        def _():
            dk_ref[...] = jnp.zeros_like(dk_ref)
            dv_ref[...] = jnp.zeros_like(dv_ref)

        q, k, v, do = q_ref[...], k_ref[...], v_ref[...], do_ref[...]
        s = _dot(q, k, NT) * scale
        m = jnp.max(s, axis=-1, keepdims=True)
        e = jnp.exp(s - m)
        p = e / jnp.sum(e, axis=-1, keepdims=True)
        dp = _dot(do, v, NT)
        ds = p * (dp - jnp.sum(p * dp, axis=-1, keepdims=True))
        dq_ref[...] = _dot(ds, k, NN) * scale
        dk_ref[...] += _dot(ds, q, TN) * scale
        dv_ref[...] += _dot(p, do, TN)

    dq, dk, dv = pl.pallas_call(
        body, name="mem_bwd", grid=(heads, S // t),
        in_specs=[pl.BlockSpec((t, HEAD), lambda h, i: (i, q_blk + h)),
                  pl.BlockSpec((ML, HEAD), lambda h, i: (0, h)),
                  pl.BlockSpec((ML, HEAD), lambda h, i: (0, heads + h)),
                  pl.BlockSpec((t, HEAD), lambda h, i: (i, d_blk + h))],
        out_specs=[pl.BlockSpec((t, HEAD), lambda h, i: (i, h)),
                   pl.BlockSpec((ML, HEAD), lambda h, i: (0, h)),
                   pl.BlockSpec((ML, HEAD), lambda h, i: (0, h))],
        out_shape=[jax.ShapeDtypeStruct((S, heads * HEAD), F32),
                   jax.ShapeDtypeStruct((ML, heads * HEAD), F32),
                   jax.ShapeDtypeStruct((ML, heads * HEAD), F32)],
        compiler_params=_cp(("parallel", "arbitrary")),
    )(proj, mkv, mkv, dcat)
    return dq, jnp.concatenate([dk, dv], axis=1)


def _fox_gates(gl, bf):
    S = gl.shape[0]

    def body(g_ref, b_ref, o_ref):
        xv = g_ref[...] + b_ref[...]
        c = jnp.minimum(xv, 0.0) - jnp.log(1.0 + jnp.exp(-jnp.abs(xv)))
        row = lax.broadcasted_iota(jnp.int32, c.shape, 0)
        d = 1
        while d < S:
            c = c + jnp.where(row >= d, pltpu.roll(c, d, 0), 0.0)
            d *= 2
        o_ref[...] = c

    return pl.pallas_call(
        body, name="fox_gates", out_shape=jax.ShapeDtypeStruct((S, 128), F32),
        in_specs=[pl.BlockSpec(memory_space=pltpu.VMEM)] * 2,
        out_specs=pl.BlockSpec(memory_space=pltpu.VMEM),
        compiler_params=_cp(),
    )(gl, bf)


def _fox_gates_bwd(gl, bf, dcf):
    S = gl.shape[0]

    def body(g_ref, b_ref, d_ref, dg_ref, db_ref):
        c = d_ref[...]
        row = lax.broadcasted_iota(jnp.int32, c.shape, 0)
        d = 1
        while d < S:
            c = c + jnp.where(row < S - d, pltpu.roll(c, S - d, 0), 0.0)
            d *= 2
        dx = c * _sigmoid(-(g_ref[...] + b_ref[...]))
        dg_ref[...] = dx
        db_ref[...] = jnp.sum(dx, axis=0, keepdims=True)

    return pl.pallas_call(
        body, name="fox_gates_bwd",
        out_shape=[jax.ShapeDtypeStruct((S, 128), F32), jax.ShapeDtypeStruct((1, 128), F32)],
        in_specs=[pl.BlockSpec(memory_space=pltpu.VMEM)] * 3,
        out_specs=[pl.BlockSpec(memory_space=pltpu.VMEM)] * 2,
        compiler_params=_cp(),
    )(gl, bf, dcf)


def _fox_mask(s, qi, ki, t):
    row = qi * t + lax.broadcasted_iota(jnp.int32, (t, t), 0)
    col = ki * t + lax.broadcasted_iota(jnp.int32, (t, t), 1)
    return jnp.where(row >= col, s, -jnp.inf)


def _fox_pairs(nq, by_key):
    if by_key:
        pairs = [(i, j) for j in range(nq) for i in range(j, nq)]
    else:
        pairs = [(i, j) for i in range(nq) for j in range(i + 1)]
    return (jnp.asarray(np.array([a for a, _ in pairs], np.int32)),
            jnp.asarray(np.array([b for _, b in pairs], np.int32)))


def _fox_fwd(proj, cfq, cfk, H):
    S = proj.shape[0]
    t = _tile(S, FOX_TILE)
    nq = S // t
    scale = HEAD ** -0.5
    qt, kt = _fox_pairs(nq, False)

    def body(qt_ref, kt_ref, q_ref, k_ref, v_ref, cq_ref, ck_ref, o_ref, lse_ref, m_s, l_s, acc_s):
        n = pl.program_id(1)
        qi, ki = qt_ref[n], kt_ref[n]

        @pl.when(ki == 0)
        def _():
            m_s[...] = jnp.full_like(m_s, -jnp.inf)
            l_s[...] = jnp.zeros_like(l_s)
            acc_s[...] = jnp.zeros_like(acc_s)

        s = _dot(q_ref[...], k_ref[...], NT) * scale + cq_ref[...] - ck_ref[...]
        s = _fox_mask(s, qi, ki, t)
        m_new = jnp.maximum(m_s[...], jnp.max(s, axis=-1, keepdims=True))
        alpha = jnp.exp(m_s[...] - m_new)
        p = jnp.exp(s - m_new)
        l_s[...] = alpha * l_s[...] + jnp.sum(p, axis=-1, keepdims=True)
        acc_s[...] = alpha * acc_s[...] + _dot(p, v_ref[...], NN)
        m_s[...] = m_new

        @pl.when(ki == qi)
        def _():
            o_ref[...] = acc_s[...] / l_s[...]
            lse_ref[...] = m_s[...] + jnp.log(l_s[...])

    return pl.pallas_call(
        body, name="fox_fwd",
        grid_spec=pltpu.PrefetchScalarGridSpec(
            num_scalar_prefetch=2, grid=(H, qt.shape[0]),
            in_specs=[pl.BlockSpec((t, HEAD), lambda h, n, qt, kt: (qt[n], h)),
                      pl.BlockSpec((t, HEAD), lambda h, n, qt, kt: (kt[n], H + h)),
                      pl.BlockSpec((t, HEAD), lambda h, n, qt, kt: (kt[n], 2 * H + h)),
                      pl.BlockSpec((None, t, 1), lambda h, n, qt, kt: (h, qt[n], 0)),
                      pl.BlockSpec((None, 1, t), lambda h, n, qt, kt: (h, 0, kt[n]))],
            out_specs=[pl.BlockSpec((t, HEAD), lambda h, n, qt, kt: (qt[n], h)),
                       pl.BlockSpec((None, t, 1), lambda h, n, qt, kt: (h, qt[n], 0))],
            scratch_shapes=[pltpu.VMEM((t, 1), F32), pltpu.VMEM((t, 1), F32), pltpu.VMEM((t, HEAD), F32)]),
        out_shape=[jax.ShapeDtypeStruct((S, H * HEAD), F32), jax.ShapeDtypeStruct((H, S, 1), F32)],
        compiler_params=_cp(("parallel", "arbitrary")),
    )(qt, kt, proj, proj, proj, cfq, cfk)


def _fox_bwd_rowdot(proj, cfq, cfk, lse, dcat, H):
    S = proj.shape[0]
    t = _tile(S, FOX_TILE)
    nq = S // t
    scale = HEAD ** -0.5
    qt, kt = _fox_pairs(nq, False)

    def body(qt_ref, kt_ref, q_ref, k_ref, v_ref, do_ref, lse_ref, cq_ref, ck_ref, d_ref):
        n = pl.program_id(1)
        qi, ki = qt_ref[n], kt_ref[n]

        @pl.when(ki == 0)
        def _():
            d_ref[...] = jnp.zeros_like(d_ref)

        s = _dot(q_ref[...], k_ref[...], NT) * scale + cq_ref[...] - ck_ref[...]
        p = jnp.exp(_fox_mask(s, qi, ki, t) - lse_ref[...])
        dp = _dot(do_ref[...], v_ref[...], NT)
        d_ref[...] += jnp.sum(p * dp, axis=-1, keepdims=True)

    return pl.pallas_call(
        body, name="fox_bwd_rowdot",
        grid_spec=pltpu.PrefetchScalarGridSpec(
            num_scalar_prefetch=2, grid=(H, qt.shape[0]),
            in_specs=[pl.BlockSpec((t, HEAD), lambda h, n, qt, kt: (qt[n], h)),
                      pl.BlockSpec((t, HEAD), lambda h, n, qt, kt: (kt[n], H + h)),
                      pl.BlockSpec((t, HEAD), lambda h, n, qt, kt: (kt[n], 2 * H + h)),
                      pl.BlockSpec((t, HEAD), lambda h, n, qt, kt: (qt[n], h)),
                      pl.BlockSpec((None, t, 1), lambda h, n, qt, kt: (h, qt[n], 0)),
                      pl.BlockSpec((None, t, 1), lambda h, n, qt, kt: (h, qt[n], 0)),
                      pl.BlockSpec((None, 1, t), lambda h, n, qt, kt: (h, 0, kt[n]))],
            out_specs=pl.BlockSpec((None, t, 1), lambda h, n, qt, kt: (h, qt[n], 0))),
        out_shape=jax.ShapeDtypeStruct((H, S, 1), F32),
        compiler_params=_cp(("parallel", "arbitrary")),
    )(qt, kt, proj, proj, proj, dcat, lse, cfq, cfk)


def _fox_bwd(proj, cfq, cfk, rowdot, lse, dcat, H):
    S = proj.shape[0]
    t = _tile(S, FOX_TILE)
    nq = S // t
    scale = HEAD ** -0.5
    qt, kt = _fox_pairs(nq, True)

    def body(qt_ref, kt_ref, q_ref, k_ref, v_ref, dd_ref, do_ref, lse_ref, cq_ref, ck_ref,
             dq_ref, dk_ref, dv_ref, dck_ref):
        n = pl.program_id(1)
        i, j = qt_ref[n], kt_ref[n]

        @pl.when(n == 0)
        def _():
            dq_ref[...] = jnp.zeros_like(dq_ref)

        @pl.when(i == j)
        def _():
            dk_ref[...] = jnp.zeros_like(dk_ref)
            dv_ref[...] = jnp.zeros_like(dv_ref)
            dck_ref[...] = jnp.zeros_like(dck_ref)

        q, k, v, do = q_ref[...], k_ref[...], v_ref[...], do_ref[...]
        s = _dot(q, k, NT) * scale + cq_ref[...] - ck_ref[...]
        s = _fox_mask(s, i, j, t)
        p = jnp.exp(s - lse_ref[...])
        dv_ref[...] += _dot(p, do, TN)
        dp = _dot(do, v, NT)
        ds = p * (dp - dd_ref[...])
        dk_ref[...] += _dot(ds, q, TN) * scale
        rows = pl.ds(pl.multiple_of(i * t, t), t)
        dq_ref[rows, :] += _dot(ds, k, NN) * scale
        dck_ref[...] -= jnp.sum(ds, axis=0, keepdims=True)

    qtile = pl.BlockSpec((t, HEAD), lambda h, n, qt, kt: (qt[n], h))
    qcol = pl.BlockSpec((None, t, 1), lambda h, n, qt, kt: (h, qt[n], 0))
    return pl.pallas_call(
        body, name="fox_bwd",
        grid_spec=pltpu.PrefetchScalarGridSpec(
            num_scalar_prefetch=2, grid=(H, qt.shape[0]),
            in_specs=[qtile,
                      pl.BlockSpec((t, HEAD), lambda h, n, qt, kt: (kt[n], H + h)),
                      pl.BlockSpec((t, HEAD), lambda h, n, qt, kt: (kt[n], 2 * H + h)),
                      qcol, qtile, qcol, qcol,
                      pl.BlockSpec((None, 1, t), lambda h, n, qt, kt: (h, 0, kt[n]))],
            out_specs=[pl.BlockSpec((S, HEAD), lambda h, n, qt, kt: (0, h)),
                       pl.BlockSpec((t, HEAD), lambda h, n, qt, kt: (kt[n], h)),
                       pl.BlockSpec((t, HEAD), lambda h, n, qt, kt: (kt[n], h)),
                       pl.BlockSpec((None, 1, t), lambda h, n, qt, kt: (h, 0, kt[n]))]),
        out_shape=[jax.ShapeDtypeStruct((S, H * HEAD), F32)] * 3 + [jax.ShapeDtypeStruct((H, 1, S), F32)],
        compiler_params=_cp(("parallel", "arbitrary")),
    )(qt, kt, proj, proj, proj, rowdot, dcat, lse, cfq, cfk)


def _s5_prep(lam_re, lam_im, log_dt, b_re, b_im, c_re, c_im):
    G, P = lam_re.shape
    ns = G // 16
    dt = jnp.exp(log_dt)[:, None]
    mag = jnp.exp(lam_re * dt)
    a_re, a_im = mag * jnp.cos(lam_im * dt), mag * jnp.sin(lam_im * dt)
    den = lam_re * lam_re + lam_im * lam_im
    z_re = ((a_re - 1.0) * lam_re + a_im * lam_im) / den
    z_im = (a_im * lam_re - (a_re - 1.0) * lam_im) / den
    bb_re = z_re[..., None] * b_re - z_im[..., None] * b_im
    bb_im = z_re[..., None] * b_im + z_im[..., None] * b_re
    eye = jnp.eye(16, dtype=F32)
    bb = jnp.stack([bb_re, bb_im]).reshape(2, ns, 16, P, S5_GROUP)
    wb = jnp.einsum('asgpc,gh->sgcahp', bb, eye).reshape(ns, S5_SLAB, 2 * 16 * P)
    cc = jnp.stack([c_re, -c_im]).reshape(2, ns, 16, S5_GROUP, P)
    wc = jnp.einsum('asgcp,gh->sagphc', cc, eye).reshape(ns, 2 * 16 * P, S5_SLAB)
    a = jnp.concatenate([a_re.reshape(ns, 1, 16 * P), a_im.reshape(ns, 1, 16 * P)], axis=-1)
    return wb, wc, a


def _s5_tables(lam_re, lam_im, log_dt):
    G, P = lam_re.shape
    ns = G // 16
    dt = jnp.exp(log_dt)[:, None]
    tt = jnp.arange(1, S5_CHUNK + 1, dtype=F32)[:, None, None]
    mag = jnp.exp(lam_re * dt * tt)
    ang = lam_im * dt * tt
    pr = (mag * jnp.cos(ang)).reshape(S5_CHUNK, ns, 16 * P).transpose(1, 0, 2)
    pi = (mag * jnp.sin(ang)).reshape(S5_CHUNK, ns, 16 * P).transpose(1, 0, 2)
    return pr, pi, pr[:, ::-1], pi[:, ::-1]


def _s5_scan_fwd(proj, wb, wc, pr, pi, dskip):
    S = proj.shape[0]
    ns = wb.shape[0]
    W = wb.shape[2]
    hw = W // 2
    T = S5_CHUNK
    nc = S // T
    mix = ns * S5_SLAB

    def body(u_ref, wb_ref, wc_ref, pr_ref, pi_ref, d_ref, v_ref, yg_ref, h_ref, cin_ref, carry):
        c = pl.program_id(1)

        @pl.when(c == 0)
        def _():
            carry[...] = jnp.zeros_like(carry)

        u = u_ref[...]
        bu = _dot(u, wb_ref[...], NN)
        xr, xi = bu[:, :hw], bu[:, hw:]
        row = lax.broadcasted_iota(jnp.int32, (T, hw), 0)
        d = 1
        while d < T:
            ar, ai = pr_ref[pl.ds(d - 1, 1), :], pi_ref[pl.ds(d - 1, 1), :]
            sr = jnp.where(row >= d, pltpu.roll(xr, d, 0), 0.0)
            si = jnp.where(row >= d, pltpu.roll(xi, d, 0), 0.0)
            xr, xi = xr + ar * sr - ai * si, xi + ar * si + ai * sr
            d *= 2
        cin_ref[...] = carry[...]
        cr, ci = carry[:, :hw], carry[:, hw:]
        pwr, pwi = pr_ref[...], pi_ref[...]
        hr = xr + pwr * cr - pwi * ci
        hi = xi + pwr * ci + pwi * cr
        h_ref[:, :hw] = hr
        h_ref[:, hw:] = hi
        carry[:, :hw] = hr[T - 1:T, :]
        carry[:, hw:] = hi[T - 1:T, :]
        y = _dot(h_ref[...], wc_ref[...], NN)
        v = y + d_ref[...] * u
        v_ref[...] = v
        yg_ref[...] = _gelu(v)

    return pl.pallas_call(
        body, name="s5_scan_fwd", grid=(ns, nc),
        in_specs=[pl.BlockSpec((T, S5_SLAB), lambda s, c: (c, s)),
                  pl.BlockSpec((None, S5_SLAB, W), lambda s, c: (s, 0, 0)),
                  pl.BlockSpec((None, W, S5_SLAB), lambda s, c: (s, 0, 0)),
                  pl.BlockSpec((None, T, hw), lambda s, c: (s, 0, 0)),
                  pl.BlockSpec((None, T, hw), lambda s, c: (s, 0, 0)),
                  pl.BlockSpec((1, S5_SLAB), lambda s, c: (0, s))],
        out_specs=[pl.BlockSpec((T, S5_SLAB), lambda s, c: (c, s)),
                   pl.BlockSpec((T, S5_SLAB), lambda s, c: (c, s)),
                   pl.BlockSpec((T, W), lambda s, c: (c, s)),
                   pl.BlockSpec((None, 1, W), lambda s, c: (c, 0, s))],
        out_shape=[jax.ShapeDtypeStruct((S, mix), F32), jax.ShapeDtypeStruct((S, mix), F32),
                   jax.ShapeDtypeStruct((S, ns * W), F32), jax.ShapeDtypeStruct((nc, 1, ns * W), F32)],
        scratch_shapes=[pltpu.VMEM((1, W), F32)],
        compiler_params=_cp(("parallel", "arbitrary")),
    )(proj, wb, wc, pr, pi, dskip)


def _s5_scan_bwd(dv, proj, hs, cin, wb, wc, pr, pi, prr, pir, dskip):
    S = proj.shape[0]
    ns = wb.shape[0]
    W = wb.shape[2]
    hw = W // 2
    T = S5_CHUNK
    nc = S // T
    mix = ns * S5_SLAB

    def body(dv_ref, u_ref, h_ref, cin_ref, wb_ref, wc_ref, pr_ref, pi_ref, prr_ref, pir_ref, d_ref,
             du_ref, dwb_ref, dwc_ref, da_ref, dd_ref, lam_s, carry):
        c = pl.program_id(1)

        @pl.when(c == 0)
        def _():
            carry[...] = jnp.zeros_like(carry)
            dwb_ref[...] = jnp.zeros_like(dwb_ref)
            dwc_ref[...] = jnp.zeros_like(dwc_ref)
            da_ref[...] = jnp.zeros_like(da_ref)
            dd_ref[...] = jnp.zeros_like(dd_ref)

        dy, u = dv_ref[...], u_ref[...]
        dh = _dot(dy, wc_ref[...], NT)
        gr, gi = dh[:, :hw], dh[:, hw:]
        row = lax.broadcasted_iota(jnp.int32, (T, hw), 0)
        d = 1
        while d < T:
            ar, ai = pr_ref[pl.ds(d - 1, 1), :], -pi_ref[pl.ds(d - 1, 1), :]
            sr = jnp.where(row < T - d, pltpu.roll(gr, T - d, 0), 0.0)
            si = jnp.where(row < T - d, pltpu.roll(gi, T - d, 0), 0.0)
            gr, gi = gr + ar * sr - ai * si, gi + ar * si + ai * sr
            d *= 2
        lr, li = carry[:, :hw], carry[:, hw:]
        pwr, pwi = prr_ref[...], -pir_ref[...]
        gr = gr + pwr * lr - pwi * li
        gi = gi + pwr * li + pwi * lr
        carry[:, :hw] = gr[0:1, :]
        carry[:, hw:] = gi[0:1, :]
        hr, hi = h_ref[:, :hw], h_ref[:, hw:]
        hpr = jnp.where(row >= 1, pltpu.roll(hr, 1, 0), cin_ref[:, :hw])
        hpi = jnp.where(row >= 1, pltpu.roll(hi, 1, 0), cin_ref[:, hw:])
        da_ref[:, :hw] += jnp.sum(hpr * gr + hpi * gi, axis=0, keepdims=True)
        da_ref[:, hw:] += jnp.sum(hpr * gi - hpi * gr, axis=0, keepdims=True)
        lam_s[:, :hw] = gr
        lam_s[:, hw:] = gi
        lam = lam_s[...]
        du_ref[...] = _dot(lam, wb_ref[...], NT) + dy * d_ref[...]
        dwb_ref[...] += _dot(u, lam, TN)
        dwc_ref[...] += _dot(h_ref[...], dy, TN)
        dd_ref[...] += jnp.sum(dy * u, axis=0, keepdims=True)

    def rc(c):
        return nc - 1 - c

    return pl.pallas_call(
        body, name="s5_scan_bwd", grid=(ns, nc),
        in_specs=[pl.BlockSpec((T, S5_SLAB), lambda s, c: (rc(c), s)),
                  pl.BlockSpec((T, S5_SLAB), lambda s, c: (rc(c), s)),
                  pl.BlockSpec((T, W), lambda s, c: (rc(c), s)),
                  pl.BlockSpec((None, 1, W), lambda s, c: (rc(c), 0, s)),
                  pl.BlockSpec((None, S5_SLAB, W), lambda s, c: (s, 0, 0)),
                  pl.BlockSpec((None, W, S5_SLAB), lambda s, c: (s, 0, 0)),
                  pl.BlockSpec((None, T, hw), lambda s, c: (s, 0, 0)),
                  pl.BlockSpec((None, T, hw), lambda s, c: (s, 0, 0)),
                  pl.BlockSpec((None, T, hw), lambda s, c: (s, 0, 0)),
                  pl.BlockSpec((None, T, hw), lambda s, c: (s, 0, 0)),
                  pl.BlockSpec((1, S5_SLAB), lambda s, c: (0, s))],
        out_specs=[pl.BlockSpec((T, S5_SLAB), lambda s, c: (rc(c), s)),
                   pl.BlockSpec((None, S5_SLAB, W), lambda s, c: (s, 0, 0)),
                   pl.BlockSpec((None, W, S5_SLAB), lambda s, c: (s, 0, 0)),
                   pl.BlockSpec((None, 1, W), lambda s, c: (s, 0, 0)),
                   pl.BlockSpec((1, S5_SLAB), lambda s, c: (0, s))],
        out_shape=[jax.ShapeDtypeStruct((S, mix), F32), jax.ShapeDtypeStruct(wb.shape, F32),
                   jax.ShapeDtypeStruct(wc.shape, F32), jax.ShapeDtypeStruct((ns, 1, W), F32),
                   jax.ShapeDtypeStruct((1, mix), F32)],
        scratch_shapes=[pltpu.VMEM((T, W), F32), pltpu.VMEM((1, W), F32)],
        compiler_params=_cp(("parallel", "arbitrary")),
    )(dv, proj, hs, cin, wb, wc, pr, pi, prr, pir, dskip)


def _s5_glu_bwd(dcat, yg, z):
    S, mix = yg.shape
    tr = _tile(S, ROW_TILE, 8)

    def body(do_ref, yg_ref, z_ref, dz_ref, dy_ref, db_ref):
        i = pl.program_id(0)

        @pl.when(i == 0)
        def _():
            db_ref[...] = jnp.zeros_like(db_ref)

        do, yg_, sz = do_ref[...], yg_ref[...], _sigmoid(z_ref[...])
        dz = do * yg_ * sz * (1.0 - sz)
        dz_ref[...] = dz
        dy_ref[...] = do * sz
        db_ref[...] += jnp.sum(dz, axis=0, keepdims=True)

    blk = pl.BlockSpec((tr, mix), lambda i: (i, 0))
    return pl.pallas_call(
        body, name="s5_glu_bwd", grid=(S // tr,),
        in_specs=[blk, blk, blk], out_specs=[blk, blk, pl.BlockSpec((1, mix), lambda i: (0, 0))],
        out_shape=[jax.ShapeDtypeStruct((S, mix), F32), jax.ShapeDtypeStruct((S, mix), F32),
                   jax.ShapeDtypeStruct((1, mix), F32)],
        compiler_params=_cp(("arbitrary",)),
    )(dcat, yg, z)


def _rows_down(x, j):
    return x if j == 0 else pltpu.roll(x, j, 0)


def _conv_rows(xe, w_ref, n):
    c = None
    for j in range(GDN_CONV):
        term = w_ref[pl.ds(GDN_CONV - 1 - j, 1), :] * _rows_down(xe, j)[8:8 + n, :]
        c = term if c is None else c + term
    return c


def _gdn_prep(proj, blk0, nblk, convw, norm, scale, name):
    S = proj.shape[0]
    tr = _tile(S, CONV_TILE, 8)
    nb8 = tr // 8

    def body(x_ref, xb_ref, w_ref, o_ref):
        i = pl.program_id(1)
        xe = jnp.concatenate([jnp.where(i == 0, 0.0, xb_ref[...]), x_ref[...]], axis=0)
        c = _conv_rows(xe, w_ref, tr)
        s = c * _sigmoid(c)
        if norm:
            s = s * lax.rsqrt(jnp.sum(s * s, axis=-1, keepdims=True) + EPS) * scale
        o_ref[...] = s

    return pl.pallas_call(
        body, name=name, grid=(nblk, S // tr),
        in_specs=[pl.BlockSpec((tr, HEAD), lambda j, i: (i, blk0 + j)),
                  pl.BlockSpec((8, HEAD), lambda j, i: (jnp.maximum(i * nb8 - 1, 0), blk0 + j)),
                  pl.BlockSpec((GDN_CONV, HEAD), lambda j, i: (0, j))],
        out_specs=pl.BlockSpec((tr, HEAD), lambda j, i: (i, j)),
        out_shape=jax.ShapeDtypeStruct((S, nblk * HEAD), F32),
        compiler_params=_cp(("parallel", "parallel")),
    )(proj, proj, convw)


def _gdn_prep_bwd(proj, blk0, nblk, convw, dout, norm, scale, name):
    S = proj.shape[0]
    tr = _tile(S, CONV_TILE, 8)
    nb8 = tr // 8
    last8 = S // 8 - 1
    nrow = S // tr

    def body(x_ref, xb_ref, xa_ref, w_ref, d_ref, da_ref, dx_ref, dw_ref):
        i = pl.program_id(1)

        @pl.when(i == 0)
        def _():
            dw_ref[...] = jnp.zeros_like(dw_ref)

        xe = jnp.concatenate([jnp.where(i == 0, 0.0, xb_ref[...]), x_ref[...], xa_ref[...]], axis=0)
        de = jnp.concatenate([d_ref[...], da_ref[...]], axis=0)
        n = tr + 8
        c = _conv_rows(xe, w_ref, n)
        sg = _sigmoid(c)
        s = c * sg
        if norm:
            r = lax.rsqrt(jnp.sum(s * s, axis=-1, keepdims=True) + EPS)
            ds = scale * r * (de - s * (r * r) * jnp.sum(de * s, axis=-1, keepdims=True))
        else:
            ds = de
        dc = ds * (sg + c * sg * (1.0 - sg))
        rowi = lax.broadcasted_iota(jnp.int32, (n, HEAD), 0)
        dc = jnp.where((i == nrow - 1) & (rowi >= tr), 0.0, dc)
        dct = dc[:tr, :]
        dx = None
        for j in range(GDN_CONV):
            tap = pl.ds(GDN_CONV - 1 - j, 1)
            up = dct if j == 0 else pltpu.roll(dc, n - j, 0)[:tr, :]
            term = w_ref[tap, :] * up
            dx = term if dx is None else dx + term
            dw_ref[tap, :] += jnp.sum(dct * _rows_down(xe, j)[8:8 + tr, :], axis=0, keepdims=True)
        dx_ref[...] = dx

    return pl.pallas_call(
        body, name=name, grid=(nblk, nrow),
        in_specs=[pl.BlockSpec((tr, HEAD), lambda j, i: (i, blk0 + j)),
                  pl.BlockSpec((8, HEAD), lambda j, i: (jnp.maximum(i * nb8 - 1, 0), blk0 + j)),
                  pl.BlockSpec((8, HEAD), lambda j, i: (jnp.minimum((i + 1) * nb8, last8), blk0 + j)),
                  pl.BlockSpec((GDN_CONV, HEAD), lambda j, i: (0, j)),
                  pl.BlockSpec((tr, HEAD), lambda j, i: (i, j)),
                  pl.BlockSpec((8, HEAD), lambda j, i: (jnp.minimum((i + 1) * nb8, last8), j))],
        out_specs=[pl.BlockSpec((tr, HEAD), lambda j, i: (i, j)),
                   pl.BlockSpec((GDN_CONV, HEAD), lambda j, i: (0, j))],
        out_shape=[jax.ShapeDtypeStruct((S, nblk * HEAD), F32),
                   jax.ShapeDtypeStruct((GDN_CONV, nblk * HEAD), F32)],
        compiler_params=_cp(("parallel", "arbitrary")),
    )(proj, proj, proj, convw, dout, dout)


def _gdn_gates(pg, alog, dtb):
    S = pg.shape[0]

    def body(a_ref, b_ref, al_ref, dt_ref, gc_ref, be_ref):
        g = -jnp.exp(al_ref[...]) * _softplus(a_ref[...] + dt_ref[...])
        rowm = lax.broadcasted_iota(jnp.int32, g.shape, 0) & (GDN_CHUNK - 1)
        c = g
        d = 1
        while d < GDN_CHUNK:
            c = c + jnp.where(rowm >= d, pltpu.roll(c, d, 0), 0.0)
            d *= 2
        gc_ref[...] = c
        be_ref[...] = _sigmoid(b_ref[...])

    blk = pl.BlockSpec((S, 128), lambda i: (0, 0))
    vec = pl.BlockSpec((1, 128), lambda i: (0, 0))
    return pl.pallas_call(
        body, name="gdn_gates", grid=(1,),
        in_specs=[blk, pl.BlockSpec((S, 128), lambda i: (0, 1)), vec, vec],
        out_specs=[blk, blk],
        out_shape=[jax.ShapeDtypeStruct((S, 128), F32)] * 2,
        compiler_params=_cp(("arbitrary",)),
    )(pg, pg, alog, dtb)


def _gdn_gates_bwd(pg, alog, dtb, dgc, dbeta):
    S = pg.shape[0]

    def body(a_ref, b_ref, al_ref, dt_ref, dgc_ref, dbe_ref, dpa_ref, dpb_ref, dal_ref, ddt_ref):
        rowm = lax.broadcasted_iota(jnp.int32, (S, 128), 0) & (GDN_CHUNK - 1)
        c = dgc_ref[...]
        d = 1
        while d < GDN_CHUNK:
            c = c + jnp.where(rowm < GDN_CHUNK - d, pltpu.roll(c, S - d, 0), 0.0)
            d *= 2
        xv = a_ref[...] + dt_ref[...]
        ea = jnp.exp(al_ref[...])
        g = -ea * _softplus(xv)
        dx = c * (-ea) * _sigmoid(xv)
        dpa_ref[...] = dx
        dal_ref[...] = jnp.sum(c * g, axis=0, keepdims=True)
        ddt_ref[...] = jnp.sum(dx, axis=0, keepdims=True)
        be = _sigmoid(b_ref[...])
        dpb_ref[...] = dbe_ref[...] * be * (1.0 - be)

    blk = pl.BlockSpec((S, 128), lambda i: (0, 0))
    blk1 = pl.BlockSpec((S, 128), lambda i: (0, 1))
    vec = pl.BlockSpec((1, 128), lambda i: (0, 0))
    dpa, dpb, dal, ddt = pl.pallas_call(
        body, name="gdn_gates_bwd", grid=(1,),
        in_specs=[blk, blk1, vec, vec, blk, blk],
        out_specs=[blk, blk, vec, vec],
        out_shape=[jax.ShapeDtypeStruct((S, 128), F32)] * 2 + [jax.ShapeDtypeStruct((1, 128), F32)] * 2,
        compiler_params=_cp(("arbitrary",)),
    )(pg, pg, alog, dtb, dgc, dbeta)
    return jnp.concatenate([dpa, dpb], axis=1), dal, ddt


def _gdn_pre(q, k, v, gc, gr, beta):
    C = GDN_CHUNK
    r = lax.broadcasted_iota(jnp.int32, (C, C), 0)
    c_ = lax.broadcasted_iota(jnp.int32, (C, C), 1)
    lower, strict = r >= c_, r > c_
    dec = jnp.exp(jnp.where(lower, gc - gr, -jnp.inf))
    kb, vb = k * beta, v * beta
    lmat = jnp.where(strict, _dot(kb, k, NT) * dec, 0.0)
    pk = -lmat
    tinv = jnp.where(r == c_, 1.0, 0.0) + pk
    for _ in range(5):
        pk = _dotf(pk, pk, NN)
        tinv = tinv + _dotf(tinv, pk, NN)
    e = jnp.exp(gc)
    glast = gc[C - 1:C, :]
    f = jnp.exp(glast - gc)
    gl = jnp.exp(glast)
    u = _dotf(tinv, vb, NN)
    w = _dotf(tinv, kb * e, NN)
    amat = jnp.where(lower, _dot(q, k, NT) * dec, 0.0)
    return dict(lower=lower, strict=strict, dec=dec, kb=kb, vb=vb, lmat=lmat, tinv=tinv, e=e, f=f, gl=gl,
                u=u, w=w, amat=amat, qd=q * e, kd=k * f)


def _gdn_heads_per_step(H):
    return max(d for d in (1, 2, 3, 4) if H % d == 0)


def _gdn_chunk_fwd(q, k, v, gcol, grow, bcol):
    S = q.shape[0]
    H, NC = gcol.shape[0], gcol.shape[1]
    C = GDN_CHUNK
    hb = _gdn_heads_per_step(H)

    def body(q_ref, k_ref, v_ref, gc_ref, gr_ref, b_ref, o_ref, st_ref, state):
        n = pl.program_id(1)

        @pl.when(n == 0)
        def _():
            state[...] = jnp.zeros_like(state)

        for i in range(hb):
            cols = slice(i * HEAD, (i + 1) * HEAD)
            p = _gdn_pre(q_ref[:, cols], k_ref[:, cols], v_ref[:, cols], gc_ref[i], gr_ref[i], b_ref[i])
            s0 = state[i]
            st_ref[i] = s0
            vn = p['u'] - _dot(p['w'], s0, NN)
            o_ref[:, cols] = _dot(p['qd'], s0, NN) + _dot(p['amat'], vn, NN)
            state[i] = s0 * p['gl'] + _dot(p['kd'], vn, TN)

    tok = pl.BlockSpec((C, hb * HEAD), lambda h, n: (n, h))
    col = pl.BlockSpec((hb, None, C, 1), lambda h, n: (h, n, 0, 0))
    rowb = pl.BlockSpec((hb, None, 1, C), lambda h, n: (h, n, 0, 0))
    return pl.pallas_call(
        body, name="gdn_chunk_fwd", grid=(H // hb, NC),
        in_specs=[tok, tok, tok, col, rowb, col],
        out_specs=[tok, pl.BlockSpec((hb, None, HEAD, HEAD), lambda h, n: (h, n, 0, 0))],
        out_shape=[jax.ShapeDtypeStruct((S, H * HEAD), F32), jax.ShapeDtypeStruct((H, NC, HEAD, HEAD), F32)],
        scratch_shapes=[pltpu.VMEM((hb, HEAD, HEAD), F32)],
        compiler_params=_cp(("parallel", "arbitrary")),
    )(q, k, v, gcol, grow, bcol)


def _gdn_chunk_bwd(q, k, v, gcol, grow, bcol, st, do):
    S = q.shape[0]
    H, NC = gcol.shape[0], gcol.shape[1]
    C = GDN_CHUNK
    hb = _gdn_heads_per_step(H)

    def one_head(q, k, v, gc, gr, beta, s0, do_, ds1):
        p = _gdn_pre(q, k, v, gc, gr, beta)
        lower, strict, dec = p['lower'], p['strict'], p['dec']
        vn = p['u'] - _dot(p['w'], s0, NN)
        dvn = _dot(p['amat'], do_, TN) + _dot(p['kd'], ds1, NN)
        damat = jnp.where(lower, _dot(do_, vn, NT), 0.0)
        dqd = _dot(do_, s0, NT)
        dkd = _dot(vn, ds1, NT)
        dgl = jnp.sum(s0 * ds1, keepdims=True)
        ds0 = p['gl'] * ds1 + _dot(p['qd'], do_, TN) - _dot(p['w'], dvn, TN)
        dw = -_dot(dvn, s0, NT)
        dvb = _dotf(p['tinv'], dvn, TN)
        dkg = _dotf(p['tinv'], dw, TN)
        dl = -jnp.where(strict, _dotf(dvb, p['u'], NT) + _dotf(dkg, p['w'], NT), 0.0)
        dkk = dl * dec
        dqk = damat * dec
        m = dl * p['lmat'] + damat * p['amat']
        dkb = _dot(dkk, k, NN) + dkg * p['e']
        dk = _dot(dkk, p['kb'], TN) + _dot(dqk, q, TN) + dkd * p['f'] + dkb * beta
        dq = _dot(dqk, k, NN) + dqd * p['e']
        de = jnp.sum(dkg * p['kb'], axis=-1, keepdims=True) + jnp.sum(dqd * q, axis=-1, keepdims=True)
        df = jnp.sum(dkd * k, axis=-1, keepdims=True)
        dbeta = jnp.sum(dkb * k, axis=-1, keepdims=True) + jnp.sum(dvb * v, axis=-1, keepdims=True)
        colsum = _dotf(m, jnp.ones((C, HEAD), F32), TN)[:, 0:1]
        dgc = jnp.sum(m, axis=-1, keepdims=True) - colsum + de * p['e'] - df * p['f']
        dlast = jnp.sum(df * p['f'], keepdims=True) + dgl * p['gl']
        rowi = lax.broadcasted_iota(jnp.int32, (C, 1), 0)
        dgc = dgc + jnp.where(rowi == C - 1, dlast, 0.0)
        return dq, dk, dvb * beta, dgc, dbeta, ds0

    def body(q_ref, k_ref, v_ref, gc_ref, gr_ref, b_ref, st_ref, do_ref,
             dq_ref, dk_ref, dv_ref, dgc_ref, dbe_ref, dstate):
        n = pl.program_id(1)

        @pl.when(n == 0)
        def _():
            dstate[...] = jnp.zeros_like(dstate)

        for i in range(hb):
            cols = slice(i * HEAD, (i + 1) * HEAD)
            dq, dk, dv, dgc, dbeta, ds0 = one_head(q_ref[:, cols], k_ref[:, cols], v_ref[:, cols], gc_ref[i],
                                                   gr_ref[i], b_ref[i], st_ref[i], do_ref[:, cols], dstate[i])
            dstate[i] = ds0
            dq_ref[:, cols] = dq
            dk_ref[:, cols] = dk
            dv_ref[:, cols] = dv
            dgc_ref[i] = dgc
            dbe_ref[i] = dbeta

    def rn(n):
        return NC - 1 - n

    tok = pl.BlockSpec((C, hb * HEAD), lambda h, n: (rn(n), h))
    col = pl.BlockSpec((hb, None, C, 1), lambda h, n: (h, rn(n), 0, 0))
    rowb = pl.BlockSpec((hb, None, 1, C), lambda h, n: (h, rn(n), 0, 0))
    return pl.pallas_call(
        body, name="gdn_chunk_bwd", grid=(H // hb, NC),
        in_specs=[tok, tok, tok, col, rowb, col,
                  pl.BlockSpec((hb, None, HEAD, HEAD), lambda h, n: (h, rn(n), 0, 0)), tok],
        out_specs=[tok, tok, tok, col, col],
        out_shape=[jax.ShapeDtypeStruct((S, H * HEAD), F32)] * 3
        + [jax.ShapeDtypeStruct((H, NC, C, 1), F32)] * 2,
        scratch_shapes=[pltpu.VMEM((hb, HEAD, HEAD), F32)],
        compiler_params=_cp(("parallel", "arbitrary")),
    )(q, k, v, gcol, grow, bcol, st, do)


def _gdn_onorm(o, proj, gate_blk, w, H):
    S = o.shape[0]
    tr = _tile(S, CONV_TILE, 8)

    def body(o_ref, g_ref, w_ref, out_ref):
        ov, gv = o_ref[...], g_ref[...]
        r = lax.rsqrt(jnp.mean(ov * ov, axis=-1, keepdims=True) + EPS)
        out_ref[...] = (ov * r * w_ref[...]) * (gv * _sigmoid(gv))

    return pl.pallas_call(
        body, name="gdn_onorm", grid=(S // tr, H),
        in_specs=[pl.BlockSpec((tr, HEAD), lambda i, h: (i, h)),
                  pl.BlockSpec((tr, HEAD), lambda i, h: (i, gate_blk + h)),
                  pl.BlockSpec((1, HEAD), lambda i, h: (0, 0))],
        out_specs=pl.BlockSpec((tr, HEAD), lambda i, h: (i, h)),
        out_shape=jax.ShapeDtypeStruct((S, H * HEAD), F32),
        compiler_params=_cp(("parallel", "parallel")),
    )(o, proj, w)


def _gdn_onorm_bwd(dcat, o, proj, gate_blk, w, H):
    S = o.shape[0]
    tr = _tile(S, CONV_TILE, 8)

    def body(d_ref, o_ref, g_ref, w_ref, do_ref, dg_ref, dw_ref):
        i, h = pl.program_id(0), pl.program_id(1)

        @pl.when((i == 0) & (h == 0))
        def _():
            dw_ref[...] = jnp.zeros_like(dw_ref)

        dm, ov, gv, wv = d_ref[...], o_ref[...], g_ref[...], w_ref[...]
        r = lax.rsqrt(jnp.mean(ov * ov, axis=-1, keepdims=True) + EPS)
        oh = ov * r
        sg = gv * _sigmoid(gv)
        dy = dm * sg
        t = dy * wv
        do_ref[...] = r * (t - oh * jnp.mean(t * oh, axis=-1, keepdims=True))
        dg_ref[...] = dm * (oh * wv) * _silu_grad(gv)
        dw_ref[...] += jnp.sum(dy * oh, axis=0, keepdims=True)

    tok = pl.BlockSpec((tr, HEAD), lambda i, h: (i, h))
    vec = pl.BlockSpec((1, HEAD), lambda i, h: (0, 0))
    return pl.pallas_call(
        body, name="gdn_onorm_bwd", grid=(S // tr, H),
        in_specs=[tok, tok, pl.BlockSpec((tr, HEAD), lambda i, h: (i, gate_blk + h)), vec],
        out_specs=[tok, tok, vec],
        out_shape=[jax.ShapeDtypeStruct((S, H * HEAD), F32)] * 2 + [jax.ShapeDtypeStruct((1, HEAD), F32)],
        compiler_params=_cp(("arbitrary", "arbitrary")),
    )(dcat, o, proj, w)


def _lanes_to_heads(a, H):
    return a[:, :H].T


def _heads_to_lanes(a):
    H = a.shape[0]
    return jnp.pad(a.T, ((0, 0), (0, 128 - H)))


def _take_cols(segs, a, b):
    out, off = [], 0
    for sg in segs:
        w = sg.shape[-1]
        lo, hi = max(a, off), min(b, off + w)
        if lo < hi:
            out.append(sg[..., lo - off:hi - off])
        off += w
    return out


def _pad_cols(pieces):
    m = jnp.concatenate(pieces, axis=-1)
    return jnp.pad(m, ((0, 0), (0, 128 - m.shape[-1])))


def _pad_lanes(v):
    return jnp.pad(v.reshape(1, -1), ((0, 0), (0, 128 - v.shape[-1])))


def _s5_layer_fwd(a, w, cfg):
    proj = _mm(a, w['w_in'], name="s5_in")
    wb, wc, _ = w['prep']
    pr, pi, prr, pir = w['tables']
    v, yg, hs, cin = _s5_scan_fwd(proj, wb, wc, pr, pi, w['d_skip'])
    z, mix = _mm(yg, w['w_glu'], name="s5_glu", extras=[(yg, 'ij'), (w['b_glu'], 'j')],
                 epi=lambda acc, y, b: (acc + b, y * _sigmoid(acc + b)), out_dtypes=(F32, F32))
    return proj, mix, dict(v=v, yg=yg, hs=hs, cin=cin, z=z)


def _s5_layer_bwd(a, w, proj, sv, dcat, dmemq, cfg):
    wb, wc, _ = w['prep']
    pr, pi, prr, pir = w['tables']
    dz, dyg1, db_glu = _s5_glu_bwd(dcat, sv['yg'], sv['z'])
    dw_glu = _mm(sv['yg'], dz, name="s5_dwglu", ta=True)
    dv = _mm(dz, w['w_glu'], name="s5_dyg", tb=True, extras=[(dyg1, 'ij'), (sv['v'], 'ij')],
             epi=lambda acc, d1, vv: ((acc + d1) * _gelu_grad(vv),))
    du, dwb, dwc, da, dd = _s5_scan_bwd(dv, proj, sv['hs'], sv['cin'], wb, wc, pr, pi, prr, pir, w['d_skip'])
    dproj = jnp.concatenate([du, dmemq], axis=1)
    dw_in = _mm(a, dproj, name="s5_dwin", ta=True)
    da_in = _mm(dproj, w['w_in'], name="s5_da", tb=True)
    dlre, dlim, dldt, dbre, dbim, dcre, dcim = w['prep_vjp']((dwb, dwc, da))
    grads = dict(w_in=dw_in, w_glu=dw_glu, b_glu=db_glu[0], d_skip=dd[0], lam_re=dlre, lam_im=dlim,
                 log_dt=dldt, b_re=dbre, b_im=dbim, c_re=dcre, c_im=dcim)
    return da_in, grads


def _gdn_relayout(a, H, NC):
    t = _lanes_to_heads(a, H).reshape(H, NC, GDN_CHUNK)
    return t[..., None], t[:, :, None, :]


def _gdn_layer_fwd(a, w, cfg):
    H, MIX, S = cfg['H'], cfg['MIX'], a.shape[0]
    NC = S // GDN_CHUNK
    proj = _mm(a, w['w_main'], name="gdn_in")
    pg = _mm(a, w['w_gate'], name="gdn_in_gates")
    cw = w['conv_w']
    q = _gdn_prep(proj, 0, H, cw[:, :MIX], True, HEAD ** -0.5, "gdn_prep_q")
    k = _gdn_prep(proj, H, H, cw[:, MIX:2 * MIX], True, 1.0, "gdn_prep_k")
    v = _gdn_prep(proj, 2 * H, H, cw[:, 2 * MIX:], False, 1.0, "gdn_prep_v")
    gc, beta = _gdn_gates(pg, w['a_log'], w['dt_bias'])
    gcol, grow = _gdn_relayout(gc, H, NC)
    bcol, _ = _gdn_relayout(beta, H, NC)
    o, st = _gdn_chunk_fwd(q, k, v, gcol, grow, bcol)
    mix = _gdn_onorm(o, proj, 3 * H, w['o_norm'], H)
    return proj, mix, dict(pg=pg, q=q, k=k, v=v, gcol=gcol, grow=grow, bcol=bcol, o=o, st=st)


def _gdn_layer_bwd(a, w, proj, sv, dcat, dmemq, cfg):
    H, MIX, S = cfg['H'], cfg['MIX'], a.shape[0]
    cw = w['conv_w']
    do, dgate, donorm = _gdn_onorm_bwd(dcat, sv['o'], proj, 3 * H, w['o_norm'], H)
    dq, dk, dv, dgcol, dbcol = _gdn_chunk_bwd(sv['q'], sv['k'], sv['v'], sv['gcol'], sv['grow'], sv['bcol'],
                                              sv['st'], do)
    dgc = _heads_to_lanes(dgcol.reshape(H, S))
    dbeta = _heads_to_lanes(dbcol.reshape(H, S))
    dpg, dalog, ddtb = _gdn_gates_bwd(sv['pg'], w['a_log'], w['dt_bias'], dgc, dbeta)
    dxq, dwq = _gdn_prep_bwd(proj, 0, H, cw[:, :MIX], dq, True, HEAD ** -0.5, "gdn_prep_bwd_q")
    dxk, dwk = _gdn_prep_bwd(proj, H, H, cw[:, MIX:2 * MIX], dk, True, 1.0, "gdn_prep_bwd_k")
    dxv, dwv = _gdn_prep_bwd(proj, 2 * H, H, cw[:, 2 * MIX:], dv, False, 1.0, "gdn_prep_bwd_v")
    dproj = jnp.concatenate([dxq, dxk, dxv, dgate, dmemq], axis=1)
    dw_main = _mm(a, dproj, name="gdn_dwmain", ta=True)
    dw_gate = _mm(a, dpg, name="gdn_dwgate", ta=True)
    da1 = _mm(dpg, w['w_gate'], name="gdn_da_gates", tb=True)
    da_in = _mm(dproj, w['w_main'], name="gdn_da", tb=True, extras=[(da1, 'ij')], epi=lambda acc, e: (acc + e,))
    grads = dict(w_main=dw_main, w_gate=dw_gate, conv_w=jnp.concatenate([dwq, dwk, dwv], axis=1),
                 a_log=dalog[0, :H], dt_bias=ddtb[0, :H], o_norm=donorm[0])
    return da_in, grads


def _fox_layer_fwd(a, w, cfg):
    H = cfg['H']
    proj = _mm(a, w['w_main'], name="fox_in")
    pg = _mm(a, w['w_gate'], name="fox_in_gates")
    cf = _fox_gates(pg, w['b_f'])
    cfh = _lanes_to_heads(cf, H)
    cfq, cfk = cfh[:, :, None], cfh[:, None, :]
    o, lse = _fox_fwd(proj, cfq, cfk, H)
    return proj, o, dict(pg=pg, cfq=cfq, cfk=cfk, lse=lse)


def _fox_layer_bwd(a, w, proj, sv, dcat, dmemq, cfg):
    H = cfg['H']
    rowdot = _fox_bwd_rowdot(proj, sv['cfq'], sv['cfk'], sv['lse'], dcat, H)
    dq, dk, dv, dck = _fox_bwd(proj, sv['cfq'], sv['cfk'], rowdot, sv['lse'], dcat, H)
    dpg, dbf = _fox_gates_bwd(sv['pg'], w['b_f'], _heads_to_lanes(dck[:, 0, :]))
    dproj = jnp.concatenate([dq, dk, dv, dmemq], axis=1)
    dw_main = _mm(a, dproj, name="fox_dwmain", ta=True)
    dw_gate = _mm(a, dpg, name="fox_dwgate", ta=True)
    da1 = _mm(dpg, w['w_gate'], name="fox_da_gates", tb=True)
    da_in = _mm(dproj, w['w_main'], name="fox_da", tb=True, extras=[(da1, 'ij')], epi=lambda acc, e: (acc + e,))
    grads = dict(w_main=dw_main, w_gate=dw_gate, b_f=dbf[0, :H])
    return da_in, grads


_LAYER_FWD = (_s5_layer_fwd, _gdn_layer_fwd, _fox_layer_fwd)
_LAYER_BWD = (_s5_layer_bwd, _gdn_layer_bwd, _fox_layer_bwd)


def _mixer_weights(kind, j, fw, p, cfg):
    H, MIX, MW = cfg['H'], cfg['MIX'], cfg['MW']
    if kind == 0:
        params = tuple(p[n][j] for n in ('s5_lam_re', 's5_lam_im', 's5_log_dt', 's5_b_re', 's5_b_im',
                                         's5_c_re', 's5_c_im'))
        prep, prep_vjp = jax.vjp(_s5_prep, *params)
        prep = (prep[0].astype(MXU_DTYPE), prep[1].astype(MXU_DTYPE), prep[2])
        tables = _s5_tables(*params[:3])
        return dict(w_in=fw['s5_w_in'][j], w_glu=fw['s5_w_glu'][j], b_glu=fw['s5_b_glu'][j],
                    d_skip=fw['s5_d_skip'][j].reshape(1, MIX), prep=prep, prep_vjp=prep_vjp, tables=tables)
    if kind == 1:
        segs = fw['gdn_w_in'][j]
        c0 = 4 * MIX
        total = c0 + 2 * H + MW
        w_main = jnp.concatenate(_take_cols(segs, 0, c0) + _take_cols(segs, c0 + 2 * H, total), axis=1)
        w_gate = jnp.concatenate([_pad_cols(_take_cols(segs, c0, c0 + H)),
                                  _pad_cols(_take_cols(segs, c0 + H, c0 + 2 * H))], axis=1)
        return dict(w_main=w_main, w_gate=w_gate, conv_w=fw['gdn_conv_w'][j],
                    a_log=_pad_lanes(p['gdn_a_log'][j]), dt_bias=_pad_lanes(p['gdn_dt_bias'][j]),
                    o_norm=p['gdn_o_norm'][j].reshape(1, HEAD))
    segs = fw['fox_w_in'][j]
    c0 = 3 * MIX
    total = c0 + H + MW
    w_main = jnp.concatenate(_take_cols(segs, 0, c0) + _take_cols(segs, c0 + H, total), axis=1)
    w_gate = _pad_cols(_take_cols(segs, c0, c0 + H))
    return dict(w_main=w_main, w_gate=w_gate, b_f=_pad_lanes(p['fox_b_f'][j]))


def _local_step(p, fw, cfg):
    H, MIX, MW, MH, depth = cfg['H'], cfg['MIX'], cfg['MW'], cfg['MH'], cfg['depth']
    x, mem, target = p['x'], p['mem'], p['loss_target']
    q_blk = {0: MIX // HEAD, 1: 4 * MIX // HEAD, 2: 3 * MIX // HEAD}

    mem_n = _rms_fwd(mem, p['mem_norm'], MXU_DTYPE, "mem_rms")
    mkv = _mm(mem_n, fw['w_mem_kv'], name="mem_kv")

    h = x
    saved = []
    for i in range(depth):
        kind, j = i % 3, i // 3
        w = _mixer_weights(kind, j, fw, p, cfg)
        a = _rms_fwd(h, p['norm1'][i], MXU_DTYPE, "rms1")
        proj, mix, sv = _LAYER_FWD[kind](a, w, cfg)
        read = _mem_fwd(proj, q_blk[kind], mkv, MH)
        cat = jnp.concatenate([mix, read], axis=1)
        h1 = _mm(cat, fw['w_out'][i], name="out_proj", extras=[(h, 'ij')], epi=lambda acc, r: (acc + r,))
        a2 = _rms_fwd(h1, p['norm2'][i], MXU_DTYPE, "rms2")
        u = _mm(a2, fw['w_up'][i], name="mlp_up")
        h2 = _mm(u, fw['w_down'][i], name="mlp_down", a_pro=_relu2, extras=[(h1, 'ij')],
                 epi=lambda acc, r: (acc + r,))
        saved.append(dict(w=w, h=h, a=a, proj=proj, sv=sv, cat=cat, h1=h1, a2=a2, u=u))
        h = h2

    loss, dh, dnorm_f = _loss_head(h, p['norm_f'], target)

    g = {n: None for n in WEIGHTS}
    g['norm_f'] = dnorm_f[0]
    per_layer = {n: [None] * depth for n in ('norm1', 'norm2', 'w_out', 'w_up', 'w_down')}
    mix_grads = {0: {}, 1: {}, 2: {}}
    dmkv = None
    for i in reversed(range(depth)):
        kind, j = i % 3, i // 3
        s = saved[i]
        w = s['w']
        du = _mm(dh, fw['w_down'][i], name="mlp_ddown", tb=True, extras=[(s['u'], 'ij')],
                 epi=lambda acc, uu: (acc * (2.0 * jnp.maximum(uu, 0.0)),))
        per_layer['w_down'][i] = _mm(s['u'], dh, name="mlp_dwdown", ta=True, a_pro=_relu2)
        per_layer['w_up'][i] = _mm(s['a2'], du, name="mlp_dwup", ta=True)
        da2 = _mm(du, fw['w_up'][i], name="mlp_dup", tb=True)
        dh1, dn2 = _rms_bwd(s['h1'], p['norm2'][i], da2, dh, "rms2_bwd")
        per_layer['norm2'][i] = dn2[0]
        dcat = _mm(dh1, fw['w_out'][i], name="out_dproj", tb=True)
        per_layer['w_out'][i] = _mm(s['cat'], dh1, name="out_dw", ta=True)
        dmemq, dmkv_i = _mem_bwd(s['proj'], q_blk[kind], mkv, dcat, MIX // HEAD, MH)
        dmkv = dmkv_i if dmkv is None else dmkv + dmkv_i
        da, mg = _LAYER_BWD[kind](s['a'], w, s['proj'], s['sv'], dcat, dmemq, cfg)
        mix_grads[kind][j] = mg
        dh, dn1 = _rms_bwd(s['h'], p['norm1'][i], da, dh1, "rms1_bwd")
        per_layer['norm1'][i] = dn1[0]
    for n in ('norm1', 'norm2'):
        g[n] = jnp.stack(per_layer[n])
    for n in ('w_out', 'w_up', 'w_down'):
        g[n] = per_layer[n]

    g['w_mem_kv'] = _mm(mem_n, dmkv, name="mem_dwkv", ta=True)
    dmem_n = _mm(dmkv, fw['w_mem_kv'], name="mem_dn", tb=True)
    _, dmn = _rms_bwd(mem, p['mem_norm'], dmem_n, None, "mem_rms_bwd")
    g['mem_norm'] = dmn[0]

    def layers(kind, key):
        return [mix_grads[kind][j][key] for j in sorted(mix_grads[kind])]

    g['s5_w_in'] = layers(0, 'w_in')
    g['s5_w_glu'] = layers(0, 'w_glu')
    for n in ('b_glu', 'd_skip', 'lam_re', 'lam_im', 'log_dt', 'b_re', 'b_im', 'c_re', 'c_im'):
        g['s5_' + n] = jnp.stack(layers(0, n))
    c0 = 4 * MIX
    g['gdn_w_in'] = [[wm[:, :c0], wg[:, :H], wg[:, 128:128 + H], wm[:, c0:]]
                     for wm, wg in zip(layers(1, 'w_main'), layers(1, 'w_gate'))]
    for n in ('conv_w', 'a_log', 'dt_bias', 'o_norm'):
        g['gdn_' + n] = jnp.stack(layers(1, n))
    c0 = 3 * MIX
    g['fox_w_in'] = [[wm[:, :c0], wg[:, :H], wm[:, c0:]]
                     for wm, wg in zip(layers(2, 'w_main'), layers(2, 'w_gate'))]
    g['fox_b_f'] = jnp.stack(layers(2, 'b_f'))
    return loss, dh, g


def _gather_all(p, me):
    fw = {}

    def rows(name):
        w = p[name]
        L, r, n = w.shape
        got = _gather(w.astype(MXU_DTYPE).reshape(L * r, n), me, "all_gather_" + name)
        return [got[:, i * r:(i + 1) * r, :].reshape(4 * r, n) for i in range(L)]

    def cols(name):
        w = p[name]
        L, r, n = w.shape
        got = _gather(w.astype(MXU_DTYPE).reshape(L * r, n), me, "all_gather_" + name)
        return [[got[s, i * r:(i + 1) * r, :] for s in range(4)] for i in range(L)]

    kv = p['w_mem_kv']
    fw['w_mem_kv'] = _gather(kv.astype(MXU_DTYPE), me, "all_gather_w_mem_kv").reshape(4 * kv.shape[0], kv.shape[1])
    for name in ('w_out', 'w_down', 's5_w_in', 's5_w_glu'):
        fw[name] = rows(name)
    fw['w_up'] = [jnp.concatenate(blocks, axis=1) for blocks in cols('w_up')]
    fw['gdn_w_in'] = cols('gdn_w_in')
    fw['fox_w_in'] = cols('fox_w_in')

    vec = _gather(_pack_small([p[n] for n in VECTOR_SHARDED], 16), me, "all_gather_vectors").reshape(4, -1)
    off = 0
    for n in VECTOR_SHARDED:
        sz = p[n].size
        stacked = vec[:, off:off + sz].reshape((4,) + p[n].shape)
        ax = SHARD_AXIS[n]
        t = jnp.moveaxis(stacked, 0, ax)
        shp = list(t.shape)
        fw[n] = t.reshape(shp[:ax] + [shp[ax] * shp[ax + 1]] + shp[ax + 2:])
        off += sz
    return fw


def _scatter_all(g, p, place):
    grads = {}

    def rows(name):
        L, r, n = p[name].shape
        blocks = jnp.stack([gl.reshape(4, r, n) for gl in g[name]], axis=1).reshape(4, L * r, n)
        return _reduce_scatter(blocks, place).reshape(L, r, n)

    def shards_of_cols(segs, n):
        return jnp.stack([jnp.concatenate(_take_cols(segs, s * n, (s + 1) * n), axis=1) for s in range(4)])

    kv = p['w_mem_kv']
    grads['w_mem_kv'] = _reduce_scatter(g['w_mem_kv'].reshape((4,) + kv.shape), place)
    for name in ('w_out', 's5_w_in', 's5_w_glu'):
        grads[name] = rows(name)
    n = p['w_down'].shape[2]
    grads['w_down'] = jnp.stack([_reduce_scatter(gl.reshape(4, -1, n), place) for gl in g['w_down']])
    n = p['w_up'].shape[2]
    grads['w_up'] = jnp.stack([_reduce_scatter(shards_of_cols([gl], n), place) for gl in g['w_up']])
    for name in ('gdn_w_in', 'fox_w_in'):
        n = p[name].shape[2]
        grads[name] = jnp.stack([_reduce_scatter(shards_of_cols(segs, n), place) for segs in g[name]])

    parts = []
    for name in VECTOR_SHARDED:
        ax = SHARD_AXIS[name]
        shp = list(g[name].shape)
        t = g[name].reshape(shp[:ax] + [4, shp[ax] // 4] + shp[ax + 1:])
        parts.append(jnp.moveaxis(t, ax, 0).reshape(4, -1))
    flat = jnp.concatenate(parts, axis=1)
    flat = jnp.pad(flat, ((0, 0), (0, 16 * LANES - flat.shape[1]))).reshape(4, 16, LANES)
    red = _reduce_scatter(flat, place).reshape(-1)
    off = 0
    for name in VECTOR_SHARDED:
        grads[name] = red[off:off + p[name].size].reshape(p[name].shape)
        off += p[name].size
    return grads


def kernel(x, mem, mem_norm, w_mem_kv, norm1, w_out, norm2, w_up, w_down, norm_f, s5_w_in, s5_lam_re, s5_lam_im, s5_log_dt, s5_b_re, s5_b_im, s5_c_re, s5_c_im, s5_d_skip, s5_w_glu, s5_b_glu, gdn_w_in, gdn_conv_w, gdn_a_log, gdn_dt_bias, gdn_o_norm, fox_w_in, fox_b_f, loss_target, m_mem_norm, m_w_mem_kv, m_norm1, m_w_out, m_norm2, m_w_up, m_w_down, m_norm_f, m_s5_w_in, m_s5_lam_re, m_s5_lam_im, m_s5_log_dt, m_s5_b_re, m_s5_b_im, m_s5_c_re, m_s5_c_im, m_s5_d_skip, m_s5_w_glu, m_s5_b_glu, m_gdn_w_in, m_gdn_conv_w, m_gdn_a_log, m_gdn_dt_bias, m_gdn_o_norm, m_fox_w_in, m_fox_b_f, v_mem_norm, v_w_mem_kv, v_norm1, v_w_out, v_norm2, v_w_up, v_w_down, v_norm_f, v_s5_w_in, v_s5_lam_re, v_s5_lam_im, v_s5_log_dt, v_s5_b_re, v_s5_b_im, v_s5_c_re, v_s5_c_im, v_s5_d_skip, v_s5_w_glu, v_s5_b_glu, v_gdn_w_in, v_gdn_conv_w, v_gdn_a_log, v_gdn_dt_bias, v_gdn_o_norm, v_fox_w_in, v_fox_b_f):
    args = locals()
    p = {n: args[n] for n in WEIGHTS}
    mom = {n: args['m_' + n] for n in WEIGHTS}
    var = {n: args['v_' + n] for n in WEIGHTS}
    S, D = x.shape[1], x.shape[2]
    MW = w_mem_kv.shape[1] // 2
    MIX = D - MW
    cfg = dict(H=MIX // HEAD, MIX=MIX, MW=MW, MH=MW // HEAD, depth=norm1.shape[0])
    p.update(x=x.reshape(S, D), mem=mem.reshape(mem.shape[1], D), loss_target=loss_target.reshape(S, D))
    c = lax.axis_index("c")
    me = 2 * lax.axis_index("x") + lax.axis_index("y")
    place = dict(c=c, c_idx=c.astype(jnp.int32).reshape(1), me_idx=me.astype(jnp.int32).reshape(1))

    fw = _gather_all(p, me)
    loss, dx, g = _local_step(p, fw, cfg)
    grads = _scatter_all(g, p, place)

    n_small = sum(p[n].size for n in REPLICATED)
    rows = -(-n_small // LANES // 8) * 8
    small = _all_reduce_small(_pack_small([g[n] for n in REPLICATED], rows), "all_reduce_small").reshape(-1)
    off = 0
    for n in REPLICATED:
        grads[n] = small[off:off + p[n].size].reshape(p[n].shape)
        off += p[n].size

    delta, new_m, new_v = {}, {}, {}
    for n in SHARD_AXIS:
        shp = p[n].shape
        two_d = (-1, shp[-1])
        d, nm, nv = _adamw(p[n].reshape(two_d), grads[n].reshape(two_d), mom[n].reshape(two_d),
                           var[n].reshape(two_d), "adamw_" + n)
        delta[n], new_m[n], new_v[n] = d.reshape(shp), nm.reshape(shp), nv.reshape(shp)
    d, nm, nv = _adamw(*[_pack_small([src[n] for n in REPLICATED], rows) for src in (p, grads, mom, var)],
                       "adamw_small")
    d, nm, nv = d.reshape(-1), nm.reshape(-1), nv.reshape(-1)
    off = 0
    for n in REPLICATED:
        sz, shp = p[n].size, p[n].shape
        delta[n], new_m[n], new_v[n] = (d[off:off + sz].reshape(shp), nm[off:off + sz].reshape(shp),
                                        nv[off:off + sz].reshape(shp))
        off += sz

    total = lax.psum(loss[0, 0], ("x", "y", "c"))
    return (total, dx.reshape(x.shape), *[grads[n] for n in WEIGHTS], *[delta[n] for n in WEIGHTS],
            *[new_m[n] for n in WEIGHTS], *[new_v[n] for n in WEIGHTS])
```

```python
import math

import jax
import jax.numpy as jnp
import numpy as np
from jax import lax
from jax.experimental import pallas as pl
from jax.experimental.pallas import tpu as pltpu

F32 = jnp.float32
MXU_DTYPE = jnp.bfloat16
EPS = 1e-6
HEAD = 128
S5_GROUP = 16
S5_STATE = 64
S5_SLAB = 256
S5_CHUNK = 128
S5_ROWS = 8
GDN_CHUNK = 64
GDN_CONV = 4
LANES = 1024
VMEM_LIMIT_BYTES = 56 * 1024 * 1024
MESH = pl.DeviceIdType.MESH
RS_PAYLOAD = jnp.bfloat16

ADAM_LR, ADAM_B1, ADAM_B2, ADAM_EPS, ADAM_WD, ADAM_STEP = 0.001, 0.9, 0.999, 1e-08, 0.01, 10

MM_TM, MM_TN, MM_TK = 1024, 1024, 1024
ROW_TILE = 256
FOX_TILE = 512
MEM_TILE = 512
CONV_TILE = 1024

NN = (((1,), (0,)), ((), ()))
NT = (((1,), (1,)), ((), ()))
TN = (((0,), (0,)), ((), ()))

WEIGHTS = ['mem_norm', 'w_mem_kv', 'norm1', 'w_out', 'norm2', 'w_up', 'w_down', 'norm_f', 's5_w_in',
           's5_lam_re', 's5_lam_im', 's5_log_dt', 's5_b_re', 's5_b_im', 's5_c_re', 's5_c_im', 's5_d_skip',
           's5_w_glu', 's5_b_glu', 'gdn_w_in', 'gdn_conv_w', 'gdn_a_log', 'gdn_dt_bias', 'gdn_o_norm',
           'fox_w_in', 'fox_b_f']
SHARD_AXIS = {'w_mem_kv': 0, 'w_out': 1, 'w_up': 2, 'w_down': 1, 's5_w_in': 1, 's5_d_skip': 1,
              's5_w_glu': 1, 's5_b_glu': 1, 'gdn_w_in': 2, 'gdn_conv_w': 2, 'fox_w_in': 2}
MATMUL_WEIGHTS = ['w_mem_kv', 'w_out', 'w_up', 'w_down', 's5_w_in', 's5_w_glu', 'gdn_w_in', 'fox_w_in']
VECTOR_SHARDED = ['s5_d_skip', 's5_b_glu', 'gdn_conv_w']
REPLICATED = [n for n in WEIGHTS if n not in SHARD_AXIS]


def _tile(dim, target, align=128):
    if dim <= target:
        return dim
    t = (target // align) * align
    while t >= align:
        if dim % t == 0:
            return t
        t -= align
    return dim


def _cp(sem=None, **kw):
    return pltpu.CompilerParams(dimension_semantics=sem, vmem_limit_bytes=VMEM_LIMIT_BYTES, **kw)


def _dot(a, b, dims):
    return lax.dot_general(a.astype(MXU_DTYPE), b.astype(MXU_DTYPE), dims, preferred_element_type=F32)


def _dotf(a, b, dims):
    return lax.dot_general(a, b, dims, precision=lax.Precision.HIGHEST, preferred_element_type=F32)


def _sigmoid(x):
    return 1.0 / (1.0 + jnp.exp(-x))


def _softplus(x):
    return jnp.maximum(x, 0.0) + jnp.log(1.0 + jnp.exp(-jnp.abs(x)))


def _relu2(x):
    r = jnp.maximum(x, 0.0)
    return r * r


_GELU_C = math.sqrt(2.0 / math.pi)


def _gelu(x):
    return 0.5 * x * (1.0 + jnp.tanh(_GELU_C * (x + 0.044715 * x * x * x)))


def _gelu_grad(x):
    t = jnp.tanh(_GELU_C * (x + 0.044715 * x * x * x))
    return 0.5 * (1.0 + t) + 0.5 * x * (1.0 - t * t) * _GELU_C * (1.0 + 3.0 * 0.044715 * x * x)


def _silu_grad(x):
    s = _sigmoid(x)
    return s + x * s * (1.0 - s)


def _mm(a, b, *, name, ta=False, tb=False, a_pro=None, extras=(), epi=None, out_dtypes=(F32,)):
    K, M = a.shape if ta else a.shape[::-1]
    N = b.shape[0] if tb else b.shape[1]
    assert (b.shape[1] if tb else b.shape[0]) == K, (a.shape, b.shape, ta, tb)
    tm, tn, tk = _tile(M, MM_TM), _tile(N, MM_TN), _tile(K, MM_TK)
    nk = K // tk
    n_ex, n_out = len(extras), len(out_dtypes)
    dims = TN if ta else (NT if tb else NN)

    def body(*refs):
        a_ref, b_ref = refs[0], refs[1]
        ex = refs[2:2 + n_ex]
        outs = refs[2 + n_ex:2 + n_ex + n_out]
        acc = refs[-1]
        k = pl.program_id(2)

        @pl.when(k == 0)
        def _():
            acc[...] = jnp.zeros_like(acc)

        at = a_ref[...]
        if a_pro is not None:
            at = a_pro(at)
        acc[...] += _dot(at, b_ref[...], dims)

        @pl.when(k == nk - 1)
        def _():
            res = acc[...]
            vals = epi(res, *[e[...] for e in ex]) if epi is not None else (res,)
            for o, v in zip(outs, vals):
                o[...] = v.astype(o.dtype)

    if ta:
        a_spec = pl.BlockSpec((tk, tm), lambda i, j, k: (k, i))
    else:
        a_spec = pl.BlockSpec((tm, tk), lambda i, j, k: (i, k))
    if tb:
        b_spec = pl.BlockSpec((tn, tk), lambda i, j, k: (j, k))
    else:
        b_spec = pl.BlockSpec((tk, tn), lambda i, j, k: (k, j))
    ex_specs, ex_arrays = [], []
    for arr, kind in extras:
        if kind == 'ij':
            ex_specs.append(pl.BlockSpec((tm, tn), lambda i, j, k: (i, j)))
            ex_arrays.append(arr)
        else:
            ex_specs.append(pl.BlockSpec((1, tn), lambda i, j, k: (0, j)))
            ex_arrays.append(arr.reshape(1, N))
    outs = pl.pallas_call(
        body, name=name, grid=(M // tm, N // tn, nk),
        in_specs=[a_spec, b_spec] + ex_specs,
        out_specs=[pl.BlockSpec((tm, tn), lambda i, j, k: (i, j)) for _ in out_dtypes],
        out_shape=[jax.ShapeDtypeStruct((M, N), dt) for dt in out_dtypes],
        scratch_shapes=[pltpu.VMEM((tm, tn), F32)],
        compiler_params=_cp(("parallel", "parallel", "arbitrary")),
    )(a, b, *ex_arrays)
    return outs[0] if n_out == 1 else tuple(outs)


def _rms_fwd(x, g, out_dtype, name):
    S, D = x.shape
    tr = _tile(S, ROW_TILE, 8)

    def body(x_ref, g_ref, o_ref):
        xv = x_ref[...]
        r = lax.rsqrt(jnp.mean(xv * xv, axis=-1, keepdims=True) + EPS)
        o_ref[...] = (xv * r * g_ref[...]).astype(o_ref.dtype)

    return pl.pallas_call(
        body, name=name, grid=(S // tr,),
        in_specs=[pl.BlockSpec((tr, D), lambda i: (i, 0)), pl.BlockSpec((1, D), lambda i: (0, 0))],
        out_specs=pl.BlockSpec((tr, D), lambda i: (i, 0)),
        out_shape=jax.ShapeDtypeStruct((S, D), out_dtype),
        compiler_params=_cp(("parallel",)),
    )(x, g.reshape(1, D))


def _rms_bwd(x, g, dy, res, name):
    S, D = x.shape
    tr = _tile(S, ROW_TILE, 8)
    has_res = res is not None

    def body(*refs):
        if has_res:
            x_ref, g_ref, dy_ref, res_ref, dx_ref, dg_ref = refs
        else:
            x_ref, g_ref, dy_ref, dx_ref, dg_ref = refs
        i = pl.program_id(0)

        @pl.when(i == 0)
        def _():
            dg_ref[...] = jnp.zeros_like(dg_ref)

        xv, d = x_ref[...], dy_ref[...].astype(F32)
        r = lax.rsqrt(jnp.mean(xv * xv, axis=-1, keepdims=True) + EPS)
        xh = xv * r
        t = d * g_ref[...]
        dx = r * (t - xh * jnp.mean(t * xh, axis=-1, keepdims=True))
        if has_res:
            dx = dx + res_ref[...]
        dx_ref[...] = dx
        dg_ref[...] += jnp.sum(d * xh, axis=0, keepdims=True)

    row = pl.BlockSpec((tr, D), lambda i: (i, 0))
    vec = pl.BlockSpec((1, D), lambda i: (0, 0))
    ins = [x, g.reshape(1, D), dy] + ([res] if has_res else [])
    return pl.pallas_call(
        body, name=name, grid=(S // tr,),
        in_specs=[row, vec, row] + ([row] if has_res else []),
        out_specs=[row, vec],
        out_shape=[jax.ShapeDtypeStruct((S, D), F32), jax.ShapeDtypeStruct((1, D), F32)],
        compiler_params=_cp(("arbitrary",)),
    )(*ins)


def _loss_head(h, g, target):
    S, D = h.shape
    tr = _tile(S, ROW_TILE, 8)

    def body(h_ref, g_ref, t_ref, loss_ref, dh_ref, dg_ref):
        i = pl.program_id(0)

        @pl.when(i == 0)
        def _():
            loss_ref[...] = jnp.zeros_like(loss_ref)
            dg_ref[...] = jnp.zeros_like(dg_ref)

        xv = h_ref[...]
        gv = g_ref[...]
        r = lax.rsqrt(jnp.mean(xv * xv, axis=-1, keepdims=True) + EPS)
        xh = xv * r
        err = xh * gv - t_ref[...]
        part = 0.5 * jnp.sum(jnp.mean(err * err, axis=-1, keepdims=True), axis=0, keepdims=True)
        loss_ref[...] += jnp.broadcast_to(part, loss_ref.shape)
        d = err * (1.0 / D)
        t = d * gv
        dh_ref[...] = r * (t - xh * jnp.mean(t * xh, axis=-1, keepdims=True))
        dg_ref[...] += jnp.sum(d * xh, axis=0, keepdims=True)

    row = pl.BlockSpec((tr, D), lambda i: (i, 0))
    vec = pl.BlockSpec((1, D), lambda i: (0, 0))
    return pl.pallas_call(
        body, name="loss_head", grid=(S // tr,),
        in_specs=[row, vec, row],
        out_specs=[pl.BlockSpec((8, 128), lambda i: (0, 0)), row, vec],
        out_shape=[jax.ShapeDtypeStruct((8, 128), F32), jax.ShapeDtypeStruct((S, D), F32),
                   jax.ShapeDtypeStruct((1, D), F32)],
        compiler_params=_cp(("arbitrary",)),
    )(h, g.reshape(1, D), target)


def _adamw(w, g, m, v, name):
    R, C = w.shape
    tr = _tile(R, max(8, (1 << 19) // max(C, 1) // 8 * 8), 8)
    c1 = 1.0 / (1.0 - ADAM_B1 ** ADAM_STEP)
    c2 = 1.0 / (1.0 - ADAM_B2 ** ADAM_STEP)

    def body(w_ref, g_ref, m_ref, v_ref, d_ref, nm_ref, nv_ref):
        gv = g_ref[...]
        nm = ADAM_B1 * m_ref[...] + (1.0 - ADAM_B1) * gv
        nv = ADAM_B2 * v_ref[...] + (1.0 - ADAM_B2) * (gv * gv)
        d_ref[...] = -ADAM_LR * ((nm * c1) / (jnp.sqrt(nv * c2) + ADAM_EPS) + ADAM_WD * w_ref[...])
        nm_ref[...] = nm
        nv_ref[...] = nv

    blk = pl.BlockSpec((tr, C), lambda i: (i, 0))
    return pl.pallas_call(
        body, name=name, grid=(R // tr,),
        in_specs=[blk] * 4, out_specs=[blk] * 3,
        out_shape=[jax.ShapeDtypeStruct((R, C), F32)] * 3,
        compiler_params=_cp(("parallel",)),
    )(w, g, m, v)


def _place():
    x, y, c = lax.axis_index("x"), lax.axis_index("y"), lax.axis_index("c")
    chips = [(1 - x, y), (x, 1 - y), (1 - x, 1 - y)]
    return x, y, c, chips


def _all_gather_chips(xs, name):
    r, n = xs.shape
    half = r // 2

    def body(x_ref, out_ref, send_sems, recv_sems):
        x, y, c, chips = _place()
        me = 2 * x + y
        sibling = (x, y, 1 - c)

        def piece(chip, hc):
            return out_ref.at[chip, pl.ds(hc * half, half), :]

        def copy(k, src, dst, to):
            return pltpu.make_async_remote_copy(src_ref=src, dst_ref=dst, send_sem=send_sems.at[k],
                                                recv_sem=recv_sems.at[k], device_id=to, device_id_type=MESH)

        src = x_ref.at[pl.ds(c * half, half), :]
        first = [copy(j, src, piece(me, c), (cx, cy, c)) for j, (cx, cy) in enumerate(chips)]
        for cp in first:
            cp.start()
        passed = []
        for j, (cx, cy) in enumerate(chips):
            got = piece(2 * cx + cy, c)
            copy(j, got, got, (cx, cy, c)).wait_recv()
            fwd = copy(3 + j, got, got, sibling)
            fwd.start()
            passed.append(fwd)
        for j, (cx, cy) in enumerate(chips):
            got = piece(2 * cx + cy, 1 - c)
            copy(3 + j, got, got, sibling).wait_recv()
        for cp in first + passed:
            cp.wait_send()

    return pl.pallas_call(
        body, name=name,
        in_specs=[pl.BlockSpec(memory_space=pl.ANY)],
        out_specs=pl.BlockSpec(memory_space=pl.ANY),
        out_shape=jax.ShapeDtypeStruct((4, r, n), xs.dtype),
        scratch_shapes=[pltpu.SemaphoreType.DMA((6,)), pltpu.SemaphoreType.DMA((6,))],
    )(xs)


def _gather(xs, me, name):
    return lax.dynamic_update_slice(_all_gather_chips(xs, name), xs[None], (me, 0, 0))


def _rs_swap_halves(g, name):
    _, r, n = g.shape
    half = r // 2

    def body(g_ref, out_ref, send_sem, recv_sem):
        x, y, c, _ = _place()
        cp = pltpu.make_async_remote_copy(
            src_ref=g_ref.at[:, pl.ds((1 - c) * half, half), :], dst_ref=out_ref,
            send_sem=send_sem, recv_sem=recv_sem, device_id=(x, y, 1 - c), device_id_type=MESH)
        cp.start()
        cp.wait()

    return pl.pallas_call(
        body, name=name,
        in_specs=[pl.BlockSpec(memory_space=pl.ANY)],
        out_specs=pl.BlockSpec(memory_space=pl.ANY),
        out_shape=jax.ShapeDtypeStruct((4, half, n), g.dtype),
        scratch_shapes=[pltpu.SemaphoreType.DMA, pltpu.SemaphoreType.DMA],
    )(g)


def _rs_add_halves(g, got, c_idx, name):
    _, r, n = g.shape
    half = r // 2
    tr = _tile(half, max(16, (1 << 19) // n // 16 * 16), 16)
    nb = half // tr

    def body(c_ref, g_ref, o_ref, out_ref, out16_ref):
        sm = g_ref[...] + o_ref[...]
        out_ref[...] = sm
        out16_ref[...] = sm.astype(out16_ref.dtype)

    blk = pl.BlockSpec((None, tr, n), lambda s, i, c: (s, i, 0))
    return pl.pallas_call(
        body, name=name,
        grid_spec=pltpu.PrefetchScalarGridSpec(
            num_scalar_prefetch=1, grid=(4, nb),
            in_specs=[pl.BlockSpec((None, tr, n), lambda s, i, c: (s, c[0] * nb + i, 0)), blk],
            out_specs=[blk, blk]),
        out_shape=[jax.ShapeDtypeStruct((4, half, n), F32), jax.ShapeDtypeStruct((4, half, n), RS_PAYLOAD)],
        compiler_params=_cp(("parallel", "parallel")),
    )(c_idx, g, got)


def _rs_exchange_chips(p, name):
    _, h, n = p.shape

    def body(p_ref, out_ref, send_sems, recv_sems):
        x, y, c, chips = _place()
        copies = [pltpu.make_async_remote_copy(
            src_ref=p_ref.at[2 * cx + cy], dst_ref=out_ref.at[j], send_sem=send_sems.at[j],
            recv_sem=recv_sems.at[j], device_id=(cx, cy, c), device_id_type=MESH)
            for j, (cx, cy) in enumerate(chips)]
        for cp in copies:
            cp.start()
        for cp in copies:
            cp.wait()

    return pl.pallas_call(
        body, name=name,
        in_specs=[pl.BlockSpec(memory_space=pl.ANY)],
        out_specs=pl.BlockSpec(memory_space=pl.ANY),
        out_shape=jax.ShapeDtypeStruct((3, h, n), p.dtype),
        scratch_shapes=[pltpu.SemaphoreType.DMA((3,)), pltpu.SemaphoreType.DMA((3,))],
    )(p)


def _rs_add_chips(p, got, me_idx, name):
    _, h, n = p.shape
    tr = _tile(h, max(16, (1 << 19) // n // 16 * 16), 16)

    def body(me_ref, p_ref, a_ref, b_ref, c_ref, out_ref):
        out_ref[...] = ((p_ref[...] + a_ref[...].astype(F32)) + b_ref[...].astype(F32)) + c_ref[...].astype(F32)

    def got_spec(j):
        return pl.BlockSpec((None, tr, n), lambda i, me: (j, i, 0))

    return pl.pallas_call(
        body, name=name,
        grid_spec=pltpu.PrefetchScalarGridSpec(
            num_scalar_prefetch=1, grid=(h // tr,),
            in_specs=[pl.BlockSpec((None, tr, n), lambda i, me: (me[0], i, 0)),
                      got_spec(0), got_spec(1), got_spec(2)],
            out_specs=pl.BlockSpec((tr, n), lambda i, me: (i, 0))),
        out_shape=jax.ShapeDtypeStruct((h, n), F32),
        compiler_params=_cp(("parallel",)),
    )(me_idx, p, got, got, got)


def _rs_share_halves(q, name):
    h, n = q.shape

    def body(q_ref, out_ref, send_sem, recv_sem):
        x, y, c, _ = _place()
        cp = pltpu.make_async_remote_copy(
            src_ref=q_ref, dst_ref=out_ref.at[pl.ds(c * h, h), :], send_sem=send_sem, recv_sem=recv_sem,
            device_id=(x, y, 1 - c), device_id_type=MESH)
        cp.start()
        pltpu.make_async_remote_copy(
            src_ref=q_ref, dst_ref=out_ref.at[pl.ds((1 - c) * h, h), :], send_sem=send_sem, recv_sem=recv_sem,
            device_id=(x, y, 1 - c), device_id_type=MESH).wait_recv()
        cp.wait_send()

    return pl.pallas_call(
        body, name=name,
        in_specs=[pl.BlockSpec(memory_space=pl.ANY)],
        out_specs=pl.BlockSpec(memory_space=pl.ANY),
        out_shape=jax.ShapeDtypeStruct((2 * h, n), q.dtype),
        scratch_shapes=[pltpu.SemaphoreType.DMA, pltpu.SemaphoreType.DMA],
    )(q)


def _reduce_scatter(g, place):
    c, c_idx, me_idx = place['c'], place['c_idx'], place['me_idx']
    got = _rs_swap_halves(g, "rs_swap_halves")
    p, p16 = _rs_add_halves(g, got, c_idx, "rs_add_halves")
    got = _rs_exchange_chips(p16, "rs_exchange_chips")
    q = _rs_add_chips(p, got, me_idx, "rs_add_chips")
    out = _rs_share_halves(q, "rs_share_halves")
    return lax.dynamic_update_slice(out, q, (c * q.shape[0], 0))


def _all_reduce_small(v, name):
    R, n = v.shape

    def body(v_ref, out_ref, buf, send_sems, recv_sems):
        x, y, c, _ = _place()
        me = 4 * x + 2 * y + c
        buf[me] = v_ref[...]
        copies = []
        for d in range(1, 8):
            dx, dy, dc = (d >> 2) & 1, (d >> 1) & 1, d & 1
            px = x if dx == 0 else 1 - x
            py = y if dy == 0 else 1 - y
            pc = c if dc == 0 else 1 - c
            copies.append(pltpu.make_async_remote_copy(
                src_ref=v_ref, dst_ref=buf.at[me], send_sem=send_sems.at[d - 1], recv_sem=recv_sems.at[d - 1],
                device_id=(px, py, pc), device_id_type=MESH))
        for cp in copies:
            cp.start()
        for d in range(1, 8):
            dx, dy, dc = (d >> 2) & 1, (d >> 1) & 1, d & 1
            px = x if dx == 0 else 1 - x
            py = y if dy == 0 else 1 - y
            pc = c if dc == 0 else 1 - c
            pltpu.make_async_remote_copy(
                src_ref=v_ref, dst_ref=buf.at[4 * px + 2 * py + pc], send_sem=send_sems.at[d - 1],
                recv_sem=recv_sems.at[d - 1], device_id=(px, py, pc), device_id_type=MESH).wait_recv()
        for cp in copies:
            cp.wait_send()
        acc = buf[0]
        for k in range(1, 8):
            acc = acc + buf[k]
        out_ref[...] = acc

    return pl.pallas_call(
        body, name=name,
        in_specs=[pl.BlockSpec(memory_space=pltpu.VMEM)],
        out_specs=pl.BlockSpec(memory_space=pltpu.VMEM),
        out_shape=jax.ShapeDtypeStruct((R, n), F32),
        scratch_shapes=[pltpu.VMEM((8, R, n), F32), pltpu.SemaphoreType.DMA((7,)), pltpu.SemaphoreType.DMA((7,))],
        compiler_params=pltpu.CompilerParams(vmem_limit_bytes=VMEM_LIMIT_BYTES),
    )(v)


def _pack_small(parts, rows):
    flat = jnp.concatenate([a.reshape(-1) for a in parts])
    return jnp.pad(flat, (0, rows * LANES - flat.shape[0])).reshape(rows, LANES)


def _mem_fwd(proj, q_blk, mkv, heads):
    S = proj.shape[0]
    ML = mkv.shape[0]
    t = _tile(S, MEM_TILE, 8)
    scale = HEAD ** -0.5

    def body(q_ref, k_ref, v_ref, o_ref):
        s = _dot(q_ref[...], k_ref[...], NT) * scale
        m = jnp.max(s, axis=-1, keepdims=True)
        e = jnp.exp(s - m)
        p = e / jnp.sum(e, axis=-1, keepdims=True)
        o_ref[...] = _dot(p, v_ref[...], NN)

    return pl.pallas_call(
        body, name="mem_fwd", grid=(S // t, heads),
        in_specs=[pl.BlockSpec((t, HEAD), lambda i, h: (i, q_blk + h)),
                  pl.BlockSpec((ML, HEAD), lambda i, h: (0, h)),
                  pl.BlockSpec((ML, HEAD), lambda i, h: (0, heads + h))],
        out_specs=pl.BlockSpec((t, HEAD), lambda i, h: (i, h)),
        out_shape=jax.ShapeDtypeStruct((S, heads * HEAD), F32),
        compiler_params=_cp(("parallel", "parallel")),
    )(proj, mkv, mkv)


def _mem_bwd(proj, q_blk, mkv, dcat, d_blk, heads):
    S = proj.shape[0]
    ML = mkv.shape[0]
    t = _tile(S, MEM_TILE, 8)
    scale = HEAD ** -0.5

    def body(q_ref, k_ref, v_ref, do_ref, dq_ref, dk_ref, dv_ref):
        i = pl.program_id(1)

        @pl.when(i == 0)
        def _():
            dk_ref[...] = jnp.zeros_like(dk_ref)
            dv_ref[...] = jnp.zeros_like(dv_ref)

        q, k, v, do = q_ref[...], k_ref[...], v_ref[...], do_ref[...]
        s = _dot(q, k, NT) * scale
        m = jnp.max(s, axis=-1, keepdims=True)
        e = jnp.exp(s - m)
        p = e / jnp.sum(e, axis=-1, keepdims=True)
        dp = _dot(do, v, NT)
        ds = p * (dp - jnp.sum(p * dp, axis=-1, keepdims=True))
        dq_ref[...] = _dot(ds, k, NN) * scale
        dk_ref[...] += _dot(ds, q, TN) * scale
        dv_ref[...] += _dot(p, do, TN)

    dq, dk, dv = pl.pallas_call(
        body, name="mem_bwd", grid=(heads, S // t),
        in_specs=[pl.BlockSpec((t, HEAD), lambda h, i: (i, q_blk + h)),
                  pl.BlockSpec((ML, HEAD), lambda h, i: (0, h)),
                  pl.BlockSpec((ML, HEAD), lambda h, i: (0, heads + h)),
                  pl.BlockSpec((t, HEAD), lambda h, i: (i, d_blk + h))],
        out_specs=[pl.BlockSpec((t, HEAD), lambda h, i: (i, h)),
                   pl.BlockSpec((ML, HEAD), lambda h, i: (0, h)),
                   pl.BlockSpec((ML, HEAD), lambda h, i: (0, h))],
        out_shape=[jax.ShapeDtypeStruct((S, heads * HEAD), F32),
                   jax.ShapeDtypeStruct((ML, heads * HEAD), F32),
                   jax.ShapeDtypeStruct((ML, heads * HEAD), F32)],
        compiler_params=_cp(("parallel", "arbitrary")),
    )(proj, mkv, mkv, dcat)
    return dq, jnp.concatenate([dk, dv], axis=1)


def _fox_gates(gl, bf):
    S = gl.shape[0]

    def body(g_ref, b_ref, o_ref):
        xv = g_ref[...] + b_ref[...]
        c = jnp.minimum(xv, 0.0) - jnp.log(1.0 + jnp.exp(-jnp.abs(xv)))
        row = lax.broadcasted_iota(jnp.int32, c.shape, 0)
        d = 1
        while d < S:
            c = c + jnp.where(row >= d, pltpu.roll(c, d, 0), 0.0)
            d *= 2
        o_ref[...] = c

    return pl.pallas_call(
        body, name="fox_gates", out_shape=jax.ShapeDtypeStruct((S, 128), F32),
        in_specs=[pl.BlockSpec(memory_space=pltpu.VMEM)] * 2,
        out_specs=pl.BlockSpec(memory_space=pltpu.VMEM),
        compiler_params=_cp(),
    )(gl, bf)


def _fox_gates_bwd(gl, bf, dcf):
    S = gl.shape[0]

    def body(g_ref, b_ref, d_ref, dg_ref, db_ref):
        c = d_ref[...]
        row = lax.broadcasted_iota(jnp.int32, c.shape, 0)
        d = 1
        while d < S:
            c = c + jnp.where(row < S - d, pltpu.roll(c, S - d, 0), 0.0)
            d *= 2
        dx = c * _sigmoid(-(g_ref[...] + b_ref[...]))
        dg_ref[...] = dx
        db_ref[...] = jnp.sum(dx, axis=0, keepdims=True)

    return pl.pallas_call(
        body, name="fox_gates_bwd",
        out_shape=[jax.ShapeDtypeStruct((S, 128), F32), jax.ShapeDtypeStruct((1, 128), F32)],
        in_specs=[pl.BlockSpec(memory_space=pltpu.VMEM)] * 3,
        out_specs=[pl.BlockSpec(memory_space=pltpu.VMEM)] * 2,
        compiler_params=_cp(),
    )(gl, bf, dcf)


def _fox_scores(q, k, cq, ck, t, masked):
    s = _dot(q, k, NT) * (HEAD ** -0.5) + cq - ck
    if masked:
        row = lax.broadcasted_iota(jnp.int32, (t, t), 0)
        col = lax.broadcasted_iota(jnp.int32, (t, t), 1)
        s = jnp.where(row >= col, s, -jnp.inf)
    return s


def _fox_pairs(nq, by_key):
    if by_key:
        pairs = [(i, j) for j in range(nq) for i in range(j, nq)]
    else:
        pairs = [(i, j) for i in range(nq) for j in range(i + 1)]
    return (jnp.asarray(np.array([a for a, _ in pairs], np.int32)),
            jnp.asarray(np.array([b for _, b in pairs], np.int32)))


def _fox_fwd(proj, cfq, cfk, H):
    S = proj.shape[0]
    t = _tile(S, FOX_TILE)
    nq = S // t
    qt, kt = _fox_pairs(nq, False)

    def body(qt_ref, kt_ref, q_ref, k_ref, v_ref, cq_ref, ck_ref, o_ref, lse_ref, m_s, l_s, acc_s):
        n = pl.program_id(1)
        qi, ki = qt_ref[n], kt_ref[n]

        @pl.when(ki == 0)
        def _():
            m_s[...] = jnp.full_like(m_s, -jnp.inf)
            l_s[...] = jnp.zeros_like(l_s)
            acc_s[...] = jnp.zeros_like(acc_s)

        def step(masked):
            s = _fox_scores(q_ref[...], k_ref[...], cq_ref[...], ck_ref[...], t, masked)
            m_new = jnp.maximum(m_s[...], jnp.max(s, axis=-1, keepdims=True))
            alpha = jnp.exp(m_s[...] - m_new)
            p = jnp.exp(s - m_new)
            l_s[...] = alpha * l_s[...] + jnp.sum(p, axis=-1, keepdims=True)
            acc_s[...] = alpha * acc_s[...] + _dot(p, v_ref[...], NN)
            m_s[...] = m_new

        @pl.when(ki != qi)
        def _():
            step(False)

        @pl.when(ki == qi)
        def _():
            step(True)
            o_ref[...] = acc_s[...] / l_s[...]
            lse_ref[...] = m_s[...] + jnp.log(l_s[...])

    return pl.pallas_call(
        body, name="fox_fwd",
        grid_spec=pltpu.PrefetchScalarGridSpec(
            num_scalar_prefetch=2, grid=(H, qt.shape[0]),
            in_specs=[pl.BlockSpec((t, HEAD), lambda h, n, qt, kt: (qt[n], h)),
                      pl.BlockSpec((t, HEAD), lambda h, n, qt, kt: (kt[n], H + h)),
                      pl.BlockSpec((t, HEAD), lambda h, n, qt, kt: (kt[n], 2 * H + h)),
                      pl.BlockSpec((None, t, 1), lambda h, n, qt, kt: (h, qt[n], 0)),
                      pl.BlockSpec((None, 1, t), lambda h, n, qt, kt: (h, 0, kt[n]))],
            out_specs=[pl.BlockSpec((t, HEAD), lambda h, n, qt, kt: (qt[n], h)),
                       pl.BlockSpec((None, t, 1), lambda h, n, qt, kt: (h, qt[n], 0))],
            scratch_shapes=[pltpu.VMEM((t, 1), F32), pltpu.VMEM((t, 1), F32), pltpu.VMEM((t, HEAD), F32)]),
        out_shape=[jax.ShapeDtypeStruct((S, H * HEAD), F32), jax.ShapeDtypeStruct((H, S, 1), F32)],
        compiler_params=_cp(("parallel", "arbitrary")),
    )(qt, kt, proj, proj, proj, cfq, cfk)


def _fox_bwd_rowdot(proj, cfq, cfk, lse, dcat, H):
    S = proj.shape[0]
    t = _tile(S, FOX_TILE)
    nq = S // t
    qt, kt = _fox_pairs(nq, False)

    def body(qt_ref, kt_ref, q_ref, k_ref, v_ref, do_ref, lse_ref, cq_ref, ck_ref, d_ref):
        n = pl.program_id(1)
        qi, ki = qt_ref[n], kt_ref[n]

        @pl.when(ki == 0)
        def _():
            d_ref[...] = jnp.zeros_like(d_ref)

        def step(masked):
            s = _fox_scores(q_ref[...], k_ref[...], cq_ref[...], ck_ref[...], t, masked)
            p = jnp.exp(s - lse_ref[...])
            dp = _dot(do_ref[...], v_ref[...], NT)
            d_ref[...] += jnp.sum(p * dp, axis=-1, keepdims=True)

        @pl.when(ki != qi)
        def _():
            step(False)

        @pl.when(ki == qi)
        def _():
            step(True)

    return pl.pallas_call(
        body, name="fox_bwd_rowdot",
        grid_spec=pltpu.PrefetchScalarGridSpec(
            num_scalar_prefetch=2, grid=(H, qt.shape[0]),
            in_specs=[pl.BlockSpec((t, HEAD), lambda h, n, qt, kt: (qt[n], h)),
                      pl.BlockSpec((t, HEAD), lambda h, n, qt, kt: (kt[n], H + h)),
                      pl.BlockSpec((t, HEAD), lambda h, n, qt, kt: (kt[n], 2 * H + h)),
                      pl.BlockSpec((t, HEAD), lambda h, n, qt, kt: (qt[n], h)),
                      pl.BlockSpec((None, t, 1), lambda h, n, qt, kt: (h, qt[n], 0)),
                      pl.BlockSpec((None, t, 1), lambda h, n, qt, kt: (h, qt[n], 0)),
                      pl.BlockSpec((None, 1, t), lambda h, n, qt, kt: (h, 0, kt[n]))],
            out_specs=pl.BlockSpec((None, t, 1), lambda h, n, qt, kt: (h, qt[n], 0))),
        out_shape=jax.ShapeDtypeStruct((H, S, 1), F32),
        compiler_params=_cp(("parallel", "arbitrary")),
    )(qt, kt, proj, proj, proj, dcat, lse, cfq, cfk)


def _fox_bwd(proj, cfq, cfk, rowdot, lse, dcat, H):
    S = proj.shape[0]
    t = _tile(S, FOX_TILE)
    nq = S // t
    scale = HEAD ** -0.5
    qt, kt = _fox_pairs(nq, True)

    def body(qt_ref, kt_ref, q_ref, k_ref, v_ref, dd_ref, do_ref, lse_ref, cq_ref, ck_ref,
             dq_ref, dk_ref, dv_ref, dck_ref):
        n = pl.program_id(1)
        i, j = qt_ref[n], kt_ref[n]

        @pl.when(n == 0)
        def _():
            dq_ref[...] = jnp.zeros_like(dq_ref)

        @pl.when(i == j)
        def _():
            dk_ref[...] = jnp.zeros_like(dk_ref)
            dv_ref[...] = jnp.zeros_like(dv_ref)
            dck_ref[...] = jnp.zeros_like(dck_ref)

        def step(masked):
            q, k, v, do = q_ref[...], k_ref[...], v_ref[...], do_ref[...]
            p = jnp.exp(_fox_scores(q, k, cq_ref[...], ck_ref[...], t, masked) - lse_ref[...])
            dv_ref[...] += _dot(p, do, TN)
            dp = _dot(do, v, NT)
            ds = p * (dp - dd_ref[...])
            dk_ref[...] += _dot(ds, q, TN) * scale
            rows = pl.ds(pl.multiple_of(i * t, t), t)
            dq_ref[rows, :] += _dot(ds, k, NN) * scale
            dck_ref[...] -= jnp.sum(ds, axis=0, keepdims=True)

        @pl.when(i != j)
        def _():
            step(False)

        @pl.when(i == j)
        def _():
            step(True)

    qtile = pl.BlockSpec((t, HEAD), lambda h, n, qt, kt: (qt[n], h))
    qcol = pl.BlockSpec((None, t, 1), lambda h, n, qt, kt: (h, qt[n], 0))
    return pl.pallas_call(
        body, name="fox_bwd",
        grid_spec=pltpu.PrefetchScalarGridSpec(
            num_scalar_prefetch=2, grid=(H, qt.shape[0]),
            in_specs=[qtile,
                      pl.BlockSpec((t, HEAD), lambda h, n, qt, kt: (kt[n], H + h)),
                      pl.BlockSpec((t, HEAD), lambda h, n, qt, kt: (kt[n], 2 * H + h)),
                      qcol, qtile, qcol, qcol,
                      pl.BlockSpec((None, 1, t), lambda h, n, qt, kt: (h, 0, kt[n]))],
            out_specs=[pl.BlockSpec((S, HEAD), lambda h, n, qt, kt: (0, h)),
                       pl.BlockSpec((t, HEAD), lambda h, n, qt, kt: (kt[n], h)),
                       pl.BlockSpec((t, HEAD), lambda h, n, qt, kt: (kt[n], h)),
                       pl.BlockSpec((None, 1, t), lambda h, n, qt, kt: (h, 0, kt[n]))]),
        out_shape=[jax.ShapeDtypeStruct((S, H * HEAD), F32)] * 3 + [jax.ShapeDtypeStruct((H, 1, S), F32)],
        compiler_params=_cp(("parallel", "arbitrary")),
    )(qt, kt, proj, proj, proj, rowdot, dcat, lse, cfq, cfk)


def _s5_prep(lam_re, lam_im, log_dt, b_re, b_im, c_re, c_im):
    G, P = lam_re.shape
    ns = G // 16
    dt = jnp.exp(log_dt)[:, None]
    mag = jnp.exp(lam_re * dt)
    a_re, a_im = mag * jnp.cos(lam_im * dt), mag * jnp.sin(lam_im * dt)
    den = lam_re * lam_re + lam_im * lam_im
    z_re = ((a_re - 1.0) * lam_re + a_im * lam_im) / den
    z_im = (a_im * lam_re - (a_re - 1.0) * lam_im) / den
    bb_re = z_re[..., None] * b_re - z_im[..., None] * b_im
    bb_im = z_re[..., None] * b_im + z_im[..., None] * b_re
    eye = jnp.eye(16, dtype=F32)
    bb = jnp.stack([bb_re, bb_im]).reshape(2, ns, 16, P, S5_GROUP)
    wb = jnp.einsum('asgpc,gh->sgcahp', bb, eye).reshape(ns, S5_SLAB, 2 * 16 * P)
    cc = jnp.stack([c_re, -c_im]).reshape(2, ns, 16, S5_GROUP, P)
    wc = jnp.einsum('asgcp,gh->sagphc', cc, eye).reshape(ns, 2 * 16 * P, S5_SLAB)
    a = jnp.concatenate([a_re.reshape(ns, 1, 16 * P), a_im.reshape(ns, 1, 16 * P)], axis=-1)
    return wb, wc, a


def _s5_tables(lam_re, lam_im, log_dt):
    G, P = lam_re.shape
    ns = G // 16
    dt = jnp.exp(log_dt)[:, None]
    tt = jnp.arange(1, S5_CHUNK + 1, dtype=F32)[:, None, None]
    mag = jnp.exp(lam_re * dt * tt)
    ang = lam_im * dt * tt
    pr = (mag * jnp.cos(ang)).reshape(S5_CHUNK, ns, 16 * P).transpose(1, 0, 2)
    pi = (mag * jnp.sin(ang)).reshape(S5_CHUNK, ns, 16 * P).transpose(1, 0, 2)
    return pr, pi, pr[:, ::-1], pi[:, ::-1]


def _s5_scan_fwd(proj, wb, wc, pr, pi, dskip):
    S = proj.shape[0]
    ns = wb.shape[0]
    W = wb.shape[2]
    hw = W // 2
    T = S5_CHUNK
    nc = S // T
    mix = ns * S5_SLAB

    def body(u_ref, wb_ref, wc_ref, pr_ref, pi_ref, d_ref, v_ref, yg_ref, h_ref, cin_ref, carry):
        c = pl.program_id(1)

        @pl.when(c == 0)
        def _():
            carry[...] = jnp.zeros_like(carry)

        u = u_ref[...]
        bu = _dot(u, wb_ref[...], NN)
        xr, xi = bu[:, :hw], bu[:, hw:]
        sub = lax.broadcasted_iota(jnp.int32, (T, hw), 0) & (S5_ROWS - 1)
        d = 1
        while d < S5_ROWS:
            ar, ai = pr_ref[pl.ds(d - 1, 1), :], pi_ref[pl.ds(d - 1, 1), :]
            sr = jnp.where(sub >= d, pltpu.roll(xr, d, 0), 0.0)
            si = jnp.where(sub >= d, pltpu.roll(xi, d, 0), 0.0)
            xr, xi = xr + ar * sr - ai * si, xi + ar * si + ai * sr
            d *= 2
        cin_ref[...] = carry[...]
        cr, ci = carry[:, :hw], carry[:, hw:]
        pwr, pwi = pr_ref[pl.ds(0, S5_ROWS), :], pi_ref[pl.ds(0, S5_ROWS), :]
        for g in range(T // S5_ROWS):
            rows = slice(g * S5_ROWS, (g + 1) * S5_ROWS)
            hr = xr[rows, :] + pwr * cr - pwi * ci
            hi = xi[rows, :] + pwr * ci + pwi * cr
            h_ref[rows, :hw] = hr
            h_ref[rows, hw:] = hi
            cr, ci = hr[S5_ROWS - 1:S5_ROWS, :], hi[S5_ROWS - 1:S5_ROWS, :]
        carry[:, :hw] = cr
        carry[:, hw:] = ci
        y = _dot(h_ref[...], wc_ref[...], NN)
        v = y + d_ref[...] * u
        v_ref[...] = v
        yg_ref[...] = _gelu(v)

    return pl.pallas_call(
        body, name="s5_scan_fwd", grid=(ns, nc),
        in_specs=[pl.BlockSpec((T, S5_SLAB), lambda s, c: (c, s)),
                  pl.BlockSpec((None, S5_SLAB, W), lambda s, c: (s, 0, 0)),
                  pl.BlockSpec((None, W, S5_SLAB), lambda s, c: (s, 0, 0)),
                  pl.BlockSpec((None, T, hw), lambda s, c: (s, 0, 0)),
                  pl.BlockSpec((None, T, hw), lambda s, c: (s, 0, 0)),
                  pl.BlockSpec((1, S5_SLAB), lambda s, c: (0, s))],
        out_specs=[pl.BlockSpec((T, S5_SLAB), lambda s, c: (c, s)),
                   pl.BlockSpec((T, S5_SLAB), lambda s, c: (c, s)),
                   pl.BlockSpec((T, W), lambda s, c: (c, s)),
                   pl.BlockSpec((None, 1, W), lambda s, c: (c, 0, s))],
        out_shape=[jax.ShapeDtypeStruct((S, mix), F32), jax.ShapeDtypeStruct((S, mix), F32),
                   jax.ShapeDtypeStruct((S, ns * W), F32), jax.ShapeDtypeStruct((nc, 1, ns * W), F32)],
        scratch_shapes=[pltpu.VMEM((1, W), F32)],
        compiler_params=_cp(("parallel", "arbitrary")),
    )(proj, wb, wc, pr, pi, dskip)


def _s5_scan_bwd(dv, proj, hs, cin, wb, wc, pr, pi, prr, pir, dskip):
    S = proj.shape[0]
    ns = wb.shape[0]
    W = wb.shape[2]
    hw = W // 2
    T = S5_CHUNK
    nc = S // T
    mix = ns * S5_SLAB

    def body(dv_ref, u_ref, h_ref, cin_ref, wb_ref, wc_ref, pr_ref, pi_ref, prr_ref, pir_ref, d_ref,
             du_ref, dwb_ref, dwc_ref, da_ref, dd_ref, lam_s, carry):
        c = pl.program_id(1)

        @pl.when(c == 0)
        def _():
            carry[...] = jnp.zeros_like(carry)
            dwb_ref[...] = jnp.zeros_like(dwb_ref)
            dwc_ref[...] = jnp.zeros_like(dwc_ref)
            da_ref[...] = jnp.zeros_like(da_ref)
            dd_ref[...] = jnp.zeros_like(dd_ref)

        dy, u = dv_ref[...], u_ref[...]
        dh = _dot(dy, wc_ref[...], NT)
        gr, gi = dh[:, :hw], dh[:, hw:]
        row = lax.broadcasted_iota(jnp.int32, (T, hw), 0)
        sub = row & (S5_ROWS - 1)
        d = 1
        while d < S5_ROWS:
            ar, ai = pr_ref[pl.ds(d - 1, 1), :], -pi_ref[pl.ds(d - 1, 1), :]
            sr = jnp.where(sub < S5_ROWS - d, pltpu.roll(gr, T - d, 0), 0.0)
            si = jnp.where(sub < S5_ROWS - d, pltpu.roll(gi, T - d, 0), 0.0)
            gr, gi = gr + ar * sr - ai * si, gi + ar * si + ai * sr
            d *= 2
        lr, li = carry[:, :hw], carry[:, hw:]
        pwr, pwi = prr_ref[pl.ds(T - S5_ROWS, S5_ROWS), :], -pir_ref[pl.ds(T - S5_ROWS, S5_ROWS), :]
        for g in reversed(range(T // S5_ROWS)):
            rows = slice(g * S5_ROWS, (g + 1) * S5_ROWS)
            lgr = gr[rows, :] + pwr * lr - pwi * li
            lgi = gi[rows, :] + pwr * li + pwi * lr
            lam_s[rows, :hw] = lgr
            lam_s[rows, hw:] = lgi
            lr, li = lgr[0:1, :], lgi[0:1, :]
        carry[:, :hw] = lr
        carry[:, hw:] = li
        gr, gi = lam_s[:, :hw], lam_s[:, hw:]
        hr, hi = h_ref[:, :hw], h_ref[:, hw:]
        hpr = jnp.where(row >= 1, pltpu.roll(hr, 1, 0), cin_ref[:, :hw])
        hpi = jnp.where(row >= 1, pltpu.roll(hi, 1, 0), cin_ref[:, hw:])
        da_ref[:, :hw] += jnp.sum(hpr * gr + hpi * gi, axis=0, keepdims=True)
        da_ref[:, hw:] += jnp.sum(hpr * gi - hpi * gr, axis=0, keepdims=True)
        lam = lam_s[...]
        du_ref[...] = _dot(lam, wb_ref[...], NT) + dy * d_ref[...]
        dwb_ref[...] += _dot(u, lam, TN)
        dwc_ref[...] += _dot(h_ref[...], dy, TN)
        dd_ref[...] += jnp.sum(dy * u, axis=0, keepdims=True)

    def rc(c):
        return nc - 1 - c

    return pl.pallas_call(
        body, name="s5_scan_bwd", grid=(ns, nc),
        in_specs=[pl.BlockSpec((T, S5_SLAB), lambda s, c: (rc(c), s)),
                  pl.BlockSpec((T, S5_SLAB), lambda s, c: (rc(c), s)),
                  pl.BlockSpec((T, W), lambda s, c: (rc(c), s)),
                  pl.BlockSpec((None, 1, W), lambda s, c: (rc(c), 0, s)),
                  pl.BlockSpec((None, S5_SLAB, W), lambda s, c: (s, 0, 0)),
                  pl.BlockSpec((None, W, S5_SLAB), lambda s, c: (s, 0, 0)),
                  pl.BlockSpec((None, T, hw), lambda s, c: (s, 0, 0)),
                  pl.BlockSpec((None, T, hw), lambda s, c: (s, 0, 0)),
                  pl.BlockSpec((None, T, hw), lambda s, c: (s, 0, 0)),
                  pl.BlockSpec((None, T, hw), lambda s, c: (s, 0, 0)),
                  pl.BlockSpec((1, S5_SLAB), lambda s, c: (0, s))],
        out_specs=[pl.BlockSpec((T, S5_SLAB), lambda s, c: (rc(c), s)),
                   pl.BlockSpec((None, S5_SLAB, W), lambda s, c: (s, 0, 0)),
                   pl.BlockSpec((None, W, S5_SLAB), lambda s, c: (s, 0, 0)),
                   pl.BlockSpec((None, 1, W), lambda s, c: (s, 0, 0)),
                   pl.BlockSpec((1, S5_SLAB), lambda s, c: (0, s))],
        out_shape=[jax.ShapeDtypeStruct((S, mix), F32), jax.ShapeDtypeStruct(wb.shape, F32),
                   jax.ShapeDtypeStruct(wc.shape, F32), jax.ShapeDtypeStruct((ns, 1, W), F32),
                   jax.ShapeDtypeStruct((1, mix), F32)],
        scratch_shapes=[pltpu.VMEM((T, W), F32), pltpu.VMEM((1, W), F32)],
        compiler_params=_cp(("parallel", "arbitrary")),
    )(dv, proj, hs, cin, wb, wc, pr, pi, prr, pir, dskip)


def _s5_glu_bwd(dcat, yg, z):
    S, mix = yg.shape
    tr = _tile(S, ROW_TILE, 8)

    def body(do_ref, yg_ref, z_ref, dz_ref, dy_ref, db_ref):
        i = pl.program_id(0)

        @pl.when(i == 0)
        def _():
            db_ref[...] = jnp.zeros_like(db_ref)

        do, yg_, sz = do_ref[...], yg_ref[...], _sigmoid(z_ref[...])
        dz = do * yg_ * sz * (1.0 - sz)
        dz_ref[...] = dz
        dy_ref[...] = do * sz
        db_ref[...] += jnp.sum(dz, axis=0, keepdims=True)

    blk = pl.BlockSpec((tr, mix), lambda i: (i, 0))
    return pl.pallas_call(
        body, name="s5_glu_bwd", grid=(S // tr,),
        in_specs=[blk, blk, blk], out_specs=[blk, blk, pl.BlockSpec((1, mix), lambda i: (0, 0))],
        out_shape=[jax.ShapeDtypeStruct((S, mix), F32), jax.ShapeDtypeStruct((S, mix), F32),
                   jax.ShapeDtypeStruct((1, mix), F32)],
        compiler_params=_cp(("arbitrary",)),
    )(dcat, yg, z)


def _rows_down(x, j):
    return x if j == 0 else pltpu.roll(x, j, 0)


def _conv_rows(xe, w_ref, n):
    c = None
    for j in range(GDN_CONV):
        term = w_ref[pl.ds(GDN_CONV - 1 - j, 1), :] * _rows_down(xe, j)[8:8 + n, :]
        c = term if c is None else c + term
    return c


def _gdn_prep(proj, blk0, nblk, convw, norm, scale, name):
    S = proj.shape[0]
    tr = _tile(S, CONV_TILE, 8)
    nb8 = tr // 8

    def body(x_ref, xb_ref, w_ref, o_ref):
        i = pl.program_id(1)
        xe = jnp.concatenate([jnp.where(i == 0, 0.0, xb_ref[...]), x_ref[...]], axis=0)
        c = _conv_rows(xe, w_ref, tr)
        s = c * _sigmoid(c)
        if norm:
            s = s * lax.rsqrt(jnp.sum(s * s, axis=-1, keepdims=True) + EPS) * scale
        o_ref[...] = s

    return pl.pallas_call(
        body, name=name, grid=(nblk, S // tr),
        in_specs=[pl.BlockSpec((tr, HEAD), lambda j, i: (i, blk0 + j)),
                  pl.BlockSpec((8, HEAD), lambda j, i: (jnp.maximum(i * nb8 - 1, 0), blk0 + j)),
                  pl.BlockSpec((GDN_CONV, HEAD), lambda j, i: (0, j))],
        out_specs=pl.BlockSpec((tr, HEAD), lambda j, i: (i, j)),
        out_shape=jax.ShapeDtypeStruct((S, nblk * HEAD), F32),
        compiler_params=_cp(("parallel", "parallel")),
    )(proj, proj, convw)


def _gdn_prep_bwd(proj, blk0, nblk, convw, dout, norm, scale, name):
    S = proj.shape[0]
    tr = _tile(S, CONV_TILE, 8)
    nb8 = tr // 8
    last8 = S // 8 - 1
    nrow = S // tr

    def body(x_ref, xb_ref, xa_ref, w_ref, d_ref, da_ref, dx_ref, dw_ref):
        i = pl.program_id(1)

        @pl.when(i == 0)
        def _():
            dw_ref[...] = jnp.zeros_like(dw_ref)

        xe = jnp.concatenate([jnp.where(i == 0, 0.0, xb_ref[...]), x_ref[...], xa_ref[...]], axis=0)
        de = jnp.concatenate([d_ref[...], da_ref[...]], axis=0)
        n = tr + 8
        c = _conv_rows(xe, w_ref, n)
        sg = _sigmoid(c)
        s = c * sg
        if norm:
            r = lax.rsqrt(jnp.sum(s * s, axis=-1, keepdims=True) + EPS)
            ds = scale * r * (de - s * (r * r) * jnp.sum(de * s, axis=-1, keepdims=True))
        else:
            ds = de
        dc = ds * (sg + c * sg * (1.0 - sg))
        rowi = lax.broadcasted_iota(jnp.int32, (n, HEAD), 0)
        dc = jnp.where((i == nrow - 1) & (rowi >= tr), 0.0, dc)
        dct = dc[:tr, :]
        dx = None
        for j in range(GDN_CONV):
            tap = pl.ds(GDN_CONV - 1 - j, 1)
            up = dct if j == 0 else pltpu.roll(dc, n - j, 0)[:tr, :]
            term = w_ref[tap, :] * up
            dx = term if dx is None else dx + term
            dw_ref[tap, :] += jnp.sum(dct * _rows_down(xe, j)[8:8 + tr, :], axis=0, keepdims=True)
        dx_ref[...] = dx

    return pl.pallas_call(
        body, name=name, grid=(nblk, nrow),
        in_specs=[pl.BlockSpec((tr, HEAD), lambda j, i: (i, blk0 + j)),
                  pl.BlockSpec((8, HEAD), lambda j, i: (jnp.maximum(i * nb8 - 1, 0), blk0 + j)),
                  pl.BlockSpec((8, HEAD), lambda j, i: (jnp.minimum((i + 1) * nb8, last8), blk0 + j)),
                  pl.BlockSpec((GDN_CONV, HEAD), lambda j, i: (0, j)),
                  pl.BlockSpec((tr, HEAD), lambda j, i: (i, j)),
                  pl.BlockSpec((8, HEAD), lambda j, i: (jnp.minimum((i + 1) * nb8, last8), j))],
        out_specs=[pl.BlockSpec((tr, HEAD), lambda j, i: (i, j)),
                   pl.BlockSpec((GDN_CONV, HEAD), lambda j, i: (0, j))],
        out_shape=[jax.ShapeDtypeStruct((S, nblk * HEAD), F32),
                   jax.ShapeDtypeStruct((GDN_CONV, nblk * HEAD), F32)],
        compiler_params=_cp(("parallel", "arbitrary")),
    )(proj, proj, proj, convw, dout, dout)


def _gdn_gates(pg, alog, dtb):
    S = pg.shape[0]

    def body(a_ref, b_ref, al_ref, dt_ref, gc_ref, be_ref):
        g = -jnp.exp(al_ref[...]) * _softplus(a_ref[...] + dt_ref[...])
        rowm = lax.broadcasted_iota(jnp.int32, g.shape, 0) & (GDN_CHUNK - 1)
        c = g
        d = 1
        while d < GDN_CHUNK:
            c = c + jnp.where(rowm >= d, pltpu.roll(c, d, 0), 0.0)
            d *= 2
        gc_ref[...] = c
        be_ref[...] = _sigmoid(b_ref[...])

    blk = pl.BlockSpec((S, 128), lambda i: (0, 0))
    vec = pl.BlockSpec((1, 128), lambda i: (0, 0))
    return pl.pallas_call(
        body, name="gdn_gates", grid=(1,),
        in_specs=[blk, pl.BlockSpec((S, 128), lambda i: (0, 1)), vec, vec],
        out_specs=[blk, blk],
        out_shape=[jax.ShapeDtypeStruct((S, 128), F32)] * 2,
        compiler_params=_cp(("arbitrary",)),
    )(pg, pg, alog, dtb)


def _gdn_gates_bwd(pg, alog, dtb, dgc, dbeta):
    S = pg.shape[0]

    def body(a_ref, b_ref, al_ref, dt_ref, dgc_ref, dbe_ref, dpa_ref, dpb_ref, dal_ref, ddt_ref):
        rowm = lax.broadcasted_iota(jnp.int32, (S, 128), 0) & (GDN_CHUNK - 1)
        c = dgc_ref[...]
        d = 1
        while d < GDN_CHUNK:
            c = c + jnp.where(rowm < GDN_CHUNK - d, pltpu.roll(c, S - d, 0), 0.0)
            d *= 2
        xv = a_ref[...] + dt_ref[...]
        ea = jnp.exp(al_ref[...])
        g = -ea * _softplus(xv)
        dx = c * (-ea) * _sigmoid(xv)
        dpa_ref[...] = dx
        dal_ref[...] = jnp.sum(c * g, axis=0, keepdims=True)
        ddt_ref[...] = jnp.sum(dx, axis=0, keepdims=True)
        be = _sigmoid(b_ref[...])
        dpb_ref[...] = dbe_ref[...] * be * (1.0 - be)

    blk = pl.BlockSpec((S, 128), lambda i: (0, 0))
    blk1 = pl.BlockSpec((S, 128), lambda i: (0, 1))
    vec = pl.BlockSpec((1, 128), lambda i: (0, 0))
    dpa, dpb, dal, ddt = pl.pallas_call(
        body, name="gdn_gates_bwd", grid=(1,),
        in_specs=[blk, blk1, vec, vec, blk, blk],
        out_specs=[blk, blk, vec, vec],
        out_shape=[jax.ShapeDtypeStruct((S, 128), F32)] * 2 + [jax.ShapeDtypeStruct((1, 128), F32)] * 2,
        compiler_params=_cp(("arbitrary",)),
    )(pg, pg, alog, dtb, dgc, dbeta)
    return jnp.concatenate([dpa, dpb], axis=1), dal, ddt


def _gdn_pre(qs, ks, vs, gcs, grs, betas):
    C = GDN_CHUNK
    n = len(qs)
    r = lax.broadcasted_iota(jnp.int32, (C, C), 0)
    c_ = lax.broadcasted_iota(jnp.int32, (C, C), 1)
    lower, strict = r >= c_, r > c_
    eye = jnp.where(r == c_, 1.0, 0.0)
    decs = [jnp.exp(jnp.where(lower, gcs[i] - grs[i], -jnp.inf)) for i in range(n)]
    kbs = [ks[i] * betas[i] for i in range(n)]
    vbs = [vs[i] * betas[i] for i in range(n)]
    lmats = [jnp.where(strict, _dot(kbs[i], ks[i], NT) * decs[i], 0.0) for i in range(n)]
    amats = [jnp.where(lower, _dot(qs[i], ks[i], NT) * decs[i], 0.0) for i in range(n)]
    pks = [-lm for lm in lmats]
    tinvs = [eye + pk for pk in pks]
    for _ in range(5):
        pks = [_dotf(pk, pk, NN) for pk in pks]
        tinvs = [tv + _dotf(tv, pk, NN) for tv, pk in zip(tinvs, pks)]
    es = [jnp.exp(gc) for gc in gcs]
    glasts = [gc[C - 1:C, :] for gc in gcs]
    fs = [jnp.exp(gl - gc) for gl, gc in zip(glasts, gcs)]
    gls = [jnp.exp(gl) for gl in glasts]
    us = [_dotf(tinvs[i], vbs[i], NN) for i in range(n)]
    ws = [_dotf(tinvs[i], kbs[i] * es[i], NN) for i in range(n)]
    return [dict(lower=lower, strict=strict, dec=decs[i], kb=kbs[i], vb=vbs[i], lmat=lmats[i], tinv=tinvs[i],
                 e=es[i], f=fs[i], gl=gls[i], u=us[i], w=ws[i], amat=amats[i], qd=qs[i] * es[i],
                 kd=ks[i] * fs[i]) for i in range(n)]


def _gdn_heads_per_step(H):
    return max(d for d in (1, 2, 3, 4) if H % d == 0)


def _gdn_chunk_fwd(q, k, v, gcol, grow, bcol):
    S = q.shape[0]
    H, NC = gcol.shape[0], gcol.shape[1]
    C = GDN_CHUNK
    hb = _gdn_heads_per_step(H)

    def body(q_ref, k_ref, v_ref, gc_ref, gr_ref, b_ref, o_ref, st_ref, state):
        n = pl.program_id(1)

        @pl.when(n == 0)
        def _():
            state[...] = jnp.zeros_like(state)

        cols = [slice(i * HEAD, (i + 1) * HEAD) for i in range(hb)]
        ps = _gdn_pre([q_ref[:, c] for c in cols], [k_ref[:, c] for c in cols], [v_ref[:, c] for c in cols],
                      [gc_ref[i] for i in range(hb)], [gr_ref[i] for i in range(hb)],
                      [b_ref[i] for i in range(hb)])
        s0s = [state[i] for i in range(hb)]
        vns = [ps[i]['u'] - _dot(ps[i]['w'], s0s[i], NN) for i in range(hb)]
        outs = [_dot(ps[i]['qd'], s0s[i], NN) + _dot(ps[i]['amat'], vns[i], NN) for i in range(hb)]
        news = [s0s[i] * ps[i]['gl'] + _dot(ps[i]['kd'], vns[i], TN) for i in range(hb)]
        for i in range(hb):
            st_ref[i] = s0s[i]
            o_ref[:, cols[i]] = outs[i]
            state[i] = news[i]

    tok = pl.BlockSpec((C, hb * HEAD), lambda h, n: (n, h))
    col = pl.BlockSpec((hb, None, C, 1), lambda h, n: (h, n, 0, 0))
    rowb = pl.BlockSpec((hb, None, 1, C), lambda h, n: (h, n, 0, 0))
    return pl.pallas_call(
        body, name="gdn_chunk_fwd", grid=(H // hb, NC),
        in_specs=[tok, tok, tok, col, rowb, col],
        out_specs=[tok, pl.BlockSpec((hb, None, HEAD, HEAD), lambda h, n: (h, n, 0, 0))],
        out_shape=[jax.ShapeDtypeStruct((S, H * HEAD), F32), jax.ShapeDtypeStruct((H, NC, HEAD, HEAD), F32)],
        scratch_shapes=[pltpu.VMEM((hb, HEAD, HEAD), F32)],
        compiler_params=_cp(("parallel", "arbitrary")),
    )(q, k, v, gcol, grow, bcol)


def _gdn_chunk_bwd(q, k, v, gcol, grow, bcol, st, do):
    S = q.shape[0]
    H, NC = gcol.shape[0], gcol.shape[1]
    C = GDN_CHUNK
    hb = _gdn_heads_per_step(H)

    def body(q_ref, k_ref, v_ref, gc_ref, gr_ref, b_ref, st_ref, do_ref,
             dq_ref, dk_ref, dv_ref, dgc_ref, dbe_ref, dstate):
        n = pl.program_id(1)

        @pl.when(n == 0)
        def _():
            dstate[...] = jnp.zeros_like(dstate)

        R = range(hb)
        cols = [slice(i * HEAD, (i + 1) * HEAD) for i in R]
        qs, ks, vs = [q_ref[:, c] for c in cols], [k_ref[:, c] for c in cols], [v_ref[:, c] for c in cols]
        betas = [b_ref[i] for i in R]
        ps = _gdn_pre(qs, ks, vs, [gc_ref[i] for i in R], [gr_ref[i] for i in R], betas)
        lower, strict = ps[0]['lower'], ps[0]['strict']
        s0s, dos, ds1s = [st_ref[i] for i in R], [do_ref[:, c] for c in cols], [dstate[i] for i in R]
        vns = [ps[i]['u'] - _dot(ps[i]['w'], s0s[i], NN) for i in R]
        dvns = [_dot(ps[i]['amat'], dos[i], TN) + _dot(ps[i]['kd'], ds1s[i], NN) for i in R]
        damats = [jnp.where(lower, _dot(dos[i], vns[i], NT), 0.0) for i in R]
        dqds = [_dot(dos[i], s0s[i], NT) for i in R]
        dkds = [_dot(vns[i], ds1s[i], NT) for i in R]
        dgls = [jnp.sum(s0s[i] * ds1s[i], keepdims=True) for i in R]
        ds0s = [ps[i]['gl'] * ds1s[i] + _dot(ps[i]['qd'], dos[i], TN) - _dot(ps[i]['w'], dvns[i], TN) for i in R]
        dws = [-_dot(dvns[i], s0s[i], NT) for i in R]
        dvbs = [_dotf(ps[i]['tinv'], dvns[i], TN) for i in R]
        dkgs = [_dotf(ps[i]['tinv'], dws[i], TN) for i in R]
        dls = [-jnp.where(strict, _dotf(dvbs[i], ps[i]['u'], NT) + _dotf(dkgs[i], ps[i]['w'], NT), 0.0) for i in R]
        dkks = [dls[i] * ps[i]['dec'] for i in R]
        dqks = [damats[i] * ps[i]['dec'] for i in R]
        ms = [dls[i] * ps[i]['lmat'] + damats[i] * ps[i]['amat'] for i in R]
        dkbs = [_dot(dkks[i], ks[i], NN) + dkgs[i] * ps[i]['e'] for i in R]
        dks = [_dot(dkks[i], ps[i]['kb'], TN) + _dot(dqks[i], qs[i], TN) + dkds[i] * ps[i]['f'] + dkbs[i] * betas[i]
               for i in R]
        dqs = [_dot(dqks[i], ks[i], NN) + dqds[i] * ps[i]['e'] for i in R]
        ones = jnp.ones((C, HEAD), F32)
        colsums = [_dotf(ms[i], ones, TN)[:, 0:1] for i in R]
        rowi = lax.broadcasted_iota(jnp.int32, (C, 1), 0)
        for i in R:
            p = ps[i]
            de = (jnp.sum(dkgs[i] * p['kb'], axis=-1, keepdims=True)
                  + jnp.sum(dqds[i] * qs[i], axis=-1, keepdims=True))
            df = jnp.sum(dkds[i] * ks[i], axis=-1, keepdims=True)
            dgc = jnp.sum(ms[i], axis=-1, keepdims=True) - colsums[i] + de * p['e'] - df * p['f']
            dlast = jnp.sum(df * p['f'], keepdims=True) + dgls[i] * p['gl']
            dgc_ref[i] = dgc + jnp.where(rowi == C - 1, dlast, 0.0)
            dbe_ref[i] = (jnp.sum(dkbs[i] * ks[i], axis=-1, keepdims=True)
                          + jnp.sum(dvbs[i] * vs[i], axis=-1, keepdims=True))
            dstate[i] = ds0s[i]
            dq_ref[:, cols[i]] = dqs[i]
            dk_ref[:, cols[i]] = dks[i]
            dv_ref[:, cols[i]] = dvbs[i] * betas[i]

    def rn(n):
        return NC - 1 - n

    tok = pl.BlockSpec((C, hb * HEAD), lambda h, n: (rn(n), h))
    col = pl.BlockSpec((hb, None, C, 1), lambda h, n: (h, rn(n), 0, 0))
    rowb = pl.BlockSpec((hb, None, 1, C), lambda h, n: (h, rn(n), 0, 0))
    return pl.pallas_call(
        body, name="gdn_chunk_bwd", grid=(H // hb, NC),
        in_specs=[tok, tok, tok, col, rowb, col,
                  pl.BlockSpec((hb, None, HEAD, HEAD), lambda h, n: (h, rn(n), 0, 0)), tok],
        out_specs=[tok, tok, tok, col, col],
        out_shape=[jax.ShapeDtypeStruct((S, H * HEAD), F32)] * 3
        + [jax.ShapeDtypeStruct((H, NC, C, 1), F32)] * 2,
        scratch_shapes=[pltpu.VMEM((hb, HEAD, HEAD), F32)],
        compiler_params=_cp(("parallel", "arbitrary")),
    )(q, k, v, gcol, grow, bcol, st, do)


def _gdn_onorm(o, proj, gate_blk, w, H):
    S = o.shape[0]
    tr = _tile(S, CONV_TILE, 8)

    def body(o_ref, g_ref, w_ref, out_ref):
        ov, gv = o_ref[...], g_ref[...]
        r = lax.rsqrt(jnp.mean(ov * ov, axis=-1, keepdims=True) + EPS)
        out_ref[...] = (ov * r * w_ref[...]) * (gv * _sigmoid(gv))

    return pl.pallas_call(
        body, name="gdn_onorm", grid=(S // tr, H),
        in_specs=[pl.BlockSpec((tr, HEAD), lambda i, h: (i, h)),
                  pl.BlockSpec((tr, HEAD), lambda i, h: (i, gate_blk + h)),
                  pl.BlockSpec((1, HEAD), lambda i, h: (0, 0))],
        out_specs=pl.BlockSpec((tr, HEAD), lambda i, h: (i, h)),
        out_shape=jax.ShapeDtypeStruct((S, H * HEAD), F32),
        compiler_params=_cp(("parallel", "parallel")),
    )(o, proj, w)


def _gdn_onorm_bwd(dcat, o, proj, gate_blk, w, H):
    S = o.shape[0]
    tr = _tile(S, CONV_TILE, 8)

    def body(d_ref, o_ref, g_ref, w_ref, do_ref, dg_ref, dw_ref):
        i, h = pl.program_id(0), pl.program_id(1)

        @pl.when((i == 0) & (h == 0))
        def _():
            dw_ref[...] = jnp.zeros_like(dw_ref)

        dm, ov, gv, wv = d_ref[...], o_ref[...], g_ref[...], w_ref[...]
        r = lax.rsqrt(jnp.mean(ov * ov, axis=-1, keepdims=True) + EPS)
        oh = ov * r
        sg = gv * _sigmoid(gv)
        dy = dm * sg
        t = dy * wv
        do_ref[...] = r * (t - oh * jnp.mean(t * oh, axis=-1, keepdims=True))
        dg_ref[...] = dm * (oh * wv) * _silu_grad(gv)
        dw_ref[...] += jnp.sum(dy * oh, axis=0, keepdims=True)

    tok = pl.BlockSpec((tr, HEAD), lambda i, h: (i, h))
    vec = pl.BlockSpec((1, HEAD), lambda i, h: (0, 0))
    return pl.pallas_call(
        body, name="gdn_onorm_bwd", grid=(S // tr, H),
        in_specs=[tok, tok, pl.BlockSpec((tr, HEAD), lambda i, h: (i, gate_blk + h)), vec],
        out_specs=[tok, tok, vec],
        out_shape=[jax.ShapeDtypeStruct((S, H * HEAD), F32)] * 2 + [jax.ShapeDtypeStruct((1, HEAD), F32)],
        compiler_params=_cp(("arbitrary", "arbitrary")),
    )(dcat, o, proj, w)


def _lanes_to_heads(a, H):
    return a[:, :H].T


def _heads_to_lanes(a):
    H = a.shape[0]
    return jnp.pad(a.T, ((0, 0), (0, 128 - H)))


def _take_cols(segs, a, b):
    out, off = [], 0
    for sg in segs:
        w = sg.shape[-1]
        lo, hi = max(a, off), min(b, off + w)
        if lo < hi:
            out.append(sg[..., lo - off:hi - off])
        off += w
    return out


def _pad_cols(pieces):
    m = jnp.concatenate(pieces, axis=-1)
    return jnp.pad(m, ((0, 0), (0, 128 - m.shape[-1])))


def _pad_lanes(v):
    return jnp.pad(v.reshape(1, -1), ((0, 0), (0, 128 - v.shape[-1])))


def _s5_layer_fwd(a, w, cfg):
    proj = _mm(a, w['w_in'], name="s5_in")
    wb, wc, _ = w['prep']
    pr, pi, prr, pir = w['tables']
    v, yg, hs, cin = _s5_scan_fwd(proj, wb, wc, pr, pi, w['d_skip'])
    z, mix = _mm(yg, w['w_glu'], name="s5_glu", extras=[(yg, 'ij'), (w['b_glu'], 'j')],
                 epi=lambda acc, y, b: (acc + b, y * _sigmoid(acc + b)), out_dtypes=(F32, F32))
    return proj, mix, dict(v=v, yg=yg, hs=hs, cin=cin, z=z)


def _s5_layer_bwd(a, w, proj, sv, dcat, dmemq, cfg):
    wb, wc, _ = w['prep']
    pr, pi, prr, pir = w['tables']
    dz, dyg1, db_glu = _s5_glu_bwd(dcat, sv['yg'], sv['z'])
    dw_glu = _mm(sv['yg'], dz, name="s5_dwglu", ta=True)
    dv = _mm(dz, w['w_glu'], name="s5_dyg", tb=True, extras=[(dyg1, 'ij'), (sv['v'], 'ij')],
             epi=lambda acc, d1, vv: ((acc + d1) * _gelu_grad(vv),))
    du, dwb, dwc, da, dd = _s5_scan_bwd(dv, proj, sv['hs'], sv['cin'], wb, wc, pr, pi, prr, pir, w['d_skip'])
    dproj = jnp.concatenate([du, dmemq], axis=1)
    dw_in = _mm(a, dproj, name="s5_dwin", ta=True)
    da_in = _mm(dproj, w['w_in'], name="s5_da", tb=True)
    dlre, dlim, dldt, dbre, dbim, dcre, dcim = w['prep_vjp']((dwb, dwc, da))
    grads = dict(w_in=dw_in, w_glu=dw_glu, b_glu=db_glu[0], d_skip=dd[0], lam_re=dlre, lam_im=dlim,
                 log_dt=dldt, b_re=dbre, b_im=dbim, c_re=dcre, c_im=dcim)
    return da_in, grads


def _gdn_relayout(a, H, NC):
    t = _lanes_to_heads(a, H).reshape(H, NC, GDN_CHUNK)
    return t[..., None], t[:, :, None, :]


def _gdn_layer_fwd(a, w, cfg):
    H, MIX, S = cfg['H'], cfg['MIX'], a.shape[0]
    NC = S // GDN_CHUNK
    proj = _mm(a, w['w_main'], name="gdn_in")
    pg = _mm(a, w['w_gate'], name="gdn_in_gates")
    cw = w['conv_w']
    q = _gdn_prep(proj, 0, H, cw[:, :MIX], True, HEAD ** -0.5, "gdn_prep_q")
    k = _gdn_prep(proj, H, H, cw[:, MIX:2 * MIX], True, 1.0, "gdn_prep_k")
    v = _gdn_prep(proj, 2 * H, H, cw[:, 2 * MIX:], False, 1.0, "gdn_prep_v")
    gc, beta = _gdn_gates(pg, w['a_log'], w['dt_bias'])
    gcol, grow = _gdn_relayout(gc, H, NC)
    bcol, _ = _gdn_relayout(beta, H, NC)
    o, st = _gdn_chunk_fwd(q, k, v, gcol, grow, bcol)
    mix = _gdn_onorm(o, proj, 3 * H, w['o_norm'], H)
    return proj, mix, dict(pg=pg, q=q, k=k, v=v, gcol=gcol, grow=grow, bcol=bcol, o=o, st=st)


def _gdn_layer_bwd(a, w, proj, sv, dcat, dmemq, cfg):
    H, MIX, S = cfg['H'], cfg['MIX'], a.shape[0]
    cw = w['conv_w']
    do, dgate, donorm = _gdn_onorm_bwd(dcat, sv['o'], proj, 3 * H, w['o_norm'], H)
    dq, dk, dv, dgcol, dbcol = _gdn_chunk_bwd(sv['q'], sv['k'], sv['v'], sv['gcol'], sv['grow'], sv['bcol'],
                                              sv['st'], do)
    dgc = _heads_to_lanes(dgcol.reshape(H, S))
    dbeta = _heads_to_lanes(dbcol.reshape(H, S))
    dpg, dalog, ddtb = _gdn_gates_bwd(sv['pg'], w['a_log'], w['dt_bias'], dgc, dbeta)
    dxq, dwq = _gdn_prep_bwd(proj, 0, H, cw[:, :MIX], dq, True, HEAD ** -0.5, "gdn_prep_bwd_q")
    dxk, dwk = _gdn_prep_bwd(proj, H, H, cw[:, MIX:2 * MIX], dk, True, 1.0, "gdn_prep_bwd_k")
    dxv, dwv = _gdn_prep_bwd(proj, 2 * H, H, cw[:, 2 * MIX:], dv, False, 1.0, "gdn_prep_bwd_v")
    dproj = jnp.concatenate([dxq, dxk, dxv, dgate, dmemq], axis=1)
    dw_main = _mm(a, dproj, name="gdn_dwmain", ta=True)
    dw_gate = _mm(a, dpg, name="gdn_dwgate", ta=True)
    da1 = _mm(dpg, w['w_gate'], name="gdn_da_gates", tb=True)
    da_in = _mm(dproj, w['w_main'], name="gdn_da", tb=True, extras=[(da1, 'ij')], epi=lambda acc, e: (acc + e,))
    grads = dict(w_main=dw_main, w_gate=dw_gate, conv_w=jnp.concatenate([dwq, dwk, dwv], axis=1),
                 a_log=dalog[0, :H], dt_bias=ddtb[0, :H], o_norm=donorm[0])
    return da_in, grads


def _fox_layer_fwd(a, w, cfg):
    H = cfg['H']
    proj = _mm(a, w['w_main'], name="fox_in")
    pg = _mm(a, w['w_gate'], name="fox_in_gates")
    cf = _fox_gates(pg, w['b_f'])
    cfh = _lanes_to_heads(cf, H)
    cfq, cfk = cfh[:, :, None], cfh[:, None, :]
    o, lse = _fox_fwd(proj, cfq, cfk, H)
    return proj, o, dict(pg=pg, cfq=cfq, cfk=cfk, lse=lse)


def _fox_layer_bwd(a, w, proj, sv, dcat, dmemq, cfg):
    H = cfg['H']
    rowdot = _fox_bwd_rowdot(proj, sv['cfq'], sv['cfk'], sv['lse'], dcat, H)
    dq, dk, dv, dck = _fox_bwd(proj, sv['cfq'], sv['cfk'], rowdot, sv['lse'], dcat, H)
    dpg, dbf = _fox_gates_bwd(sv['pg'], w['b_f'], _heads_to_lanes(dck[:, 0, :]))
    dproj = jnp.concatenate([dq, dk, dv, dmemq], axis=1)
    dw_main = _mm(a, dproj, name="fox_dwmain", ta=True)
    dw_gate = _mm(a, dpg, name="fox_dwgate", ta=True)
    da1 = _mm(dpg, w['w_gate'], name="fox_da_gates", tb=True)
    da_in = _mm(dproj, w['w_main'], name="fox_da", tb=True, extras=[(da1, 'ij')], epi=lambda acc, e: (acc + e,))
    grads = dict(w_main=dw_main, w_gate=dw_gate, b_f=dbf[0, :H])
    return da_in, grads


_LAYER_FWD = (_s5_layer_fwd, _gdn_layer_fwd, _fox_layer_fwd)
_LAYER_BWD = (_s5_layer_bwd, _gdn_layer_bwd, _fox_layer_bwd)


def _mixer_weights(kind, j, fw, p, cfg):
    H, MIX, MW = cfg['H'], cfg['MIX'], cfg['MW']
    if kind == 0:
        params = tuple(p[n][j] for n in ('s5_lam_re', 's5_lam_im', 's5_log_dt', 's5_b_re', 's5_b_im',
                                         's5_c_re', 's5_c_im'))
        prep, prep_vjp = jax.vjp(_s5_prep, *params)
        prep = (prep[0].astype(MXU_DTYPE), prep[1].astype(MXU_DTYPE), prep[2])
        tables = _s5_tables(*params[:3])
        return dict(w_in=fw['s5_w_in'][j], w_glu=fw['s5_w_glu'][j], b_glu=fw['s5_b_glu'][j],
                    d_skip=fw['s5_d_skip'][j].reshape(1, MIX), prep=prep, prep_vjp=prep_vjp, tables=tables)
    if kind == 1:
        segs = fw['gdn_w_in'][j]
        c0 = 4 * MIX
        total = c0 + 2 * H + MW
        w_main = jnp.concatenate(_take_cols(segs, 0, c0) + _take_cols(segs, c0 + 2 * H, total), axis=1)
        w_gate = jnp.concatenate([_pad_cols(_take_cols(segs, c0, c0 + H)),
                                  _pad_cols(_take_cols(segs, c0 + H, c0 + 2 * H))], axis=1)
        return dict(w_main=w_main, w_gate=w_gate, conv_w=fw['gdn_conv_w'][j],
                    a_log=_pad_lanes(p['gdn_a_log'][j]), dt_bias=_pad_lanes(p['gdn_dt_bias'][j]),
                    o_norm=p['gdn_o_norm'][j].reshape(1, HEAD))
    segs = fw['fox_w_in'][j]
    c0 = 3 * MIX
    total = c0 + H + MW
    w_main = jnp.concatenate(_take_cols(segs, 0, c0) + _take_cols(segs, c0 + H, total), axis=1)
    w_gate = _pad_cols(_take_cols(segs, c0, c0 + H))
    return dict(w_main=w_main, w_gate=w_gate, b_f=_pad_lanes(p['fox_b_f'][j]))


def _local_step(p, fw, cfg):
    H, MIX, MW, MH, depth = cfg['H'], cfg['MIX'], cfg['MW'], cfg['MH'], cfg['depth']
    x, mem, target = p['x'], p['mem'], p['loss_target']
    q_blk = {0: MIX // HEAD, 1: 4 * MIX // HEAD, 2: 3 * MIX // HEAD}

    mem_n = _rms_fwd(mem, p['mem_norm'], MXU_DTYPE, "mem_rms")
    mkv = _mm(mem_n, fw['w_mem_kv'], name="mem_kv")

    h = x
    saved = []
    for i in range(depth):
        kind, j = i % 3, i // 3
        w = _mixer_weights(kind, j, fw, p, cfg)
        a = _rms_fwd(h, p['norm1'][i], MXU_DTYPE, "rms1")
        proj, mix, sv = _LAYER_FWD[kind](a, w, cfg)
        read = _mem_fwd(proj, q_blk[kind], mkv, MH)
        cat = jnp.concatenate([mix, read], axis=1)
        h1 = _mm(cat, fw['w_out'][i], name="out_proj", extras=[(h, 'ij')], epi=lambda acc, r: (acc + r,))
        a2 = _rms_fwd(h1, p['norm2'][i], MXU_DTYPE, "rms2")
        u = _mm(a2, fw['w_up'][i], name="mlp_up")
        h2 = _mm(u, fw['w_down'][i], name="mlp_down", a_pro=_relu2, extras=[(h1, 'ij')],
                 epi=lambda acc, r: (acc + r,))
        saved.append(dict(w=w, h=h, a=a, proj=proj, sv=sv, cat=cat, h1=h1, a2=a2, u=u))
        h = h2

    loss, dh, dnorm_f = _loss_head(h, p['norm_f'], target)

    g = {n: None for n in WEIGHTS}
    g['norm_f'] = dnorm_f[0]
    per_layer = {n: [None] * depth for n in ('norm1', 'norm2', 'w_out', 'w_up', 'w_down')}
    mix_grads = {0: {}, 1: {}, 2: {}}
    dmkv = None
    for i in reversed(range(depth)):
        kind, j = i % 3, i // 3
        s = saved[i]
        w = s['w']
        du = _mm(dh, fw['w_down'][i], name="mlp_ddown", tb=True, extras=[(s['u'], 'ij')],
                 epi=lambda acc, uu: (acc * (2.0 * jnp.maximum(uu, 0.0)),))
        per_layer['w_down'][i] = _mm(s['u'], dh, name="mlp_dwdown", ta=True, a_pro=_relu2)
        per_layer['w_up'][i] = _mm(s['a2'], du, name="mlp_dwup", ta=True)
        da2 = _mm(du, fw['w_up'][i], name="mlp_dup", tb=True)
        dh1, dn2 = _rms_bwd(s['h1'], p['norm2'][i], da2, dh, "rms2_bwd")
        per_layer['norm2'][i] = dn2[0]
        dcat = _mm(dh1, fw['w_out'][i], name="out_dproj", tb=True)
        per_layer['w_out'][i] = _mm(s['cat'], dh1, name="out_dw", ta=True)
        dmemq, dmkv_i = _mem_bwd(s['proj'], q_blk[kind], mkv, dcat, MIX // HEAD, MH)
        dmkv = dmkv_i if dmkv is None else dmkv + dmkv_i
        da, mg = _LAYER_BWD[kind](s['a'], w, s['proj'], s['sv'], dcat, dmemq, cfg)
        mix_grads[kind][j] = mg
        dh, dn1 = _rms_bwd(s['h'], p['norm1'][i], da, dh1, "rms1_bwd")
        per_layer['norm1'][i] = dn1[0]
    for n in ('norm1', 'norm2'):
        g[n] = jnp.stack(per_layer[n])
    for n in ('w_out', 'w_up', 'w_down'):
        g[n] = per_layer[n]

    g['w_mem_kv'] = _mm(mem_n, dmkv, name="mem_dwkv", ta=True)
    dmem_n = _mm(dmkv, fw['w_mem_kv'], name="mem_dn", tb=True)
    _, dmn = _rms_bwd(mem, p['mem_norm'], dmem_n, None, "mem_rms_bwd")
    g['mem_norm'] = dmn[0]

    def layers(kind, key):
        return [mix_grads[kind][j][key] for j in sorted(mix_grads[kind])]

    g['s5_w_in'] = layers(0, 'w_in')
    g['s5_w_glu'] = layers(0, 'w_glu')
    for n in ('b_glu', 'd_skip', 'lam_re', 'lam_im', 'log_dt', 'b_re', 'b_im', 'c_re', 'c_im'):
        g['s5_' + n] = jnp.stack(layers(0, n))
    c0 = 4 * MIX
    g['gdn_w_in'] = [[wm[:, :c0], wg[:, :H], wg[:, 128:128 + H], wm[:, c0:]]
                     for wm, wg in zip(layers(1, 'w_main'), layers(1, 'w_gate'))]
    for n in ('conv_w', 'a_log', 'dt_bias', 'o_norm'):
        g['gdn_' + n] = jnp.stack(layers(1, n))
    c0 = 3 * MIX
    g['fox_w_in'] = [[wm[:, :c0], wg[:, :H], wm[:, c0:]]
                     for wm, wg in zip(layers(2, 'w_main'), layers(2, 'w_gate'))]
    g['fox_b_f'] = jnp.stack(layers(2, 'b_f'))
    return loss, dh, g


def _gather_all(p, me):
    fw = {}

    def rows(name):
        w = p[name]
        L, r, n = w.shape
        got = _gather(w.astype(MXU_DTYPE).reshape(L * r, n), me, "all_gather_" + name)
        return [got[:, i * r:(i + 1) * r, :].reshape(4 * r, n) for i in range(L)]

    def cols(name):
        w = p[name]
        L, r, n = w.shape
        got = _gather(w.astype(MXU_DTYPE).reshape(L * r, n), me, "all_gather_" + name)
        return [[got[s, i * r:(i + 1) * r, :] for s in range(4)] for i in range(L)]

    kv = p['w_mem_kv']
    fw['w_mem_kv'] = _gather(kv.astype(MXU_DTYPE), me, "all_gather_w_mem_kv").reshape(4 * kv.shape[0], kv.shape[1])
    for name in ('w_out', 'w_down', 's5_w_in', 's5_w_glu'):
        fw[name] = rows(name)
    fw['w_up'] = [jnp.concatenate(blocks, axis=1) for blocks in cols('w_up')]
    fw['gdn_w_in'] = cols('gdn_w_in')
    fw['fox_w_in'] = cols('fox_w_in')

    vec = _gather(_pack_small([p[n] for n in VECTOR_SHARDED], 16), me, "all_gather_vectors").reshape(4, -1)
    off = 0
    for n in VECTOR_SHARDED:
        sz = p[n].size
        stacked = vec[:, off:off + sz].reshape((4,) + p[n].shape)
        ax = SHARD_AXIS[n]
        t = jnp.moveaxis(stacked, 0, ax)
        shp = list(t.shape)
        fw[n] = t.reshape(shp[:ax] + [shp[ax] * shp[ax + 1]] + shp[ax + 2:])
        off += sz
    return fw


def _scatter_all(g, p, place):
    grads = {}

    def rows(name):
        L, r, n = p[name].shape
        blocks = jnp.stack([gl.reshape(4, r, n) for gl in g[name]], axis=1).reshape(4, L * r, n)
        return _reduce_scatter(blocks, place).reshape(L, r, n)

    def shards_of_cols(segs, n):
        return jnp.stack([jnp.concatenate(_take_cols(segs, s * n, (s + 1) * n), axis=1) for s in range(4)])

    kv = p['w_mem_kv']
    grads['w_mem_kv'] = _reduce_scatter(g['w_mem_kv'].reshape((4,) + kv.shape), place)
    for name in ('w_out', 's5_w_in', 's5_w_glu'):
        grads[name] = rows(name)
    n = p['w_down'].shape[2]
    grads['w_down'] = jnp.stack([_reduce_scatter(gl.reshape(4, -1, n), place) for gl in g['w_down']])
    n = p['w_up'].shape[2]
    grads['w_up'] = jnp.stack([_reduce_scatter(shards_of_cols([gl], n), place) for gl in g['w_up']])
    for name in ('gdn_w_in', 'fox_w_in'):
        n = p[name].shape[2]
        grads[name] = jnp.stack([_reduce_scatter(shards_of_cols(segs, n), place) for segs in g[name]])

    parts = []
    for name in VECTOR_SHARDED:
        ax = SHARD_AXIS[name]
        shp = list(g[name].shape)
        t = g[name].reshape(shp[:ax] + [4, shp[ax] // 4] + shp[ax + 1:])
        parts.append(jnp.moveaxis(t, ax, 0).reshape(4, -1))
    flat = jnp.concatenate(parts, axis=1)
    flat = jnp.pad(flat, ((0, 0), (0, 16 * LANES - flat.shape[1]))).reshape(4, 16, LANES)
    red = _reduce_scatter(flat, place).reshape(-1)
    off = 0
    for name in VECTOR_SHARDED:
        grads[name] = red[off:off + p[name].size].reshape(p[name].shape)
        off += p[name].size
    return grads


def kernel(x, mem, mem_norm, w_mem_kv, norm1, w_out, norm2, w_up, w_down, norm_f, s5_w_in, s5_lam_re, s5_lam_im, s5_log_dt, s5_b_re, s5_b_im, s5_c_re, s5_c_im, s5_d_skip, s5_w_glu, s5_b_glu, gdn_w_in, gdn_conv_w, gdn_a_log, gdn_dt_bias, gdn_o_norm, fox_w_in, fox_b_f, loss_target, m_mem_norm, m_w_mem_kv, m_norm1, m_w_out, m_norm2, m_w_up, m_w_down, m_norm_f, m_s5_w_in, m_s5_lam_re, m_s5_lam_im, m_s5_log_dt, m_s5_b_re, m_s5_b_im, m_s5_c_re, m_s5_c_im, m_s5_d_skip, m_s5_w_glu, m_s5_b_glu, m_gdn_w_in, m_gdn_conv_w, m_gdn_a_log, m_gdn_dt_bias, m_gdn_o_norm, m_fox_w_in, m_fox_b_f, v_mem_norm, v_w_mem_kv, v_norm1, v_w_out, v_norm2, v_w_up, v_w_down, v_norm_f, v_s5_w_in, v_s5_lam_re, v_s5_lam_im, v_s5_log_dt, v_s5_b_re, v_s5_b_im, v_s5_c_re, v_s5_c_im, v_s5_d_skip, v_s5_w_glu, v_s5_b_glu, v_gdn_w_in, v_gdn_conv_w, v_gdn_a_log, v_gdn_dt_bias, v_gdn_o_norm, v_fox_w_in, v_fox_b_f):
    args = locals()
    p = {n: args[n] for n in WEIGHTS}
    mom = {n: args['m_' + n] for n in WEIGHTS}
    var = {n: args['v_' + n] for n in WEIGHTS}
    S, D = x.shape[1], x.shape[2]
    MW = w_mem_kv.shape[1] // 2
    MIX = D - MW
    cfg = dict(H=MIX // HEAD, MIX=MIX, MW=MW, MH=MW // HEAD, depth=norm1.shape[0])
    p.update(x=x.reshape(S, D), mem=mem.reshape(mem.shape[1], D), loss_target=loss_target.reshape(S, D))
    c = lax.axis_index("c")
    me = 2 * lax.axis_index("x") + lax.axis_index("y")
    place = dict(c=c, c_idx=c.astype(jnp.int32).reshape(1), me_idx=me.astype(jnp.int32).reshape(1))

    fw = _gather_all(p, me)
    loss, dx, g = _local_step(p, fw, cfg)
    grads = _scatter_all(g, p, place)

    n_small = sum(p[n].size for n in REPLICATED)
    rows = -(-n_small // LANES // 8) * 8
    small = _all_reduce_small(_pack_small([g[n] for n in REPLICATED], rows), "all_reduce_small").reshape(-1)
    off = 0
    for n in REPLICATED:
        grads[n] = small[off:off + p[n].size].reshape(p[n].shape)
        off += p[n].size

    delta, new_m, new_v = {}, {}, {}
    for n in SHARD_AXIS:
        shp = p[n].shape
        two_d = (-1, shp[-1])
        d, nm, nv = _adamw(p[n].reshape(two_d), grads[n].reshape(two_d), mom[n].reshape(two_d),
                           var[n].reshape(two_d), "adamw_" + n)
        delta[n], new_m[n], new_v[n] = d.reshape(shp), nm.reshape(shp), nv.reshape(shp)
    d, nm, nv = _adamw(*[_pack_small([src[n] for n in REPLICATED], rows) for src in (p, grads, mom, var)],
                       "adamw_small")
    d, nm, nv = d.reshape(-1), nm.reshape(-1), nv.reshape(-1)
    off = 0
    for n in REPLICATED:
        sz, shp = p[n].size, p[n].shape
        delta[n], new_m[n], new_v[n] = (d[off:off + sz].reshape(shp), nm[off:off + sz].reshape(shp),
                                        nv[off:off + sz].reshape(shp))
        off += sz

    total = lax.psum(loss[0, 0], ("x", "y", "c"))
    return (total, dx.reshape(x.shape), *[grads[n] for n in WEIGHTS], *[delta[n] for n in WEIGHTS],
            *[new_m[n] for n in WEIGHTS], *[new_v[n] for n in WEIGHTS])
```

```python
import math

import jax
import jax.numpy as jnp
import numpy as np
from jax import lax
from jax.experimental import pallas as pl
from jax.experimental.pallas import tpu as pltpu

F32 = jnp.float32
MXU_DTYPE = jnp.bfloat16
EPS = 1e-6
HEAD = 128
S5_GROUP = 16
S5_STATE = 64
S5_SLAB = 256
S5_CHUNK = 128
S5_ROWS = 8
GDN_CHUNK = 64
GDN_CONV = 4
LANES = 1024
VMEM_LIMIT_BYTES = 56 * 1024 * 1024
MESH = pl.DeviceIdType.MESH
RS_PAYLOAD = jnp.bfloat16

ADAM_LR, ADAM_B1, ADAM_B2, ADAM_EPS, ADAM_WD, ADAM_STEP = 0.001, 0.9, 0.999, 1e-08, 0.01, 10

MM_TM, MM_TN, MM_TK = 1024, 1024, 1024
ROW_TILE = 256
FOX_TILE = 512
MEM_TILE = 512
CONV_TILE = 1024

NN = (((1,), (0,)), ((), ()))
NT = (((1,), (1,)), ((), ()))
TN = (((0,), (0,)), ((), ()))

WEIGHTS = ['mem_norm', 'w_mem_kv', 'norm1', 'w_out', 'norm2', 'w_up', 'w_down', 'norm_f', 's5_w_in',
           's5_lam_re', 's5_lam_im', 's5_log_dt', 's5_b_re', 's5_b_im', 's5_c_re', 's5_c_im', 's5_d_skip',
           's5_w_glu', 's5_b_glu', 'gdn_w_in', 'gdn_conv_w', 'gdn_a_log', 'gdn_dt_bias', 'gdn_o_norm',
           'fox_w_in', 'fox_b_f']
SHARD_AXIS = {'w_mem_kv': 0, 'w_out': 1, 'w_up': 2, 'w_down': 1, 's5_w_in': 1, 's5_d_skip': 1,
              's5_w_glu': 1, 's5_b_glu': 1, 'gdn_w_in': 2, 'gdn_conv_w': 2, 'fox_w_in': 2}
MATMUL_WEIGHTS = ['w_mem_kv', 'w_out', 'w_up', 'w_down', 's5_w_in', 's5_w_glu', 'gdn_w_in', 'fox_w_in']
VECTOR_SHARDED = ['s5_d_skip', 's5_b_glu', 'gdn_conv_w']
REPLICATED = [n for n in WEIGHTS if n not in SHARD_AXIS]


def _tile(dim, target, align=128):
    if dim <= target:
        return dim
    t = (target // align) * align
    while t >= align:
        if dim % t == 0:
            return t
        t -= align
    return dim


def _cp(sem=None, **kw):
    return pltpu.CompilerParams(dimension_semantics=sem, vmem_limit_bytes=VMEM_LIMIT_BYTES, **kw)


def _dot(a, b, dims):
    return lax.dot_general(a.astype(MXU_DTYPE), b.astype(MXU_DTYPE), dims, preferred_element_type=F32)


def _dotf(a, b, dims):
    return lax.dot_general(a, b, dims, precision=lax.Precision.HIGHEST, preferred_element_type=F32)


def _sigmoid(x):
    return 1.0 / (1.0 + jnp.exp(-x))


def _softplus(x):
    return jnp.maximum(x, 0.0) + jnp.log(1.0 + jnp.exp(-jnp.abs(x)))


def _relu2(x):
    r = jnp.maximum(x, 0.0)
    return r * r


_GELU_C = math.sqrt(2.0 / math.pi)


def _gelu(x):
    return 0.5 * x * (1.0 + jnp.tanh(_GELU_C * (x + 0.044715 * x * x * x)))


def _gelu_grad(x):
    t = jnp.tanh(_GELU_C * (x + 0.044715 * x * x * x))
    return 0.5 * (1.0 + t) + 0.5 * x * (1.0 - t * t) * _GELU_C * (1.0 + 3.0 * 0.044715 * x * x)


def _silu_grad(x):
    s = _sigmoid(x)
    return s + x * s * (1.0 - s)


def _mm(a, b, *, name, ta=False, tb=False, a_pro=None, extras=(), epi=None, out_dtypes=(F32,)):
    K, M = a.shape if ta else a.shape[::-1]
    N = b.shape[0] if tb else b.shape[1]
    assert (b.shape[1] if tb else b.shape[0]) == K, (a.shape, b.shape, ta, tb)
    tm, tn, tk = _tile(M, MM_TM), _tile(N, MM_TN), _tile(K, MM_TK)
    nk = K // tk
    n_ex, n_out = len(extras), len(out_dtypes)
    dims = TN if ta else (NT if tb else NN)

    def body(*refs):
        a_ref, b_ref = refs[0], refs[1]
        ex = refs[2:2 + n_ex]
        outs = refs[2 + n_ex:2 + n_ex + n_out]
        acc = refs[-1]
        k = pl.program_id(2)

        @pl.when(k == 0)
        def _():
            acc[...] = jnp.zeros_like(acc)

        at = a_ref[...]
        if a_pro is not None:
            at = a_pro(at)
        acc[...] += _dot(at, b_ref[...], dims)

        @pl.when(k == nk - 1)
        def _():
            res = acc[...]
            vals = epi(res, *[e[...] for e in ex]) if epi is not None else (res,)
            for o, v in zip(outs, vals):
                o[...] = v.astype(o.dtype)

    if ta:
        a_spec = pl.BlockSpec((tk, tm), lambda i, j, k: (k, i))
    else:
        a_spec = pl.BlockSpec((tm, tk), lambda i, j, k: (i, k))
    if tb:
        b_spec = pl.BlockSpec((tn, tk), lambda i, j, k: (j, k))
    else:
        b_spec = pl.BlockSpec((tk, tn), lambda i, j, k: (k, j))
    ex_specs, ex_arrays = [], []
    for arr, kind in extras:
        if kind == 'ij':
            ex_specs.append(pl.BlockSpec((tm, tn), lambda i, j, k: (i, j)))
            ex_arrays.append(arr)
        else:
            ex_specs.append(pl.BlockSpec((1, tn), lambda i, j, k: (0, j)))
            ex_arrays.append(arr.reshape(1, N))
    outs = pl.pallas_call(
        body, name=name, grid=(M // tm, N // tn, nk),
        in_specs=[a_spec, b_spec] + ex_specs,
        out_specs=[pl.BlockSpec((tm, tn), lambda i, j, k: (i, j)) for _ in out_dtypes],
        out_shape=[jax.ShapeDtypeStruct((M, N), dt) for dt in out_dtypes],
        scratch_shapes=[pltpu.VMEM((tm, tn), F32)],
        compiler_params=_cp(("parallel", "parallel", "arbitrary")),
    )(a, b, *ex_arrays)
    return outs[0] if n_out == 1 else tuple(outs)


def _rms_fwd(x, g, out_dtype, name):
    S, D = x.shape
    tr = _tile(S, ROW_TILE, 8)

    def body(x_ref, g_ref, o_ref):
        xv = x_ref[...]
        r = lax.rsqrt(jnp.mean(xv * xv, axis=-1, keepdims=True) + EPS)
        o_ref[...] = (xv * r * g_ref[...]).astype(o_ref.dtype)

    return pl.pallas_call(
        body, name=name, grid=(S // tr,),
        in_specs=[pl.BlockSpec((tr, D), lambda i: (i, 0)), pl.BlockSpec((1, D), lambda i: (0, 0))],
        out_specs=pl.BlockSpec((tr, D), lambda i: (i, 0)),
        out_shape=jax.ShapeDtypeStruct((S, D), out_dtype),
        compiler_params=_cp(("parallel",)),
    )(x, g.reshape(1, D))


def _rms_bwd(x, g, dy, res, name):
    S, D = x.shape
    tr = _tile(S, ROW_TILE, 8)
    has_res = res is not None

    def body(*refs):
        if has_res:
            x_ref, g_ref, dy_ref, res_ref, dx_ref, dg_ref = refs
        else:
            x_ref, g_ref, dy_ref, dx_ref, dg_ref = refs
        i = pl.program_id(0)

        @pl.when(i == 0)
        def _():
            dg_ref[...] = jnp.zeros_like(dg_ref)

        xv, d = x_ref[...], dy_ref[...].astype(F32)
        r = lax.rsqrt(jnp.mean(xv * xv, axis=-1, keepdims=True) + EPS)
        xh = xv * r
        t = d * g_ref[...]
        dx = r * (t - xh * jnp.mean(t * xh, axis=-1, keepdims=True))
        if has_res:
            dx = dx + res_ref[...]
        dx_ref[...] = dx
        dg_ref[...] += jnp.sum(d * xh, axis=0, keepdims=True)

    row = pl.BlockSpec((tr, D), lambda i: (i, 0))
    vec = pl.BlockSpec((1, D), lambda i: (0, 0))
    ins = [x, g.reshape(1, D), dy] + ([res] if has_res else [])
    return pl.pallas_call(
        body, name=name, grid=(S // tr,),
        in_specs=[row, vec, row] + ([row] if has_res else []),
        out_specs=[row, vec],
        out_shape=[jax.ShapeDtypeStruct((S, D), F32), jax.ShapeDtypeStruct((1, D), F32)],
        compiler_params=_cp(("arbitrary",)),
    )(*ins)


def _loss_head(h, g, target):
    S, D = h.shape
    tr = _tile(S, ROW_TILE, 8)

    def body(h_ref, g_ref, t_ref, loss_ref, dh_ref, dg_ref):
        i = pl.program_id(0)

        @pl.when(i == 0)
        def _():
            loss_ref[...] = jnp.zeros_like(loss_ref)
            dg_ref[...] = jnp.zeros_like(dg_ref)

        xv = h_ref[...]
        gv = g_ref[...]
        r = lax.rsqrt(jnp.mean(xv * xv, axis=-1, keepdims=True) + EPS)
        xh = xv * r
        err = xh * gv - t_ref[...]
        part = 0.5 * jnp.sum(jnp.mean(err * err, axis=-1, keepdims=True), axis=0, keepdims=True)
        loss_ref[...] += jnp.broadcast_to(part, loss_ref.shape)
        d = err * (1.0 / D)
        t = d * gv
        dh_ref[...] = r * (t - xh * jnp.mean(t * xh, axis=-1, keepdims=True))
        dg_ref[...] += jnp.sum(d * xh, axis=0, keepdims=True)

    row = pl.BlockSpec((tr, D), lambda i: (i, 0))
    vec = pl.BlockSpec((1, D), lambda i: (0, 0))
    return pl.pallas_call(
        body, name="loss_head", grid=(S // tr,),
        in_specs=[row, vec, row],
        out_specs=[pl.BlockSpec((8, 128), lambda i: (0, 0)), row, vec],
        out_shape=[jax.ShapeDtypeStruct((8, 128), F32), jax.ShapeDtypeStruct((S, D), F32),
                   jax.ShapeDtypeStruct((1, D), F32)],
        compiler_params=_cp(("arbitrary",)),
    )(h, g.reshape(1, D), target)


def _adamw(w, g, m, v, name):
    R, C = w.shape
    tr = _tile(R, max(8, (1 << 19) // max(C, 1) // 8 * 8), 8)
    c1 = 1.0 / (1.0 - ADAM_B1 ** ADAM_STEP)
    c2 = 1.0 / (1.0 - ADAM_B2 ** ADAM_STEP)

    def body(w_ref, g_ref, m_ref, v_ref, d_ref, nm_ref, nv_ref):
        gv = g_ref[...]
        nm = ADAM_B1 * m_ref[...] + (1.0 - ADAM_B1) * gv
        nv = ADAM_B2 * v_ref[...] + (1.0 - ADAM_B2) * (gv * gv)
        d_ref[...] = -ADAM_LR * ((nm * c1) / (jnp.sqrt(nv * c2) + ADAM_EPS) + ADAM_WD * w_ref[...])
        nm_ref[...] = nm
        nv_ref[...] = nv

    blk = pl.BlockSpec((tr, C), lambda i: (i, 0))
    return pl.pallas_call(
        body, name=name, grid=(R // tr,),
        in_specs=[blk] * 4, out_specs=[blk] * 3,
        out_shape=[jax.ShapeDtypeStruct((R, C), F32)] * 3,
        compiler_params=_cp(("parallel",)),
    )(w, g, m, v)


def _place():
    x, y, c = lax.axis_index("x"), lax.axis_index("y"), lax.axis_index("c")
    chips = [(1 - x, y), (x, 1 - y), (1 - x, 1 - y)]
    return x, y, c, chips


def _all_gather_chips(xs, name):
    r, n = xs.shape
    half = r // 2

    def body(x_ref, out_ref, send_sems, recv_sems):
        x, y, c, chips = _place()
        me = 2 * x + y
        sibling = (x, y, 1 - c)

        def piece(chip, hc):
            return out_ref.at[chip, pl.ds(hc * half, half), :]

        def copy(k, src, dst, to):
            return pltpu.make_async_remote_copy(src_ref=src, dst_ref=dst, send_sem=send_sems.at[k],
                                                recv_sem=recv_sems.at[k], device_id=to, device_id_type=MESH)

        src = x_ref.at[pl.ds(c * half, half), :]
        first = [copy(j, src, piece(me, c), (cx, cy, c)) for j, (cx, cy) in enumerate(chips)]
        for cp in first:
            cp.start()
        passed = []
        for j, (cx, cy) in enumerate(chips):
            got = piece(2 * cx + cy, c)
            copy(j, got, got, (cx, cy, c)).wait_recv()
            fwd = copy(3 + j, got, got, sibling)
            fwd.start()
            passed.append(fwd)
        for j, (cx, cy) in enumerate(chips):
            got = piece(2 * cx + cy, 1 - c)
            copy(3 + j, got, got, sibling).wait_recv()
        for cp in first + passed:
            cp.wait_send()

    return pl.pallas_call(
        body, name=name,
        in_specs=[pl.BlockSpec(memory_space=pl.ANY)],
        out_specs=pl.BlockSpec(memory_space=pl.ANY),
        out_shape=jax.ShapeDtypeStruct((4, r, n), xs.dtype),
        scratch_shapes=[pltpu.SemaphoreType.DMA((6,)), pltpu.SemaphoreType.DMA((6,))],
    )(xs)


def _gather(xs, me, name):
    return lax.dynamic_update_slice(_all_gather_chips(xs, name), xs[None], (me, 0, 0))


def _rs_swap_halves(g, name):
    _, r, n = g.shape
    half = r // 2

    def body(g_ref, out_ref, send_sem, recv_sem):
        x, y, c, _ = _place()
        cp = pltpu.make_async_remote_copy(
            src_ref=g_ref.at[:, pl.ds((1 - c) * half, half), :], dst_ref=out_ref,
            send_sem=send_sem, recv_sem=recv_sem, device_id=(x, y, 1 - c), device_id_type=MESH)
        cp.start()
        cp.wait()

    return pl.pallas_call(
        body, name=name,
        in_specs=[pl.BlockSpec(memory_space=pl.ANY)],
        out_specs=pl.BlockSpec(memory_space=pl.ANY),
        out_shape=jax.ShapeDtypeStruct((4, half, n), g.dtype),
        scratch_shapes=[pltpu.SemaphoreType.DMA, pltpu.SemaphoreType.DMA],
    )(g)


def _rs_add_halves(g, got, c_idx, name):
    _, r, n = g.shape
    half = r // 2
    tr = _tile(half, max(16, (1 << 19) // n // 16 * 16), 16)
    nb = half // tr

    def body(c_ref, g_ref, o_ref, out_ref, out16_ref):
        sm = g_ref[...] + o_ref[...]
        out_ref[...] = sm
        out16_ref[...] = sm.astype(out16_ref.dtype)

    blk = pl.BlockSpec((None, tr, n), lambda s, i, c: (s, i, 0))
    return pl.pallas_call(
        body, name=name,
        grid_spec=pltpu.PrefetchScalarGridSpec(
            num_scalar_prefetch=1, grid=(4, nb),
            in_specs=[pl.BlockSpec((None, tr, n), lambda s, i, c: (s, c[0] * nb + i, 0)), blk],
            out_specs=[blk, blk]),
        out_shape=[jax.ShapeDtypeStruct((4, half, n), F32), jax.ShapeDtypeStruct((4, half, n), RS_PAYLOAD)],
        compiler_params=_cp(("parallel", "parallel")),
    )(c_idx, g, got)


def _rs_exchange_chips(p, name):
    _, h, n = p.shape

    def body(p_ref, out_ref, send_sems, recv_sems):
        x, y, c, chips = _place()
        copies = [pltpu.make_async_remote_copy(
            src_ref=p_ref.at[2 * cx + cy], dst_ref=out_ref.at[j], send_sem=send_sems.at[j],
            recv_sem=recv_sems.at[j], device_id=(cx, cy, c), device_id_type=MESH)
            for j, (cx, cy) in enumerate(chips)]
        for cp in copies:
            cp.start()
        for cp in copies:
            cp.wait()

    return pl.pallas_call(
        body, name=name,
        in_specs=[pl.BlockSpec(memory_space=pl.ANY)],
        out_specs=pl.BlockSpec(memory_space=pl.ANY),
        out_shape=jax.ShapeDtypeStruct((3, h, n), p.dtype),
        scratch_shapes=[pltpu.SemaphoreType.DMA((3,)), pltpu.SemaphoreType.DMA((3,))],
    )(p)


def _rs_add_chips(p, got, me_idx, name):
    _, h, n = p.shape
    tr = _tile(h, max(16, (1 << 19) // n // 16 * 16), 16)

    def body(me_ref, p_ref, a_ref, b_ref, c_ref, out_ref):
        out_ref[...] = ((p_ref[...] + a_ref[...].astype(F32)) + b_ref[...].astype(F32)) + c_ref[...].astype(F32)

    def got_spec(j):
        return pl.BlockSpec((None, tr, n), lambda i, me: (j, i, 0))

    return pl.pallas_call(
        body, name=name,
        grid_spec=pltpu.PrefetchScalarGridSpec(
            num_scalar_prefetch=1, grid=(h // tr,),
            in_specs=[pl.BlockSpec((None, tr, n), lambda i, me: (me[0], i, 0)),
                      got_spec(0), got_spec(1), got_spec(2)],
            out_specs=pl.BlockSpec((tr, n), lambda i, me: (i, 0))),
        out_shape=jax.ShapeDtypeStruct((h, n), F32),
        compiler_params=_cp(("parallel",)),
    )(me_idx, p, got, got, got)


def _rs_share_halves(q, name):
    h, n = q.shape

    def body(q_ref, out_ref, send_sem, recv_sem):
        x, y, c, _ = _place()
        cp = pltpu.make_async_remote_copy(
            src_ref=q_ref, dst_ref=out_ref.at[pl.ds(c * h, h), :], send_sem=send_sem, recv_sem=recv_sem,
            device_id=(x, y, 1 - c), device_id_type=MESH)
        cp.start()
        pltpu.make_async_remote_copy(
            src_ref=q_ref, dst_ref=out_ref.at[pl.ds((1 - c) * h, h), :], send_sem=send_sem, recv_sem=recv_sem,
            device_id=(x, y, 1 - c), device_id_type=MESH).wait_recv()
        cp.wait_send()

    return pl.pallas_call(
        body, name=name,
        in_specs=[pl.BlockSpec(memory_space=pl.ANY)],
        out_specs=pl.BlockSpec(memory_space=pl.ANY),
        out_shape=jax.ShapeDtypeStruct((2 * h, n), q.dtype),
        scratch_shapes=[pltpu.SemaphoreType.DMA, pltpu.SemaphoreType.DMA],
    )(q)


def _reduce_scatter(g, place):
    c, c_idx, me_idx = place['c'], place['c_idx'], place['me_idx']
    got = _rs_swap_halves(g, "rs_swap_halves")
    p, p16 = _rs_add_halves(g, got, c_idx, "rs_add_halves")
    got = _rs_exchange_chips(p16, "rs_exchange_chips")
    q = _rs_add_chips(p, got, me_idx, "rs_add_chips")
    out = _rs_share_halves(q, "rs_share_halves")
    return lax.dynamic_update_slice(out, q, (c * q.shape[0], 0))


def _all_reduce_small(v, name):
    R, n = v.shape

    def body(v_ref, out_ref, buf, send_sems, recv_sems):
        x, y, c, _ = _place()
        me = 4 * x + 2 * y + c
        buf[me] = v_ref[...]
        copies = []
        for d in range(1, 8):
            dx, dy, dc = (d >> 2) & 1, (d >> 1) & 1, d & 1
            px = x if dx == 0 else 1 - x
            py = y if dy == 0 else 1 - y
            pc = c if dc == 0 else 1 - c
            copies.append(pltpu.make_async_remote_copy(
                src_ref=v_ref, dst_ref=buf.at[me], send_sem=send_sems.at[d - 1], recv_sem=recv_sems.at[d - 1],
                device_id=(px, py, pc), device_id_type=MESH))
        for cp in copies:
            cp.start()
        for d in range(1, 8):
            dx, dy, dc = (d >> 2) & 1, (d >> 1) & 1, d & 1
            px = x if dx == 0 else 1 - x
            py = y if dy == 0 else 1 - y
            pc = c if dc == 0 else 1 - c
            pltpu.make_async_remote_copy(
                src_ref=v_ref, dst_ref=buf.at[4 * px + 2 * py + pc], send_sem=send_sems.at[d - 1],
                recv_sem=recv_sems.at[d - 1], device_id=(px, py, pc), device_id_type=MESH).wait_recv()
        for cp in copies:
            cp.wait_send()
        acc = buf[0]
        for k in range(1, 8):
            acc = acc + buf[k]
        out_ref[...] = acc

    return pl.pallas_call(
        body, name=name,
        in_specs=[pl.BlockSpec(memory_space=pltpu.VMEM)],
        out_specs=pl.BlockSpec(memory_space=pltpu.VMEM),
        out_shape=jax.ShapeDtypeStruct((R, n), F32),
        scratch_shapes=[pltpu.VMEM((8, R, n), F32), pltpu.SemaphoreType.DMA((7,)), pltpu.SemaphoreType.DMA((7,))],
        compiler_params=pltpu.CompilerParams(vmem_limit_bytes=VMEM_LIMIT_BYTES),
    )(v)


def _pack_small(parts, rows):
    flat = jnp.concatenate([a.reshape(-1) for a in parts])
    return jnp.pad(flat, (0, rows * LANES - flat.shape[0])).reshape(rows, LANES)


def _mem_fwd(proj, q_blk, mkv, heads):
    S = proj.shape[0]
    ML = mkv.shape[0]
    t = _tile(S, MEM_TILE, 8)
    scale = HEAD ** -0.5

    def body(q_ref, k_ref, v_ref, o_ref):
        s = _dot(q_ref[...], k_ref[...], NT) * scale
        m = jnp.max(s, axis=-1, keepdims=True)
        e = jnp.exp(s - m)
        p = e / jnp.sum(e, axis=-1, keepdims=True)
        o_ref[...] = _dot(p, v_ref[...], NN)

    return pl.pallas_call(
        body, name="mem_fwd", grid=(S // t, heads),
        in_specs=[pl.BlockSpec((t, HEAD), lambda i, h: (i, q_blk + h)),
                  pl.BlockSpec((ML, HEAD), lambda i, h: (0, h)),
                  pl.BlockSpec((ML, HEAD), lambda i, h: (0, heads + h))],
        out_specs=pl.BlockSpec((t, HEAD), lambda i, h: (i, h)),
        out_shape=jax.ShapeDtypeStruct((S, heads * HEAD), F32),
        compiler_params=_cp(("parallel", "parallel")),
    )(proj, mkv, mkv)


def _mem_bwd(proj, q_blk, mkv, dcat, d_blk, heads):
    S = proj.shape[0]
    ML = mkv.shape[0]
    t = _tile(S, MEM_TILE, 8)
    scale = HEAD ** -0.5

    def body(q_ref, k_ref, v_ref, do_ref, dq_ref, dk_ref, dv_ref):
        i = pl.program_id(1)

        @pl.when(i == 0)
        def _():
            dk_ref[...] = jnp.zeros_like(dk_ref)
            dv_ref[...] = jnp.zeros_like(dv_ref)

        q, k, v, do = q_ref[...], k_ref[...], v_ref[...], do_ref[...]
        s = _dot(q, k, NT) * scale
        m = jnp.max(s, axis=-1, keepdims=True)
        e = jnp.exp(s - m)
        p = e / jnp.sum(e, axis=-1, keepdims=True)
        dp = _dot(do, v, NT)
        ds = p * (dp - jnp.sum(p * dp, axis=-1, keepdims=True))
        dq_ref[...] = _dot(ds, k, NN) * scale
        dk_ref[...] += _dot(ds, q, TN) * scale
        dv_ref[...] += _dot(p, do, TN)

    dq, dk, dv = pl.pallas_call(
        body, name="mem_bwd", grid=(heads, S // t),
        in_specs=[pl.BlockSpec((t, HEAD), lambda h, i: (i, q_blk + h)),
                  pl.BlockSpec((ML, HEAD), lambda h, i: (0, h)),
                  pl.BlockSpec((ML, HEAD), lambda h, i: (0, heads + h)),
                  pl.BlockSpec((t, HEAD), lambda h, i: (i, d_blk + h))],
        out_specs=[pl.BlockSpec((t, HEAD), lambda h, i: (i, h)),
                   pl.BlockSpec((ML, HEAD), lambda h, i: (0, h)),
                   pl.BlockSpec((ML, HEAD), lambda h, i: (0, h))],
        out_shape=[jax.ShapeDtypeStruct((S, heads * HEAD), F32),
                   jax.ShapeDtypeStruct((ML, heads * HEAD), F32),
                   jax.ShapeDtypeStruct((ML, heads * HEAD), F32)],
        compiler_params=_cp(("parallel", "arbitrary")),
    )(proj, mkv, mkv, dcat)
    return dq, jnp.concatenate([dk, dv], axis=1)


def _fox_gates(gl, bf):
    S = gl.shape[0]

    def body(g_ref, b_ref, o_ref):
        xv = g_ref[...] + b_ref[...]
        c = jnp.minimum(xv, 0.0) - jnp.log(1.0 + jnp.exp(-jnp.abs(xv)))
        row = lax.broadcasted_iota(jnp.int32, c.shape, 0)
        d = 1
        while d < S:
            c = c + jnp.where(row >= d, pltpu.roll(c, d, 0), 0.0)
            d *= 2
        o_ref[...] = c

    return pl.pallas_call(
        body, name="fox_gates", out_shape=jax.ShapeDtypeStruct((S, 128), F32),
        in_specs=[pl.BlockSpec(memory_space=pltpu.VMEM)] * 2,
        out_specs=pl.BlockSpec(memory_space=pltpu.VMEM),
        compiler_params=_cp(),
    )(gl, bf)


def _fox_gates_bwd(gl, bf, dcf):
    S = gl.shape[0]

    def body(g_ref, b_ref, d_ref, dg_ref, db_ref):
        c = d_ref[...]
        row = lax.broadcasted_iota(jnp.int32, c.shape, 0)
        d = 1
        while d < S:
            c = c + jnp.where(row < S - d, pltpu.roll(c, S - d, 0), 0.0)
            d *= 2
        dx = c * _sigmoid(-(g_ref[...] + b_ref[...]))
        dg_ref[...] = dx
        db_ref[...] = jnp.sum(dx, axis=0, keepdims=True)

    return pl.pallas_call(
        body, name="fox_gates_bwd",
        out_shape=[jax.ShapeDtypeStruct((S, 128), F32), jax.ShapeDtypeStruct((1, 128), F32)],
        in_specs=[pl.BlockSpec(memory_space=pltpu.VMEM)] * 3,
        out_specs=[pl.BlockSpec(memory_space=pltpu.VMEM)] * 2,
        compiler_params=_cp(),
    )(gl, bf, dcf)


def _fox_scores(q, k, cq, ck, t, masked):
    s = _dot(q, k, NT) * (HEAD ** -0.5) + cq - ck
    if masked:
        row = lax.broadcasted_iota(jnp.int32, (t, t), 0)
        col = lax.broadcasted_iota(jnp.int32, (t, t), 1)
        s = jnp.where(row >= col, s, -jnp.inf)
    return s


def _fox_pairs(nq, by_key):
    if by_key:
        pairs = [(i, j) for j in range(nq) for i in range(j, nq)]
    else:
        pairs = [(i, j) for i in range(nq) for j in range(i + 1)]
    return (jnp.asarray(np.array([a for a, _ in pairs], np.int32)),
            jnp.asarray(np.array([b for _, b in pairs], np.int32)))


def _fox_heads_per_step(H):
    return 2 if H % 2 == 0 else 1


def _fox_fwd(proj, cfq, cfk, H):
    S = proj.shape[0]
    t = _tile(S, FOX_TILE)
    nq = S // t
    hb = _fox_heads_per_step(H)
    W, G = hb * HEAD, H // hb
    cols = [slice(i * HEAD, (i + 1) * HEAD) for i in range(hb)]
    qt, kt = _fox_pairs(nq, False)

    def body(qt_ref, kt_ref, q_ref, k_ref, v_ref, cq_ref, ck_ref, o_ref, lse_ref, m_s, l_s, acc_s):
        n = pl.program_id(1)
        qi, ki = qt_ref[n], kt_ref[n]

        @pl.when(ki == 0)
        def _():
            m_s[...] = jnp.full_like(m_s, -jnp.inf)
            l_s[...] = jnp.zeros_like(l_s)
            acc_s[...] = jnp.zeros_like(acc_s)

        def step(masked):
            R = range(hb)
            ss = [_fox_scores(q_ref[:, cols[i]], k_ref[:, cols[i]], cq_ref[i], ck_ref[i], t, masked) for i in R]
            m_new = [jnp.maximum(m_s[i], jnp.max(ss[i], axis=-1, keepdims=True)) for i in R]
            alpha = [jnp.exp(m_s[i] - m_new[i]) for i in R]
            ps = [jnp.exp(ss[i] - m_new[i]) for i in R]
            pv = [_dot(ps[i], v_ref[:, cols[i]], NN) for i in R]
            for i in R:
                l_s[i] = alpha[i] * l_s[i] + jnp.sum(ps[i], axis=-1, keepdims=True)
                acc_s[:, cols[i]] = alpha[i] * acc_s[:, cols[i]] + pv[i]
                m_s[i] = m_new[i]

        @pl.when(ki != qi)
        def _():
            step(False)

        @pl.when(ki == qi)
        def _():
            step(True)
            for i in range(hb):
                o_ref[:, cols[i]] = acc_s[:, cols[i]] / l_s[i]
                lse_ref[i] = m_s[i] + jnp.log(l_s[i])

    qcol = pl.BlockSpec((hb, t, 1), lambda h, n, qt, kt: (h, qt[n], 0))
    return pl.pallas_call(
        body, name="fox_fwd",
        grid_spec=pltpu.PrefetchScalarGridSpec(
            num_scalar_prefetch=2, grid=(G, qt.shape[0]),
            in_specs=[pl.BlockSpec((t, W), lambda h, n, qt, kt: (qt[n], h)),
                      pl.BlockSpec((t, W), lambda h, n, qt, kt: (kt[n], G + h)),
                      pl.BlockSpec((t, W), lambda h, n, qt, kt: (kt[n], 2 * G + h)),
                      qcol,
                      pl.BlockSpec((hb, 1, t), lambda h, n, qt, kt: (h, 0, kt[n]))],
            out_specs=[pl.BlockSpec((t, W), lambda h, n, qt, kt: (qt[n], h)), qcol],
            scratch_shapes=[pltpu.VMEM((hb, t, 1), F32), pltpu.VMEM((hb, t, 1), F32), pltpu.VMEM((t, W), F32)]),
        out_shape=[jax.ShapeDtypeStruct((S, H * HEAD), F32), jax.ShapeDtypeStruct((H, S, 1), F32)],
        compiler_params=_cp(("parallel", "arbitrary")),
    )(qt, kt, proj, proj, proj, cfq, cfk)


def _fox_bwd_rowdot(proj, cfq, cfk, lse, dcat, H):
    S = proj.shape[0]
    t = _tile(S, FOX_TILE)
    nq = S // t
    hb = _fox_heads_per_step(H)
    W, G = hb * HEAD, H // hb
    cols = [slice(i * HEAD, (i + 1) * HEAD) for i in range(hb)]
    qt, kt = _fox_pairs(nq, False)

    def body(qt_ref, kt_ref, q_ref, k_ref, v_ref, do_ref, lse_ref, cq_ref, ck_ref, d_ref):
        n = pl.program_id(1)
        qi, ki = qt_ref[n], kt_ref[n]

        @pl.when(ki == 0)
        def _():
            d_ref[...] = jnp.zeros_like(d_ref)

        def step(masked):
            R = range(hb)
            ss = [_fox_scores(q_ref[:, cols[i]], k_ref[:, cols[i]], cq_ref[i], ck_ref[i], t, masked) for i in R]
            dps = [_dot(do_ref[:, cols[i]], v_ref[:, cols[i]], NT) for i in R]
            ps = [jnp.exp(ss[i] - lse_ref[i]) for i in R]
            for i in R:
                d_ref[i] += jnp.sum(ps[i] * dps[i], axis=-1, keepdims=True)

        @pl.when(ki != qi)
        def _():
            step(False)

        @pl.when(ki == qi)
        def _():
            step(True)

    qtile = pl.BlockSpec((t, W), lambda h, n, qt, kt: (qt[n], h))
    qcol = pl.BlockSpec((hb, t, 1), lambda h, n, qt, kt: (h, qt[n], 0))
    return pl.pallas_call(
        body, name="fox_bwd_rowdot",
        grid_spec=pltpu.PrefetchScalarGridSpec(
            num_scalar_prefetch=2, grid=(G, qt.shape[0]),
            in_specs=[qtile,
                      pl.BlockSpec((t, W), lambda h, n, qt, kt: (kt[n], G + h)),
                      pl.BlockSpec((t, W), lambda h, n, qt, kt: (kt[n], 2 * G + h)),
                      qtile, qcol, qcol,
                      pl.BlockSpec((hb, 1, t), lambda h, n, qt, kt: (h, 0, kt[n]))],
            out_specs=qcol),
        out_shape=jax.ShapeDtypeStruct((H, S, 1), F32),
        compiler_params=_cp(("parallel", "arbitrary")),
    )(qt, kt, proj, proj, proj, dcat, lse, cfq, cfk)


def _fox_bwd(proj, cfq, cfk, rowdot, lse, dcat, H):
    S = proj.shape[0]
    t = _tile(S, FOX_TILE)
    nq = S // t
    scale = HEAD ** -0.5
    hb = _fox_heads_per_step(H)
    W, G = hb * HEAD, H // hb
    cols = [slice(i * HEAD, (i + 1) * HEAD) for i in range(hb)]
    qt, kt = _fox_pairs(nq, True)

    def body(qt_ref, kt_ref, q_ref, k_ref, v_ref, dd_ref, do_ref, lse_ref, cq_ref, ck_ref,
             dq_ref, dk_ref, dv_ref, dck_ref):
        n = pl.program_id(1)
        i_, j_ = qt_ref[n], kt_ref[n]

        @pl.when(n == 0)
        def _():
            dq_ref[...] = jnp.zeros_like(dq_ref)

        @pl.when(i_ == j_)
        def _():
            dk_ref[...] = jnp.zeros_like(dk_ref)
            dv_ref[...] = jnp.zeros_like(dv_ref)
            dck_ref[...] = jnp.zeros_like(dck_ref)

        def step(masked):
            R = range(hb)
            qs, ks = [q_ref[:, c] for c in cols], [k_ref[:, c] for c in cols]
            dos = [do_ref[:, c] for c in cols]
            ss = [_fox_scores(qs[i], ks[i], cq_ref[i], ck_ref[i], t, masked) for i in R]
            dps = [_dot(dos[i], v_ref[:, cols[i]], NT) for i in R]
            ps = [jnp.exp(ss[i] - lse_ref[i]) for i in R]
            dss = [ps[i] * (dps[i] - dd_ref[i]) for i in R]
            dvs = [_dot(ps[i], dos[i], TN) for i in R]
            dks = [_dot(dss[i], qs[i], TN) * scale for i in R]
            dqs = [_dot(dss[i], ks[i], NN) * scale for i in R]
            rows = pl.ds(pl.multiple_of(i_ * t, t), t)
            for i in R:
                dv_ref[:, cols[i]] += dvs[i]
                dk_ref[:, cols[i]] += dks[i]
                dq_ref[rows, cols[i]] += dqs[i]
                dck_ref[i] -= jnp.sum(dss[i], axis=0, keepdims=True)

        @pl.when(i_ != j_)
        def _():
            step(False)

        @pl.when(i_ == j_)
        def _():
            step(True)

    qtile = pl.BlockSpec((t, W), lambda h, n, qt, kt: (qt[n], h))
    qcol = pl.BlockSpec((hb, t, 1), lambda h, n, qt, kt: (h, qt[n], 0))
    ktile = pl.BlockSpec((t, W), lambda h, n, qt, kt: (kt[n], h))
    krow = pl.BlockSpec((hb, 1, t), lambda h, n, qt, kt: (h, 0, kt[n]))
    return pl.pallas_call(
        body, name="fox_bwd",
        grid_spec=pltpu.PrefetchScalarGridSpec(
            num_scalar_prefetch=2, grid=(G, qt.shape[0]),
            in_specs=[qtile,
                      pl.BlockSpec((t, W), lambda h, n, qt, kt: (kt[n], G + h)),
                      pl.BlockSpec((t, W), lambda h, n, qt, kt: (kt[n], 2 * G + h)),
                      qcol, qtile, qcol, qcol, krow],
            out_specs=[pl.BlockSpec((S, W), lambda h, n, qt, kt: (0, h)), ktile, ktile, krow]),
        out_shape=[jax.ShapeDtypeStruct((S, H * HEAD), F32)] * 3 + [jax.ShapeDtypeStruct((H, 1, S), F32)],
        compiler_params=_cp(("parallel", "arbitrary")),
    )(qt, kt, proj, proj, proj, rowdot, dcat, lse, cfq, cfk)


def _s5_prep(lam_re, lam_im, log_dt, b_re, b_im, c_re, c_im):
    G, P = lam_re.shape
    ns = G // 16
    dt = jnp.exp(log_dt)[:, None]
    mag = jnp.exp(lam_re * dt)
    a_re, a_im = mag * jnp.cos(lam_im * dt), mag * jnp.sin(lam_im * dt)
    den = lam_re * lam_re + lam_im * lam_im
    z_re = ((a_re - 1.0) * lam_re + a_im * lam_im) / den
    z_im = (a_im * lam_re - (a_re - 1.0) * lam_im) / den
    bb_re = z_re[..., None] * b_re - z_im[..., None] * b_im
    bb_im = z_re[..., None] * b_im + z_im[..., None] * b_re
    eye = jnp.eye(16, dtype=F32)
    bb = jnp.stack([bb_re, bb_im]).reshape(2, ns, 16, P, S5_GROUP)
    wb = jnp.einsum('asgpc,gh->sgcahp', bb, eye).reshape(ns, S5_SLAB, 2 * 16 * P)
    cc = jnp.stack([c_re, -c_im]).reshape(2, ns, 16, S5_GROUP, P)
    wc = jnp.einsum('asgcp,gh->sagphc', cc, eye).reshape(ns, 2 * 16 * P, S5_SLAB)
    a = jnp.concatenate([a_re.reshape(ns, 1, 16 * P), a_im.reshape(ns, 1, 16 * P)], axis=-1)
    return wb, wc, a


def _s5_tables(lam_re, lam_im, log_dt):
    G, P = lam_re.shape
    ns = G // 16
    dt = jnp.exp(log_dt)[:, None]
    tt = jnp.arange(1, S5_CHUNK + 1, dtype=F32)[:, None, None]
    mag = jnp.exp(lam_re * dt * tt)
    ang = lam_im * dt * tt
    pr = (mag * jnp.cos(ang)).reshape(S5_CHUNK, ns, 16 * P).transpose(1, 0, 2)
    pi = (mag * jnp.sin(ang)).reshape(S5_CHUNK, ns, 16 * P).transpose(1, 0, 2)
    return pr, pi, pr[:, ::-1], pi[:, ::-1]


def _s5_scan_fwd(proj, wb, wc, pr, pi, dskip):
    S = proj.shape[0]
    ns = wb.shape[0]
    W = wb.shape[2]
    hw = W // 2
    T = S5_CHUNK
    nc = S // T
    mix = ns * S5_SLAB

    def body(u_ref, wb_ref, wc_ref, pr_ref, pi_ref, d_ref, v_ref, yg_ref, h_ref, cin_ref, carry):
        c = pl.program_id(1)

        @pl.when(c == 0)
        def _():
            carry[...] = jnp.zeros_like(carry)

        u = u_ref[...]
        bu = _dot(u, wb_ref[...], NN)
        xr, xi = bu[:, :hw], bu[:, hw:]
        sub = lax.broadcasted_iota(jnp.int32, (T, hw), 0) & (S5_ROWS - 1)
        d = 1
        while d < S5_ROWS:
            ar, ai = pr_ref[pl.ds(d - 1, 1), :], pi_ref[pl.ds(d - 1, 1), :]
            sr = jnp.where(sub >= d, pltpu.roll(xr, d, 0), 0.0)
            si = jnp.where(sub >= d, pltpu.roll(xi, d, 0), 0.0)
            xr, xi = xr + ar * sr - ai * si, xi + ar * si + ai * sr
            d *= 2
        cin_ref[...] = carry[...]
        cr, ci = carry[:, :hw], carry[:, hw:]
        pwr, pwi = pr_ref[pl.ds(0, S5_ROWS), :], pi_ref[pl.ds(0, S5_ROWS), :]
        for g in range(T // S5_ROWS):
            rows = slice(g * S5_ROWS, (g + 1) * S5_ROWS)
            hr = xr[rows, :] + pwr * cr - pwi * ci
            hi = xi[rows, :] + pwr * ci + pwi * cr
            h_ref[rows, :hw] = hr
            h_ref[rows, hw:] = hi
            cr, ci = hr[S5_ROWS - 1:S5_ROWS, :], hi[S5_ROWS - 1:S5_ROWS, :]
        carry[:, :hw] = cr
        carry[:, hw:] = ci
        y = _dot(h_ref[...], wc_ref[...], NN)
        v = y + d_ref[...] * u
        v_ref[...] = v
        yg_ref[...] = _gelu(v)

    return pl.pallas_call(
        body, name="s5_scan_fwd", grid=(ns, nc),
        in_specs=[pl.BlockSpec((T, S5_SLAB), lambda s, c: (c, s)),
                  pl.BlockSpec((None, S5_SLAB, W), lambda s, c: (s, 0, 0)),
                  pl.BlockSpec((None, W, S5_SLAB), lambda s, c: (s, 0, 0)),
                  pl.BlockSpec((None, T, hw), lambda s, c: (s, 0, 0)),
                  pl.BlockSpec((None, T, hw), lambda s, c: (s, 0, 0)),
                  pl.BlockSpec((1, S5_SLAB), lambda s, c: (0, s))],
        out_specs=[pl.BlockSpec((T, S5_SLAB), lambda s, c: (c, s)),
                   pl.BlockSpec((T, S5_SLAB), lambda s, c: (c, s)),
                   pl.BlockSpec((T, W), lambda s, c: (c, s)),
                   pl.BlockSpec((None, 1, W), lambda s, c: (c, 0, s))],
        out_shape=[jax.ShapeDtypeStruct((S, mix), F32), jax.ShapeDtypeStruct((S, mix), F32),
                   jax.ShapeDtypeStruct((S, ns * W), F32), jax.ShapeDtypeStruct((nc, 1, ns * W), F32)],
        scratch_shapes=[pltpu.VMEM((1, W), F32)],
        compiler_params=_cp(("parallel", "arbitrary")),
    )(proj, wb, wc, pr, pi, dskip)


def _s5_scan_bwd(dv, proj, hs, cin, wb, wc, pr, pi, prr, pir, dskip):
    S = proj.shape[0]
    ns = wb.shape[0]
    W = wb.shape[2]
    hw = W // 2
    T = S5_CHUNK
    nc = S // T
    mix = ns * S5_SLAB

    def body(dv_ref, u_ref, h_ref, cin_ref, wb_ref, wc_ref, pr_ref, pi_ref, prr_ref, pir_ref, d_ref,
             du_ref, dwb_ref, dwc_ref, da_ref, dd_ref, lam_s, carry):
        c = pl.program_id(1)

        @pl.when(c == 0)
        def _():
            carry[...] = jnp.zeros_like(carry)
            dwb_ref[...] = jnp.zeros_like(dwb_ref)
            dwc_ref[...] = jnp.zeros_like(dwc_ref)
            da_ref[...] = jnp.zeros_like(da_ref)
            dd_ref[...] = jnp.zeros_like(dd_ref)

        dy, u = dv_ref[...], u_ref[...]
        dh = _dot(dy, wc_ref[...], NT)
        gr, gi = dh[:, :hw], dh[:, hw:]
        row = lax.broadcasted_iota(jnp.int32, (T, hw), 0)
        sub = row & (S5_ROWS - 1)
        d = 1
        while d < S5_ROWS:
            ar, ai = pr_ref[pl.ds(d - 1, 1), :], -pi_ref[pl.ds(d - 1, 1), :]
            sr = jnp.where(sub < S5_ROWS - d, pltpu.roll(gr, T - d, 0), 0.0)
            si = jnp.where(sub < S5_ROWS - d, pltpu.roll(gi, T - d, 0), 0.0)
            gr, gi = gr + ar * sr - ai * si, gi + ar * si + ai * sr
            d *= 2
        lr, li = carry[:, :hw], carry[:, hw:]
        pwr, pwi = prr_ref[pl.ds(T - S5_ROWS, S5_ROWS), :], -pir_ref[pl.ds(T - S5_ROWS, S5_ROWS), :]
        for g in reversed(range(T // S5_ROWS)):
            rows = slice(g * S5_ROWS, (g + 1) * S5_ROWS)
            lgr = gr[rows, :] + pwr * lr - pwi * li
            lgi = gi[rows, :] + pwr * li + pwi * lr
            lam_s[rows, :hw] = lgr
            lam_s[rows, hw:] = lgi
            lr, li = lgr[0:1, :], lgi[0:1, :]
        carry[:, :hw] = lr
        carry[:, hw:] = li
        gr, gi = lam_s[:, :hw], lam_s[:, hw:]
        hr, hi = h_ref[:, :hw], h_ref[:, hw:]
        hpr = jnp.where(row >= 1, pltpu.roll(hr, 1, 0), cin_ref[:, :hw])
        hpi = jnp.where(row >= 1, pltpu.roll(hi, 1, 0), cin_ref[:, hw:])
        da_ref[:, :hw] += jnp.sum(hpr * gr + hpi * gi, axis=0, keepdims=True)
        da_ref[:, hw:] += jnp.sum(hpr * gi - hpi * gr, axis=0, keepdims=True)
        lam = lam_s[...]
        du_ref[...] = _dot(lam, wb_ref[...], NT) + dy * d_ref[...]
        dwb_ref[...] += _dot(u, lam, TN)
        dwc_ref[...] += _dot(h_ref[...], dy, TN)
        dd_ref[...] += jnp.sum(dy * u, axis=0, keepdims=True)

    def rc(c):
        return nc - 1 - c

    return pl.pallas_call(
        body, name="s5_scan_bwd", grid=(ns, nc),
        in_specs=[pl.BlockSpec((T, S5_SLAB), lambda s, c: (rc(c), s)),
                  pl.BlockSpec((T, S5_SLAB), lambda s, c: (rc(c), s)),
                  pl.BlockSpec((T, W), lambda s, c: (rc(c), s)),
                  pl.BlockSpec((None, 1, W), lambda s, c: (rc(c), 0, s)),
                  pl.BlockSpec((None, S5_SLAB, W), lambda s, c: (s, 0, 0)),
                  pl.BlockSpec((None, W, S5_SLAB), lambda s, c: (s, 0, 0)),
                  pl.BlockSpec((None, T, hw), lambda s, c: (s, 0, 0)),
                  pl.BlockSpec((None, T, hw), lambda s, c: (s, 0, 0)),
                  pl.BlockSpec((None, T, hw), lambda s, c: (s, 0, 0)),
                  pl.BlockSpec((None, T, hw), lambda s, c: (s, 0, 0)),
                  pl.BlockSpec((1, S5_SLAB), lambda s, c: (0, s))],
        out_specs=[pl.BlockSpec((T, S5_SLAB), lambda s, c: (rc(c), s)),
                   pl.BlockSpec((None, S5_SLAB, W), lambda s, c: (s, 0, 0)),
                   pl.BlockSpec((None, W, S5_SLAB), lambda s, c: (s, 0, 0)),
                   pl.BlockSpec((None, 1, W), lambda s, c: (s, 0, 0)),
                   pl.BlockSpec((1, S5_SLAB), lambda s, c: (0, s))],
        out_shape=[jax.ShapeDtypeStruct((S, mix), F32), jax.ShapeDtypeStruct(wb.shape, F32),
                   jax.ShapeDtypeStruct(wc.shape, F32), jax.ShapeDtypeStruct((ns, 1, W), F32),
                   jax.ShapeDtypeStruct((1, mix), F32)],
        scratch_shapes=[pltpu.VMEM((T, W), F32), pltpu.VMEM((1, W), F32)],
        compiler_params=_cp(("parallel", "arbitrary")),
    )(dv, proj, hs, cin, wb, wc, pr, pi, prr, pir, dskip)


def _s5_glu_bwd(dcat, yg, z):
    S, mix = yg.shape
    tr = _tile(S, ROW_TILE, 8)

    def body(do_ref, yg_ref, z_ref, dz_ref, dy_ref, db_ref):
        i = pl.program_id(0)

        @pl.when(i == 0)
        def _():
            db_ref[...] = jnp.zeros_like(db_ref)

        do, yg_, sz = do_ref[...], yg_ref[...], _sigmoid(z_ref[...])
        dz = do * yg_ * sz * (1.0 - sz)
        dz_ref[...] = dz
        dy_ref[...] = do * sz
        db_ref[...] += jnp.sum(dz, axis=0, keepdims=True)

    blk = pl.BlockSpec((tr, mix), lambda i: (i, 0))
    return pl.pallas_call(
        body, name="s5_glu_bwd", grid=(S // tr,),
        in_specs=[blk, blk, blk], out_specs=[blk, blk, pl.BlockSpec((1, mix), lambda i: (0, 0))],
        out_shape=[jax.ShapeDtypeStruct((S, mix), F32), jax.ShapeDtypeStruct((S, mix), F32),
                   jax.ShapeDtypeStruct((1, mix), F32)],
        compiler_params=_cp(("arbitrary",)),
    )(dcat, yg, z)


def _rows_down(x, j):
    return x if j == 0 else pltpu.roll(x, j, 0)


def _conv_rows(xe, w_ref, n):
    c = None
    for j in range(GDN_CONV):
        term = w_ref[pl.ds(GDN_CONV - 1 - j, 1), :] * _rows_down(xe, j)[8:8 + n, :]
        c = term if c is None else c + term
    return c


def _gdn_prep(proj, blk0, nblk, convw, norm, scale, name):
    S = proj.shape[0]
    tr = _tile(S, CONV_TILE, 8)
    nb8 = tr // 8

    def body(x_ref, xb_ref, w_ref, o_ref):
        i = pl.program_id(1)
        xe = jnp.concatenate([jnp.where(i == 0, 0.0, xb_ref[...]), x_ref[...]], axis=0)
        c = _conv_rows(xe, w_ref, tr)
        s = c * _sigmoid(c)
        if norm:
            s = s * lax.rsqrt(jnp.sum(s * s, axis=-1, keepdims=True) + EPS) * scale
        o_ref[...] = s

    return pl.pallas_call(
        body, name=name, grid=(nblk, S // tr),
        in_specs=[pl.BlockSpec((tr, HEAD), lambda j, i: (i, blk0 + j)),
                  pl.BlockSpec((8, HEAD), lambda j, i: (jnp.maximum(i * nb8 - 1, 0), blk0 + j)),
                  pl.BlockSpec((GDN_CONV, HEAD), lambda j, i: (0, j))],
        out_specs=pl.BlockSpec((tr, HEAD), lambda j, i: (i, j)),
        out_shape=jax.ShapeDtypeStruct((S, nblk * HEAD), F32),
        compiler_params=_cp(("parallel", "parallel")),
    )(proj, proj, convw)


def _gdn_prep_bwd(proj, blk0, nblk, convw, dout, norm, scale, name):
    S = proj.shape[0]
    tr = _tile(S, CONV_TILE, 8)
    nb8 = tr // 8
    last8 = S // 8 - 1
    nrow = S // tr

    def body(x_ref, xb_ref, xa_ref, w_ref, d_ref, da_ref, dx_ref, dw_ref):
        i = pl.program_id(1)

        @pl.when(i == 0)
        def _():
            dw_ref[...] = jnp.zeros_like(dw_ref)

        xe = jnp.concatenate([jnp.where(i == 0, 0.0, xb_ref[...]), x_ref[...], xa_ref[...]], axis=0)
        de = jnp.concatenate([d_ref[...], da_ref[...]], axis=0)
        n = tr + 8
        c = _conv_rows(xe, w_ref, n)
        sg = _sigmoid(c)
        s = c * sg
        if norm:
            r = lax.rsqrt(jnp.sum(s * s, axis=-1, keepdims=True) + EPS)
            ds = scale * r * (de - s * (r * r) * jnp.sum(de * s, axis=-1, keepdims=True))
        else:
            ds = de
        dc = ds * (sg + c * sg * (1.0 - sg))
        rowi = lax.broadcasted_iota(jnp.int32, (n, HEAD), 0)
        dc = jnp.where((i == nrow - 1) & (rowi >= tr), 0.0, dc)
        dct = dc[:tr, :]
        dx = None
        for j in range(GDN_CONV):
            tap = pl.ds(GDN_CONV - 1 - j, 1)
            up = dct if j == 0 else pltpu.roll(dc, n - j, 0)[:tr, :]
            term = w_ref[tap, :] * up
            dx = term if dx is None else dx + term
            dw_ref[tap, :] += jnp.sum(dct * _rows_down(xe, j)[8:8 + tr, :], axis=0, keepdims=True)
        dx_ref[...] = dx

    return pl.pallas_call(
        body, name=name, grid=(nblk, nrow),
        in_specs=[pl.BlockSpec((tr, HEAD), lambda j, i: (i, blk0 + j)),
                  pl.BlockSpec((8, HEAD), lambda j, i: (jnp.maximum(i * nb8 - 1, 0), blk0 + j)),
                  pl.BlockSpec((8, HEAD), lambda j, i: (jnp.minimum((i + 1) * nb8, last8), blk0 + j)),
                  pl.BlockSpec((GDN_CONV, HEAD), lambda j, i: (0, j)),
                  pl.BlockSpec((tr, HEAD), lambda j, i: (i, j)),
                  pl.BlockSpec((8, HEAD), lambda j, i: (jnp.minimum((i + 1) * nb8, last8), j))],
        out_specs=[pl.BlockSpec((tr, HEAD), lambda j, i: (i, j)),
                   pl.BlockSpec((GDN_CONV, HEAD), lambda j, i: (0, j))],
        out_shape=[jax.ShapeDtypeStruct((S, nblk * HEAD), F32),
                   jax.ShapeDtypeStruct((GDN_CONV, nblk * HEAD), F32)],
        compiler_params=_cp(("parallel", "arbitrary")),
    )(proj, proj, proj, convw, dout, dout)


def _gdn_gates(pg, alog, dtb):
    S = pg.shape[0]

    def body(a_ref, b_ref, al_ref, dt_ref, gc_ref, be_ref):
        g = -jnp.exp(al_ref[...]) * _softplus(a_ref[...] + dt_ref[...])
        rowm = lax.broadcasted_iota(jnp.int32, g.shape, 0) & (GDN_CHUNK - 1)
        c = g
        d = 1
        while d < GDN_CHUNK:
            c = c + jnp.where(rowm >= d, pltpu.roll(c, d, 0), 0.0)
            d *= 2
        gc_ref[...] = c
        be_ref[...] = _sigmoid(b_ref[...])

    blk = pl.BlockSpec((S, 128), lambda i: (0, 0))
    vec = pl.BlockSpec((1, 128), lambda i: (0, 0))
    return pl.pallas_call(
        body, name="gdn_gates", grid=(1,),
        in_specs=[blk, pl.BlockSpec((S, 128), lambda i: (0, 1)), vec, vec],
        out_specs=[blk, blk],
        out_shape=[jax.ShapeDtypeStruct((S, 128), F32)] * 2,
        compiler_params=_cp(("arbitrary",)),
    )(pg, pg, alog, dtb)


def _gdn_gates_bwd(pg, alog, dtb, dgc, dbeta):
    S = pg.shape[0]

    def body(a_ref, b_ref, al_ref, dt_ref, dgc_ref, dbe_ref, dpa_ref, dpb_ref, dal_ref, ddt_ref):
        rowm = lax.broadcasted_iota(jnp.int32, (S, 128), 0) & (GDN_CHUNK - 1)
        c = dgc_ref[...]
        d = 1
        while d < GDN_CHUNK:
            c = c + jnp.where(rowm < GDN_CHUNK - d, pltpu.roll(c, S - d, 0), 0.0)
            d *= 2
        xv = a_ref[...] + dt_ref[...]
        ea = jnp.exp(al_ref[...])
        g = -ea * _softplus(xv)
        dx = c * (-ea) * _sigmoid(xv)
        dpa_ref[...] = dx
        dal_ref[...] = jnp.sum(c * g, axis=0, keepdims=True)
        ddt_ref[...] = jnp.sum(dx, axis=0, keepdims=True)
        be = _sigmoid(b_ref[...])
        dpb_ref[...] = dbe_ref[...] * be * (1.0 - be)

    blk = pl.BlockSpec((S, 128), lambda i: (0, 0))
    blk1 = pl.BlockSpec((S, 128), lambda i: (0, 1))
    vec = pl.BlockSpec((1, 128), lambda i: (0, 0))
    dpa, dpb, dal, ddt = pl.pallas_call(
        body, name="gdn_gates_bwd", grid=(1,),
        in_specs=[blk, blk1, vec, vec, blk, blk],
        out_specs=[blk, blk, vec, vec],
        out_shape=[jax.ShapeDtypeStruct((S, 128), F32)] * 2 + [jax.ShapeDtypeStruct((1, 128), F32)] * 2,
        compiler_params=_cp(("arbitrary",)),
    )(pg, pg, alog, dtb, dgc, dbeta)
    return jnp.concatenate([dpa, dpb], axis=1), dal, ddt


def _gdn_pre(qs, ks, vs, gcs, grs, betas):
    C = GDN_CHUNK
    n = len(qs)
    r = lax.broadcasted_iota(jnp.int32, (C, C), 0)
    c_ = lax.broadcasted_iota(jnp.int32, (C, C), 1)
    lower, strict = r >= c_, r > c_
    eye = jnp.where(r == c_, 1.0, 0.0)
    decs = [jnp.exp(jnp.where(lower, gcs[i] - grs[i], -jnp.inf)) for i in range(n)]
    kbs = [ks[i] * betas[i] for i in range(n)]
    vbs = [vs[i] * betas[i] for i in range(n)]
    lmats = [jnp.where(strict, _dot(kbs[i], ks[i], NT) * decs[i], 0.0) for i in range(n)]
    amats = [jnp.where(lower, _dot(qs[i], ks[i], NT) * decs[i], 0.0) for i in range(n)]
    pks = [-lm for lm in lmats]
    tinvs = [eye + pk for pk in pks]
    for _ in range(5):
        pks = [_dotf(pk, pk, NN) for pk in pks]
        tinvs = [tv + _dotf(tv, pk, NN) for tv, pk in zip(tinvs, pks)]
    es = [jnp.exp(gc) for gc in gcs]
    glasts = [gc[C - 1:C, :] for gc in gcs]
    fs = [jnp.exp(gl - gc) for gl, gc in zip(glasts, gcs)]
    gls = [jnp.exp(gl) for gl in glasts]
    us = [_dotf(tinvs[i], vbs[i], NN) for i in range(n)]
    ws = [_dotf(tinvs[i], kbs[i] * es[i], NN) for i in range(n)]
    return [dict(lower=lower, strict=strict, dec=decs[i], kb=kbs[i], vb=vbs[i], lmat=lmats[i], tinv=tinvs[i],
                 e=es[i], f=fs[i], gl=gls[i], u=us[i], w=ws[i], amat=amats[i], qd=qs[i] * es[i],
                 kd=ks[i] * fs[i]) for i in range(n)]


def _gdn_heads_per_step(H):
    return max(d for d in (1, 2, 3, 4) if H % d == 0)


def _gdn_chunk_fwd(q, k, v, gcol, grow, bcol):
    S = q.shape[0]
    H, NC = gcol.shape[0], gcol.shape[1]
    C = GDN_CHUNK
    hb = _gdn_heads_per_step(H)

    def body(q_ref, k_ref, v_ref, gc_ref, gr_ref, b_ref, o_ref, st_ref, state):
        n = pl.program_id(1)

        @pl.when(n == 0)
        def _():
            state[...] = jnp.zeros_like(state)

        cols = [slice(i * HEAD, (i + 1) * HEAD) for i in range(hb)]
        ps = _gdn_pre([q_ref[:, c] for c in cols], [k_ref[:, c] for c in cols], [v_ref[:, c] for c in cols],
                      [gc_ref[i] for i in range(hb)], [gr_ref[i] for i in range(hb)],
                      [b_ref[i] for i in range(hb)])
        s0s = [state[i] for i in range(hb)]
        vns = [ps[i]['u'] - _dot(ps[i]['w'], s0s[i], NN) for i in range(hb)]
        outs = [_dot(ps[i]['qd'], s0s[i], NN) + _dot(ps[i]['amat'], vns[i], NN) for i in range(hb)]
        news = [s0s[i] * ps[i]['gl'] + _dot(ps[i]['kd'], vns[i], TN) for i in range(hb)]
        for i in range(hb):
            st_ref[i] = s0s[i]
            o_ref[:, cols[i]] = outs[i]
            state[i] = news[i]

    tok = pl.BlockSpec((C, hb * HEAD), lambda h, n: (n, h))
    col = pl.BlockSpec((hb, None, C, 1), lambda h, n: (h, n, 0, 0))
    rowb = pl.BlockSpec((hb, None, 1, C), lambda h, n: (h, n, 0, 0))
    return pl.pallas_call(
        body, name="gdn_chunk_fwd", grid=(H // hb, NC),
        in_specs=[tok, tok, tok, col, rowb, col],
        out_specs=[tok, pl.BlockSpec((hb, None, HEAD, HEAD), lambda h, n: (h, n, 0, 0))],
        out_shape=[jax.ShapeDtypeStruct((S, H * HEAD), F32), jax.ShapeDtypeStruct((H, NC, HEAD, HEAD), F32)],
        scratch_shapes=[pltpu.VMEM((hb, HEAD, HEAD), F32)],
        compiler_params=_cp(("parallel", "arbitrary")),
    )(q, k, v, gcol, grow, bcol)


def _gdn_chunk_bwd(q, k, v, gcol, grow, bcol, st, do):
    S = q.shape[0]
    H, NC = gcol.shape[0], gcol.shape[1]
    C = GDN_CHUNK
    hb = _gdn_heads_per_step(H)

    def body(q_ref, k_ref, v_ref, gc_ref, gr_ref, b_ref, st_ref, do_ref,
             dq_ref, dk_ref, dv_ref, dgc_ref, dbe_ref, dstate):
        n = pl.program_id(1)

        @pl.when(n == 0)
        def _():
            dstate[...] = jnp.zeros_like(dstate)

        R = range(hb)
        cols = [slice(i * HEAD, (i + 1) * HEAD) for i in R]
        qs, ks, vs = [q_ref[:, c] for c in cols], [k_ref[:, c] for c in cols], [v_ref[:, c] for c in cols]
        betas = [b_ref[i] for i in R]
        ps = _gdn_pre(qs, ks, vs, [gc_ref[i] for i in R], [gr_ref[i] for i in R], betas)
        lower, strict = ps[0]['lower'], ps[0]['strict']
        s0s, dos, ds1s = [st_ref[i] for i in R], [do_ref[:, c] for c in cols], [dstate[i] for i in R]
        vns = [ps[i]['u'] - _dot(ps[i]['w'], s0s[i], NN) for i in R]
        dvns = [_dot(ps[i]['amat'], dos[i], TN) + _dot(ps[i]['kd'], ds1s[i], NN) for i in R]
        damats = [jnp.where(lower, _dot(dos[i], vns[i], NT), 0.0) for i in R]
        dqds = [_dot(dos[i], s0s[i], NT) for i in R]
        dkds = [_dot(vns[i], ds1s[i], NT) for i in R]
        dgls = [jnp.sum(s0s[i] * ds1s[i], keepdims=True) for i in R]
        ds0s = [ps[i]['gl'] * ds1s[i] + _dot(ps[i]['qd'], dos[i], TN) - _dot(ps[i]['w'], dvns[i], TN) for i in R]
        dws = [-_dot(dvns[i], s0s[i], NT) for i in R]
        dvbs = [_dotf(ps[i]['tinv'], dvns[i], TN) for i in R]
        dkgs = [_dotf(ps[i]['tinv'], dws[i], TN) for i in R]
        dls = [-jnp.where(strict, _dotf(dvbs[i], ps[i]['u'], NT) + _dotf(dkgs[i], ps[i]['w'], NT), 0.0) for i in R]
        dkks = [dls[i] * ps[i]['dec'] for i in R]
        dqks = [damats[i] * ps[i]['dec'] for i in R]
        ms = [dls[i] * ps[i]['lmat'] + damats[i] * ps[i]['amat'] for i in R]
        dkbs = [_dot(dkks[i], ks[i], NN) + dkgs[i] * ps[i]['e'] for i in R]
        dks = [_dot(dkks[i], ps[i]['kb'], TN) + _dot(dqks[i], qs[i], TN) + dkds[i] * ps[i]['f'] + dkbs[i] * betas[i]
               for i in R]
        dqs = [_dot(dqks[i], ks[i], NN) + dqds[i] * ps[i]['e'] for i in R]
        ones = jnp.ones((C, HEAD), F32)
        colsums = [_dotf(ms[i], ones, TN)[:, 0:1] for i in R]
        rowi = lax.broadcasted_iota(jnp.int32, (C, 1), 0)
        for i in R:
            p = ps[i]
            de = (jnp.sum(dkgs[i] * p['kb'], axis=-1, keepdims=True)
                  + jnp.sum(dqds[i] * qs[i], axis=-1, keepdims=True))
            df = jnp.sum(dkds[i] * ks[i], axis=-1, keepdims=True)
            dgc = jnp.sum(ms[i], axis=-1, keepdims=True) - colsums[i] + de * p['e'] - df * p['f']
            dlast = jnp.sum(df * p['f'], keepdims=True) + dgls[i] * p['gl']
            dgc_ref[i] = dgc + jnp.where(rowi == C - 1, dlast, 0.0)
            dbe_ref[i] = (jnp.sum(dkbs[i] * ks[i], axis=-1, keepdims=True)
                          + jnp.sum(dvbs[i] * vs[i], axis=-1, keepdims=True))
            dstate[i] = ds0s[i]
            dq_ref[:, cols[i]] = dqs[i]
            dk_ref[:, cols[i]] = dks[i]
            dv_ref[:, cols[i]] = dvbs[i] * betas[i]

    def rn(n):
        return NC - 1 - n

    tok = pl.BlockSpec((C, hb * HEAD), lambda h, n: (rn(n), h))
    col = pl.BlockSpec((hb, None, C, 1), lambda h, n: (h, rn(n), 0, 0))
    rowb = pl.BlockSpec((hb, None, 1, C), lambda h, n: (h, rn(n), 0, 0))
    return pl.pallas_call(
        body, name="gdn_chunk_bwd", grid=(H // hb, NC),
        in_specs=[tok, tok, tok, col, rowb, col,
                  pl.BlockSpec((hb, None, HEAD, HEAD), lambda h, n: (h, rn(n), 0, 0)), tok],
        out_specs=[tok, tok, tok, col, col],
        out_shape=[jax.ShapeDtypeStruct((S, H * HEAD), F32)] * 3
        + [jax.ShapeDtypeStruct((H, NC, C, 1), F32)] * 2,
        scratch_shapes=[pltpu.VMEM((hb, HEAD, HEAD), F32)],
        compiler_params=_cp(("parallel", "arbitrary")),
    )(q, k, v, gcol, grow, bcol, st, do)


def _gdn_onorm(o, proj, gate_blk, w, H):
    S = o.shape[0]
    tr = _tile(S, CONV_TILE, 8)

    def body(o_ref, g_ref, w_ref, out_ref):
        ov, gv = o_ref[...], g_ref[...]
        r = lax.rsqrt(jnp.mean(ov * ov, axis=-1, keepdims=True) + EPS)
        out_ref[...] = (ov * r * w_ref[...]) * (gv * _sigmoid(gv))

    return pl.pallas_call(
        body, name="gdn_onorm", grid=(S // tr, H),
        in_specs=[pl.BlockSpec((tr, HEAD), lambda i, h: (i, h)),
                  pl.BlockSpec((tr, HEAD), lambda i, h: (i, gate_blk + h)),
                  pl.BlockSpec((1, HEAD), lambda i, h: (0, 0))],
        out_specs=pl.BlockSpec((tr, HEAD), lambda i, h: (i, h)),
        out_shape=jax.ShapeDtypeStruct((S, H * HEAD), F32),
        compiler_params=_cp(("parallel", "parallel")),
    )(o, proj, w)


def _gdn_onorm_bwd(dcat, o, proj, gate_blk, w, H):
    S = o.shape[0]
    tr = _tile(S, CONV_TILE, 8)

    def body(d_ref, o_ref, g_ref, w_ref, do_ref, dg_ref, dw_ref):
        i, h = pl.program_id(0), pl.program_id(1)

        @pl.when((i == 0) & (h == 0))
        def _():
            dw_ref[...] = jnp.zeros_like(dw_ref)

        dm, ov, gv, wv = d_ref[...], o_ref[...], g_ref[...], w_ref[...]
        r = lax.rsqrt(jnp.mean(ov * ov, axis=-1, keepdims=True) + EPS)
        oh = ov * r
        sg = gv * _sigmoid(gv)
        dy = dm * sg
        t = dy * wv
        do_ref[...] = r * (t - oh * jnp.mean(t * oh, axis=-1, keepdims=True))
        dg_ref[...] = dm * (oh * wv) * _silu_grad(gv)
        dw_ref[...] += jnp.sum(dy * oh, axis=0, keepdims=True)

    tok = pl.BlockSpec((tr, HEAD), lambda i, h: (i, h))
    vec = pl.BlockSpec((1, HEAD), lambda i, h: (0, 0))
    return pl.pallas_call(
        body, name="gdn_onorm_bwd", grid=(S // tr, H),
        in_specs=[tok, tok, pl.BlockSpec((tr, HEAD), lambda i, h: (i, gate_blk + h)), vec],
        out_specs=[tok, tok, vec],
        out_shape=[jax.ShapeDtypeStruct((S, H * HEAD), F32)] * 2 + [jax.ShapeDtypeStruct((1, HEAD), F32)],
        compiler_params=_cp(("arbitrary", "arbitrary")),
    )(dcat, o, proj, w)


def _lanes_to_heads(a, H):
    return a[:, :H].T


def _heads_to_lanes(a):
    H = a.shape[0]
    return jnp.pad(a.T, ((0, 0), (0, 128 - H)))


def _take_cols(segs, a, b):
    out, off = [], 0
    for sg in segs:
        w = sg.shape[-1]
        lo, hi = max(a, off), min(b, off + w)
        if lo < hi:
            out.append(sg[..., lo - off:hi - off])
        off += w
    return out


def _pad_cols(pieces):
    m = jnp.concatenate(pieces, axis=-1)
    return jnp.pad(m, ((0, 0), (0, 128 - m.shape[-1])))


def _pad_lanes(v):
    return jnp.pad(v.reshape(1, -1), ((0, 0), (0, 128 - v.shape[-1])))


def _s5_layer_fwd(a, w, cfg):
    proj = _mm(a, w['w_in'], name="s5_in")
    wb, wc, _ = w['prep']
    pr, pi, prr, pir = w['tables']
    v, yg, hs, cin = _s5_scan_fwd(proj, wb, wc, pr, pi, w['d_skip'])
    z, mix = _mm(yg, w['w_glu'], name="s5_glu", extras=[(yg, 'ij'), (w['b_glu'], 'j')],
                 epi=lambda acc, y, b: (acc + b, y * _sigmoid(acc + b)), out_dtypes=(F32, F32))
    return proj, mix, dict(v=v, yg=yg, hs=hs, cin=cin, z=z)


def _s5_layer_bwd(a, w, proj, sv, dcat, dmemq, cfg):
    wb, wc, _ = w['prep']
    pr, pi, prr, pir = w['tables']
    dz, dyg1, db_glu = _s5_glu_bwd(dcat, sv['yg'], sv['z'])
    dw_glu = _mm(sv['yg'], dz, name="s5_dwglu", ta=True)
    dv = _mm(dz, w['w_glu'], name="s5_dyg", tb=True, extras=[(dyg1, 'ij'), (sv['v'], 'ij')],
             epi=lambda acc, d1, vv: ((acc + d1) * _gelu_grad(vv),))
    du, dwb, dwc, da, dd = _s5_scan_bwd(dv, proj, sv['hs'], sv['cin'], wb, wc, pr, pi, prr, pir, w['d_skip'])
    dproj = jnp.concatenate([du, dmemq], axis=1)
    dw_in = _mm(a, dproj, name="s5_dwin", ta=True)
    da_in = _mm(dproj, w['w_in'], name="s5_da", tb=True)
    dlre, dlim, dldt, dbre, dbim, dcre, dcim = w['prep_vjp']((dwb, dwc, da))
    grads = dict(w_in=dw_in, w_glu=dw_glu, b_glu=db_glu[0], d_skip=dd[0], lam_re=dlre, lam_im=dlim,
                 log_dt=dldt, b_re=dbre, b_im=dbim, c_re=dcre, c_im=dcim)
    return da_in, grads


def _gdn_relayout(a, H, NC):
    t = _lanes_to_heads(a, H).reshape(H, NC, GDN_CHUNK)
    return t[..., None], t[:, :, None, :]


def _gdn_layer_fwd(a, w, cfg):
    H, MIX, S = cfg['H'], cfg['MIX'], a.shape[0]
    NC = S // GDN_CHUNK
    proj = _mm(a, w['w_main'], name="gdn_in")
    pg = _mm(a, w['w_gate'], name="gdn_in_gates")
    cw = w['conv_w']
    q = _gdn_prep(proj, 0, H, cw[:, :MIX], True, HEAD ** -0.5, "gdn_prep_q")
    k = _gdn_prep(proj, H, H, cw[:, MIX:2 * MIX], True, 1.0, "gdn_prep_k")
    v = _gdn_prep(proj, 2 * H, H, cw[:, 2 * MIX:], False, 1.0, "gdn_prep_v")
    gc, beta = _gdn_gates(pg, w['a_log'], w['dt_bias'])
    gcol, grow = _gdn_relayout(gc, H, NC)
    bcol, _ = _gdn_relayout(beta, H, NC)
    o, st = _gdn_chunk_fwd(q, k, v, gcol, grow, bcol)
    mix = _gdn_onorm(o, proj, 3 * H, w['o_norm'], H)
    return proj, mix, dict(pg=pg, q=q, k=k, v=v, gcol=gcol, grow=grow, bcol=bcol, o=o, st=st)


def _gdn_layer_bwd(a, w, proj, sv, dcat, dmemq, cfg):
    H, MIX, S = cfg['H'], cfg['MIX'], a.shape[0]
    cw = w['conv_w']
    do, dgate, donorm = _gdn_onorm_bwd(dcat, sv['o'], proj, 3 * H, w['o_norm'], H)
    dq, dk, dv, dgcol, dbcol = _gdn_chunk_bwd(sv['q'], sv['k'], sv['v'], sv['gcol'], sv['grow'], sv['bcol'],
                                              sv['st'], do)
    dgc = _heads_to_lanes(dgcol.reshape(H, S))
    dbeta = _heads_to_lanes(dbcol.reshape(H, S))
    dpg, dalog, ddtb = _gdn_gates_bwd(sv['pg'], w['a_log'], w['dt_bias'], dgc, dbeta)
    dxq, dwq = _gdn_prep_bwd(proj, 0, H, cw[:, :MIX], dq, True, HEAD ** -0.5, "gdn_prep_bwd_q")
    dxk, dwk = _gdn_prep_bwd(proj, H, H, cw[:, MIX:2 * MIX], dk, True, 1.0, "gdn_prep_bwd_k")
    dxv, dwv = _gdn_prep_bwd(proj, 2 * H, H, cw[:, 2 * MIX:], dv, False, 1.0, "gdn_prep_bwd_v")
    dproj = jnp.concatenate([dxq, dxk, dxv, dgate, dmemq], axis=1)
    dw_main = _mm(a, dproj, name="gdn_dwmain", ta=True)
    dw_gate = _mm(a, dpg, name="gdn_dwgate", ta=True)
    da1 = _mm(dpg, w['w_gate'], name="gdn_da_gates", tb=True)
    da_in = _mm(dproj, w['w_main'], name="gdn_da", tb=True, extras=[(da1, 'ij')], epi=lambda acc, e: (acc + e,))
    grads = dict(w_main=dw_main, w_gate=dw_gate, conv_w=jnp.concatenate([dwq, dwk, dwv], axis=1),
                 a_log=dalog[0, :H], dt_bias=ddtb[0, :H], o_norm=donorm[0])
    return da_in, grads


def _fox_layer_fwd(a, w, cfg):
    H = cfg['H']
    proj = _mm(a, w['w_main'], name="fox_in")
    pg = _mm(a, w['w_gate'], name="fox_in_gates")
    cf = _fox_gates(pg, w['b_f'])
    cfh = _lanes_to_heads(cf, H)
    cfq, cfk = cfh[:, :, None], cfh[:, None, :]
    o, lse = _fox_fwd(proj, cfq, cfk, H)
    return proj, o, dict(pg=pg, cfq=cfq, cfk=cfk, lse=lse)


def _fox_layer_bwd(a, w, proj, sv, dcat, dmemq, cfg):
    H = cfg['H']
    rowdot = _fox_bwd_rowdot(proj, sv['cfq'], sv['cfk'], sv['lse'], dcat, H)
    dq, dk, dv, dck = _fox_bwd(proj, sv['cfq'], sv['cfk'], rowdot, sv['lse'], dcat, H)
    dpg, dbf = _fox_gates_bwd(sv['pg'], w['b_f'], _heads_to_lanes(dck[:, 0, :]))
    dproj = jnp.concatenate([dq, dk, dv, dmemq], axis=1)
    dw_main = _mm(a, dproj, name="fox_dwmain", ta=True)
    dw_gate = _mm(a, dpg, name="fox_dwgate", ta=True)
    da1 = _mm(dpg, w['w_gate'], name="fox_da_gates", tb=True)
    da_in = _mm(dproj, w['w_main'], name="fox_da", tb=True, extras=[(da1, 'ij')], epi=lambda acc, e: (acc + e,))
    grads = dict(w_main=dw_main, w_gate=dw_gate, b_f=dbf[0, :H])
    return da_in, grads


_LAYER_FWD = (_s5_layer_fwd, _gdn_layer_fwd, _fox_layer_fwd)
_LAYER_BWD = (_s5_layer_bwd, _gdn_layer_bwd, _fox_layer_bwd)


def _mixer_weights(kind, j, fw, p, cfg):
    H, MIX, MW = cfg['H'], cfg['MIX'], cfg['MW']
    if kind == 0:
        params = tuple(p[n][j] for n in ('s5_lam_re', 's5_lam_im', 's5_log_dt', 's5_b_re', 's5_b_im',
                                         's5_c_re', 's5_c_im'))
        prep, prep_vjp = jax.vjp(_s5_prep, *params)
        prep = (prep[0].astype(MXU_DTYPE), prep[1].astype(MXU_DTYPE), prep[2])
        tables = _s5_tables(*params[:3])
        return dict(w_in=fw['s5_w_in'][j], w_glu=fw['s5_w_glu'][j], b_glu=fw['s5_b_glu'][j],
                    d_skip=fw['s5_d_skip'][j].reshape(1, MIX), prep=prep, prep_vjp=prep_vjp, tables=tables)
    if kind == 1:
        segs = fw['gdn_w_in'][j]
        c0 = 4 * MIX
        total = c0 + 2 * H + MW
        w_main = jnp.concatenate(_take_cols(segs, 0, c0) + _take_cols(segs, c0 + 2 * H, total), axis=1)
        w_gate = jnp.concatenate([_pad_cols(_take_cols(segs, c0, c0 + H)),
                                  _pad_cols(_take_cols(segs, c0 + H, c0 + 2 * H))], axis=1)
        return dict(w_main=w_main, w_gate=w_gate, conv_w=fw['gdn_conv_w'][j],
                    a_log=_pad_lanes(p['gdn_a_log'][j]), dt_bias=_pad_lanes(p['gdn_dt_bias'][j]),
                    o_norm=p['gdn_o_norm'][j].reshape(1, HEAD))
    segs = fw['fox_w_in'][j]
    c0 = 3 * MIX
    total = c0 + H + MW
    w_main = jnp.concatenate(_take_cols(segs, 0, c0) + _take_cols(segs, c0 + H, total), axis=1)
    w_gate = _pad_cols(_take_cols(segs, c0, c0 + H))
    return dict(w_main=w_main, w_gate=w_gate, b_f=_pad_lanes(p['fox_b_f'][j]))


def _local_step(p, fw, cfg):
    H, MIX, MW, MH, depth = cfg['H'], cfg['MIX'], cfg['MW'], cfg['MH'], cfg['depth']
    x, mem, target = p['x'], p['mem'], p['loss_target']
    q_blk = {0: MIX // HEAD, 1: 4 * MIX // HEAD, 2: 3 * MIX // HEAD}

    mem_n = _rms_fwd(mem, p['mem_norm'], MXU_DTYPE, "mem_rms")
    mkv = _mm(mem_n, fw['w_mem_kv'], name="mem_kv")

    h = x
    saved = []
    for i in range(depth):
        kind, j = i % 3, i // 3
        w = _mixer_weights(kind, j, fw, p, cfg)
        a = _rms_fwd(h, p['norm1'][i], MXU_DTYPE, "rms1")
        proj, mix, sv = _LAYER_FWD[kind](a, w, cfg)
        read = _mem_fwd(proj, q_blk[kind], mkv, MH)
        cat = jnp.concatenate([mix, read], axis=1)
        h1 = _mm(cat, fw['w_out'][i], name="out_proj", extras=[(h, 'ij')], epi=lambda acc, r: (acc + r,))
        a2 = _rms_fwd(h1, p['norm2'][i], MXU_DTYPE, "rms2")
        u = _mm(a2, fw['w_up'][i], name="mlp_up")
        h2 = _mm(u, fw['w_down'][i], name="mlp_down", a_pro=_relu2, extras=[(h1, 'ij')],
                 epi=lambda acc, r: (acc + r,))
        saved.append(dict(w=w, h=h, a=a, proj=proj, sv=sv, cat=cat, h1=h1, a2=a2, u=u))
        h = h2

    loss, dh, dnorm_f = _loss_head(h, p['norm_f'], target)

    g = {n: None for n in WEIGHTS}
    g['norm_f'] = dnorm_f[0]
    per_layer = {n: [None] * depth for n in ('norm1', 'norm2', 'w_out', 'w_up', 'w_down')}
    mix_grads = {0: {}, 1: {}, 2: {}}
    dmkv = None
    for i in reversed(range(depth)):
        kind, j = i % 3, i // 3
        s = saved[i]
        w = s['w']
        du = _mm(dh, fw['w_down'][i], name="mlp_ddown", tb=True, extras=[(s['u'], 'ij')],
                 epi=lambda acc, uu: (acc * (2.0 * jnp.maximum(uu, 0.0)),))
        per_layer['w_down'][i] = _mm(s['u'], dh, name="mlp_dwdown", ta=True, a_pro=_relu2)
        per_layer['w_up'][i] = _mm(s['a2'], du, name="mlp_dwup", ta=True)
        da2 = _mm(du, fw['w_up'][i], name="mlp_dup", tb=True)
        dh1, dn2 = _rms_bwd(s['h1'], p['norm2'][i], da2, dh, "rms2_bwd")
        per_layer['norm2'][i] = dn2[0]
        dcat = _mm(dh1, fw['w_out'][i], name="out_dproj", tb=True)
        per_layer['w_out'][i] = _mm(s['cat'], dh1, name="out_dw", ta=True)
        dmemq, dmkv_i = _mem_bwd(s['proj'], q_blk[kind], mkv, dcat, MIX // HEAD, MH)
        dmkv = dmkv_i if dmkv is None else dmkv + dmkv_i
        da, mg = _LAYER_BWD[kind](s['a'], w, s['proj'], s['sv'], dcat, dmemq, cfg)
        mix_grads[kind][j] = mg
        dh, dn1 = _rms_bwd(s['h'], p['norm1'][i], da, dh1, "rms1_bwd")
        per_layer['norm1'][i] = dn1[0]
    for n in ('norm1', 'norm2'):
        g[n] = jnp.stack(per_layer[n])
    for n in ('w_out', 'w_up', 'w_down'):
        g[n] = per_layer[n]

    g['w_mem_kv'] = _mm(mem_n, dmkv, name="mem_dwkv", ta=True)
    dmem_n = _mm(dmkv, fw['w_mem_kv'], name="mem_dn", tb=True)
    _, dmn = _rms_bwd(mem, p['mem_norm'], dmem_n, None, "mem_rms_bwd")
    g['mem_norm'] = dmn[0]

    def layers(kind, key):
        return [mix_grads[kind][j][key] for j in sorted(mix_grads[kind])]

    g['s5_w_in'] = layers(0, 'w_in')
    g['s5_w_glu'] = layers(0, 'w_glu')
    for n in ('b_glu', 'd_skip', 'lam_re', 'lam_im', 'log_dt', 'b_re', 'b_im', 'c_re', 'c_im'):
        g['s5_' + n] = jnp.stack(layers(0, n))
    c0 = 4 * MIX
    g['gdn_w_in'] = [[wm[:, :c0], wg[:, :H], wg[:, 128:128 + H], wm[:, c0:]]
                     for wm, wg in zip(layers(1, 'w_main'), layers(1, 'w_gate'))]
    for n in ('conv_w', 'a_log', 'dt_bias', 'o_norm'):
        g['gdn_' + n] = jnp.stack(layers(1, n))
    c0 = 3 * MIX
    g['fox_w_in'] = [[wm[:, :c0], wg[:, :H], wm[:, c0:]]
                     for wm, wg in zip(layers(2, 'w_main'), layers(2, 'w_gate'))]
    g['fox_b_f'] = jnp.stack(layers(2, 'b_f'))
    return loss, dh, g


def _gather_all(p, me):
    fw = {}

    def rows(name):
        w = p[name]
        L, r, n = w.shape
        got = _gather(w.astype(MXU_DTYPE).reshape(L * r, n), me, "all_gather_" + name)
        return [got[:, i * r:(i + 1) * r, :].reshape(4 * r, n) for i in range(L)]

    def cols(name):
        w = p[name]
        L, r, n = w.shape
        got = _gather(w.astype(MXU_DTYPE).reshape(L * r, n), me, "all_gather_" + name)
        return [[got[s, i * r:(i + 1) * r, :] for s in range(4)] for i in range(L)]

    kv = p['w_mem_kv']
    fw['w_mem_kv'] = _gather(kv.astype(MXU_DTYPE), me, "all_gather_w_mem_kv").reshape(4 * kv.shape[0], kv.shape[1])
    for name in ('w_out', 'w_down', 's5_w_in', 's5_w_glu'):
        fw[name] = rows(name)
    fw['w_up'] = [jnp.concatenate(blocks, axis=1) for blocks in cols('w_up')]
    fw['gdn_w_in'] = cols('gdn_w_in')
    fw['fox_w_in'] = cols('fox_w_in')

    vec = _gather(_pack_small([p[n] for n in VECTOR_SHARDED], 16), me, "all_gather_vectors").reshape(4, -1)
    off = 0
    for n in VECTOR_SHARDED:
        sz = p[n].size
        stacked = vec[:, off:off + sz].reshape((4,) + p[n].shape)
        ax = SHARD_AXIS[n]
        t = jnp.moveaxis(stacked, 0, ax)
        shp = list(t.shape)
        fw[n] = t.reshape(shp[:ax] + [shp[ax] * shp[ax + 1]] + shp[ax + 2:])
        off += sz
    return fw


def _scatter_all(g, p, place):
    grads = {}

    def rows(name):
        L, r, n = p[name].shape
        blocks = jnp.stack([gl.reshape(4, r, n) for gl in g[name]], axis=1).reshape(4, L * r, n)
        return _reduce_scatter(blocks, place).reshape(L, r, n)

    def shards_of_cols(segs, n):
        return jnp.stack([jnp.concatenate(_take_cols(segs, s * n, (s + 1) * n), axis=1) for s in range(4)])

    kv = p['w_mem_kv']
    grads['w_mem_kv'] = _reduce_scatter(g['w_mem_kv'].reshape((4,) + kv.shape), place)
    for name in ('w_out', 's5_w_in', 's5_w_glu'):
        grads[name] = rows(name)
    n = p['w_down'].shape[2]
    grads['w_down'] = jnp.stack([_reduce_scatter(gl.reshape(4, -1, n), place) for gl in g['w_down']])
    n = p['w_up'].shape[2]
    grads['w_up'] = jnp.stack([_reduce_scatter(shards_of_cols([gl], n), place) for gl in g['w_up']])
    for name in ('gdn_w_in', 'fox_w_in'):
        n = p[name].shape[2]
        grads[name] = jnp.stack([_reduce_scatter(shards_of_cols(segs, n), place) for segs in g[name]])

    parts = []
    for name in VECTOR_SHARDED:
        ax = SHARD_AXIS[name]
        shp = list(g[name].shape)
        t = g[name].reshape(shp[:ax] + [4, shp[ax] // 4] + shp[ax + 1:])
        parts.append(jnp.moveaxis(t, ax, 0).reshape(4, -1))
    flat = jnp.concatenate(parts, axis=1)
    flat = jnp.pad(flat, ((0, 0), (0, 16 * LANES - flat.shape[1]))).reshape(4, 16, LANES)
    red = _reduce_scatter(flat, place).reshape(-1)
    off = 0
    for name in VECTOR_SHARDED:
        grads[name] = red[off:off + p[name].size].reshape(p[name].shape)
        off += p[name].size
    return grads


def kernel(x, mem, mem_norm, w_mem_kv, norm1, w_out, norm2, w_up, w_down, norm_f, s5_w_in, s5_lam_re, s5_lam_im, s5_log_dt, s5_b_re, s5_b_im, s5_c_re, s5_c_im, s5_d_skip, s5_w_glu, s5_b_glu, gdn_w_in, gdn_conv_w, gdn_a_log, gdn_dt_bias, gdn_o_norm, fox_w_in, fox_b_f, loss_target, m_mem_norm, m_w_mem_kv, m_norm1, m_w_out, m_norm2, m_w_up, m_w_down, m_norm_f, m_s5_w_in, m_s5_lam_re, m_s5_lam_im, m_s5_log_dt, m_s5_b_re, m_s5_b_im, m_s5_c_re, m_s5_c_im, m_s5_d_skip, m_s5_w_glu, m_s5_b_glu, m_gdn_w_in, m_gdn_conv_w, m_gdn_a_log, m_gdn_dt_bias, m_gdn_o_norm, m_fox_w_in, m_fox_b_f, v_mem_norm, v_w_mem_kv, v_norm1, v_w_out, v_norm2, v_w_up, v_w_down, v_norm_f, v_s5_w_in, v_s5_lam_re, v_s5_lam_im, v_s5_log_dt, v_s5_b_re, v_s5_b_im, v_s5_c_re, v_s5_c_im, v_s5_d_skip, v_s5_w_glu, v_s5_b_glu, v_gdn_w_in, v_gdn_conv_w, v_gdn_a_log, v_gdn_dt_bias, v_gdn_o_norm, v_fox_w_in, v_fox_b_f):
    args = locals()
    p = {n: args[n] for n in WEIGHTS}
    mom = {n: args['m_' + n] for n in WEIGHTS}
    var = {n: args['v_' + n] for n in WEIGHTS}
    S, D = x.shape[1], x.shape[2]
    MW = w_mem_kv.shape[1] // 2
    MIX = D - MW
    cfg = dict(H=MIX // HEAD, MIX=MIX, MW=MW, MH=MW // HEAD, depth=norm1.shape[0])
    p.update(x=x.reshape(S, D), mem=mem.reshape(mem.shape[1], D), loss_target=loss_target.reshape(S, D))
    c = lax.axis_index("c")
    me = 2 * lax.axis_index("x") + lax.axis_index("y")
    place = dict(c=c, c_idx=c.astype(jnp.int32).reshape(1), me_idx=me.astype(jnp.int32).reshape(1))

    fw = _gather_all(p, me)
    loss, dx, g = _local_step(p, fw, cfg)
    grads = _scatter_all(g, p, place)

    n_small = sum(p[n].size for n in REPLICATED)
    rows = -(-n_small // LANES // 8) * 8
    small = _all_reduce_small(_pack_small([g[n] for n in REPLICATED], rows), "all_reduce_small").reshape(-1)
    off = 0
    for n in REPLICATED:
        grads[n] = small[off:off + p[n].size].reshape(p[n].shape)
        off += p[n].size

    delta, new_m, new_v = {}, {}, {}
    for n in SHARD_AXIS:
        shp = p[n].shape
        two_d = (-1, shp[-1])
        d, nm, nv = _adamw(p[n].reshape(two_d), grads[n].reshape(two_d), mom[n].reshape(two_d),
                           var[n].reshape(two_d), "adamw_" + n)
        delta[n], new_m[n], new_v[n] = d.reshape(shp), nm.reshape(shp), nv.reshape(shp)
    d, nm, nv = _adamw(*[_pack_small([src[n] for n in REPLICATED], rows) for src in (p, grads, mom, var)],
                       "adamw_small")
    d, nm, nv = d.reshape(-1), nm.reshape(-1), nv.reshape(-1)
    off = 0
    for n in REPLICATED:
        sz, shp = p[n].size, p[n].shape
        delta[n], new_m[n], new_v[n] = (d[off:off + sz].reshape(shp), nm[off:off + sz].reshape(shp),
                                        nv[off:off + sz].reshape(shp))
        off += sz

    total = lax.psum(loss[0, 0], ("x", "y", "c"))
    return (total, dx.reshape(x.shape), *[grads[n] for n in WEIGHTS], *[delta[n] for n in WEIGHTS],
            *[new_m[n] for n in WEIGHTS], *[new_v[n] for n in WEIGHTS])
```

```python
import math

import jax
import jax.numpy as jnp
import numpy as np
from jax import lax
from jax.experimental import pallas as pl
from jax.experimental.pallas import tpu as pltpu

F32 = jnp.float32
MXU_DTYPE = jnp.bfloat16
EPS = 1e-6
HEAD = 128
S5_GROUP = 16
S5_STATE = 64
S5_SLAB = 256
S5_CHUNK = 128
S5_ROWS = 8
GDN_CHUNK = 64
GDN_CONV = 4
LANES = 1024
VMEM_LIMIT_BYTES = 56 * 1024 * 1024
MESH = pl.DeviceIdType.MESH
RS_PAYLOAD = jnp.bfloat16
HBM_SPEC = pl.BlockSpec(memory_space=pltpu.HBM)
SEM_SPEC = pl.BlockSpec(memory_space=pltpu.SEMAPHORE)
SPLIT_EFFECT = pltpu.SideEffectType.DATAFLOW_SIDE_EFFECTING

ADAM_LR, ADAM_B1, ADAM_B2, ADAM_EPS, ADAM_WD, ADAM_STEP = 0.001, 0.9, 0.999, 1e-08, 0.01, 10

MM_TM, MM_TN, MM_TK = 1024, 1024, 1024
ROW_TILE = 256
FOX_TILE = 512
MEM_TILE = 512
CONV_TILE = 1024

NN = (((1,), (0,)), ((), ()))
NT = (((1,), (1,)), ((), ()))
TN = (((0,), (0,)), ((), ()))

WEIGHTS = ['mem_norm', 'w_mem_kv', 'norm1', 'w_out', 'norm2', 'w_up', 'w_down', 'norm_f', 's5_w_in',
           's5_lam_re', 's5_lam_im', 's5_log_dt', 's5_b_re', 's5_b_im', 's5_c_re', 's5_c_im', 's5_d_skip',
           's5_w_glu', 's5_b_glu', 'gdn_w_in', 'gdn_conv_w', 'gdn_a_log', 'gdn_dt_bias', 'gdn_o_norm',
           'fox_w_in', 'fox_b_f']
SHARD_AXIS = {'w_mem_kv': 0, 'w_out': 1, 'w_up': 2, 'w_down': 1, 's5_w_in': 1, 's5_d_skip': 1,
              's5_w_glu': 1, 's5_b_glu': 1, 'gdn_w_in': 2, 'gdn_conv_w': 2, 'fox_w_in': 2}
MATMUL_WEIGHTS = ['w_mem_kv', 'w_out', 'w_up', 'w_down', 's5_w_in', 's5_w_glu', 'gdn_w_in', 'fox_w_in']
VECTOR_SHARDED = ['s5_d_skip', 's5_b_glu', 'gdn_conv_w']
REPLICATED = [n for n in WEIGHTS if n not in SHARD_AXIS]


def _tile(dim, target, align=128):
    if dim <= target:
        return dim
    t = (target // align) * align
    while t >= align:
        if dim % t == 0:
            return t
        t -= align
    return dim


def _cp(sem=None, **kw):
    return pltpu.CompilerParams(dimension_semantics=sem, vmem_limit_bytes=VMEM_LIMIT_BYTES, **kw)


def _dot(a, b, dims):
    return lax.dot_general(a.astype(MXU_DTYPE), b.astype(MXU_DTYPE), dims, preferred_element_type=F32)


def _dotf(a, b, dims):
    return lax.dot_general(a, b, dims, precision=lax.Precision.HIGHEST, preferred_element_type=F32)


def _sigmoid(x):
    return 1.0 / (1.0 + jnp.exp(-x))


def _softplus(x):
    return jnp.maximum(x, 0.0) + jnp.log(1.0 + jnp.exp(-jnp.abs(x)))


def _relu2(x):
    r = jnp.maximum(x, 0.0)
    return r * r


_GELU_C = math.sqrt(2.0 / math.pi)


def _gelu(x):
    return 0.5 * x * (1.0 + jnp.tanh(_GELU_C * (x + 0.044715 * x * x * x)))


def _gelu_grad(x):
    t = jnp.tanh(_GELU_C * (x + 0.044715 * x * x * x))
    return 0.5 * (1.0 + t) + 0.5 * x * (1.0 - t * t) * _GELU_C * (1.0 + 3.0 * 0.044715 * x * x)


def _silu_grad(x):
    s = _sigmoid(x)
    return s + x * s * (1.0 - s)


def _mm(a, b, *, name, ta=False, tb=False, a_pro=None, extras=(), epi=None, out_dtypes=(F32,)):
    K, M = a.shape if ta else a.shape[::-1]
    N = b.shape[0] if tb else b.shape[1]
    assert (b.shape[1] if tb else b.shape[0]) == K, (a.shape, b.shape, ta, tb)
    tm, tn, tk = _tile(M, MM_TM), _tile(N, MM_TN), _tile(K, MM_TK)
    nk = K // tk
    n_ex, n_out = len(extras), len(out_dtypes)
    dims = TN if ta else (NT if tb else NN)

    def body(*refs):
        a_ref, b_ref = refs[0], refs[1]
        ex = refs[2:2 + n_ex]
        outs = refs[2 + n_ex:2 + n_ex + n_out]
        acc = refs[-1]
        k = pl.program_id(2)

        @pl.when(k == 0)
        def _():
            acc[...] = jnp.zeros_like(acc)

        at = a_ref[...]
        if a_pro is not None:
            at = a_pro(at)
        acc[...] += _dot(at, b_ref[...], dims)

        @pl.when(k == nk - 1)
        def _():
            res = acc[...]
            vals = epi(res, *[e[...] for e in ex]) if epi is not None else (res,)
            for o, v in zip(outs, vals):
                o[...] = v.astype(o.dtype)

    if ta:
        a_spec = pl.BlockSpec((tk, tm), lambda i, j, k: (k, i))
    else:
        a_spec = pl.BlockSpec((tm, tk), lambda i, j, k: (i, k))
    if tb:
        b_spec = pl.BlockSpec((tn, tk), lambda i, j, k: (j, k))
    else:
        b_spec = pl.BlockSpec((tk, tn), lambda i, j, k: (k, j))
    ex_specs, ex_arrays = [], []
    for arr, kind in extras:
        if kind == 'ij':
            ex_specs.append(pl.BlockSpec((tm, tn), lambda i, j, k: (i, j)))
            ex_arrays.append(arr)
        else:
            ex_specs.append(pl.BlockSpec((1, tn), lambda i, j, k: (0, j)))
            ex_arrays.append(arr.reshape(1, N))
    outs = pl.pallas_call(
        body, name=name, grid=(M // tm, N // tn, nk),
        in_specs=[a_spec, b_spec] + ex_specs,
        out_specs=[pl.BlockSpec((tm, tn), lambda i, j, k: (i, j)) for _ in out_dtypes],
        out_shape=[jax.ShapeDtypeStruct((M, N), dt) for dt in out_dtypes],
        scratch_shapes=[pltpu.VMEM((tm, tn), F32)],
        compiler_params=_cp(("parallel", "parallel", "arbitrary")),
    )(a, b, *ex_arrays)
    return outs[0] if n_out == 1 else tuple(outs)


def _rms_fwd(x, g, out_dtype, name):
    S, D = x.shape
    tr = _tile(S, ROW_TILE, 8)

    def body(x_ref, g_ref, o_ref):
        xv = x_ref[...]
        r = lax.rsqrt(jnp.mean(xv * xv, axis=-1, keepdims=True) + EPS)
        o_ref[...] = (xv * r * g_ref[...]).astype(o_ref.dtype)

    return pl.pallas_call(
        body, name=name, grid=(S // tr,),
        in_specs=[pl.BlockSpec((tr, D), lambda i: (i, 0)), pl.BlockSpec((1, D), lambda i: (0, 0))],
        out_specs=pl.BlockSpec((tr, D), lambda i: (i, 0)),
        out_shape=jax.ShapeDtypeStruct((S, D), out_dtype),
        compiler_params=_cp(("parallel",)),
    )(x, g.reshape(1, D))


def _rms_bwd(x, g, dy, res, name):
    S, D = x.shape
    tr = _tile(S, ROW_TILE, 8)
    has_res = res is not None

    def body(*refs):
        if has_res:
            x_ref, g_ref, dy_ref, res_ref, dx_ref, dg_ref = refs
        else:
            x_ref, g_ref, dy_ref, dx_ref, dg_ref = refs
        i = pl.program_id(0)

        @pl.when(i == 0)
        def _():
            dg_ref[...] = jnp.zeros_like(dg_ref)

        xv, d = x_ref[...], dy_ref[...].astype(F32)
        r = lax.rsqrt(jnp.mean(xv * xv, axis=-1, keepdims=True) + EPS)
        xh = xv * r
        t = d * g_ref[...]
        dx = r * (t - xh * jnp.mean(t * xh, axis=-1, keepdims=True))
        if has_res:
            dx = dx + res_ref[...]
        dx_ref[...] = dx
        dg_ref[...] += jnp.sum(d * xh, axis=0, keepdims=True)

    row = pl.BlockSpec((tr, D), lambda i: (i, 0))
    vec = pl.BlockSpec((1, D), lambda i: (0, 0))
    ins = [x, g.reshape(1, D), dy] + ([res] if has_res else [])
    return pl.pallas_call(
        body, name=name, grid=(S // tr,),
        in_specs=[row, vec, row] + ([row] if has_res else []),
        out_specs=[row, vec],
        out_shape=[jax.ShapeDtypeStruct((S, D), F32), jax.ShapeDtypeStruct((1, D), F32)],
        compiler_params=_cp(("arbitrary",)),
    )(*ins)


def _loss_head(h, g, target):
    S, D = h.shape
    tr = _tile(S, ROW_TILE, 8)

    def body(h_ref, g_ref, t_ref, loss_ref, dh_ref, dg_ref):
        i = pl.program_id(0)

        @pl.when(i == 0)
        def _():
            loss_ref[...] = jnp.zeros_like(loss_ref)
            dg_ref[...] = jnp.zeros_like(dg_ref)

        xv = h_ref[...]
        gv = g_ref[...]
        r = lax.rsqrt(jnp.mean(xv * xv, axis=-1, keepdims=True) + EPS)
        xh = xv * r
        err = xh * gv - t_ref[...]
        part = 0.5 * jnp.sum(jnp.mean(err * err, axis=-1, keepdims=True), axis=0, keepdims=True)
        loss_ref[...] += jnp.broadcast_to(part, loss_ref.shape)
        d = err * (1.0 / D)
        t = d * gv
        dh_ref[...] = r * (t - xh * jnp.mean(t * xh, axis=-1, keepdims=True))
        dg_ref[...] += jnp.sum(d * xh, axis=0, keepdims=True)

    row = pl.BlockSpec((tr, D), lambda i: (i, 0))
    vec = pl.BlockSpec((1, D), lambda i: (0, 0))
    return pl.pallas_call(
        body, name="loss_head", grid=(S // tr,),
        in_specs=[row, vec, row],
        out_specs=[pl.BlockSpec((8, 128), lambda i: (0, 0)), row, vec],
        out_shape=[jax.ShapeDtypeStruct((8, 128), F32), jax.ShapeDtypeStruct((S, D), F32),
                   jax.ShapeDtypeStruct((1, D), F32)],
        compiler_params=_cp(("arbitrary",)),
    )(h, g.reshape(1, D), target)


def _adamw(w, g, m, v, name):
    R, C = w.shape
    tr = _tile(R, max(8, (1 << 19) // max(C, 1) // 8 * 8), 8)
    c1 = 1.0 / (1.0 - ADAM_B1 ** ADAM_STEP)
    c2 = 1.0 / (1.0 - ADAM_B2 ** ADAM_STEP)

    def body(w_ref, g_ref, m_ref, v_ref, d_ref, nm_ref, nv_ref):
        gv = g_ref[...]
        nm = ADAM_B1 * m_ref[...] + (1.0 - ADAM_B1) * gv
        nv = ADAM_B2 * v_ref[...] + (1.0 - ADAM_B2) * (gv * gv)
        d_ref[...] = -ADAM_LR * ((nm * c1) / (jnp.sqrt(nv * c2) + ADAM_EPS) + ADAM_WD * w_ref[...])
        nm_ref[...] = nm
        nv_ref[...] = nv

    blk = pl.BlockSpec((tr, C), lambda i: (i, 0))
    return pl.pallas_call(
        body, name=name, grid=(R // tr,),
        in_specs=[blk] * 4, out_specs=[blk] * 3,
        out_shape=[jax.ShapeDtypeStruct((R, C), F32)] * 3,
        compiler_params=_cp(("parallel",)),
    )(w, g, m, v)


def _place():
    x, y, c = lax.axis_index("x"), lax.axis_index("y"), lax.axis_index("c")
    chips = [(1 - x, y), (x, 1 - y), (1 - x, 1 - y)]
    return x, y, c, chips


def _in_hbm(a):
    return pltpu.with_memory_space_constraint(a, pltpu.HBM)


def _ag_start(xs, name):
    r, n = xs.shape
    half = r // 2

    def body(x_ref, land_ref, send_sems, recv_sems, x_thru, land_thru, token):
        x, y, c, chips = _place()
        rows = pl.ds(c * half, half)
        for j, (cx, cy) in enumerate(chips):
            pltpu.make_async_remote_copy(
                src_ref=x_ref.at[rows, :], dst_ref=land_ref.at[2 * x + y, rows, :], send_sem=send_sems.at[j],
                recv_sem=recv_sems.at[j], device_id=(cx, cy, c), device_id_type=MESH).start()
        token[...] = jnp.zeros_like(token)

    sems = pltpu.SemaphoreType.DMA((3,))
    out = pl.pallas_call(
        body, name=name,
        out_shape=(sems, sems, pltpu.HBM(xs.shape, xs.dtype), pltpu.HBM((4, r, n), xs.dtype),
                   jax.ShapeDtypeStruct((8, 128), F32)),
        in_specs=(HBM_SPEC, HBM_SPEC),
        out_specs=(SEM_SPEC, SEM_SPEC, HBM_SPEC, HBM_SPEC, pl.BlockSpec(memory_space=pltpu.VMEM)),
        input_output_aliases={0: 2, 1: 3},
        compiler_params=pltpu.CompilerParams(has_side_effects=SPLIT_EFFECT),
    )(_in_hbm(xs), _in_hbm(lax.empty((4, r, n), xs.dtype)))
    return out[:4], out[4][0, 0]


def _ag_wait(handle, after, name):
    send_sems, recv_sems, xs, land = handle
    r, n = xs.shape
    half = r // 2

    def body(x_ref, land_ref, send_sems, recv_sems, after_ref, x_out, land_out):
        x, y, c, chips = _place()
        rows = pl.ds(c * half, half)
        for j, (cx, cy) in enumerate(chips):
            cp = pltpu.make_async_remote_copy(
                src_ref=x_ref.at[rows, :], dst_ref=land_ref.at[2 * cx + cy, rows, :], send_sem=send_sems.at[j],
                recv_sem=recv_sems.at[j], device_id=(cx, cy, c), device_id_type=MESH)
            cp.wait_send()
            cp.wait_recv()

    return pl.pallas_call(
        body, name=name,
        out_shape=(pltpu.HBM(xs.shape, xs.dtype), pltpu.HBM(land.shape, land.dtype)),
        in_specs=(HBM_SPEC, HBM_SPEC, SEM_SPEC, SEM_SPEC, pl.BlockSpec(memory_space=pl.ANY)),
        out_specs=(HBM_SPEC, HBM_SPEC),
        input_output_aliases={0: 0, 1: 1},
        compiler_params=pltpu.CompilerParams(has_side_effects=SPLIT_EFFECT),
    )(xs, land, send_sems, recv_sems, after)[1]


def _ag_forward(got, name):
    _, r, n = got.shape
    half = r // 2

    def body(g_ref, out_ref, send_sems, recv_sems):
        x, y, c, chips = _place()
        sibling = (x, y, 1 - c)
        sent = []
        for j, (cx, cy) in enumerate(chips):
            piece = out_ref.at[2 * cx + cy, pl.ds(c * half, half), :]
            cp = pltpu.make_async_remote_copy(src_ref=piece, dst_ref=piece, send_sem=send_sems.at[j],
                                              recv_sem=recv_sems.at[j], device_id=sibling, device_id_type=MESH)
            cp.start()
            sent.append(cp)
        for j, (cx, cy) in enumerate(chips):
            piece = out_ref.at[2 * cx + cy, pl.ds((1 - c) * half, half), :]
            pltpu.make_async_remote_copy(src_ref=piece, dst_ref=piece, send_sem=send_sems.at[j],
                                         recv_sem=recv_sems.at[j], device_id=sibling, device_id_type=MESH).wait_recv()
        for cp in sent:
            cp.wait_send()

    return pl.pallas_call(
        body, name=name,
        in_specs=[pl.BlockSpec(memory_space=pl.ANY)],
        out_specs=pl.BlockSpec(memory_space=pl.ANY),
        out_shape=jax.ShapeDtypeStruct(got.shape, got.dtype),
        input_output_aliases={0: 0},
        scratch_shapes=[pltpu.SemaphoreType.DMA((3,)), pltpu.SemaphoreType.DMA((3,))],
    )(got)


def _all_gather_chips(xs, name):
    r, n = xs.shape
    half = r // 2

    def body(x_ref, out_ref, send_sems, recv_sems):
        x, y, c, chips = _place()
        me = 2 * x + y
        sibling = (x, y, 1 - c)

        def piece(chip, hc):
            return out_ref.at[chip, pl.ds(hc * half, half), :]

        def copy(k, src, dst, to):
            return pltpu.make_async_remote_copy(src_ref=src, dst_ref=dst, send_sem=send_sems.at[k],
                                                recv_sem=recv_sems.at[k], device_id=to, device_id_type=MESH)

        src = x_ref.at[pl.ds(c * half, half), :]
        first = [copy(j, src, piece(me, c), (cx, cy, c)) for j, (cx, cy) in enumerate(chips)]
        for cp in first:
            cp.start()
        passed = []
        for j, (cx, cy) in enumerate(chips):
            got = piece(2 * cx + cy, c)
            copy(j, got, got, (cx, cy, c)).wait_recv()
            fwd = copy(3 + j, got, got, sibling)
            fwd.start()
            passed.append(fwd)
        for j, (cx, cy) in enumerate(chips):
            got = piece(2 * cx + cy, 1 - c)
            copy(3 + j, got, got, sibling).wait_recv()
        for cp in first + passed:
            cp.wait_send()

    return pl.pallas_call(
        body, name=name,
        in_specs=[pl.BlockSpec(memory_space=pl.ANY)],
        out_specs=pl.BlockSpec(memory_space=pl.ANY),
        out_shape=jax.ShapeDtypeStruct((4, r, n), xs.dtype),
        scratch_shapes=[pltpu.SemaphoreType.DMA((6,)), pltpu.SemaphoreType.DMA((6,))],
    )(xs)


def _gather(xs, me, name):
    return lax.dynamic_update_slice(_all_gather_chips(xs, name), xs[None], (me, 0, 0))


def _rs_swap_halves(g, name):
    _, r, n = g.shape
    half = r // 2

    def body(g_ref, out_ref, send_sem, recv_sem):
        x, y, c, _ = _place()
        cp = pltpu.make_async_remote_copy(
            src_ref=g_ref.at[:, pl.ds((1 - c) * half, half), :], dst_ref=out_ref,
            send_sem=send_sem, recv_sem=recv_sem, device_id=(x, y, 1 - c), device_id_type=MESH)
        cp.start()
        cp.wait()

    return pl.pallas_call(
        body, name=name,
        in_specs=[pl.BlockSpec(memory_space=pl.ANY)],
        out_specs=pl.BlockSpec(memory_space=pl.ANY),
        out_shape=jax.ShapeDtypeStruct((4, half, n), g.dtype),
        scratch_shapes=[pltpu.SemaphoreType.DMA, pltpu.SemaphoreType.DMA],
    )(g)


def _rs_add_halves(g, got, c_idx, name):
    _, r, n = g.shape
    half = r // 2
    tr = _tile(half, max(16, (1 << 19) // n // 16 * 16), 16)
    nb = half // tr

    def body(c_ref, g_ref, o_ref, out_ref, out16_ref):
        sm = g_ref[...] + o_ref[...]
        out_ref[...] = sm
        out16_ref[...] = sm.astype(out16_ref.dtype)

    blk = pl.BlockSpec((None, tr, n), lambda s, i, c: (s, i, 0))
    return pl.pallas_call(
        body, name=name,
        grid_spec=pltpu.PrefetchScalarGridSpec(
            num_scalar_prefetch=1, grid=(4, nb),
            in_specs=[pl.BlockSpec((None, tr, n), lambda s, i, c: (s, c[0] * nb + i, 0)), blk],
            out_specs=[blk, blk]),
        out_shape=[jax.ShapeDtypeStruct((4, half, n), F32), jax.ShapeDtypeStruct((4, half, n), RS_PAYLOAD)],
        compiler_params=_cp(("parallel", "parallel")),
    )(c_idx, g, got)


def _rs_start(p32, p16, name):
    _, h, n = p16.shape

    def body(p32_ref, p16_ref, l16_ref, l32_ref, send_sems, recv_sems, p32_t, p16_t, l16_t, l32_t, token):
        x, y, c, chips = _place()
        for j, (cx, cy) in enumerate(chips):
            for k, pc in enumerate((c, 1 - c)):
                pltpu.make_async_remote_copy(
                    src_ref=p16_ref.at[2 * cx + cy], dst_ref=l16_ref.at[c, j], send_sem=send_sems.at[3 * k + j],
                    recv_sem=recv_sems.at[3 * k + j], device_id=(cx, cy, pc), device_id_type=MESH).start()
        pltpu.make_async_remote_copy(
            src_ref=p32_ref.at[2 * x + y], dst_ref=l32_ref, send_sem=send_sems.at[6], recv_sem=recv_sems.at[6],
            device_id=(x, y, 1 - c), device_id_type=MESH).start()
        token[...] = jnp.zeros_like(token)

    sems = pltpu.SemaphoreType.DMA((7,))
    l16 = lax.empty((2, 3, h, n), p16.dtype)
    l32 = lax.empty((h, n), F32)
    out = pl.pallas_call(
        body, name=name,
        out_shape=(sems, sems, pltpu.HBM(p32.shape, F32), pltpu.HBM(p16.shape, p16.dtype),
                   pltpu.HBM(l16.shape, l16.dtype), pltpu.HBM(l32.shape, F32), jax.ShapeDtypeStruct((8, 128), F32)),
        in_specs=(HBM_SPEC,) * 4,
        out_specs=(SEM_SPEC, SEM_SPEC) + (HBM_SPEC,) * 4 + (pl.BlockSpec(memory_space=pltpu.VMEM),),
        input_output_aliases={0: 2, 1: 3, 2: 4, 3: 5},
        compiler_params=pltpu.CompilerParams(has_side_effects=SPLIT_EFFECT),
    )(_in_hbm(p32), _in_hbm(p16), _in_hbm(l16), _in_hbm(l32))
    return out[:6], out[6][0, 0]


def _rs_wait(handle, after, name):
    send_sems, recv_sems, p32, p16, l16, l32 = handle

    def body(p32_ref, p16_ref, l16_ref, l32_ref, send_sems, recv_sems, after_ref, p32_o, p16_o, l16_o, l32_o):
        x, y, c, chips = _place()
        for j, (cx, cy) in enumerate(chips):
            for k, pc in enumerate((c, 1 - c)):
                cp = pltpu.make_async_remote_copy(
                    src_ref=p16_ref.at[2 * cx + cy], dst_ref=l16_ref.at[pc, j], send_sem=send_sems.at[3 * k + j],
                    recv_sem=recv_sems.at[3 * k + j], device_id=(cx, cy, pc), device_id_type=MESH)
                cp.wait_send()
                cp.wait_recv()
        cp = pltpu.make_async_remote_copy(
            src_ref=p32_ref.at[2 * x + y], dst_ref=l32_ref, send_sem=send_sems.at[6], recv_sem=recv_sems.at[6],
            device_id=(x, y, 1 - c), device_id_type=MESH)
        cp.wait_send()
        cp.wait_recv()

    out = pl.pallas_call(
        body, name=name,
        out_shape=(pltpu.HBM(p32.shape, F32), pltpu.HBM(p16.shape, p16.dtype), pltpu.HBM(l16.shape, l16.dtype),
                   pltpu.HBM(l32.shape, F32)),
        in_specs=(HBM_SPEC,) * 4 + (SEM_SPEC, SEM_SPEC, pl.BlockSpec(memory_space=pl.ANY)),
        out_specs=(HBM_SPEC,) * 4,
        input_output_aliases={0: 0, 1: 1, 2: 2, 3: 3},
        compiler_params=pltpu.CompilerParams(has_side_effects=SPLIT_EFFECT),
    )(p32, p16, l16, l32, send_sems, recv_sems, after)
    return out[0], out[2], out[3]


def _rs_finish(p32, l16, l32, c_idx, me_idx, name):
    _, h, n = p32.shape
    tr = _tile(h, max(16, (1 << 18) // n // 16 * 16), 16)
    nb = h // tr

    def body(c_ref, me_ref, own_ref, sib_ref, a_ref, b_ref, d_ref, out_ref):
        base = jnp.where(pl.program_id(0) == c_ref[0], own_ref[...], sib_ref[...])
        out_ref[...] = ((base + a_ref[...].astype(F32)) + b_ref[...].astype(F32)) + d_ref[...].astype(F32)

    def piece(j):
        return pl.BlockSpec((None, None, tr, n), lambda hc, i, c, me: (hc, j, i, 0))

    return pl.pallas_call(
        body, name=name,
        grid_spec=pltpu.PrefetchScalarGridSpec(
            num_scalar_prefetch=2, grid=(2, nb),
            in_specs=[pl.BlockSpec((None, tr, n), lambda hc, i, c, me: (me[0], i, 0)),
                      pl.BlockSpec((tr, n), lambda hc, i, c, me: (i, 0)),
                      piece(0), piece(1), piece(2)],
            out_specs=pl.BlockSpec((tr, n), lambda hc, i, c, me: (hc * nb + i, 0))),
        out_shape=jax.ShapeDtypeStruct((2 * h, n), F32),
        compiler_params=_cp(("parallel", "parallel")),
    )(c_idx, me_idx, p32, l32, l16, l16, l16)


def _rs_begin(g, place, tag):
    got = _rs_swap_halves(g, "rs_swap_halves")
    p32, p16 = _rs_add_halves(g, got, place['c_idx'], "rs_add_halves")
    return _rs_start(p32, p16, "rs_start_" + tag)


def _rs_end(handle, after, place, tag):
    p32, l16, l32 = _rs_wait(handle, after, "rs_wait_" + tag)
    return _rs_finish(p32, l16, l32, place['c_idx'], place['me_idx'], "rs_finish")


def _all_reduce_small(v, name):
    R, n = v.shape

    def body(v_ref, out_ref, buf, send_sems, recv_sems):
        x, y, c, _ = _place()
        me = 4 * x + 2 * y + c
        buf[me] = v_ref[...]
        copies = []
        for d in range(1, 8):
            dx, dy, dc = (d >> 2) & 1, (d >> 1) & 1, d & 1
            px = x if dx == 0 else 1 - x
            py = y if dy == 0 else 1 - y
            pc = c if dc == 0 else 1 - c
            copies.append(pltpu.make_async_remote_copy(
                src_ref=v_ref, dst_ref=buf.at[me], send_sem=send_sems.at[d - 1], recv_sem=recv_sems.at[d - 1],
                device_id=(px, py, pc), device_id_type=MESH))
        for cp in copies:
            cp.start()
        for d in range(1, 8):
            dx, dy, dc = (d >> 2) & 1, (d >> 1) & 1, d & 1
            px = x if dx == 0 else 1 - x
            py = y if dy == 0 else 1 - y
            pc = c if dc == 0 else 1 - c
            pltpu.make_async_remote_copy(
                src_ref=v_ref, dst_ref=buf.at[4 * px + 2 * py + pc], send_sem=send_sems.at[d - 1],
                recv_sem=recv_sems.at[d - 1], device_id=(px, py, pc), device_id_type=MESH).wait_recv()
        for cp in copies:
            cp.wait_send()
        acc = buf[0]
        for k in range(1, 8):
            acc = acc + buf[k]
        out_ref[...] = acc

    return pl.pallas_call(
        body, name=name,
        in_specs=[pl.BlockSpec(memory_space=pltpu.VMEM)],
        out_specs=pl.BlockSpec(memory_space=pltpu.VMEM),
        out_shape=jax.ShapeDtypeStruct((R, n), F32),
        scratch_shapes=[pltpu.VMEM((8, R, n), F32), pltpu.SemaphoreType.DMA((7,)), pltpu.SemaphoreType.DMA((7,))],
        compiler_params=pltpu.CompilerParams(vmem_limit_bytes=VMEM_LIMIT_BYTES),
    )(v)


def _pack_small(parts, rows):
    flat = jnp.concatenate([a.reshape(-1) for a in parts])
    return jnp.pad(flat, (0, rows * LANES - flat.shape[0])).reshape(rows, LANES)


def _mem_fwd(proj, q_blk, mkv, heads):
    S = proj.shape[0]
    ML = mkv.shape[0]
    t = _tile(S, MEM_TILE, 8)
    scale = HEAD ** -0.5

    def body(q_ref, k_ref, v_ref, o_ref):
        s = _dot(q_ref[...], k_ref[...], NT) * scale
        m = jnp.max(s, axis=-1, keepdims=True)
        e = jnp.exp(s - m)
        p = e / jnp.sum(e, axis=-1, keepdims=True)
        o_ref[...] = _dot(p, v_ref[...], NN)

    return pl.pallas_call(
        body, name="mem_fwd", grid=(S // t, heads),
        in_specs=[pl.BlockSpec((t, HEAD), lambda i, h: (i, q_blk + h)),
                  pl.BlockSpec((ML, HEAD), lambda i, h: (0, h)),
                  pl.BlockSpec((ML, HEAD), lambda i, h: (0, heads + h))],
        out_specs=pl.BlockSpec((t, HEAD), lambda i, h: (i, h)),
        out_shape=jax.ShapeDtypeStruct((S, heads * HEAD), F32),
        compiler_params=_cp(("parallel", "parallel")),
    )(proj, mkv, mkv)


def _mem_bwd(proj, q_blk, mkv, dcat, d_blk, heads):
    S = proj.shape[0]
    ML = mkv.shape[0]
    t = _tile(S, MEM_TILE, 8)
    scale = HEAD ** -0.5

    def body(q_ref, k_ref, v_ref, do_ref, dq_ref, dk_ref, dv_ref):
        i = pl.program_id(1)

        @pl.when(i == 0)
        def _():
            dk_ref[...] = jnp.zeros_like(dk_ref)
            dv_ref[...] = jnp.zeros_like(dv_ref)

        q, k, v, do = q_ref[...], k_ref[...], v_ref[...], do_ref[...]
        s = _dot(q, k, NT) * scale
        m = jnp.max(s, axis=-1, keepdims=True)
        e = jnp.exp(s - m)
        p = e / jnp.sum(e, axis=-1, keepdims=True)
        dp = _dot(do, v, NT)
        ds = p * (dp - jnp.sum(p * dp, axis=-1, keepdims=True))
        dq_ref[...] = _dot(ds, k, NN) * scale
        dk_ref[...] += _dot(ds, q, TN) * scale
        dv_ref[...] += _dot(p, do, TN)

    dq, dk, dv = pl.pallas_call(
        body, name="mem_bwd", grid=(heads, S // t),
        in_specs=[pl.BlockSpec((t, HEAD), lambda h, i: (i, q_blk + h)),
                  pl.BlockSpec((ML, HEAD), lambda h, i: (0, h)),
                  pl.BlockSpec((ML, HEAD), lambda h, i: (0, heads + h)),
                  pl.BlockSpec((t, HEAD), lambda h, i: (i, d_blk + h))],
        out_specs=[pl.BlockSpec((t, HEAD), lambda h, i: (i, h)),
                   pl.BlockSpec((ML, HEAD), lambda h, i: (0, h)),
                   pl.BlockSpec((ML, HEAD), lambda h, i: (0, h))],
        out_shape=[jax.ShapeDtypeStruct((S, heads * HEAD), F32),
                   jax.ShapeDtypeStruct((ML, heads * HEAD), F32),
                   jax.ShapeDtypeStruct((ML, heads * HEAD), F32)],
        compiler_params=_cp(("parallel", "arbitrary")),
    )(proj, mkv, mkv, dcat)
    return dq, jnp.concatenate([dk, dv], axis=1)


def _fox_gates(gl, bf):
    S = gl.shape[0]

    def body(g_ref, b_ref, o_ref):
        xv = g_ref[...] + b_ref[...]
        c = jnp.minimum(xv, 0.0) - jnp.log(1.0 + jnp.exp(-jnp.abs(xv)))
        row = lax.broadcasted_iota(jnp.int32, c.shape, 0)
        d = 1
        while d < S:
            c = c + jnp.where(row >= d, pltpu.roll(c, d, 0), 0.0)
            d *= 2
        o_ref[...] = c

    return pl.pallas_call(
        body, name="fox_gates", out_shape=jax.ShapeDtypeStruct((S, 128), F32),
        in_specs=[pl.BlockSpec(memory_space=pltpu.VMEM)] * 2,
        out_specs=pl.BlockSpec(memory_space=pltpu.VMEM),
        compiler_params=_cp(),
    )(gl, bf)


def _fox_gates_bwd(gl, bf, dcf):
    S = gl.shape[0]

    def body(g_ref, b_ref, d_ref, dg_ref, db_ref):
        c = d_ref[...]
        row = lax.broadcasted_iota(jnp.int32, c.shape, 0)
        d = 1
        while d < S:
            c = c + jnp.where(row < S - d, pltpu.roll(c, S - d, 0), 0.0)
            d *= 2
        dx = c * _sigmoid(-(g_ref[...] + b_ref[...]))
        dg_ref[...] = dx
        db_ref[...] = jnp.sum(dx, axis=0, keepdims=True)

    return pl.pallas_call(
        body, name="fox_gates_bwd",
        out_shape=[jax.ShapeDtypeStruct((S, 128), F32), jax.ShapeDtypeStruct((1, 128), F32)],
        in_specs=[pl.BlockSpec(memory_space=pltpu.VMEM)] * 3,
        out_specs=[pl.BlockSpec(memory_space=pltpu.VMEM)] * 2,
        compiler_params=_cp(),
    )(gl, bf, dcf)


def _fox_scores(q, k, cq, ck, t, masked):
    s = _dot(q, k, NT) * (HEAD ** -0.5) + cq - ck
    if masked:
        row = lax.broadcasted_iota(jnp.int32, (t, t), 0)
        col = lax.broadcasted_iota(jnp.int32, (t, t), 1)
        s = jnp.where(row >= col, s, -jnp.inf)
    return s


def _fox_pairs(nq, by_key):
    if by_key:
        pairs = [(i, j) for j in range(nq) for i in range(j, nq)]
    else:
        pairs = [(i, j) for i in range(nq) for j in range(i + 1)]
    return (jnp.asarray(np.array([a for a, _ in pairs], np.int32)),
            jnp.asarray(np.array([b for _, b in pairs], np.int32)))


def _fox_heads_per_step(H):
    return 2 if H % 2 == 0 else 1


def _fox_fwd(proj, cfq, cfk, H):
    S = proj.shape[0]
    t = _tile(S, FOX_TILE)
    nq = S // t
    hb = _fox_heads_per_step(H)
    W, G = hb * HEAD, H // hb
    cols = [slice(i * HEAD, (i + 1) * HEAD) for i in range(hb)]
    qt, kt = _fox_pairs(nq, False)

    def body(qt_ref, kt_ref, q_ref, k_ref, v_ref, cq_ref, ck_ref, o_ref, lse_ref, m_s, l_s, acc_s):
        n = pl.program_id(1)
        qi, ki = qt_ref[n], kt_ref[n]

        @pl.when(ki == 0)
        def _():
            m_s[...] = jnp.full_like(m_s, -jnp.inf)
            l_s[...] = jnp.zeros_like(l_s)
            acc_s[...] = jnp.zeros_like(acc_s)

        def step(masked):
            R = range(hb)
            ss = [_fox_scores(q_ref[:, cols[i]], k_ref[:, cols[i]], cq_ref[i], ck_ref[i], t, masked) for i in R]
            m_new = [jnp.maximum(m_s[i], jnp.max(ss[i], axis=-1, keepdims=True)) for i in R]
            alpha = [jnp.exp(m_s[i] - m_new[i]) for i in R]
            ps = [jnp.exp(ss[i] - m_new[i]) for i in R]
            pv = [_dot(ps[i], v_ref[:, cols[i]], NN) for i in R]
            for i in R:
                l_s[i] = alpha[i] * l_s[i] + jnp.sum(ps[i], axis=-1, keepdims=True)
                acc_s[:, cols[i]] = alpha[i] * acc_s[:, cols[i]] + pv[i]
                m_s[i] = m_new[i]

        @pl.when(ki != qi)
        def _():
            step(False)

        @pl.when(ki == qi)
        def _():
            step(True)
            for i in range(hb):
                o_ref[:, cols[i]] = acc_s[:, cols[i]] / l_s[i]
                lse_ref[i] = m_s[i] + jnp.log(l_s[i])

    qcol = pl.BlockSpec((hb, t, 1), lambda h, n, qt, kt: (h, qt[n], 0))
    return pl.pallas_call(
        body, name="fox_fwd",
        grid_spec=pltpu.PrefetchScalarGridSpec(
            num_scalar_prefetch=2, grid=(G, qt.shape[0]),
            in_specs=[pl.BlockSpec((t, W), lambda h, n, qt, kt: (qt[n], h)),
                      pl.BlockSpec((t, W), lambda h, n, qt, kt: (kt[n], G + h)),
                      pl.BlockSpec((t, W), lambda h, n, qt, kt: (kt[n], 2 * G + h)),
                      qcol,
                      pl.BlockSpec((hb, 1, t), lambda h, n, qt, kt: (h, 0, kt[n]))],
            out_specs=[pl.BlockSpec((t, W), lambda h, n, qt, kt: (qt[n], h)), qcol],
            scratch_shapes=[pltpu.VMEM((hb, t, 1), F32), pltpu.VMEM((hb, t, 1), F32), pltpu.VMEM((t, W), F32)]),
        out_shape=[jax.ShapeDtypeStruct((S, H * HEAD), F32), jax.ShapeDtypeStruct((H, S, 1), F32)],
        compiler_params=_cp(("parallel", "arbitrary")),
    )(qt, kt, proj, proj, proj, cfq, cfk)


def _fox_bwd_rowdot(proj, cfq, cfk, lse, dcat, H):
    S = proj.shape[0]
    t = _tile(S, FOX_TILE)
    nq = S // t
    hb = _fox_heads_per_step(H)
    W, G = hb * HEAD, H // hb
    cols = [slice(i * HEAD, (i + 1) * HEAD) for i in range(hb)]
    qt, kt = _fox_pairs(nq, False)

    def body(qt_ref, kt_ref, q_ref, k_ref, v_ref, do_ref, lse_ref, cq_ref, ck_ref, d_ref):
        n = pl.program_id(1)
        qi, ki = qt_ref[n], kt_ref[n]

        @pl.when(ki == 0)
        def _():
            d_ref[...] = jnp.zeros_like(d_ref)

        def step(masked):
            R = range(hb)
            ss = [_fox_scores(q_ref[:, cols[i]], k_ref[:, cols[i]], cq_ref[i], ck_ref[i], t, masked) for i in R]
            dps = [_dot(do_ref[:, cols[i]], v_ref[:, cols[i]], NT) for i in R]
            ps = [jnp.exp(ss[i] - lse_ref[i]) for i in R]
            for i in R:
                d_ref[i] += jnp.sum(ps[i] * dps[i], axis=-1, keepdims=True)

        @pl.when(ki != qi)
        def _():
            step(False)

        @pl.when(ki == qi)
        def _():
            step(True)

    qtile = pl.BlockSpec((t, W), lambda h, n, qt, kt: (qt[n], h))
    qcol = pl.BlockSpec((hb, t, 1), lambda h, n, qt, kt: (h, qt[n], 0))
    return pl.pallas_call(
        body, name="fox_bwd_rowdot",
        grid_spec=pltpu.PrefetchScalarGridSpec(
            num_scalar_prefetch=2, grid=(G, qt.shape[0]),
            in_specs=[qtile,
                      pl.BlockSpec((t, W), lambda h, n, qt, kt: (kt[n], G + h)),
                      pl.BlockSpec((t, W), lambda h, n, qt, kt: (kt[n], 2 * G + h)),
                      qtile, qcol, qcol,
                      pl.BlockSpec((hb, 1, t), lambda h, n, qt, kt: (h, 0, kt[n]))],
            out_specs=qcol),
        out_shape=jax.ShapeDtypeStruct((H, S, 1), F32),
        compiler_params=_cp(("parallel", "arbitrary")),
    )(qt, kt, proj, proj, proj, dcat, lse, cfq, cfk)


def _fox_bwd(proj, cfq, cfk, rowdot, lse, dcat, H):
    S = proj.shape[0]
    t = _tile(S, FOX_TILE)
    nq = S // t
    scale = HEAD ** -0.5
    hb = _fox_heads_per_step(H)
    W, G = hb * HEAD, H // hb
    cols = [slice(i * HEAD, (i + 1) * HEAD) for i in range(hb)]
    qt, kt = _fox_pairs(nq, True)

    def body(qt_ref, kt_ref, q_ref, k_ref, v_ref, dd_ref, do_ref, lse_ref, cq_ref, ck_ref,
             dq_ref, dk_ref, dv_ref, dck_ref):
        n = pl.program_id(1)
        i_, j_ = qt_ref[n], kt_ref[n]

        @pl.when(n == 0)
        def _():
            dq_ref[...] = jnp.zeros_like(dq_ref)

        @pl.when(i_ == j_)
        def _():
            dk_ref[...] = jnp.zeros_like(dk_ref)
            dv_ref[...] = jnp.zeros_like(dv_ref)
            dck_ref[...] = jnp.zeros_like(dck_ref)

        def step(masked):
            R = range(hb)
            qs, ks = [q_ref[:, c] for c in cols], [k_ref[:, c] for c in cols]
            dos = [do_ref[:, c] for c in cols]
            ss = [_fox_scores(qs[i], ks[i], cq_ref[i], ck_ref[i], t, masked) for i in R]
            dps = [_dot(dos[i], v_ref[:, cols[i]], NT) for i in R]
            ps = [jnp.exp(ss[i] - lse_ref[i]) for i in R]
            dss = [ps[i] * (dps[i] - dd_ref[i]) for i in R]
            dvs = [_dot(ps[i], dos[i], TN) for i in R]
            dks = [_dot(dss[i], qs[i], TN) * scale for i in R]
            dqs = [_dot(dss[i], ks[i], NN) * scale for i in R]
            rows = pl.ds(pl.multiple_of(i_ * t, t), t)
            for i in R:
                dv_ref[:, cols[i]] += dvs[i]
                dk_ref[:, cols[i]] += dks[i]
                dq_ref[rows, cols[i]] += dqs[i]
                dck_ref[i] -= jnp.sum(dss[i], axis=0, keepdims=True)

        @pl.when(i_ != j_)
        def _():
            step(False)

        @pl.when(i_ == j_)
        def _():
            step(True)

    qtile = pl.BlockSpec((t, W), lambda h, n, qt, kt: (qt[n], h))
    qcol = pl.BlockSpec((hb, t, 1), lambda h, n, qt, kt: (h, qt[n], 0))
    ktile = pl.BlockSpec((t, W), lambda h, n, qt, kt: (kt[n], h))
    krow = pl.BlockSpec((hb, 1, t), lambda h, n, qt, kt: (h, 0, kt[n]))
    return pl.pallas_call(
        body, name="fox_bwd",
        grid_spec=pltpu.PrefetchScalarGridSpec(
            num_scalar_prefetch=2, grid=(G, qt.shape[0]),
            in_specs=[qtile,
                      pl.BlockSpec((t, W), lambda h, n, qt, kt: (kt[n], G + h)),
                      pl.BlockSpec((t, W), lambda h, n, qt, kt: (kt[n], 2 * G + h)),
                      qcol, qtile, qcol, qcol, krow],
            out_specs=[pl.BlockSpec((S, W), lambda h, n, qt, kt: (0, h)), ktile, ktile, krow]),
        out_shape=[jax.ShapeDtypeStruct((S, H * HEAD), F32)] * 3 + [jax.ShapeDtypeStruct((H, 1, S), F32)],
        compiler_params=_cp(("parallel", "arbitrary")),
    )(qt, kt, proj, proj, proj, rowdot, dcat, lse, cfq, cfk)


def _s5_prep(lam_re, lam_im, log_dt, b_re, b_im, c_re, c_im):
    G, P = lam_re.shape
    ns = G // 16
    dt = jnp.exp(log_dt)[:, None]
    mag = jnp.exp(lam_re * dt)
    a_re, a_im = mag * jnp.cos(lam_im * dt), mag * jnp.sin(lam_im * dt)
    den = lam_re * lam_re + lam_im * lam_im
    z_re = ((a_re - 1.0) * lam_re + a_im * lam_im) / den
    z_im = (a_im * lam_re - (a_re - 1.0) * lam_im) / den
    bb_re = z_re[..., None] * b_re - z_im[..., None] * b_im
    bb_im = z_re[..., None] * b_im + z_im[..., None] * b_re
    eye = jnp.eye(16, dtype=F32)
    bb = jnp.stack([bb_re, bb_im]).reshape(2, ns, 16, P, S5_GROUP)
    wb = jnp.einsum('asgpc,gh->sgcahp', bb, eye).reshape(ns, S5_SLAB, 2 * 16 * P)
    cc = jnp.stack([c_re, -c_im]).reshape(2, ns, 16, S5_GROUP, P)
    wc = jnp.einsum('asgcp,gh->sagphc', cc, eye).reshape(ns, 2 * 16 * P, S5_SLAB)
    a = jnp.concatenate([a_re.reshape(ns, 1, 16 * P), a_im.reshape(ns, 1, 16 * P)], axis=-1)
    return wb, wc, a


def _s5_tables(lam_re, lam_im, log_dt):
    G, P = lam_re.shape
    ns = G // 16
    dt = jnp.exp(log_dt)[:, None]
    tt = jnp.arange(1, S5_CHUNK + 1, dtype=F32)[:, None, None]
    mag = jnp.exp(lam_re * dt * tt)
    ang = lam_im * dt * tt
    pr = (mag * jnp.cos(ang)).reshape(S5_CHUNK, ns, 16 * P).transpose(1, 0, 2)
    pi = (mag * jnp.sin(ang)).reshape(S5_CHUNK, ns, 16 * P).transpose(1, 0, 2)
    return pr, pi, pr[:, ::-1], pi[:, ::-1]


def _s5_scan_fwd(proj, wb, wc, pr, pi, dskip):
    S = proj.shape[0]
    ns = wb.shape[0]
    W = wb.shape[2]
    hw = W // 2
    T = S5_CHUNK
    nc = S // T
    mix = ns * S5_SLAB

    def body(u_ref, wb_ref, wc_ref, pr_ref, pi_ref, d_ref, v_ref, yg_ref, h_ref, cin_ref, carry):
        c = pl.program_id(1)

        @pl.when(c == 0)
        def _():
            carry[...] = jnp.zeros_like(carry)

        u = u_ref[...]
        bu = _dot(u, wb_ref[...], NN)
        xr, xi = bu[:, :hw], bu[:, hw:]
        sub = lax.broadcasted_iota(jnp.int32, (T, hw), 0) & (S5_ROWS - 1)
        d = 1
        while d < S5_ROWS:
            ar, ai = pr_ref[pl.ds(d - 1, 1), :], pi_ref[pl.ds(d - 1, 1), :]
            sr = jnp.where(sub >= d, pltpu.roll(xr, d, 0), 0.0)
            si = jnp.where(sub >= d, pltpu.roll(xi, d, 0), 0.0)
            xr, xi = xr + ar * sr - ai * si, xi + ar * si + ai * sr
            d *= 2
        cin_ref[...] = carry[...]
        cr, ci = carry[:, :hw], carry[:, hw:]
        pwr, pwi = pr_ref[pl.ds(0, S5_ROWS), :], pi_ref[pl.ds(0, S5_ROWS), :]
        for g in range(T // S5_ROWS):
            rows = slice(g * S5_ROWS, (g + 1) * S5_ROWS)
            hr = xr[rows, :] + pwr * cr - pwi * ci
            hi = xi[rows, :] + pwr * ci + pwi * cr
            h_ref[rows, :hw] = hr
            h_ref[rows, hw:] = hi
            cr, ci = hr[S5_ROWS - 1:S5_ROWS, :], hi[S5_ROWS - 1:S5_ROWS, :]
        carry[:, :hw] = cr
        carry[:, hw:] = ci
        y = _dot(h_ref[...], wc_ref[...], NN)
        v = y + d_ref[...] * u
        v_ref[...] = v
        yg_ref[...] = _gelu(v)

    return pl.pallas_call(
        body, name="s5_scan_fwd", grid=(ns, nc),
        in_specs=[pl.BlockSpec((T, S5_SLAB), lambda s, c: (c, s)),
                  pl.BlockSpec((None, S5_SLAB, W), lambda s, c: (s, 0, 0)),
                  pl.BlockSpec((None, W, S5_SLAB), lambda s, c: (s, 0, 0)),
                  pl.BlockSpec((None, T, hw), lambda s, c: (s, 0, 0)),
                  pl.BlockSpec((None, T, hw), lambda s, c: (s, 0, 0)),
                  pl.BlockSpec((1, S5_SLAB), lambda s, c: (0, s))],
        out_specs=[pl.BlockSpec((T, S5_SLAB), lambda s, c: (c, s)),
                   pl.BlockSpec((T, S5_SLAB), lambda s, c: (c, s)),
                   pl.BlockSpec((T, W), lambda s, c: (c, s)),
                   pl.BlockSpec((None, 1, W), lambda s, c: (c, 0, s))],
        out_shape=[jax.ShapeDtypeStruct((S, mix), F32), jax.ShapeDtypeStruct((S, mix), F32),
                   jax.ShapeDtypeStruct((S, ns * W), F32), jax.ShapeDtypeStruct((nc, 1, ns * W), F32)],
        scratch_shapes=[pltpu.VMEM((1, W), F32)],
        compiler_params=_cp(("parallel", "arbitrary")),
    )(proj, wb, wc, pr, pi, dskip)


def _s5_scan_bwd(dv, proj, hs, cin, wb, wc, pr, pi, prr, pir, dskip):
    S = proj.shape[0]
    ns = wb.shape[0]
    W = wb.shape[2]
    hw = W // 2
    T = S5_CHUNK
    nc = S // T
    mix = ns * S5_SLAB

    def body(dv_ref, u_ref, h_ref, cin_ref, wb_ref, wc_ref, pr_ref, pi_ref, prr_ref, pir_ref, d_ref,
             du_ref, dwb_ref, dwc_ref, da_ref, dd_ref, lam_s, carry):
        c = pl.program_id(1)

        @pl.when(c == 0)
        def _():
            carry[...] = jnp.zeros_like(carry)
            dwb_ref[...] = jnp.zeros_like(dwb_ref)
            dwc_ref[...] = jnp.zeros_like(dwc_ref)
            da_ref[...] = jnp.zeros_like(da_ref)
            dd_ref[...] = jnp.zeros_like(dd_ref)

        dy, u = dv_ref[...], u_ref[...]
        dh = _dot(dy, wc_ref[...], NT)
        gr, gi = dh[:, :hw], dh[:, hw:]
        row = lax.broadcasted_iota(jnp.int32, (T, hw), 0)
        sub = row & (S5_ROWS - 1)
        d = 1
        while d < S5_ROWS:
            ar, ai = pr_ref[pl.ds(d - 1, 1), :], -pi_ref[pl.ds(d - 1, 1), :]
            sr = jnp.where(sub < S5_ROWS - d, pltpu.roll(gr, T - d, 0), 0.0)
            si = jnp.where(sub < S5_ROWS - d, pltpu.roll(gi, T - d, 0), 0.0)
            gr, gi = gr + ar * sr - ai * si, gi + ar * si + ai * sr
            d *= 2
        lr, li = carry[:, :hw], carry[:, hw:]
        pwr, pwi = prr_ref[pl.ds(T - S5_ROWS, S5_ROWS), :], -pir_ref[pl.ds(T - S5_ROWS, S5_ROWS), :]
        for g in reversed(range(T // S5_ROWS)):
            rows = slice(g * S5_ROWS, (g + 1) * S5_ROWS)
            lgr = gr[rows, :] + pwr * lr - pwi * li
            lgi = gi[rows, :] + pwr * li + pwi * lr
            lam_s[rows, :hw] = lgr
            lam_s[rows, hw:] = lgi
            lr, li = lgr[0:1, :], lgi[0:1, :]
        carry[:, :hw] = lr
        carry[:, hw:] = li
        gr, gi = lam_s[:, :hw], lam_s[:, hw:]
        hr, hi = h_ref[:, :hw], h_ref[:, hw:]
        hpr = jnp.where(row >= 1, pltpu.roll(hr, 1, 0), cin_ref[:, :hw])
        hpi = jnp.where(row >= 1, pltpu.roll(hi, 1, 0), cin_ref[:, hw:])
        da_ref[:, :hw] += jnp.sum(hpr * gr + hpi * gi, axis=0, keepdims=True)
        da_ref[:, hw:] += jnp.sum(hpr * gi - hpi * gr, axis=0, keepdims=True)
        lam = lam_s[...]
        du_ref[...] = _dot(lam, wb_ref[...], NT) + dy * d_ref[...]
        dwb_ref[...] += _dot(u, lam, TN)
        dwc_ref[...] += _dot(h_ref[...], dy, TN)
        dd_ref[...] += jnp.sum(dy * u, axis=0, keepdims=True)

    def rc(c):
        return nc - 1 - c

    return pl.pallas_call(
        body, name="s5_scan_bwd", grid=(ns, nc),
        in_specs=[pl.BlockSpec((T, S5_SLAB), lambda s, c: (rc(c), s)),
                  pl.BlockSpec((T, S5_SLAB), lambda s, c: (rc(c), s)),
                  pl.BlockSpec((T, W), lambda s, c: (rc(c), s)),
                  pl.BlockSpec((None, 1, W), lambda s, c: (rc(c), 0, s)),
                  pl.BlockSpec((None, S5_SLAB, W), lambda s, c: (s, 0, 0)),
                  pl.BlockSpec((None, W, S5_SLAB), lambda s, c: (s, 0, 0)),
                  pl.BlockSpec((None, T, hw), lambda s, c: (s, 0, 0)),
                  pl.BlockSpec((None, T, hw), lambda s, c: (s, 0, 0)),
                  pl.BlockSpec((None, T, hw), lambda s, c: (s, 0, 0)),
                  pl.BlockSpec((None, T, hw), lambda s, c: (s, 0, 0)),
                  pl.BlockSpec((1, S5_SLAB), lambda s, c: (0, s))],
        out_specs=[pl.BlockSpec((T, S5_SLAB), lambda s, c: (rc(c), s)),
                   pl.BlockSpec((None, S5_SLAB, W), lambda s, c: (s, 0, 0)),
                   pl.BlockSpec((None, W, S5_SLAB), lambda s, c: (s, 0, 0)),
                   pl.BlockSpec((None, 1, W), lambda s, c: (s, 0, 0)),
                   pl.BlockSpec((1, S5_SLAB), lambda s, c: (0, s))],
        out_shape=[jax.ShapeDtypeStruct((S, mix), F32), jax.ShapeDtypeStruct(wb.shape, F32),
                   jax.ShapeDtypeStruct(wc.shape, F32), jax.ShapeDtypeStruct((ns, 1, W), F32),
                   jax.ShapeDtypeStruct((1, mix), F32)],
        scratch_shapes=[pltpu.VMEM((T, W), F32), pltpu.VMEM((1, W), F32)],
        compiler_params=_cp(("parallel", "arbitrary")),
    )(dv, proj, hs, cin, wb, wc, pr, pi, prr, pir, dskip)


def _s5_glu_bwd(dcat, yg, z):
    S, mix = yg.shape
    tr = _tile(S, ROW_TILE, 8)

    def body(do_ref, yg_ref, z_ref, dz_ref, dy_ref, db_ref):
        i = pl.program_id(0)

        @pl.when(i == 0)
        def _():
            db_ref[...] = jnp.zeros_like(db_ref)

        do, yg_, sz = do_ref[...], yg_ref[...], _sigmoid(z_ref[...])
        dz = do * yg_ * sz * (1.0 - sz)
        dz_ref[...] = dz
        dy_ref[...] = do * sz
        db_ref[...] += jnp.sum(dz, axis=0, keepdims=True)

    blk = pl.BlockSpec((tr, mix), lambda i: (i, 0))
    return pl.pallas_call(
        body, name="s5_glu_bwd", grid=(S // tr,),
        in_specs=[blk, blk, blk], out_specs=[blk, blk, pl.BlockSpec((1, mix), lambda i: (0, 0))],
        out_shape=[jax.ShapeDtypeStruct((S, mix), F32), jax.ShapeDtypeStruct((S, mix), F32),
                   jax.ShapeDtypeStruct((1, mix), F32)],
        compiler_params=_cp(("arbitrary",)),
    )(dcat, yg, z)


def _rows_down(x, j):
    return x if j == 0 else pltpu.roll(x, j, 0)


def _conv_rows(xe, w_ref, n):
    c = None
    for j in range(GDN_CONV):
        term = w_ref[pl.ds(GDN_CONV - 1 - j, 1), :] * _rows_down(xe, j)[8:8 + n, :]
        c = term if c is None else c + term
    return c


def _gdn_prep(proj, blk0, nblk, convw, norm, scale, name):
    S = proj.shape[0]
    tr = _tile(S, CONV_TILE, 8)
    nb8 = tr // 8

    def body(x_ref, xb_ref, w_ref, o_ref):
        i = pl.program_id(1)
        xe = jnp.concatenate([jnp.where(i == 0, 0.0, xb_ref[...]), x_ref[...]], axis=0)
        c = _conv_rows(xe, w_ref, tr)
        s = c * _sigmoid(c)
        if norm:
            s = s * lax.rsqrt(jnp.sum(s * s, axis=-1, keepdims=True) + EPS) * scale
        o_ref[...] = s

    return pl.pallas_call(
        body, name=name, grid=(nblk, S // tr),
        in_specs=[pl.BlockSpec((tr, HEAD), lambda j, i: (i, blk0 + j)),
                  pl.BlockSpec((8, HEAD), lambda j, i: (jnp.maximum(i * nb8 - 1, 0), blk0 + j)),
                  pl.BlockSpec((GDN_CONV, HEAD), lambda j, i: (0, j))],
        out_specs=pl.BlockSpec((tr, HEAD), lambda j, i: (i, j)),
        out_shape=jax.ShapeDtypeStruct((S, nblk * HEAD), F32),
        compiler_params=_cp(("parallel", "parallel")),
    )(proj, proj, convw)


def _gdn_prep_bwd(proj, blk0, nblk, convw, dout, norm, scale, name):
    S = proj.shape[0]
    tr = _tile(S, CONV_TILE, 8)
    nb8 = tr // 8
    last8 = S // 8 - 1
    nrow = S // tr

    def body(x_ref, xb_ref, xa_ref, w_ref, d_ref, da_ref, dx_ref, dw_ref):
        i = pl.program_id(1)

        @pl.when(i == 0)
        def _():
            dw_ref[...] = jnp.zeros_like(dw_ref)

        xe = jnp.concatenate([jnp.where(i == 0, 0.0, xb_ref[...]), x_ref[...], xa_ref[...]], axis=0)
        de = jnp.concatenate([d_ref[...], da_ref[...]], axis=0)
        n = tr + 8
        c = _conv_rows(xe, w_ref, n)
        sg = _sigmoid(c)
        s = c * sg
        if norm:
            r = lax.rsqrt(jnp.sum(s * s, axis=-1, keepdims=True) + EPS)
            ds = scale * r * (de - s * (r * r) * jnp.sum(de * s, axis=-1, keepdims=True))
        else:
            ds = de
        dc = ds * (sg + c * sg * (1.0 - sg))
        rowi = lax.broadcasted_iota(jnp.int32, (n, HEAD), 0)
        dc = jnp.where((i == nrow - 1) & (rowi >= tr), 0.0, dc)
        dct = dc[:tr, :]
        dx = None
        for j in range(GDN_CONV):
            tap = pl.ds(GDN_CONV - 1 - j, 1)
            up = dct if j == 0 else pltpu.roll(dc, n - j, 0)[:tr, :]
            term = w_ref[tap, :] * up
            dx = term if dx is None else dx + term
            dw_ref[tap, :] += jnp.sum(dct * _rows_down(xe, j)[8:8 + tr, :], axis=0, keepdims=True)
        dx_ref[...] = dx

    return pl.pallas_call(
        body, name=name, grid=(nblk, nrow),
        in_specs=[pl.BlockSpec((tr, HEAD), lambda j, i: (i, blk0 + j)),
                  pl.BlockSpec((8, HEAD), lambda j, i: (jnp.maximum(i * nb8 - 1, 0), blk0 + j)),
                  pl.BlockSpec((8, HEAD), lambda j, i: (jnp.minimum((i + 1) * nb8, last8), blk0 + j)),
                  pl.BlockSpec((GDN_CONV, HEAD), lambda j, i: (0, j)),
                  pl.BlockSpec((tr, HEAD), lambda j, i: (i, j)),
                  pl.BlockSpec((8, HEAD), lambda j, i: (jnp.minimum((i + 1) * nb8, last8), j))],
        out_specs=[pl.BlockSpec((tr, HEAD), lambda j, i: (i, j)),
                   pl.BlockSpec((GDN_CONV, HEAD), lambda j, i: (0, j))],
        out_shape=[jax.ShapeDtypeStruct((S, nblk * HEAD), F32),
                   jax.ShapeDtypeStruct((GDN_CONV, nblk * HEAD), F32)],
        compiler_params=_cp(("parallel", "arbitrary")),
    )(proj, proj, proj, convw, dout, dout)


def _gdn_gates(pg, alog, dtb):
    S = pg.shape[0]

    def body(a_ref, b_ref, al_ref, dt_ref, gc_ref, be_ref):
        g = -jnp.exp(al_ref[...]) * _softplus(a_ref[...] + dt_ref[...])
        rowm = lax.broadcasted_iota(jnp.int32, g.shape, 0) & (GDN_CHUNK - 1)
        c = g
        d = 1
        while d < GDN_CHUNK:
            c = c + jnp.where(rowm >= d, pltpu.roll(c, d, 0), 0.0)
            d *= 2
        gc_ref[...] = c
        be_ref[...] = _sigmoid(b_ref[...])

    blk = pl.BlockSpec((S, 128), lambda i: (0, 0))
    vec = pl.BlockSpec((1, 128), lambda i: (0, 0))
    return pl.pallas_call(
        body, name="gdn_gates", grid=(1,),
        in_specs=[blk, pl.BlockSpec((S, 128), lambda i: (0, 1)), vec, vec],
        out_specs=[blk, blk],
        out_shape=[jax.ShapeDtypeStruct((S, 128), F32)] * 2,
        compiler_params=_cp(("arbitrary",)),
    )(pg, pg, alog, dtb)


def _gdn_gates_bwd(pg, alog, dtb, dgc, dbeta):
    S = pg.shape[0]

    def body(a_ref, b_ref, al_ref, dt_ref, dgc_ref, dbe_ref, dpa_ref, dpb_ref, dal_ref, ddt_ref):
        rowm = lax.broadcasted_iota(jnp.int32, (S, 128), 0) & (GDN_CHUNK - 1)
        c = dgc_ref[...]
        d = 1
        while d < GDN_CHUNK:
            c = c + jnp.where(rowm < GDN_CHUNK - d, pltpu.roll(c, S - d, 0), 0.0)
            d *= 2
        xv = a_ref[...] + dt_ref[...]
        ea = jnp.exp(al_ref[...])
        g = -ea * _softplus(xv)
        dx = c * (-ea) * _sigmoid(xv)
        dpa_ref[...] = dx
        dal_ref[...] = jnp.sum(c * g, axis=0, keepdims=True)
        ddt_ref[...] = jnp.sum(dx, axis=0, keepdims=True)
        be = _sigmoid(b_ref[...])
        dpb_ref[...] = dbe_ref[...] * be * (1.0 - be)

    blk = pl.BlockSpec((S, 128), lambda i: (0, 0))
    blk1 = pl.BlockSpec((S, 128), lambda i: (0, 1))
    vec = pl.BlockSpec((1, 128), lambda i: (0, 0))
    dpa, dpb, dal, ddt = pl.pallas_call(
        body, name="gdn_gates_bwd", grid=(1,),
        in_specs=[blk, blk1, vec, vec, blk, blk],
        out_specs=[blk, blk, vec, vec],
        out_shape=[jax.ShapeDtypeStruct((S, 128), F32)] * 2 + [jax.ShapeDtypeStruct((1, 128), F32)] * 2,
        compiler_params=_cp(("arbitrary",)),
    )(pg, pg, alog, dtb, dgc, dbeta)
    return jnp.concatenate([dpa, dpb], axis=1), dal, ddt


def _gdn_pre(qs, ks, vs, gcs, grs, betas):
    C = GDN_CHUNK
    n = len(qs)
    r = lax.broadcasted_iota(jnp.int32, (C, C), 0)
    c_ = lax.broadcasted_iota(jnp.int32, (C, C), 1)
    lower, strict = r >= c_, r > c_
    eye = jnp.where(r == c_, 1.0, 0.0)
    decs = [jnp.exp(jnp.where(lower, gcs[i] - grs[i], -jnp.inf)) for i in range(n)]
    kbs = [ks[i] * betas[i] for i in range(n)]
    vbs = [vs[i] * betas[i] for i in range(n)]
    lmats = [jnp.where(strict, _dot(kbs[i], ks[i], NT) * decs[i], 0.0) for i in range(n)]
    amats = [jnp.where(lower, _dot(qs[i], ks[i], NT) * decs[i], 0.0) for i in range(n)]
    pks = [-lm for lm in lmats]
    tinvs = [eye + pk for pk in pks]
    for _ in range(5):
        pks = [_dotf(pk, pk, NN) for pk in pks]
        tinvs = [tv + _dotf(tv, pk, NN) for tv, pk in zip(tinvs, pks)]
    es = [jnp.exp(gc) for gc in gcs]
    glasts = [gc[C - 1:C, :] for gc in gcs]
    fs = [jnp.exp(gl - gc) for gl, gc in zip(glasts, gcs)]
    gls = [jnp.exp(gl) for gl in glasts]
    us = [_dotf(tinvs[i], vbs[i], NN) for i in range(n)]
    ws = [_dotf(tinvs[i], kbs[i] * es[i], NN) for i in range(n)]
    return [dict(lower=lower, strict=strict, dec=decs[i], kb=kbs[i], vb=vbs[i], lmat=lmats[i], tinv=tinvs[i],
                 e=es[i], f=fs[i], gl=gls[i], u=us[i], w=ws[i], amat=amats[i], qd=qs[i] * es[i],
                 kd=ks[i] * fs[i]) for i in range(n)]


def _gdn_heads_per_step(H):
    return max(d for d in (1, 2, 3, 4) if H % d == 0)


def _gdn_chunk_fwd(q, k, v, gcol, grow, bcol):
    S = q.shape[0]
    H, NC = gcol.shape[0], gcol.shape[1]
    C = GDN_CHUNK
    hb = _gdn_heads_per_step(H)

    def body(q_ref, k_ref, v_ref, gc_ref, gr_ref, b_ref, o_ref, st_ref, state):
        n = pl.program_id(1)

        @pl.when(n == 0)
        def _():
            state[...] = jnp.zeros_like(state)

        cols = [slice(i * HEAD, (i + 1) * HEAD) for i in range(hb)]
        ps = _gdn_pre([q_ref[:, c] for c in cols], [k_ref[:, c] for c in cols], [v_ref[:, c] for c in cols],
                      [gc_ref[i] for i in range(hb)], [gr_ref[i] for i in range(hb)],
                      [b_ref[i] for i in range(hb)])
        s0s = [state[i] for i in range(hb)]
        vns = [ps[i]['u'] - _dot(ps[i]['w'], s0s[i], NN) for i in range(hb)]
        outs = [_dot(ps[i]['qd'], s0s[i], NN) + _dot(ps[i]['amat'], vns[i], NN) for i in range(hb)]
        news = [s0s[i] * ps[i]['gl'] + _dot(ps[i]['kd'], vns[i], TN) for i in range(hb)]
        for i in range(hb):
            st_ref[i] = s0s[i]
            o_ref[:, cols[i]] = outs[i]
            state[i] = news[i]

    tok = pl.BlockSpec((C, hb * HEAD), lambda h, n: (n, h))
    col = pl.BlockSpec((hb, None, C, 1), lambda h, n: (h, n, 0, 0))
    rowb = pl.BlockSpec((hb, None, 1, C), lambda h, n: (h, n, 0, 0))
    return pl.pallas_call(
        body, name="gdn_chunk_fwd", grid=(H // hb, NC),
        in_specs=[tok, tok, tok, col, rowb, col],
        out_specs=[tok, pl.BlockSpec((hb, None, HEAD, HEAD), lambda h, n: (h, n, 0, 0))],
        out_shape=[jax.ShapeDtypeStruct((S, H * HEAD), F32), jax.ShapeDtypeStruct((H, NC, HEAD, HEAD), F32)],
        scratch_shapes=[pltpu.VMEM((hb, HEAD, HEAD), F32)],
        compiler_params=_cp(("parallel", "arbitrary")),
    )(q, k, v, gcol, grow, bcol)


def _gdn_chunk_bwd(q, k, v, gcol, grow, bcol, st, do):
    S = q.shape[0]
    H, NC = gcol.shape[0], gcol.shape[1]
    C = GDN_CHUNK
    hb = _gdn_heads_per_step(H)

    def body(q_ref, k_ref, v_ref, gc_ref, gr_ref, b_ref, st_ref, do_ref,
             dq_ref, dk_ref, dv_ref, dgc_ref, dbe_ref, dstate):
        n = pl.program_id(1)

        @pl.when(n == 0)
        def _():
            dstate[...] = jnp.zeros_like(dstate)

        R = range(hb)
        cols = [slice(i * HEAD, (i + 1) * HEAD) for i in R]
        qs, ks, vs = [q_ref[:, c] for c in cols], [k_ref[:, c] for c in cols], [v_ref[:, c] for c in cols]
        betas = [b_ref[i] for i in R]
        ps = _gdn_pre(qs, ks, vs, [gc_ref[i] for i in R], [gr_ref[i] for i in R], betas)
        lower, strict = ps[0]['lower'], ps[0]['strict']
        s0s, dos, ds1s = [st_ref[i] for i in R], [do_ref[:, c] for c in cols], [dstate[i] for i in R]
        vns = [ps[i]['u'] - _dot(ps[i]['w'], s0s[i], NN) for i in R]
        dvns = [_dot(ps[i]['amat'], dos[i], TN) + _dot(ps[i]['kd'], ds1s[i], NN) for i in R]
        damats = [jnp.where(lower, _dot(dos[i], vns[i], NT), 0.0) for i in R]
        dqds = [_dot(dos[i], s0s[i], NT) for i in R]
        dkds = [_dot(vns[i], ds1s[i], NT) for i in R]
        dgls = [jnp.sum(s0s[i] * ds1s[i], keepdims=True) for i in R]
        ds0s = [ps[i]['gl'] * ds1s[i] + _dot(ps[i]['qd'], dos[i], TN) - _dot(ps[i]['w'], dvns[i], TN) for i in R]
        dws = [-_dot(dvns[i], s0s[i], NT) for i in R]
        dvbs = [_dotf(ps[i]['tinv'], dvns[i], TN) for i in R]
        dkgs = [_dotf(ps[i]['tinv'], dws[i], TN) for i in R]
        dls = [-jnp.where(strict, _dotf(dvbs[i], ps[i]['u'], NT) + _dotf(dkgs[i], ps[i]['w'], NT), 0.0) for i in R]
        dkks = [dls[i] * ps[i]['dec'] for i in R]
        dqks = [damats[i] * ps[i]['dec'] for i in R]
        ms = [dls[i] * ps[i]['lmat'] + damats[i] * ps[i]['amat'] for i in R]
        dkbs = [_dot(dkks[i], ks[i], NN) + dkgs[i] * ps[i]['e'] for i in R]
        dks = [_dot(dkks[i], ps[i]['kb'], TN) + _dot(dqks[i], qs[i], TN) + dkds[i] * ps[i]['f'] + dkbs[i] * betas[i]
               for i in R]
        dqs = [_dot(dqks[i], ks[i], NN) + dqds[i] * ps[i]['e'] for i in R]
        ones = jnp.ones((C, HEAD), F32)
        colsums = [_dotf(ms[i], ones, TN)[:, 0:1] for i in R]
        rowi = lax.broadcasted_iota(jnp.int32, (C, 1), 0)
        for i in R:
            p = ps[i]
            de = (jnp.sum(dkgs[i] * p['kb'], axis=-1, keepdims=True)
                  + jnp.sum(dqds[i] * qs[i], axis=-1, keepdims=True))
            df = jnp.sum(dkds[i] * ks[i], axis=-1, keepdims=True)
            dgc = jnp.sum(ms[i], axis=-1, keepdims=True) - colsums[i] + de * p['e'] - df * p['f']
            dlast = jnp.sum(df * p['f'], keepdims=True) + dgls[i] * p['gl']
            dgc_ref[i] = dgc + jnp.where(rowi == C - 1, dlast, 0.0)
            dbe_ref[i] = (jnp.sum(dkbs[i] * ks[i], axis=-1, keepdims=True)
                          + jnp.sum(dvbs[i] * vs[i], axis=-1, keepdims=True))
            dstate[i] = ds0s[i]
            dq_ref[:, cols[i]] = dqs[i]
            dk_ref[:, cols[i]] = dks[i]
            dv_ref[:, cols[i]] = dvbs[i] * betas[i]

    def rn(n):
        return NC - 1 - n

    tok = pl.BlockSpec((C, hb * HEAD), lambda h, n: (rn(n), h))
    col = pl.BlockSpec((hb, None, C, 1), lambda h, n: (h, rn(n), 0, 0))
    rowb = pl.BlockSpec((hb, None, 1, C), lambda h, n: (h, rn(n), 0, 0))
    return pl.pallas_call(
        body, name="gdn_chunk_bwd", grid=(H // hb, NC),
        in_specs=[tok, tok, tok, col, rowb, col,
                  pl.BlockSpec((hb, None, HEAD, HEAD), lambda h, n: (h, rn(n), 0, 0)), tok],
        out_specs=[tok, tok, tok, col, col],
        out_shape=[jax.ShapeDtypeStruct((S, H * HEAD), F32)] * 3
        + [jax.ShapeDtypeStruct((H, NC, C, 1), F32)] * 2,
        scratch_shapes=[pltpu.VMEM((hb, HEAD, HEAD), F32)],
        compiler_params=_cp(("parallel", "arbitrary")),
    )(q, k, v, gcol, grow, bcol, st, do)


def _gdn_onorm(o, proj, gate_blk, w, H):
    S = o.shape[0]
    tr = _tile(S, CONV_TILE, 8)

    def body(o_ref, g_ref, w_ref, out_ref):
        ov, gv = o_ref[...], g_ref[...]
        r = lax.rsqrt(jnp.mean(ov * ov, axis=-1, keepdims=True) + EPS)
        out_ref[...] = (ov * r * w_ref[...]) * (gv * _sigmoid(gv))

    return pl.pallas_call(
        body, name="gdn_onorm", grid=(S // tr, H),
        in_specs=[pl.BlockSpec((tr, HEAD), lambda i, h: (i, h)),
                  pl.BlockSpec((tr, HEAD), lambda i, h: (i, gate_blk + h)),
                  pl.BlockSpec((1, HEAD), lambda i, h: (0, 0))],
        out_specs=pl.BlockSpec((tr, HEAD), lambda i, h: (i, h)),
        out_shape=jax.ShapeDtypeStruct((S, H * HEAD), F32),
        compiler_params=_cp(("parallel", "parallel")),
    )(o, proj, w)


def _gdn_onorm_bwd(dcat, o, proj, gate_blk, w, H):
    S = o.shape[0]
    tr = _tile(S, CONV_TILE, 8)

    def body(d_ref, o_ref, g_ref, w_ref, do_ref, dg_ref, dw_ref):
        i, h = pl.program_id(0), pl.program_id(1)

        @pl.when((i == 0) & (h == 0))
        def _():
            dw_ref[...] = jnp.zeros_like(dw_ref)

        dm, ov, gv, wv = d_ref[...], o_ref[...], g_ref[...], w_ref[...]
        r = lax.rsqrt(jnp.mean(ov * ov, axis=-1, keepdims=True) + EPS)
        oh = ov * r
        sg = gv * _sigmoid(gv)
        dy = dm * sg
        t = dy * wv
        do_ref[...] = r * (t - oh * jnp.mean(t * oh, axis=-1, keepdims=True))
        dg_ref[...] = dm * (oh * wv) * _silu_grad(gv)
        dw_ref[...] += jnp.sum(dy * oh, axis=0, keepdims=True)

    tok = pl.BlockSpec((tr, HEAD), lambda i, h: (i, h))
    vec = pl.BlockSpec((1, HEAD), lambda i, h: (0, 0))
    return pl.pallas_call(
        body, name="gdn_onorm_bwd", grid=(S // tr, H),
        in_specs=[tok, tok, pl.BlockSpec((tr, HEAD), lambda i, h: (i, gate_blk + h)), vec],
        out_specs=[tok, tok, vec],
        out_shape=[jax.ShapeDtypeStruct((S, H * HEAD), F32)] * 2 + [jax.ShapeDtypeStruct((1, HEAD), F32)],
        compiler_params=_cp(("arbitrary", "arbitrary")),
    )(dcat, o, proj, w)


def _lanes_to_heads(a, H):
    return a[:, :H].T


def _heads_to_lanes(a):
    H = a.shape[0]
    return jnp.pad(a.T, ((0, 0), (0, 128 - H)))


def _take_cols(segs, a, b):
    out, off = [], 0
    for sg in segs:
        w = sg.shape[-1]
        lo, hi = max(a, off), min(b, off + w)
        if lo < hi:
            out.append(sg[..., lo - off:hi - off])
        off += w
    return out


def _pad_cols(pieces):
    m = jnp.concatenate(pieces, axis=-1)
    return jnp.pad(m, ((0, 0), (0, 128 - m.shape[-1])))


def _pad_lanes(v):
    return jnp.pad(v.reshape(1, -1), ((0, 0), (0, 128 - v.shape[-1])))


def _s5_layer_fwd(a, w, cfg):
    proj = _mm(a, w['w_in'], name="s5_in")
    wb, wc, _ = w['prep']
    pr, pi, prr, pir = w['tables']
    v, yg, hs, cin = _s5_scan_fwd(proj, wb, wc, pr, pi, w['d_skip'])
    z, mix = _mm(yg, w['w_glu'], name="s5_glu", extras=[(yg, 'ij'), (w['b_glu'], 'j')],
                 epi=lambda acc, y, b: (acc + b, y * _sigmoid(acc + b)), out_dtypes=(F32, F32))
    return proj, mix, dict(v=v, yg=yg, hs=hs, cin=cin, z=z)


def _s5_layer_bwd(a, w, proj, sv, dcat, dmemq, cfg):
    wb, wc, _ = w['prep']
    pr, pi, prr, pir = w['tables']
    dz, dyg1, db_glu = _s5_glu_bwd(dcat, sv['yg'], sv['z'])
    dw_glu = _mm(sv['yg'], dz, name="s5_dwglu", ta=True)
    dv = _mm(dz, w['w_glu'], name="s5_dyg", tb=True, extras=[(dyg1, 'ij'), (sv['v'], 'ij')],
             epi=lambda acc, d1, vv: ((acc + d1) * _gelu_grad(vv),))
    du, dwb, dwc, da, dd = _s5_scan_bwd(dv, proj, sv['hs'], sv['cin'], wb, wc, pr, pi, prr, pir, w['d_skip'])
    dproj = jnp.concatenate([du, dmemq], axis=1)
    dw_in = _mm(a, dproj, name="s5_dwin", ta=True)
    da_in = _mm(dproj, w['w_in'], name="s5_da", tb=True)
    dlre, dlim, dldt, dbre, dbim, dcre, dcim = w['prep_vjp']((dwb, dwc, da))
    grads = dict(w_in=dw_in, w_glu=dw_glu, b_glu=db_glu[0], d_skip=dd[0], lam_re=dlre, lam_im=dlim,
                 log_dt=dldt, b_re=dbre, b_im=dbim, c_re=dcre, c_im=dcim)
    return da_in, grads


def _gdn_relayout(a, H, NC):
    t = _lanes_to_heads(a, H).reshape(H, NC, GDN_CHUNK)
    return t[..., None], t[:, :, None, :]


def _gdn_layer_fwd(a, w, cfg):
    H, MIX, S = cfg['H'], cfg['MIX'], a.shape[0]
    NC = S // GDN_CHUNK
    proj = _mm(a, w['w_main'], name="gdn_in")
    pg = _mm(a, w['w_gate'], name="gdn_in_gates")
    cw = w['conv_w']
    q = _gdn_prep(proj, 0, H, cw[:, :MIX], True, HEAD ** -0.5, "gdn_prep_q")
    k = _gdn_prep(proj, H, H, cw[:, MIX:2 * MIX], True, 1.0, "gdn_prep_k")
    v = _gdn_prep(proj, 2 * H, H, cw[:, 2 * MIX:], False, 1.0, "gdn_prep_v")
    gc, beta = _gdn_gates(pg, w['a_log'], w['dt_bias'])
    gcol, grow = _gdn_relayout(gc, H, NC)
    bcol, _ = _gdn_relayout(beta, H, NC)
    o, st = _gdn_chunk_fwd(q, k, v, gcol, grow, bcol)
    mix = _gdn_onorm(o, proj, 3 * H, w['o_norm'], H)
    return proj, mix, dict(pg=pg, q=q, k=k, v=v, gcol=gcol, grow=grow, bcol=bcol, o=o, st=st)


def _gdn_layer_bwd(a, w, proj, sv, dcat, dmemq, cfg):
    H, MIX, S = cfg['H'], cfg['MIX'], a.shape[0]
    cw = w['conv_w']
    do, dgate, donorm = _gdn_onorm_bwd(dcat, sv['o'], proj, 3 * H, w['o_norm'], H)
    dq, dk, dv, dgcol, dbcol = _gdn_chunk_bwd(sv['q'], sv['k'], sv['v'], sv['gcol'], sv['grow'], sv['bcol'],
                                              sv['st'], do)
    dgc = _heads_to_lanes(dgcol.reshape(H, S))
    dbeta = _heads_to_lanes(dbcol.reshape(H, S))
    dpg, dalog, ddtb = _gdn_gates_bwd(sv['pg'], w['a_log'], w['dt_bias'], dgc, dbeta)
    dxq, dwq = _gdn_prep_bwd(proj, 0, H, cw[:, :MIX], dq, True, HEAD ** -0.5, "gdn_prep_bwd_q")
    dxk, dwk = _gdn_prep_bwd(proj, H, H, cw[:, MIX:2 * MIX], dk, True, 1.0, "gdn_prep_bwd_k")
    dxv, dwv = _gdn_prep_bwd(proj, 2 * H, H, cw[:, 2 * MIX:], dv, False, 1.0, "gdn_prep_bwd_v")
    dproj = jnp.concatenate([dxq, dxk, dxv, dgate, dmemq], axis=1)
    dw_main = _mm(a, dproj, name="gdn_dwmain", ta=True)
    dw_gate = _mm(a, dpg, name="gdn_dwgate", ta=True)
    da1 = _mm(dpg, w['w_gate'], name="gdn_da_gates", tb=True)
    da_in = _mm(dproj, w['w_main'], name="gdn_da", tb=True, extras=[(da1, 'ij')], epi=lambda acc, e: (acc + e,))
    grads = dict(w_main=dw_main, w_gate=dw_gate, conv_w=jnp.concatenate([dwq, dwk, dwv], axis=1),
                 a_log=dalog[0, :H], dt_bias=ddtb[0, :H], o_norm=donorm[0])
    return da_in, grads


def _fox_layer_fwd(a, w, cfg):
    H = cfg['H']
    proj = _mm(a, w['w_main'], name="fox_in")
    pg = _mm(a, w['w_gate'], name="fox_in_gates")
    cf = _fox_gates(pg, w['b_f'])
    cfh = _lanes_to_heads(cf, H)
    cfq, cfk = cfh[:, :, None], cfh[:, None, :]
    o, lse = _fox_fwd(proj, cfq, cfk, H)
    return proj, o, dict(pg=pg, cfq=cfq, cfk=cfk, lse=lse)


def _fox_layer_bwd(a, w, proj, sv, dcat, dmemq, cfg):
    H = cfg['H']
    rowdot = _fox_bwd_rowdot(proj, sv['cfq'], sv['cfk'], sv['lse'], dcat, H)
    dq, dk, dv, dck = _fox_bwd(proj, sv['cfq'], sv['cfk'], rowdot, sv['lse'], dcat, H)
    dpg, dbf = _fox_gates_bwd(sv['pg'], w['b_f'], _heads_to_lanes(dck[:, 0, :]))
    dproj = jnp.concatenate([dq, dk, dv, dmemq], axis=1)
    dw_main = _mm(a, dproj, name="fox_dwmain", ta=True)
    dw_gate = _mm(a, dpg, name="fox_dwgate", ta=True)
    da1 = _mm(dpg, w['w_gate'], name="fox_da_gates", tb=True)
    da_in = _mm(dproj, w['w_main'], name="fox_da", tb=True, extras=[(da1, 'ij')], epi=lambda acc, e: (acc + e,))
    grads = dict(w_main=dw_main, w_gate=dw_gate, b_f=dbf[0, :H])
    return da_in, grads


_LAYER_FWD = (_s5_layer_fwd, _gdn_layer_fwd, _fox_layer_fwd)
_LAYER_BWD = (_s5_layer_bwd, _gdn_layer_bwd, _fox_layer_bwd)


def _mixer_weights(kind, j, fw, p, cfg, after):
    H, MIX, MW = cfg['H'], cfg['MIX'], cfg['MW']
    if kind == 0:
        params = tuple(p[n][j] for n in ('s5_lam_re', 's5_lam_im', 's5_log_dt', 's5_b_re', 's5_b_im',
                                         's5_c_re', 's5_c_im'))
        prep, prep_vjp = jax.vjp(_s5_prep, *params)
        prep = (prep[0].astype(MXU_DTYPE), prep[1].astype(MXU_DTYPE), prep[2])
        tables = _s5_tables(*params[:3])
        return dict(w_in=fw.get('s5_w_in', j, after), w_glu=fw.get('s5_w_glu', j, after),
                    b_glu=fw.get('s5_b_glu', j, after), d_skip=fw.get('s5_d_skip', j, after).reshape(1, MIX),
                    prep=prep, prep_vjp=prep_vjp, tables=tables)
    if kind == 1:
        segs = fw.get('gdn_w_in', j, after)
        c0 = 4 * MIX
        total = c0 + 2 * H + MW
        w_main = jnp.concatenate(_take_cols(segs, 0, c0) + _take_cols(segs, c0 + 2 * H, total), axis=1)
        w_gate = jnp.concatenate([_pad_cols(_take_cols(segs, c0, c0 + H)),
                                  _pad_cols(_take_cols(segs, c0 + H, c0 + 2 * H))], axis=1)
        return dict(w_main=w_main, w_gate=w_gate, conv_w=fw.get('gdn_conv_w', j, after),
                    a_log=_pad_lanes(p['gdn_a_log'][j]), dt_bias=_pad_lanes(p['gdn_dt_bias'][j]),
                    o_norm=p['gdn_o_norm'][j].reshape(1, HEAD))
    segs = fw.get('fox_w_in', j, after)
    c0 = 3 * MIX
    total = c0 + H + MW
    w_main = jnp.concatenate(_take_cols(segs, 0, c0) + _take_cols(segs, c0 + H, total), axis=1)
    w_gate = _pad_cols(_take_cols(segs, c0, c0 + H))
    return dict(w_main=w_main, w_gate=w_gate, b_f=_pad_lanes(p['fox_b_f'][j]))


class _Weights:
    def __init__(self, resolve):
        self._resolve, self._have = resolve, {}

    def get(self, name, layer, after):
        if name not in self._have:
            self._have[name] = self._resolve(name, after)
        return self._have[name][layer]


def _local_step(p, fw, cfg, on_grad=None):
    H, MIX, MW, MH, depth = cfg['H'], cfg['MIX'], cfg['MW'], cfg['MH'], cfg['depth']
    x, mem, target = p['x'], p['mem'], p['loss_target']
    q_blk = {0: MIX // HEAD, 1: 4 * MIX // HEAD, 2: 3 * MIX // HEAD}
    zero = jnp.zeros((), F32)
    tok = [zero]

    def told(name, layer, value):
        if on_grad is not None:
            tok[0] = tok[0] + on_grad(name, layer, value)
        return value

    mem_n = _rms_fwd(mem, p['mem_norm'], MXU_DTYPE, "mem_rms")
    w_kv = fw.get('w_mem_kv', 0, mem_n)
    mkv = _mm(mem_n, w_kv, name="mem_kv")

    h = x
    saved = []
    for i in range(depth):
        kind, j = i % 3, i // 3
        a = _rms_fwd(h, p['norm1'][i], MXU_DTYPE, "rms1")
        w = _mixer_weights(kind, j, fw, p, cfg, a)
        proj, mix, sv = _LAYER_FWD[kind](a, w, cfg)
        read = _mem_fwd(proj, q_blk[kind], mkv, MH)
        cat = jnp.concatenate([mix, read], axis=1)
        w_out, w_up = fw.get('w_out', i, proj), fw.get('w_up', i, proj)
        h1 = _mm(cat, w_out, name="out_proj", extras=[(h, 'ij')], epi=lambda acc, r: (acc + r,))
        a2 = _rms_fwd(h1, p['norm2'][i], MXU_DTYPE, "rms2")
        u = _mm(a2, w_up, name="mlp_up")
        w_down = fw.get('w_down', i, h1)
        h2 = _mm(u, w_down, name="mlp_down", a_pro=_relu2, extras=[(h1, 'ij')],
                 epi=lambda acc, r: (acc + r,))
        saved.append(dict(w=w, h=h, a=a, proj=proj, sv=sv, cat=cat, h1=h1, a2=a2, u=u,
                          w_out=w_out, w_up=w_up, w_down=w_down))
        h = h2

    loss, dh, dnorm_f = _loss_head(h, p['norm_f'], target)

    g = {n: None for n in WEIGHTS}
    g['norm_f'] = dnorm_f[0]
    per_layer = {n: [None] * depth for n in ('norm1', 'norm2', 'w_out', 'w_up', 'w_down')}
    mix_grads = {0: {}, 1: {}, 2: {}}
    big = {0: (('s5_w_in', 'w_in'), ('s5_w_glu', 'w_glu')), 1: (), 2: ()}
    dmkv = None
    for i in reversed(range(depth)):
        kind, j = i % 3, i // 3
        s = saved[i]
        w = s['w']
        du = _mm(dh, s['w_down'], name="mlp_ddown", tb=True, extras=[(s['u'], 'ij')],
                 epi=lambda acc, uu: (acc * (2.0 * jnp.maximum(uu, 0.0)),))
        per_layer['w_down'][i] = told('w_down', i, _mm(s['u'], dh, name="mlp_dwdown", ta=True, a_pro=_relu2))
        per_layer['w_up'][i] = told('w_up', i, _mm(s['a2'], du, name="mlp_dwup", ta=True))
        da2 = _mm(du, s['w_up'], name="mlp_dup", tb=True)
        dh1, dn2 = _rms_bwd(s['h1'], p['norm2'][i] + tok[0], da2, dh, "rms2_bwd")
        per_layer['norm2'][i] = dn2[0]
        dcat = _mm(dh1, s['w_out'], name="out_dproj", tb=True)
        per_layer['w_out'][i] = told('w_out', i, _mm(s['cat'], dh1, name="out_dw", ta=True))
        dmemq, dmkv_i = _mem_bwd(s['proj'], q_blk[kind], mkv, dcat, MIX // HEAD, MH)
        dmkv = dmkv_i if dmkv is None else dmkv + dmkv_i
        da, mg = _LAYER_BWD[kind](s['a'], w, s['proj'], s['sv'], dcat, dmemq, cfg)
        mix_grads[kind][j] = mg
        for name, key in big[kind]:
            told(name, j, mg[key])
        c0 = 4 * MIX
        if kind == 1:
            mg['segs'] = told('gdn_w_in', j, [mg['w_main'][:, :c0], mg['w_gate'][:, :H],
                                               mg['w_gate'][:, 128:128 + H], mg['w_main'][:, c0:]])
        c0 = 3 * MIX
        if kind == 2:
            mg['segs'] = told('fox_w_in', j, [mg['w_main'][:, :c0], mg['w_gate'][:, :H], mg['w_main'][:, c0:]])
        dh, dn1 = _rms_bwd(s['h'], p['norm1'][i] + tok[0], da, dh1, "rms1_bwd")
        per_layer['norm1'][i] = dn1[0]
    for n in ('norm1', 'norm2'):
        g[n] = jnp.stack(per_layer[n])
    for n in ('w_out', 'w_up', 'w_down'):
        g[n] = per_layer[n]

    g['w_mem_kv'] = told('w_mem_kv', 0, _mm(mem_n, dmkv, name="mem_dwkv", ta=True))
    dmem_n = _mm(dmkv, w_kv, name="mem_dn", tb=True)
    _, dmn = _rms_bwd(mem, p['mem_norm'] + tok[0], dmem_n, None, "mem_rms_bwd")
    g['mem_norm'] = dmn[0]

    def layers(kind, key):
        return [mix_grads[kind][j][key] for j in sorted(mix_grads[kind])]

    g['s5_w_in'] = layers(0, 'w_in')
    g['s5_w_glu'] = layers(0, 'w_glu')
    for n in ('b_glu', 'd_skip', 'lam_re', 'lam_im', 'log_dt', 'b_re', 'b_im', 'c_re', 'c_im'):
        g['s5_' + n] = jnp.stack(layers(0, n))
    g['gdn_w_in'] = layers(1, 'segs')
    for n in ('conv_w', 'a_log', 'dt_bias', 'o_norm'):
        g['gdn_' + n] = jnp.stack(layers(1, n))
    g['fox_w_in'] = layers(2, 'segs')
    g['fox_b_f'] = jnp.stack(layers(2, 'b_f'))
    return loss, dh, g


AG_ORDER = ('w_mem_kv', 's5_w_in', 's5_w_glu', 'w_out', 'w_up', 'w_down', 'gdn_w_in', 'fox_w_in')


def _gather_begin(p, me):
    handles, tok = {}, jnp.zeros((), F32)
    for name in AG_ORDER:
        w = p[name]
        xs = w.astype(MXU_DTYPE).reshape(-1, w.shape[-1])
        handle, t = _ag_start(xs, "ag_start_" + name)
        handles[name] = (handle, xs)
        tok = tok + t
    vec = _gather(_pack_small([p[n] for n in VECTOR_SHARDED], 16), me, "all_gather_vectors").reshape(4, -1)
    vectors, off = {}, 0
    for n in VECTOR_SHARDED:
        sz = p[n].size
        stacked = vec[:, off:off + sz].reshape((4,) + p[n].shape)
        ax = SHARD_AXIS[n]
        t = jnp.moveaxis(stacked, 0, ax)
        shp = list(t.shape)
        vectors[n] = t.reshape(shp[:ax] + [shp[ax] * shp[ax + 1]] + shp[ax + 2:])
        off += sz
    return handles, vectors, tok


def _gather_end(name, after, p, me, handles, vectors):
    if name in vectors:
        return vectors[name]
    handle, xs = handles[name]
    got = _ag_wait(handle, after, "ag_wait_" + name)
    got = _ag_forward(lax.dynamic_update_slice(got, xs[None], (me, 0, 0)), "ag_forward_" + name)
    w = p[name]
    if w.ndim == 2:
        return [got.reshape(4 * w.shape[0], w.shape[1])]
    L, r, n = w.shape
    if SHARD_AXIS[name] == 1:
        return [got[:, i * r:(i + 1) * r, :].reshape(4 * r, n) for i in range(L)]
    blocks = [[got[s, i * r:(i + 1) * r, :] for s in range(4)] for i in range(L)]
    if name == 'w_up':
        return [jnp.concatenate(b, axis=1) for b in blocks]
    return blocks


def _shard_blocks(name, value, p):
    shp = p[name].shape
    r, n = shp[-2], shp[-1]
    if SHARD_AXIS[name] == len(shp) - 2:
        return value.reshape(4, r, n)
    segs = value if isinstance(value, list) else [value]
    return jnp.stack([jnp.concatenate(_take_cols(segs, s * n, (s + 1) * n), axis=1) for s in range(4)])


def kernel(x, mem, mem_norm, w_mem_kv, norm1, w_out, norm2, w_up, w_down, norm_f, s5_w_in, s5_lam_re, s5_lam_im, s5_log_dt, s5_b_re, s5_b_im, s5_c_re, s5_c_im, s5_d_skip, s5_w_glu, s5_b_glu, gdn_w_in, gdn_conv_w, gdn_a_log, gdn_dt_bias, gdn_o_norm, fox_w_in, fox_b_f, loss_target, m_mem_norm, m_w_mem_kv, m_norm1, m_w_out, m_norm2, m_w_up, m_w_down, m_norm_f, m_s5_w_in, m_s5_lam_re, m_s5_lam_im, m_s5_log_dt, m_s5_b_re, m_s5_b_im, m_s5_c_re, m_s5_c_im, m_s5_d_skip, m_s5_w_glu, m_s5_b_glu, m_gdn_w_in, m_gdn_conv_w, m_gdn_a_log, m_gdn_dt_bias, m_gdn_o_norm, m_fox_w_in, m_fox_b_f, v_mem_norm, v_w_mem_kv, v_norm1, v_w_out, v_norm2, v_w_up, v_w_down, v_norm_f, v_s5_w_in, v_s5_lam_re, v_s5_lam_im, v_s5_log_dt, v_s5_b_re, v_s5_b_im, v_s5_c_re, v_s5_c_im, v_s5_d_skip, v_s5_w_glu, v_s5_b_glu, v_gdn_w_in, v_gdn_conv_w, v_gdn_a_log, v_gdn_dt_bias, v_gdn_o_norm, v_fox_w_in, v_fox_b_f):
    args = locals()
    p = {n: args[n] for n in WEIGHTS}
    mom = {n: args['m_' + n] for n in WEIGHTS}
    var = {n: args['v_' + n] for n in WEIGHTS}
    S, D = x.shape[1], x.shape[2]
    MW = w_mem_kv.shape[1] // 2
    MIX = D - MW
    cfg = dict(H=MIX // HEAD, MIX=MIX, MW=MW, MH=MW // HEAD, depth=norm1.shape[0])
    p.update(x=x.reshape(S, D), mem=mem.reshape(mem.shape[1], D), loss_target=loss_target.reshape(S, D))
    c = lax.axis_index("c")
    me = 2 * lax.axis_index("x") + lax.axis_index("y")
    place = dict(c=c, c_idx=c.astype(jnp.int32).reshape(1), me_idx=me.astype(jnp.int32).reshape(1))

    handles, vectors, tok = _gather_begin(p, me)
    p['mem_norm'] = mem_norm + tok
    fw = _Weights(lambda name, after: _gather_end(name, after, p, me, handles, vectors))

    pending = {}

    def on_grad(name, layer, value):
        handle, t = _rs_begin(_shard_blocks(name, value, p), place, "%s_%d" % (name, layer))
        pending[name, layer] = handle
        return t

    loss, dx, g = _local_step(p, fw, cfg, on_grad)
    p['mem_norm'] = mem_norm

    grads = {}
    for name in MATMUL_WEIGHTS:
        shp = p[name].shape
        layers = [_rs_end(pending[name, i], dx, place, "%s_%d" % (name, i))
                  for i in range(1 if len(shp) == 2 else shp[0])]
        grads[name] = layers[0] if len(shp) == 2 else jnp.stack(layers)

    parts = []
    for name in VECTOR_SHARDED:
        ax = SHARD_AXIS[name]
        shp = list(g[name].shape)
        t = g[name].reshape(shp[:ax] + [4, shp[ax] // 4] + shp[ax + 1:])
        parts.append(jnp.moveaxis(t, ax, 0).reshape(4, -1))
    flat = jnp.concatenate(parts, axis=1)
    flat = jnp.pad(flat, ((0, 0), (0, 16 * LANES - flat.shape[1]))).reshape(4, 16, LANES)
    handle, _ = _rs_begin(flat, place, "vectors")
    red = _rs_end(handle, dx, place, "vectors").reshape(-1)
    off = 0
    for name in VECTOR_SHARDED:
        grads[name] = red[off:off + p[name].size].reshape(p[name].shape)
        off += p[name].size

    n_small = sum(p[n].size for n in REPLICATED)
    rows = -(-n_small // LANES // 8) * 8
    small = _all_reduce_small(_pack_small([g[n] for n in REPLICATED], rows), "all_reduce_small").reshape(-1)
    off = 0
    for n in REPLICATED:
        grads[n] = small[off:off + p[n].size].reshape(p[n].shape)
        off += p[n].size

    delta, new_m, new_v = {}, {}, {}
    for n in SHARD_AXIS:
        shp = p[n].shape
        two_d = (-1, shp[-1])
        d, nm, nv = _adamw(p[n].reshape(two_d), grads[n].reshape(two_d), mom[n].reshape(two_d),
                           var[n].reshape(two_d), "adamw_" + n)
        delta[n], new_m[n], new_v[n] = d.reshape(shp), nm.reshape(shp), nv.reshape(shp)
    d, nm, nv = _adamw(*[_pack_small([src[n] for n in REPLICATED], rows) for src in (p, grads, mom, var)],
                       "adamw_small")
    d, nm, nv = d.reshape(-1), nm.reshape(-1), nv.reshape(-1)
    off = 0
    for n in REPLICATED:
        sz, shp = p[n].size, p[n].shape
        delta[n], new_m[n], new_v[n] = (d[off:off + sz].reshape(shp), nm[off:off + sz].reshape(shp),
                                        nv[off:off + sz].reshape(shp))
        off += sz

    total = lax.psum(loss[0, 0], ("x", "y", "c"))
    return (total, dx.reshape(x.shape), *[grads[n] for n in WEIGHTS], *[delta[n] for n in WEIGHTS],
            *[new_m[n] for n in WEIGHTS], *[new_v[n] for n in WEIGHTS])
```

```python
import math

import jax
import jax.numpy as jnp
import numpy as np
from jax import lax
from jax.experimental import pallas as pl
from jax.experimental.pallas import tpu as pltpu

F32 = jnp.float32
MXU_DTYPE = jnp.bfloat16
EPS = 1e-6
HEAD = 128
S5_GROUP = 16
S5_STATE = 64
S5_SLAB = 256
S5_CHUNK = 128
S5_ROWS = 8
GDN_CHUNK = 64
GDN_CONV = 4
LANES = 1024
VMEM_LIMIT_BYTES = 56 * 1024 * 1024
MESH = pl.DeviceIdType.MESH
RS_PAYLOAD = jnp.bfloat16
HBM_SPEC = pl.BlockSpec(memory_space=pltpu.HBM)
SEM_SPEC = pl.BlockSpec(memory_space=pltpu.SEMAPHORE)
SPLIT_EFFECT = pltpu.SideEffectType.DATAFLOW_SIDE_EFFECTING

ADAM_LR, ADAM_B1, ADAM_B2, ADAM_EPS, ADAM_WD, ADAM_STEP = 0.001, 0.9, 0.999, 1e-08, 0.01, 10

MM_TM, MM_TN, MM_TK = 1024, 1024, 1024
ROW_TILE = 256
FOX_TILE = 512
MEM_TILE = 512
CONV_TILE = 1024

NN = (((1,), (0,)), ((), ()))
NT = (((1,), (1,)), ((), ()))
TN = (((0,), (0,)), ((), ()))

WEIGHTS = ['mem_norm', 'w_mem_kv', 'norm1', 'w_out', 'norm2', 'w_up', 'w_down', 'norm_f', 's5_w_in',
           's5_lam_re', 's5_lam_im', 's5_log_dt', 's5_b_re', 's5_b_im', 's5_c_re', 's5_c_im', 's5_d_skip',
           's5_w_glu', 's5_b_glu', 'gdn_w_in', 'gdn_conv_w', 'gdn_a_log', 'gdn_dt_bias', 'gdn_o_norm',
           'fox_w_in', 'fox_b_f']
SHARD_AXIS = {'w_mem_kv': 0, 'w_out': 1, 'w_up': 2, 'w_down': 1, 's5_w_in': 1, 's5_d_skip': 1,
              's5_w_glu': 1, 's5_b_glu': 1, 'gdn_w_in': 2, 'gdn_conv_w': 2, 'fox_w_in': 2}
MATMUL_WEIGHTS = ['w_mem_kv', 'w_out', 'w_up', 'w_down', 's5_w_in', 's5_w_glu', 'gdn_w_in', 'fox_w_in']
VECTOR_SHARDED = ['s5_d_skip', 's5_b_glu', 'gdn_conv_w']
REPLICATED = [n for n in WEIGHTS if n not in SHARD_AXIS]


def _tile(dim, target, align=128):
    if dim <= target:
        return dim
    t = (target // align) * align
    while t >= align:
        if dim % t == 0:
            return t
        t -= align
    return dim


def _cp(sem=None, **kw):
    return pltpu.CompilerParams(dimension_semantics=sem, vmem_limit_bytes=VMEM_LIMIT_BYTES, **kw)


def _dot(a, b, dims):
    return lax.dot_general(a.astype(MXU_DTYPE), b.astype(MXU_DTYPE), dims, preferred_element_type=F32)


def _dotf(a, b, dims):
    return lax.dot_general(a, b, dims, precision=lax.Precision.HIGHEST, preferred_element_type=F32)


def _sigmoid(x):
    return 1.0 / (1.0 + jnp.exp(-x))


def _softplus(x):
    return jnp.maximum(x, 0.0) + jnp.log(1.0 + jnp.exp(-jnp.abs(x)))


def _relu2(x):
    r = jnp.maximum(x, 0.0)
    return r * r


_GELU_C = math.sqrt(2.0 / math.pi)


def _gelu(x):
    return 0.5 * x * (1.0 + jnp.tanh(_GELU_C * (x + 0.044715 * x * x * x)))


def _gelu_grad(x):
    t = jnp.tanh(_GELU_C * (x + 0.044715 * x * x * x))
    return 0.5 * (1.0 + t) + 0.5 * x * (1.0 - t * t) * _GELU_C * (1.0 + 3.0 * 0.044715 * x * x)


def _silu_grad(x):
    s = _sigmoid(x)
    return s + x * s * (1.0 - s)


def _mm(a, b, *, name, ta=False, tb=False, extras=(), epi=None, out_dtypes=(F32,)):
    K, M = a.shape if ta else a.shape[::-1]
    N = b.shape[0] if tb else b.shape[1]
    assert (b.shape[1] if tb else b.shape[0]) == K, (a.shape, b.shape, ta, tb)
    tm, tn, tk = _tile(M, MM_TM), _tile(N, MM_TN), _tile(K, MM_TK)
    nk = K // tk
    n_ex, n_out = len(extras), len(out_dtypes)
    dims = TN if ta else (NT if tb else NN)

    def body(*refs):
        a_ref, b_ref = refs[0], refs[1]
        ex = refs[2:2 + n_ex]
        outs = refs[2 + n_ex:2 + n_ex + n_out]
        acc = refs[-1]
        k = pl.program_id(2)

        @pl.when(k == 0)
        def _():
            acc[...] = jnp.zeros_like(acc)

        acc[...] += _dot(a_ref[...], b_ref[...], dims)

        @pl.when(k == nk - 1)
        def _():
            res = acc[...]
            vals = epi(res, *[e[...] for e in ex]) if epi is not None else (res,)
            for o, v in zip(outs, vals):
                o[...] = v.astype(o.dtype)

    if ta:
        a_spec = pl.BlockSpec((tk, tm), lambda i, j, k: (k, i))
    else:
        a_spec = pl.BlockSpec((tm, tk), lambda i, j, k: (i, k))
    if tb:
        b_spec = pl.BlockSpec((tn, tk), lambda i, j, k: (j, k))
    else:
        b_spec = pl.BlockSpec((tk, tn), lambda i, j, k: (k, j))
    ex_specs, ex_arrays = [], []
    for arr, kind in extras:
        if kind == 'ij':
            ex_specs.append(pl.BlockSpec((tm, tn), lambda i, j, k: (i, j)))
            ex_arrays.append(arr)
        else:
            ex_specs.append(pl.BlockSpec((1, tn), lambda i, j, k: (0, j)))
            ex_arrays.append(arr.reshape(1, N))
    outs = pl.pallas_call(
        body, name=name, grid=(M // tm, N // tn, nk),
        in_specs=[a_spec, b_spec] + ex_specs,
        out_specs=[pl.BlockSpec((tm, tn), lambda i, j, k: (i, j)) for _ in out_dtypes],
        out_shape=[jax.ShapeDtypeStruct((M, N), dt) for dt in out_dtypes],
        scratch_shapes=[pltpu.VMEM((tm, tn), F32)],
        compiler_params=_cp(("parallel", "parallel", "arbitrary")),
    )(a, b, *ex_arrays)
    return outs[0] if n_out == 1 else tuple(outs)


def _rms_fwd(x, g, out_dtype, name):
    S, D = x.shape
    tr = _tile(S, ROW_TILE, 8)

    def body(x_ref, g_ref, o_ref):
        xv = x_ref[...]
        r = lax.rsqrt(jnp.mean(xv * xv, axis=-1, keepdims=True) + EPS)
        o_ref[...] = (xv * r * g_ref[...]).astype(o_ref.dtype)

    return pl.pallas_call(
        body, name=name, grid=(S // tr,),
        in_specs=[pl.BlockSpec((tr, D), lambda i: (i, 0)), pl.BlockSpec((1, D), lambda i: (0, 0))],
        out_specs=pl.BlockSpec((tr, D), lambda i: (i, 0)),
        out_shape=jax.ShapeDtypeStruct((S, D), out_dtype),
        compiler_params=_cp(("parallel",)),
    )(x, g.reshape(1, D))


def _rms_bwd(x, g, dy, res, name):
    S, D = x.shape
    tr = _tile(S, ROW_TILE, 8)
    has_res = res is not None

    def body(*refs):
        if has_res:
            x_ref, g_ref, dy_ref, res_ref, dx_ref, dg_ref, dx16_ref = refs
        else:
            x_ref, g_ref, dy_ref, dx_ref, dg_ref, dx16_ref = refs
        i = pl.program_id(0)

        @pl.when(i == 0)
        def _():
            dg_ref[...] = jnp.zeros_like(dg_ref)

        xv, d = x_ref[...], dy_ref[...].astype(F32)
        r = lax.rsqrt(jnp.mean(xv * xv, axis=-1, keepdims=True) + EPS)
        xh = xv * r
        t = d * g_ref[...]
        dx = r * (t - xh * jnp.mean(t * xh, axis=-1, keepdims=True))
        if has_res:
            dx = dx + res_ref[...]
        dx_ref[...] = dx
        dx16_ref[...] = dx.astype(dx16_ref.dtype)
        dg_ref[...] += jnp.sum(d * xh, axis=0, keepdims=True)

    row = pl.BlockSpec((tr, D), lambda i: (i, 0))
    vec = pl.BlockSpec((1, D), lambda i: (0, 0))
    ins = [x, g.reshape(1, D), dy] + ([res] if has_res else [])
    return pl.pallas_call(
        body, name=name, grid=(S // tr,),
        in_specs=[row, vec, row] + ([row] if has_res else []),
        out_specs=[row, vec, row],
        out_shape=[jax.ShapeDtypeStruct((S, D), F32), jax.ShapeDtypeStruct((1, D), F32),
                   jax.ShapeDtypeStruct((S, D), MXU_DTYPE)],
        compiler_params=_cp(("arbitrary",)),
    )(*ins)


def _loss_head(h, g, target):
    S, D = h.shape
    tr = _tile(S, ROW_TILE, 8)

    def body(h_ref, g_ref, t_ref, loss_ref, dh_ref, dg_ref, dh16_ref):
        i = pl.program_id(0)

        @pl.when(i == 0)
        def _():
            loss_ref[...] = jnp.zeros_like(loss_ref)
            dg_ref[...] = jnp.zeros_like(dg_ref)

        xv = h_ref[...]
        gv = g_ref[...]
        r = lax.rsqrt(jnp.mean(xv * xv, axis=-1, keepdims=True) + EPS)
        xh = xv * r
        err = xh * gv - t_ref[...]
        part = 0.5 * jnp.sum(jnp.mean(err * err, axis=-1, keepdims=True), axis=0, keepdims=True)
        loss_ref[...] += jnp.broadcast_to(part, loss_ref.shape)
        d = err * (1.0 / D)
        t = d * gv
        dh = r * (t - xh * jnp.mean(t * xh, axis=-1, keepdims=True))
        dh_ref[...] = dh
        dh16_ref[...] = dh.astype(dh16_ref.dtype)
        dg_ref[...] += jnp.sum(d * xh, axis=0, keepdims=True)

    row = pl.BlockSpec((tr, D), lambda i: (i, 0))
    vec = pl.BlockSpec((1, D), lambda i: (0, 0))
    return pl.pallas_call(
        body, name="loss_head", grid=(S // tr,),
        in_specs=[row, vec, row],
        out_specs=[pl.BlockSpec((8, 128), lambda i: (0, 0)), row, vec, row],
        out_shape=[jax.ShapeDtypeStruct((8, 128), F32), jax.ShapeDtypeStruct((S, D), F32),
                   jax.ShapeDtypeStruct((1, D), F32), jax.ShapeDtypeStruct((S, D), MXU_DTYPE)],
        compiler_params=_cp(("arbitrary",)),
    )(h, g.reshape(1, D), target)


def _adamw(w, g, m, v, name):
    R, C = w.shape
    tr = _tile(R, max(8, (1 << 19) // max(C, 1) // 8 * 8), 8)
    c1 = 1.0 / (1.0 - ADAM_B1 ** ADAM_STEP)
    c2 = 1.0 / (1.0 - ADAM_B2 ** ADAM_STEP)

    def body(w_ref, g_ref, m_ref, v_ref, d_ref, nm_ref, nv_ref):
        gv = g_ref[...]
        nm = ADAM_B1 * m_ref[...] + (1.0 - ADAM_B1) * gv
        nv = ADAM_B2 * v_ref[...] + (1.0 - ADAM_B2) * (gv * gv)
        d_ref[...] = -ADAM_LR * ((nm * c1) / (jnp.sqrt(nv * c2) + ADAM_EPS) + ADAM_WD * w_ref[...])
        nm_ref[...] = nm
        nv_ref[...] = nv

    blk = pl.BlockSpec((tr, C), lambda i: (i, 0))
    return pl.pallas_call(
        body, name=name, grid=(R // tr,),
        in_specs=[blk] * 4, out_specs=[blk] * 3,
        out_shape=[jax.ShapeDtypeStruct((R, C), F32)] * 3,
        compiler_params=_cp(("parallel",)),
    )(w, g, m, v)


def _place():
    x, y, c = lax.axis_index("x"), lax.axis_index("y"), lax.axis_index("c")
    chips = [(1 - x, y), (x, 1 - y), (1 - x, 1 - y)]
    return x, y, c, chips


def _in_hbm(a):
    return pltpu.with_memory_space_constraint(a, pltpu.HBM)


def _ag_start(xs, name):
    r, n = xs.shape
    half = r // 2

    def body(x_ref, land_ref, send_sems, recv_sems, x_thru, land_thru, token):
        x, y, c, chips = _place()
        rows = pl.ds(c * half, half)
        for j, (cx, cy) in enumerate(chips):
            pltpu.make_async_remote_copy(
                src_ref=x_ref.at[rows, :], dst_ref=land_ref.at[2 * x + y, rows, :], send_sem=send_sems.at[j],
                recv_sem=recv_sems.at[j], device_id=(cx, cy, c), device_id_type=MESH).start()
        token[...] = jnp.zeros_like(token)

    sems = pltpu.SemaphoreType.DMA((3,))
    out = pl.pallas_call(
        body, name=name,
        out_shape=(sems, sems, pltpu.HBM(xs.shape, xs.dtype), pltpu.HBM((4, r, n), xs.dtype),
                   jax.ShapeDtypeStruct((8, 128), F32)),
        in_specs=(HBM_SPEC, HBM_SPEC),
        out_specs=(SEM_SPEC, SEM_SPEC, HBM_SPEC, HBM_SPEC, pl.BlockSpec(memory_space=pltpu.VMEM)),
        input_output_aliases={0: 2, 1: 3},
        compiler_params=pltpu.CompilerParams(has_side_effects=SPLIT_EFFECT),
    )(_in_hbm(xs), _in_hbm(lax.empty((4, r, n), xs.dtype)))
    return out[:4], out[4][0, 0]


def _ag_wait(handle, after, name):
    send_sems, recv_sems, xs, land = handle
    r, n = xs.shape
    half = r // 2

    def body(x_ref, land_ref, send_sems, recv_sems, after_ref, x_out, land_out):
        x, y, c, chips = _place()
        rows = pl.ds(c * half, half)
        for j, (cx, cy) in enumerate(chips):
            cp = pltpu.make_async_remote_copy(
                src_ref=x_ref.at[rows, :], dst_ref=land_ref.at[2 * cx + cy, rows, :], send_sem=send_sems.at[j],
                recv_sem=recv_sems.at[j], device_id=(cx, cy, c), device_id_type=MESH)
            cp.wait_send()
            cp.wait_recv()

    return pl.pallas_call(
        body, name=name,
        out_shape=(pltpu.HBM(xs.shape, xs.dtype), pltpu.HBM(land.shape, land.dtype)),
        in_specs=(HBM_SPEC, HBM_SPEC, SEM_SPEC, SEM_SPEC, pl.BlockSpec(memory_space=pl.ANY)),
        out_specs=(HBM_SPEC, HBM_SPEC),
        input_output_aliases={0: 0, 1: 1},
        compiler_params=pltpu.CompilerParams(has_side_effects=SPLIT_EFFECT),
    )(xs, land, send_sems, recv_sems, after)[1]


def _ag_forward(got, name):
    _, r, n = got.shape
    half = r // 2

    def body(g_ref, out_ref, send_sems, recv_sems):
        x, y, c, chips = _place()
        sibling = (x, y, 1 - c)
        sent = []
        for j, (cx, cy) in enumerate(chips):
            piece = out_ref.at[2 * cx + cy, pl.ds(c * half, half), :]
            cp = pltpu.make_async_remote_copy(src_ref=piece, dst_ref=piece, send_sem=send_sems.at[j],
                                              recv_sem=recv_sems.at[j], device_id=sibling, device_id_type=MESH)
            cp.start()
            sent.append(cp)
        for j, (cx, cy) in enumerate(chips):
            piece = out_ref.at[2 * cx + cy, pl.ds((1 - c) * half, half), :]
            pltpu.make_async_remote_copy(src_ref=piece, dst_ref=piece, send_sem=send_sems.at[j],
                                         recv_sem=recv_sems.at[j], device_id=sibling, device_id_type=MESH).wait_recv()
        for cp in sent:
            cp.wait_send()

    return pl.pallas_call(
        body, name=name,
        in_specs=[pl.BlockSpec(memory_space=pl.ANY)],
        out_specs=pl.BlockSpec(memory_space=pl.ANY),
        out_shape=jax.ShapeDtypeStruct(got.shape, got.dtype),
        input_output_aliases={0: 0},
        scratch_shapes=[pltpu.SemaphoreType.DMA((3,)), pltpu.SemaphoreType.DMA((3,))],
    )(got)


def _all_gather_chips(xs, name):
    r, n = xs.shape
    half = r // 2

    def body(x_ref, out_ref, send_sems, recv_sems):
        x, y, c, chips = _place()
        me = 2 * x + y
        sibling = (x, y, 1 - c)

        def piece(chip, hc):
            return out_ref.at[chip, pl.ds(hc * half, half), :]

        def copy(k, src, dst, to):
            return pltpu.make_async_remote_copy(src_ref=src, dst_ref=dst, send_sem=send_sems.at[k],
                                                recv_sem=recv_sems.at[k], device_id=to, device_id_type=MESH)

        src = x_ref.at[pl.ds(c * half, half), :]
        first = [copy(j, src, piece(me, c), (cx, cy, c)) for j, (cx, cy) in enumerate(chips)]
        for cp in first:
            cp.start()
        passed = []
        for j, (cx, cy) in enumerate(chips):
            got = piece(2 * cx + cy, c)
            copy(j, got, got, (cx, cy, c)).wait_recv()
            fwd = copy(3 + j, got, got, sibling)
            fwd.start()
            passed.append(fwd)
        for j, (cx, cy) in enumerate(chips):
            got = piece(2 * cx + cy, 1 - c)
            copy(3 + j, got, got, sibling).wait_recv()
        for cp in first + passed:
            cp.wait_send()

    return pl.pallas_call(
        body, name=name,
        in_specs=[pl.BlockSpec(memory_space=pl.ANY)],
        out_specs=pl.BlockSpec(memory_space=pl.ANY),
        out_shape=jax.ShapeDtypeStruct((4, r, n), xs.dtype),
        scratch_shapes=[pltpu.SemaphoreType.DMA((6,)), pltpu.SemaphoreType.DMA((6,))],
    )(xs)


def _gather(xs, me, name):
    return lax.dynamic_update_slice(_all_gather_chips(xs, name), xs[None], (me, 0, 0))


def _rs_swap_halves(g, name):
    _, r, n = g.shape
    half = r // 2

    def body(g_ref, out_ref, send_sem, recv_sem):
        x, y, c, _ = _place()
        cp = pltpu.make_async_remote_copy(
            src_ref=g_ref.at[:, pl.ds((1 - c) * half, half), :], dst_ref=out_ref,
            send_sem=send_sem, recv_sem=recv_sem, device_id=(x, y, 1 - c), device_id_type=MESH)
        cp.start()
        cp.wait()

    return pl.pallas_call(
        body, name=name,
        in_specs=[pl.BlockSpec(memory_space=pl.ANY)],
        out_specs=pl.BlockSpec(memory_space=pl.ANY),
        out_shape=jax.ShapeDtypeStruct((4, half, n), g.dtype),
        scratch_shapes=[pltpu.SemaphoreType.DMA, pltpu.SemaphoreType.DMA],
    )(g)


def _rs_add_halves(g, got, c_idx, name):
    _, r, n = g.shape
    half = r // 2
    tr = _tile(half, max(16, (1 << 19) // n // 16 * 16), 16)
    nb = half // tr

    def body(c_ref, g_ref, o_ref, out_ref, out16_ref):
        sm = g_ref[...] + o_ref[...]
        out_ref[...] = sm
        out16_ref[...] = sm.astype(out16_ref.dtype)

    blk = pl.BlockSpec((None, tr, n), lambda s, i, c: (s, i, 0))
    return pl.pallas_call(
        body, name=name,
        grid_spec=pltpu.PrefetchScalarGridSpec(
            num_scalar_prefetch=1, grid=(4, nb),
            in_specs=[pl.BlockSpec((None, tr, n), lambda s, i, c: (s, c[0] * nb + i, 0)), blk],
            out_specs=[blk, blk]),
        out_shape=[jax.ShapeDtypeStruct((4, half, n), F32), jax.ShapeDtypeStruct((4, half, n), RS_PAYLOAD)],
        compiler_params=_cp(("parallel", "parallel")),
    )(c_idx, g, got)


def _rs_start(p32, p16, name):
    _, h, n = p16.shape

    def body(p32_ref, p16_ref, l16_ref, l32_ref, send_sems, recv_sems, p32_t, p16_t, l16_t, l32_t, token):
        x, y, c, chips = _place()
        for j, (cx, cy) in enumerate(chips):
            for k, pc in enumerate((c, 1 - c)):
                pltpu.make_async_remote_copy(
                    src_ref=p16_ref.at[2 * cx + cy], dst_ref=l16_ref.at[c, j], send_sem=send_sems.at[3 * k + j],
                    recv_sem=recv_sems.at[3 * k + j], device_id=(cx, cy, pc), device_id_type=MESH).start()
        pltpu.make_async_remote_copy(
            src_ref=p32_ref.at[2 * x + y], dst_ref=l32_ref, send_sem=send_sems.at[6], recv_sem=recv_sems.at[6],
            device_id=(x, y, 1 - c), device_id_type=MESH).start()
        token[...] = jnp.zeros_like(token)

    sems = pltpu.SemaphoreType.DMA((7,))
    l16 = lax.empty((2, 3, h, n), p16.dtype)
    l32 = lax.empty((h, n), F32)
    out = pl.pallas_call(
        body, name=name,
        out_shape=(sems, sems, pltpu.HBM(p32.shape, F32), pltpu.HBM(p16.shape, p16.dtype),
                   pltpu.HBM(l16.shape, l16.dtype), pltpu.HBM(l32.shape, F32), jax.ShapeDtypeStruct((8, 128), F32)),
        in_specs=(HBM_SPEC,) * 4,
        out_specs=(SEM_SPEC, SEM_SPEC) + (HBM_SPEC,) * 4 + (pl.BlockSpec(memory_space=pltpu.VMEM),),
        input_output_aliases={0: 2, 1: 3, 2: 4, 3: 5},
        compiler_params=pltpu.CompilerParams(has_side_effects=SPLIT_EFFECT),
    )(_in_hbm(p32), _in_hbm(p16), _in_hbm(l16), _in_hbm(l32))
    return out[:6], out[6][0, 0]


def _rs_wait(handle, after, name):
    send_sems, recv_sems, p32, p16, l16, l32 = handle

    def body(p32_ref, p16_ref, l16_ref, l32_ref, send_sems, recv_sems, after_ref, p32_o, p16_o, l16_o, l32_o):
        x, y, c, chips = _place()
        for j, (cx, cy) in enumerate(chips):
            for k, pc in enumerate((c, 1 - c)):
                cp = pltpu.make_async_remote_copy(
                    src_ref=p16_ref.at[2 * cx + cy], dst_ref=l16_ref.at[pc, j], send_sem=send_sems.at[3 * k + j],
                    recv_sem=recv_sems.at[3 * k + j], device_id=(cx, cy, pc), device_id_type=MESH)
                cp.wait_send()
                cp.wait_recv()
        cp = pltpu.make_async_remote_copy(
            src_ref=p32_ref.at[2 * x + y], dst_ref=l32_ref, send_sem=send_sems.at[6], recv_sem=recv_sems.at[6],
            device_id=(x, y, 1 - c), device_id_type=MESH)
        cp.wait_send()
        cp.wait_recv()

    out = pl.pallas_call(
        body, name=name,
        out_shape=(pltpu.HBM(p32.shape, F32), pltpu.HBM(p16.shape, p16.dtype), pltpu.HBM(l16.shape, l16.dtype),
                   pltpu.HBM(l32.shape, F32)),
        in_specs=(HBM_SPEC,) * 4 + (SEM_SPEC, SEM_SPEC, pl.BlockSpec(memory_space=pl.ANY)),
        out_specs=(HBM_SPEC,) * 4,
        input_output_aliases={0: 0, 1: 1, 2: 2, 3: 3},
        compiler_params=pltpu.CompilerParams(has_side_effects=SPLIT_EFFECT),
    )(p32, p16, l16, l32, send_sems, recv_sems, after)
    return out[0], out[2], out[3]


def _rs_finish(p32, l16, l32, c_idx, me_idx, name):
    _, h, n = p32.shape
    tr = _tile(h, max(16, (1 << 18) // n // 16 * 16), 16)
    nb = h // tr

    def body(c_ref, me_ref, own_ref, sib_ref, a_ref, b_ref, d_ref, out_ref):
        base = jnp.where(pl.program_id(0) == c_ref[0], own_ref[...], sib_ref[...])
        out_ref[...] = ((base + a_ref[...].astype(F32)) + b_ref[...].astype(F32)) + d_ref[...].astype(F32)

    def piece(j):
        return pl.BlockSpec((None, None, tr, n), lambda hc, i, c, me: (hc, j, i, 0))

    return pl.pallas_call(
        body, name=name,
        grid_spec=pltpu.PrefetchScalarGridSpec(
            num_scalar_prefetch=2, grid=(2, nb),
            in_specs=[pl.BlockSpec((None, tr, n), lambda hc, i, c, me: (me[0], i, 0)),
                      pl.BlockSpec((tr, n), lambda hc, i, c, me: (i, 0)),
                      piece(0), piece(1), piece(2)],
            out_specs=pl.BlockSpec((tr, n), lambda hc, i, c, me: (hc * nb + i, 0))),
        out_shape=jax.ShapeDtypeStruct((2 * h, n), F32),
        compiler_params=_cp(("parallel", "parallel")),
    )(c_idx, me_idx, p32, l32, l16, l16, l16)


def _rs_begin(g, place, tag):
    got = _rs_swap_halves(g, "rs_swap_halves")
    p32, p16 = _rs_add_halves(g, got, place['c_idx'], "rs_add_halves")
    return _rs_start(p32, p16, "rs_start_" + tag)


def _rs_end(handle, after, place, tag):
    p32, l16, l32 = _rs_wait(handle, after, "rs_wait_" + tag)
    return _rs_finish(p32, l16, l32, place['c_idx'], place['me_idx'], "rs_finish")


def _all_reduce_small(v, name):
    R, n = v.shape

    def body(v_ref, out_ref, buf, send_sems, recv_sems):
        x, y, c, _ = _place()
        me = 4 * x + 2 * y + c
        buf[me] = v_ref[...]
        copies = []
        for d in range(1, 8):
            dx, dy, dc = (d >> 2) & 1, (d >> 1) & 1, d & 1
            px = x if dx == 0 else 1 - x
            py = y if dy == 0 else 1 - y
            pc = c if dc == 0 else 1 - c
            copies.append(pltpu.make_async_remote_copy(
                src_ref=v_ref, dst_ref=buf.at[me], send_sem=send_sems.at[d - 1], recv_sem=recv_sems.at[d - 1],
                device_id=(px, py, pc), device_id_type=MESH))
        for cp in copies:
            cp.start()
        for d in range(1, 8):
            dx, dy, dc = (d >> 2) & 1, (d >> 1) & 1, d & 1
            px = x if dx == 0 else 1 - x
            py = y if dy == 0 else 1 - y
            pc = c if dc == 0 else 1 - c
            pltpu.make_async_remote_copy(
                src_ref=v_ref, dst_ref=buf.at[4 * px + 2 * py + pc], send_sem=send_sems.at[d - 1],
                recv_sem=recv_sems.at[d - 1], device_id=(px, py, pc), device_id_type=MESH).wait_recv()
        for cp in copies:
            cp.wait_send()
        acc = buf[0]
        for k in range(1, 8):
            acc = acc + buf[k]
        out_ref[...] = acc

    return pl.pallas_call(
        body, name=name,
        in_specs=[pl.BlockSpec(memory_space=pltpu.VMEM)],
        out_specs=pl.BlockSpec(memory_space=pltpu.VMEM),
        out_shape=jax.ShapeDtypeStruct((R, n), F32),
        scratch_shapes=[pltpu.VMEM((8, R, n), F32), pltpu.SemaphoreType.DMA((7,)), pltpu.SemaphoreType.DMA((7,))],
        compiler_params=pltpu.CompilerParams(vmem_limit_bytes=VMEM_LIMIT_BYTES),
    )(v)


def _pack_small(parts, rows):
    flat = jnp.concatenate([a.reshape(-1) for a in parts])
    return jnp.pad(flat, (0, rows * LANES - flat.shape[0])).reshape(rows, LANES)


def _mem_fwd(proj, q_blk, mkv, heads):
    S = proj.shape[0]
    ML = mkv.shape[0]
    t = _tile(S, MEM_TILE, 8)
    scale = HEAD ** -0.5

    def body(q_ref, k_ref, v_ref, o_ref):
        s = _dot(q_ref[...], k_ref[...], NT) * scale
        m = jnp.max(s, axis=-1, keepdims=True)
        e = jnp.exp(s - m)
        p = e / jnp.sum(e, axis=-1, keepdims=True)
        o_ref[...] = _dot(p, v_ref[...], NN)

    return pl.pallas_call(
        body, name="mem_fwd", grid=(S // t, heads),
        in_specs=[pl.BlockSpec((t, HEAD), lambda i, h: (i, q_blk + h)),
                  pl.BlockSpec((ML, HEAD), lambda i, h: (0, h)),
                  pl.BlockSpec((ML, HEAD), lambda i, h: (0, heads + h))],
        out_specs=pl.BlockSpec((t, HEAD), lambda i, h: (i, h)),
        out_shape=jax.ShapeDtypeStruct((S, heads * HEAD), F32),
        compiler_params=_cp(("parallel", "parallel")),
    )(proj, mkv, mkv)


def _mem_bwd(proj, q_blk, mkv, dcat, d_blk, heads):
    S = proj.shape[0]
    ML = mkv.shape[0]
    t = _tile(S, MEM_TILE, 8)
    scale = HEAD ** -0.5

    def body(q_ref, k_ref, v_ref, do_ref, dq_ref, dk_ref, dv_ref):
        i = pl.program_id(1)

        @pl.when(i == 0)
        def _():
            dk_ref[...] = jnp.zeros_like(dk_ref)
            dv_ref[...] = jnp.zeros_like(dv_ref)

        q, k, v, do = q_ref[...], k_ref[...], v_ref[...], do_ref[...]
        s = _dot(q, k, NT) * scale
        m = jnp.max(s, axis=-1, keepdims=True)
        e = jnp.exp(s - m)
        p = e / jnp.sum(e, axis=-1, keepdims=True)
        dp = _dot(do, v, NT)
        ds = p * (dp - jnp.sum(p * dp, axis=-1, keepdims=True))
        dq_ref[...] = _dot(ds, k, NN) * scale
        dk_ref[...] += _dot(ds, q, TN) * scale
        dv_ref[...] += _dot(p, do, TN)

    dq, dk, dv = pl.pallas_call(
        body, name="mem_bwd", grid=(heads, S // t),
        in_specs=[pl.BlockSpec((t, HEAD), lambda h, i: (i, q_blk + h)),
                  pl.BlockSpec((ML, HEAD), lambda h, i: (0, h)),
                  pl.BlockSpec((ML, HEAD), lambda h, i: (0, heads + h)),
                  pl.BlockSpec((t, HEAD), lambda h, i: (i, d_blk + h))],
        out_specs=[pl.BlockSpec((t, HEAD), lambda h, i: (i, h)),
                   pl.BlockSpec((ML, HEAD), lambda h, i: (0, h)),
                   pl.BlockSpec((ML, HEAD), lambda h, i: (0, h))],
        out_shape=[jax.ShapeDtypeStruct((S, heads * HEAD), F32),
                   jax.ShapeDtypeStruct((ML, heads * HEAD), F32),
                   jax.ShapeDtypeStruct((ML, heads * HEAD), F32)],
        compiler_params=_cp(("parallel", "arbitrary")),
    )(proj, mkv, mkv, dcat)
    return dq, jnp.concatenate([dk, dv], axis=1)


def _fox_gates(gl, bf):
    S = gl.shape[0]

    def body(g_ref, b_ref, o_ref):
        xv = g_ref[...] + b_ref[...]
        c = jnp.minimum(xv, 0.0) - jnp.log(1.0 + jnp.exp(-jnp.abs(xv)))
        row = lax.broadcasted_iota(jnp.int32, c.shape, 0)
        d = 1
        while d < S:
            c = c + jnp.where(row >= d, pltpu.roll(c, d, 0), 0.0)
            d *= 2
        o_ref[...] = c

    return pl.pallas_call(
        body, name="fox_gates", out_shape=jax.ShapeDtypeStruct((S, 128), F32),
        in_specs=[pl.BlockSpec(memory_space=pltpu.VMEM)] * 2,
        out_specs=pl.BlockSpec(memory_space=pltpu.VMEM),
        compiler_params=_cp(),
    )(gl, bf)


def _fox_gates_bwd(gl, bf, dcf):
    S = gl.shape[0]

    def body(g_ref, b_ref, d_ref, dg_ref, db_ref):
        c = d_ref[...]
        row = lax.broadcasted_iota(jnp.int32, c.shape, 0)
        d = 1
        while d < S:
            c = c + jnp.where(row < S - d, pltpu.roll(c, S - d, 0), 0.0)
            d *= 2
        dx = c * _sigmoid(-(g_ref[...] + b_ref[...]))
        dg_ref[...] = dx
        db_ref[...] = jnp.sum(dx, axis=0, keepdims=True)

    return pl.pallas_call(
        body, name="fox_gates_bwd",
        out_shape=[jax.ShapeDtypeStruct((S, 128), F32), jax.ShapeDtypeStruct((1, 128), F32)],
        in_specs=[pl.BlockSpec(memory_space=pltpu.VMEM)] * 3,
        out_specs=[pl.BlockSpec(memory_space=pltpu.VMEM)] * 2,
        compiler_params=_cp(),
    )(gl, bf, dcf)


def _fox_scores(q, k, cq, ck, t, masked):
    s = _dot(q, k, NT) * (HEAD ** -0.5) + cq - ck
    if masked:
        row = lax.broadcasted_iota(jnp.int32, (t, t), 0)
        col = lax.broadcasted_iota(jnp.int32, (t, t), 1)
        s = jnp.where(row >= col, s, -jnp.inf)
    return s


def _fox_pairs(nq, by_key):
    if by_key:
        pairs = [(i, j) for j in range(nq) for i in range(j, nq)]
    else:
        pairs = [(i, j) for i in range(nq) for j in range(i + 1)]
    return (jnp.asarray(np.array([a for a, _ in pairs], np.int32)),
            jnp.asarray(np.array([b for _, b in pairs], np.int32)))


def _fox_heads_per_step(H):
    return 2 if H % 2 == 0 else 1


def _fox_fwd(proj, cfq, cfk, H):
    S = proj.shape[0]
    t = _tile(S, FOX_TILE)
    nq = S // t
    hb = _fox_heads_per_step(H)
    W, G = hb * HEAD, H // hb
    cols = [slice(i * HEAD, (i + 1) * HEAD) for i in range(hb)]
    qt, kt = _fox_pairs(nq, False)

    def body(qt_ref, kt_ref, q_ref, k_ref, v_ref, cq_ref, ck_ref, o_ref, lse_ref, m_s, l_s, acc_s):
        n = pl.program_id(1)
        qi, ki = qt_ref[n], kt_ref[n]

        @pl.when(ki == 0)
        def _():
            m_s[...] = jnp.full_like(m_s, -jnp.inf)
            l_s[...] = jnp.zeros_like(l_s)
            acc_s[...] = jnp.zeros_like(acc_s)

        def step(masked):
            R = range(hb)
            ss = [_fox_scores(q_ref[:, cols[i]], k_ref[:, cols[i]], cq_ref[i], ck_ref[i], t, masked) for i in R]
            m_new = [jnp.maximum(m_s[i], jnp.max(ss[i], axis=-1, keepdims=True)) for i in R]
            alpha = [jnp.exp(m_s[i] - m_new[i]) for i in R]
            ps = [jnp.exp(ss[i] - m_new[i]) for i in R]
            pv = [_dot(ps[i], v_ref[:, cols[i]], NN) for i in R]
            for i in R:
                l_s[i] = alpha[i] * l_s[i] + jnp.sum(ps[i], axis=-1, keepdims=True)
                acc_s[:, cols[i]] = alpha[i] * acc_s[:, cols[i]] + pv[i]
                m_s[i] = m_new[i]

        @pl.when(ki != qi)
        def _():
            step(False)

        @pl.when(ki == qi)
        def _():
            step(True)
            for i in range(hb):
                o_ref[:, cols[i]] = acc_s[:, cols[i]] / l_s[i]
                lse_ref[i] = m_s[i] + jnp.log(l_s[i])

    qcol = pl.BlockSpec((hb, t, 1), lambda h, n, qt, kt: (h, qt[n], 0))
    return pl.pallas_call(
        body, name="fox_fwd",
        grid_spec=pltpu.PrefetchScalarGridSpec(
            num_scalar_prefetch=2, grid=(G, qt.shape[0]),
            in_specs=[pl.BlockSpec((t, W), lambda h, n, qt, kt: (qt[n], h)),
                      pl.BlockSpec((t, W), lambda h, n, qt, kt: (kt[n], G + h)),
                      pl.BlockSpec((t, W), lambda h, n, qt, kt: (kt[n], 2 * G + h)),
                      qcol,
                      pl.BlockSpec((hb, 1, t), lambda h, n, qt, kt: (h, 0, kt[n]))],
            out_specs=[pl.BlockSpec((t, W), lambda h, n, qt, kt: (qt[n], h)), qcol],
            scratch_shapes=[pltpu.VMEM((hb, t, 1), F32), pltpu.VMEM((hb, t, 1), F32), pltpu.VMEM((t, W), F32)]),
        out_shape=[jax.ShapeDtypeStruct((S, H * HEAD), F32), jax.ShapeDtypeStruct((H, S, 1), F32)],
        compiler_params=_cp(("parallel", "arbitrary")),
    )(qt, kt, proj, proj, proj, cfq, cfk)


def _fox_bwd_rowdot(proj, cfq, cfk, lse, dcat, H):
    S = proj.shape[0]
    t = _tile(S, FOX_TILE)
    nq = S // t
    hb = _fox_heads_per_step(H)
    W, G = hb * HEAD, H // hb
    cols = [slice(i * HEAD, (i + 1) * HEAD) for i in range(hb)]
    qt, kt = _fox_pairs(nq, False)

    def body(qt_ref, kt_ref, q_ref, k_ref, v_ref, do_ref, lse_ref, cq_ref, ck_ref, d_ref):
        n = pl.program_id(1)
        qi, ki = qt_ref[n], kt_ref[n]

        @pl.when(ki == 0)
        def _():
            d_ref[...] = jnp.zeros_like(d_ref)

        def step(masked):
            R = range(hb)
            ss = [_fox_scores(q_ref[:, cols[i]], k_ref[:, cols[i]], cq_ref[i], ck_ref[i], t, masked) for i in R]
            dps = [_dot(do_ref[:, cols[i]], v_ref[:, cols[i]], NT) for i in R]
            ps = [jnp.exp(ss[i] - lse_ref[i]) for i in R]
            for i in R:
                d_ref[i] += jnp.sum(ps[i] * dps[i], axis=-1, keepdims=True)

        @pl.when(ki != qi)
        def _():
            step(False)

        @pl.when(ki == qi)
        def _():
            step(True)

    qtile = pl.BlockSpec((t, W), lambda h, n, qt, kt: (qt[n], h))
    qcol = pl.BlockSpec((hb, t, 1), lambda h, n, qt, kt: (h, qt[n], 0))
    return pl.pallas_call(
        body, name="fox_bwd_rowdot",
        grid_spec=pltpu.PrefetchScalarGridSpec(
            num_scalar_prefetch=2, grid=(G, qt.shape[0]),
            in_specs=[qtile,
                      pl.BlockSpec((t, W), lambda h, n, qt, kt: (kt[n], G + h)),
                      pl.BlockSpec((t, W), lambda h, n, qt, kt: (kt[n], 2 * G + h)),
                      qtile, qcol, qcol,
                      pl.BlockSpec((hb, 1, t), lambda h, n, qt, kt: (h, 0, kt[n]))],
            out_specs=qcol),
        out_shape=jax.ShapeDtypeStruct((H, S, 1), F32),
        compiler_params=_cp(("parallel", "arbitrary")),
    )(qt, kt, proj, proj, proj, dcat, lse, cfq, cfk)


def _fox_bwd(proj, cfq, cfk, rowdot, lse, dcat, H):
    S = proj.shape[0]
    t = _tile(S, FOX_TILE)
    nq = S // t
    scale = HEAD ** -0.5
    hb = _fox_heads_per_step(H)
    W, G = hb * HEAD, H // hb
    cols = [slice(i * HEAD, (i + 1) * HEAD) for i in range(hb)]
    qt, kt = _fox_pairs(nq, True)

    def body(qt_ref, kt_ref, q_ref, k_ref, v_ref, dd_ref, do_ref, lse_ref, cq_ref, ck_ref,
             dq_ref, dk_ref, dv_ref, dck_ref):
        n = pl.program_id(1)
        i_, j_ = qt_ref[n], kt_ref[n]

        @pl.when(n == 0)
        def _():
            dq_ref[...] = jnp.zeros_like(dq_ref)

        @pl.when(i_ == j_)
        def _():
            dk_ref[...] = jnp.zeros_like(dk_ref)
            dv_ref[...] = jnp.zeros_like(dv_ref)
            dck_ref[...] = jnp.zeros_like(dck_ref)

        def step(masked):
            R = range(hb)
            qs, ks = [q_ref[:, c] for c in cols], [k_ref[:, c] for c in cols]
            dos = [do_ref[:, c] for c in cols]
            ss = [_fox_scores(qs[i], ks[i], cq_ref[i], ck_ref[i], t, masked) for i in R]
            dps = [_dot(dos[i], v_ref[:, cols[i]], NT) for i in R]
            ps = [jnp.exp(ss[i] - lse_ref[i]) for i in R]
            dss = [ps[i] * (dps[i] - dd_ref[i]) for i in R]
            dvs = [_dot(ps[i], dos[i], TN) for i in R]
            dks = [_dot(dss[i], qs[i], TN) * scale for i in R]
            dqs = [_dot(dss[i], ks[i], NN) * scale for i in R]
            rows = pl.ds(pl.multiple_of(i_ * t, t), t)
            for i in R:
                dv_ref[:, cols[i]] += dvs[i]
                dk_ref[:, cols[i]] += dks[i]
                dq_ref[rows, cols[i]] += dqs[i]
                dck_ref[i] -= jnp.sum(dss[i], axis=0, keepdims=True)

        @pl.when(i_ != j_)
        def _():
            step(False)

        @pl.when(i_ == j_)
        def _():
            step(True)

    qtile = pl.BlockSpec((t, W), lambda h, n, qt, kt: (qt[n], h))
    qcol = pl.BlockSpec((hb, t, 1), lambda h, n, qt, kt: (h, qt[n], 0))
    ktile = pl.BlockSpec((t, W), lambda h, n, qt, kt: (kt[n], h))
    krow = pl.BlockSpec((hb, 1, t), lambda h, n, qt, kt: (h, 0, kt[n]))
    return pl.pallas_call(
        body, name="fox_bwd",
        grid_spec=pltpu.PrefetchScalarGridSpec(
            num_scalar_prefetch=2, grid=(G, qt.shape[0]),
            in_specs=[qtile,
                      pl.BlockSpec((t, W), lambda h, n, qt, kt: (kt[n], G + h)),
                      pl.BlockSpec((t, W), lambda h, n, qt, kt: (kt[n], 2 * G + h)),
                      qcol, qtile, qcol, qcol, krow],
            out_specs=[pl.BlockSpec((S, W), lambda h, n, qt, kt: (0, h)), ktile, ktile, krow]),
        out_shape=[jax.ShapeDtypeStruct((S, H * HEAD), F32)] * 3 + [jax.ShapeDtypeStruct((H, 1, S), F32)],
        compiler_params=_cp(("parallel", "arbitrary")),
    )(qt, kt, proj, proj, proj, rowdot, dcat, lse, cfq, cfk)


def _s5_prep(lam_re, lam_im, log_dt, b_re, b_im, c_re, c_im):
    G, P = lam_re.shape
    ns = G // 16
    dt = jnp.exp(log_dt)[:, None]
    mag = jnp.exp(lam_re * dt)
    a_re, a_im = mag * jnp.cos(lam_im * dt), mag * jnp.sin(lam_im * dt)
    den = lam_re * lam_re + lam_im * lam_im
    z_re = ((a_re - 1.0) * lam_re + a_im * lam_im) / den
    z_im = (a_im * lam_re - (a_re - 1.0) * lam_im) / den
    bb_re = z_re[..., None] * b_re - z_im[..., None] * b_im
    bb_im = z_re[..., None] * b_im + z_im[..., None] * b_re
    eye = jnp.eye(16, dtype=F32)
    bb = jnp.stack([bb_re, bb_im]).reshape(2, ns, 16, P, S5_GROUP)
    wb = jnp.einsum('asgpc,gh->sgcahp', bb, eye).reshape(ns, S5_SLAB, 2 * 16 * P)
    cc = jnp.stack([c_re, -c_im]).reshape(2, ns, 16, S5_GROUP, P)
    wc = jnp.einsum('asgcp,gh->sagphc', cc, eye).reshape(ns, 2 * 16 * P, S5_SLAB)
    a = jnp.concatenate([a_re.reshape(ns, 1, 16 * P), a_im.reshape(ns, 1, 16 * P)], axis=-1)
    return wb, wc, a


def _s5_tables(lam_re, lam_im, log_dt):
    G, P = lam_re.shape
    ns = G // 16
    dt = jnp.exp(log_dt)[:, None]
    tt = jnp.arange(1, S5_CHUNK + 1, dtype=F32)[:, None, None]
    mag = jnp.exp(lam_re * dt * tt)
    ang = lam_im * dt * tt
    pr = (mag * jnp.cos(ang)).reshape(S5_CHUNK, ns, 16 * P).transpose(1, 0, 2)
    pi = (mag * jnp.sin(ang)).reshape(S5_CHUNK, ns, 16 * P).transpose(1, 0, 2)
    return pr, pi, pr[:, ::-1], pi[:, ::-1]


def _s5_scan_fwd(proj, wb, wc, pr, pi, dskip):
    S = proj.shape[0]
    ns = wb.shape[0]
    W = wb.shape[2]
    hw = W // 2
    T = S5_CHUNK
    nc = S // T
    mix = ns * S5_SLAB

    def body(u_ref, wb_ref, wc_ref, pr_ref, pi_ref, d_ref, v_ref, yg_ref, h_ref, cin_ref, carry):
        c = pl.program_id(1)

        @pl.when(c == 0)
        def _():
            carry[...] = jnp.zeros_like(carry)

        u = u_ref[...]
        bu = _dot(u, wb_ref[...], NN)
        xr, xi = bu[:, :hw], bu[:, hw:]
        sub = lax.broadcasted_iota(jnp.int32, (T, hw), 0) & (S5_ROWS - 1)
        d = 1
        while d < S5_ROWS:
            ar, ai = pr_ref[pl.ds(d - 1, 1), :], pi_ref[pl.ds(d - 1, 1), :]
            sr = jnp.where(sub >= d, pltpu.roll(xr, d, 0), 0.0)
            si = jnp.where(sub >= d, pltpu.roll(xi, d, 0), 0.0)
            xr, xi = xr + ar * sr - ai * si, xi + ar * si + ai * sr
            d *= 2
        cin_ref[...] = carry[...]
        cr, ci = carry[:, :hw], carry[:, hw:]
        pwr, pwi = pr_ref[pl.ds(0, S5_ROWS), :], pi_ref[pl.ds(0, S5_ROWS), :]
        for g in range(T // S5_ROWS):
            rows = slice(g * S5_ROWS, (g + 1) * S5_ROWS)
            hr = xr[rows, :] + pwr * cr - pwi * ci
            hi = xi[rows, :] + pwr * ci + pwi * cr
            h_ref[rows, :hw] = hr
            h_ref[rows, hw:] = hi
            cr, ci = hr[S5_ROWS - 1:S5_ROWS, :], hi[S5_ROWS - 1:S5_ROWS, :]
        carry[:, :hw] = cr
        carry[:, hw:] = ci
        y = _dot(h_ref[...], wc_ref[...], NN)
        v = y + d_ref[...] * u
        v_ref[...] = v
        yg_ref[...] = _gelu(v)

    return pl.pallas_call(
        body, name="s5_scan_fwd", grid=(ns, nc),
        in_specs=[pl.BlockSpec((T, S5_SLAB), lambda s, c: (c, s)),
                  pl.BlockSpec((None, S5_SLAB, W), lambda s, c: (s, 0, 0)),
                  pl.BlockSpec((None, W, S5_SLAB), lambda s, c: (s, 0, 0)),
                  pl.BlockSpec((None, T, hw), lambda s, c: (s, 0, 0)),
                  pl.BlockSpec((None, T, hw), lambda s, c: (s, 0, 0)),
                  pl.BlockSpec((1, S5_SLAB), lambda s, c: (0, s))],
        out_specs=[pl.BlockSpec((T, S5_SLAB), lambda s, c: (c, s)),
                   pl.BlockSpec((T, S5_SLAB), lambda s, c: (c, s)),
                   pl.BlockSpec((T, W), lambda s, c: (c, s)),
                   pl.BlockSpec((None, 1, W), lambda s, c: (c, 0, s))],
        out_shape=[jax.ShapeDtypeStruct((S, mix), F32), jax.ShapeDtypeStruct((S, mix), F32),
                   jax.ShapeDtypeStruct((S, ns * W), F32), jax.ShapeDtypeStruct((nc, 1, ns * W), F32)],
        scratch_shapes=[pltpu.VMEM((1, W), F32)],
        compiler_params=_cp(("parallel", "arbitrary")),
    )(proj, wb, wc, pr, pi, dskip)


def _s5_scan_bwd(dv, proj, hs, cin, wb, wc, pr, pi, prr, pir, dskip):
    S = proj.shape[0]
    ns = wb.shape[0]
    W = wb.shape[2]
    hw = W // 2
    T = S5_CHUNK
    nc = S // T
    mix = ns * S5_SLAB

    def body(dv_ref, u_ref, h_ref, cin_ref, wb_ref, wc_ref, pr_ref, pi_ref, prr_ref, pir_ref, d_ref,
             du_ref, dwb_ref, dwc_ref, da_ref, dd_ref, lam_s, carry):
        c = pl.program_id(1)

        @pl.when(c == 0)
        def _():
            carry[...] = jnp.zeros_like(carry)
            dwb_ref[...] = jnp.zeros_like(dwb_ref)
            dwc_ref[...] = jnp.zeros_like(dwc_ref)
            da_ref[...] = jnp.zeros_like(da_ref)
            dd_ref[...] = jnp.zeros_like(dd_ref)

        dy, u = dv_ref[...], u_ref[...]
        dh = _dot(dy, wc_ref[...], NT)
        gr, gi = dh[:, :hw], dh[:, hw:]
        row = lax.broadcasted_iota(jnp.int32, (T, hw), 0)
        sub = row & (S5_ROWS - 1)
        d = 1
        while d < S5_ROWS:
            ar, ai = pr_ref[pl.ds(d - 1, 1), :], -pi_ref[pl.ds(d - 1, 1), :]
            sr = jnp.where(sub < S5_ROWS - d, pltpu.roll(gr, T - d, 0), 0.0)
            si = jnp.where(sub < S5_ROWS - d, pltpu.roll(gi, T - d, 0), 0.0)
            gr, gi = gr + ar * sr - ai * si, gi + ar * si + ai * sr
            d *= 2
        lr, li = carry[:, :hw], carry[:, hw:]
        pwr, pwi = prr_ref[pl.ds(T - S5_ROWS, S5_ROWS), :], -pir_ref[pl.ds(T - S5_ROWS, S5_ROWS), :]
        for g in reversed(range(T // S5_ROWS)):
            rows = slice(g * S5_ROWS, (g + 1) * S5_ROWS)
            lgr = gr[rows, :] + pwr * lr - pwi * li
            lgi = gi[rows, :] + pwr * li + pwi * lr
            lam_s[rows, :hw] = lgr
            lam_s[rows, hw:] = lgi
            lr, li = lgr[0:1, :], lgi[0:1, :]
        carry[:, :hw] = lr
        carry[:, hw:] = li
        gr, gi = lam_s[:, :hw], lam_s[:, hw:]
        hr, hi = h_ref[:, :hw], h_ref[:, hw:]
        hpr = jnp.where(row >= 1, pltpu.roll(hr, 1, 0), cin_ref[:, :hw])
        hpi = jnp.where(row >= 1, pltpu.roll(hi, 1, 0), cin_ref[:, hw:])
        da_ref[:, :hw] += jnp.sum(hpr * gr + hpi * gi, axis=0, keepdims=True)
        da_ref[:, hw:] += jnp.sum(hpr * gi - hpi * gr, axis=0, keepdims=True)
        lam = lam_s[...]
        du_ref[...] = _dot(lam, wb_ref[...], NT) + dy * d_ref[...]
        dwb_ref[...] += _dot(u, lam, TN)
        dwc_ref[...] += _dot(h_ref[...], dy, TN)
        dd_ref[...] += jnp.sum(dy * u, axis=0, keepdims=True)

    def rc(c):
        return nc - 1 - c

    return pl.pallas_call(
        body, name="s5_scan_bwd", grid=(ns, nc),
        in_specs=[pl.BlockSpec((T, S5_SLAB), lambda s, c: (rc(c), s)),
                  pl.BlockSpec((T, S5_SLAB), lambda s, c: (rc(c), s)),
                  pl.BlockSpec((T, W), lambda s, c: (rc(c), s)),
                  pl.BlockSpec((None, 1, W), lambda s, c: (rc(c), 0, s)),
                  pl.BlockSpec((None, S5_SLAB, W), lambda s, c: (s, 0, 0)),
                  pl.BlockSpec((None, W, S5_SLAB), lambda s, c: (s, 0, 0)),
                  pl.BlockSpec((None, T, hw), lambda s, c: (s, 0, 0)),
                  pl.BlockSpec((None, T, hw), lambda s, c: (s, 0, 0)),
                  pl.BlockSpec((None, T, hw), lambda s, c: (s, 0, 0)),
                  pl.BlockSpec((None, T, hw), lambda s, c: (s, 0, 0)),
                  pl.BlockSpec((1, S5_SLAB), lambda s, c: (0, s))],
        out_specs=[pl.BlockSpec((T, S5_SLAB), lambda s, c: (rc(c), s)),
                   pl.BlockSpec((None, S5_SLAB, W), lambda s, c: (s, 0, 0)),
                   pl.BlockSpec((None, W, S5_SLAB), lambda s, c: (s, 0, 0)),
                   pl.BlockSpec((None, 1, W), lambda s, c: (s, 0, 0)),
                   pl.BlockSpec((1, S5_SLAB), lambda s, c: (0, s))],
        out_shape=[jax.ShapeDtypeStruct((S, mix), F32), jax.ShapeDtypeStruct(wb.shape, F32),
                   jax.ShapeDtypeStruct(wc.shape, F32), jax.ShapeDtypeStruct((ns, 1, W), F32),
                   jax.ShapeDtypeStruct((1, mix), F32)],
        scratch_shapes=[pltpu.VMEM((T, W), F32), pltpu.VMEM((1, W), F32)],
        compiler_params=_cp(("parallel", "arbitrary")),
    )(dv, proj, hs, cin, wb, wc, pr, pi, prr, pir, dskip)


def _s5_glu_bwd(dcat, yg, z):
    S, mix = yg.shape
    tr = _tile(S, ROW_TILE, 8)

    def body(do_ref, yg_ref, z_ref, dz_ref, dy_ref, db_ref):
        i = pl.program_id(0)

        @pl.when(i == 0)
        def _():
            db_ref[...] = jnp.zeros_like(db_ref)

        do, yg_, sz = do_ref[...], yg_ref[...], _sigmoid(z_ref[...])
        dz = do * yg_ * sz * (1.0 - sz)
        dz_ref[...] = dz
        dy_ref[...] = do * sz
        db_ref[...] += jnp.sum(dz, axis=0, keepdims=True)

    blk = pl.BlockSpec((tr, mix), lambda i: (i, 0))
    return pl.pallas_call(
        body, name="s5_glu_bwd", grid=(S // tr,),
        in_specs=[blk, blk, blk], out_specs=[blk, blk, pl.BlockSpec((1, mix), lambda i: (0, 0))],
        out_shape=[jax.ShapeDtypeStruct((S, mix), F32), jax.ShapeDtypeStruct((S, mix), F32),
                   jax.ShapeDtypeStruct((1, mix), F32)],
        compiler_params=_cp(("arbitrary",)),
    )(dcat, yg, z)


def _rows_down(x, j):
    return x if j == 0 else pltpu.roll(x, j, 0)


def _conv_rows(xe, w_ref, n):
    c = None
    for j in range(GDN_CONV):
        term = w_ref[pl.ds(GDN_CONV - 1 - j, 1), :] * _rows_down(xe, j)[8:8 + n, :]
        c = term if c is None else c + term
    return c


def _gdn_prep(proj, blk0, nblk, convw, norm, scale, name):
    S = proj.shape[0]
    tr = _tile(S, CONV_TILE, 8)
    nb8 = tr // 8

    def body(x_ref, xb_ref, w_ref, o_ref):
        i = pl.program_id(1)
        xe = jnp.concatenate([jnp.where(i == 0, 0.0, xb_ref[...]), x_ref[...]], axis=0)
        c = _conv_rows(xe, w_ref, tr)
        s = c * _sigmoid(c)
        if norm:
            s = s * lax.rsqrt(jnp.sum(s * s, axis=-1, keepdims=True) + EPS) * scale
        o_ref[...] = s

    return pl.pallas_call(
        body, name=name, grid=(nblk, S // tr),
        in_specs=[pl.BlockSpec((tr, HEAD), lambda j, i: (i, blk0 + j)),
                  pl.BlockSpec((8, HEAD), lambda j, i: (jnp.maximum(i * nb8 - 1, 0), blk0 + j)),
                  pl.BlockSpec((GDN_CONV, HEAD), lambda j, i: (0, j))],
        out_specs=pl.BlockSpec((tr, HEAD), lambda j, i: (i, j)),
        out_shape=jax.ShapeDtypeStruct((S, nblk * HEAD), F32),
        compiler_params=_cp(("parallel", "parallel")),
    )(proj, proj, convw)


def _gdn_prep_bwd(proj, blk0, nblk, convw, dout, norm, scale, name):
    S = proj.shape[0]
    tr = _tile(S, CONV_TILE, 8)
    nb8 = tr // 8
    last8 = S // 8 - 1
    nrow = S // tr

    def body(x_ref, xb_ref, xa_ref, w_ref, d_ref, da_ref, dx_ref, dw_ref):
        i = pl.program_id(1)

        @pl.when(i == 0)
        def _():
            dw_ref[...] = jnp.zeros_like(dw_ref)

        xe = jnp.concatenate([jnp.where(i == 0, 0.0, xb_ref[...]), x_ref[...], xa_ref[...]], axis=0)
        de = jnp.concatenate([d_ref[...], da_ref[...]], axis=0)
        n = tr + 8
        c = _conv_rows(xe, w_ref, n)
        sg = _sigmoid(c)
        s = c * sg
        if norm:
            r = lax.rsqrt(jnp.sum(s * s, axis=-1, keepdims=True) + EPS)
            ds = scale * r * (de - s * (r * r) * jnp.sum(de * s, axis=-1, keepdims=True))
        else:
            ds = de
        dc = ds * (sg + c * sg * (1.0 - sg))
        rowi = lax.broadcasted_iota(jnp.int32, (n, HEAD), 0)
        dc = jnp.where((i == nrow - 1) & (rowi >= tr), 0.0, dc)
        dct = dc[:tr, :]
        dx = None
        for j in range(GDN_CONV):
            tap = pl.ds(GDN_CONV - 1 - j, 1)
            up = dct if j == 0 else pltpu.roll(dc, n - j, 0)[:tr, :]
            term = w_ref[tap, :] * up
            dx = term if dx is None else dx + term
            dw_ref[tap, :] += jnp.sum(dct * _rows_down(xe, j)[8:8 + tr, :], axis=0, keepdims=True)
        dx_ref[...] = dx

    return pl.pallas_call(
        body, name=name, grid=(nblk, nrow),
        in_specs=[pl.BlockSpec((tr, HEAD), lambda j, i: (i, blk0 + j)),
                  pl.BlockSpec((8, HEAD), lambda j, i: (jnp.maximum(i * nb8 - 1, 0), blk0 + j)),
                  pl.BlockSpec((8, HEAD), lambda j, i: (jnp.minimum((i + 1) * nb8, last8), blk0 + j)),
                  pl.BlockSpec((GDN_CONV, HEAD), lambda j, i: (0, j)),
                  pl.BlockSpec((tr, HEAD), lambda j, i: (i, j)),
                  pl.BlockSpec((8, HEAD), lambda j, i: (jnp.minimum((i + 1) * nb8, last8), j))],
        out_specs=[pl.BlockSpec((tr, HEAD), lambda j, i: (i, j)),
                   pl.BlockSpec((GDN_CONV, HEAD), lambda j, i: (0, j))],
        out_shape=[jax.ShapeDtypeStruct((S, nblk * HEAD), F32),
                   jax.ShapeDtypeStruct((GDN_CONV, nblk * HEAD), F32)],
        compiler_params=_cp(("parallel", "arbitrary")),
    )(proj, proj, proj, convw, dout, dout)


def _gdn_gates(pg, alog, dtb):
    S = pg.shape[0]

    def body(a_ref, b_ref, al_ref, dt_ref, gc_ref, be_ref):
        g = -jnp.exp(al_ref[...]) * _softplus(a_ref[...] + dt_ref[...])
        rowm = lax.broadcasted_iota(jnp.int32, g.shape, 0) & (GDN_CHUNK - 1)
        c = g
        d = 1
        while d < GDN_CHUNK:
            c = c + jnp.where(rowm >= d, pltpu.roll(c, d, 0), 0.0)
            d *= 2
        gc_ref[...] = c
        be_ref[...] = _sigmoid(b_ref[...])

    blk = pl.BlockSpec((S, 128), lambda i: (0, 0))
    vec = pl.BlockSpec((1, 128), lambda i: (0, 0))
    return pl.pallas_call(
        body, name="gdn_gates", grid=(1,),
        in_specs=[blk, pl.BlockSpec((S, 128), lambda i: (0, 1)), vec, vec],
        out_specs=[blk, blk],
        out_shape=[jax.ShapeDtypeStruct((S, 128), F32)] * 2,
        compiler_params=_cp(("arbitrary",)),
    )(pg, pg, alog, dtb)


def _gdn_gates_bwd(pg, alog, dtb, dgc, dbeta):
    S = pg.shape[0]

    def body(a_ref, b_ref, al_ref, dt_ref, dgc_ref, dbe_ref, dpa_ref, dpb_ref, dal_ref, ddt_ref):
        rowm = lax.broadcasted_iota(jnp.int32, (S, 128), 0) & (GDN_CHUNK - 1)
        c = dgc_ref[...]
        d = 1
        while d < GDN_CHUNK:
            c = c + jnp.where(rowm < GDN_CHUNK - d, pltpu.roll(c, S - d, 0), 0.0)
            d *= 2
        xv = a_ref[...] + dt_ref[...]
        ea = jnp.exp(al_ref[...])
        g = -ea * _softplus(xv)
        dx = c * (-ea) * _sigmoid(xv)
        dpa_ref[...] = dx
        dal_ref[...] = jnp.sum(c * g, axis=0, keepdims=True)
        ddt_ref[...] = jnp.sum(dx, axis=0, keepdims=True)
        be = _sigmoid(b_ref[...])
        dpb_ref[...] = dbe_ref[...] * be * (1.0 - be)

    blk = pl.BlockSpec((S, 128), lambda i: (0, 0))
    blk1 = pl.BlockSpec((S, 128), lambda i: (0, 1))
    vec = pl.BlockSpec((1, 128), lambda i: (0, 0))
    dpa, dpb, dal, ddt = pl.pallas_call(
        body, name="gdn_gates_bwd", grid=(1,),
        in_specs=[blk, blk1, vec, vec, blk, blk],
        out_specs=[blk, blk, vec, vec],
        out_shape=[jax.ShapeDtypeStruct((S, 128), F32)] * 2 + [jax.ShapeDtypeStruct((1, 128), F32)] * 2,
        compiler_params=_cp(("arbitrary",)),
    )(pg, pg, alog, dtb, dgc, dbeta)
    return jnp.concatenate([dpa, dpb], axis=1), dal, ddt


def _gdn_pre(qs, ks, vs, gcs, grs, betas):
    C = GDN_CHUNK
    n = len(qs)
    r = lax.broadcasted_iota(jnp.int32, (C, C), 0)
    c_ = lax.broadcasted_iota(jnp.int32, (C, C), 1)
    lower, strict = r >= c_, r > c_
    eye = jnp.where(r == c_, 1.0, 0.0)
    decs = [jnp.exp(jnp.where(lower, gcs[i] - grs[i], -jnp.inf)) for i in range(n)]
    kbs = [ks[i] * betas[i] for i in range(n)]
    vbs = [vs[i] * betas[i] for i in range(n)]
    lmats = [jnp.where(strict, _dot(kbs[i], ks[i], NT) * decs[i], 0.0) for i in range(n)]
    amats = [jnp.where(lower, _dot(qs[i], ks[i], NT) * decs[i], 0.0) for i in range(n)]
    pks = [-lm for lm in lmats]
    tinvs = [eye + pk for pk in pks]
    for _ in range(5):
        pks = [_dotf(pk, pk, NN) for pk in pks]
        tinvs = [tv + _dotf(tv, pk, NN) for tv, pk in zip(tinvs, pks)]
    es = [jnp.exp(gc) for gc in gcs]
    glasts = [gc[C - 1:C, :] for gc in gcs]
    fs = [jnp.exp(gl - gc) for gl, gc in zip(glasts, gcs)]
    gls = [jnp.exp(gl) for gl in glasts]
    us = [_dotf(tinvs[i], vbs[i], NN) for i in range(n)]
    ws = [_dotf(tinvs[i], kbs[i] * es[i], NN) for i in range(n)]
    return [dict(lower=lower, strict=strict, dec=decs[i], kb=kbs[i], vb=vbs[i], lmat=lmats[i], tinv=tinvs[i],
                 e=es[i], f=fs[i], gl=gls[i], u=us[i], w=ws[i], amat=amats[i], qd=qs[i] * es[i],
                 kd=ks[i] * fs[i]) for i in range(n)]


def _gdn_heads_per_step(H):
    return max(d for d in (1, 2, 3, 4) if H % d == 0)


def _gdn_chunk_fwd(q, k, v, gcol, grow, bcol):
    S = q.shape[0]
    H, NC = gcol.shape[0], gcol.shape[1]
    C = GDN_CHUNK
    hb = _gdn_heads_per_step(H)

    def body(q_ref, k_ref, v_ref, gc_ref, gr_ref, b_ref, o_ref, st_ref, state):
        n = pl.program_id(1)

        @pl.when(n == 0)
        def _():
            state[...] = jnp.zeros_like(state)

        cols = [slice(i * HEAD, (i + 1) * HEAD) for i in range(hb)]
        ps = _gdn_pre([q_ref[:, c] for c in cols], [k_ref[:, c] for c in cols], [v_ref[:, c] for c in cols],
                      [gc_ref[i] for i in range(hb)], [gr_ref[i] for i in range(hb)],
                      [b_ref[i] for i in range(hb)])
        s0s = [state[i] for i in range(hb)]
        vns = [ps[i]['u'] - _dot(ps[i]['w'], s0s[i], NN) for i in range(hb)]
        outs = [_dot(ps[i]['qd'], s0s[i], NN) + _dot(ps[i]['amat'], vns[i], NN) for i in range(hb)]
        news = [s0s[i] * ps[i]['gl'] + _dot(ps[i]['kd'], vns[i], TN) for i in range(hb)]
        for i in range(hb):
            st_ref[i] = s0s[i]
            o_ref[:, cols[i]] = outs[i]
            state[i] = news[i]

    tok = pl.BlockSpec((C, hb * HEAD), lambda h, n: (n, h))
    col = pl.BlockSpec((hb, None, C, 1), lambda h, n: (h, n, 0, 0))
    rowb = pl.BlockSpec((hb, None, 1, C), lambda h, n: (h, n, 0, 0))
    return pl.pallas_call(
        body, name="gdn_chunk_fwd", grid=(H // hb, NC),
        in_specs=[tok, tok, tok, col, rowb, col],
        out_specs=[tok, pl.BlockSpec((hb, None, HEAD, HEAD), lambda h, n: (h, n, 0, 0))],
        out_shape=[jax.ShapeDtypeStruct((S, H * HEAD), F32), jax.ShapeDtypeStruct((H, NC, HEAD, HEAD), F32)],
        scratch_shapes=[pltpu.VMEM((hb, HEAD, HEAD), F32)],
        compiler_params=_cp(("parallel", "arbitrary")),
    )(q, k, v, gcol, grow, bcol)


def _gdn_chunk_bwd(q, k, v, gcol, grow, bcol, st, do):
    S = q.shape[0]
    H, NC = gcol.shape[0], gcol.shape[1]
    C = GDN_CHUNK
    hb = _gdn_heads_per_step(H)

    def body(q_ref, k_ref, v_ref, gc_ref, gr_ref, b_ref, st_ref, do_ref,
             dq_ref, dk_ref, dv_ref, dgc_ref, dbe_ref, dstate):
        n = pl.program_id(1)

        @pl.when(n == 0)
        def _():
            dstate[...] = jnp.zeros_like(dstate)

        R = range(hb)
        cols = [slice(i * HEAD, (i + 1) * HEAD) for i in R]
        qs, ks, vs = [q_ref[:, c] for c in cols], [k_ref[:, c] for c in cols], [v_ref[:, c] for c in cols]
        betas = [b_ref[i] for i in R]
        ps = _gdn_pre(qs, ks, vs, [gc_ref[i] for i in R], [gr_ref[i] for i in R], betas)
        lower, strict = ps[0]['lower'], ps[0]['strict']
        s0s, dos, ds1s = [st_ref[i] for i in R], [do_ref[:, c] for c in cols], [dstate[i] for i in R]
        vns = [ps[i]['u'] - _dot(ps[i]['w'], s0s[i], NN) for i in R]
        dvns = [_dot(ps[i]['amat'], dos[i], TN) + _dot(ps[i]['kd'], ds1s[i], NN) for i in R]
        damats = [jnp.where(lower, _dot(dos[i], vns[i], NT), 0.0) for i in R]
        dqds = [_dot(dos[i], s0s[i], NT) for i in R]
        dkds = [_dot(vns[i], ds1s[i], NT) for i in R]
        dgls = [jnp.sum(s0s[i] * ds1s[i], keepdims=True) for i in R]
        ds0s = [ps[i]['gl'] * ds1s[i] + _dot(ps[i]['qd'], dos[i], TN) - _dot(ps[i]['w'], dvns[i], TN) for i in R]
        dws = [-_dot(dvns[i], s0s[i], NT) for i in R]
        dvbs = [_dotf(ps[i]['tinv'], dvns[i], TN) for i in R]
        dkgs = [_dotf(ps[i]['tinv'], dws[i], TN) for i in R]
        dls = [-jnp.where(strict, _dotf(dvbs[i], ps[i]['u'], NT) + _dotf(dkgs[i], ps[i]['w'], NT), 0.0) for i in R]
        dkks = [dls[i] * ps[i]['dec'] for i in R]
        dqks = [damats[i] * ps[i]['dec'] for i in R]
        ms = [dls[i] * ps[i]['lmat'] + damats[i] * ps[i]['amat'] for i in R]
        dkbs = [_dot(dkks[i], ks[i], NN) + dkgs[i] * ps[i]['e'] for i in R]
        dks = [_dot(dkks[i], ps[i]['kb'], TN) + _dot(dqks[i], qs[i], TN) + dkds[i] * ps[i]['f'] + dkbs[i] * betas[i]
               for i in R]
        dqs = [_dot(dqks[i], ks[i], NN) + dqds[i] * ps[i]['e'] for i in R]
        ones = jnp.ones((C, HEAD), F32)
        colsums = [_dotf(ms[i], ones, TN)[:, 0:1] for i in R]
        rowi = lax.broadcasted_iota(jnp.int32, (C, 1), 0)
        for i in R:
            p = ps[i]
            de = (jnp.sum(dkgs[i] * p['kb'], axis=-1, keepdims=True)
                  + jnp.sum(dqds[i] * qs[i], axis=-1, keepdims=True))
            df = jnp.sum(dkds[i] * ks[i], axis=-1, keepdims=True)
            dgc = jnp.sum(ms[i], axis=-1, keepdims=True) - colsums[i] + de * p['e'] - df * p['f']
            dlast = jnp.sum(df * p['f'], keepdims=True) + dgls[i] * p['gl']
            dgc_ref[i] = dgc + jnp.where(rowi == C - 1, dlast, 0.0)
            dbe_ref[i] = (jnp.sum(dkbs[i] * ks[i], axis=-1, keepdims=True)
                          + jnp.sum(dvbs[i] * vs[i], axis=-1, keepdims=True))
            dstate[i] = ds0s[i]
            dq_ref[:, cols[i]] = dqs[i]
            dk_ref[:, cols[i]] = dks[i]
            dv_ref[:, cols[i]] = dvbs[i] * betas[i]

    def rn(n):
        return NC - 1 - n

    tok = pl.BlockSpec((C, hb * HEAD), lambda h, n: (rn(n), h))
    col = pl.BlockSpec((hb, None, C, 1), lambda h, n: (h, rn(n), 0, 0))
    rowb = pl.BlockSpec((hb, None, 1, C), lambda h, n: (h, rn(n), 0, 0))
    return pl.pallas_call(
        body, name="gdn_chunk_bwd", grid=(H // hb, NC),
        in_specs=[tok, tok, tok, col, rowb, col,
                  pl.BlockSpec((hb, None, HEAD, HEAD), lambda h, n: (h, rn(n), 0, 0)), tok],
        out_specs=[tok, tok, tok, col, col],
        out_shape=[jax.ShapeDtypeStruct((S, H * HEAD), F32)] * 3
        + [jax.ShapeDtypeStruct((H, NC, C, 1), F32)] * 2,
        scratch_shapes=[pltpu.VMEM((hb, HEAD, HEAD), F32)],
        compiler_params=_cp(("parallel", "arbitrary")),
    )(q, k, v, gcol, grow, bcol, st, do)


def _gdn_onorm(o, proj, gate_blk, w, H):
    S = o.shape[0]
    tr = _tile(S, CONV_TILE, 8)

    def body(o_ref, g_ref, w_ref, out_ref):
        ov, gv = o_ref[...], g_ref[...]
        r = lax.rsqrt(jnp.mean(ov * ov, axis=-1, keepdims=True) + EPS)
        out_ref[...] = (ov * r * w_ref[...]) * (gv * _sigmoid(gv))

    return pl.pallas_call(
        body, name="gdn_onorm", grid=(S // tr, H),
        in_specs=[pl.BlockSpec((tr, HEAD), lambda i, h: (i, h)),
                  pl.BlockSpec((tr, HEAD), lambda i, h: (i, gate_blk + h)),
                  pl.BlockSpec((1, HEAD), lambda i, h: (0, 0))],
        out_specs=pl.BlockSpec((tr, HEAD), lambda i, h: (i, h)),
        out_shape=jax.ShapeDtypeStruct((S, H * HEAD), F32),
        compiler_params=_cp(("parallel", "parallel")),
    )(o, proj, w)


def _gdn_onorm_bwd(dcat, o, proj, gate_blk, w, H):
    S = o.shape[0]
    tr = _tile(S, CONV_TILE, 8)

    def body(d_ref, o_ref, g_ref, w_ref, do_ref, dg_ref, dw_ref):
        i, h = pl.program_id(0), pl.program_id(1)

        @pl.when((i == 0) & (h == 0))
        def _():
            dw_ref[...] = jnp.zeros_like(dw_ref)

        dm, ov, gv, wv = d_ref[...], o_ref[...], g_ref[...], w_ref[...]
        r = lax.rsqrt(jnp.mean(ov * ov, axis=-1, keepdims=True) + EPS)
        oh = ov * r
        sg = gv * _sigmoid(gv)
        dy = dm * sg
        t = dy * wv
        do_ref[...] = r * (t - oh * jnp.mean(t * oh, axis=-1, keepdims=True))
        dg_ref[...] = dm * (oh * wv) * _silu_grad(gv)
        dw_ref[...] += jnp.sum(dy * oh, axis=0, keepdims=True)

    tok = pl.BlockSpec((tr, HEAD), lambda i, h: (i, h))
    vec = pl.BlockSpec((1, HEAD), lambda i, h: (0, 0))
    return pl.pallas_call(
        body, name="gdn_onorm_bwd", grid=(S // tr, H),
        in_specs=[tok, tok, pl.BlockSpec((tr, HEAD), lambda i, h: (i, gate_blk + h)), vec],
        out_specs=[tok, tok, vec],
        out_shape=[jax.ShapeDtypeStruct((S, H * HEAD), F32)] * 2 + [jax.ShapeDtypeStruct((1, HEAD), F32)],
        compiler_params=_cp(("arbitrary", "arbitrary")),
    )(dcat, o, proj, w)


def _lanes_to_heads(a, H):
    return a[:, :H].T


def _heads_to_lanes(a):
    H = a.shape[0]
    return jnp.pad(a.T, ((0, 0), (0, 128 - H)))


def _take_cols(segs, a, b):
    out, off = [], 0
    for sg in segs:
        w = sg.shape[-1]
        lo, hi = max(a, off), min(b, off + w)
        if lo < hi:
            out.append(sg[..., lo - off:hi - off])
        off += w
    return out


def _pad_cols(pieces):
    m = jnp.concatenate(pieces, axis=-1)
    return jnp.pad(m, ((0, 0), (0, 128 - m.shape[-1])))


def _pad_lanes(v):
    return jnp.pad(v.reshape(1, -1), ((0, 0), (0, 128 - v.shape[-1])))


def _s5_layer_fwd(a, w, cfg):
    proj = _mm(a, w['w_in'], name="s5_in")
    wb, wc, _ = w['prep']
    pr, pi, prr, pir = w['tables']
    v, yg, hs, cin = _s5_scan_fwd(proj, wb, wc, pr, pi, w['d_skip'])
    z, mix = _mm(yg, w['w_glu'], name="s5_glu", extras=[(yg, 'ij'), (w['b_glu'], 'j')],
                 epi=lambda acc, y, b: (acc + b, y * _sigmoid(acc + b)), out_dtypes=(F32, F32))
    return proj, mix, dict(v=v, yg=yg, hs=hs, cin=cin, z=z)


def _s5_layer_bwd(a, w, proj, sv, dcat, dmemq, cfg):
    wb, wc, _ = w['prep']
    pr, pi, prr, pir = w['tables']
    dz, dyg1, db_glu = _s5_glu_bwd(dcat, sv['yg'], sv['z'])
    dw_glu = _mm(sv['yg'], dz, name="s5_dwglu", ta=True)
    dv = _mm(dz, w['w_glu'], name="s5_dyg", tb=True, extras=[(dyg1, 'ij'), (sv['v'], 'ij')],
             epi=lambda acc, d1, vv: ((acc + d1) * _gelu_grad(vv),))
    du, dwb, dwc, da, dd = _s5_scan_bwd(dv, proj, sv['hs'], sv['cin'], wb, wc, pr, pi, prr, pir, w['d_skip'])
    dproj = jnp.concatenate([du, dmemq], axis=1).astype(MXU_DTYPE)
    dw_in = _mm(a, dproj, name="s5_dwin", ta=True)
    da_in = _mm(dproj, w['w_in'], name="s5_da", tb=True)
    dlre, dlim, dldt, dbre, dbim, dcre, dcim = w['prep_vjp']((dwb, dwc, da))
    grads = dict(w_in=dw_in, w_glu=dw_glu, b_glu=db_glu[0], d_skip=dd[0], lam_re=dlre, lam_im=dlim,
                 log_dt=dldt, b_re=dbre, b_im=dbim, c_re=dcre, c_im=dcim)
    return da_in, grads


def _gdn_relayout(a, H, NC):
    t = _lanes_to_heads(a, H).reshape(H, NC, GDN_CHUNK)
    return t[..., None], t[:, :, None, :]


def _gdn_layer_fwd(a, w, cfg):
    H, MIX, S = cfg['H'], cfg['MIX'], a.shape[0]
    NC = S // GDN_CHUNK
    proj = _mm(a, w['w_main'], name="gdn_in")
    pg = _mm(a, w['w_gate'], name="gdn_in_gates")
    cw = w['conv_w']
    q = _gdn_prep(proj, 0, H, cw[:, :MIX], True, HEAD ** -0.5, "gdn_prep_q")
    k = _gdn_prep(proj, H, H, cw[:, MIX:2 * MIX], True, 1.0, "gdn_prep_k")
    v = _gdn_prep(proj, 2 * H, H, cw[:, 2 * MIX:], False, 1.0, "gdn_prep_v")
    gc, beta = _gdn_gates(pg, w['a_log'], w['dt_bias'])
    gcol, grow = _gdn_relayout(gc, H, NC)
    bcol, _ = _gdn_relayout(beta, H, NC)
    o, st = _gdn_chunk_fwd(q, k, v, gcol, grow, bcol)
    mix = _gdn_onorm(o, proj, 3 * H, w['o_norm'], H)
    return proj, mix, dict(pg=pg, q=q, k=k, v=v, gcol=gcol, grow=grow, bcol=bcol, o=o, st=st)


def _gdn_layer_bwd(a, w, proj, sv, dcat, dmemq, cfg):
    H, MIX, S = cfg['H'], cfg['MIX'], a.shape[0]
    cw = w['conv_w']
    do, dgate, donorm = _gdn_onorm_bwd(dcat, sv['o'], proj, 3 * H, w['o_norm'], H)
    dq, dk, dv, dgcol, dbcol = _gdn_chunk_bwd(sv['q'], sv['k'], sv['v'], sv['gcol'], sv['grow'], sv['bcol'],
                                              sv['st'], do)
    dgc = _heads_to_lanes(dgcol.reshape(H, S))
    dbeta = _heads_to_lanes(dbcol.reshape(H, S))
    dpg, dalog, ddtb = _gdn_gates_bwd(sv['pg'], w['a_log'], w['dt_bias'], dgc, dbeta)
    dxq, dwq = _gdn_prep_bwd(proj, 0, H, cw[:, :MIX], dq, True, HEAD ** -0.5, "gdn_prep_bwd_q")
    dxk, dwk = _gdn_prep_bwd(proj, H, H, cw[:, MIX:2 * MIX], dk, True, 1.0, "gdn_prep_bwd_k")
    dxv, dwv = _gdn_prep_bwd(proj, 2 * H, H, cw[:, 2 * MIX:], dv, False, 1.0, "gdn_prep_bwd_v")
    dproj = jnp.concatenate([dxq, dxk, dxv, dgate, dmemq], axis=1).astype(MXU_DTYPE)
    dw_main = _mm(a, dproj, name="gdn_dwmain", ta=True)
    dw_gate = _mm(a, dpg, name="gdn_dwgate", ta=True)
    da1 = _mm(dpg, w['w_gate'], name="gdn_da_gates", tb=True)
    da_in = _mm(dproj, w['w_main'], name="gdn_da", tb=True, extras=[(da1, 'ij')], epi=lambda acc, e: (acc + e,))
    grads = dict(w_main=dw_main, w_gate=dw_gate, conv_w=jnp.concatenate([dwq, dwk, dwv], axis=1),
                 a_log=dalog[0, :H], dt_bias=ddtb[0, :H], o_norm=donorm[0])
    return da_in, grads


def _fox_layer_fwd(a, w, cfg):
    H = cfg['H']
    proj = _mm(a, w['w_main'], name="fox_in")
    pg = _mm(a, w['w_gate'], name="fox_in_gates")
    cf = _fox_gates(pg, w['b_f'])
    cfh = _lanes_to_heads(cf, H)
    cfq, cfk = cfh[:, :, None], cfh[:, None, :]
    o, lse = _fox_fwd(proj, cfq, cfk, H)
    return proj, o, dict(pg=pg, cfq=cfq, cfk=cfk, lse=lse)


def _fox_layer_bwd(a, w, proj, sv, dcat, dmemq, cfg):
    H = cfg['H']
    rowdot = _fox_bwd_rowdot(proj, sv['cfq'], sv['cfk'], sv['lse'], dcat, H)
    dq, dk, dv, dck = _fox_bwd(proj, sv['cfq'], sv['cfk'], rowdot, sv['lse'], dcat, H)
    dpg, dbf = _fox_gates_bwd(sv['pg'], w['b_f'], _heads_to_lanes(dck[:, 0, :]))
    dproj = jnp.concatenate([dq, dk, dv, dmemq], axis=1).astype(MXU_DTYPE)
    dw_main = _mm(a, dproj, name="fox_dwmain", ta=True)
    dw_gate = _mm(a, dpg, name="fox_dwgate", ta=True)
    da1 = _mm(dpg, w['w_gate'], name="fox_da_gates", tb=True)
    da_in = _mm(dproj, w['w_main'], name="fox_da", tb=True, extras=[(da1, 'ij')], epi=lambda acc, e: (acc + e,))
    grads = dict(w_main=dw_main, w_gate=dw_gate, b_f=dbf[0, :H])
    return da_in, grads


_LAYER_FWD = (_s5_layer_fwd, _gdn_layer_fwd, _fox_layer_fwd)
_LAYER_BWD = (_s5_layer_bwd, _gdn_layer_bwd, _fox_layer_bwd)


def _mixer_weights(kind, j, fw, p, cfg, after):
    H, MIX, MW = cfg['H'], cfg['MIX'], cfg['MW']
    if kind == 0:
        params = tuple(p[n][j] for n in ('s5_lam_re', 's5_lam_im', 's5_log_dt', 's5_b_re', 's5_b_im',
                                         's5_c_re', 's5_c_im'))
        prep, prep_vjp = jax.vjp(_s5_prep, *params)
        prep = (prep[0].astype(MXU_DTYPE), prep[1].astype(MXU_DTYPE), prep[2])
        tables = _s5_tables(*params[:3])
        return dict(w_in=fw.get('s5_w_in', j, after), w_glu=fw.get('s5_w_glu', j, after),
                    b_glu=fw.get('s5_b_glu', j, after), d_skip=fw.get('s5_d_skip', j, after).reshape(1, MIX),
                    prep=prep, prep_vjp=prep_vjp, tables=tables)
    if kind == 1:
        segs = fw.get('gdn_w_in', j, after)
        c0 = 4 * MIX
        total = c0 + 2 * H + MW
        w_main = jnp.concatenate(_take_cols(segs, 0, c0) + _take_cols(segs, c0 + 2 * H, total), axis=1)
        w_gate = jnp.concatenate([_pad_cols(_take_cols(segs, c0, c0 + H)),
                                  _pad_cols(_take_cols(segs, c0 + H, c0 + 2 * H))], axis=1)
        return dict(w_main=w_main, w_gate=w_gate, conv_w=fw.get('gdn_conv_w', j, after),
                    a_log=_pad_lanes(p['gdn_a_log'][j]), dt_bias=_pad_lanes(p['gdn_dt_bias'][j]),
                    o_norm=p['gdn_o_norm'][j].reshape(1, HEAD))
    segs = fw.get('fox_w_in', j, after)
    c0 = 3 * MIX
    total = c0 + H + MW
    w_main = jnp.concatenate(_take_cols(segs, 0, c0) + _take_cols(segs, c0 + H, total), axis=1)
    w_gate = _pad_cols(_take_cols(segs, c0, c0 + H))
    return dict(w_main=w_main, w_gate=w_gate, b_f=_pad_lanes(p['fox_b_f'][j]))


class _Weights:
    def __init__(self, resolve):
        self._resolve, self._have = resolve, {}

    def get(self, name, layer, after):
        if name not in self._have:
            self._have[name] = self._resolve(name, after)
        return self._have[name][layer]


def _local_step(p, fw, cfg, on_grad=None):
    H, MIX, MW, MH, depth = cfg['H'], cfg['MIX'], cfg['MW'], cfg['MH'], cfg['depth']
    x, mem, target = p['x'], p['mem'], p['loss_target']
    q_blk = {0: MIX // HEAD, 1: 4 * MIX // HEAD, 2: 3 * MIX // HEAD}
    zero = jnp.zeros((), F32)
    tok = [zero]

    def told(name, layer, value):
        if on_grad is not None:
            tok[0] = tok[0] + on_grad(name, layer, value)
        return value

    mem_n = _rms_fwd(mem, p['mem_norm'], MXU_DTYPE, "mem_rms")
    w_kv = fw.get('w_mem_kv', 0, mem_n)
    mkv = _mm(mem_n, w_kv, name="mem_kv")

    h = x
    saved = []
    for i in range(depth):
        kind, j = i % 3, i // 3
        a = _rms_fwd(h, p['norm1'][i], MXU_DTYPE, "rms1")
        w = _mixer_weights(kind, j, fw, p, cfg, a)
        proj, mix, sv = _LAYER_FWD[kind](a, w, cfg)
        read = _mem_fwd(proj, q_blk[kind], mkv, MH)
        cat = jnp.concatenate([mix, read], axis=1).astype(MXU_DTYPE)
        w_out, w_up = fw.get('w_out', i, proj), fw.get('w_up', i, proj)
        h1 = _mm(cat, w_out, name="out_proj", extras=[(h, 'ij')], epi=lambda acc, r: (acc + r,))
        a2 = _rms_fwd(h1, p['norm2'][i], MXU_DTYPE, "rms2")
        act = _mm(a2, w_up, name="mlp_up", epi=lambda acc: (_relu2(acc),), out_dtypes=(MXU_DTYPE,))
        w_down = fw.get('w_down', i, h1)
        h2 = _mm(act, w_down, name="mlp_down", extras=[(h1, 'ij')], epi=lambda acc, r: (acc + r,))
        saved.append(dict(w=w, h=h, a=a, proj=proj, sv=sv, cat=cat, h1=h1, a2=a2, act=act,
                          w_out=w_out, w_up=w_up, w_down=w_down))
        h = h2

    loss, dh, dnorm_f, dh16 = _loss_head(h, p['norm_f'], target)

    g = {n: None for n in WEIGHTS}
    g['norm_f'] = dnorm_f[0]
    per_layer = {n: [None] * depth for n in ('norm1', 'norm2', 'w_out', 'w_up', 'w_down')}
    mix_grads = {0: {}, 1: {}, 2: {}}
    big = {0: (('s5_w_in', 'w_in'), ('s5_w_glu', 'w_glu')), 1: (), 2: ()}
    dmkv = None
    for i in reversed(range(depth)):
        kind, j = i % 3, i // 3
        s = saved[i]
        w = s['w']
        du = _mm(dh16, s['w_down'], name="mlp_ddown", tb=True, extras=[(s['act'], 'ij')],
                 epi=lambda acc, aa: (acc * (2.0 * jnp.sqrt(aa.astype(F32))),), out_dtypes=(MXU_DTYPE,))
        per_layer['w_down'][i] = told('w_down', i, _mm(s['act'], dh16, name="mlp_dwdown", ta=True))
        per_layer['w_up'][i] = told('w_up', i, _mm(s['a2'], du, name="mlp_dwup", ta=True))
        da2 = _mm(du, s['w_up'], name="mlp_dup", tb=True)
        dh1, dn2, dh1_16 = _rms_bwd(s['h1'], p['norm2'][i] + tok[0], da2, dh, "rms2_bwd")
        per_layer['norm2'][i] = dn2[0]
        dcat = _mm(dh1_16, s['w_out'], name="out_dproj", tb=True)
        per_layer['w_out'][i] = told('w_out', i, _mm(s['cat'], dh1_16, name="out_dw", ta=True))
        dmemq, dmkv_i = _mem_bwd(s['proj'], q_blk[kind], mkv, dcat, MIX // HEAD, MH)
        dmkv = dmkv_i if dmkv is None else dmkv + dmkv_i
        da, mg = _LAYER_BWD[kind](s['a'], w, s['proj'], s['sv'], dcat, dmemq, cfg)
        mix_grads[kind][j] = mg
        for name, key in big[kind]:
            told(name, j, mg[key])
        c0 = 4 * MIX
        if kind == 1:
            mg['segs'] = told('gdn_w_in', j, [mg['w_main'][:, :c0], mg['w_gate'][:, :H],
                                               mg['w_gate'][:, 128:128 + H], mg['w_main'][:, c0:]])
        c0 = 3 * MIX
        if kind == 2:
            mg['segs'] = told('fox_w_in', j, [mg['w_main'][:, :c0], mg['w_gate'][:, :H], mg['w_main'][:, c0:]])
        dh, dn1, dh16 = _rms_bwd(s['h'], p['norm1'][i] + tok[0], da, dh1, "rms1_bwd")
        per_layer['norm1'][i] = dn1[0]
    for n in ('norm1', 'norm2'):
        g[n] = jnp.stack(per_layer[n])
    for n in ('w_out', 'w_up', 'w_down'):
        g[n] = per_layer[n]

    g['w_mem_kv'] = told('w_mem_kv', 0, _mm(mem_n, dmkv, name="mem_dwkv", ta=True))
    dmem_n = _mm(dmkv, w_kv, name="mem_dn", tb=True)
    _, dmn, _ = _rms_bwd(mem, p['mem_norm'] + tok[0], dmem_n, None, "mem_rms_bwd")
    g['mem_norm'] = dmn[0]

    def layers(kind, key):
        return [mix_grads[kind][j][key] for j in sorted(mix_grads[kind])]

    g['s5_w_in'] = layers(0, 'w_in')
    g['s5_w_glu'] = layers(0, 'w_glu')
    for n in ('b_glu', 'd_skip', 'lam_re', 'lam_im', 'log_dt', 'b_re', 'b_im', 'c_re', 'c_im'):
        g['s5_' + n] = jnp.stack(layers(0, n))
    g['gdn_w_in'] = layers(1, 'segs')
    for n in ('conv_w', 'a_log', 'dt_bias', 'o_norm'):
        g['gdn_' + n] = jnp.stack(layers(1, n))
    g['fox_w_in'] = layers(2, 'segs')
    g['fox_b_f'] = jnp.stack(layers(2, 'b_f'))
    return loss, dh, g


AG_ORDER = ('w_mem_kv', 's5_w_in', 's5_w_glu', 'w_out', 'w_up', 'w_down', 'gdn_w_in', 'fox_w_in')


def _gather_begin(p, me):
    handles, tok = {}, jnp.zeros((), F32)
    for name in AG_ORDER:
        w = p[name]
        xs = w.astype(MXU_DTYPE).reshape(-1, w.shape[-1])
        handle, t = _ag_start(xs, "ag_start_" + name)
        handles[name] = (handle, xs)
        tok = tok + t
    vec = _gather(_pack_small([p[n] for n in VECTOR_SHARDED], 16), me, "all_gather_vectors").reshape(4, -1)
    vectors, off = {}, 0
    for n in VECTOR_SHARDED:
        sz = p[n].size
        stacked = vec[:, off:off + sz].reshape((4,) + p[n].shape)
        ax = SHARD_AXIS[n]
        t = jnp.moveaxis(stacked, 0, ax)
        shp = list(t.shape)
        vectors[n] = t.reshape(shp[:ax] + [shp[ax] * shp[ax + 1]] + shp[ax + 2:])
        off += sz
    return handles, vectors, tok


def _gather_end(name, after, p, me, handles, vectors):
    if name in vectors:
        return vectors[name]
    handle, xs = handles[name]
    got = _ag_wait(handle, after, "ag_wait_" + name)
    got = _ag_forward(lax.dynamic_update_slice(got, xs[None], (me, 0, 0)), "ag_forward_" + name)
    w = p[name]
    if w.ndim == 2:
        return [got.reshape(4 * w.shape[0], w.shape[1])]
    L, r, n = w.shape
    if SHARD_AXIS[name] == 1:
        return [got[:, i * r:(i + 1) * r, :].reshape(4 * r, n) for i in range(L)]
    blocks = [[got[s, i * r:(i + 1) * r, :] for s in range(4)] for i in range(L)]
    if name == 'w_up':
        return [jnp.concatenate(b, axis=1) for b in blocks]
    return blocks


def _shard_blocks(name, value, p):
    shp = p[name].shape
    r, n = shp[-2], shp[-1]
    if SHARD_AXIS[name] == len(shp) - 2:
        return value.reshape(4, r, n)
    segs = value if isinstance(value, list) else [value]
    return jnp.stack([jnp.concatenate(_take_cols(segs, s * n, (s + 1) * n), axis=1) for s in range(4)])


def kernel(x, mem, mem_norm, w_mem_kv, norm1, w_out, norm2, w_up, w_down, norm_f, s5_w_in, s5_lam_re, s5_lam_im, s5_log_dt, s5_b_re, s5_b_im, s5_c_re, s5_c_im, s5_d_skip, s5_w_glu, s5_b_glu, gdn_w_in, gdn_conv_w, gdn_a_log, gdn_dt_bias, gdn_o_norm, fox_w_in, fox_b_f, loss_target, m_mem_norm, m_w_mem_kv, m_norm1, m_w_out, m_norm2, m_w_up, m_w_down, m_norm_f, m_s5_w_in, m_s5_lam_re, m_s5_lam_im, m_s5_log_dt, m_s5_b_re, m_s5_b_im, m_s5_c_re, m_s5_c_im, m_s5_d_skip, m_s5_w_glu, m_s5_b_glu, m_gdn_w_in, m_gdn_conv_w, m_gdn_a_log, m_gdn_dt_bias, m_gdn_o_norm, m_fox_w_in, m_fox_b_f, v_mem_norm, v_w_mem_kv, v_norm1, v_w_out, v_norm2, v_w_up, v_w_down, v_norm_f, v_s5_w_in, v_s5_lam_re, v_s5_lam_im, v_s5_log_dt, v_s5_b_re, v_s5_b_im, v_s5_c_re, v_s5_c_im, v_s5_d_skip, v_s5_w_glu, v_s5_b_glu, v_gdn_w_in, v_gdn_conv_w, v_gdn_a_log, v_gdn_dt_bias, v_gdn_o_norm, v_fox_w_in, v_fox_b_f):
    args = locals()
    p = {n: args[n] for n in WEIGHTS}
    mom = {n: args['m_' + n] for n in WEIGHTS}
    var = {n: args['v_' + n] for n in WEIGHTS}
    S, D = x.shape[1], x.shape[2]
    MW = w_mem_kv.shape[1] // 2
    MIX = D - MW
    cfg = dict(H=MIX // HEAD, MIX=MIX, MW=MW, MH=MW // HEAD, depth=norm1.shape[0])
    p.update(x=x.reshape(S, D), mem=mem.reshape(mem.shape[1], D), loss_target=loss_target.reshape(S, D))
    c = lax.axis_index("c")
    me = 2 * lax.axis_index("x") + lax.axis_index("y")
    place = dict(c=c, c_idx=c.astype(jnp.int32).reshape(1), me_idx=me.astype(jnp.int32).reshape(1))

    handles, vectors, tok = _gather_begin(p, me)
    p['mem_norm'] = mem_norm + tok
    fw = _Weights(lambda name, after: _gather_end(name, after, p, me, handles, vectors))

    pending = {}

    def on_grad(name, layer, value):
        handle, t = _rs_begin(_shard_blocks(name, value, p), place, "%s_%d" % (name, layer))
        pending[name, layer] = handle
        return t

    loss, dx, g = _local_step(p, fw, cfg, on_grad)
    p['mem_norm'] = mem_norm

    grads = {}
    for name in MATMUL_WEIGHTS:
        shp = p[name].shape
        layers = [_rs_end(pending[name, i], dx, place, "%s_%d" % (name, i))
                  for i in range(1 if len(shp) == 2 else shp[0])]
        grads[name] = layers[0] if len(shp) == 2 else jnp.stack(layers)

    parts = []
    for name in VECTOR_SHARDED:
        ax = SHARD_AXIS[name]
        shp = list(g[name].shape)
        t = g[name].reshape(shp[:ax] + [4, shp[ax] // 4] + shp[ax + 1:])
        parts.append(jnp.moveaxis(t, ax, 0).reshape(4, -1))
    flat = jnp.concatenate(parts, axis=1)
    flat = jnp.pad(flat, ((0, 0), (0, 16 * LANES - flat.shape[1]))).reshape(4, 16, LANES)
    handle, _ = _rs_begin(flat, place, "vectors")
    red = _rs_end(handle, dx, place, "vectors").reshape(-1)
    off = 0
    for name in VECTOR_SHARDED:
        grads[name] = red[off:off + p[name].size].reshape(p[name].shape)
        off += p[name].size

    n_small = sum(p[n].size for n in REPLICATED)
    rows = -(-n_small // LANES // 8) * 8
    small = _all_reduce_small(_pack_small([g[n] for n in REPLICATED], rows), "all_reduce_small").reshape(-1)
    off = 0
    for n in REPLICATED:
        grads[n] = small[off:off + p[n].size].reshape(p[n].shape)
        off += p[n].size

    delta, new_m, new_v = {}, {}, {}
    for n in SHARD_AXIS:
        shp = p[n].shape
        two_d = (-1, shp[-1])
        d, nm, nv = _adamw(p[n].reshape(two_d), grads[n].reshape(two_d), mom[n].reshape(two_d),
                           var[n].reshape(two_d), "adamw_" + n)
        delta[n], new_m[n], new_v[n] = d.reshape(shp), nm.reshape(shp), nv.reshape(shp)
    d, nm, nv = _adamw(*[_pack_small([src[n] for n in REPLICATED], rows) for src in (p, grads, mom, var)],
                       "adamw_small")
    d, nm, nv = d.reshape(-1), nm.reshape(-1), nv.reshape(-1)
    off = 0
    for n in REPLICATED:
        sz, shp = p[n].size, p[n].shape
        delta[n], new_m[n], new_v[n] = (d[off:off + sz].reshape(shp), nm[off:off + sz].reshape(shp),
                                        nv[off:off + sz].reshape(shp))
        off += sz

    total = lax.psum(loss[0, 0], ("x", "y", "c"))
    return (total, dx.reshape(x.shape), *[grads[n] for n in WEIGHTS], *[delta[n] for n in WEIGHTS],
            *[new_m[n] for n in WEIGHTS], *[new_v[n] for n in WEIGHTS])
```

```python
import math

import jax
import jax.numpy as jnp
import numpy as np
from jax import lax
from jax.experimental import pallas as pl
from jax.experimental.pallas import tpu as pltpu

F32 = jnp.float32
MXU_DTYPE = jnp.bfloat16
EPS = 1e-6
HEAD = 128
S5_GROUP = 16
S5_STATE = 64
S5_SLAB = 256
S5_CHUNK = 128
S5_ROWS = 8
GDN_CHUNK = 64
GDN_CONV = 4
LANES = 1024
VMEM_LIMIT_BYTES = 56 * 1024 * 1024
MESH = pl.DeviceIdType.MESH
RS_PAYLOAD = jnp.bfloat16
HBM_SPEC = pl.BlockSpec(memory_space=pltpu.HBM)
SEM_SPEC = pl.BlockSpec(memory_space=pltpu.SEMAPHORE)
SPLIT_EFFECT = pltpu.SideEffectType.DATAFLOW_SIDE_EFFECTING

ADAM_LR, ADAM_B1, ADAM_B2, ADAM_EPS, ADAM_WD, ADAM_STEP = 0.001, 0.9, 0.999, 1e-08, 0.01, 10

MM_TM, MM_TN, MM_TK = 1024, 1024, 1024
ROW_TILE = 256
FOX_TILE = 512
MEM_TILE = 512
CONV_TILE = 1024

NN = (((1,), (0,)), ((), ()))
NT = (((1,), (1,)), ((), ()))
TN = (((0,), (0,)), ((), ()))

WEIGHTS = ['mem_norm', 'w_mem_kv', 'norm1', 'w_out', 'norm2', 'w_up', 'w_down', 'norm_f', 's5_w_in',
           's5_lam_re', 's5_lam_im', 's5_log_dt', 's5_b_re', 's5_b_im', 's5_c_re', 's5_c_im', 's5_d_skip',
           's5_w_glu', 's5_b_glu', 'gdn_w_in', 'gdn_conv_w', 'gdn_a_log', 'gdn_dt_bias', 'gdn_o_norm',
           'fox_w_in', 'fox_b_f']
SHARD_AXIS = {'w_mem_kv': 0, 'w_out': 1, 'w_up': 2, 'w_down': 1, 's5_w_in': 1, 's5_d_skip': 1,
              's5_w_glu': 1, 's5_b_glu': 1, 'gdn_w_in': 2, 'gdn_conv_w': 2, 'fox_w_in': 2}
MATMUL_WEIGHTS = ['w_mem_kv', 'w_out', 'w_up', 'w_down', 's5_w_in', 's5_w_glu', 'gdn_w_in', 'fox_w_in']
VECTOR_SHARDED = ['s5_d_skip', 's5_b_glu', 'gdn_conv_w']
REPLICATED = [n for n in WEIGHTS if n not in SHARD_AXIS]


def _tile(dim, target, align=128):
    if dim <= target:
        return dim
    t = (target // align) * align
    while t >= align:
        if dim % t == 0:
            return t
        t -= align
    return dim


def _cp(sem=None, **kw):
    return pltpu.CompilerParams(dimension_semantics=sem, vmem_limit_bytes=VMEM_LIMIT_BYTES, **kw)


def _dot(a, b, dims):
    return lax.dot_general(a.astype(MXU_DTYPE), b.astype(MXU_DTYPE), dims, preferred_element_type=F32)


def _dotf(a, b, dims):
    return lax.dot_general(a, b, dims, precision=lax.Precision.HIGHEST, preferred_element_type=F32)


def _sigmoid(x):
    return 1.0 / (1.0 + jnp.exp(-x))


def _softplus(x):
    return jnp.maximum(x, 0.0) + jnp.log(1.0 + jnp.exp(-jnp.abs(x)))


def _relu2(x):
    r = jnp.maximum(x, 0.0)
    return r * r


_GELU_C = math.sqrt(2.0 / math.pi)


def _gelu(x):
    return 0.5 * x * (1.0 + jnp.tanh(_GELU_C * (x + 0.044715 * x * x * x)))


def _gelu_grad(x):
    t = jnp.tanh(_GELU_C * (x + 0.044715 * x * x * x))
    return 0.5 * (1.0 + t) + 0.5 * x * (1.0 - t * t) * _GELU_C * (1.0 + 3.0 * 0.044715 * x * x)


def _silu_grad(x):
    s = _sigmoid(x)
    return s + x * s * (1.0 - s)


def _mm(a, b, *, name, ta=False, tb=False, extras=(), epi=None, out_dtypes=(F32,)):
    K, M = a.shape if ta else a.shape[::-1]
    N = b.shape[0] if tb else b.shape[1]
    assert (b.shape[1] if tb else b.shape[0]) == K, (a.shape, b.shape, ta, tb)
    tm, tn, tk = _tile(M, MM_TM), _tile(N, MM_TN), _tile(K, MM_TK)
    nk = K // tk
    n_ex, n_out = len(extras), len(out_dtypes)
    dims = TN if ta else (NT if tb else NN)

    def body(*refs):
        a_ref, b_ref = refs[0], refs[1]
        ex = refs[2:2 + n_ex]
        outs = refs[2 + n_ex:2 + n_ex + n_out]
        acc = refs[-1]
        k = pl.program_id(2)

        @pl.when(k == 0)
        def _():
            acc[...] = jnp.zeros_like(acc)

        acc[...] += _dot(a_ref[...], b_ref[...], dims)

        @pl.when(k == nk - 1)
        def _():
            res = acc[...]
            vals = epi(res, *[e[...] for e in ex]) if epi is not None else (res,)
            for o, v in zip(outs, vals):
                o[...] = v.astype(o.dtype)

    if ta:
        a_spec = pl.BlockSpec((tk, tm), lambda i, j, k: (k, i))
    else:
        a_spec = pl.BlockSpec((tm, tk), lambda i, j, k: (i, k))
    if tb:
        b_spec = pl.BlockSpec((tn, tk), lambda i, j, k: (j, k))
    else:
        b_spec = pl.BlockSpec((tk, tn), lambda i, j, k: (k, j))
    ex_specs, ex_arrays = [], []
    for arr, kind in extras:
        if kind == 'ij':
            ex_specs.append(pl.BlockSpec((tm, tn), lambda i, j, k: (i, j)))
            ex_arrays.append(arr)
        else:
            ex_specs.append(pl.BlockSpec((1, tn), lambda i, j, k: (0, j)))
            ex_arrays.append(arr.reshape(1, N))
    outs = pl.pallas_call(
        body, name=name, grid=(M // tm, N // tn, nk),
        in_specs=[a_spec, b_spec] + ex_specs,
        out_specs=[pl.BlockSpec((tm, tn), lambda i, j, k: (i, j)) for _ in out_dtypes],
        out_shape=[jax.ShapeDtypeStruct((M, N), dt) for dt in out_dtypes],
        scratch_shapes=[pltpu.VMEM((tm, tn), F32)],
        compiler_params=_cp(("parallel", "parallel", "arbitrary")),
    )(a, b, *ex_arrays)
    return outs[0] if n_out == 1 else tuple(outs)


def _rms_fwd(x, g, out_dtype, name):
    S, D = x.shape
    tr = _tile(S, ROW_TILE, 8)

    def body(x_ref, g_ref, o_ref):
        xv = x_ref[...]
        r = lax.rsqrt(jnp.mean(xv * xv, axis=-1, keepdims=True) + EPS)
        o_ref[...] = (xv * r * g_ref[...]).astype(o_ref.dtype)

    return pl.pallas_call(
        body, name=name, grid=(S // tr,),
        in_specs=[pl.BlockSpec((tr, D), lambda i: (i, 0)), pl.BlockSpec((1, D), lambda i: (0, 0))],
        out_specs=pl.BlockSpec((tr, D), lambda i: (i, 0)),
        out_shape=jax.ShapeDtypeStruct((S, D), out_dtype),
        compiler_params=_cp(("parallel",)),
    )(x, g.reshape(1, D))


def _rms_bwd(x, g, dy, res, name):
    S, D = x.shape
    tr = _tile(S, ROW_TILE, 8)
    has_res = res is not None

    def body(*refs):
        if has_res:
            x_ref, g_ref, dy_ref, res_ref, dx_ref, dg_ref, dx16_ref = refs
        else:
            x_ref, g_ref, dy_ref, dx_ref, dg_ref, dx16_ref = refs
        i = pl.program_id(0)

        @pl.when(i == 0)
        def _():
            dg_ref[...] = jnp.zeros_like(dg_ref)

        xv, d = x_ref[...], dy_ref[...].astype(F32)
        r = lax.rsqrt(jnp.mean(xv * xv, axis=-1, keepdims=True) + EPS)
        xh = xv * r
        t = d * g_ref[...]
        dx = r * (t - xh * jnp.mean(t * xh, axis=-1, keepdims=True))
        if has_res:
            dx = dx + res_ref[...]
        dx_ref[...] = dx
        dx16_ref[...] = dx.astype(dx16_ref.dtype)
        dg_ref[...] += jnp.sum(d * xh, axis=0, keepdims=True)

    row = pl.BlockSpec((tr, D), lambda i: (i, 0))
    vec = pl.BlockSpec((1, D), lambda i: (0, 0))
    ins = [x, g.reshape(1, D), dy] + ([res] if has_res else [])
    return pl.pallas_call(
        body, name=name, grid=(S // tr,),
        in_specs=[row, vec, row] + ([row] if has_res else []),
        out_specs=[row, vec, row],
        out_shape=[jax.ShapeDtypeStruct((S, D), F32), jax.ShapeDtypeStruct((1, D), F32),
                   jax.ShapeDtypeStruct((S, D), MXU_DTYPE)],
        compiler_params=_cp(("arbitrary",)),
    )(*ins)


def _loss_head(h, g, target):
    S, D = h.shape
    tr = _tile(S, ROW_TILE, 8)

    def body(h_ref, g_ref, t_ref, loss_ref, dh_ref, dg_ref, dh16_ref):
        i = pl.program_id(0)

        @pl.when(i == 0)
        def _():
            loss_ref[...] = jnp.zeros_like(loss_ref)
            dg_ref[...] = jnp.zeros_like(dg_ref)

        xv = h_ref[...]
        gv = g_ref[...]
        r = lax.rsqrt(jnp.mean(xv * xv, axis=-1, keepdims=True) + EPS)
        xh = xv * r
        err = xh * gv - t_ref[...]
        part = 0.5 * jnp.sum(jnp.mean(err * err, axis=-1, keepdims=True), axis=0, keepdims=True)
        loss_ref[...] += jnp.broadcast_to(part, loss_ref.shape)
        d = err * (1.0 / D)
        t = d * gv
        dh = r * (t - xh * jnp.mean(t * xh, axis=-1, keepdims=True))
        dh_ref[...] = dh
        dh16_ref[...] = dh.astype(dh16_ref.dtype)
        dg_ref[...] += jnp.sum(d * xh, axis=0, keepdims=True)

    row = pl.BlockSpec((tr, D), lambda i: (i, 0))
    vec = pl.BlockSpec((1, D), lambda i: (0, 0))
    return pl.pallas_call(
        body, name="loss_head", grid=(S // tr,),
        in_specs=[row, vec, row],
        out_specs=[pl.BlockSpec((8, 128), lambda i: (0, 0)), row, vec, row],
        out_shape=[jax.ShapeDtypeStruct((8, 128), F32), jax.ShapeDtypeStruct((S, D), F32),
                   jax.ShapeDtypeStruct((1, D), F32), jax.ShapeDtypeStruct((S, D), MXU_DTYPE)],
        compiler_params=_cp(("arbitrary",)),
    )(h, g.reshape(1, D), target)


def _adamw(w, g, m, v, name):
    R, C = w.shape
    tr = _tile(R, max(8, (1 << 19) // max(C, 1) // 8 * 8), 8)
    c1 = 1.0 / (1.0 - ADAM_B1 ** ADAM_STEP)
    c2 = 1.0 / (1.0 - ADAM_B2 ** ADAM_STEP)

    def body(w_ref, g_ref, m_ref, v_ref, d_ref, nm_ref, nv_ref):
        gv = g_ref[...]
        nm = ADAM_B1 * m_ref[...] + (1.0 - ADAM_B1) * gv
        nv = ADAM_B2 * v_ref[...] + (1.0 - ADAM_B2) * (gv * gv)
        d_ref[...] = -ADAM_LR * ((nm * c1) / (jnp.sqrt(nv * c2) + ADAM_EPS) + ADAM_WD * w_ref[...])
        nm_ref[...] = nm
        nv_ref[...] = nv

    blk = pl.BlockSpec((tr, C), lambda i: (i, 0))
    return pl.pallas_call(
        body, name=name, grid=(R // tr,),
        in_specs=[blk] * 4, out_specs=[blk] * 3,
        out_shape=[jax.ShapeDtypeStruct((R, C), F32)] * 3,
        compiler_params=_cp(("parallel",)),
    )(w, g, m, v)


def _place():
    x, y, c = lax.axis_index("x"), lax.axis_index("y"), lax.axis_index("c")
    chips = [(1 - x, y), (x, 1 - y), (1 - x, 1 - y)]
    return x, y, c, chips


def _in_hbm(a):
    return pltpu.with_memory_space_constraint(a, pltpu.HBM)


def _ag_start(xs, name):
    r, n = xs.shape
    half = r // 2

    def body(x_ref, land_ref, send_sems, recv_sems, x_thru, land_thru, token):
        x, y, c, chips = _place()
        rows = pl.ds(c * half, half)
        for j, (cx, cy) in enumerate(chips):
            pltpu.make_async_remote_copy(
                src_ref=x_ref.at[rows, :], dst_ref=land_ref.at[2 * x + y, rows, :], send_sem=send_sems.at[j],
                recv_sem=recv_sems.at[j], device_id=(cx, cy, c), device_id_type=MESH).start()
        token[...] = jnp.zeros_like(token)

    sems = pltpu.SemaphoreType.DMA((3,))
    out = pl.pallas_call(
        body, name=name,
        out_shape=(sems, sems, pltpu.HBM(xs.shape, xs.dtype), pltpu.HBM((4, r, n), xs.dtype),
                   jax.ShapeDtypeStruct((8, 128), F32)),
        in_specs=(HBM_SPEC, HBM_SPEC),
        out_specs=(SEM_SPEC, SEM_SPEC, HBM_SPEC, HBM_SPEC, pl.BlockSpec(memory_space=pltpu.VMEM)),
        input_output_aliases={0: 2, 1: 3},
        compiler_params=pltpu.CompilerParams(has_side_effects=SPLIT_EFFECT),
    )(_in_hbm(xs), _in_hbm(lax.empty((4, r, n), xs.dtype)))
    return out[:4], out[4][0, 0]


def _ag_wait(handle, after, name):
    send_sems, recv_sems, xs, land = handle
    r, n = xs.shape
    half = r // 2

    def body(x_ref, land_ref, send_sems, recv_sems, after_ref, x_out, land_out):
        x, y, c, chips = _place()
        rows = pl.ds(c * half, half)
        for j, (cx, cy) in enumerate(chips):
            cp = pltpu.make_async_remote_copy(
                src_ref=x_ref.at[rows, :], dst_ref=land_ref.at[2 * cx + cy, rows, :], send_sem=send_sems.at[j],
                recv_sem=recv_sems.at[j], device_id=(cx, cy, c), device_id_type=MESH)
            cp.wait_send()
            cp.wait_recv()

    return pl.pallas_call(
        body, name=name,
        out_shape=(pltpu.HBM(xs.shape, xs.dtype), pltpu.HBM(land.shape, land.dtype)),
        in_specs=(HBM_SPEC, HBM_SPEC, SEM_SPEC, SEM_SPEC, pl.BlockSpec(memory_space=pl.ANY)),
        out_specs=(HBM_SPEC, HBM_SPEC),
        input_output_aliases={0: 0, 1: 1},
        compiler_params=pltpu.CompilerParams(has_side_effects=SPLIT_EFFECT),
    )(xs, land, send_sems, recv_sems, after)[1]


def _ag_forward(got, name):
    _, r, n = got.shape
    half = r // 2

    def body(g_ref, out_ref, send_sems, recv_sems):
        x, y, c, chips = _place()
        sibling = (x, y, 1 - c)
        sent = []
        for j, (cx, cy) in enumerate(chips):
            piece = out_ref.at[2 * cx + cy, pl.ds(c * half, half), :]
            cp = pltpu.make_async_remote_copy(src_ref=piece, dst_ref=piece, send_sem=send_sems.at[j],
                                              recv_sem=recv_sems.at[j], device_id=sibling, device_id_type=MESH)
            cp.start()
            sent.append(cp)
        for j, (cx, cy) in enumerate(chips):
            piece = out_ref.at[2 * cx + cy, pl.ds((1 - c) * half, half), :]
            pltpu.make_async_remote_copy(src_ref=piece, dst_ref=piece, send_sem=send_sems.at[j],
                                         recv_sem=recv_sems.at[j], device_id=sibling, device_id_type=MESH).wait_recv()
        for cp in sent:
            cp.wait_send()

    return pl.pallas_call(
        body, name=name,
        in_specs=[pl.BlockSpec(memory_space=pl.ANY)],
        out_specs=pl.BlockSpec(memory_space=pl.ANY),
        out_shape=jax.ShapeDtypeStruct(got.shape, got.dtype),
        input_output_aliases={0: 0},
        scratch_shapes=[pltpu.SemaphoreType.DMA((3,)), pltpu.SemaphoreType.DMA((3,))],
    )(got)


def _all_gather_chips(xs, name):
    r, n = xs.shape
    half = r // 2

    def body(x_ref, out_ref, send_sems, recv_sems):
        x, y, c, chips = _place()
        me = 2 * x + y
        sibling = (x, y, 1 - c)

        def piece(chip, hc):
            return out_ref.at[chip, pl.ds(hc * half, half), :]

        def copy(k, src, dst, to):
            return pltpu.make_async_remote_copy(src_ref=src, dst_ref=dst, send_sem=send_sems.at[k],
                                                recv_sem=recv_sems.at[k], device_id=to, device_id_type=MESH)

        src = x_ref.at[pl.ds(c * half, half), :]
        first = [copy(j, src, piece(me, c), (cx, cy, c)) for j, (cx, cy) in enumerate(chips)]
        for cp in first:
            cp.start()
        passed = []
        for j, (cx, cy) in enumerate(chips):
            got = piece(2 * cx + cy, c)
            copy(j, got, got, (cx, cy, c)).wait_recv()
            fwd = copy(3 + j, got, got, sibling)
            fwd.start()
            passed.append(fwd)
        for j, (cx, cy) in enumerate(chips):
            got = piece(2 * cx + cy, 1 - c)
            copy(3 + j, got, got, sibling).wait_recv()
        for cp in first + passed:
            cp.wait_send()

    return pl.pallas_call(
        body, name=name,
        in_specs=[pl.BlockSpec(memory_space=pl.ANY)],
        out_specs=pl.BlockSpec(memory_space=pl.ANY),
        out_shape=jax.ShapeDtypeStruct((4, r, n), xs.dtype),
        scratch_shapes=[pltpu.SemaphoreType.DMA((6,)), pltpu.SemaphoreType.DMA((6,))],
    )(xs)


def _gather(xs, me, name):
    return lax.dynamic_update_slice(_all_gather_chips(xs, name), xs[None], (me, 0, 0))


def _rs_swap_start(g, name):
    _, r, n = g.shape
    half = r // 2

    def body(g_ref, land_ref, send_sem, recv_sem, g_thru, land_thru, token):
        x, y, c, _ = _place()
        pltpu.make_async_remote_copy(
            src_ref=g_ref.at[:, pl.ds((1 - c) * half, half), :], dst_ref=land_ref,
            send_sem=send_sem, recv_sem=recv_sem, device_id=(x, y, 1 - c), device_id_type=MESH).start()
        token[...] = jnp.zeros_like(token)

    sem = pltpu.SemaphoreType.DMA(())
    out = pl.pallas_call(
        body, name=name,
        out_shape=(sem, sem, pltpu.HBM(g.shape, g.dtype), pltpu.HBM((4, half, n), g.dtype),
                   jax.ShapeDtypeStruct((8, 128), F32)),
        in_specs=(HBM_SPEC, HBM_SPEC),
        out_specs=(SEM_SPEC, SEM_SPEC, HBM_SPEC, HBM_SPEC, pl.BlockSpec(memory_space=pltpu.VMEM)),
        input_output_aliases={0: 2, 1: 3},
        compiler_params=pltpu.CompilerParams(has_side_effects=SPLIT_EFFECT),
    )(_in_hbm(g), _in_hbm(lax.empty((4, half, n), g.dtype)))
    return out[:4], out[4][0, 0]


def _rs_swap_wait(handle, after, name):
    send_sem, recv_sem, g, land = handle
    half = land.shape[1]

    def body(g_ref, land_ref, send_sem, recv_sem, after_ref, g_out, land_out):
        x, y, c, _ = _place()
        cp = pltpu.make_async_remote_copy(
            src_ref=g_ref.at[:, pl.ds((1 - c) * half, half), :], dst_ref=land_ref,
            send_sem=send_sem, recv_sem=recv_sem, device_id=(x, y, 1 - c), device_id_type=MESH)
        cp.wait_send()
        cp.wait_recv()

    return pl.pallas_call(
        body, name=name,
        out_shape=(pltpu.HBM(g.shape, g.dtype), pltpu.HBM(land.shape, land.dtype)),
        in_specs=(HBM_SPEC, HBM_SPEC, SEM_SPEC, SEM_SPEC, pl.BlockSpec(memory_space=pl.ANY)),
        out_specs=(HBM_SPEC, HBM_SPEC),
        input_output_aliases={0: 0, 1: 1},
        compiler_params=pltpu.CompilerParams(has_side_effects=SPLIT_EFFECT),
    )(g, land, send_sem, recv_sem, after)


def _rs_add_halves(g, got, c_idx, name):
    _, r, n = g.shape
    half = r // 2
    tr = _tile(half, max(16, (1 << 19) // n // 16 * 16), 16)
    nb = half // tr

    def body(c_ref, g_ref, o_ref, out_ref, out16_ref):
        sm = g_ref[...] + o_ref[...]
        out_ref[...] = sm
        out16_ref[...] = sm.astype(out16_ref.dtype)

    blk = pl.BlockSpec((None, tr, n), lambda s, i, c: (s, i, 0))
    return pl.pallas_call(
        body, name=name,
        grid_spec=pltpu.PrefetchScalarGridSpec(
            num_scalar_prefetch=1, grid=(4, nb),
            in_specs=[pl.BlockSpec((None, tr, n), lambda s, i, c: (s, c[0] * nb + i, 0)), blk],
            out_specs=[blk, blk]),
        out_shape=[jax.ShapeDtypeStruct((4, half, n), F32), jax.ShapeDtypeStruct((4, half, n), RS_PAYLOAD)],
        compiler_params=_cp(("parallel", "parallel")),
    )(c_idx, g, got)


def _rs_start(p32, p16, name):
    _, h, n = p16.shape

    def body(p32_ref, p16_ref, l16_ref, l32_ref, send_sems, recv_sems, p32_t, p16_t, l16_t, l32_t, token):
        x, y, c, chips = _place()
        for j, (cx, cy) in enumerate(chips):
            for k, pc in enumerate((c, 1 - c)):
                pltpu.make_async_remote_copy(
                    src_ref=p16_ref.at[2 * cx + cy], dst_ref=l16_ref.at[c, j], send_sem=send_sems.at[3 * k + j],
                    recv_sem=recv_sems.at[3 * k + j], device_id=(cx, cy, pc), device_id_type=MESH).start()
        pltpu.make_async_remote_copy(
            src_ref=p32_ref.at[2 * x + y], dst_ref=l32_ref, send_sem=send_sems.at[6], recv_sem=recv_sems.at[6],
            device_id=(x, y, 1 - c), device_id_type=MESH).start()
        token[...] = jnp.zeros_like(token)

    sems = pltpu.SemaphoreType.DMA((7,))
    l16 = lax.empty((2, 3, h, n), p16.dtype)
    l32 = lax.empty((h, n), F32)
    out = pl.pallas_call(
        body, name=name,
        out_shape=(sems, sems, pltpu.HBM(p32.shape, F32), pltpu.HBM(p16.shape, p16.dtype),
                   pltpu.HBM(l16.shape, l16.dtype), pltpu.HBM(l32.shape, F32), jax.ShapeDtypeStruct((8, 128), F32)),
        in_specs=(HBM_SPEC,) * 4,
        out_specs=(SEM_SPEC, SEM_SPEC) + (HBM_SPEC,) * 4 + (pl.BlockSpec(memory_space=pltpu.VMEM),),
        input_output_aliases={0: 2, 1: 3, 2: 4, 3: 5},
        compiler_params=pltpu.CompilerParams(has_side_effects=SPLIT_EFFECT),
    )(_in_hbm(p32), _in_hbm(p16), _in_hbm(l16), _in_hbm(l32))
    return out[:6], out[6][0, 0]


def _rs_wait(handle, after, name):
    send_sems, recv_sems, p32, p16, l16, l32 = handle

    def body(p32_ref, p16_ref, l16_ref, l32_ref, send_sems, recv_sems, after_ref, p32_o, p16_o, l16_o, l32_o):
        x, y, c, chips = _place()
        for j, (cx, cy) in enumerate(chips):
            for k, pc in enumerate((c, 1 - c)):
                cp = pltpu.make_async_remote_copy(
                    src_ref=p16_ref.at[2 * cx + cy], dst_ref=l16_ref.at[pc, j], send_sem=send_sems.at[3 * k + j],
                    recv_sem=recv_sems.at[3 * k + j], device_id=(cx, cy, pc), device_id_type=MESH)
                cp.wait_send()
                cp.wait_recv()
        cp = pltpu.make_async_remote_copy(
            src_ref=p32_ref.at[2 * x + y], dst_ref=l32_ref, send_sem=send_sems.at[6], recv_sem=recv_sems.at[6],
            device_id=(x, y, 1 - c), device_id_type=MESH)
        cp.wait_send()
        cp.wait_recv()

    out = pl.pallas_call(
        body, name=name,
        out_shape=(pltpu.HBM(p32.shape, F32), pltpu.HBM(p16.shape, p16.dtype), pltpu.HBM(l16.shape, l16.dtype),
                   pltpu.HBM(l32.shape, F32)),
        in_specs=(HBM_SPEC,) * 4 + (SEM_SPEC, SEM_SPEC, pl.BlockSpec(memory_space=pl.ANY)),
        out_specs=(HBM_SPEC,) * 4,
        input_output_aliases={0: 0, 1: 1, 2: 2, 3: 3},
        compiler_params=pltpu.CompilerParams(has_side_effects=SPLIT_EFFECT),
    )(p32, p16, l16, l32, send_sems, recv_sems, after)
    return out[0], out[2], out[3]


def _rs_finish(p32, l16, l32, c_idx, me_idx, name):
    _, h, n = p32.shape
    tr = _tile(h, max(16, (1 << 18) // n // 16 * 16), 16)
    nb = h // tr

    def body(c_ref, me_ref, own_ref, sib_ref, a_ref, b_ref, d_ref, out_ref):
        base = jnp.where(pl.program_id(0) == c_ref[0], own_ref[...], sib_ref[...])
        out_ref[...] = ((base + a_ref[...].astype(F32)) + b_ref[...].astype(F32)) + d_ref[...].astype(F32)

    def piece(j):
        return pl.BlockSpec((None, None, tr, n), lambda hc, i, c, me: (hc, j, i, 0))

    return pl.pallas_call(
        body, name=name,
        grid_spec=pltpu.PrefetchScalarGridSpec(
            num_scalar_prefetch=2, grid=(2, nb),
            in_specs=[pl.BlockSpec((None, tr, n), lambda hc, i, c, me: (me[0], i, 0)),
                      pl.BlockSpec((tr, n), lambda hc, i, c, me: (i, 0)),
                      piece(0), piece(1), piece(2)],
            out_specs=pl.BlockSpec((tr, n), lambda hc, i, c, me: (hc * nb + i, 0))),
        out_shape=jax.ShapeDtypeStruct((2 * h, n), F32),
        compiler_params=_cp(("parallel", "parallel")),
    )(c_idx, me_idx, p32, l32, l16, l16, l16)


def _rs_begin(swap, after, place, tag):
    g, got = _rs_swap_wait(swap, after, "rs_swap_wait_" + tag)
    p32, p16 = _rs_add_halves(g, got, place['c_idx'], "rs_add_halves")
    return _rs_start(p32, p16, "rs_start_" + tag)


def _rs_end(handle, after, place, tag):
    p32, l16, l32 = _rs_wait(handle, after, "rs_wait_" + tag)
    return _rs_finish(p32, l16, l32, place['c_idx'], place['me_idx'], "rs_finish")


def _all_reduce_small(v, name):
    R, n = v.shape

    def body(v_ref, out_ref, buf, send_sems, recv_sems):
        x, y, c, _ = _place()
        me = 4 * x + 2 * y + c
        buf[me] = v_ref[...]
        copies = []
        for d in range(1, 8):
            dx, dy, dc = (d >> 2) & 1, (d >> 1) & 1, d & 1
            px = x if dx == 0 else 1 - x
            py = y if dy == 0 else 1 - y
            pc = c if dc == 0 else 1 - c
            copies.append(pltpu.make_async_remote_copy(
                src_ref=v_ref, dst_ref=buf.at[me], send_sem=send_sems.at[d - 1], recv_sem=recv_sems.at[d - 1],
                device_id=(px, py, pc), device_id_type=MESH))
        for cp in copies:
            cp.start()
        for d in range(1, 8):
            dx, dy, dc = (d >> 2) & 1, (d >> 1) & 1, d & 1
            px = x if dx == 0 else 1 - x
            py = y if dy == 0 else 1 - y
            pc = c if dc == 0 else 1 - c
            pltpu.make_async_remote_copy(
                src_ref=v_ref, dst_ref=buf.at[4 * px + 2 * py + pc], send_sem=send_sems.at[d - 1],
                recv_sem=recv_sems.at[d - 1], device_id=(px, py, pc), device_id_type=MESH).wait_recv()
        for cp in copies:
            cp.wait_send()
        acc = buf[0]
        for k in range(1, 8):
            acc = acc + buf[k]
        out_ref[...] = acc

    return pl.pallas_call(
        body, name=name,
        in_specs=[pl.BlockSpec(memory_space=pltpu.VMEM)],
        out_specs=pl.BlockSpec(memory_space=pltpu.VMEM),
        out_shape=jax.ShapeDtypeStruct((R, n), F32),
        scratch_shapes=[pltpu.VMEM((8, R, n), F32), pltpu.SemaphoreType.DMA((7,)), pltpu.SemaphoreType.DMA((7,))],
        compiler_params=pltpu.CompilerParams(vmem_limit_bytes=VMEM_LIMIT_BYTES),
    )(v)


def _pack_small(parts, rows):
    flat = jnp.concatenate([a.reshape(-1) for a in parts])
    return jnp.pad(flat, (0, rows * LANES - flat.shape[0])).reshape(rows, LANES)


def _mem_fwd(proj, q_blk, mkv, heads):
    S = proj.shape[0]
    ML = mkv.shape[0]
    t = _tile(S, MEM_TILE, 8)
    scale = HEAD ** -0.5

    def body(q_ref, k_ref, v_ref, o_ref):
        s = _dot(q_ref[...], k_ref[...], NT) * scale
        m = jnp.max(s, axis=-1, keepdims=True)
        e = jnp.exp(s - m)
        p = e / jnp.sum(e, axis=-1, keepdims=True)
        o_ref[...] = _dot(p, v_ref[...], NN)

    return pl.pallas_call(
        body, name="mem_fwd", grid=(S // t, heads),
        in_specs=[pl.BlockSpec((t, HEAD), lambda i, h: (i, q_blk + h)),
                  pl.BlockSpec((ML, HEAD), lambda i, h: (0, h)),
                  pl.BlockSpec((ML, HEAD), lambda i, h: (0, heads + h))],
        out_specs=pl.BlockSpec((t, HEAD), lambda i, h: (i, h)),
        out_shape=jax.ShapeDtypeStruct((S, heads * HEAD), F32),
        compiler_params=_cp(("parallel", "parallel")),
    )(proj, mkv, mkv)


def _mem_bwd(proj, q_blk, mkv, dcat, d_blk, heads):
    S = proj.shape[0]
    ML = mkv.shape[0]
    t = _tile(S, MEM_TILE, 8)
    scale = HEAD ** -0.5

    def body(q_ref, k_ref, v_ref, do_ref, dq_ref, dk_ref, dv_ref):
        i = pl.program_id(1)

        @pl.when(i == 0)
        def _():
            dk_ref[...] = jnp.zeros_like(dk_ref)
            dv_ref[...] = jnp.zeros_like(dv_ref)

        q, k, v, do = q_ref[...], k_ref[...], v_ref[...], do_ref[...]
        s = _dot(q, k, NT) * scale
        m = jnp.max(s, axis=-1, keepdims=True)
        e = jnp.exp(s - m)
        p = e / jnp.sum(e, axis=-1, keepdims=True)
        dp = _dot(do, v, NT)
        ds = p * (dp - jnp.sum(p * dp, axis=-1, keepdims=True))
        dq_ref[...] = _dot(ds, k, NN) * scale
        dk_ref[...] += _dot(ds, q, TN) * scale
        dv_ref[...] += _dot(p, do, TN)

    dq, dk, dv = pl.pallas_call(
        body, name="mem_bwd", grid=(heads, S // t),
        in_specs=[pl.BlockSpec((t, HEAD), lambda h, i: (i, q_blk + h)),
                  pl.BlockSpec((ML, HEAD), lambda h, i: (0, h)),
                  pl.BlockSpec((ML, HEAD), lambda h, i: (0, heads + h)),
                  pl.BlockSpec((t, HEAD), lambda h, i: (i, d_blk + h))],
        out_specs=[pl.BlockSpec((t, HEAD), lambda h, i: (i, h)),
                   pl.BlockSpec((ML, HEAD), lambda h, i: (0, h)),
                   pl.BlockSpec((ML, HEAD), lambda h, i: (0, h))],
        out_shape=[jax.ShapeDtypeStruct((S, heads * HEAD), F32),
                   jax.ShapeDtypeStruct((ML, heads * HEAD), F32),
                   jax.ShapeDtypeStruct((ML, heads * HEAD), F32)],
        compiler_params=_cp(("parallel", "arbitrary")),
    )(proj, mkv, mkv, dcat)
    return dq, jnp.concatenate([dk, dv], axis=1)


def _fox_gates(gl, bf):
    S = gl.shape[0]

    def body(g_ref, b_ref, o_ref):
        xv = g_ref[...] + b_ref[...]
        c = jnp.minimum(xv, 0.0) - jnp.log(1.0 + jnp.exp(-jnp.abs(xv)))
        row = lax.broadcasted_iota(jnp.int32, c.shape, 0)
        d = 1
        while d < S:
            c = c + jnp.where(row >= d, pltpu.roll(c, d, 0), 0.0)
            d *= 2
        o_ref[...] = c

    return pl.pallas_call(
        body, name="fox_gates", out_shape=jax.ShapeDtypeStruct((S, 128), F32),
        in_specs=[pl.BlockSpec(memory_space=pltpu.VMEM)] * 2,
        out_specs=pl.BlockSpec(memory_space=pltpu.VMEM),
        compiler_params=_cp(),
    )(gl, bf)


def _fox_gates_bwd(gl, bf, dcf):
    S = gl.shape[0]

    def body(g_ref, b_ref, d_ref, dg_ref, db_ref):
        c = d_ref[...]
        row = lax.broadcasted_iota(jnp.int32, c.shape, 0)
        d = 1
        while d < S:
            c = c + jnp.where(row < S - d, pltpu.roll(c, S - d, 0), 0.0)
            d *= 2
        dx = c * _sigmoid(-(g_ref[...] + b_ref[...]))
        dg_ref[...] = dx
        db_ref[...] = jnp.sum(dx, axis=0, keepdims=True)

    return pl.pallas_call(
        body, name="fox_gates_bwd",
        out_shape=[jax.ShapeDtypeStruct((S, 128), F32), jax.ShapeDtypeStruct((1, 128), F32)],
        in_specs=[pl.BlockSpec(memory_space=pltpu.VMEM)] * 3,
        out_specs=[pl.BlockSpec(memory_space=pltpu.VMEM)] * 2,
        compiler_params=_cp(),
    )(gl, bf, dcf)


def _fox_scores(q, k, cq, ck, t, masked):
    s = _dot(q, k, NT) * (HEAD ** -0.5) + cq - ck
    if masked:
        row = lax.broadcasted_iota(jnp.int32, (t, t), 0)
        col = lax.broadcasted_iota(jnp.int32, (t, t), 1)
        s = jnp.where(row >= col, s, -jnp.inf)
    return s


def _fox_pairs(nq, by_key):
    if by_key:
        pairs = [(i, j) for j in range(nq) for i in range(j, nq)]
    else:
        pairs = [(i, j) for i in range(nq) for j in range(i + 1)]
    return (jnp.asarray(np.array([a for a, _ in pairs], np.int32)),
            jnp.asarray(np.array([b for _, b in pairs], np.int32)))


def _fox_heads_per_step(H):
    return 2 if H % 2 == 0 else 1


def _fox_fwd(proj, cfq, cfk, H):
    S = proj.shape[0]
    t = _tile(S, FOX_TILE)
    nq = S // t
    hb = _fox_heads_per_step(H)
    W, G = hb * HEAD, H // hb
    cols = [slice(i * HEAD, (i + 1) * HEAD) for i in range(hb)]
    qt, kt = _fox_pairs(nq, False)

    def body(qt_ref, kt_ref, q_ref, k_ref, v_ref, cq_ref, ck_ref, o_ref, lse_ref, m_s, l_s, acc_s):
        n = pl.program_id(1)
        qi, ki = qt_ref[n], kt_ref[n]

        @pl.when(ki == 0)
        def _():
            m_s[...] = jnp.full_like(m_s, -jnp.inf)
            l_s[...] = jnp.zeros_like(l_s)
            acc_s[...] = jnp.zeros_like(acc_s)

        def step(masked):
            R = range(hb)
            ss = [_fox_scores(q_ref[:, cols[i]], k_ref[:, cols[i]], cq_ref[i], ck_ref[i], t, masked) for i in R]
            m_new = [jnp.maximum(m_s[i], jnp.max(ss[i], axis=-1, keepdims=True)) for i in R]
            alpha = [jnp.exp(m_s[i] - m_new[i]) for i in R]
            ps = [jnp.exp(ss[i] - m_new[i]) for i in R]
            pv = [_dot(ps[i], v_ref[:, cols[i]], NN) for i in R]
            for i in R:
                l_s[i] = alpha[i] * l_s[i] + jnp.sum(ps[i], axis=-1, keepdims=True)
                acc_s[:, cols[i]] = alpha[i] * acc_s[:, cols[i]] + pv[i]
                m_s[i] = m_new[i]

        @pl.when(ki != qi)
        def _():
            step(False)

        @pl.when(ki == qi)
        def _():
            step(True)
            for i in range(hb):
                o_ref[:, cols[i]] = acc_s[:, cols[i]] / l_s[i]
                lse_ref[i] = m_s[i] + jnp.log(l_s[i])

    qcol = pl.BlockSpec((hb, t, 1), lambda h, n, qt, kt: (h, qt[n], 0))
    return pl.pallas_call(
        body, name="fox_fwd",
        grid_spec=pltpu.PrefetchScalarGridSpec(
            num_scalar_prefetch=2, grid=(G, qt.shape[0]),
            in_specs=[pl.BlockSpec((t, W), lambda h, n, qt, kt: (qt[n], h)),
                      pl.BlockSpec((t, W), lambda h, n, qt, kt: (kt[n], G + h)),
                      pl.BlockSpec((t, W), lambda h, n, qt, kt: (kt[n], 2 * G + h)),
                      qcol,
                      pl.BlockSpec((hb, 1, t), lambda h, n, qt, kt: (h, 0, kt[n]))],
            out_specs=[pl.BlockSpec((t, W), lambda h, n, qt, kt: (qt[n], h)), qcol],
            scratch_shapes=[pltpu.VMEM((hb, t, 1), F32), pltpu.VMEM((hb, t, 1), F32), pltpu.VMEM((t, W), F32)]),
        out_shape=[jax.ShapeDtypeStruct((S, H * HEAD), F32), jax.ShapeDtypeStruct((H, S, 1), F32)],
        compiler_params=_cp(("parallel", "arbitrary")),
    )(qt, kt, proj, proj, proj, cfq, cfk)


def _fox_bwd_rowdot(proj, cfq, cfk, lse, dcat, H):
    S = proj.shape[0]
    t = _tile(S, FOX_TILE)
    nq = S // t
    hb = _fox_heads_per_step(H)
    W, G = hb * HEAD, H // hb
    cols = [slice(i * HEAD, (i + 1) * HEAD) for i in range(hb)]
    qt, kt = _fox_pairs(nq, False)

    def body(qt_ref, kt_ref, q_ref, k_ref, v_ref, do_ref, lse_ref, cq_ref, ck_ref, d_ref):
        n = pl.program_id(1)
        qi, ki = qt_ref[n], kt_ref[n]

        @pl.when(ki == 0)
        def _():
            d_ref[...] = jnp.zeros_like(d_ref)

        def step(masked):
            R = range(hb)
            ss = [_fox_scores(q_ref[:, cols[i]], k_ref[:, cols[i]], cq_ref[i], ck_ref[i], t, masked) for i in R]
            dps = [_dot(do_ref[:, cols[i]], v_ref[:, cols[i]], NT) for i in R]
            ps = [jnp.exp(ss[i] - lse_ref[i]) for i in R]
            for i in R:
                d_ref[i] += jnp.sum(ps[i] * dps[i], axis=-1, keepdims=True)

        @pl.when(ki != qi)
        def _():
            step(False)

        @pl.when(ki == qi)
        def _():
            step(True)

    qtile = pl.BlockSpec((t, W), lambda h, n, qt, kt: (qt[n], h))
    qcol = pl.BlockSpec((hb, t, 1), lambda h, n, qt, kt: (h, qt[n], 0))
    return pl.pallas_call(
        body, name="fox_bwd_rowdot",
        grid_spec=pltpu.PrefetchScalarGridSpec(
            num_scalar_prefetch=2, grid=(G, qt.shape[0]),
            in_specs=[qtile,
                      pl.BlockSpec((t, W), lambda h, n, qt, kt: (kt[n], G + h)),
                      pl.BlockSpec((t, W), lambda h, n, qt, kt: (kt[n], 2 * G + h)),
                      qtile, qcol, qcol,
                      pl.BlockSpec((hb, 1, t), lambda h, n, qt, kt: (h, 0, kt[n]))],
            out_specs=qcol),
        out_shape=jax.ShapeDtypeStruct((H, S, 1), F32),
        compiler_params=_cp(("parallel", "arbitrary")),
    )(qt, kt, proj, proj, proj, dcat, lse, cfq, cfk)


def _fox_bwd(proj, cfq, cfk, rowdot, lse, dcat, H):
    S = proj.shape[0]
    t = _tile(S, FOX_TILE)
    nq = S // t
    scale = HEAD ** -0.5
    hb = _fox_heads_per_step(H)
    W, G = hb * HEAD, H // hb
    cols = [slice(i * HEAD, (i + 1) * HEAD) for i in range(hb)]
    qt, kt = _fox_pairs(nq, True)

    def body(qt_ref, kt_ref, q_ref, k_ref, v_ref, dd_ref, do_ref, lse_ref, cq_ref, ck_ref,
             dq_ref, dk_ref, dv_ref, dck_ref):
        n = pl.program_id(1)
        i_, j_ = qt_ref[n], kt_ref[n]

        @pl.when(n == 0)
        def _():
            dq_ref[...] = jnp.zeros_like(dq_ref)

        @pl.when(i_ == j_)
        def _():
            dk_ref[...] = jnp.zeros_like(dk_ref)
            dv_ref[...] = jnp.zeros_like(dv_ref)
            dck_ref[...] = jnp.zeros_like(dck_ref)

        def step(masked):
            R = range(hb)
            qs, ks = [q_ref[:, c] for c in cols], [k_ref[:, c] for c in cols]
            dos = [do_ref[:, c] for c in cols]
            ss = [_fox_scores(qs[i], ks[i], cq_ref[i], ck_ref[i], t, masked) for i in R]
            dps = [_dot(dos[i], v_ref[:, cols[i]], NT) for i in R]
            ps = [jnp.exp(ss[i] - lse_ref[i]) for i in R]
            dss = [ps[i] * (dps[i] - dd_ref[i]) for i in R]
            dvs = [_dot(ps[i], dos[i], TN) for i in R]
            dks = [_dot(dss[i], qs[i], TN) * scale for i in R]
            dqs = [_dot(dss[i], ks[i], NN) * scale for i in R]
            rows = pl.ds(pl.multiple_of(i_ * t, t), t)
            for i in R:
                dv_ref[:, cols[i]] += dvs[i]
                dk_ref[:, cols[i]] += dks[i]
                dq_ref[rows, cols[i]] += dqs[i]
                dck_ref[i] -= jnp.sum(dss[i], axis=0, keepdims=True)

        @pl.when(i_ != j_)
        def _():
            step(False)

        @pl.when(i_ == j_)
        def _():
            step(True)

    qtile = pl.BlockSpec((t, W), lambda h, n, qt, kt: (qt[n], h))
    qcol = pl.BlockSpec((hb, t, 1), lambda h, n, qt, kt: (h, qt[n], 0))
    ktile = pl.BlockSpec((t, W), lambda h, n, qt, kt: (kt[n], h))
    krow = pl.BlockSpec((hb, 1, t), lambda h, n, qt, kt: (h, 0, kt[n]))
    return pl.pallas_call(
        body, name="fox_bwd",
        grid_spec=pltpu.PrefetchScalarGridSpec(
            num_scalar_prefetch=2, grid=(G, qt.shape[0]),
            in_specs=[qtile,
                      pl.BlockSpec((t, W), lambda h, n, qt, kt: (kt[n], G + h)),
                      pl.BlockSpec((t, W), lambda h, n, qt, kt: (kt[n], 2 * G + h)),
                      qcol, qtile, qcol, qcol, krow],
            out_specs=[pl.BlockSpec((S, W), lambda h, n, qt, kt: (0, h)), ktile, ktile, krow]),
        out_shape=[jax.ShapeDtypeStruct((S, H * HEAD), F32)] * 3 + [jax.ShapeDtypeStruct((H, 1, S), F32)],
        compiler_params=_cp(("parallel", "arbitrary")),
    )(qt, kt, proj, proj, proj, rowdot, dcat, lse, cfq, cfk)


def _s5_prep(lam_re, lam_im, log_dt, b_re, b_im, c_re, c_im):
    G, P = lam_re.shape
    ns = G // 16
    dt = jnp.exp(log_dt)[:, None]
    mag = jnp.exp(lam_re * dt)
    a_re, a_im = mag * jnp.cos(lam_im * dt), mag * jnp.sin(lam_im * dt)
    den = lam_re * lam_re + lam_im * lam_im
    z_re = ((a_re - 1.0) * lam_re + a_im * lam_im) / den
    z_im = (a_im * lam_re - (a_re - 1.0) * lam_im) / den
    bb_re = z_re[..., None] * b_re - z_im[..., None] * b_im
    bb_im = z_re[..., None] * b_im + z_im[..., None] * b_re
    eye = jnp.eye(16, dtype=F32)
    bb = jnp.stack([bb_re, bb_im]).reshape(2, ns, 16, P, S5_GROUP)
    wb = jnp.einsum('asgpc,gh->sgcahp', bb, eye).reshape(ns, S5_SLAB, 2 * 16 * P)
    cc = jnp.stack([c_re, -c_im]).reshape(2, ns, 16, S5_GROUP, P)
    wc = jnp.einsum('asgcp,gh->sagphc', cc, eye).reshape(ns, 2 * 16 * P, S5_SLAB)
    a = jnp.concatenate([a_re.reshape(ns, 1, 16 * P), a_im.reshape(ns, 1, 16 * P)], axis=-1)
    return wb, wc, a


def _s5_tables(lam_re, lam_im, log_dt):
    G, P = lam_re.shape
    ns = G // 16
    dt = jnp.exp(log_dt)[:, None]
    tt = jnp.arange(1, S5_CHUNK + 1, dtype=F32)[:, None, None]
    mag = jnp.exp(lam_re * dt * tt)
    ang = lam_im * dt * tt
    pr = (mag * jnp.cos(ang)).reshape(S5_CHUNK, ns, 16 * P).transpose(1, 0, 2)
    pi = (mag * jnp.sin(ang)).reshape(S5_CHUNK, ns, 16 * P).transpose(1, 0, 2)
    return pr, pi, pr[:, ::-1], pi[:, ::-1]


def _s5_scan_fwd(proj, wb, wc, pr, pi, dskip):
    S = proj.shape[0]
    ns = wb.shape[0]
    W = wb.shape[2]
    hw = W // 2
    T = S5_CHUNK
    nc = S // T
    mix = ns * S5_SLAB

    def body(u_ref, wb_ref, wc_ref, pr_ref, pi_ref, d_ref, v_ref, yg_ref, h_ref, cin_ref, carry):
        c = pl.program_id(1)

        @pl.when(c == 0)
        def _():
            carry[...] = jnp.zeros_like(carry)

        u = u_ref[...]
        bu = _dot(u, wb_ref[...], NN)
        xr, xi = bu[:, :hw], bu[:, hw:]
        sub = lax.broadcasted_iota(jnp.int32, (T, hw), 0) & (S5_ROWS - 1)
        d = 1
        while d < S5_ROWS:
            ar, ai = pr_ref[pl.ds(d - 1, 1), :], pi_ref[pl.ds(d - 1, 1), :]
            sr = jnp.where(sub >= d, pltpu.roll(xr, d, 0), 0.0)
            si = jnp.where(sub >= d, pltpu.roll(xi, d, 0), 0.0)
            xr, xi = xr + ar * sr - ai * si, xi + ar * si + ai * sr
            d *= 2
        cin_ref[...] = carry[...]
        cr, ci = carry[:, :hw], carry[:, hw:]
        pwr, pwi = pr_ref[pl.ds(0, S5_ROWS), :], pi_ref[pl.ds(0, S5_ROWS), :]
        for g in range(T // S5_ROWS):
            rows = slice(g * S5_ROWS, (g + 1) * S5_ROWS)
            hr = xr[rows, :] + pwr * cr - pwi * ci
            hi = xi[rows, :] + pwr * ci + pwi * cr
            h_ref[rows, :hw] = hr
            h_ref[rows, hw:] = hi
            cr, ci = hr[S5_ROWS - 1:S5_ROWS, :], hi[S5_ROWS - 1:S5_ROWS, :]
        carry[:, :hw] = cr
        carry[:, hw:] = ci
        y = _dot(h_ref[...], wc_ref[...], NN)
        v = y + d_ref[...] * u
        v_ref[...] = v
        yg_ref[...] = _gelu(v)

    return pl.pallas_call(
        body, name="s5_scan_fwd", grid=(ns, nc),
        in_specs=[pl.BlockSpec((T, S5_SLAB), lambda s, c: (c, s)),
                  pl.BlockSpec((None, S5_SLAB, W), lambda s, c: (s, 0, 0)),
                  pl.BlockSpec((None, W, S5_SLAB), lambda s, c: (s, 0, 0)),
                  pl.BlockSpec((None, T, hw), lambda s, c: (s, 0, 0)),
                  pl.BlockSpec((None, T, hw), lambda s, c: (s, 0, 0)),
                  pl.BlockSpec((1, S5_SLAB), lambda s, c: (0, s))],
        out_specs=[pl.BlockSpec((T, S5_SLAB), lambda s, c: (c, s)),
                   pl.BlockSpec((T, S5_SLAB), lambda s, c: (c, s)),
                   pl.BlockSpec((T, W), lambda s, c: (c, s)),
                   pl.BlockSpec((None, 1, W), lambda s, c: (c, 0, s))],
        out_shape=[jax.ShapeDtypeStruct((S, mix), F32), jax.ShapeDtypeStruct((S, mix), F32),
                   jax.ShapeDtypeStruct((S, ns * W), F32), jax.ShapeDtypeStruct((nc, 1, ns * W), F32)],
        scratch_shapes=[pltpu.VMEM((1, W), F32)],
        compiler_params=_cp(("parallel", "arbitrary")),
    )(proj, wb, wc, pr, pi, dskip)


def _s5_scan_bwd(dv, proj, hs, cin, wb, wc, pr, pi, prr, pir, dskip):
    S = proj.shape[0]
    ns = wb.shape[0]
    W = wb.shape[2]
    hw = W // 2
    T = S5_CHUNK
    nc = S // T
    mix = ns * S5_SLAB

    def body(dv_ref, u_ref, h_ref, cin_ref, wb_ref, wc_ref, pr_ref, pi_ref, prr_ref, pir_ref, d_ref,
             du_ref, dwb_ref, dwc_ref, da_ref, dd_ref, lam_s, carry):
        c = pl.program_id(1)

        @pl.when(c == 0)
        def _():
            carry[...] = jnp.zeros_like(carry)
            dwb_ref[...] = jnp.zeros_like(dwb_ref)
            dwc_ref[...] = jnp.zeros_like(dwc_ref)
            da_ref[...] = jnp.zeros_like(da_ref)
            dd_ref[...] = jnp.zeros_like(dd_ref)

        dy, u = dv_ref[...], u_ref[...]
        dh = _dot(dy, wc_ref[...], NT)
        gr, gi = dh[:, :hw], dh[:, hw:]
        row = lax.broadcasted_iota(jnp.int32, (T, hw), 0)
        sub = row & (S5_ROWS - 1)
        d = 1
        while d < S5_ROWS:
            ar, ai = pr_ref[pl.ds(d - 1, 1), :], -pi_ref[pl.ds(d - 1, 1), :]
            sr = jnp.where(sub < S5_ROWS - d, pltpu.roll(gr, T - d, 0), 0.0)
            si = jnp.where(sub < S5_ROWS - d, pltpu.roll(gi, T - d, 0), 0.0)
            gr, gi = gr + ar * sr - ai * si, gi + ar * si + ai * sr
            d *= 2
        lr, li = carry[:, :hw], carry[:, hw:]
        pwr, pwi = prr_ref[pl.ds(T - S5_ROWS, S5_ROWS), :], -pir_ref[pl.ds(T - S5_ROWS, S5_ROWS), :]
        for g in reversed(range(T // S5_ROWS)):
            rows = slice(g * S5_ROWS, (g + 1) * S5_ROWS)
            lgr = gr[rows, :] + pwr * lr - pwi * li
            lgi = gi[rows, :] + pwr * li + pwi * lr
            lam_s[rows, :hw] = lgr
            lam_s[rows, hw:] = lgi
            lr, li = lgr[0:1, :], lgi[0:1, :]
        carry[:, :hw] = lr
        carry[:, hw:] = li
        gr, gi = lam_s[:, :hw], lam_s[:, hw:]
        hr, hi = h_ref[:, :hw], h_ref[:, hw:]
        hpr = jnp.where(row >= 1, pltpu.roll(hr, 1, 0), cin_ref[:, :hw])
        hpi = jnp.where(row >= 1, pltpu.roll(hi, 1, 0), cin_ref[:, hw:])
        da_ref[:, :hw] += jnp.sum(hpr * gr + hpi * gi, axis=0, keepdims=True)
        da_ref[:, hw:] += jnp.sum(hpr * gi - hpi * gr, axis=0, keepdims=True)
        lam = lam_s[...]
        du_ref[...] = _dot(lam, wb_ref[...], NT) + dy * d_ref[...]
        dwb_ref[...] += _dot(u, lam, TN)
        dwc_ref[...] += _dot(h_ref[...], dy, TN)
        dd_ref[...] += jnp.sum(dy * u, axis=0, keepdims=True)

    def rc(c):
        return nc - 1 - c

    return pl.pallas_call(
        body, name="s5_scan_bwd", grid=(ns, nc),
        in_specs=[pl.BlockSpec((T, S5_SLAB), lambda s, c: (rc(c), s)),
                  pl.BlockSpec((T, S5_SLAB), lambda s, c: (rc(c), s)),
                  pl.BlockSpec((T, W), lambda s, c: (rc(c), s)),
                  pl.BlockSpec((None, 1, W), lambda s, c: (rc(c), 0, s)),
                  pl.BlockSpec((None, S5_SLAB, W), lambda s, c: (s, 0, 0)),
                  pl.BlockSpec((None, W, S5_SLAB), lambda s, c: (s, 0, 0)),
                  pl.BlockSpec((None, T, hw), lambda s, c: (s, 0, 0)),
                  pl.BlockSpec((None, T, hw), lambda s, c: (s, 0, 0)),
                  pl.BlockSpec((None, T, hw), lambda s, c: (s, 0, 0)),
                  pl.BlockSpec((None, T, hw), lambda s, c: (s, 0, 0)),
                  pl.BlockSpec((1, S5_SLAB), lambda s, c: (0, s))],
        out_specs=[pl.BlockSpec((T, S5_SLAB), lambda s, c: (rc(c), s)),
                   pl.BlockSpec((None, S5_SLAB, W), lambda s, c: (s, 0, 0)),
                   pl.BlockSpec((None, W, S5_SLAB), lambda s, c: (s, 0, 0)),
                   pl.BlockSpec((None, 1, W), lambda s, c: (s, 0, 0)),
                   pl.BlockSpec((1, S5_SLAB), lambda s, c: (0, s))],
        out_shape=[jax.ShapeDtypeStruct((S, mix), F32), jax.ShapeDtypeStruct(wb.shape, F32),
                   jax.ShapeDtypeStruct(wc.shape, F32), jax.ShapeDtypeStruct((ns, 1, W), F32),
                   jax.ShapeDtypeStruct((1, mix), F32)],
        scratch_shapes=[pltpu.VMEM((T, W), F32), pltpu.VMEM((1, W), F32)],
        compiler_params=_cp(("parallel", "arbitrary")),
    )(dv, proj, hs, cin, wb, wc, pr, pi, prr, pir, dskip)


def _s5_glu_bwd(dcat, yg, z):
    S, mix = yg.shape
    tr = _tile(S, ROW_TILE, 8)

    def body(do_ref, yg_ref, z_ref, dz_ref, dy_ref, db_ref):
        i = pl.program_id(0)

        @pl.when(i == 0)
        def _():
            db_ref[...] = jnp.zeros_like(db_ref)

        do, yg_, sz = do_ref[...], yg_ref[...], _sigmoid(z_ref[...])
        dz = do * yg_ * sz * (1.0 - sz)
        dz_ref[...] = dz
        dy_ref[...] = do * sz
        db_ref[...] += jnp.sum(dz, axis=0, keepdims=True)

    blk = pl.BlockSpec((tr, mix), lambda i: (i, 0))
    return pl.pallas_call(
        body, name="s5_glu_bwd", grid=(S // tr,),
        in_specs=[blk, blk, blk], out_specs=[blk, blk, pl.BlockSpec((1, mix), lambda i: (0, 0))],
        out_shape=[jax.ShapeDtypeStruct((S, mix), F32), jax.ShapeDtypeStruct((S, mix), F32),
                   jax.ShapeDtypeStruct((1, mix), F32)],
        compiler_params=_cp(("arbitrary",)),
    )(dcat, yg, z)


def _rows_down(x, j):
    return x if j == 0 else pltpu.roll(x, j, 0)


def _conv_rows(xe, w_ref, n):
    c = None
    for j in range(GDN_CONV):
        term = w_ref[pl.ds(GDN_CONV - 1 - j, 1), :] * _rows_down(xe, j)[8:8 + n, :]
        c = term if c is None else c + term
    return c


def _gdn_prep(proj, blk0, nblk, convw, norm, scale, name):
    S = proj.shape[0]
    tr = _tile(S, CONV_TILE, 8)
    nb8 = tr // 8

    def body(x_ref, xb_ref, w_ref, o_ref):
        i = pl.program_id(1)
        xe = jnp.concatenate([jnp.where(i == 0, 0.0, xb_ref[...]), x_ref[...]], axis=0)
        c = _conv_rows(xe, w_ref, tr)
        s = c * _sigmoid(c)
        if norm:
            s = s * lax.rsqrt(jnp.sum(s * s, axis=-1, keepdims=True) + EPS) * scale
        o_ref[...] = s

    return pl.pallas_call(
        body, name=name, grid=(nblk, S // tr),
        in_specs=[pl.BlockSpec((tr, HEAD), lambda j, i: (i, blk0 + j)),
                  pl.BlockSpec((8, HEAD), lambda j, i: (jnp.maximum(i * nb8 - 1, 0), blk0 + j)),
                  pl.BlockSpec((GDN_CONV, HEAD), lambda j, i: (0, j))],
        out_specs=pl.BlockSpec((tr, HEAD), lambda j, i: (i, j)),
        out_shape=jax.ShapeDtypeStruct((S, nblk * HEAD), F32),
        compiler_params=_cp(("parallel", "parallel")),
    )(proj, proj, convw)


def _gdn_prep_bwd(proj, blk0, nblk, convw, dout, norm, scale, name):
    S = proj.shape[0]
    tr = _tile(S, CONV_TILE, 8)
    nb8 = tr // 8
    last8 = S // 8 - 1
    nrow = S // tr

    def body(x_ref, xb_ref, xa_ref, w_ref, d_ref, da_ref, dx_ref, dw_ref):
        i = pl.program_id(1)

        @pl.when(i == 0)
        def _():
            dw_ref[...] = jnp.zeros_like(dw_ref)

        xe = jnp.concatenate([jnp.where(i == 0, 0.0, xb_ref[...]), x_ref[...], xa_ref[...]], axis=0)
        de = jnp.concatenate([d_ref[...], da_ref[...]], axis=0)
        n = tr + 8
        c = _conv_rows(xe, w_ref, n)
        sg = _sigmoid(c)
        s = c * sg
        if norm:
            r = lax.rsqrt(jnp.sum(s * s, axis=-1, keepdims=True) + EPS)
            ds = scale * r * (de - s * (r * r) * jnp.sum(de * s, axis=-1, keepdims=True))
        else:
            ds = de
        dc = ds * (sg + c * sg * (1.0 - sg))
        rowi = lax.broadcasted_iota(jnp.int32, (n, HEAD), 0)
        dc = jnp.where((i == nrow - 1) & (rowi >= tr), 0.0, dc)
        dct = dc[:tr, :]
        dx = None
        for j in range(GDN_CONV):
            tap = pl.ds(GDN_CONV - 1 - j, 1)
            up = dct if j == 0 else pltpu.roll(dc, n - j, 0)[:tr, :]
            term = w_ref[tap, :] * up
            dx = term if dx is None else dx + term
            dw_ref[tap, :] += jnp.sum(dct * _rows_down(xe, j)[8:8 + tr, :], axis=0, keepdims=True)
        dx_ref[...] = dx

    return pl.pallas_call(
        body, name=name, grid=(nblk, nrow),
        in_specs=[pl.BlockSpec((tr, HEAD), lambda j, i: (i, blk0 + j)),
                  pl.BlockSpec((8, HEAD), lambda j, i: (jnp.maximum(i * nb8 - 1, 0), blk0 + j)),
                  pl.BlockSpec((8, HEAD), lambda j, i: (jnp.minimum((i + 1) * nb8, last8), blk0 + j)),
                  pl.BlockSpec((GDN_CONV, HEAD), lambda j, i: (0, j)),
                  pl.BlockSpec((tr, HEAD), lambda j, i: (i, j)),
                  pl.BlockSpec((8, HEAD), lambda j, i: (jnp.minimum((i + 1) * nb8, last8), j))],
        out_specs=[pl.BlockSpec((tr, HEAD), lambda j, i: (i, j)),
                   pl.BlockSpec((GDN_CONV, HEAD), lambda j, i: (0, j))],
        out_shape=[jax.ShapeDtypeStruct((S, nblk * HEAD), F32),
                   jax.ShapeDtypeStruct((GDN_CONV, nblk * HEAD), F32)],
        compiler_params=_cp(("parallel", "arbitrary")),
    )(proj, proj, proj, convw, dout, dout)


def _gdn_gates(pg, alog, dtb):
    S = pg.shape[0]

    def body(a_ref, b_ref, al_ref, dt_ref, gc_ref, be_ref):
        g = -jnp.exp(al_ref[...]) * _softplus(a_ref[...] + dt_ref[...])
        rowm = lax.broadcasted_iota(jnp.int32, g.shape, 0) & (GDN_CHUNK - 1)
        c = g
        d = 1
        while d < GDN_CHUNK:
            c = c + jnp.where(rowm >= d, pltpu.roll(c, d, 0), 0.0)
            d *= 2
        gc_ref[...] = c
        be_ref[...] = _sigmoid(b_ref[...])

    blk = pl.BlockSpec((S, 128), lambda i: (0, 0))
    vec = pl.BlockSpec((1, 128), lambda i: (0, 0))
    return pl.pallas_call(
        body, name="gdn_gates", grid=(1,),
        in_specs=[blk, pl.BlockSpec((S, 128), lambda i: (0, 1)), vec, vec],
        out_specs=[blk, blk],
        out_shape=[jax.ShapeDtypeStruct((S, 128), F32)] * 2,
        compiler_params=_cp(("arbitrary",)),
    )(pg, pg, alog, dtb)


def _gdn_gates_bwd(pg, alog, dtb, dgc, dbeta):
    S = pg.shape[0]

    def body(a_ref, b_ref, al_ref, dt_ref, dgc_ref, dbe_ref, dpa_ref, dpb_ref, dal_ref, ddt_ref):
        rowm = lax.broadcasted_iota(jnp.int32, (S, 128), 0) & (GDN_CHUNK - 1)
        c = dgc_ref[...]
        d = 1
        while d < GDN_CHUNK:
            c = c + jnp.where(rowm < GDN_CHUNK - d, pltpu.roll(c, S - d, 0), 0.0)
            d *= 2
        xv = a_ref[...] + dt_ref[...]
        ea = jnp.exp(al_ref[...])
        g = -ea * _softplus(xv)
        dx = c * (-ea) * _sigmoid(xv)
        dpa_ref[...] = dx
        dal_ref[...] = jnp.sum(c * g, axis=0, keepdims=True)
        ddt_ref[...] = jnp.sum(dx, axis=0, keepdims=True)
        be = _sigmoid(b_ref[...])
        dpb_ref[...] = dbe_ref[...] * be * (1.0 - be)

    blk = pl.BlockSpec((S, 128), lambda i: (0, 0))
    blk1 = pl.BlockSpec((S, 128), lambda i: (0, 1))
    vec = pl.BlockSpec((1, 128), lambda i: (0, 0))
    dpa, dpb, dal, ddt = pl.pallas_call(
        body, name="gdn_gates_bwd", grid=(1,),
        in_specs=[blk, blk1, vec, vec, blk, blk],
        out_specs=[blk, blk, vec, vec],
        out_shape=[jax.ShapeDtypeStruct((S, 128), F32)] * 2 + [jax.ShapeDtypeStruct((1, 128), F32)] * 2,
        compiler_params=_cp(("arbitrary",)),
    )(pg, pg, alog, dtb, dgc, dbeta)
    return jnp.concatenate([dpa, dpb], axis=1), dal, ddt


def _gdn_pre(qs, ks, vs, gcs, grs, betas):
    C = GDN_CHUNK
    n = len(qs)
    r = lax.broadcasted_iota(jnp.int32, (C, C), 0)
    c_ = lax.broadcasted_iota(jnp.int32, (C, C), 1)
    lower, strict = r >= c_, r > c_
    eye = jnp.where(r == c_, 1.0, 0.0)
    decs = [jnp.exp(jnp.where(lower, gcs[i] - grs[i], -jnp.inf)) for i in range(n)]
    kbs = [ks[i] * betas[i] for i in range(n)]
    vbs = [vs[i] * betas[i] for i in range(n)]
    lmats = [jnp.where(strict, _dot(kbs[i], ks[i], NT) * decs[i], 0.0) for i in range(n)]
    amats = [jnp.where(lower, _dot(qs[i], ks[i], NT) * decs[i], 0.0) for i in range(n)]
    pks = [-lm for lm in lmats]
    tinvs = [eye + pk for pk in pks]
    for _ in range(5):
        pks = [_dotf(pk, pk, NN) for pk in pks]
        tinvs = [tv + _dotf(tv, pk, NN) for tv, pk in zip(tinvs, pks)]
    es = [jnp.exp(gc) for gc in gcs]
    glasts = [gc[C - 1:C, :] for gc in gcs]
    fs = [jnp.exp(gl - gc) for gl, gc in zip(glasts, gcs)]
    gls = [jnp.exp(gl) for gl in glasts]
    us = [_dotf(tinvs[i], vbs[i], NN) for i in range(n)]
    ws = [_dotf(tinvs[i], kbs[i] * es[i], NN) for i in range(n)]
    return [dict(lower=lower, strict=strict, dec=decs[i], kb=kbs[i], vb=vbs[i], lmat=lmats[i], tinv=tinvs[i],
                 e=es[i], f=fs[i], gl=gls[i], u=us[i], w=ws[i], amat=amats[i], qd=qs[i] * es[i],
                 kd=ks[i] * fs[i]) for i in range(n)]


def _gdn_heads_per_step(H):
    return max(d for d in (1, 2, 3, 4) if H % d == 0)


def _gdn_chunk_fwd(q, k, v, gcol, grow, bcol):
    S = q.shape[0]
    H, NC = gcol.shape[0], gcol.shape[1]
    C = GDN_CHUNK
    hb = _gdn_heads_per_step(H)

    def body(q_ref, k_ref, v_ref, gc_ref, gr_ref, b_ref, o_ref, st_ref, state):
        n = pl.program_id(1)

        @pl.when(n == 0)
        def _():
            state[...] = jnp.zeros_like(state)

        cols = [slice(i * HEAD, (i + 1) * HEAD) for i in range(hb)]
        ps = _gdn_pre([q_ref[:, c] for c in cols], [k_ref[:, c] for c in cols], [v_ref[:, c] for c in cols],
                      [gc_ref[i] for i in range(hb)], [gr_ref[i] for i in range(hb)],
                      [b_ref[i] for i in range(hb)])
        s0s = [state[i] for i in range(hb)]
        vns = [ps[i]['u'] - _dot(ps[i]['w'], s0s[i], NN) for i in range(hb)]
        outs = [_dot(ps[i]['qd'], s0s[i], NN) + _dot(ps[i]['amat'], vns[i], NN) for i in range(hb)]
        news = [s0s[i] * ps[i]['gl'] + _dot(ps[i]['kd'], vns[i], TN) for i in range(hb)]
        for i in range(hb):
            st_ref[i] = s0s[i]
            o_ref[:, cols[i]] = outs[i]
            state[i] = news[i]

    tok = pl.BlockSpec((C, hb * HEAD), lambda h, n: (n, h))
    col = pl.BlockSpec((hb, None, C, 1), lambda h, n: (h, n, 0, 0))
    rowb = pl.BlockSpec((hb, None, 1, C), lambda h, n: (h, n, 0, 0))
    return pl.pallas_call(
        body, name="gdn_chunk_fwd", grid=(H // hb, NC),
        in_specs=[tok, tok, tok, col, rowb, col],
        out_specs=[tok, pl.BlockSpec((hb, None, HEAD, HEAD), lambda h, n: (h, n, 0, 0))],
        out_shape=[jax.ShapeDtypeStruct((S, H * HEAD), F32), jax.ShapeDtypeStruct((H, NC, HEAD, HEAD), F32)],
        scratch_shapes=[pltpu.VMEM((hb, HEAD, HEAD), F32)],
        compiler_params=_cp(("parallel", "arbitrary")),
    )(q, k, v, gcol, grow, bcol)


def _gdn_chunk_bwd(q, k, v, gcol, grow, bcol, st, do):
    S = q.shape[0]
    H, NC = gcol.shape[0], gcol.shape[1]
    C = GDN_CHUNK
    hb = _gdn_heads_per_step(H)

    def body(q_ref, k_ref, v_ref, gc_ref, gr_ref, b_ref, st_ref, do_ref,
             dq_ref, dk_ref, dv_ref, dgc_ref, dbe_ref, dstate):
        n = pl.program_id(1)

        @pl.when(n == 0)
        def _():
            dstate[...] = jnp.zeros_like(dstate)

        R = range(hb)
        cols = [slice(i * HEAD, (i + 1) * HEAD) for i in R]
        qs, ks, vs = [q_ref[:, c] for c in cols], [k_ref[:, c] for c in cols], [v_ref[:, c] for c in cols]
        betas = [b_ref[i] for i in R]
        ps = _gdn_pre(qs, ks, vs, [gc_ref[i] for i in R], [gr_ref[i] for i in R], betas)
        lower, strict = ps[0]['lower'], ps[0]['strict']
        s0s, dos, ds1s = [st_ref[i] for i in R], [do_ref[:, c] for c in cols], [dstate[i] for i in R]
        vns = [ps[i]['u'] - _dot(ps[i]['w'], s0s[i], NN) for i in R]
        dvns = [_dot(ps[i]['amat'], dos[i], TN) + _dot(ps[i]['kd'], ds1s[i], NN) for i in R]
        damats = [jnp.where(lower, _dot(dos[i], vns[i], NT), 0.0) for i in R]
        dqds = [_dot(dos[i], s0s[i], NT) for i in R]
        dkds = [_dot(vns[i], ds1s[i], NT) for i in R]
        dgls = [jnp.sum(s0s[i] * ds1s[i], keepdims=True) for i in R]
        ds0s = [ps[i]['gl'] * ds1s[i] + _dot(ps[i]['qd'], dos[i], TN) - _dot(ps[i]['w'], dvns[i], TN) for i in R]
        dws = [-_dot(dvns[i], s0s[i], NT) for i in R]
        dvbs = [_dotf(ps[i]['tinv'], dvns[i], TN) for i in R]
        dkgs = [_dotf(ps[i]['tinv'], dws[i], TN) for i in R]
        dls = [-jnp.where(strict, _dotf(dvbs[i], ps[i]['u'], NT) + _dotf(dkgs[i], ps[i]['w'], NT), 0.0) for i in R]
        dkks = [dls[i] * ps[i]['dec'] for i in R]
        dqks = [damats[i] * ps[i]['dec'] for i in R]
        ms = [dls[i] * ps[i]['lmat'] + damats[i] * ps[i]['amat'] for i in R]
        dkbs = [_dot(dkks[i], ks[i], NN) + dkgs[i] * ps[i]['e'] for i in R]
        dks = [_dot(dkks[i], ps[i]['kb'], TN) + _dot(dqks[i], qs[i], TN) + dkds[i] * ps[i]['f'] + dkbs[i] * betas[i]
               for i in R]
        dqs = [_dot(dqks[i], ks[i], NN) + dqds[i] * ps[i]['e'] for i in R]
        ones = jnp.ones((C, HEAD), F32)
        colsums = [_dotf(ms[i], ones, TN)[:, 0:1] for i in R]
        rowi = lax.broadcasted_iota(jnp.int32, (C, 1), 0)
        for i in R:
            p = ps[i]
            de = (jnp.sum(dkgs[i] * p['kb'], axis=-1, keepdims=True)
                  + jnp.sum(dqds[i] * qs[i], axis=-1, keepdims=True))
            df = jnp.sum(dkds[i] * ks[i], axis=-1, keepdims=True)
            dgc = jnp.sum(ms[i], axis=-1, keepdims=True) - colsums[i] + de * p['e'] - df * p['f']
            dlast = jnp.sum(df * p['f'], keepdims=True) + dgls[i] * p['gl']
            dgc_ref[i] = dgc + jnp.where(rowi == C - 1, dlast, 0.0)
            dbe_ref[i] = (jnp.sum(dkbs[i] * ks[i], axis=-1, keepdims=True)
                          + jnp.sum(dvbs[i] * vs[i], axis=-1, keepdims=True))
            dstate[i] = ds0s[i]
            dq_ref[:, cols[i]] = dqs[i]
            dk_ref[:, cols[i]] = dks[i]
            dv_ref[:, cols[i]] = dvbs[i] * betas[i]

    def rn(n):
        return NC - 1 - n

    tok = pl.BlockSpec((C, hb * HEAD), lambda h, n: (rn(n), h))
    col = pl.BlockSpec((hb, None, C, 1), lambda h, n: (h, rn(n), 0, 0))
    rowb = pl.BlockSpec((hb, None, 1, C), lambda h, n: (h, rn(n), 0, 0))
    return pl.pallas_call(
        body, name="gdn_chunk_bwd", grid=(H // hb, NC),
        in_specs=[tok, tok, tok, col, rowb, col,
                  pl.BlockSpec((hb, None, HEAD, HEAD), lambda h, n: (h, rn(n), 0, 0)), tok],
        out_specs=[tok, tok, tok, col, col],
        out_shape=[jax.ShapeDtypeStruct((S, H * HEAD), F32)] * 3
        + [jax.ShapeDtypeStruct((H, NC, C, 1), F32)] * 2,
        scratch_shapes=[pltpu.VMEM((hb, HEAD, HEAD), F32)],
        compiler_params=_cp(("parallel", "arbitrary")),
    )(q, k, v, gcol, grow, bcol, st, do)


def _gdn_onorm(o, proj, gate_blk, w, H):
    S = o.shape[0]
    tr = _tile(S, CONV_TILE, 8)

    def body(o_ref, g_ref, w_ref, out_ref):
        ov, gv = o_ref[...], g_ref[...]
        r = lax.rsqrt(jnp.mean(ov * ov, axis=-1, keepdims=True) + EPS)
        out_ref[...] = (ov * r * w_ref[...]) * (gv * _sigmoid(gv))

    return pl.pallas_call(
        body, name="gdn_onorm", grid=(S // tr, H),
        in_specs=[pl.BlockSpec((tr, HEAD), lambda i, h: (i, h)),
                  pl.BlockSpec((tr, HEAD), lambda i, h: (i, gate_blk + h)),
                  pl.BlockSpec((1, HEAD), lambda i, h: (0, 0))],
        out_specs=pl.BlockSpec((tr, HEAD), lambda i, h: (i, h)),
        out_shape=jax.ShapeDtypeStruct((S, H * HEAD), F32),
        compiler_params=_cp(("parallel", "parallel")),
    )(o, proj, w)


def _gdn_onorm_bwd(dcat, o, proj, gate_blk, w, H):
    S = o.shape[0]
    tr = _tile(S, CONV_TILE, 8)

    def body(d_ref, o_ref, g_ref, w_ref, do_ref, dg_ref, dw_ref):
        i, h = pl.program_id(0), pl.program_id(1)

        @pl.when((i == 0) & (h == 0))
        def _():
            dw_ref[...] = jnp.zeros_like(dw_ref)

        dm, ov, gv, wv = d_ref[...], o_ref[...], g_ref[...], w_ref[...]
        r = lax.rsqrt(jnp.mean(ov * ov, axis=-1, keepdims=True) + EPS)
        oh = ov * r
        sg = gv * _sigmoid(gv)
        dy = dm * sg
        t = dy * wv
        do_ref[...] = r * (t - oh * jnp.mean(t * oh, axis=-1, keepdims=True))
        dg_ref[...] = dm * (oh * wv) * _silu_grad(gv)
        dw_ref[...] += jnp.sum(dy * oh, axis=0, keepdims=True)

    tok = pl.BlockSpec((tr, HEAD), lambda i, h: (i, h))
    vec = pl.BlockSpec((1, HEAD), lambda i, h: (0, 0))
    return pl.pallas_call(
        body, name="gdn_onorm_bwd", grid=(S // tr, H),
        in_specs=[tok, tok, pl.BlockSpec((tr, HEAD), lambda i, h: (i, gate_blk + h)), vec],
        out_specs=[tok, tok, vec],
        out_shape=[jax.ShapeDtypeStruct((S, H * HEAD), F32)] * 2 + [jax.ShapeDtypeStruct((1, HEAD), F32)],
        compiler_params=_cp(("arbitrary", "arbitrary")),
    )(dcat, o, proj, w)


def _lanes_to_heads(a, H):
    return a[:, :H].T


def _heads_to_lanes(a):
    H = a.shape[0]
    return jnp.pad(a.T, ((0, 0), (0, 128 - H)))


def _take_cols(segs, a, b):
    out, off = [], 0
    for sg in segs:
        w = sg.shape[-1]
        lo, hi = max(a, off), min(b, off + w)
        if lo < hi:
            out.append(sg[..., lo - off:hi - off])
        off += w
    return out


def _pad_cols(pieces):
    m = jnp.concatenate(pieces, axis=-1)
    return jnp.pad(m, ((0, 0), (0, 128 - m.shape[-1])))


def _pad_lanes(v):
    return jnp.pad(v.reshape(1, -1), ((0, 0), (0, 128 - v.shape[-1])))


def _s5_layer_fwd(a, w, cfg):
    proj = _mm(a, w['w_in'], name="s5_in")
    wb, wc, _ = w['prep']
    pr, pi, prr, pir = w['tables']
    v, yg, hs, cin = _s5_scan_fwd(proj, wb, wc, pr, pi, w['d_skip'])
    z, mix = _mm(yg, w['w_glu'], name="s5_glu", extras=[(yg, 'ij'), (w['b_glu'], 'j')],
                 epi=lambda acc, y, b: (acc + b, y * _sigmoid(acc + b)), out_dtypes=(F32, F32))
    return proj, mix, dict(v=v, yg=yg, hs=hs, cin=cin, z=z)


def _s5_layer_bwd(a, w, proj, sv, dcat, dmemq, cfg):
    wb, wc, _ = w['prep']
    pr, pi, prr, pir = w['tables']
    dz, dyg1, db_glu = _s5_glu_bwd(dcat, sv['yg'], sv['z'])
    dw_glu = _mm(sv['yg'], dz, name="s5_dwglu", ta=True)
    dv = _mm(dz, w['w_glu'], name="s5_dyg", tb=True, extras=[(dyg1, 'ij'), (sv['v'], 'ij')],
             epi=lambda acc, d1, vv: ((acc + d1) * _gelu_grad(vv),))
    du, dwb, dwc, da, dd = _s5_scan_bwd(dv, proj, sv['hs'], sv['cin'], wb, wc, pr, pi, prr, pir, w['d_skip'])
    dproj = jnp.concatenate([du, dmemq], axis=1).astype(MXU_DTYPE)
    dw_in = _mm(a, dproj, name="s5_dwin", ta=True)
    da_in = _mm(dproj, w['w_in'], name="s5_da", tb=True)
    dlre, dlim, dldt, dbre, dbim, dcre, dcim = w['prep_vjp']((dwb, dwc, da))
    grads = dict(w_in=dw_in, w_glu=dw_glu, b_glu=db_glu[0], d_skip=dd[0], lam_re=dlre, lam_im=dlim,
                 log_dt=dldt, b_re=dbre, b_im=dbim, c_re=dcre, c_im=dcim)
    return da_in, grads


def _gdn_relayout(a, H, NC):
    t = _lanes_to_heads(a, H).reshape(H, NC, GDN_CHUNK)
    return t[..., None], t[:, :, None, :]


def _gdn_layer_fwd(a, w, cfg):
    H, MIX, S = cfg['H'], cfg['MIX'], a.shape[0]
    NC = S // GDN_CHUNK
    proj = _mm(a, w['w_main'], name="gdn_in")
    pg = _mm(a, w['w_gate'], name="gdn_in_gates")
    cw = w['conv_w']
    q = _gdn_prep(proj, 0, H, cw[:, :MIX], True, HEAD ** -0.5, "gdn_prep_q")
    k = _gdn_prep(proj, H, H, cw[:, MIX:2 * MIX], True, 1.0, "gdn_prep_k")
    v = _gdn_prep(proj, 2 * H, H, cw[:, 2 * MIX:], False, 1.0, "gdn_prep_v")
    gc, beta = _gdn_gates(pg, w['a_log'], w['dt_bias'])
    gcol, grow = _gdn_relayout(gc, H, NC)
    bcol, _ = _gdn_relayout(beta, H, NC)
    o, st = _gdn_chunk_fwd(q, k, v, gcol, grow, bcol)
    mix = _gdn_onorm(o, proj, 3 * H, w['o_norm'], H)
    return proj, mix, dict(pg=pg, q=q, k=k, v=v, gcol=gcol, grow=grow, bcol=bcol, o=o, st=st)


def _gdn_layer_bwd(a, w, proj, sv, dcat, dmemq, cfg):
    H, MIX, S = cfg['H'], cfg['MIX'], a.shape[0]
    cw = w['conv_w']
    do, dgate, donorm = _gdn_onorm_bwd(dcat, sv['o'], proj, 3 * H, w['o_norm'], H)
    dq, dk, dv, dgcol, dbcol = _gdn_chunk_bwd(sv['q'], sv['k'], sv['v'], sv['gcol'], sv['grow'], sv['bcol'],
                                              sv['st'], do)
    dgc = _heads_to_lanes(dgcol.reshape(H, S))
    dbeta = _heads_to_lanes(dbcol.reshape(H, S))
    dpg, dalog, ddtb = _gdn_gates_bwd(sv['pg'], w['a_log'], w['dt_bias'], dgc, dbeta)
    dxq, dwq = _gdn_prep_bwd(proj, 0, H, cw[:, :MIX], dq, True, HEAD ** -0.5, "gdn_prep_bwd_q")
    dxk, dwk = _gdn_prep_bwd(proj, H, H, cw[:, MIX:2 * MIX], dk, True, 1.0, "gdn_prep_bwd_k")
    dxv, dwv = _gdn_prep_bwd(proj, 2 * H, H, cw[:, 2 * MIX:], dv, False, 1.0, "gdn_prep_bwd_v")
    dproj = jnp.concatenate([dxq, dxk, dxv, dgate, dmemq], axis=1).astype(MXU_DTYPE)
    dw_main = _mm(a, dproj, name="gdn_dwmain", ta=True)
    dw_gate = _mm(a, dpg, name="gdn_dwgate", ta=True)
    da1 = _mm(dpg, w['w_gate'], name="gdn_da_gates", tb=True)
    da_in = _mm(dproj, w['w_main'], name="gdn_da", tb=True, extras=[(da1, 'ij')], epi=lambda acc, e: (acc + e,))
    grads = dict(w_main=dw_main, w_gate=dw_gate, conv_w=jnp.concatenate([dwq, dwk, dwv], axis=1),
                 a_log=dalog[0, :H], dt_bias=ddtb[0, :H], o_norm=donorm[0])
    return da_in, grads


def _fox_layer_fwd(a, w, cfg):
    H = cfg['H']
    proj = _mm(a, w['w_main'], name="fox_in")
    pg = _mm(a, w['w_gate'], name="fox_in_gates")
    cf = _fox_gates(pg, w['b_f'])
    cfh = _lanes_to_heads(cf, H)
    cfq, cfk = cfh[:, :, None], cfh[:, None, :]
    o, lse = _fox_fwd(proj, cfq, cfk, H)
    return proj, o, dict(pg=pg, cfq=cfq, cfk=cfk, lse=lse)


def _fox_layer_bwd(a, w, proj, sv, dcat, dmemq, cfg):
    H = cfg['H']
    rowdot = _fox_bwd_rowdot(proj, sv['cfq'], sv['cfk'], sv['lse'], dcat, H)
    dq, dk, dv, dck = _fox_bwd(proj, sv['cfq'], sv['cfk'], rowdot, sv['lse'], dcat, H)
    dpg, dbf = _fox_gates_bwd(sv['pg'], w['b_f'], _heads_to_lanes(dck[:, 0, :]))
    dproj = jnp.concatenate([dq, dk, dv, dmemq], axis=1).astype(MXU_DTYPE)
    dw_main = _mm(a, dproj, name="fox_dwmain", ta=True)
    dw_gate = _mm(a, dpg, name="fox_dwgate", ta=True)
    da1 = _mm(dpg, w['w_gate'], name="fox_da_gates", tb=True)
    da_in = _mm(dproj, w['w_main'], name="fox_da", tb=True, extras=[(da1, 'ij')], epi=lambda acc, e: (acc + e,))
    grads = dict(w_main=dw_main, w_gate=dw_gate, b_f=dbf[0, :H])
    return da_in, grads


_LAYER_FWD = (_s5_layer_fwd, _gdn_layer_fwd, _fox_layer_fwd)
_LAYER_BWD = (_s5_layer_bwd, _gdn_layer_bwd, _fox_layer_bwd)


def _mixer_weights(kind, j, fw, p, cfg, after):
    H, MIX, MW = cfg['H'], cfg['MIX'], cfg['MW']
    if kind == 0:
        params = tuple(p[n][j] for n in ('s5_lam_re', 's5_lam_im', 's5_log_dt', 's5_b_re', 's5_b_im',
                                         's5_c_re', 's5_c_im'))
        prep, prep_vjp = jax.vjp(_s5_prep, *params)
        prep = (prep[0].astype(MXU_DTYPE), prep[1].astype(MXU_DTYPE), prep[2])
        tables = _s5_tables(*params[:3])
        return dict(w_in=fw.get('s5_w_in', j, after), w_glu=fw.get('s5_w_glu', j, after),
                    b_glu=fw.get('s5_b_glu', j, after), d_skip=fw.get('s5_d_skip', j, after).reshape(1, MIX),
                    prep=prep, prep_vjp=prep_vjp, tables=tables)
    if kind == 1:
        segs = fw.get('gdn_w_in', j, after)
        c0 = 4 * MIX
        total = c0 + 2 * H + MW
        w_main = jnp.concatenate(_take_cols(segs, 0, c0) + _take_cols(segs, c0 + 2 * H, total), axis=1)
        w_gate = jnp.concatenate([_pad_cols(_take_cols(segs, c0, c0 + H)),
                                  _pad_cols(_take_cols(segs, c0 + H, c0 + 2 * H))], axis=1)
        return dict(w_main=w_main, w_gate=w_gate, conv_w=fw.get('gdn_conv_w', j, after),
                    a_log=_pad_lanes(p['gdn_a_log'][j]), dt_bias=_pad_lanes(p['gdn_dt_bias'][j]),
                    o_norm=p['gdn_o_norm'][j].reshape(1, HEAD))
    segs = fw.get('fox_w_in', j, after)
    c0 = 3 * MIX
    total = c0 + H + MW
    w_main = jnp.concatenate(_take_cols(segs, 0, c0) + _take_cols(segs, c0 + H, total), axis=1)
    w_gate = _pad_cols(_take_cols(segs, c0, c0 + H))
    return dict(w_main=w_main, w_gate=w_gate, b_f=_pad_lanes(p['fox_b_f'][j]))


class _Weights:
    def __init__(self, resolve):
        self._resolve, self._have = resolve, {}

    def get(self, name, layer, after):
        if (name, layer) not in self._have:
            self._have[name, layer] = self._resolve(name, layer, after)
        return self._have[name, layer]


def _local_step(p, fw, cfg, on_grad=None):
    H, MIX, MW, MH, depth = cfg['H'], cfg['MIX'], cfg['MW'], cfg['MH'], cfg['depth']
    x, mem, target = p['x'], p['mem'], p['loss_target']
    q_blk = {0: MIX // HEAD, 1: 4 * MIX // HEAD, 2: 3 * MIX // HEAD}
    zero = jnp.zeros((), F32)
    tok = [zero]

    def told(name, layer, value):
        if on_grad is not None:
            tok[0] = tok[0] + on_grad(name, layer, value)
        return value

    mem_n = _rms_fwd(mem, p['mem_norm'], MXU_DTYPE, "mem_rms")
    w_kv = fw.get('w_mem_kv', 0, mem_n)
    mkv = _mm(mem_n, w_kv, name="mem_kv")

    h = x
    saved = []
    for i in range(depth):
        kind, j = i % 3, i // 3
        a = _rms_fwd(h, p['norm1'][i], MXU_DTYPE, "rms1")
        w = _mixer_weights(kind, j, fw, p, cfg, a)
        proj, mix, sv = _LAYER_FWD[kind](a, w, cfg)
        read = _mem_fwd(proj, q_blk[kind], mkv, MH)
        cat = jnp.concatenate([mix, read], axis=1).astype(MXU_DTYPE)
        w_out, w_up = fw.get('w_out', i, proj), fw.get('w_up', i, proj)
        h1 = _mm(cat, w_out, name="out_proj", extras=[(h, 'ij')], epi=lambda acc, r: (acc + r,))
        a2 = _rms_fwd(h1, p['norm2'][i], MXU_DTYPE, "rms2")
        act = _mm(a2, w_up, name="mlp_up", epi=lambda acc: (_relu2(acc),), out_dtypes=(MXU_DTYPE,))
        w_down = fw.get('w_down', i, h1)
        h2 = _mm(act, w_down, name="mlp_down", extras=[(h1, 'ij')], epi=lambda acc, r: (acc + r,))
        saved.append(dict(w=w, h=h, a=a, proj=proj, sv=sv, cat=cat, h1=h1, a2=a2, act=act,
                          w_out=w_out, w_up=w_up, w_down=w_down))
        h = h2

    loss, dh, dnorm_f, dh16 = _loss_head(h, p['norm_f'], target)

    g = {n: None for n in WEIGHTS}
    g['norm_f'] = dnorm_f[0]
    per_layer = {n: [None] * depth for n in ('norm1', 'norm2', 'w_out', 'w_up', 'w_down')}
    mix_grads = {0: {}, 1: {}, 2: {}}
    big = {0: (('s5_w_in', 'w_in'), ('s5_w_glu', 'w_glu')), 1: (), 2: ()}
    dmkv = None
    for i in reversed(range(depth)):
        kind, j = i % 3, i // 3
        s = saved[i]
        w = s['w']
        du = _mm(dh16, s['w_down'], name="mlp_ddown", tb=True, extras=[(s['act'], 'ij')],
                 epi=lambda acc, aa: (acc * (2.0 * jnp.sqrt(aa.astype(F32))),), out_dtypes=(MXU_DTYPE,))
        per_layer['w_down'][i] = told('w_down', i, _mm(s['act'], dh16, name="mlp_dwdown", ta=True))
        per_layer['w_up'][i] = told('w_up', i, _mm(s['a2'], du, name="mlp_dwup", ta=True))
        da2 = _mm(du, s['w_up'], name="mlp_dup", tb=True)
        dh1, dn2, dh1_16 = _rms_bwd(s['h1'], p['norm2'][i] + tok[0], da2, dh, "rms2_bwd")
        per_layer['norm2'][i] = dn2[0]
        dcat = _mm(dh1_16, s['w_out'], name="out_dproj", tb=True)
        per_layer['w_out'][i] = told('w_out', i, _mm(s['cat'], dh1_16, name="out_dw", ta=True))
        dmemq, dmkv_i = _mem_bwd(s['proj'], q_blk[kind], mkv, dcat, MIX // HEAD, MH)
        dmkv = dmkv_i if dmkv is None else dmkv + dmkv_i
        da, mg = _LAYER_BWD[kind](s['a'], w, s['proj'], s['sv'], dcat, dmemq, cfg)
        mix_grads[kind][j] = mg
        for name, key in big[kind]:
            told(name, j, mg[key])
        c0 = 4 * MIX
        if kind == 1:
            mg['segs'] = told('gdn_w_in', j, [mg['w_main'][:, :c0], mg['w_gate'][:, :H],
                                               mg['w_gate'][:, 128:128 + H], mg['w_main'][:, c0:]])
        c0 = 3 * MIX
        if kind == 2:
            mg['segs'] = told('fox_w_in', j, [mg['w_main'][:, :c0], mg['w_gate'][:, :H], mg['w_main'][:, c0:]])
        dh, dn1, dh16 = _rms_bwd(s['h'], p['norm1'][i] + tok[0], da, dh1, "rms1_bwd")
        per_layer['norm1'][i] = dn1[0]
    for n in ('norm1', 'norm2'):
        g[n] = jnp.stack(per_layer[n])
    for n in ('w_out', 'w_up', 'w_down'):
        g[n] = per_layer[n]

    g['w_mem_kv'] = told('w_mem_kv', 0, _mm(mem_n, dmkv, name="mem_dwkv", ta=True))
    dmem_n = _mm(dmkv, w_kv, name="mem_dn", tb=True)
    _, dmn, _ = _rms_bwd(mem, p['mem_norm'] + tok[0], dmem_n, None, "mem_rms_bwd")
    g['mem_norm'] = dmn[0]

    def layers(kind, key):
        return [mix_grads[kind][j][key] for j in sorted(mix_grads[kind])]

    g['s5_w_in'] = layers(0, 'w_in')
    g['s5_w_glu'] = layers(0, 'w_glu')
    for n in ('b_glu', 'd_skip', 'lam_re', 'lam_im', 'log_dt', 'b_re', 'b_im', 'c_re', 'c_im'):
        g['s5_' + n] = jnp.stack(layers(0, n))
    g['gdn_w_in'] = layers(1, 'segs')
    for n in ('conv_w', 'a_log', 'dt_bias', 'o_norm'):
        g['gdn_' + n] = jnp.stack(layers(1, n))
    g['fox_w_in'] = layers(2, 'segs')
    g['fox_b_f'] = jnp.stack(layers(2, 'b_f'))
    return loss, dh, g


def _ag_order(p):
    depth, order = p['norm1'].shape[0], [('w_mem_kv', 0)]
    for i in range(depth):
        kind, j = i % 3, i // 3
        order += [[('s5_w_in', j), ('s5_w_glu', j)], [('gdn_w_in', j)], [('fox_w_in', j)]][kind]
        order += [('w_out', i), ('w_up', i), ('w_down', i)]
    return order


def _gather_begin(p, me):
    handles, tok = {}, jnp.zeros((), F32)
    for name, layer in _ag_order(p):
        xs = (p[name] if p[name].ndim == 2 else p[name][layer]).astype(MXU_DTYPE)
        handle, t = _ag_start(xs, "ag_start_%s_%d" % (name, layer))
        handles[name, layer] = (handle, xs)
        tok = tok + t
    vec = _gather(_pack_small([p[n] for n in VECTOR_SHARDED], 16), me, "all_gather_vectors").reshape(4, -1)
    vectors, off = {}, 0
    for n in VECTOR_SHARDED:
        sz = p[n].size
        stacked = vec[:, off:off + sz].reshape((4,) + p[n].shape)
        ax = SHARD_AXIS[n]
        t = jnp.moveaxis(stacked, 0, ax)
        shp = list(t.shape)
        vectors[n] = t.reshape(shp[:ax] + [shp[ax] * shp[ax + 1]] + shp[ax + 2:])
        off += sz
    return handles, vectors, tok


def _gather_end(name, layer, after, me, handles, vectors):
    if name in vectors:
        return vectors[name][layer]
    handle, xs = handles[name, layer]
    tag = "%s_%d" % (name, layer)
    got = _ag_wait(handle, after, "ag_wait_" + tag)
    got = _ag_forward(lax.dynamic_update_slice(got, xs[None], (me, 0, 0)), "ag_forward_" + tag)
    r, n = xs.shape
    if name == 'w_mem_kv' or SHARD_AXIS[name] == 1:
        return got.reshape(4 * r, n)
    blocks = [got[s] for s in range(4)]
    return jnp.concatenate(blocks, axis=1) if name == 'w_up' else blocks


def _shard_blocks(name, value, p):
    shp = p[name].shape
    r, n = shp[-2], shp[-1]
    if SHARD_AXIS[name] == len(shp) - 2:
        return value.reshape(4, r, n)
    segs = value if isinstance(value, list) else [value]
    return jnp.stack([jnp.concatenate(_take_cols(segs, s * n, (s + 1) * n), axis=1) for s in range(4)])


def kernel(x, mem, mem_norm, w_mem_kv, norm1, w_out, norm2, w_up, w_down, norm_f, s5_w_in, s5_lam_re, s5_lam_im, s5_log_dt, s5_b_re, s5_b_im, s5_c_re, s5_c_im, s5_d_skip, s5_w_glu, s5_b_glu, gdn_w_in, gdn_conv_w, gdn_a_log, gdn_dt_bias, gdn_o_norm, fox_w_in, fox_b_f, loss_target, m_mem_norm, m_w_mem_kv, m_norm1, m_w_out, m_norm2, m_w_up, m_w_down, m_norm_f, m_s5_w_in, m_s5_lam_re, m_s5_lam_im, m_s5_log_dt, m_s5_b_re, m_s5_b_im, m_s5_c_re, m_s5_c_im, m_s5_d_skip, m_s5_w_glu, m_s5_b_glu, m_gdn_w_in, m_gdn_conv_w, m_gdn_a_log, m_gdn_dt_bias, m_gdn_o_norm, m_fox_w_in, m_fox_b_f, v_mem_norm, v_w_mem_kv, v_norm1, v_w_out, v_norm2, v_w_up, v_w_down, v_norm_f, v_s5_w_in, v_s5_lam_re, v_s5_lam_im, v_s5_log_dt, v_s5_b_re, v_s5_b_im, v_s5_c_re, v_s5_c_im, v_s5_d_skip, v_s5_w_glu, v_s5_b_glu, v_gdn_w_in, v_gdn_conv_w, v_gdn_a_log, v_gdn_dt_bias, v_gdn_o_norm, v_fox_w_in, v_fox_b_f):
    args = locals()
    p = {n: args[n] for n in WEIGHTS}
    mom = {n: args['m_' + n] for n in WEIGHTS}
    var = {n: args['v_' + n] for n in WEIGHTS}
    S, D = x.shape[1], x.shape[2]
    MW = w_mem_kv.shape[1] // 2
    MIX = D - MW
    cfg = dict(H=MIX // HEAD, MIX=MIX, MW=MW, MH=MW // HEAD, depth=norm1.shape[0])
    p.update(x=x.reshape(S, D), mem=mem.reshape(mem.shape[1], D), loss_target=loss_target.reshape(S, D))
    c = lax.axis_index("c")
    me = 2 * lax.axis_index("x") + lax.axis_index("y")
    place = dict(c=c, c_idx=c.astype(jnp.int32).reshape(1), me_idx=me.astype(jnp.int32).reshape(1))

    handles, vectors, tok = _gather_begin(p, me)
    p['mem_norm'] = mem_norm + tok
    fw = _Weights(lambda name, layer, after: _gather_end(name, layer, after, me, handles, vectors))

    pending, swapping = {}, []

    def exchange(after):
        name, layer, swap = swapping.pop()
        pending[name, layer], t = _rs_begin(swap, after, place, "%s_%d" % (name, layer))
        return t

    def on_grad(name, layer, value):
        swap, t = _rs_swap_start(_shard_blocks(name, value, p), "rs_swap_start_%s_%d" % (name, layer))
        if swapping:
            t = t + exchange(value[0] if isinstance(value, list) else value)
        swapping.append((name, layer, swap))
        return t

    loss, dx, g = _local_step(p, fw, cfg, on_grad)
    p['mem_norm'] = mem_norm
    exchange(dx)

    grads = {}
    for name in MATMUL_WEIGHTS:
        shp = p[name].shape
        layers = [_rs_end(pending[name, i], dx, place, "%s_%d" % (name, i))
                  for i in range(1 if len(shp) == 2 else shp[0])]
        grads[name] = layers[0] if len(shp) == 2 else jnp.stack(layers)

    parts = []
    for name in VECTOR_SHARDED:
        ax = SHARD_AXIS[name]
        shp = list(g[name].shape)
        t = g[name].reshape(shp[:ax] + [4, shp[ax] // 4] + shp[ax + 1:])
        parts.append(jnp.moveaxis(t, ax, 0).reshape(4, -1))
    flat = jnp.concatenate(parts, axis=1)
    flat = jnp.pad(flat, ((0, 0), (0, 16 * LANES - flat.shape[1]))).reshape(4, 16, LANES)
    swap, _ = _rs_swap_start(flat, "rs_swap_start_vectors")
    handle, _ = _rs_begin(swap, dx, place, "vectors")
    red = _rs_end(handle, dx, place, "vectors").reshape(-1)
    off = 0
    for name in VECTOR_SHARDED:
        grads[name] = red[off:off + p[name].size].reshape(p[name].shape)
        off += p[name].size

    n_small = sum(p[n].size for n in REPLICATED)
    rows = -(-n_small // LANES // 8) * 8
    small = _all_reduce_small(_pack_small([g[n] for n in REPLICATED], rows), "all_reduce_small").reshape(-1)
    off = 0
    for n in REPLICATED:
        grads[n] = small[off:off + p[n].size].reshape(p[n].shape)
        off += p[n].size

    delta, new_m, new_v = {}, {}, {}
    for n in SHARD_AXIS:
        shp = p[n].shape
        two_d = (-1, shp[-1])
        d, nm, nv = _adamw(p[n].reshape(two_d), grads[n].reshape(two_d), mom[n].reshape(two_d),
                           var[n].reshape(two_d), "adamw_" + n)
        delta[n], new_m[n], new_v[n] = d.reshape(shp), nm.reshape(shp), nv.reshape(shp)
    d, nm, nv = _adamw(*[_pack_small([src[n] for n in REPLICATED], rows) for src in (p, grads, mom, var)],
                       "adamw_small")
    d, nm, nv = d.reshape(-1), nm.reshape(-1), nv.reshape(-1)
    off = 0
    for n in REPLICATED:
        sz, shp = p[n].size, p[n].shape
        delta[n], new_m[n], new_v[n] = (d[off:off + sz].reshape(shp), nm[off:off + sz].reshape(shp),
                                        nv[off:off + sz].reshape(shp))
        off += sz

    total = lax.psum(loss[0, 0], ("x", "y", "c"))
    return (total, dx.reshape(x.shape), *[grads[n] for n in WEIGHTS], *[delta[n] for n in WEIGHTS],
            *[new_m[n] for n in WEIGHTS], *[new_v[n] for n in WEIGHTS])
```

```python
import math

import jax
import jax.numpy as jnp
import numpy as np
from jax import lax
from jax.experimental import pallas as pl
from jax.experimental.pallas import tpu as pltpu

F32 = jnp.float32
MXU_DTYPE = jnp.bfloat16
EPS = 1e-6
HEAD = 128
S5_GROUP = 16
S5_STATE = 64
S5_SLAB = 256
S5_CHUNK = 128
S5_ROWS = 8
GDN_CHUNK = 64
GDN_CONV = 4
LANES = 1024
VMEM_LIMIT_BYTES = 56 * 1024 * 1024
MESH = pl.DeviceIdType.MESH
RS_PAYLOAD = jnp.bfloat16
HBM_SPEC = pl.BlockSpec(memory_space=pltpu.HBM)
SEM_SPEC = pl.BlockSpec(memory_space=pltpu.SEMAPHORE)
SPLIT_EFFECT = pltpu.SideEffectType.DATAFLOW_SIDE_EFFECTING

ADAM_LR, ADAM_B1, ADAM_B2, ADAM_EPS, ADAM_WD, ADAM_STEP = 0.001, 0.9, 0.999, 1e-08, 0.01, 10

MM_TM, MM_TN, MM_TK = 1024, 1024, 1024
ROW_TILE = 256
FOX_TILE = 512
MEM_TILE = 512
CONV_TILE = 1024

NN = (((1,), (0,)), ((), ()))
NT = (((1,), (1,)), ((), ()))
TN = (((0,), (0,)), ((), ()))

WEIGHTS = ['mem_norm', 'w_mem_kv', 'norm1', 'w_out', 'norm2', 'w_up', 'w_down', 'norm_f', 's5_w_in',
           's5_lam_re', 's5_lam_im', 's5_log_dt', 's5_b_re', 's5_b_im', 's5_c_re', 's5_c_im', 's5_d_skip',
           's5_w_glu', 's5_b_glu', 'gdn_w_in', 'gdn_conv_w', 'gdn_a_log', 'gdn_dt_bias', 'gdn_o_norm',
           'fox_w_in', 'fox_b_f']
SHARD_AXIS = {'w_mem_kv': 0, 'w_out': 1, 'w_up': 2, 'w_down': 1, 's5_w_in': 1, 's5_d_skip': 1,
              's5_w_glu': 1, 's5_b_glu': 1, 'gdn_w_in': 2, 'gdn_conv_w': 2, 'fox_w_in': 2}
MATMUL_WEIGHTS = ['w_mem_kv', 'w_out', 'w_up', 'w_down', 's5_w_in', 's5_w_glu', 'gdn_w_in', 'fox_w_in']
VECTOR_SHARDED = ['s5_d_skip', 's5_b_glu', 'gdn_conv_w']
REPLICATED = [n for n in WEIGHTS if n not in SHARD_AXIS]


def _tile(dim, target, align=128):
    if dim <= target:
        return dim
    t = (target // align) * align
    while t >= align:
        if dim % t == 0:
            return t
        t -= align
    return dim


def _cp(sem=None, **kw):
    return pltpu.CompilerParams(dimension_semantics=sem, vmem_limit_bytes=VMEM_LIMIT_BYTES, **kw)


def _dot(a, b, dims):
    return lax.dot_general(a.astype(MXU_DTYPE), b.astype(MXU_DTYPE), dims, preferred_element_type=F32)


def _dotf(a, b, dims):
    return lax.dot_general(a, b, dims, precision=lax.Precision.HIGHEST, preferred_element_type=F32)


def _sigmoid(x):
    return 1.0 / (1.0 + jnp.exp(-x))


def _softplus(x):
    return jnp.maximum(x, 0.0) + jnp.log(1.0 + jnp.exp(-jnp.abs(x)))


def _relu2(x):
    r = jnp.maximum(x, 0.0)
    return r * r


_GELU_C = math.sqrt(2.0 / math.pi)


def _gelu(x):
    return 0.5 * x * (1.0 + jnp.tanh(_GELU_C * (x + 0.044715 * x * x * x)))


def _gelu_grad(x):
    t = jnp.tanh(_GELU_C * (x + 0.044715 * x * x * x))
    return 0.5 * (1.0 + t) + 0.5 * x * (1.0 - t * t) * _GELU_C * (1.0 + 3.0 * 0.044715 * x * x)


def _silu_grad(x):
    s = _sigmoid(x)
    return s + x * s * (1.0 - s)


def _mm(a, b, *, name, ta=False, tb=False, extras=(), epi=None, out_dtypes=(F32,)):
    K, M = a.shape if ta else a.shape[::-1]
    N = b.shape[0] if tb else b.shape[1]
    assert (b.shape[1] if tb else b.shape[0]) == K, (a.shape, b.shape, ta, tb)
    tm, tn, tk = _tile(M, MM_TM), _tile(N, MM_TN), _tile(K, MM_TK)
    nk = K // tk
    n_ex, n_out = len(extras), len(out_dtypes)
    dims = TN if ta else (NT if tb else NN)

    def body(*refs):
        a_ref, b_ref = refs[0], refs[1]
        ex = refs[2:2 + n_ex]
        outs = refs[2 + n_ex:2 + n_ex + n_out]
        acc = refs[-1]
        k = pl.program_id(2)

        @pl.when(k == 0)
        def _():
            acc[...] = jnp.zeros_like(acc)

        acc[...] += _dot(a_ref[...], b_ref[...], dims)

        @pl.when(k == nk - 1)
        def _():
            res = acc[...]
            vals = epi(res, *[e[...] for e in ex]) if epi is not None else (res,)
            for o, v in zip(outs, vals):
                o[...] = v.astype(o.dtype)

    if ta:
        a_spec = pl.BlockSpec((tk, tm), lambda i, j, k: (k, i))
    else:
        a_spec = pl.BlockSpec((tm, tk), lambda i, j, k: (i, k))
    if tb:
        b_spec = pl.BlockSpec((tn, tk), lambda i, j, k: (j, k))
    else:
        b_spec = pl.BlockSpec((tk, tn), lambda i, j, k: (k, j))
    ex_specs, ex_arrays = [], []
    for arr, kind in extras:
        if kind == 'ij':
            ex_specs.append(pl.BlockSpec((tm, tn), lambda i, j, k: (i, j)))
            ex_arrays.append(arr)
        else:
            ex_specs.append(pl.BlockSpec((1, tn), lambda i, j, k: (0, j)))
            ex_arrays.append(arr.reshape(1, N))
    outs = pl.pallas_call(
        body, name=name, grid=(M // tm, N // tn, nk),
        in_specs=[a_spec, b_spec] + ex_specs,
        out_specs=[pl.BlockSpec((tm, tn), lambda i, j, k: (i, j)) for _ in out_dtypes],
        out_shape=[jax.ShapeDtypeStruct((M, N), dt) for dt in out_dtypes],
        scratch_shapes=[pltpu.VMEM((tm, tn), F32)],
        compiler_params=_cp(("parallel", "parallel", "arbitrary")),
    )(a, b, *ex_arrays)
    return outs[0] if n_out == 1 else tuple(outs)


def _rms_fwd(x, g, out_dtype, name):
    S, D = x.shape
    tr = _tile(S, ROW_TILE, 8)

    def body(x_ref, g_ref, o_ref):
        xv = x_ref[...]
        r = lax.rsqrt(jnp.mean(xv * xv, axis=-1, keepdims=True) + EPS)
        o_ref[...] = (xv * r * g_ref[...]).astype(o_ref.dtype)

    return pl.pallas_call(
        body, name=name, grid=(S // tr,),
        in_specs=[pl.BlockSpec((tr, D), lambda i: (i, 0)), pl.BlockSpec((1, D), lambda i: (0, 0))],
        out_specs=pl.BlockSpec((tr, D), lambda i: (i, 0)),
        out_shape=jax.ShapeDtypeStruct((S, D), out_dtype),
        compiler_params=_cp(("parallel",)),
    )(x, g.reshape(1, D))


def _rms_bwd(x, g, dy, res, name):
    S, D = x.shape
    tr = _tile(S, ROW_TILE, 8)
    has_res = res is not None

    def body(*refs):
        if has_res:
            x_ref, g_ref, dy_ref, res_ref, dx_ref, dg_ref, dx16_ref = refs
        else:
            x_ref, g_ref, dy_ref, dx_ref, dg_ref, dx16_ref = refs
        i = pl.program_id(0)

        @pl.when(i == 0)
        def _():
            dg_ref[...] = jnp.zeros_like(dg_ref)

        xv, d = x_ref[...], dy_ref[...].astype(F32)
        r = lax.rsqrt(jnp.mean(xv * xv, axis=-1, keepdims=True) + EPS)
        xh = xv * r
        t = d * g_ref[...]
        dx = r * (t - xh * jnp.mean(t * xh, axis=-1, keepdims=True))
        if has_res:
            dx = dx + res_ref[...]
        dx_ref[...] = dx
        dx16_ref[...] = dx.astype(dx16_ref.dtype)
        dg_ref[...] += jnp.sum(d * xh, axis=0, keepdims=True)

    row = pl.BlockSpec((tr, D), lambda i: (i, 0))
    vec = pl.BlockSpec((1, D), lambda i: (0, 0))
    ins = [x, g.reshape(1, D), dy] + ([res] if has_res else [])
    return pl.pallas_call(
        body, name=name, grid=(S // tr,),
        in_specs=[row, vec, row] + ([row] if has_res else []),
        out_specs=[row, vec, row],
        out_shape=[jax.ShapeDtypeStruct((S, D), F32), jax.ShapeDtypeStruct((1, D), F32),
                   jax.ShapeDtypeStruct((S, D), MXU_DTYPE)],
        compiler_params=_cp(("arbitrary",)),
    )(*ins)


def _loss_head(h, g, target):
    S, D = h.shape
    tr = _tile(S, ROW_TILE, 8)

    def body(h_ref, g_ref, t_ref, loss_ref, dh_ref, dg_ref, dh16_ref):
        i = pl.program_id(0)

        @pl.when(i == 0)
        def _():
            loss_ref[...] = jnp.zeros_like(loss_ref)
            dg_ref[...] = jnp.zeros_like(dg_ref)

        xv = h_ref[...]
        gv = g_ref[...]
        r = lax.rsqrt(jnp.mean(xv * xv, axis=-1, keepdims=True) + EPS)
        xh = xv * r
        err = xh * gv - t_ref[...]
        part = 0.5 * jnp.sum(jnp.mean(err * err, axis=-1, keepdims=True), axis=0, keepdims=True)
        loss_ref[...] += jnp.broadcast_to(part, loss_ref.shape)
        d = err * (1.0 / D)
        t = d * gv
        dh = r * (t - xh * jnp.mean(t * xh, axis=-1, keepdims=True))
        dh_ref[...] = dh
        dh16_ref[...] = dh.astype(dh16_ref.dtype)
        dg_ref[...] += jnp.sum(d * xh, axis=0, keepdims=True)

    row = pl.BlockSpec((tr, D), lambda i: (i, 0))
    vec = pl.BlockSpec((1, D), lambda i: (0, 0))
    return pl.pallas_call(
        body, name="loss_head", grid=(S // tr,),
        in_specs=[row, vec, row],
        out_specs=[pl.BlockSpec((8, 128), lambda i: (0, 0)), row, vec, row],
        out_shape=[jax.ShapeDtypeStruct((8, 128), F32), jax.ShapeDtypeStruct((S, D), F32),
                   jax.ShapeDtypeStruct((1, D), F32), jax.ShapeDtypeStruct((S, D), MXU_DTYPE)],
        compiler_params=_cp(("arbitrary",)),
    )(h, g.reshape(1, D), target)


def _adamw(w, g, m, v, name):
    R, C = w.shape
    tr = _tile(R, max(8, (1 << 19) // max(C, 1) // 8 * 8), 8)
    c1 = 1.0 / (1.0 - ADAM_B1 ** ADAM_STEP)
    c2 = 1.0 / (1.0 - ADAM_B2 ** ADAM_STEP)

    def body(w_ref, g_ref, m_ref, v_ref, d_ref, nm_ref, nv_ref):
        gv = g_ref[...]
        nm = ADAM_B1 * m_ref[...] + (1.0 - ADAM_B1) * gv
        nv = ADAM_B2 * v_ref[...] + (1.0 - ADAM_B2) * (gv * gv)
        d_ref[...] = -ADAM_LR * ((nm * c1) / (jnp.sqrt(nv * c2) + ADAM_EPS) + ADAM_WD * w_ref[...])
        nm_ref[...] = nm
        nv_ref[...] = nv

    blk = pl.BlockSpec((tr, C), lambda i: (i, 0))
    return pl.pallas_call(
        body, name=name, grid=(R // tr,),
        in_specs=[blk] * 4, out_specs=[blk] * 3,
        out_shape=[jax.ShapeDtypeStruct((R, C), F32)] * 3,
        compiler_params=_cp(("parallel",)),
    )(w, g, m, v)


def _place():
    x, y, c = lax.axis_index("x"), lax.axis_index("y"), lax.axis_index("c")
    chips = [(1 - x, y), (x, 1 - y), (1 - x, 1 - y)]
    return x, y, c, chips


def _in_hbm(a):
    return pltpu.with_memory_space_constraint(a, pltpu.HBM)


def _ag_start(xs, name):
    r, n = xs.shape
    half = r // 2

    def body(x_ref, land_ref, send_sems, recv_sems, x_thru, land_thru, token):
        x, y, c, chips = _place()
        rows = pl.ds(c * half, half)
        for j, (cx, cy) in enumerate(chips):
            pltpu.make_async_remote_copy(
                src_ref=x_ref.at[rows, :], dst_ref=land_ref.at[2 * x + y, rows, :], send_sem=send_sems.at[j],
                recv_sem=recv_sems.at[j], device_id=(cx, cy, c), device_id_type=MESH).start()
        token[...] = jnp.zeros_like(token)

    sems = pltpu.SemaphoreType.DMA((3,))
    out = pl.pallas_call(
        body, name=name,
        out_shape=(sems, sems, pltpu.HBM(xs.shape, xs.dtype), pltpu.HBM((4, r, n), xs.dtype),
                   jax.ShapeDtypeStruct((8, 128), F32)),
        in_specs=(HBM_SPEC, HBM_SPEC),
        out_specs=(SEM_SPEC, SEM_SPEC, HBM_SPEC, HBM_SPEC, pl.BlockSpec(memory_space=pltpu.VMEM)),
        input_output_aliases={0: 2, 1: 3},
        compiler_params=pltpu.CompilerParams(has_side_effects=SPLIT_EFFECT),
    )(_in_hbm(xs), _in_hbm(lax.empty((4, r, n), xs.dtype)))
    return out[:4], out[4][0, 0]


def _ag_wait(handle, after, name):
    send_sems, recv_sems, xs, land = handle
    r, n = xs.shape
    half = r // 2

    def body(x_ref, land_ref, send_sems, recv_sems, after_ref, x_out, land_out):
        x, y, c, chips = _place()
        rows = pl.ds(c * half, half)
        for j, (cx, cy) in enumerate(chips):
            cp = pltpu.make_async_remote_copy(
                src_ref=x_ref.at[rows, :], dst_ref=land_ref.at[2 * cx + cy, rows, :], send_sem=send_sems.at[j],
                recv_sem=recv_sems.at[j], device_id=(cx, cy, c), device_id_type=MESH)
            cp.wait_send()
            cp.wait_recv()

    return pl.pallas_call(
        body, name=name,
        out_shape=(pltpu.HBM(xs.shape, xs.dtype), pltpu.HBM(land.shape, land.dtype)),
        in_specs=(HBM_SPEC, HBM_SPEC, SEM_SPEC, SEM_SPEC, pl.BlockSpec(memory_space=pl.ANY)),
        out_specs=(HBM_SPEC, HBM_SPEC),
        input_output_aliases={0: 0, 1: 1},
        compiler_params=pltpu.CompilerParams(has_side_effects=SPLIT_EFFECT),
    )(xs, land, send_sems, recv_sems, after)[1]


def _ag_forward_start(got, name):
    _, r, n = got.shape
    half = r // 2

    def body(g_ref, send_sems, recv_sems, g_thru):
        x, y, c, chips = _place()
        for j, (cx, cy) in enumerate(chips):
            piece = g_ref.at[2 * cx + cy, pl.ds(c * half, half), :]
            pltpu.make_async_remote_copy(src_ref=piece, dst_ref=piece, send_sem=send_sems.at[j],
                                         recv_sem=recv_sems.at[j], device_id=(x, y, 1 - c),
                                         device_id_type=MESH).start()

    sems = pltpu.SemaphoreType.DMA((3,))
    return pl.pallas_call(
        body, name=name,
        out_shape=(sems, sems, pltpu.HBM(got.shape, got.dtype)),
        in_specs=(HBM_SPEC,),
        out_specs=(SEM_SPEC, SEM_SPEC, HBM_SPEC),
        input_output_aliases={0: 2},
        compiler_params=pltpu.CompilerParams(has_side_effects=SPLIT_EFFECT),
    )(got)


def _ag_forward_wait(handle, name):
    send_sems, recv_sems, got = handle
    _, r, n = got.shape
    half = r // 2

    def body(g_ref, send_sems, recv_sems, g_out):
        x, y, c, chips = _place()
        for j, (cx, cy) in enumerate(chips):
            sent = g_ref.at[2 * cx + cy, pl.ds(c * half, half), :]
            pltpu.make_async_remote_copy(src_ref=sent, dst_ref=sent, send_sem=send_sems.at[j],
                                         recv_sem=recv_sems.at[j], device_id=(x, y, 1 - c),
                                         device_id_type=MESH).wait_send()
            came = g_ref.at[2 * cx + cy, pl.ds((1 - c) * half, half), :]
            pltpu.make_async_remote_copy(src_ref=came, dst_ref=came, send_sem=send_sems.at[j],
                                         recv_sem=recv_sems.at[j], device_id=(x, y, 1 - c),
                                         device_id_type=MESH).wait_recv()

    return pl.pallas_call(
        body, name=name,
        out_shape=pltpu.HBM(got.shape, got.dtype),
        in_specs=(HBM_SPEC, SEM_SPEC, SEM_SPEC),
        out_specs=HBM_SPEC,
        input_output_aliases={0: 0},
        compiler_params=pltpu.CompilerParams(has_side_effects=SPLIT_EFFECT),
    )(got, send_sems, recv_sems)


def _ag_fill_own(got, xs, me_idx, name):
    r, n = xs.shape
    tr = _tile(r, max(16, (1 << 20) // n // 16 * 16), 16)

    def body(me_ref, x_ref, g_ref, out_ref):
        out_ref[...] = x_ref[...]

    return pl.pallas_call(
        body, name=name,
        grid_spec=pltpu.PrefetchScalarGridSpec(
            num_scalar_prefetch=1, grid=(r // tr,),
            in_specs=[pl.BlockSpec((tr, n), lambda i, me: (i, 0)), pl.BlockSpec(memory_space=pl.ANY)],
            out_specs=pl.BlockSpec((None, tr, n), lambda i, me: (me[0], i, 0))),
        out_shape=jax.ShapeDtypeStruct(got.shape, got.dtype),
        input_output_aliases={2: 0},
        compiler_params=_cp(("parallel",)),
    )(me_idx, xs, got)


def _all_gather_chips(xs, name):
    r, n = xs.shape
    half = r // 2

    def body(x_ref, out_ref, send_sems, recv_sems):
        x, y, c, chips = _place()
        me = 2 * x + y
        sibling = (x, y, 1 - c)

        def piece(chip, hc):
            return out_ref.at[chip, pl.ds(hc * half, half), :]

        def copy(k, src, dst, to):
            return pltpu.make_async_remote_copy(src_ref=src, dst_ref=dst, send_sem=send_sems.at[k],
                                                recv_sem=recv_sems.at[k], device_id=to, device_id_type=MESH)

        src = x_ref.at[pl.ds(c * half, half), :]
        first = [copy(j, src, piece(me, c), (cx, cy, c)) for j, (cx, cy) in enumerate(chips)]
        for cp in first:
            cp.start()
        passed = []
        for j, (cx, cy) in enumerate(chips):
            got = piece(2 * cx + cy, c)
            copy(j, got, got, (cx, cy, c)).wait_recv()
            fwd = copy(3 + j, got, got, sibling)
            fwd.start()
            passed.append(fwd)
        for j, (cx, cy) in enumerate(chips):
            got = piece(2 * cx + cy, 1 - c)
            copy(3 + j, got, got, sibling).wait_recv()
        for cp in first + passed:
            cp.wait_send()

    return pl.pallas_call(
        body, name=name,
        in_specs=[pl.BlockSpec(memory_space=pl.ANY)],
        out_specs=pl.BlockSpec(memory_space=pl.ANY),
        out_shape=jax.ShapeDtypeStruct((4, r, n), xs.dtype),
        scratch_shapes=[pltpu.SemaphoreType.DMA((6,)), pltpu.SemaphoreType.DMA((6,))],
    )(xs)


def _gather(xs, me, name):
    return lax.dynamic_update_slice(_all_gather_chips(xs, name), xs[None], (me, 0, 0))


def _rs_swap_start(g, name):
    _, r, n = g.shape
    half = r // 2

    def body(g_ref, land_ref, send_sem, recv_sem, g_thru, land_thru, token):
        x, y, c, _ = _place()
        pltpu.make_async_remote_copy(
            src_ref=g_ref.at[:, pl.ds((1 - c) * half, half), :], dst_ref=land_ref,
            send_sem=send_sem, recv_sem=recv_sem, device_id=(x, y, 1 - c), device_id_type=MESH).start()
        token[...] = jnp.zeros_like(token)

    sem = pltpu.SemaphoreType.DMA(())
    out = pl.pallas_call(
        body, name=name,
        out_shape=(sem, sem, pltpu.HBM(g.shape, g.dtype), pltpu.HBM((4, half, n), g.dtype),
                   jax.ShapeDtypeStruct((8, 128), F32)),
        in_specs=(HBM_SPEC, HBM_SPEC),
        out_specs=(SEM_SPEC, SEM_SPEC, HBM_SPEC, HBM_SPEC, pl.BlockSpec(memory_space=pltpu.VMEM)),
        input_output_aliases={0: 2, 1: 3},
        compiler_params=pltpu.CompilerParams(has_side_effects=SPLIT_EFFECT),
    )(_in_hbm(g), _in_hbm(lax.empty((4, half, n), g.dtype)))
    return out[:4], out[4][0, 0]


def _rs_swap_wait(handle, after, name):
    send_sem, recv_sem, g, land = handle
    half = land.shape[1]

    def body(g_ref, land_ref, send_sem, recv_sem, after_ref, g_out, land_out):
        x, y, c, _ = _place()
        cp = pltpu.make_async_remote_copy(
            src_ref=g_ref.at[:, pl.ds((1 - c) * half, half), :], dst_ref=land_ref,
            send_sem=send_sem, recv_sem=recv_sem, device_id=(x, y, 1 - c), device_id_type=MESH)
        cp.wait_send()
        cp.wait_recv()

    return pl.pallas_call(
        body, name=name,
        out_shape=(pltpu.HBM(g.shape, g.dtype), pltpu.HBM(land.shape, land.dtype)),
        in_specs=(HBM_SPEC, HBM_SPEC, SEM_SPEC, SEM_SPEC, pl.BlockSpec(memory_space=pl.ANY)),
        out_specs=(HBM_SPEC, HBM_SPEC),
        input_output_aliases={0: 0, 1: 1},
        compiler_params=pltpu.CompilerParams(has_side_effects=SPLIT_EFFECT),
    )(g, land, send_sem, recv_sem, after)


def _rs_add_halves(g, got, c_idx, name):
    _, r, n = g.shape
    half = r // 2
    tr = _tile(half, max(16, (1 << 19) // n // 16 * 16), 16)
    nb = half // tr

    def body(c_ref, g_ref, o_ref, out_ref, out16_ref):
        sm = g_ref[...] + o_ref[...]
        out_ref[...] = sm
        out16_ref[...] = sm.astype(out16_ref.dtype)

    blk = pl.BlockSpec((None, tr, n), lambda s, i, c: (s, i, 0))
    return pl.pallas_call(
        body, name=name,
        grid_spec=pltpu.PrefetchScalarGridSpec(
            num_scalar_prefetch=1, grid=(4, nb),
            in_specs=[pl.BlockSpec((None, tr, n), lambda s, i, c: (s, c[0] * nb + i, 0)), blk],
            out_specs=[blk, blk]),
        out_shape=[jax.ShapeDtypeStruct((4, half, n), F32), jax.ShapeDtypeStruct((4, half, n), RS_PAYLOAD)],
        compiler_params=_cp(("parallel", "parallel")),
    )(c_idx, g, got)


def _rs_start(p32, p16, name):
    _, h, n = p16.shape

    def body(p32_ref, p16_ref, l16_ref, l32_ref, send_sems, recv_sems, p32_t, p16_t, l16_t, l32_t, token):
        x, y, c, chips = _place()
        for j, (cx, cy) in enumerate(chips):
            for k, pc in enumerate((c, 1 - c)):
                pltpu.make_async_remote_copy(
                    src_ref=p16_ref.at[2 * cx + cy], dst_ref=l16_ref.at[c, j], send_sem=send_sems.at[3 * k + j],
                    recv_sem=recv_sems.at[3 * k + j], device_id=(cx, cy, pc), device_id_type=MESH).start()
        pltpu.make_async_remote_copy(
            src_ref=p32_ref.at[2 * x + y], dst_ref=l32_ref, send_sem=send_sems.at[6], recv_sem=recv_sems.at[6],
            device_id=(x, y, 1 - c), device_id_type=MESH).start()
        token[...] = jnp.zeros_like(token)

    sems = pltpu.SemaphoreType.DMA((7,))
    l16 = lax.empty((2, 3, h, n), p16.dtype)
    l32 = lax.empty((h, n), F32)
    out = pl.pallas_call(
        body, name=name,
        out_shape=(sems, sems, pltpu.HBM(p32.shape, F32), pltpu.HBM(p16.shape, p16.dtype),
                   pltpu.HBM(l16.shape, l16.dtype), pltpu.HBM(l32.shape, F32), jax.ShapeDtypeStruct((8, 128), F32)),
        in_specs=(HBM_SPEC,) * 4,
        out_specs=(SEM_SPEC, SEM_SPEC) + (HBM_SPEC,) * 4 + (pl.BlockSpec(memory_space=pltpu.VMEM),),
        input_output_aliases={0: 2, 1: 3, 2: 4, 3: 5},
        compiler_params=pltpu.CompilerParams(has_side_effects=SPLIT_EFFECT),
    )(_in_hbm(p32), _in_hbm(p16), _in_hbm(l16), _in_hbm(l32))
    return out[:6], out[6][0, 0]


def _rs_wait(handle, after, name):
    send_sems, recv_sems, p32, p16, l16, l32 = handle

    def body(p32_ref, p16_ref, l16_ref, l32_ref, send_sems, recv_sems, after_ref, p32_o, p16_o, l16_o, l32_o):
        x, y, c, chips = _place()
        for j, (cx, cy) in enumerate(chips):
            for k, pc in enumerate((c, 1 - c)):
                cp = pltpu.make_async_remote_copy(
                    src_ref=p16_ref.at[2 * cx + cy], dst_ref=l16_ref.at[pc, j], send_sem=send_sems.at[3 * k + j],
                    recv_sem=recv_sems.at[3 * k + j], device_id=(cx, cy, pc), device_id_type=MESH)
                cp.wait_send()
                cp.wait_recv()
        cp = pltpu.make_async_remote_copy(
            src_ref=p32_ref.at[2 * x + y], dst_ref=l32_ref, send_sem=send_sems.at[6], recv_sem=recv_sems.at[6],
            device_id=(x, y, 1 - c), device_id_type=MESH)
        cp.wait_send()
        cp.wait_recv()

    out = pl.pallas_call(
        body, name=name,
        out_shape=(pltpu.HBM(p32.shape, F32), pltpu.HBM(p16.shape, p16.dtype), pltpu.HBM(l16.shape, l16.dtype),
                   pltpu.HBM(l32.shape, F32)),
        in_specs=(HBM_SPEC,) * 4 + (SEM_SPEC, SEM_SPEC, pl.BlockSpec(memory_space=pl.ANY)),
        out_specs=(HBM_SPEC,) * 4,
        input_output_aliases={0: 0, 1: 1, 2: 2, 3: 3},
        compiler_params=pltpu.CompilerParams(has_side_effects=SPLIT_EFFECT),
    )(p32, p16, l16, l32, send_sems, recv_sems, after)
    return out[0], out[2], out[3]


def _rs_finish(p32, l16, l32, c_idx, me_idx, name):
    _, h, n = p32.shape
    tr = _tile(h, max(16, (1 << 18) // n // 16 * 16), 16)
    nb = h // tr

    def body(c_ref, me_ref, own_ref, sib_ref, a_ref, b_ref, d_ref, out_ref):
        base = jnp.where(pl.program_id(0) == c_ref[0], own_ref[...], sib_ref[...])
        out_ref[...] = ((base + a_ref[...].astype(F32)) + b_ref[...].astype(F32)) + d_ref[...].astype(F32)

    def piece(j):
        return pl.BlockSpec((None, None, tr, n), lambda hc, i, c, me: (hc, j, i, 0))

    return pl.pallas_call(
        body, name=name,
        grid_spec=pltpu.PrefetchScalarGridSpec(
            num_scalar_prefetch=2, grid=(2, nb),
            in_specs=[pl.BlockSpec((None, tr, n), lambda hc, i, c, me: (me[0], i, 0)),
                      pl.BlockSpec((tr, n), lambda hc, i, c, me: (i, 0)),
                      piece(0), piece(1), piece(2)],
            out_specs=pl.BlockSpec((tr, n), lambda hc, i, c, me: (hc * nb + i, 0))),
        out_shape=jax.ShapeDtypeStruct((2 * h, n), F32),
        compiler_params=_cp(("parallel", "parallel")),
    )(c_idx, me_idx, p32, l32, l16, l16, l16)


def _rs_begin(swap, after, place, tag):
    g, got = _rs_swap_wait(swap, after, "rs_swap_wait_" + tag)
    p32, p16 = _rs_add_halves(g, got, place['c_idx'], "rs_add_halves")
    return _rs_start(p32, p16, "rs_start_" + tag)


def _rs_end(handle, after, place, tag):
    p32, l16, l32 = _rs_wait(handle, after, "rs_wait_" + tag)
    return _rs_finish(p32, l16, l32, place['c_idx'], place['me_idx'], "rs_finish")


def _all_reduce_small(v, name):
    R, n = v.shape

    def body(v_ref, out_ref, buf, send_sems, recv_sems):
        x, y, c, _ = _place()
        me = 4 * x + 2 * y + c
        buf[me] = v_ref[...]
        copies = []
        for d in range(1, 8):
            dx, dy, dc = (d >> 2) & 1, (d >> 1) & 1, d & 1
            px = x if dx == 0 else 1 - x
            py = y if dy == 0 else 1 - y
            pc = c if dc == 0 else 1 - c
            copies.append(pltpu.make_async_remote_copy(
                src_ref=v_ref, dst_ref=buf.at[me], send_sem=send_sems.at[d - 1], recv_sem=recv_sems.at[d - 1],
                device_id=(px, py, pc), device_id_type=MESH))
        for cp in copies:
            cp.start()
        for d in range(1, 8):
            dx, dy, dc = (d >> 2) & 1, (d >> 1) & 1, d & 1
            px = x if dx == 0 else 1 - x
            py = y if dy == 0 else 1 - y
            pc = c if dc == 0 else 1 - c
            pltpu.make_async_remote_copy(
                src_ref=v_ref, dst_ref=buf.at[4 * px + 2 * py + pc], send_sem=send_sems.at[d - 1],
                recv_sem=recv_sems.at[d - 1], device_id=(px, py, pc), device_id_type=MESH).wait_recv()
        for cp in copies:
            cp.wait_send()
        acc = buf[0]
        for k in range(1, 8):
            acc = acc + buf[k]
        out_ref[...] = acc

    return pl.pallas_call(
        body, name=name,
        in_specs=[pl.BlockSpec(memory_space=pltpu.VMEM)],
        out_specs=pl.BlockSpec(memory_space=pltpu.VMEM),
        out_shape=jax.ShapeDtypeStruct((R, n), F32),
        scratch_shapes=[pltpu.VMEM((8, R, n), F32), pltpu.SemaphoreType.DMA((7,)), pltpu.SemaphoreType.DMA((7,))],
        compiler_params=pltpu.CompilerParams(vmem_limit_bytes=VMEM_LIMIT_BYTES),
    )(v)


def _pack_small(parts, rows):
    flat = jnp.concatenate([a.reshape(-1) for a in parts])
    return jnp.pad(flat, (0, rows * LANES - flat.shape[0])).reshape(rows, LANES)


def _mem_fwd(proj, q_blk, mkv, heads):
    S = proj.shape[0]
    ML = mkv.shape[0]
    t = _tile(S, MEM_TILE, 8)
    scale = HEAD ** -0.5

    def body(q_ref, k_ref, v_ref, o_ref):
        s = _dot(q_ref[...], k_ref[...], NT) * scale
        m = jnp.max(s, axis=-1, keepdims=True)
        e = jnp.exp(s - m)
        p = e / jnp.sum(e, axis=-1, keepdims=True)
        o_ref[...] = _dot(p, v_ref[...], NN)

    return pl.pallas_call(
        body, name="mem_fwd", grid=(S // t, heads),
        in_specs=[pl.BlockSpec((t, HEAD), lambda i, h: (i, q_blk + h)),
                  pl.BlockSpec((ML, HEAD), lambda i, h: (0, h)),
                  pl.BlockSpec((ML, HEAD), lambda i, h: (0, heads + h))],
        out_specs=pl.BlockSpec((t, HEAD), lambda i, h: (i, h)),
        out_shape=jax.ShapeDtypeStruct((S, heads * HEAD), F32),
        compiler_params=_cp(("parallel", "parallel")),
    )(proj, mkv, mkv)


def _mem_bwd(proj, q_blk, mkv, dcat, d_blk, heads):
    S = proj.shape[0]
    ML = mkv.shape[0]
    t = _tile(S, MEM_TILE, 8)
    scale = HEAD ** -0.5

    def body(q_ref, k_ref, v_ref, do_ref, dq_ref, dk_ref, dv_ref):
        i = pl.program_id(1)

        @pl.when(i == 0)
        def _():
            dk_ref[...] = jnp.zeros_like(dk_ref)
            dv_ref[...] = jnp.zeros_like(dv_ref)

        q, k, v, do = q_ref[...], k_ref[...], v_ref[...], do_ref[...]
        s = _dot(q, k, NT) * scale
        m = jnp.max(s, axis=-1, keepdims=True)
        e = jnp.exp(s - m)
        p = e / jnp.sum(e, axis=-1, keepdims=True)
        dp = _dot(do, v, NT)
        ds = p * (dp - jnp.sum(p * dp, axis=-1, keepdims=True))
        dq_ref[...] = _dot(ds, k, NN) * scale
        dk_ref[...] += _dot(ds, q, TN) * scale
        dv_ref[...] += _dot(p, do, TN)

    dq, dk, dv = pl.pallas_call(
        body, name="mem_bwd", grid=(heads, S // t),
        in_specs=[pl.BlockSpec((t, HEAD), lambda h, i: (i, q_blk + h)),
                  pl.BlockSpec((ML, HEAD), lambda h, i: (0, h)),
                  pl.BlockSpec((ML, HEAD), lambda h, i: (0, heads + h)),
                  pl.BlockSpec((t, HEAD), lambda h, i: (i, d_blk + h))],
        out_specs=[pl.BlockSpec((t, HEAD), lambda h, i: (i, h)),
                   pl.BlockSpec((ML, HEAD), lambda h, i: (0, h)),
                   pl.BlockSpec((ML, HEAD), lambda h, i: (0, h))],
        out_shape=[jax.ShapeDtypeStruct((S, heads * HEAD), F32),
                   jax.ShapeDtypeStruct((ML, heads * HEAD), F32),
                   jax.ShapeDtypeStruct((ML, heads * HEAD), F32)],
        compiler_params=_cp(("parallel", "arbitrary")),
    )(proj, mkv, mkv, dcat)
    return dq, jnp.concatenate([dk, dv], axis=1)


def _fox_gates(gl, bf):
    S = gl.shape[0]

    def body(g_ref, b_ref, o_ref):
        xv = g_ref[...] + b_ref[...]
        c = jnp.minimum(xv, 0.0) - jnp.log(1.0 + jnp.exp(-jnp.abs(xv)))
        row = lax.broadcasted_iota(jnp.int32, c.shape, 0)
        d = 1
        while d < S:
            c = c + jnp.where(row >= d, pltpu.roll(c, d, 0), 0.0)
            d *= 2
        o_ref[...] = c

    return pl.pallas_call(
        body, name="fox_gates", out_shape=jax.ShapeDtypeStruct((S, 128), F32),
        in_specs=[pl.BlockSpec(memory_space=pltpu.VMEM)] * 2,
        out_specs=pl.BlockSpec(memory_space=pltpu.VMEM),
        compiler_params=_cp(),
    )(gl, bf)


def _fox_gates_bwd(gl, bf, dcf):
    S = gl.shape[0]

    def body(g_ref, b_ref, d_ref, dg_ref, db_ref):
        c = d_ref[...]
        row = lax.broadcasted_iota(jnp.int32, c.shape, 0)
        d = 1
        while d < S:
            c = c + jnp.where(row < S - d, pltpu.roll(c, S - d, 0), 0.0)
            d *= 2
        dx = c * _sigmoid(-(g_ref[...] + b_ref[...]))
        dg_ref[...] = dx
        db_ref[...] = jnp.sum(dx, axis=0, keepdims=True)

    return pl.pallas_call(
        body, name="fox_gates_bwd",
        out_shape=[jax.ShapeDtypeStruct((S, 128), F32), jax.ShapeDtypeStruct((1, 128), F32)],
        in_specs=[pl.BlockSpec(memory_space=pltpu.VMEM)] * 3,
        out_specs=[pl.BlockSpec(memory_space=pltpu.VMEM)] * 2,
        compiler_params=_cp(),
    )(gl, bf, dcf)


def _fox_scores(q, k, cq, ck, t, masked):
    s = _dot(q, k, NT) * (HEAD ** -0.5) + cq - ck
    if masked:
        row = lax.broadcasted_iota(jnp.int32, (t, t), 0)
        col = lax.broadcasted_iota(jnp.int32, (t, t), 1)
        s = jnp.where(row >= col, s, -jnp.inf)
    return s


def _fox_pairs(nq, by_key):
    if by_key:
        pairs = [(i, j) for j in range(nq) for i in range(j, nq)]
    else:
        pairs = [(i, j) for i in range(nq) for j in range(i + 1)]
    return (jnp.asarray(np.array([a for a, _ in pairs], np.int32)),
            jnp.asarray(np.array([b for _, b in pairs], np.int32)))


def _fox_heads_per_step(H):
    return 2 if H % 2 == 0 else 1


def _fox_fwd(proj, cfq, cfk, H):
    S = proj.shape[0]
    t = _tile(S, FOX_TILE)
    nq = S // t
    hb = _fox_heads_per_step(H)
    W, G = hb * HEAD, H // hb
    cols = [slice(i * HEAD, (i + 1) * HEAD) for i in range(hb)]
    qt, kt = _fox_pairs(nq, False)

    def body(qt_ref, kt_ref, q_ref, k_ref, v_ref, cq_ref, ck_ref, o_ref, lse_ref, m_s, l_s, acc_s):
        n = pl.program_id(1)
        qi, ki = qt_ref[n], kt_ref[n]

        @pl.when(ki == 0)
        def _():
            m_s[...] = jnp.full_like(m_s, -jnp.inf)
            l_s[...] = jnp.zeros_like(l_s)
            acc_s[...] = jnp.zeros_like(acc_s)

        def step(masked):
            R = range(hb)
            ss = [_fox_scores(q_ref[:, cols[i]], k_ref[:, cols[i]], cq_ref[i], ck_ref[i], t, masked) for i in R]
            m_new = [jnp.maximum(m_s[i], jnp.max(ss[i], axis=-1, keepdims=True)) for i in R]
            alpha = [jnp.exp(m_s[i] - m_new[i]) for i in R]
            ps = [jnp.exp(ss[i] - m_new[i]) for i in R]
            pv = [_dot(ps[i], v_ref[:, cols[i]], NN) for i in R]
            for i in R:
                l_s[i] = alpha[i] * l_s[i] + jnp.sum(ps[i], axis=-1, keepdims=True)
                acc_s[:, cols[i]] = alpha[i] * acc_s[:, cols[i]] + pv[i]
                m_s[i] = m_new[i]

        @pl.when(ki != qi)
        def _():
            step(False)

        @pl.when(ki == qi)
        def _():
            step(True)
            for i in range(hb):
                o_ref[:, cols[i]] = acc_s[:, cols[i]] / l_s[i]
                lse_ref[i] = m_s[i] + jnp.log(l_s[i])

    qcol = pl.BlockSpec((hb, t, 1), lambda h, n, qt, kt: (h, qt[n], 0))
    return pl.pallas_call(
        body, name="fox_fwd",
        grid_spec=pltpu.PrefetchScalarGridSpec(
            num_scalar_prefetch=2, grid=(G, qt.shape[0]),
            in_specs=[pl.BlockSpec((t, W), lambda h, n, qt, kt: (qt[n], h)),
                      pl.BlockSpec((t, W), lambda h, n, qt, kt: (kt[n], G + h)),
                      pl.BlockSpec((t, W), lambda h, n, qt, kt: (kt[n], 2 * G + h)),
                      qcol,
                      pl.BlockSpec((hb, 1, t), lambda h, n, qt, kt: (h, 0, kt[n]))],
            out_specs=[pl.BlockSpec((t, W), lambda h, n, qt, kt: (qt[n], h)), qcol],
            scratch_shapes=[pltpu.VMEM((hb, t, 1), F32), pltpu.VMEM((hb, t, 1), F32), pltpu.VMEM((t, W), F32)]),
        out_shape=[jax.ShapeDtypeStruct((S, H * HEAD), F32), jax.ShapeDtypeStruct((H, S, 1), F32)],
        compiler_params=_cp(("parallel", "arbitrary")),
    )(qt, kt, proj, proj, proj, cfq, cfk)


def _fox_bwd_rowdot(proj, cfq, cfk, lse, dcat, H):
    S = proj.shape[0]
    t = _tile(S, FOX_TILE)
    nq = S // t
    hb = _fox_heads_per_step(H)
    W, G = hb * HEAD, H // hb
    cols = [slice(i * HEAD, (i + 1) * HEAD) for i in range(hb)]
    qt, kt = _fox_pairs(nq, False)

    def body(qt_ref, kt_ref, q_ref, k_ref, v_ref, do_ref, lse_ref, cq_ref, ck_ref, d_ref):
        n = pl.program_id(1)
        qi, ki = qt_ref[n], kt_ref[n]

        @pl.when(ki == 0)
        def _():
            d_ref[...] = jnp.zeros_like(d_ref)

        def step(masked):
            R = range(hb)
            ss = [_fox_scores(q_ref[:, cols[i]], k_ref[:, cols[i]], cq_ref[i], ck_ref[i], t, masked) for i in R]
            dps = [_dot(do_ref[:, cols[i]], v_ref[:, cols[i]], NT) for i in R]
            ps = [jnp.exp(ss[i] - lse_ref[i]) for i in R]
            for i in R:
                d_ref[i] += jnp.sum(ps[i] * dps[i], axis=-1, keepdims=True)

        @pl.when(ki != qi)
        def _():
            step(False)

        @pl.when(ki == qi)
        def _():
            step(True)

    qtile = pl.BlockSpec((t, W), lambda h, n, qt, kt: (qt[n], h))
    qcol = pl.BlockSpec((hb, t, 1), lambda h, n, qt, kt: (h, qt[n], 0))
    return pl.pallas_call(
        body, name="fox_bwd_rowdot",
        grid_spec=pltpu.PrefetchScalarGridSpec(
            num_scalar_prefetch=2, grid=(G, qt.shape[0]),
            in_specs=[qtile,
                      pl.BlockSpec((t, W), lambda h, n, qt, kt: (kt[n], G + h)),
                      pl.BlockSpec((t, W), lambda h, n, qt, kt: (kt[n], 2 * G + h)),
                      qtile, qcol, qcol,
                      pl.BlockSpec((hb, 1, t), lambda h, n, qt, kt: (h, 0, kt[n]))],
            out_specs=qcol),
        out_shape=jax.ShapeDtypeStruct((H, S, 1), F32),
        compiler_params=_cp(("parallel", "arbitrary")),
    )(qt, kt, proj, proj, proj, dcat, lse, cfq, cfk)


def _fox_bwd(proj, cfq, cfk, rowdot, lse, dcat, H):
    S = proj.shape[0]
    t = _tile(S, FOX_TILE)
    nq = S // t
    scale = HEAD ** -0.5
    hb = _fox_heads_per_step(H)
    W, G = hb * HEAD, H // hb
    cols = [slice(i * HEAD, (i + 1) * HEAD) for i in range(hb)]
    qt, kt = _fox_pairs(nq, True)

    def body(qt_ref, kt_ref, q_ref, k_ref, v_ref, dd_ref, do_ref, lse_ref, cq_ref, ck_ref,
             dq_ref, dk_ref, dv_ref, dck_ref):
        n = pl.program_id(1)
        i_, j_ = qt_ref[n], kt_ref[n]

        @pl.when(n == 0)
        def _():
            dq_ref[...] = jnp.zeros_like(dq_ref)

        @pl.when(i_ == j_)
        def _():
            dk_ref[...] = jnp.zeros_like(dk_ref)
            dv_ref[...] = jnp.zeros_like(dv_ref)
            dck_ref[...] = jnp.zeros_like(dck_ref)

        def step(masked):
            R = range(hb)
            qs, ks = [q_ref[:, c] for c in cols], [k_ref[:, c] for c in cols]
            dos = [do_ref[:, c] for c in cols]
            ss = [_fox_scores(qs[i], ks[i], cq_ref[i], ck_ref[i], t, masked) for i in R]
            dps = [_dot(dos[i], v_ref[:, cols[i]], NT) for i in R]
            ps = [jnp.exp(ss[i] - lse_ref[i]) for i in R]
            dss = [ps[i] * (dps[i] - dd_ref[i]) for i in R]
            dvs = [_dot(ps[i], dos[i], TN) for i in R]
            dks = [_dot(dss[i], qs[i], TN) * scale for i in R]
            dqs = [_dot(dss[i], ks[i], NN) * scale for i in R]
            rows = pl.ds(pl.multiple_of(i_ * t, t), t)
            for i in R:
                dv_ref[:, cols[i]] += dvs[i]
                dk_ref[:, cols[i]] += dks[i]
                dq_ref[rows, cols[i]] += dqs[i]
                dck_ref[i] -= jnp.sum(dss[i], axis=0, keepdims=True)

        @pl.when(i_ != j_)
        def _():
            step(False)

        @pl.when(i_ == j_)
        def _():
            step(True)

    qtile = pl.BlockSpec((t, W), lambda h, n, qt, kt: (qt[n], h))
    qcol = pl.BlockSpec((hb, t, 1), lambda h, n, qt, kt: (h, qt[n], 0))
    ktile = pl.BlockSpec((t, W), lambda h, n, qt, kt: (kt[n], h))
    krow = pl.BlockSpec((hb, 1, t), lambda h, n, qt, kt: (h, 0, kt[n]))
    return pl.pallas_call(
        body, name="fox_bwd",
        grid_spec=pltpu.PrefetchScalarGridSpec(
            num_scalar_prefetch=2, grid=(G, qt.shape[0]),
            in_specs=[qtile,
                      pl.BlockSpec((t, W), lambda h, n, qt, kt: (kt[n], G + h)),
                      pl.BlockSpec((t, W), lambda h, n, qt, kt: (kt[n], 2 * G + h)),
                      qcol, qtile, qcol, qcol, krow],
            out_specs=[pl.BlockSpec((S, W), lambda h, n, qt, kt: (0, h)), ktile, ktile, krow]),
        out_shape=[jax.ShapeDtypeStruct((S, H * HEAD), F32)] * 3 + [jax.ShapeDtypeStruct((H, 1, S), F32)],
        compiler_params=_cp(("parallel", "arbitrary")),
    )(qt, kt, proj, proj, proj, rowdot, dcat, lse, cfq, cfk)


def _s5_prep(lam_re, lam_im, log_dt, b_re, b_im, c_re, c_im):
    G, P = lam_re.shape
    ns = G // 16
    dt = jnp.exp(log_dt)[:, None]
    mag = jnp.exp(lam_re * dt)
    a_re, a_im = mag * jnp.cos(lam_im * dt), mag * jnp.sin(lam_im * dt)
    den = lam_re * lam_re + lam_im * lam_im
    z_re = ((a_re - 1.0) * lam_re + a_im * lam_im) / den
    z_im = (a_im * lam_re - (a_re - 1.0) * lam_im) / den
    bb_re = z_re[..., None] * b_re - z_im[..., None] * b_im
    bb_im = z_re[..., None] * b_im + z_im[..., None] * b_re
    eye = jnp.eye(16, dtype=F32)
    bb = jnp.stack([bb_re, bb_im]).reshape(2, ns, 16, P, S5_GROUP)
    wb = jnp.einsum('asgpc,gh->sgcahp', bb, eye).reshape(ns, S5_SLAB, 2 * 16 * P)
    cc = jnp.stack([c_re, -c_im]).reshape(2, ns, 16, S5_GROUP, P)
    wc = jnp.einsum('asgcp,gh->sagphc', cc, eye).reshape(ns, 2 * 16 * P, S5_SLAB)
    a = jnp.concatenate([a_re.reshape(ns, 1, 16 * P), a_im.reshape(ns, 1, 16 * P)], axis=-1)
    return wb, wc, a


def _s5_tables(lam_re, lam_im, log_dt):
    G, P = lam_re.shape
    ns = G // 16
    dt = jnp.exp(log_dt)[:, None]
    tt = jnp.arange(1, S5_CHUNK + 1, dtype=F32)[:, None, None]
    mag = jnp.exp(lam_re * dt * tt)
    ang = lam_im * dt * tt
    pr = (mag * jnp.cos(ang)).reshape(S5_CHUNK, ns, 16 * P).transpose(1, 0, 2)
    pi = (mag * jnp.sin(ang)).reshape(S5_CHUNK, ns, 16 * P).transpose(1, 0, 2)
    return pr, pi, pr[:, ::-1], pi[:, ::-1]


def _s5_scan_fwd(proj, wb, wc, pr, pi, dskip):
    S = proj.shape[0]
    ns = wb.shape[0]
    W = wb.shape[2]
    hw = W // 2
    T = S5_CHUNK
    nc = S // T
    mix = ns * S5_SLAB

    def body(u_ref, wb_ref, wc_ref, pr_ref, pi_ref, d_ref, v_ref, yg_ref, h_ref, cin_ref, carry):
        c = pl.program_id(1)

        @pl.when(c == 0)
        def _():
            carry[...] = jnp.zeros_like(carry)

        u = u_ref[...]
        bu = _dot(u, wb_ref[...], NN)
        xr, xi = bu[:, :hw], bu[:, hw:]
        sub = lax.broadcasted_iota(jnp.int32, (T, hw), 0) & (S5_ROWS - 1)
        d = 1
        while d < S5_ROWS:
            ar, ai = pr_ref[pl.ds(d - 1, 1), :], pi_ref[pl.ds(d - 1, 1), :]
            sr = jnp.where(sub >= d, pltpu.roll(xr, d, 0), 0.0)
            si = jnp.where(sub >= d, pltpu.roll(xi, d, 0), 0.0)
            xr, xi = xr + ar * sr - ai * si, xi + ar * si + ai * sr
            d *= 2
        cin_ref[...] = carry[...]
        cr, ci = carry[:, :hw], carry[:, hw:]
        pwr, pwi = pr_ref[pl.ds(0, S5_ROWS), :], pi_ref[pl.ds(0, S5_ROWS), :]
        for g in range(T // S5_ROWS):
            rows = slice(g * S5_ROWS, (g + 1) * S5_ROWS)
            hr = xr[rows, :] + pwr * cr - pwi * ci
            hi = xi[rows, :] + pwr * ci + pwi * cr
            h_ref[rows, :hw] = hr
            h_ref[rows, hw:] = hi
            cr, ci = hr[S5_ROWS - 1:S5_ROWS, :], hi[S5_ROWS - 1:S5_ROWS, :]
        carry[:, :hw] = cr
        carry[:, hw:] = ci
        y = _dot(h_ref[...], wc_ref[...], NN)
        v = y + d_ref[...] * u
        v_ref[...] = v
        yg_ref[...] = _gelu(v)

    return pl.pallas_call(
        body, name="s5_scan_fwd", grid=(ns, nc),
        in_specs=[pl.BlockSpec((T, S5_SLAB), lambda s, c: (c, s)),
                  pl.BlockSpec((None, S5_SLAB, W), lambda s, c: (s, 0, 0)),
                  pl.BlockSpec((None, W, S5_SLAB), lambda s, c: (s, 0, 0)),
                  pl.BlockSpec((None, T, hw), lambda s, c: (s, 0, 0)),
                  pl.BlockSpec((None, T, hw), lambda s, c: (s, 0, 0)),
                  pl.BlockSpec((1, S5_SLAB), lambda s, c: (0, s))],
        out_specs=[pl.BlockSpec((T, S5_SLAB), lambda s, c: (c, s)),
                   pl.BlockSpec((T, S5_SLAB), lambda s, c: (c, s)),
                   pl.BlockSpec((T, W), lambda s, c: (c, s)),
                   pl.BlockSpec((None, 1, W), lambda s, c: (c, 0, s))],
        out_shape=[jax.ShapeDtypeStruct((S, mix), F32), jax.ShapeDtypeStruct((S, mix), F32),
                   jax.ShapeDtypeStruct((S, ns * W), F32), jax.ShapeDtypeStruct((nc, 1, ns * W), F32)],
        scratch_shapes=[pltpu.VMEM((1, W), F32)],
        compiler_params=_cp(("parallel", "arbitrary")),
    )(proj, wb, wc, pr, pi, dskip)


def _s5_scan_bwd(dv, proj, hs, cin, wb, wc, pr, pi, prr, pir, dskip):
    S = proj.shape[0]
    ns = wb.shape[0]
    W = wb.shape[2]
    hw = W // 2
    T = S5_CHUNK
    nc = S // T
    mix = ns * S5_SLAB

    def body(dv_ref, u_ref, h_ref, cin_ref, wb_ref, wc_ref, pr_ref, pi_ref, prr_ref, pir_ref, d_ref,
             du_ref, dwb_ref, dwc_ref, da_ref, dd_ref, lam_s, carry):
        c = pl.program_id(1)

        @pl.when(c == 0)
        def _():
            carry[...] = jnp.zeros_like(carry)
            dwb_ref[...] = jnp.zeros_like(dwb_ref)
            dwc_ref[...] = jnp.zeros_like(dwc_ref)
            da_ref[...] = jnp.zeros_like(da_ref)
            dd_ref[...] = jnp.zeros_like(dd_ref)

        dy, u = dv_ref[...], u_ref[...]
        dh = _dot(dy, wc_ref[...], NT)
        gr, gi = dh[:, :hw], dh[:, hw:]
        row = lax.broadcasted_iota(jnp.int32, (T, hw), 0)
        sub = row & (S5_ROWS - 1)
        d = 1
        while d < S5_ROWS:
            ar, ai = pr_ref[pl.ds(d - 1, 1), :], -pi_ref[pl.ds(d - 1, 1), :]
            sr = jnp.where(sub < S5_ROWS - d, pltpu.roll(gr, T - d, 0), 0.0)
            si = jnp.where(sub < S5_ROWS - d, pltpu.roll(gi, T - d, 0), 0.0)
            gr, gi = gr + ar * sr - ai * si, gi + ar * si + ai * sr
            d *= 2
        lr, li = carry[:, :hw], carry[:, hw:]
        pwr, pwi = prr_ref[pl.ds(T - S5_ROWS, S5_ROWS), :], -pir_ref[pl.ds(T - S5_ROWS, S5_ROWS), :]
        for g in reversed(range(T // S5_ROWS)):
            rows = slice(g * S5_ROWS, (g + 1) * S5_ROWS)
            lgr = gr[rows, :] + pwr * lr - pwi * li
            lgi = gi[rows, :] + pwr * li + pwi * lr
            lam_s[rows, :hw] = lgr
            lam_s[rows, hw:] = lgi
            lr, li = lgr[0:1, :], lgi[0:1, :]
        carry[:, :hw] = lr
        carry[:, hw:] = li
        gr, gi = lam_s[:, :hw], lam_s[:, hw:]
        hr, hi = h_ref[:, :hw], h_ref[:, hw:]
        hpr = jnp.where(row >= 1, pltpu.roll(hr, 1, 0), cin_ref[:, :hw])
        hpi = jnp.where(row >= 1, pltpu.roll(hi, 1, 0), cin_ref[:, hw:])
        da_ref[:, :hw] += jnp.sum(hpr * gr + hpi * gi, axis=0, keepdims=True)
        da_ref[:, hw:] += jnp.sum(hpr * gi - hpi * gr, axis=0, keepdims=True)
        lam = lam_s[...]
        du_ref[...] = _dot(lam, wb_ref[...], NT) + dy * d_ref[...]
        dwb_ref[...] += _dot(u, lam, TN)
        dwc_ref[...] += _dot(h_ref[...], dy, TN)
        dd_ref[...] += jnp.sum(dy * u, axis=0, keepdims=True)

    def rc(c):
        return nc - 1 - c

    return pl.pallas_call(
        body, name="s5_scan_bwd", grid=(ns, nc),
        in_specs=[pl.BlockSpec((T, S5_SLAB), lambda s, c: (rc(c), s)),
                  pl.BlockSpec((T, S5_SLAB), lambda s, c: (rc(c), s)),
                  pl.BlockSpec((T, W), lambda s, c: (rc(c), s)),
                  pl.BlockSpec((None, 1, W), lambda s, c: (rc(c), 0, s)),
                  pl.BlockSpec((None, S5_SLAB, W), lambda s, c: (s, 0, 0)),
                  pl.BlockSpec((None, W, S5_SLAB), lambda s, c: (s, 0, 0)),
                  pl.BlockSpec((None, T, hw), lambda s, c: (s, 0, 0)),
                  pl.BlockSpec((None, T, hw), lambda s, c: (s, 0, 0)),
                  pl.BlockSpec((None, T, hw), lambda s, c: (s, 0, 0)),
                  pl.BlockSpec((None, T, hw), lambda s, c: (s, 0, 0)),
                  pl.BlockSpec((1, S5_SLAB), lambda s, c: (0, s))],
        out_specs=[pl.BlockSpec((T, S5_SLAB), lambda s, c: (rc(c), s)),
                   pl.BlockSpec((None, S5_SLAB, W), lambda s, c: (s, 0, 0)),
                   pl.BlockSpec((None, W, S5_SLAB), lambda s, c: (s, 0, 0)),
                   pl.BlockSpec((None, 1, W), lambda s, c: (s, 0, 0)),
                   pl.BlockSpec((1, S5_SLAB), lambda s, c: (0, s))],
        out_shape=[jax.ShapeDtypeStruct((S, mix), F32), jax.ShapeDtypeStruct(wb.shape, F32),
                   jax.ShapeDtypeStruct(wc.shape, F32), jax.ShapeDtypeStruct((ns, 1, W), F32),
                   jax.ShapeDtypeStruct((1, mix), F32)],
        scratch_shapes=[pltpu.VMEM((T, W), F32), pltpu.VMEM((1, W), F32)],
        compiler_params=_cp(("parallel", "arbitrary")),
    )(dv, proj, hs, cin, wb, wc, pr, pi, prr, pir, dskip)


def _s5_glu_bwd(dcat, yg, z):
    S, mix = yg.shape
    tr = _tile(S, ROW_TILE, 8)

    def body(do_ref, yg_ref, z_ref, dz_ref, dy_ref, db_ref):
        i = pl.program_id(0)

        @pl.when(i == 0)
        def _():
            db_ref[...] = jnp.zeros_like(db_ref)

        do, yg_, sz = do_ref[...], yg_ref[...], _sigmoid(z_ref[...])
        dz = do * yg_ * sz * (1.0 - sz)
        dz_ref[...] = dz
        dy_ref[...] = do * sz
        db_ref[...] += jnp.sum(dz, axis=0, keepdims=True)

    blk = pl.BlockSpec((tr, mix), lambda i: (i, 0))
    return pl.pallas_call(
        body, name="s5_glu_bwd", grid=(S // tr,),
        in_specs=[blk, blk, blk], out_specs=[blk, blk, pl.BlockSpec((1, mix), lambda i: (0, 0))],
        out_shape=[jax.ShapeDtypeStruct((S, mix), F32), jax.ShapeDtypeStruct((S, mix), F32),
                   jax.ShapeDtypeStruct((1, mix), F32)],
        compiler_params=_cp(("arbitrary",)),
    )(dcat, yg, z)


def _rows_down(x, j):
    return x if j == 0 else pltpu.roll(x, j, 0)


def _conv_rows(xe, w_ref, n):
    c = None
    for j in range(GDN_CONV):
        term = w_ref[pl.ds(GDN_CONV - 1 - j, 1), :] * _rows_down(xe, j)[8:8 + n, :]
        c = term if c is None else c + term
    return c


def _gdn_prep(proj, blk0, nblk, convw, norm, scale, name):
    S = proj.shape[0]
    tr = _tile(S, CONV_TILE, 8)
    nb8 = tr // 8

    def body(x_ref, xb_ref, w_ref, o_ref):
        i = pl.program_id(1)
        xe = jnp.concatenate([jnp.where(i == 0, 0.0, xb_ref[...]), x_ref[...]], axis=0)
        c = _conv_rows(xe, w_ref, tr)
        s = c * _sigmoid(c)
        if norm:
            s = s * lax.rsqrt(jnp.sum(s * s, axis=-1, keepdims=True) + EPS) * scale
        o_ref[...] = s

    return pl.pallas_call(
        body, name=name, grid=(nblk, S // tr),
        in_specs=[pl.BlockSpec((tr, HEAD), lambda j, i: (i, blk0 + j)),
                  pl.BlockSpec((8, HEAD), lambda j, i: (jnp.maximum(i * nb8 - 1, 0), blk0 + j)),
                  pl.BlockSpec((GDN_CONV, HEAD), lambda j, i: (0, j))],
        out_specs=pl.BlockSpec((tr, HEAD), lambda j, i: (i, j)),
        out_shape=jax.ShapeDtypeStruct((S, nblk * HEAD), F32),
        compiler_params=_cp(("parallel", "parallel")),
    )(proj, proj, convw)


def _gdn_prep_bwd(proj, blk0, nblk, convw, dout, norm, scale, name):
    S = proj.shape[0]
    tr = _tile(S, CONV_TILE, 8)
    nb8 = tr // 8
    last8 = S // 8 - 1
    nrow = S // tr

    def body(x_ref, xb_ref, xa_ref, w_ref, d_ref, da_ref, dx_ref, dw_ref):
        i = pl.program_id(1)

        @pl.when(i == 0)
        def _():
            dw_ref[...] = jnp.zeros_like(dw_ref)

        xe = jnp.concatenate([jnp.where(i == 0, 0.0, xb_ref[...]), x_ref[...], xa_ref[...]], axis=0)
        de = jnp.concatenate([d_ref[...], da_ref[...]], axis=0)
        n = tr + 8
        c = _conv_rows(xe, w_ref, n)
        sg = _sigmoid(c)
        s = c * sg
        if norm:
            r = lax.rsqrt(jnp.sum(s * s, axis=-1, keepdims=True) + EPS)
            ds = scale * r * (de - s * (r * r) * jnp.sum(de * s, axis=-1, keepdims=True))
        else:
            ds = de
        dc = ds * (sg + c * sg * (1.0 - sg))
        rowi = lax.broadcasted_iota(jnp.int32, (n, HEAD), 0)
        dc = jnp.where((i == nrow - 1) & (rowi >= tr), 0.0, dc)
        dct = dc[:tr, :]
        dx = None
        for j in range(GDN_CONV):
            tap = pl.ds(GDN_CONV - 1 - j, 1)
            up = dct if j == 0 else pltpu.roll(dc, n - j, 0)[:tr, :]
            term = w_ref[tap, :] * up
            dx = term if dx is None else dx + term
            dw_ref[tap, :] += jnp.sum(dct * _rows_down(xe, j)[8:8 + tr, :], axis=0, keepdims=True)
        dx_ref[...] = dx

    return pl.pallas_call(
        body, name=name, grid=(nblk, nrow),
        in_specs=[pl.BlockSpec((tr, HEAD), lambda j, i: (i, blk0 + j)),
                  pl.BlockSpec((8, HEAD), lambda j, i: (jnp.maximum(i * nb8 - 1, 0), blk0 + j)),
                  pl.BlockSpec((8, HEAD), lambda j, i: (jnp.minimum((i + 1) * nb8, last8), blk0 + j)),
                  pl.BlockSpec((GDN_CONV, HEAD), lambda j, i: (0, j)),
                  pl.BlockSpec((tr, HEAD), lambda j, i: (i, j)),
                  pl.BlockSpec((8, HEAD), lambda j, i: (jnp.minimum((i + 1) * nb8, last8), j))],
        out_specs=[pl.BlockSpec((tr, HEAD), lambda j, i: (i, j)),
                   pl.BlockSpec((GDN_CONV, HEAD), lambda j, i: (0, j))],
        out_shape=[jax.ShapeDtypeStruct((S, nblk * HEAD), F32),
                   jax.ShapeDtypeStruct((GDN_CONV, nblk * HEAD), F32)],
        compiler_params=_cp(("parallel", "arbitrary")),
    )(proj, proj, proj, convw, dout, dout)


def _gdn_gates(pg, alog, dtb):
    S = pg.shape[0]

    def body(a_ref, b_ref, al_ref, dt_ref, gc_ref, be_ref):
        g = -jnp.exp(al_ref[...]) * _softplus(a_ref[...] + dt_ref[...])
        rowm = lax.broadcasted_iota(jnp.int32, g.shape, 0) & (GDN_CHUNK - 1)
        c = g
        d = 1
        while d < GDN_CHUNK:
            c = c + jnp.where(rowm >= d, pltpu.roll(c, d, 0), 0.0)
            d *= 2
        gc_ref[...] = c
        be_ref[...] = _sigmoid(b_ref[...])

    blk = pl.BlockSpec((S, 128), lambda i: (0, 0))
    vec = pl.BlockSpec((1, 128), lambda i: (0, 0))
    return pl.pallas_call(
        body, name="gdn_gates", grid=(1,),
        in_specs=[blk, pl.BlockSpec((S, 128), lambda i: (0, 1)), vec, vec],
        out_specs=[blk, blk],
        out_shape=[jax.ShapeDtypeStruct((S, 128), F32)] * 2,
        compiler_params=_cp(("arbitrary",)),
    )(pg, pg, alog, dtb)


def _gdn_gates_bwd(pg, alog, dtb, dgc, dbeta):
    S = pg.shape[0]

    def body(a_ref, b_ref, al_ref, dt_ref, dgc_ref, dbe_ref, dpa_ref, dpb_ref, dal_ref, ddt_ref):
        rowm = lax.broadcasted_iota(jnp.int32, (S, 128), 0) & (GDN_CHUNK - 1)
        c = dgc_ref[...]
        d = 1
        while d < GDN_CHUNK:
            c = c + jnp.where(rowm < GDN_CHUNK - d, pltpu.roll(c, S - d, 0), 0.0)
            d *= 2
        xv = a_ref[...] + dt_ref[...]
        ea = jnp.exp(al_ref[...])
        g = -ea * _softplus(xv)
        dx = c * (-ea) * _sigmoid(xv)
        dpa_ref[...] = dx
        dal_ref[...] = jnp.sum(c * g, axis=0, keepdims=True)
        ddt_ref[...] = jnp.sum(dx, axis=0, keepdims=True)
        be = _sigmoid(b_ref[...])
        dpb_ref[...] = dbe_ref[...] * be * (1.0 - be)

    blk = pl.BlockSpec((S, 128), lambda i: (0, 0))
    blk1 = pl.BlockSpec((S, 128), lambda i: (0, 1))
    vec = pl.BlockSpec((1, 128), lambda i: (0, 0))
    dpa, dpb, dal, ddt = pl.pallas_call(
        body, name="gdn_gates_bwd", grid=(1,),
        in_specs=[blk, blk1, vec, vec, blk, blk],
        out_specs=[blk, blk, vec, vec],
        out_shape=[jax.ShapeDtypeStruct((S, 128), F32)] * 2 + [jax.ShapeDtypeStruct((1, 128), F32)] * 2,
        compiler_params=_cp(("arbitrary",)),
    )(pg, pg, alog, dtb, dgc, dbeta)
    return jnp.concatenate([dpa, dpb], axis=1), dal, ddt


def _gdn_pre(qs, ks, vs, gcs, grs, betas):
    C = GDN_CHUNK
    n = len(qs)
    r = lax.broadcasted_iota(jnp.int32, (C, C), 0)
    c_ = lax.broadcasted_iota(jnp.int32, (C, C), 1)
    lower, strict = r >= c_, r > c_
    eye = jnp.where(r == c_, 1.0, 0.0)
    decs = [jnp.exp(jnp.where(lower, gcs[i] - grs[i], -jnp.inf)) for i in range(n)]
    kbs = [ks[i] * betas[i] for i in range(n)]
    vbs = [vs[i] * betas[i] for i in range(n)]
    lmats = [jnp.where(strict, _dot(kbs[i], ks[i], NT) * decs[i], 0.0) for i in range(n)]
    amats = [jnp.where(lower, _dot(qs[i], ks[i], NT) * decs[i], 0.0) for i in range(n)]
    pks = [-lm for lm in lmats]
    tinvs = [eye + pk for pk in pks]
    for _ in range(5):
        pks = [_dotf(pk, pk, NN) for pk in pks]
        tinvs = [tv + _dotf(tv, pk, NN) for tv, pk in zip(tinvs, pks)]
    es = [jnp.exp(gc) for gc in gcs]
    glasts = [gc[C - 1:C, :] for gc in gcs]
    fs = [jnp.exp(gl - gc) for gl, gc in zip(glasts, gcs)]
    gls = [jnp.exp(gl) for gl in glasts]
    us = [_dotf(tinvs[i], vbs[i], NN) for i in range(n)]
    ws = [_dotf(tinvs[i], kbs[i] * es[i], NN) for i in range(n)]
    return [dict(lower=lower, strict=strict, dec=decs[i], kb=kbs[i], vb=vbs[i], lmat=lmats[i], tinv=tinvs[i],
                 e=es[i], f=fs[i], gl=gls[i], u=us[i], w=ws[i], amat=amats[i], qd=qs[i] * es[i],
                 kd=ks[i] * fs[i]) for i in range(n)]


def _gdn_heads_per_step(H):
    return max(d for d in (1, 2, 3, 4) if H % d == 0)


def _gdn_chunk_fwd(q, k, v, gcol, grow, bcol):
    S = q.shape[0]
    H, NC = gcol.shape[0], gcol.shape[1]
    C = GDN_CHUNK
    hb = _gdn_heads_per_step(H)

    def body(q_ref, k_ref, v_ref, gc_ref, gr_ref, b_ref, o_ref, st_ref, state):
        n = pl.program_id(1)

        @pl.when(n == 0)
        def _():
            state[...] = jnp.zeros_like(state)

        cols = [slice(i * HEAD, (i + 1) * HEAD) for i in range(hb)]
        ps = _gdn_pre([q_ref[:, c] for c in cols], [k_ref[:, c] for c in cols], [v_ref[:, c] for c in cols],
                      [gc_ref[i] for i in range(hb)], [gr_ref[i] for i in range(hb)],
                      [b_ref[i] for i in range(hb)])
        s0s = [state[i] for i in range(hb)]
        vns = [ps[i]['u'] - _dot(ps[i]['w'], s0s[i], NN) for i in range(hb)]
        outs = [_dot(ps[i]['qd'], s0s[i], NN) + _dot(ps[i]['amat'], vns[i], NN) for i in range(hb)]
        news = [s0s[i] * ps[i]['gl'] + _dot(ps[i]['kd'], vns[i], TN) for i in range(hb)]
        for i in range(hb):
            st_ref[i] = s0s[i]
            o_ref[:, cols[i]] = outs[i]
            state[i] = news[i]

    tok = pl.BlockSpec((C, hb * HEAD), lambda h, n: (n, h))
    col = pl.BlockSpec((hb, None, C, 1), lambda h, n: (h, n, 0, 0))
    rowb = pl.BlockSpec((hb, None, 1, C), lambda h, n: (h, n, 0, 0))
    return pl.pallas_call(
        body, name="gdn_chunk_fwd", grid=(H // hb, NC),
        in_specs=[tok, tok, tok, col, rowb, col],
        out_specs=[tok, pl.BlockSpec((hb, None, HEAD, HEAD), lambda h, n: (h, n, 0, 0))],
        out_shape=[jax.ShapeDtypeStruct((S, H * HEAD), F32), jax.ShapeDtypeStruct((H, NC, HEAD, HEAD), F32)],
        scratch_shapes=[pltpu.VMEM((hb, HEAD, HEAD), F32)],
        compiler_params=_cp(("parallel", "arbitrary")),
    )(q, k, v, gcol, grow, bcol)


def _gdn_chunk_bwd(q, k, v, gcol, grow, bcol, st, do):
    S = q.shape[0]
    H, NC = gcol.shape[0], gcol.shape[1]
    C = GDN_CHUNK
    hb = _gdn_heads_per_step(H)

    def body(q_ref, k_ref, v_ref, gc_ref, gr_ref, b_ref, st_ref, do_ref,
             dq_ref, dk_ref, dv_ref, dgc_ref, dbe_ref, dstate):
        n = pl.program_id(1)

        @pl.when(n == 0)
        def _():
            dstate[...] = jnp.zeros_like(dstate)

        R = range(hb)
        cols = [slice(i * HEAD, (i + 1) * HEAD) for i in R]
        qs, ks, vs = [q_ref[:, c] for c in cols], [k_ref[:, c] for c in cols], [v_ref[:, c] for c in cols]
        betas = [b_ref[i] for i in R]
        ps = _gdn_pre(qs, ks, vs, [gc_ref[i] for i in R], [gr_ref[i] for i in R], betas)
        lower, strict = ps[0]['lower'], ps[0]['strict']
        s0s, dos, ds1s = [st_ref[i] for i in R], [do_ref[:, c] for c in cols], [dstate[i] for i in R]
        vns = [ps[i]['u'] - _dot(ps[i]['w'], s0s[i], NN) for i in R]
        dvns = [_dot(ps[i]['amat'], dos[i], TN) + _dot(ps[i]['kd'], ds1s[i], NN) for i in R]
        damats = [jnp.where(lower, _dot(dos[i], vns[i], NT), 0.0) for i in R]
        dqds = [_dot(dos[i], s0s[i], NT) for i in R]
        dkds = [_dot(vns[i], ds1s[i], NT) for i in R]
        dgls = [jnp.sum(s0s[i] * ds1s[i], keepdims=True) for i in R]
        ds0s = [ps[i]['gl'] * ds1s[i] + _dot(ps[i]['qd'], dos[i], TN) - _dot(ps[i]['w'], dvns[i], TN) for i in R]
        dws = [-_dot(dvns[i], s0s[i], NT) for i in R]
        dvbs = [_dotf(ps[i]['tinv'], dvns[i], TN) for i in R]
        dkgs = [_dotf(ps[i]['tinv'], dws[i], TN) for i in R]
        dls = [-jnp.where(strict, _dotf(dvbs[i], ps[i]['u'], NT) + _dotf(dkgs[i], ps[i]['w'], NT), 0.0) for i in R]
        dkks = [dls[i] * ps[i]['dec'] for i in R]
        dqks = [damats[i] * ps[i]['dec'] for i in R]
        ms = [dls[i] * ps[i]['lmat'] + damats[i] * ps[i]['amat'] for i in R]
        dkbs = [_dot(dkks[i], ks[i], NN) + dkgs[i] * ps[i]['e'] for i in R]
        dks = [_dot(dkks[i], ps[i]['kb'], TN) + _dot(dqks[i], qs[i], TN) + dkds[i] * ps[i]['f'] + dkbs[i] * betas[i]
               for i in R]
        dqs = [_dot(dqks[i], ks[i], NN) + dqds[i] * ps[i]['e'] for i in R]
        ones = jnp.ones((C, HEAD), F32)
        colsums = [_dotf(ms[i], ones, TN)[:, 0:1] for i in R]
        rowi = lax.broadcasted_iota(jnp.int32, (C, 1), 0)
        for i in R:
            p = ps[i]
            de = (jnp.sum(dkgs[i] * p['kb'], axis=-1, keepdims=True)
                  + jnp.sum(dqds[i] * qs[i], axis=-1, keepdims=True))
            df = jnp.sum(dkds[i] * ks[i], axis=-1, keepdims=True)
            dgc = jnp.sum(ms[i], axis=-1, keepdims=True) - colsums[i] + de * p['e'] - df * p['f']
            dlast = jnp.sum(df * p['f'], keepdims=True) + dgls[i] * p['gl']
            dgc_ref[i] = dgc + jnp.where(rowi == C - 1, dlast, 0.0)
            dbe_ref[i] = (jnp.sum(dkbs[i] * ks[i], axis=-1, keepdims=True)
                          + jnp.sum(dvbs[i] * vs[i], axis=-1, keepdims=True))
            dstate[i] = ds0s[i]
            dq_ref[:, cols[i]] = dqs[i]
            dk_ref[:, cols[i]] = dks[i]
            dv_ref[:, cols[i]] = dvbs[i] * betas[i]

    def rn(n):
        return NC - 1 - n

    tok = pl.BlockSpec((C, hb * HEAD), lambda h, n: (rn(n), h))
    col = pl.BlockSpec((hb, None, C, 1), lambda h, n: (h, rn(n), 0, 0))
    rowb = pl.BlockSpec((hb, None, 1, C), lambda h, n: (h, rn(n), 0, 0))
    return pl.pallas_call(
        body, name="gdn_chunk_bwd", grid=(H // hb, NC),
        in_specs=[tok, tok, tok, col, rowb, col,
                  pl.BlockSpec((hb, None, HEAD, HEAD), lambda h, n: (h, rn(n), 0, 0)), tok],
        out_specs=[tok, tok, tok, col, col],
        out_shape=[jax.ShapeDtypeStruct((S, H * HEAD), F32)] * 3
        + [jax.ShapeDtypeStruct((H, NC, C, 1), F32)] * 2,
        scratch_shapes=[pltpu.VMEM((hb, HEAD, HEAD), F32)],
        compiler_params=_cp(("parallel", "arbitrary")),
    )(q, k, v, gcol, grow, bcol, st, do)


def _gdn_onorm(o, proj, gate_blk, w, H):
    S = o.shape[0]
    tr = _tile(S, CONV_TILE, 8)

    def body(o_ref, g_ref, w_ref, out_ref):
        ov, gv = o_ref[...], g_ref[...]
        r = lax.rsqrt(jnp.mean(ov * ov, axis=-1, keepdims=True) + EPS)
        out_ref[...] = (ov * r * w_ref[...]) * (gv * _sigmoid(gv))

    return pl.pallas_call(
        body, name="gdn_onorm", grid=(S // tr, H),
        in_specs=[pl.BlockSpec((tr, HEAD), lambda i, h: (i, h)),
                  pl.BlockSpec((tr, HEAD), lambda i, h: (i, gate_blk + h)),
                  pl.BlockSpec((1, HEAD), lambda i, h: (0, 0))],
        out_specs=pl.BlockSpec((tr, HEAD), lambda i, h: (i, h)),
        out_shape=jax.ShapeDtypeStruct((S, H * HEAD), F32),
        compiler_params=_cp(("parallel", "parallel")),
    )(o, proj, w)


def _gdn_onorm_bwd(dcat, o, proj, gate_blk, w, H):
    S = o.shape[0]
    tr = _tile(S, CONV_TILE, 8)

    def body(d_ref, o_ref, g_ref, w_ref, do_ref, dg_ref, dw_ref):
        i, h = pl.program_id(0), pl.program_id(1)

        @pl.when((i == 0) & (h == 0))
        def _():
            dw_ref[...] = jnp.zeros_like(dw_ref)

        dm, ov, gv, wv = d_ref[...], o_ref[...], g_ref[...], w_ref[...]
        r = lax.rsqrt(jnp.mean(ov * ov, axis=-1, keepdims=True) + EPS)
        oh = ov * r
        sg = gv * _sigmoid(gv)
        dy = dm * sg
        t = dy * wv
        do_ref[...] = r * (t - oh * jnp.mean(t * oh, axis=-1, keepdims=True))
        dg_ref[...] = dm * (oh * wv) * _silu_grad(gv)
        dw_ref[...] += jnp.sum(dy * oh, axis=0, keepdims=True)

    tok = pl.BlockSpec((tr, HEAD), lambda i, h: (i, h))
    vec = pl.BlockSpec((1, HEAD), lambda i, h: (0, 0))
    return pl.pallas_call(
        body, name="gdn_onorm_bwd", grid=(S // tr, H),
        in_specs=[tok, tok, pl.BlockSpec((tr, HEAD), lambda i, h: (i, gate_blk + h)), vec],
        out_specs=[tok, tok, vec],
        out_shape=[jax.ShapeDtypeStruct((S, H * HEAD), F32)] * 2 + [jax.ShapeDtypeStruct((1, HEAD), F32)],
        compiler_params=_cp(("arbitrary", "arbitrary")),
    )(dcat, o, proj, w)


def _lanes_to_heads(a, H):
    return a[:, :H].T


def _heads_to_lanes(a):
    H = a.shape[0]
    return jnp.pad(a.T, ((0, 0), (0, 128 - H)))


def _take_cols(segs, a, b):
    out, off = [], 0
    for sg in segs:
        w = sg.shape[-1]
        lo, hi = max(a, off), min(b, off + w)
        if lo < hi:
            out.append(sg[..., lo - off:hi - off])
        off += w
    return out


def _pad_cols(pieces):
    m = jnp.concatenate(pieces, axis=-1)
    return jnp.pad(m, ((0, 0), (0, 128 - m.shape[-1])))


def _pad_lanes(v):
    return jnp.pad(v.reshape(1, -1), ((0, 0), (0, 128 - v.shape[-1])))


def _s5_layer_fwd(a, w, cfg):
    proj = _mm(a, w['w_in'], name="s5_in")
    wb, wc, _ = w['prep']
    pr, pi, prr, pir = w['tables']
    v, yg, hs, cin = _s5_scan_fwd(proj, wb, wc, pr, pi, w['d_skip'])
    z, mix = _mm(yg, w['w_glu'], name="s5_glu", extras=[(yg, 'ij'), (w['b_glu'], 'j')],
                 epi=lambda acc, y, b: (acc + b, y * _sigmoid(acc + b)), out_dtypes=(F32, F32))
    return proj, mix, dict(v=v, yg=yg, hs=hs, cin=cin, z=z)


def _s5_layer_bwd(a, w, proj, sv, dcat, dmemq, cfg):
    wb, wc, _ = w['prep']
    pr, pi, prr, pir = w['tables']
    dz, dyg1, db_glu = _s5_glu_bwd(dcat, sv['yg'], sv['z'])
    dw_glu = _mm(sv['yg'], dz, name="s5_dwglu", ta=True)
    dv = _mm(dz, w['w_glu'], name="s5_dyg", tb=True, extras=[(dyg1, 'ij'), (sv['v'], 'ij')],
             epi=lambda acc, d1, vv: ((acc + d1) * _gelu_grad(vv),))
    du, dwb, dwc, da, dd = _s5_scan_bwd(dv, proj, sv['hs'], sv['cin'], wb, wc, pr, pi, prr, pir, w['d_skip'])
    dproj = jnp.concatenate([du, dmemq], axis=1).astype(MXU_DTYPE)
    dw_in = _mm(a, dproj, name="s5_dwin", ta=True)
    da_in = _mm(dproj, w['w_in'], name="s5_da", tb=True)
    dlre, dlim, dldt, dbre, dbim, dcre, dcim = w['prep_vjp']((dwb, dwc, da))
    grads = dict(w_in=dw_in, w_glu=dw_glu, b_glu=db_glu[0], d_skip=dd[0], lam_re=dlre, lam_im=dlim,
                 log_dt=dldt, b_re=dbre, b_im=dbim, c_re=dcre, c_im=dcim)
    return da_in, grads


def _gdn_relayout(a, H, NC):
    t = _lanes_to_heads(a, H).reshape(H, NC, GDN_CHUNK)
    return t[..., None], t[:, :, None, :]


def _gdn_layer_fwd(a, w, cfg):
    H, MIX, S = cfg['H'], cfg['MIX'], a.shape[0]
    NC = S // GDN_CHUNK
    proj = _mm(a, w['w_main'], name="gdn_in")
    pg = _mm(a, w['w_gate'], name="gdn_in_gates")
    cw = w['conv_w']
    q = _gdn_prep(proj, 0, H, cw[:, :MIX], True, HEAD ** -0.5, "gdn_prep_q")
    k = _gdn_prep(proj, H, H, cw[:, MIX:2 * MIX], True, 1.0, "gdn_prep_k")
    v = _gdn_prep(proj, 2 * H, H, cw[:, 2 * MIX:], False, 1.0, "gdn_prep_v")
    gc, beta = _gdn_gates(pg, w['a_log'], w['dt_bias'])
    gcol, grow = _gdn_relayout(gc, H, NC)
    bcol, _ = _gdn_relayout(beta, H, NC)
    o, st = _gdn_chunk_fwd(q, k, v, gcol, grow, bcol)
    mix = _gdn_onorm(o, proj, 3 * H, w['o_norm'], H)
    return proj, mix, dict(pg=pg, q=q, k=k, v=v, gcol=gcol, grow=grow, bcol=bcol, o=o, st=st)


def _gdn_layer_bwd(a, w, proj, sv, dcat, dmemq, cfg):
    H, MIX, S = cfg['H'], cfg['MIX'], a.shape[0]
    cw = w['conv_w']
    do, dgate, donorm = _gdn_onorm_bwd(dcat, sv['o'], proj, 3 * H, w['o_norm'], H)
    dq, dk, dv, dgcol, dbcol = _gdn_chunk_bwd(sv['q'], sv['k'], sv['v'], sv['gcol'], sv['grow'], sv['bcol'],
                                              sv['st'], do)
    dgc = _heads_to_lanes(dgcol.reshape(H, S))
    dbeta = _heads_to_lanes(dbcol.reshape(H, S))
    dpg, dalog, ddtb = _gdn_gates_bwd(sv['pg'], w['a_log'], w['dt_bias'], dgc, dbeta)
    dxq, dwq = _gdn_prep_bwd(proj, 0, H, cw[:, :MIX], dq, True, HEAD ** -0.5, "gdn_prep_bwd_q")
    dxk, dwk = _gdn_prep_bwd(proj, H, H, cw[:, MIX:2 * MIX], dk, True, 1.0, "gdn_prep_bwd_k")
    dxv, dwv = _gdn_prep_bwd(proj, 2 * H, H, cw[:, 2 * MIX:], dv, False, 1.0, "gdn_prep_bwd_v")
    dproj = jnp.concatenate([dxq, dxk, dxv, dgate, dmemq], axis=1).astype(MXU_DTYPE)
    dw_main = _mm(a, dproj, name="gdn_dwmain", ta=True)
    dw_gate = _mm(a, dpg, name="gdn_dwgate", ta=True)
    da1 = _mm(dpg, w['w_gate'], name="gdn_da_gates", tb=True)
    da_in = _mm(dproj, w['w_main'], name="gdn_da", tb=True, extras=[(da1, 'ij')], epi=lambda acc, e: (acc + e,))
    grads = dict(w_main=dw_main, w_gate=dw_gate, conv_w=jnp.concatenate([dwq, dwk, dwv], axis=1),
                 a_log=dalog[0, :H], dt_bias=ddtb[0, :H], o_norm=donorm[0])
    return da_in, grads


def _fox_layer_fwd(a, w, cfg):
    H = cfg['H']
    proj = _mm(a, w['w_main'], name="fox_in")
    pg = _mm(a, w['w_gate'], name="fox_in_gates")
    cf = _fox_gates(pg, w['b_f'])
    cfh = _lanes_to_heads(cf, H)
    cfq, cfk = cfh[:, :, None], cfh[:, None, :]
    o, lse = _fox_fwd(proj, cfq, cfk, H)
    return proj, o, dict(pg=pg, cfq=cfq, cfk=cfk, lse=lse)


def _fox_layer_bwd(a, w, proj, sv, dcat, dmemq, cfg):
    H = cfg['H']
    rowdot = _fox_bwd_rowdot(proj, sv['cfq'], sv['cfk'], sv['lse'], dcat, H)
    dq, dk, dv, dck = _fox_bwd(proj, sv['cfq'], sv['cfk'], rowdot, sv['lse'], dcat, H)
    dpg, dbf = _fox_gates_bwd(sv['pg'], w['b_f'], _heads_to_lanes(dck[:, 0, :]))
    dproj = jnp.concatenate([dq, dk, dv, dmemq], axis=1).astype(MXU_DTYPE)
    dw_main = _mm(a, dproj, name="fox_dwmain", ta=True)
    dw_gate = _mm(a, dpg, name="fox_dwgate", ta=True)
    da1 = _mm(dpg, w['w_gate'], name="fox_da_gates", tb=True)
    da_in = _mm(dproj, w['w_main'], name="fox_da", tb=True, extras=[(da1, 'ij')], epi=lambda acc, e: (acc + e,))
    grads = dict(w_main=dw_main, w_gate=dw_gate, b_f=dbf[0, :H])
    return da_in, grads


_LAYER_FWD = (_s5_layer_fwd, _gdn_layer_fwd, _fox_layer_fwd)
_LAYER_BWD = (_s5_layer_bwd, _gdn_layer_bwd, _fox_layer_bwd)


def _mixer_weights(kind, j, fw, p, cfg, after):
    H, MIX, MW = cfg['H'], cfg['MIX'], cfg['MW']
    if kind == 0:
        params = tuple(p[n][j] for n in ('s5_lam_re', 's5_lam_im', 's5_log_dt', 's5_b_re', 's5_b_im',
                                         's5_c_re', 's5_c_im'))
        prep, prep_vjp = jax.vjp(_s5_prep, *params)
        prep = (prep[0].astype(MXU_DTYPE), prep[1].astype(MXU_DTYPE), prep[2])
        tables = _s5_tables(*params[:3])
        return dict(w_in=fw.get('s5_w_in', j, after), w_glu=fw.get('s5_w_glu', j, after),
                    b_glu=fw.get('s5_b_glu', j, after), d_skip=fw.get('s5_d_skip', j, after).reshape(1, MIX),
                    prep=prep, prep_vjp=prep_vjp, tables=tables)
    if kind == 1:
        segs = fw.get('gdn_w_in', j, after)
        c0 = 4 * MIX
        total = c0 + 2 * H + MW
        w_main = jnp.concatenate(_take_cols(segs, 0, c0) + _take_cols(segs, c0 + 2 * H, total), axis=1)
        w_gate = jnp.concatenate([_pad_cols(_take_cols(segs, c0, c0 + H)),
                                  _pad_cols(_take_cols(segs, c0 + H, c0 + 2 * H))], axis=1)
        return dict(w_main=w_main, w_gate=w_gate, conv_w=fw.get('gdn_conv_w', j, after),
                    a_log=_pad_lanes(p['gdn_a_log'][j]), dt_bias=_pad_lanes(p['gdn_dt_bias'][j]),
                    o_norm=p['gdn_o_norm'][j].reshape(1, HEAD))
    segs = fw.get('fox_w_in', j, after)
    c0 = 3 * MIX
    total = c0 + H + MW
    w_main = jnp.concatenate(_take_cols(segs, 0, c0) + _take_cols(segs, c0 + H, total), axis=1)
    w_gate = _pad_cols(_take_cols(segs, c0, c0 + H))
    return dict(w_main=w_main, w_gate=w_gate, b_f=_pad_lanes(p['fox_b_f'][j]))


class _Weights:
    def __init__(self, resolve):
        self._resolve, self._have = resolve, {}

    def get(self, name, layer, after):
        if (name, layer) not in self._have:
            self._have[name, layer] = self._resolve(name, layer, after)
        return self._have[name, layer]


def _local_step(p, fw, cfg, on_grad=None):
    H, MIX, MW, MH, depth = cfg['H'], cfg['MIX'], cfg['MW'], cfg['MH'], cfg['depth']
    x, mem, target = p['x'], p['mem'], p['loss_target']
    q_blk = {0: MIX // HEAD, 1: 4 * MIX // HEAD, 2: 3 * MIX // HEAD}
    zero = jnp.zeros((), F32)
    tok = [zero]

    def told(name, layer, value):
        if on_grad is not None:
            tok[0] = tok[0] + on_grad(name, layer, value)
        return value

    mem_n = _rms_fwd(mem, p['mem_norm'], MXU_DTYPE, "mem_rms")
    w_kv = fw.get('w_mem_kv', 0, mem_n)
    mkv = _mm(mem_n, w_kv, name="mem_kv")

    h = x
    saved = []
    for i in range(depth):
        kind, j = i % 3, i // 3
        a = _rms_fwd(h, p['norm1'][i], MXU_DTYPE, "rms1")
        w = _mixer_weights(kind, j, fw, p, cfg, a)
        proj, mix, sv = _LAYER_FWD[kind](a, w, cfg)
        read = _mem_fwd(proj, q_blk[kind], mkv, MH)
        cat = jnp.concatenate([mix, read], axis=1).astype(MXU_DTYPE)
        w_out, w_up = fw.get('w_out', i, proj), fw.get('w_up', i, proj)
        h1 = _mm(cat, w_out, name="out_proj", extras=[(h, 'ij')], epi=lambda acc, r: (acc + r,))
        a2 = _rms_fwd(h1, p['norm2'][i], MXU_DTYPE, "rms2")
        act = _mm(a2, w_up, name="mlp_up", epi=lambda acc: (_relu2(acc),), out_dtypes=(MXU_DTYPE,))
        w_down = fw.get('w_down', i, h1)
        h2 = _mm(act, w_down, name="mlp_down", extras=[(h1, 'ij')], epi=lambda acc, r: (acc + r,))
        saved.append(dict(w=w, h=h, a=a, proj=proj, sv=sv, cat=cat, h1=h1, a2=a2, act=act,
                          w_out=w_out, w_up=w_up, w_down=w_down))
        h = h2

    loss, dh, dnorm_f, dh16 = _loss_head(h, p['norm_f'], target)

    g = {n: None for n in WEIGHTS}
    g['norm_f'] = dnorm_f[0]
    per_layer = {n: [None] * depth for n in ('norm1', 'norm2', 'w_out', 'w_up', 'w_down')}
    mix_grads = {0: {}, 1: {}, 2: {}}
    big = {0: (('s5_w_in', 'w_in'), ('s5_w_glu', 'w_glu')), 1: (), 2: ()}
    dmkv = None
    for i in reversed(range(depth)):
        kind, j = i % 3, i // 3
        s = saved[i]
        w = s['w']
        du = _mm(dh16, s['w_down'], name="mlp_ddown", tb=True, extras=[(s['act'], 'ij')],
                 epi=lambda acc, aa: (acc * (2.0 * jnp.sqrt(aa.astype(F32))),), out_dtypes=(MXU_DTYPE,))
        per_layer['w_down'][i] = told('w_down', i, _mm(s['act'], dh16, name="mlp_dwdown", ta=True))
        per_layer['w_up'][i] = told('w_up', i, _mm(s['a2'], du, name="mlp_dwup", ta=True))
        da2 = _mm(du, s['w_up'], name="mlp_dup", tb=True)
        dh1, dn2, dh1_16 = _rms_bwd(s['h1'], p['norm2'][i] + tok[0], da2, dh, "rms2_bwd")
        per_layer['norm2'][i] = dn2[0]
        dcat = _mm(dh1_16, s['w_out'], name="out_dproj", tb=True)
        per_layer['w_out'][i] = told('w_out', i, _mm(s['cat'], dh1_16, name="out_dw", ta=True))
        dmemq, dmkv_i = _mem_bwd(s['proj'], q_blk[kind], mkv, dcat, MIX // HEAD, MH)
        dmkv = dmkv_i if dmkv is None else dmkv + dmkv_i
        da, mg = _LAYER_BWD[kind](s['a'], w, s['proj'], s['sv'], dcat, dmemq, cfg)
        mix_grads[kind][j] = mg
        for name, key in big[kind]:
            told(name, j, mg[key])
        c0 = 4 * MIX
        if kind == 1:
            mg['segs'] = told('gdn_w_in', j, [mg['w_main'][:, :c0], mg['w_gate'][:, :H],
                                               mg['w_gate'][:, 128:128 + H], mg['w_main'][:, c0:]])
        c0 = 3 * MIX
        if kind == 2:
            mg['segs'] = told('fox_w_in', j, [mg['w_main'][:, :c0], mg['w_gate'][:, :H], mg['w_main'][:, c0:]])
        dh, dn1, dh16 = _rms_bwd(s['h'], p['norm1'][i] + tok[0], da, dh1, "rms1_bwd")
        per_layer['norm1'][i] = dn1[0]
    for n in ('norm1', 'norm2'):
        g[n] = jnp.stack(per_layer[n])
    for n in ('w_out', 'w_up', 'w_down'):
        g[n] = per_layer[n]

    g['w_mem_kv'] = told('w_mem_kv', 0, _mm(mem_n, dmkv, name="mem_dwkv", ta=True))
    dmem_n = _mm(dmkv, w_kv, name="mem_dn", tb=True)
    _, dmn, _ = _rms_bwd(mem, p['mem_norm'] + tok[0], dmem_n, None, "mem_rms_bwd")
    g['mem_norm'] = dmn[0]

    def layers(kind, key):
        return [mix_grads[kind][j][key] for j in sorted(mix_grads[kind])]

    g['s5_w_in'] = layers(0, 'w_in')
    g['s5_w_glu'] = layers(0, 'w_glu')
    for n in ('b_glu', 'd_skip', 'lam_re', 'lam_im', 'log_dt', 'b_re', 'b_im', 'c_re', 'c_im'):
        g['s5_' + n] = jnp.stack(layers(0, n))
    g['gdn_w_in'] = layers(1, 'segs')
    for n in ('conv_w', 'a_log', 'dt_bias', 'o_norm'):
        g['gdn_' + n] = jnp.stack(layers(1, n))
    g['fox_w_in'] = layers(2, 'segs')
    g['fox_b_f'] = jnp.stack(layers(2, 'b_f'))
    return loss, dh, g


def _ag_order(p):
    depth, order = p['norm1'].shape[0], [('w_mem_kv', 0)]
    for i in range(depth):
        kind, j = i % 3, i // 3
        order += [[('s5_w_in', j), ('s5_w_glu', j)], [('gdn_w_in', j)], [('fox_w_in', j)]][kind]
        order += [('w_out', i), ('w_up', i), ('w_down', i)]
    return order


def _gather_begin(p, me):
    handles, tok = {}, jnp.zeros((), F32)
    for name, layer in _ag_order(p):
        xs = (p[name] if p[name].ndim == 2 else p[name][layer]).astype(MXU_DTYPE)
        handle, t = _ag_start(xs, "ag_start_%s_%d" % (name, layer))
        handles[name, layer] = (handle, xs)
        tok = tok + t
    vec = _gather(_pack_small([p[n] for n in VECTOR_SHARDED], 16), me, "all_gather_vectors").reshape(4, -1)
    vectors, off = {}, 0
    for n in VECTOR_SHARDED:
        sz = p[n].size
        stacked = vec[:, off:off + sz].reshape((4,) + p[n].shape)
        ax = SHARD_AXIS[n]
        t = jnp.moveaxis(stacked, 0, ax)
        shp = list(t.shape)
        vectors[n] = t.reshape(shp[:ax] + [shp[ax] * shp[ax + 1]] + shp[ax + 2:])
        off += sz
    return handles, vectors, tok


def _gather_end(name, layer, after, me_idx, handles, vectors):
    if name in vectors:
        return vectors[name][layer]
    handle, xs = handles[name, layer]
    tag = "%s_%d" % (name, layer)
    got = _ag_wait(handle, after, "ag_wait_" + tag)
    fwd = _ag_forward_start(got, "ag_forward_start_" + tag)
    got = _ag_fill_own(fwd[2], xs, me_idx, "ag_fill_own")
    got = _ag_forward_wait((fwd[0], fwd[1], got), "ag_forward_wait_" + tag)
    r, n = xs.shape
    if name == 'w_mem_kv' or SHARD_AXIS[name] == 1:
        return got.reshape(4 * r, n)
    blocks = [got[s] for s in range(4)]
    return jnp.concatenate(blocks, axis=1) if name == 'w_up' else blocks


def _shard_blocks(name, value, p):
    shp = p[name].shape
    r, n = shp[-2], shp[-1]
    if SHARD_AXIS[name] == len(shp) - 2:
        return value.reshape(4, r, n)
    segs = value if isinstance(value, list) else [value]
    return jnp.stack([jnp.concatenate(_take_cols(segs, s * n, (s + 1) * n), axis=1) for s in range(4)])


def kernel(x, mem, mem_norm, w_mem_kv, norm1, w_out, norm2, w_up, w_down, norm_f, s5_w_in, s5_lam_re, s5_lam_im, s5_log_dt, s5_b_re, s5_b_im, s5_c_re, s5_c_im, s5_d_skip, s5_w_glu, s5_b_glu, gdn_w_in, gdn_conv_w, gdn_a_log, gdn_dt_bias, gdn_o_norm, fox_w_in, fox_b_f, loss_target, m_mem_norm, m_w_mem_kv, m_norm1, m_w_out, m_norm2, m_w_up, m_w_down, m_norm_f, m_s5_w_in, m_s5_lam_re, m_s5_lam_im, m_s5_log_dt, m_s5_b_re, m_s5_b_im, m_s5_c_re, m_s5_c_im, m_s5_d_skip, m_s5_w_glu, m_s5_b_glu, m_gdn_w_in, m_gdn_conv_w, m_gdn_a_log, m_gdn_dt_bias, m_gdn_o_norm, m_fox_w_in, m_fox_b_f, v_mem_norm, v_w_mem_kv, v_norm1, v_w_out, v_norm2, v_w_up, v_w_down, v_norm_f, v_s5_w_in, v_s5_lam_re, v_s5_lam_im, v_s5_log_dt, v_s5_b_re, v_s5_b_im, v_s5_c_re, v_s5_c_im, v_s5_d_skip, v_s5_w_glu, v_s5_b_glu, v_gdn_w_in, v_gdn_conv_w, v_gdn_a_log, v_gdn_dt_bias, v_gdn_o_norm, v_fox_w_in, v_fox_b_f):
    args = locals()
    p = {n: args[n] for n in WEIGHTS}
    mom = {n: args['m_' + n] for n in WEIGHTS}
    var = {n: args['v_' + n] for n in WEIGHTS}
    S, D = x.shape[1], x.shape[2]
    MW = w_mem_kv.shape[1] // 2
    MIX = D - MW
    cfg = dict(H=MIX // HEAD, MIX=MIX, MW=MW, MH=MW // HEAD, depth=norm1.shape[0])
    p.update(x=x.reshape(S, D), mem=mem.reshape(mem.shape[1], D), loss_target=loss_target.reshape(S, D))
    c = lax.axis_index("c")
    me = 2 * lax.axis_index("x") + lax.axis_index("y")
    place = dict(c=c, c_idx=c.astype(jnp.int32).reshape(1), me_idx=me.astype(jnp.int32).reshape(1))

    handles, vectors, tok = _gather_begin(p, me)
    p['mem_norm'] = mem_norm + tok
    fw = _Weights(lambda name, layer, after: _gather_end(name, layer, after, place['me_idx'], handles, vectors))

    pending, swapping = {}, []

    def exchange(after):
        name, layer, swap = swapping.pop()
        pending[name, layer], t = _rs_begin(swap, after, place, "%s_%d" % (name, layer))
        return t

    def on_grad(name, layer, value):
        swap, t = _rs_swap_start(_shard_blocks(name, value, p), "rs_swap_start_%s_%d" % (name, layer))
        if swapping:
            t = t + exchange(value[0] if isinstance(value, list) else value)
        swapping.append((name, layer, swap))
        return t

    loss, dx, g = _local_step(p, fw, cfg, on_grad)
    p['mem_norm'] = mem_norm
    exchange(dx)

    grads = {}
    for name in MATMUL_WEIGHTS:
        shp = p[name].shape
        layers = [_rs_end(pending[name, i], dx, place, "%s_%d" % (name, i))
                  for i in range(1 if len(shp) == 2 else shp[0])]
        grads[name] = layers[0] if len(shp) == 2 else jnp.stack(layers)

    parts = []
    for name in VECTOR_SHARDED:
        ax = SHARD_AXIS[name]
        shp = list(g[name].shape)
        t = g[name].reshape(shp[:ax] + [4, shp[ax] // 4] + shp[ax + 1:])
        parts.append(jnp.moveaxis(t, ax, 0).reshape(4, -1))
    flat = jnp.concatenate(parts, axis=1)
    flat = jnp.pad(flat, ((0, 0), (0, 16 * LANES - flat.shape[1]))).reshape(4, 16, LANES)
    swap, _ = _rs_swap_start(flat, "rs_swap_start_vectors")
    handle, _ = _rs_begin(swap, dx, place, "vectors")
    red = _rs_end(handle, dx, place, "vectors").reshape(-1)
    off = 0
    for name in VECTOR_SHARDED:
        grads[name] = red[off:off + p[name].size].reshape(p[name].shape)
        off += p[name].size

    n_small = sum(p[n].size for n in REPLICATED)
    rows = -(-n_small // LANES // 8) * 8
    small = _all_reduce_small(_pack_small([g[n] for n in REPLICATED], rows), "all_reduce_small").reshape(-1)
    off = 0
    for n in REPLICATED:
        grads[n] = small[off:off + p[n].size].reshape(p[n].shape)
        off += p[n].size

    delta, new_m, new_v = {}, {}, {}
    for n in SHARD_AXIS:
        shp = p[n].shape
        two_d = (-1, shp[-1])
        d, nm, nv = _adamw(p[n].reshape(two_d), grads[n].reshape(two_d), mom[n].reshape(two_d),
                           var[n].reshape(two_d), "adamw_" + n)
        delta[n], new_m[n], new_v[n] = d.reshape(shp), nm.reshape(shp), nv.reshape(shp)
    d, nm, nv = _adamw(*[_pack_small([src[n] for n in REPLICATED], rows) for src in (p, grads, mom, var)],
                       "adamw_small")
    d, nm, nv = d.reshape(-1), nm.reshape(-1), nv.reshape(-1)
    off = 0
    for n in REPLICATED:
        sz, shp = p[n].size, p[n].shape
        delta[n], new_m[n], new_v[n] = (d[off:off + sz].reshape(shp), nm[off:off + sz].reshape(shp),
                                        nv[off:off + sz].reshape(shp))
        off += sz

    total = lax.psum(loss[0, 0], ("x", "y", "c"))
    return (total, dx.reshape(x.shape), *[grads[n] for n in WEIGHTS], *[delta[n] for n in WEIGHTS],
            *[new_m[n] for n in WEIGHTS], *[new_v[n] for n in WEIGHTS])
```

```python
import math

import jax
import jax.numpy as jnp
import numpy as np
from jax import lax
from jax.experimental import pallas as pl
from jax.experimental.pallas import tpu as pltpu

F32 = jnp.float32
MXU_DTYPE = jnp.bfloat16
EPS = 1e-6
HEAD = 128
S5_GROUP = 16
S5_STATE = 64
S5_SLAB = 256
S5_CHUNK = 128
S5_ROWS = 8
GDN_CHUNK = 64
GDN_CONV = 4
LANES = 1024
VMEM_LIMIT_BYTES = 56 * 1024 * 1024
MESH = pl.DeviceIdType.MESH
RS_PAYLOAD = jnp.bfloat16
HBM_SPEC = pl.BlockSpec(memory_space=pltpu.HBM)
SEM_SPEC = pl.BlockSpec(memory_space=pltpu.SEMAPHORE)
SPLIT_EFFECT = pltpu.SideEffectType.DATAFLOW_SIDE_EFFECTING

ADAM_LR, ADAM_B1, ADAM_B2, ADAM_EPS, ADAM_WD, ADAM_STEP = 0.001, 0.9, 0.999, 1e-08, 0.01, 10

MM_TM, MM_TN, MM_TK = 1024, 1024, 1024
ROW_TILE = 256
FOX_TILE = 512
MEM_TILE = 512
CONV_TILE = 1024

NN = (((1,), (0,)), ((), ()))
NT = (((1,), (1,)), ((), ()))
TN = (((0,), (0,)), ((), ()))

WEIGHTS = ['mem_norm', 'w_mem_kv', 'norm1', 'w_out', 'norm2', 'w_up', 'w_down', 'norm_f', 's5_w_in',
           's5_lam_re', 's5_lam_im', 's5_log_dt', 's5_b_re', 's5_b_im', 's5_c_re', 's5_c_im', 's5_d_skip',
           's5_w_glu', 's5_b_glu', 'gdn_w_in', 'gdn_conv_w', 'gdn_a_log', 'gdn_dt_bias', 'gdn_o_norm',
           'fox_w_in', 'fox_b_f']
SHARD_AXIS = {'w_mem_kv': 0, 'w_out': 1, 'w_up': 2, 'w_down': 1, 's5_w_in': 1, 's5_d_skip': 1,
              's5_w_glu': 1, 's5_b_glu': 1, 'gdn_w_in': 2, 'gdn_conv_w': 2, 'fox_w_in': 2}
MATMUL_WEIGHTS = ['w_mem_kv', 'w_out', 'w_up', 'w_down', 's5_w_in', 's5_w_glu', 'gdn_w_in', 'fox_w_in']
VECTOR_SHARDED = ['s5_d_skip', 's5_b_glu', 'gdn_conv_w']
REPLICATED = [n for n in WEIGHTS if n not in SHARD_AXIS]


def _tile(dim, target, align=128):
    if dim <= target:
        return dim
    t = (target // align) * align
    while t >= align:
        if dim % t == 0:
            return t
        t -= align
    return dim


def _cp(sem=None, **kw):
    return pltpu.CompilerParams(dimension_semantics=sem, vmem_limit_bytes=VMEM_LIMIT_BYTES, **kw)


def _dot(a, b, dims):
    return lax.dot_general(a.astype(MXU_DTYPE), b.astype(MXU_DTYPE), dims, preferred_element_type=F32)


def _dotf(a, b, dims):
    return lax.dot_general(a, b, dims, precision=lax.Precision.HIGHEST, preferred_element_type=F32)


def _sigmoid(x):
    return 1.0 / (1.0 + jnp.exp(-x))


def _softplus(x):
    return jnp.maximum(x, 0.0) + jnp.log(1.0 + jnp.exp(-jnp.abs(x)))


def _relu2(x):
    r = jnp.maximum(x, 0.0)
    return r * r


_GELU_C = math.sqrt(2.0 / math.pi)


def _gelu(x):
    return 0.5 * x * (1.0 + jnp.tanh(_GELU_C * (x + 0.044715 * x * x * x)))


def _gelu_grad(x):
    t = jnp.tanh(_GELU_C * (x + 0.044715 * x * x * x))
    return 0.5 * (1.0 + t) + 0.5 * x * (1.0 - t * t) * _GELU_C * (1.0 + 3.0 * 0.044715 * x * x)


def _silu_grad(x):
    s = _sigmoid(x)
    return s + x * s * (1.0 - s)


def _mm(a, b, *, name, ta=False, tb=False, extras=(), epi=None, out_dtypes=(F32,)):
    K, M = a.shape if ta else a.shape[::-1]
    N = b.shape[0] if tb else b.shape[1]
    assert (b.shape[1] if tb else b.shape[0]) == K, (a.shape, b.shape, ta, tb)
    tm, tn, tk = _tile(M, MM_TM), _tile(N, MM_TN), _tile(K, MM_TK)
    nk = K // tk
    n_ex, n_out = len(extras), len(out_dtypes)
    dims = TN if ta else (NT if tb else NN)

    def body(*refs):
        a_ref, b_ref = refs[0], refs[1]
        ex = refs[2:2 + n_ex]
        outs = refs[2 + n_ex:2 + n_ex + n_out]
        acc = refs[-1]
        k = pl.program_id(2)

        @pl.when(k == 0)
        def _():
            acc[...] = jnp.zeros_like(acc)

        acc[...] += _dot(a_ref[...], b_ref[...], dims)

        @pl.when(k == nk - 1)
        def _():
            res = acc[...]
            vals = epi(res, *[e[...] for e in ex]) if epi is not None else (res,)
            for o, v in zip(outs, vals):
                o[...] = v.astype(o.dtype)

    if ta:
        a_spec = pl.BlockSpec((tk, tm), lambda i, j, k: (k, i))
    else:
        a_spec = pl.BlockSpec((tm, tk), lambda i, j, k: (i, k))
    if tb:
        b_spec = pl.BlockSpec((tn, tk), lambda i, j, k: (j, k))
    else:
        b_spec = pl.BlockSpec((tk, tn), lambda i, j, k: (k, j))
    ex_specs, ex_arrays = [], []
    for arr, kind in extras:
        if kind == 'ij':
            ex_specs.append(pl.BlockSpec((tm, tn), lambda i, j, k: (i, j)))
            ex_arrays.append(arr)
        else:
            ex_specs.append(pl.BlockSpec((1, tn), lambda i, j, k: (0, j)))
            ex_arrays.append(arr.reshape(1, N))
    outs = pl.pallas_call(
        body, name=name, grid=(M // tm, N // tn, nk),
        in_specs=[a_spec, b_spec] + ex_specs,
        out_specs=[pl.BlockSpec((tm, tn), lambda i, j, k: (i, j)) for _ in out_dtypes],
        out_shape=[jax.ShapeDtypeStruct((M, N), dt) for dt in out_dtypes],
        scratch_shapes=[pltpu.VMEM((tm, tn), F32)],
        compiler_params=_cp(("parallel", "parallel", "arbitrary")),
    )(a, b, *ex_arrays)
    return outs[0] if n_out == 1 else tuple(outs)


def _rms_fwd(x, g, out_dtype, name):
    S, D = x.shape
    tr = _tile(S, ROW_TILE, 8)

    def body(x_ref, g_ref, o_ref):
        xv = x_ref[...]
        r = lax.rsqrt(jnp.mean(xv * xv, axis=-1, keepdims=True) + EPS)
        o_ref[...] = (xv * r * g_ref[...]).astype(o_ref.dtype)

    return pl.pallas_call(
        body, name=name, grid=(S // tr,),
        in_specs=[pl.BlockSpec((tr, D), lambda i: (i, 0)), pl.BlockSpec((1, D), lambda i: (0, 0))],
        out_specs=pl.BlockSpec((tr, D), lambda i: (i, 0)),
        out_shape=jax.ShapeDtypeStruct((S, D), out_dtype),
        compiler_params=_cp(("parallel",)),
    )(x, g.reshape(1, D))


def _rms_bwd(x, g, dy, res, name):
    S, D = x.shape
    tr = _tile(S, ROW_TILE, 8)
    has_res = res is not None

    def body(*refs):
        if has_res:
            x_ref, g_ref, dy_ref, res_ref, dx_ref, dg_ref, dx16_ref = refs
        else:
            x_ref, g_ref, dy_ref, dx_ref, dg_ref, dx16_ref = refs
        i = pl.program_id(0)

        @pl.when(i == 0)
        def _():
            dg_ref[...] = jnp.zeros_like(dg_ref)

        xv, d = x_ref[...], dy_ref[...].astype(F32)
        r = lax.rsqrt(jnp.mean(xv * xv, axis=-1, keepdims=True) + EPS)
        xh = xv * r
        t = d * g_ref[...]
        dx = r * (t - xh * jnp.mean(t * xh, axis=-1, keepdims=True))
        if has_res:
            dx = dx + res_ref[...]
        dx_ref[...] = dx
        dx16_ref[...] = dx.astype(dx16_ref.dtype)
        dg_ref[...] += jnp.sum(d * xh, axis=0, keepdims=True)

    row = pl.BlockSpec((tr, D), lambda i: (i, 0))
    vec = pl.BlockSpec((1, D), lambda i: (0, 0))
    ins = [x, g.reshape(1, D), dy] + ([res] if has_res else [])
    return pl.pallas_call(
        body, name=name, grid=(S // tr,),
        in_specs=[row, vec, row] + ([row] if has_res else []),
        out_specs=[row, vec, row],
        out_shape=[jax.ShapeDtypeStruct((S, D), F32), jax.ShapeDtypeStruct((1, D), F32),
                   jax.ShapeDtypeStruct((S, D), MXU_DTYPE)],
        compiler_params=_cp(("arbitrary",)),
    )(*ins)


def _loss_head(h, g, target):
    S, D = h.shape
    tr = _tile(S, ROW_TILE, 8)

    def body(h_ref, g_ref, t_ref, loss_ref, dh_ref, dg_ref, dh16_ref):
        i = pl.program_id(0)

        @pl.when(i == 0)
        def _():
            loss_ref[...] = jnp.zeros_like(loss_ref)
            dg_ref[...] = jnp.zeros_like(dg_ref)

        xv = h_ref[...]
        gv = g_ref[...]
        r = lax.rsqrt(jnp.mean(xv * xv, axis=-1, keepdims=True) + EPS)
        xh = xv * r
        err = xh * gv - t_ref[...]
        part = 0.5 * jnp.sum(jnp.mean(err * err, axis=-1, keepdims=True), axis=0, keepdims=True)
        loss_ref[...] += jnp.broadcast_to(part, loss_ref.shape)
        d = err * (1.0 / D)
        t = d * gv
        dh = r * (t - xh * jnp.mean(t * xh, axis=-1, keepdims=True))
        dh_ref[...] = dh
        dh16_ref[...] = dh.astype(dh16_ref.dtype)
        dg_ref[...] += jnp.sum(d * xh, axis=0, keepdims=True)

    row = pl.BlockSpec((tr, D), lambda i: (i, 0))
    vec = pl.BlockSpec((1, D), lambda i: (0, 0))
    return pl.pallas_call(
        body, name="loss_head", grid=(S // tr,),
        in_specs=[row, vec, row],
        out_specs=[pl.BlockSpec((8, 128), lambda i: (0, 0)), row, vec, row],
        out_shape=[jax.ShapeDtypeStruct((8, 128), F32), jax.ShapeDtypeStruct((S, D), F32),
                   jax.ShapeDtypeStruct((1, D), F32), jax.ShapeDtypeStruct((S, D), MXU_DTYPE)],
        compiler_params=_cp(("arbitrary",)),
    )(h, g.reshape(1, D), target)


def _adamw(w, g, m, v, name):
    R, C = w.shape
    tr = _tile(R, max(8, (1 << 19) // max(C, 1) // 8 * 8), 8)
    c1 = 1.0 / (1.0 - ADAM_B1 ** ADAM_STEP)
    c2 = 1.0 / (1.0 - ADAM_B2 ** ADAM_STEP)

    def body(w_ref, g_ref, m_ref, v_ref, d_ref, nm_ref, nv_ref):
        gv = g_ref[...]
        nm = ADAM_B1 * m_ref[...] + (1.0 - ADAM_B1) * gv
        nv = ADAM_B2 * v_ref[...] + (1.0 - ADAM_B2) * (gv * gv)
        d_ref[...] = -ADAM_LR * ((nm * c1) / (jnp.sqrt(nv * c2) + ADAM_EPS) + ADAM_WD * w_ref[...])
        nm_ref[...] = nm
        nv_ref[...] = nv

    blk = pl.BlockSpec((tr, C), lambda i: (i, 0))
    return pl.pallas_call(
        body, name=name, grid=(R // tr,),
        in_specs=[blk] * 4, out_specs=[blk] * 3,
        out_shape=[jax.ShapeDtypeStruct((R, C), F32)] * 3,
        compiler_params=_cp(("parallel",)),
    )(w, g, m, v)


def _place():
    x, y, c = lax.axis_index("x"), lax.axis_index("y"), lax.axis_index("c")
    chips = [(1 - x, y), (x, 1 - y), (1 - x, 1 - y)]
    return x, y, c, chips


def _in_hbm(a):
    return pltpu.with_memory_space_constraint(a, pltpu.HBM)


def _ag_start(xs, name):
    r, n = xs.shape
    half = r // 2

    def body(x_ref, land_ref, send_sems, recv_sems, x_thru, land_thru, token):
        x, y, c, chips = _place()
        rows = pl.ds(c * half, half)
        for j, (cx, cy) in enumerate(chips):
            pltpu.make_async_remote_copy(
                src_ref=x_ref.at[rows, :], dst_ref=land_ref.at[2 * x + y, rows, :], send_sem=send_sems.at[j],
                recv_sem=recv_sems.at[j], device_id=(cx, cy, c), device_id_type=MESH).start()
        token[...] = jnp.zeros_like(token)

    sems = pltpu.SemaphoreType.DMA((3,))
    out = pl.pallas_call(
        body, name=name,
        out_shape=(sems, sems, pltpu.HBM(xs.shape, xs.dtype), pltpu.HBM((4, r, n), xs.dtype),
                   jax.ShapeDtypeStruct((8, 128), F32)),
        in_specs=(HBM_SPEC, HBM_SPEC),
        out_specs=(SEM_SPEC, SEM_SPEC, HBM_SPEC, HBM_SPEC, pl.BlockSpec(memory_space=pltpu.VMEM)),
        input_output_aliases={0: 2, 1: 3},
        compiler_params=pltpu.CompilerParams(has_side_effects=SPLIT_EFFECT),
    )(_in_hbm(xs), _in_hbm(lax.empty((4, r, n), xs.dtype)))
    return out[:4], out[4][0, 0]


def _ag_wait(handle, after, name):
    send_sems, recv_sems, xs, land = handle
    r, n = xs.shape
    half = r // 2

    def body(x_ref, land_ref, send_sems, recv_sems, after_ref, x_out, land_out):
        x, y, c, chips = _place()
        rows = pl.ds(c * half, half)
        for j, (cx, cy) in enumerate(chips):
            cp = pltpu.make_async_remote_copy(
                src_ref=x_ref.at[rows, :], dst_ref=land_ref.at[2 * cx + cy, rows, :], send_sem=send_sems.at[j],
                recv_sem=recv_sems.at[j], device_id=(cx, cy, c), device_id_type=MESH)
            cp.wait_send()
            cp.wait_recv()

    return pl.pallas_call(
        body, name=name,
        out_shape=(pltpu.HBM(xs.shape, xs.dtype), pltpu.HBM(land.shape, land.dtype)),
        in_specs=(HBM_SPEC, HBM_SPEC, SEM_SPEC, SEM_SPEC, pl.BlockSpec(memory_space=pl.ANY)),
        out_specs=(HBM_SPEC, HBM_SPEC),
        input_output_aliases={0: 0, 1: 1},
        compiler_params=pltpu.CompilerParams(has_side_effects=SPLIT_EFFECT),
    )(xs, land, send_sems, recv_sems, after)[1]


def _ag_forward_start(got, name):
    _, r, n = got.shape
    half = r // 2

    def body(g_ref, send_sems, recv_sems, g_thru):
        x, y, c, chips = _place()
        for j, (cx, cy) in enumerate(chips):
            piece = g_ref.at[2 * cx + cy, pl.ds(c * half, half), :]
            pltpu.make_async_remote_copy(src_ref=piece, dst_ref=piece, send_sem=send_sems.at[j],
                                         recv_sem=recv_sems.at[j], device_id=(x, y, 1 - c),
                                         device_id_type=MESH).start()

    sems = pltpu.SemaphoreType.DMA((3,))
    return pl.pallas_call(
        body, name=name,
        out_shape=(sems, sems, pltpu.HBM(got.shape, got.dtype)),
        in_specs=(HBM_SPEC,),
        out_specs=(SEM_SPEC, SEM_SPEC, HBM_SPEC),
        input_output_aliases={0: 2},
        compiler_params=pltpu.CompilerParams(has_side_effects=SPLIT_EFFECT),
    )(got)


def _ag_forward_wait(handle, name):
    send_sems, recv_sems, got = handle
    _, r, n = got.shape
    half = r // 2

    def body(g_ref, send_sems, recv_sems, g_out):
        x, y, c, chips = _place()
        for j, (cx, cy) in enumerate(chips):
            sent = g_ref.at[2 * cx + cy, pl.ds(c * half, half), :]
            pltpu.make_async_remote_copy(src_ref=sent, dst_ref=sent, send_sem=send_sems.at[j],
                                         recv_sem=recv_sems.at[j], device_id=(x, y, 1 - c),
                                         device_id_type=MESH).wait_send()
            came = g_ref.at[2 * cx + cy, pl.ds((1 - c) * half, half), :]
            pltpu.make_async_remote_copy(src_ref=came, dst_ref=came, send_sem=send_sems.at[j],
                                         recv_sem=recv_sems.at[j], device_id=(x, y, 1 - c),
                                         device_id_type=MESH).wait_recv()

    return pl.pallas_call(
        body, name=name,
        out_shape=pltpu.HBM(got.shape, got.dtype),
        in_specs=(HBM_SPEC, SEM_SPEC, SEM_SPEC),
        out_specs=HBM_SPEC,
        input_output_aliases={0: 0},
        compiler_params=pltpu.CompilerParams(has_side_effects=SPLIT_EFFECT),
    )(got, send_sems, recv_sems)


def _ag_fill_own(got, xs, me_idx, name):
    r, n = xs.shape
    tr = _tile(r, max(16, (1 << 20) // n // 16 * 16), 16)

    def body(me_ref, x_ref, g_ref, out_ref):
        out_ref[...] = x_ref[...]

    return pl.pallas_call(
        body, name=name,
        grid_spec=pltpu.PrefetchScalarGridSpec(
            num_scalar_prefetch=1, grid=(r // tr,),
            in_specs=[pl.BlockSpec((tr, n), lambda i, me: (i, 0)), pl.BlockSpec(memory_space=pl.ANY)],
            out_specs=pl.BlockSpec((None, tr, n), lambda i, me: (me[0], i, 0))),
        out_shape=jax.ShapeDtypeStruct(got.shape, got.dtype),
        input_output_aliases={2: 0},
        compiler_params=_cp(("parallel",)),
    )(me_idx, xs, got)


def _all_gather_chips(xs, name):
    r, n = xs.shape
    half = r // 2

    def body(x_ref, out_ref, send_sems, recv_sems):
        x, y, c, chips = _place()
        me = 2 * x + y
        sibling = (x, y, 1 - c)

        def piece(chip, hc):
            return out_ref.at[chip, pl.ds(hc * half, half), :]

        def copy(k, src, dst, to):
            return pltpu.make_async_remote_copy(src_ref=src, dst_ref=dst, send_sem=send_sems.at[k],
                                                recv_sem=recv_sems.at[k], device_id=to, device_id_type=MESH)

        src = x_ref.at[pl.ds(c * half, half), :]
        first = [copy(j, src, piece(me, c), (cx, cy, c)) for j, (cx, cy) in enumerate(chips)]
        for cp in first:
            cp.start()
        passed = []
        for j, (cx, cy) in enumerate(chips):
            got = piece(2 * cx + cy, c)
            copy(j, got, got, (cx, cy, c)).wait_recv()
            fwd = copy(3 + j, got, got, sibling)
            fwd.start()
            passed.append(fwd)
        for j, (cx, cy) in enumerate(chips):
            got = piece(2 * cx + cy, 1 - c)
            copy(3 + j, got, got, sibling).wait_recv()
        for cp in first + passed:
            cp.wait_send()

    return pl.pallas_call(
        body, name=name,
        in_specs=[pl.BlockSpec(memory_space=pl.ANY)],
        out_specs=pl.BlockSpec(memory_space=pl.ANY),
        out_shape=jax.ShapeDtypeStruct((4, r, n), xs.dtype),
        scratch_shapes=[pltpu.SemaphoreType.DMA((6,)), pltpu.SemaphoreType.DMA((6,))],
    )(xs)


def _gather(xs, me, name):
    return lax.dynamic_update_slice(_all_gather_chips(xs, name), xs[None], (me, 0, 0))


def _rs_swap_start(g, name):
    _, r, n = g.shape
    half = r // 2

    def body(g_ref, land_ref, send_sem, recv_sem, g_thru, land_thru, token):
        x, y, c, _ = _place()
        pltpu.make_async_remote_copy(
            src_ref=g_ref.at[:, pl.ds((1 - c) * half, half), :], dst_ref=land_ref,
            send_sem=send_sem, recv_sem=recv_sem, device_id=(x, y, 1 - c), device_id_type=MESH).start()
        token[...] = jnp.zeros_like(token)

    sem = pltpu.SemaphoreType.DMA(())
    out = pl.pallas_call(
        body, name=name,
        out_shape=(sem, sem, pltpu.HBM(g.shape, g.dtype), pltpu.HBM((4, half, n), g.dtype),
                   jax.ShapeDtypeStruct((8, 128), F32)),
        in_specs=(HBM_SPEC, HBM_SPEC),
        out_specs=(SEM_SPEC, SEM_SPEC, HBM_SPEC, HBM_SPEC, pl.BlockSpec(memory_space=pltpu.VMEM)),
        input_output_aliases={0: 2, 1: 3},
        compiler_params=pltpu.CompilerParams(has_side_effects=SPLIT_EFFECT),
    )(_in_hbm(g), _in_hbm(lax.empty((4, half, n), g.dtype)))
    return out[:4], out[4][0, 0]


def _rs_swap_wait(handle, after, name):
    send_sem, recv_sem, g, land = handle
    half = land.shape[1]

    def body(g_ref, land_ref, send_sem, recv_sem, after_ref, g_out, land_out):
        x, y, c, _ = _place()
        cp = pltpu.make_async_remote_copy(
            src_ref=g_ref.at[:, pl.ds((1 - c) * half, half), :], dst_ref=land_ref,
            send_sem=send_sem, recv_sem=recv_sem, device_id=(x, y, 1 - c), device_id_type=MESH)
        cp.wait_send()
        cp.wait_recv()

    return pl.pallas_call(
        body, name=name,
        out_shape=(pltpu.HBM(g.shape, g.dtype), pltpu.HBM(land.shape, land.dtype)),
        in_specs=(HBM_SPEC, HBM_SPEC, SEM_SPEC, SEM_SPEC, pl.BlockSpec(memory_space=pl.ANY)),
        out_specs=(HBM_SPEC, HBM_SPEC),
        input_output_aliases={0: 0, 1: 1},
        compiler_params=pltpu.CompilerParams(has_side_effects=SPLIT_EFFECT),
    )(g, land, send_sem, recv_sem, after)


def _rs_add_halves(g, got, c_idx, name):
    _, r, n = g.shape
    half = r // 2
    tr = _tile(half, max(16, (1 << 19) // n // 16 * 16), 16)
    nb = half // tr

    def body(c_ref, g_ref, o_ref, out_ref, out16_ref):
        sm = g_ref[...] + o_ref[...]
        out_ref[...] = sm
        out16_ref[...] = sm.astype(out16_ref.dtype)

    blk = pl.BlockSpec((None, tr, n), lambda s, i, c: (s, i, 0))
    return pl.pallas_call(
        body, name=name,
        grid_spec=pltpu.PrefetchScalarGridSpec(
            num_scalar_prefetch=1, grid=(4, nb),
            in_specs=[pl.BlockSpec((None, tr, n), lambda s, i, c: (s, c[0] * nb + i, 0)), blk],
            out_specs=[blk, blk]),
        out_shape=[jax.ShapeDtypeStruct((4, half, n), F32), jax.ShapeDtypeStruct((4, half, n), RS_PAYLOAD)],
        compiler_params=_cp(("parallel", "parallel")),
    )(c_idx, g, got)


def _rs_start(p32, p16, name):
    _, h, n = p16.shape

    def body(p32_ref, p16_ref, l16_ref, l32_ref, send_sems, recv_sems, p32_t, p16_t, l16_t, l32_t, token):
        x, y, c, chips = _place()
        for j, (cx, cy) in enumerate(chips):
            for k, pc in enumerate((c, 1 - c)):
                pltpu.make_async_remote_copy(
                    src_ref=p16_ref.at[2 * cx + cy], dst_ref=l16_ref.at[c, j], send_sem=send_sems.at[3 * k + j],
                    recv_sem=recv_sems.at[3 * k + j], device_id=(cx, cy, pc), device_id_type=MESH).start()
        pltpu.make_async_remote_copy(
            src_ref=p32_ref.at[2 * x + y], dst_ref=l32_ref, send_sem=send_sems.at[6], recv_sem=recv_sems.at[6],
            device_id=(x, y, 1 - c), device_id_type=MESH).start()
        token[...] = jnp.zeros_like(token)

    sems = pltpu.SemaphoreType.DMA((7,))
    l16 = lax.empty((2, 3, h, n), p16.dtype)
    l32 = lax.empty((h, n), F32)
    out = pl.pallas_call(
        body, name=name,
        out_shape=(sems, sems, pltpu.HBM(p32.shape, F32), pltpu.HBM(p16.shape, p16.dtype),
                   pltpu.HBM(l16.shape, l16.dtype), pltpu.HBM(l32.shape, F32), jax.ShapeDtypeStruct((8, 128), F32)),
        in_specs=(HBM_SPEC,) * 4,
        out_specs=(SEM_SPEC, SEM_SPEC) + (HBM_SPEC,) * 4 + (pl.BlockSpec(memory_space=pltpu.VMEM),),
        input_output_aliases={0: 2, 1: 3, 2: 4, 3: 5},
        compiler_params=pltpu.CompilerParams(has_side_effects=SPLIT_EFFECT),
    )(_in_hbm(p32), _in_hbm(p16), _in_hbm(l16), _in_hbm(l32))
    return out[:6], out[6][0, 0]


def _rs_wait(handle, after, name):
    send_sems, recv_sems, p32, p16, l16, l32 = handle

    def body(p32_ref, p16_ref, l16_ref, l32_ref, send_sems, recv_sems, after_ref, p32_o, p16_o, l16_o, l32_o):
        x, y, c, chips = _place()
        for j, (cx, cy) in enumerate(chips):
            for k, pc in enumerate((c, 1 - c)):
                cp = pltpu.make_async_remote_copy(
                    src_ref=p16_ref.at[2 * cx + cy], dst_ref=l16_ref.at[pc, j], send_sem=send_sems.at[3 * k + j],
                    recv_sem=recv_sems.at[3 * k + j], device_id=(cx, cy, pc), device_id_type=MESH)
                cp.wait_send()
                cp.wait_recv()
        cp = pltpu.make_async_remote_copy(
            src_ref=p32_ref.at[2 * x + y], dst_ref=l32_ref, send_sem=send_sems.at[6], recv_sem=recv_sems.at[6],
            device_id=(x, y, 1 - c), device_id_type=MESH)
        cp.wait_send()
        cp.wait_recv()

    out = pl.pallas_call(
        body, name=name,
        out_shape=(pltpu.HBM(p32.shape, F32), pltpu.HBM(p16.shape, p16.dtype), pltpu.HBM(l16.shape, l16.dtype),
                   pltpu.HBM(l32.shape, F32)),
        in_specs=(HBM_SPEC,) * 4 + (SEM_SPEC, SEM_SPEC, pl.BlockSpec(memory_space=pl.ANY)),
        out_specs=(HBM_SPEC,) * 4,
        input_output_aliases={0: 0, 1: 1, 2: 2, 3: 3},
        compiler_params=pltpu.CompilerParams(has_side_effects=SPLIT_EFFECT),
    )(p32, p16, l16, l32, send_sems, recv_sems, after)
    return out[0], out[2], out[3]


def _rs_finish(p32, l16, l32, c_idx, me_idx, name):
    _, h, n = p32.shape
    tr = _tile(h, max(16, (1 << 18) // n // 16 * 16), 16)
    nb = h // tr

    def body(c_ref, me_ref, own_ref, sib_ref, a_ref, b_ref, d_ref, out_ref):
        base = jnp.where(pl.program_id(0) == c_ref[0], own_ref[...], sib_ref[...])
        out_ref[...] = ((base + a_ref[...].astype(F32)) + b_ref[...].astype(F32)) + d_ref[...].astype(F32)

    def piece(j):
        return pl.BlockSpec((None, None, tr, n), lambda hc, i, c, me: (hc, j, i, 0))

    return pl.pallas_call(
        body, name=name,
        grid_spec=pltpu.PrefetchScalarGridSpec(
            num_scalar_prefetch=2, grid=(2, nb),
            in_specs=[pl.BlockSpec((None, tr, n), lambda hc, i, c, me: (me[0], i, 0)),
                      pl.BlockSpec((tr, n), lambda hc, i, c, me: (i, 0)),
                      piece(0), piece(1), piece(2)],
            out_specs=pl.BlockSpec((tr, n), lambda hc, i, c, me: (hc * nb + i, 0))),
        out_shape=jax.ShapeDtypeStruct((2 * h, n), F32),
        compiler_params=_cp(("parallel", "parallel")),
    )(c_idx, me_idx, p32, l32, l16, l16, l16)


def _rs_begin(swap, after, place, tag):
    g, got = _rs_swap_wait(swap, after, "rs_swap_wait_" + tag)
    p32, p16 = _rs_add_halves(g, got, place['c_idx'], "rs_add_halves")
    return _rs_start(p32, p16, "rs_start_" + tag)


def _rs_end(handle, after, place, tag):
    p32, l16, l32 = _rs_wait(handle, after, "rs_wait_" + tag)
    return _rs_finish(p32, l16, l32, place['c_idx'], place['me_idx'], "rs_finish")


def _all_reduce_small(v, name):
    R, n = v.shape

    def body(v_ref, out_ref, buf, send_sems, recv_sems):
        x, y, c, _ = _place()
        me = 4 * x + 2 * y + c
        buf[me] = v_ref[...]
        copies = []
        for d in range(1, 8):
            dx, dy, dc = (d >> 2) & 1, (d >> 1) & 1, d & 1
            px = x if dx == 0 else 1 - x
            py = y if dy == 0 else 1 - y
            pc = c if dc == 0 else 1 - c
            copies.append(pltpu.make_async_remote_copy(
                src_ref=v_ref, dst_ref=buf.at[me], send_sem=send_sems.at[d - 1], recv_sem=recv_sems.at[d - 1],
                device_id=(px, py, pc), device_id_type=MESH))
        for cp in copies:
            cp.start()
        for d in range(1, 8):
            dx, dy, dc = (d >> 2) & 1, (d >> 1) & 1, d & 1
            px = x if dx == 0 else 1 - x
            py = y if dy == 0 else 1 - y
            pc = c if dc == 0 else 1 - c
            pltpu.make_async_remote_copy(
                src_ref=v_ref, dst_ref=buf.at[4 * px + 2 * py + pc], send_sem=send_sems.at[d - 1],
                recv_sem=recv_sems.at[d - 1], device_id=(px, py, pc), device_id_type=MESH).wait_recv()
        for cp in copies:
            cp.wait_send()
        acc = buf[0]
        for k in range(1, 8):
            acc = acc + buf[k]
        out_ref[...] = acc

    return pl.pallas_call(
        body, name=name,
        in_specs=[pl.BlockSpec(memory_space=pltpu.VMEM)],
        out_specs=pl.BlockSpec(memory_space=pltpu.VMEM),
        out_shape=jax.ShapeDtypeStruct((R, n), F32),
        scratch_shapes=[pltpu.VMEM((8, R, n), F32), pltpu.SemaphoreType.DMA((7,)), pltpu.SemaphoreType.DMA((7,))],
        compiler_params=pltpu.CompilerParams(vmem_limit_bytes=VMEM_LIMIT_BYTES),
    )(v)


def _pack_small(parts, rows):
    flat = jnp.concatenate([a.reshape(-1) for a in parts])
    return jnp.pad(flat, (0, rows * LANES - flat.shape[0])).reshape(rows, LANES)


def _mem_fwd(proj, q_blk, mkv, heads):
    S = proj.shape[0]
    ML = mkv.shape[0]
    t = _tile(S, MEM_TILE, 8)
    scale = HEAD ** -0.5

    def body(q_ref, k_ref, v_ref, o_ref):
        s = _dot(q_ref[...], k_ref[...], NT) * scale
        m = jnp.max(s, axis=-1, keepdims=True)
        e = jnp.exp(s - m)
        p = e / jnp.sum(e, axis=-1, keepdims=True)
        o_ref[...] = _dot(p, v_ref[...], NN)

    return pl.pallas_call(
        body, name="mem_fwd", grid=(S // t, heads),
        in_specs=[pl.BlockSpec((t, HEAD), lambda i, h: (i, q_blk + h)),
                  pl.BlockSpec((ML, HEAD), lambda i, h: (0, h)),
                  pl.BlockSpec((ML, HEAD), lambda i, h: (0, heads + h))],
        out_specs=pl.BlockSpec((t, HEAD), lambda i, h: (i, h)),
        out_shape=jax.ShapeDtypeStruct((S, heads * HEAD), F32),
        compiler_params=_cp(("parallel", "parallel")),
    )(proj, mkv, mkv)


def _mem_bwd(proj, q_blk, mkv, dcat, d_blk, heads):
    S = proj.shape[0]
    ML = mkv.shape[0]
    t = _tile(S, MEM_TILE, 8)
    scale = HEAD ** -0.5

    def body(q_ref, k_ref, v_ref, do_ref, dq_ref, dk_ref, dv_ref):
        i = pl.program_id(1)

        @pl.when(i == 0)
        def _():
            dk_ref[...] = jnp.zeros_like(dk_ref)
            dv_ref[...] = jnp.zeros_like(dv_ref)

        q, k, v, do = q_ref[...], k_ref[...], v_ref[...], do_ref[...]
        s = _dot(q, k, NT) * scale
        m = jnp.max(s, axis=-1, keepdims=True)
        e = jnp.exp(s - m)
        p = e / jnp.sum(e, axis=-1, keepdims=True)
        dp = _dot(do, v, NT)
        ds = p * (dp - jnp.sum(p * dp, axis=-1, keepdims=True))
        dq_ref[...] = _dot(ds, k, NN) * scale
        dk_ref[...] += _dot(ds, q, TN) * scale
        dv_ref[...] += _dot(p, do, TN)

    dq, dk, dv = pl.pallas_call(
        body, name="mem_bwd", grid=(heads, S // t),
        in_specs=[pl.BlockSpec((t, HEAD), lambda h, i: (i, q_blk + h)),
                  pl.BlockSpec((ML, HEAD), lambda h, i: (0, h)),
                  pl.BlockSpec((ML, HEAD), lambda h, i: (0, heads + h)),
                  pl.BlockSpec((t, HEAD), lambda h, i: (i, d_blk + h))],
        out_specs=[pl.BlockSpec((t, HEAD), lambda h, i: (i, h)),
                   pl.BlockSpec((ML, HEAD), lambda h, i: (0, h)),
                   pl.BlockSpec((ML, HEAD), lambda h, i: (0, h))],
        out_shape=[jax.ShapeDtypeStruct((S, heads * HEAD), F32),
                   jax.ShapeDtypeStruct((ML, heads * HEAD), F32),
                   jax.ShapeDtypeStruct((ML, heads * HEAD), F32)],
        compiler_params=_cp(("parallel", "arbitrary")),
    )(proj, mkv, mkv, dcat)
    return dq, jnp.concatenate([dk, dv], axis=1)


def _fox_gates(gl, bf):
    S = gl.shape[0]

    def body(g_ref, b_ref, o_ref):
        xv = g_ref[...] + b_ref[...]
        c = jnp.minimum(xv, 0.0) - jnp.log(1.0 + jnp.exp(-jnp.abs(xv)))
        row = lax.broadcasted_iota(jnp.int32, c.shape, 0)
        d = 1
        while d < S:
            c = c + jnp.where(row >= d, pltpu.roll(c, d, 0), 0.0)
            d *= 2
        o_ref[...] = c

    return pl.pallas_call(
        body, name="fox_gates", out_shape=jax.ShapeDtypeStruct((S, 128), F32),
        in_specs=[pl.BlockSpec(memory_space=pltpu.VMEM)] * 2,
        out_specs=pl.BlockSpec(memory_space=pltpu.VMEM),
        compiler_params=_cp(),
    )(gl, bf)


def _fox_gates_bwd(gl, bf, dcf):
    S = gl.shape[0]

    def body(g_ref, b_ref, d_ref, dg_ref, db_ref):
        c = d_ref[...]
        row = lax.broadcasted_iota(jnp.int32, c.shape, 0)
        d = 1
        while d < S:
            c = c + jnp.where(row < S - d, pltpu.roll(c, S - d, 0), 0.0)
            d *= 2
        dx = c * _sigmoid(-(g_ref[...] + b_ref[...]))
        dg_ref[...] = dx
        db_ref[...] = jnp.sum(dx, axis=0, keepdims=True)

    return pl.pallas_call(
        body, name="fox_gates_bwd",
        out_shape=[jax.ShapeDtypeStruct((S, 128), F32), jax.ShapeDtypeStruct((1, 128), F32)],
        in_specs=[pl.BlockSpec(memory_space=pltpu.VMEM)] * 3,
        out_specs=[pl.BlockSpec(memory_space=pltpu.VMEM)] * 2,
        compiler_params=_cp(),
    )(gl, bf, dcf)


def _fox_scores(q, k, cq, ck, t, masked):
    s = _dot(q, k, NT) * (HEAD ** -0.5) + cq - ck
    if masked:
        row = lax.broadcasted_iota(jnp.int32, (t, t), 0)
        col = lax.broadcasted_iota(jnp.int32, (t, t), 1)
        s = jnp.where(row >= col, s, -jnp.inf)
    return s


def _fox_pairs(nq, by_key):
    if by_key:
        pairs = [(i, j) for j in range(nq) for i in range(j, nq)]
    else:
        pairs = [(i, j) for i in range(nq) for j in range(i + 1)]
    return (jnp.asarray(np.array([a for a, _ in pairs], np.int32)),
            jnp.asarray(np.array([b for _, b in pairs], np.int32)))


def _fox_heads_per_step(H):
    return 2 if H % 2 == 0 else 1


def _fox_fwd(proj, cfq, cfk, H):
    S = proj.shape[0]
    t = _tile(S, FOX_TILE)
    nq = S // t
    hb = _fox_heads_per_step(H)
    W, G = hb * HEAD, H // hb
    cols = [slice(i * HEAD, (i + 1) * HEAD) for i in range(hb)]
    qt, kt = _fox_pairs(nq, False)

    def body(qt_ref, kt_ref, q_ref, k_ref, v_ref, cq_ref, ck_ref, o_ref, lse_ref, m_s, l_s, acc_s):
        n = pl.program_id(1)
        qi, ki = qt_ref[n], kt_ref[n]

        @pl.when(ki == 0)
        def _():
            m_s[...] = jnp.full_like(m_s, -jnp.inf)
            l_s[...] = jnp.zeros_like(l_s)
            acc_s[...] = jnp.zeros_like(acc_s)

        def step(masked):
            R = range(hb)
            ss = [_fox_scores(q_ref[:, cols[i]], k_ref[:, cols[i]], cq_ref[i], ck_ref[i], t, masked) for i in R]
            m_new = [jnp.maximum(m_s[i], jnp.max(ss[i], axis=-1, keepdims=True)) for i in R]
            alpha = [jnp.exp(m_s[i] - m_new[i]) for i in R]
            ps = [jnp.exp(ss[i] - m_new[i]) for i in R]
            pv = [_dot(ps[i], v_ref[:, cols[i]], NN) for i in R]
            for i in R:
                l_s[i] = alpha[i] * l_s[i] + jnp.sum(ps[i], axis=-1, keepdims=True)
                acc_s[:, cols[i]] = alpha[i] * acc_s[:, cols[i]] + pv[i]
                m_s[i] = m_new[i]

        @pl.when(ki != qi)
        def _():
            step(False)

        @pl.when(ki == qi)
        def _():
            step(True)
            for i in range(hb):
                o_ref[:, cols[i]] = acc_s[:, cols[i]] / l_s[i]
                lse_ref[i] = m_s[i] + jnp.log(l_s[i])

    qcol = pl.BlockSpec((hb, t, 1), lambda h, n, qt, kt: (h, qt[n], 0))
    return pl.pallas_call(
        body, name="fox_fwd",
        grid_spec=pltpu.PrefetchScalarGridSpec(
            num_scalar_prefetch=2, grid=(G, qt.shape[0]),
            in_specs=[pl.BlockSpec((t, W), lambda h, n, qt, kt: (qt[n], h)),
                      pl.BlockSpec((t, W), lambda h, n, qt, kt: (kt[n], G + h)),
                      pl.BlockSpec((t, W), lambda h, n, qt, kt: (kt[n], 2 * G + h)),
                      qcol,
                      pl.BlockSpec((hb, 1, t), lambda h, n, qt, kt: (h, 0, kt[n]))],
            out_specs=[pl.BlockSpec((t, W), lambda h, n, qt, kt: (qt[n], h)), qcol],
            scratch_shapes=[pltpu.VMEM((hb, t, 1), F32), pltpu.VMEM((hb, t, 1), F32), pltpu.VMEM((t, W), F32)]),
        out_shape=[jax.ShapeDtypeStruct((S, H * HEAD), F32), jax.ShapeDtypeStruct((H, S, 1), F32)],
        compiler_params=_cp(("parallel", "arbitrary")),
    )(qt, kt, proj, proj, proj, cfq, cfk)


def _fox_bwd_rowdot(proj, cfq, cfk, lse, dcat, H):
    S = proj.shape[0]
    t = _tile(S, FOX_TILE)
    nq = S // t
    hb = _fox_heads_per_step(H)
    W, G = hb * HEAD, H // hb
    cols = [slice(i * HEAD, (i + 1) * HEAD) for i in range(hb)]
    qt, kt = _fox_pairs(nq, False)

    def body(qt_ref, kt_ref, q_ref, k_ref, v_ref, do_ref, lse_ref, cq_ref, ck_ref, d_ref):
        n = pl.program_id(1)
        qi, ki = qt_ref[n], kt_ref[n]

        @pl.when(ki == 0)
        def _():
            d_ref[...] = jnp.zeros_like(d_ref)

        def step(masked):
            R = range(hb)
            ss = [_fox_scores(q_ref[:, cols[i]], k_ref[:, cols[i]], cq_ref[i], ck_ref[i], t, masked) for i in R]
            dps = [_dot(do_ref[:, cols[i]], v_ref[:, cols[i]], NT) for i in R]
            ps = [jnp.exp(ss[i] - lse_ref[i]) for i in R]
            for i in R:
                d_ref[i] += jnp.sum(ps[i] * dps[i], axis=-1, keepdims=True)

        @pl.when(ki != qi)
        def _():
            step(False)

        @pl.when(ki == qi)
        def _():
            step(True)

    qtile = pl.BlockSpec((t, W), lambda h, n, qt, kt: (qt[n], h))
    qcol = pl.BlockSpec((hb, t, 1), lambda h, n, qt, kt: (h, qt[n], 0))
    return pl.pallas_call(
        body, name="fox_bwd_rowdot",
        grid_spec=pltpu.PrefetchScalarGridSpec(
            num_scalar_prefetch=2, grid=(G, qt.shape[0]),
            in_specs=[qtile,
                      pl.BlockSpec((t, W), lambda h, n, qt, kt: (kt[n], G + h)),
                      pl.BlockSpec((t, W), lambda h, n, qt, kt: (kt[n], 2 * G + h)),
                      qtile, qcol, qcol,
                      pl.BlockSpec((hb, 1, t), lambda h, n, qt, kt: (h, 0, kt[n]))],
            out_specs=qcol),
        out_shape=jax.ShapeDtypeStruct((H, S, 1), F32),
        compiler_params=_cp(("parallel", "arbitrary")),
    )(qt, kt, proj, proj, proj, dcat, lse, cfq, cfk)


def _fox_bwd(proj, cfq, cfk, rowdot, lse, dcat, H):
    S = proj.shape[0]
    t = _tile(S, FOX_TILE)
    nq = S // t
    scale = HEAD ** -0.5
    hb = _fox_heads_per_step(H)
    W, G = hb * HEAD, H // hb
    cols = [slice(i * HEAD, (i + 1) * HEAD) for i in range(hb)]
    qt, kt = _fox_pairs(nq, True)

    def body(qt_ref, kt_ref, q_ref, k_ref, v_ref, dd_ref, do_ref, lse_ref, cq_ref, ck_ref,
             dq_ref, dk_ref, dv_ref, dck_ref):
        n = pl.program_id(1)
        i_, j_ = qt_ref[n], kt_ref[n]

        @pl.when(n == 0)
        def _():
            dq_ref[...] = jnp.zeros_like(dq_ref)

        @pl.when(i_ == j_)
        def _():
            dk_ref[...] = jnp.zeros_like(dk_ref)
            dv_ref[...] = jnp.zeros_like(dv_ref)
            dck_ref[...] = jnp.zeros_like(dck_ref)

        def step(masked):
            R = range(hb)
            qs, ks = [q_ref[:, c] for c in cols], [k_ref[:, c] for c in cols]
            dos = [do_ref[:, c] for c in cols]
            ss = [_fox_scores(qs[i], ks[i], cq_ref[i], ck_ref[i], t, masked) for i in R]
            dps = [_dot(dos[i], v_ref[:, cols[i]], NT) for i in R]
            ps = [jnp.exp(ss[i] - lse_ref[i]) for i in R]
            dss = [ps[i] * (dps[i] - dd_ref[i]) for i in R]
            dvs = [_dot(ps[i], dos[i], TN) for i in R]
            dks = [_dot(dss[i], qs[i], TN) * scale for i in R]
            dqs = [_dot(dss[i], ks[i], NN) * scale for i in R]
            rows = pl.ds(pl.multiple_of(i_ * t, t), t)
            for i in R:
                dv_ref[:, cols[i]] += dvs[i]
                dk_ref[:, cols[i]] += dks[i]
                dq_ref[rows, cols[i]] += dqs[i]
                dck_ref[i] -= jnp.sum(dss[i], axis=0, keepdims=True)

        @pl.when(i_ != j_)
        def _():
            step(False)

        @pl.when(i_ == j_)
        def _():
            step(True)

    qtile = pl.BlockSpec((t, W), lambda h, n, qt, kt: (qt[n], h))
    qcol = pl.BlockSpec((hb, t, 1), lambda h, n, qt, kt: (h, qt[n], 0))
    ktile = pl.BlockSpec((t, W), lambda h, n, qt, kt: (kt[n], h))
    krow = pl.BlockSpec((hb, 1, t), lambda h, n, qt, kt: (h, 0, kt[n]))
    return pl.pallas_call(
        body, name="fox_bwd",
        grid_spec=pltpu.PrefetchScalarGridSpec(
            num_scalar_prefetch=2, grid=(G, qt.shape[0]),
            in_specs=[qtile,
                      pl.BlockSpec((t, W), lambda h, n, qt, kt: (kt[n], G + h)),
                      pl.BlockSpec((t, W), lambda h, n, qt, kt: (kt[n], 2 * G + h)),
                      qcol, qtile, qcol, qcol, krow],
            out_specs=[pl.BlockSpec((S, W), lambda h, n, qt, kt: (0, h)), ktile, ktile, krow]),
        out_shape=[jax.ShapeDtypeStruct((S, H * HEAD), F32)] * 3 + [jax.ShapeDtypeStruct((H, 1, S), F32)],
        compiler_params=_cp(("parallel", "arbitrary")),
    )(qt, kt, proj, proj, proj, rowdot, dcat, lse, cfq, cfk)


def _s5_prep(lam_re, lam_im, log_dt, b_re, b_im, c_re, c_im):
    G, P = lam_re.shape
    ns = G // 16
    dt = jnp.exp(log_dt)[:, None]
    mag = jnp.exp(lam_re * dt)
    a_re, a_im = mag * jnp.cos(lam_im * dt), mag * jnp.sin(lam_im * dt)
    den = lam_re * lam_re + lam_im * lam_im
    z_re = ((a_re - 1.0) * lam_re + a_im * lam_im) / den
    z_im = (a_im * lam_re - (a_re - 1.0) * lam_im) / den
    bb_re = z_re[..., None] * b_re - z_im[..., None] * b_im
    bb_im = z_re[..., None] * b_im + z_im[..., None] * b_re
    eye = jnp.eye(16, dtype=F32)
    bb = jnp.stack([bb_re, bb_im]).reshape(2, ns, 16, P, S5_GROUP)
    wb = jnp.einsum('asgpc,gh->sgcahp', bb, eye).reshape(ns, S5_SLAB, 2 * 16 * P)
    cc = jnp.stack([c_re, -c_im]).reshape(2, ns, 16, S5_GROUP, P)
    wc = jnp.einsum('asgcp,gh->sagphc', cc, eye).reshape(ns, 2 * 16 * P, S5_SLAB)
    a = jnp.concatenate([a_re.reshape(ns, 1, 16 * P), a_im.reshape(ns, 1, 16 * P)], axis=-1)
    return wb, wc, a


def _s5_tables(lam_re, lam_im, log_dt):
    G, P = lam_re.shape
    ns = G // 16
    dt = jnp.exp(log_dt)[:, None]
    tt = jnp.arange(1, S5_CHUNK + 1, dtype=F32)[:, None, None]
    mag = jnp.exp(lam_re * dt * tt)
    ang = lam_im * dt * tt
    pr = (mag * jnp.cos(ang)).reshape(S5_CHUNK, ns, 16 * P).transpose(1, 0, 2)
    pi = (mag * jnp.sin(ang)).reshape(S5_CHUNK, ns, 16 * P).transpose(1, 0, 2)
    return pr, pi, pr[:, ::-1], pi[:, ::-1]


def _s5_scan_fwd(proj, wb, wc, pr, pi, dskip):
    S = proj.shape[0]
    ns = wb.shape[0]
    W = wb.shape[2]
    hw = W // 2
    T = S5_CHUNK
    nc = S // T
    mix = ns * S5_SLAB

    def body(u_ref, wb_ref, wc_ref, pr_ref, pi_ref, d_ref, v_ref, yg_ref, h_ref, cin_ref, carry):
        c = pl.program_id(1)

        @pl.when(c == 0)
        def _():
            carry[...] = jnp.zeros_like(carry)

        u = u_ref[...]
        bu = _dot(u, wb_ref[...], NN)
        xr, xi = bu[:, :hw], bu[:, hw:]
        sub = lax.broadcasted_iota(jnp.int32, (T, hw), 0) & (S5_ROWS - 1)
        d = 1
        while d < S5_ROWS:
            ar, ai = pr_ref[pl.ds(d - 1, 1), :], pi_ref[pl.ds(d - 1, 1), :]
            sr = jnp.where(sub >= d, pltpu.roll(xr, d, 0), 0.0)
            si = jnp.where(sub >= d, pltpu.roll(xi, d, 0), 0.0)
            xr, xi = xr + ar * sr - ai * si, xi + ar * si + ai * sr
            d *= 2
        cin_ref[...] = carry[...]
        cr, ci = carry[:, :hw], carry[:, hw:]
        pwr, pwi = pr_ref[pl.ds(0, S5_ROWS), :], pi_ref[pl.ds(0, S5_ROWS), :]
        for g in range(T // S5_ROWS):
            rows = slice(g * S5_ROWS, (g + 1) * S5_ROWS)
            hr = xr[rows, :] + pwr * cr - pwi * ci
            hi = xi[rows, :] + pwr * ci + pwi * cr
            h_ref[rows, :hw] = hr
            h_ref[rows, hw:] = hi
            cr, ci = hr[S5_ROWS - 1:S5_ROWS, :], hi[S5_ROWS - 1:S5_ROWS, :]
        carry[:, :hw] = cr
        carry[:, hw:] = ci
        y = _dot(h_ref[...], wc_ref[...], NN)
        v = y + d_ref[...] * u
        v_ref[...] = v
        yg_ref[...] = _gelu(v)

    return pl.pallas_call(
        body, name="s5_scan_fwd", grid=(ns, nc),
        in_specs=[pl.BlockSpec((T, S5_SLAB), lambda s, c: (c, s)),
                  pl.BlockSpec((None, S5_SLAB, W), lambda s, c: (s, 0, 0)),
                  pl.BlockSpec((None, W, S5_SLAB), lambda s, c: (s, 0, 0)),
                  pl.BlockSpec((None, T, hw), lambda s, c: (s, 0, 0)),
                  pl.BlockSpec((None, T, hw), lambda s, c: (s, 0, 0)),
                  pl.BlockSpec((1, S5_SLAB), lambda s, c: (0, s))],
        out_specs=[pl.BlockSpec((T, S5_SLAB), lambda s, c: (c, s)),
                   pl.BlockSpec((T, S5_SLAB), lambda s, c: (c, s)),
                   pl.BlockSpec((T, W), lambda s, c: (c, s)),
                   pl.BlockSpec((None, 1, W), lambda s, c: (c, 0, s))],
        out_shape=[jax.ShapeDtypeStruct((S, mix), F32), jax.ShapeDtypeStruct((S, mix), F32),
                   jax.ShapeDtypeStruct((S, ns * W), F32), jax.ShapeDtypeStruct((nc, 1, ns * W), F32)],
        scratch_shapes=[pltpu.VMEM((1, W), F32)],
        compiler_params=_cp(("parallel", "arbitrary")),
    )(proj, wb, wc, pr, pi, dskip)


def _s5_scan_bwd(dv, proj, hs, cin, wb, wc, pr, pi, prr, pir, dskip):
    S = proj.shape[0]
    ns = wb.shape[0]
    W = wb.shape[2]
    hw = W // 2
    T = S5_CHUNK
    nc = S // T
    mix = ns * S5_SLAB

    def body(dv_ref, u_ref, h_ref, cin_ref, wb_ref, wc_ref, pr_ref, pi_ref, prr_ref, pir_ref, d_ref,
             du_ref, dwb_ref, dwc_ref, da_ref, dd_ref, lam_s, carry):
        c = pl.program_id(1)

        @pl.when(c == 0)
        def _():
            carry[...] = jnp.zeros_like(carry)
            dwb_ref[...] = jnp.zeros_like(dwb_ref)
            dwc_ref[...] = jnp.zeros_like(dwc_ref)
            da_ref[...] = jnp.zeros_like(da_ref)
            dd_ref[...] = jnp.zeros_like(dd_ref)

        dy, u = dv_ref[...], u_ref[...]
        dh = _dot(dy, wc_ref[...], NT)
        gr, gi = dh[:, :hw], dh[:, hw:]
        row = lax.broadcasted_iota(jnp.int32, (T, hw), 0)
        sub = row & (S5_ROWS - 1)
        d = 1
        while d < S5_ROWS:
            ar, ai = pr_ref[pl.ds(d - 1, 1), :], -pi_ref[pl.ds(d - 1, 1), :]
            sr = jnp.where(sub < S5_ROWS - d, pltpu.roll(gr, T - d, 0), 0.0)
            si = jnp.where(sub < S5_ROWS - d, pltpu.roll(gi, T - d, 0), 0.0)
            gr, gi = gr + ar * sr - ai * si, gi + ar * si + ai * sr
            d *= 2
        lr, li = carry[:, :hw], carry[:, hw:]
        pwr, pwi = prr_ref[pl.ds(T - S5_ROWS, S5_ROWS), :], -pir_ref[pl.ds(T - S5_ROWS, S5_ROWS), :]
        for g in reversed(range(T // S5_ROWS)):
            rows = slice(g * S5_ROWS, (g + 1) * S5_ROWS)
            lgr = gr[rows, :] + pwr * lr - pwi * li
            lgi = gi[rows, :] + pwr * li + pwi * lr
            lam_s[rows, :hw] = lgr
            lam_s[rows, hw:] = lgi
            lr, li = lgr[0:1, :], lgi[0:1, :]
        carry[:, :hw] = lr
        carry[:, hw:] = li
        gr, gi = lam_s[:, :hw], lam_s[:, hw:]
        hr, hi = h_ref[:, :hw], h_ref[:, hw:]
        hpr = jnp.where(row >= 1, pltpu.roll(hr, 1, 0), cin_ref[:, :hw])
        hpi = jnp.where(row >= 1, pltpu.roll(hi, 1, 0), cin_ref[:, hw:])
        da_ref[:, :hw] += jnp.sum(hpr * gr + hpi * gi, axis=0, keepdims=True)
        da_ref[:, hw:] += jnp.sum(hpr * gi - hpi * gr, axis=0, keepdims=True)
        lam = lam_s[...]
        du_ref[...] = _dot(lam, wb_ref[...], NT) + dy * d_ref[...]
        dwb_ref[...] += _dot(u, lam, TN)
        dwc_ref[...] += _dot(h_ref[...], dy, TN)
        dd_ref[...] += jnp.sum(dy * u, axis=0, keepdims=True)

    def rc(c):
        return nc - 1 - c

    return pl.pallas_call(
        body, name="s5_scan_bwd", grid=(ns, nc),
        in_specs=[pl.BlockSpec((T, S5_SLAB), lambda s, c: (rc(c), s)),
                  pl.BlockSpec((T, S5_SLAB), lambda s, c: (rc(c), s)),
                  pl.BlockSpec((T, W), lambda s, c: (rc(c), s)),
                  pl.BlockSpec((None, 1, W), lambda s, c: (rc(c), 0, s)),
                  pl.BlockSpec((None, S5_SLAB, W), lambda s, c: (s, 0, 0)),
                  pl.BlockSpec((None, W, S5_SLAB), lambda s, c: (s, 0, 0)),
                  pl.BlockSpec((None, T, hw), lambda s, c: (s, 0, 0)),
                  pl.BlockSpec((None, T, hw), lambda s, c: (s, 0, 0)),
                  pl.BlockSpec((None, T, hw), lambda s, c: (s, 0, 0)),
                  pl.BlockSpec((None, T, hw), lambda s, c: (s, 0, 0)),
                  pl.BlockSpec((1, S5_SLAB), lambda s, c: (0, s))],
        out_specs=[pl.BlockSpec((T, S5_SLAB), lambda s, c: (rc(c), s)),
                   pl.BlockSpec((None, S5_SLAB, W), lambda s, c: (s, 0, 0)),
                   pl.BlockSpec((None, W, S5_SLAB), lambda s, c: (s, 0, 0)),
                   pl.BlockSpec((None, 1, W), lambda s, c: (s, 0, 0)),
                   pl.BlockSpec((1, S5_SLAB), lambda s, c: (0, s))],
        out_shape=[jax.ShapeDtypeStruct((S, mix), F32), jax.ShapeDtypeStruct(wb.shape, F32),
                   jax.ShapeDtypeStruct(wc.shape, F32), jax.ShapeDtypeStruct((ns, 1, W), F32),
                   jax.ShapeDtypeStruct((1, mix), F32)],
        scratch_shapes=[pltpu.VMEM((T, W), F32), pltpu.VMEM((1, W), F32)],
        compiler_params=_cp(("parallel", "arbitrary")),
    )(dv, proj, hs, cin, wb, wc, pr, pi, prr, pir, dskip)


def _s5_glu_bwd(dcat, yg, z):
    S, mix = yg.shape
    tr = _tile(S, ROW_TILE, 8)

    def body(do_ref, yg_ref, z_ref, dz_ref, dy_ref, db_ref):
        i = pl.program_id(0)

        @pl.when(i == 0)
        def _():
            db_ref[...] = jnp.zeros_like(db_ref)

        do, yg_, sz = do_ref[...], yg_ref[...], _sigmoid(z_ref[...])
        dz = do * yg_ * sz * (1.0 - sz)
        dz_ref[...] = dz
        dy_ref[...] = do * sz
        db_ref[...] += jnp.sum(dz, axis=0, keepdims=True)

    blk = pl.BlockSpec((tr, mix), lambda i: (i, 0))
    return pl.pallas_call(
        body, name="s5_glu_bwd", grid=(S // tr,),
        in_specs=[blk, blk, blk], out_specs=[blk, blk, pl.BlockSpec((1, mix), lambda i: (0, 0))],
        out_shape=[jax.ShapeDtypeStruct((S, mix), F32), jax.ShapeDtypeStruct((S, mix), F32),
                   jax.ShapeDtypeStruct((1, mix), F32)],
        compiler_params=_cp(("arbitrary",)),
    )(dcat, yg, z)


def _rows_down(x, j):
    return x if j == 0 else pltpu.roll(x, j, 0)


def _conv_rows(xe, w_ref, n):
    c = None
    for j in range(GDN_CONV):
        term = w_ref[pl.ds(GDN_CONV - 1 - j, 1), :] * _rows_down(xe, j)[8:8 + n, :]
        c = term if c is None else c + term
    return c


def _gdn_prep(proj, blk0, nblk, convw, norm, scale, name):
    S = proj.shape[0]
    tr = _tile(S, CONV_TILE, 8)
    nb8 = tr // 8

    def body(x_ref, xb_ref, w_ref, o_ref):
        i = pl.program_id(1)
        xe = jnp.concatenate([jnp.where(i == 0, 0.0, xb_ref[...]), x_ref[...]], axis=0)
        c = _conv_rows(xe, w_ref, tr)
        s = c * _sigmoid(c)
        if norm:
            s = s * lax.rsqrt(jnp.sum(s * s, axis=-1, keepdims=True) + EPS) * scale
        o_ref[...] = s

    return pl.pallas_call(
        body, name=name, grid=(nblk, S // tr),
        in_specs=[pl.BlockSpec((tr, HEAD), lambda j, i: (i, blk0 + j)),
                  pl.BlockSpec((8, HEAD), lambda j, i: (jnp.maximum(i * nb8 - 1, 0), blk0 + j)),
                  pl.BlockSpec((GDN_CONV, HEAD), lambda j, i: (0, j))],
        out_specs=pl.BlockSpec((tr, HEAD), lambda j, i: (i, j)),
        out_shape=jax.ShapeDtypeStruct((S, nblk * HEAD), F32),
        compiler_params=_cp(("parallel", "parallel")),
    )(proj, proj, convw)


def _gdn_prep_bwd(proj, blk0, nblk, convw, dout, norm, scale, name):
    S = proj.shape[0]
    tr = _tile(S, CONV_TILE, 8)
    nb8 = tr // 8
    last8 = S // 8 - 1
    nrow = S // tr

    def body(x_ref, xb_ref, xa_ref, w_ref, d_ref, da_ref, dx_ref, dw_ref):
        i = pl.program_id(1)

        @pl.when(i == 0)
        def _():
            dw_ref[...] = jnp.zeros_like(dw_ref)

        xe = jnp.concatenate([jnp.where(i == 0, 0.0, xb_ref[...]), x_ref[...], xa_ref[...]], axis=0)
        de = jnp.concatenate([d_ref[...], da_ref[...]], axis=0)
        n = tr + 8
        c = _conv_rows(xe, w_ref, n)
        sg = _sigmoid(c)
        s = c * sg
        if norm:
            r = lax.rsqrt(jnp.sum(s * s, axis=-1, keepdims=True) + EPS)
            ds = scale * r * (de - s * (r * r) * jnp.sum(de * s, axis=-1, keepdims=True))
        else:
            ds = de
        dc = ds * (sg + c * sg * (1.0 - sg))
        rowi = lax.broadcasted_iota(jnp.int32, (n, HEAD), 0)
        dc = jnp.where((i == nrow - 1) & (rowi >= tr), 0.0, dc)
        dct = dc[:tr, :]
        dx = None
        for j in range(GDN_CONV):
            tap = pl.ds(GDN_CONV - 1 - j, 1)
            up = dct if j == 0 else pltpu.roll(dc, n - j, 0)[:tr, :]
            term = w_ref[tap, :] * up
            dx = term if dx is None else dx + term
            dw_ref[tap, :] += jnp.sum(dct * _rows_down(xe, j)[8:8 + tr, :], axis=0, keepdims=True)
        dx_ref[...] = dx

    return pl.pallas_call(
        body, name=name, grid=(nblk, nrow),
        in_specs=[pl.BlockSpec((tr, HEAD), lambda j, i: (i, blk0 + j)),
                  pl.BlockSpec((8, HEAD), lambda j, i: (jnp.maximum(i * nb8 - 1, 0), blk0 + j)),
                  pl.BlockSpec((8, HEAD), lambda j, i: (jnp.minimum((i + 1) * nb8, last8), blk0 + j)),
                  pl.BlockSpec((GDN_CONV, HEAD), lambda j, i: (0, j)),
                  pl.BlockSpec((tr, HEAD), lambda j, i: (i, j)),
                  pl.BlockSpec((8, HEAD), lambda j, i: (jnp.minimum((i + 1) * nb8, last8), j))],
        out_specs=[pl.BlockSpec((tr, HEAD), lambda j, i: (i, j)),
                   pl.BlockSpec((GDN_CONV, HEAD), lambda j, i: (0, j))],
        out_shape=[jax.ShapeDtypeStruct((S, nblk * HEAD), F32),
                   jax.ShapeDtypeStruct((GDN_CONV, nblk * HEAD), F32)],
        compiler_params=_cp(("parallel", "arbitrary")),
    )(proj, proj, proj, convw, dout, dout)


def _gdn_gates(pg, alog, dtb):
    S = pg.shape[0]

    def body(a_ref, b_ref, al_ref, dt_ref, gc_ref, be_ref):
        g = -jnp.exp(al_ref[...]) * _softplus(a_ref[...] + dt_ref[...])
        rowm = lax.broadcasted_iota(jnp.int32, g.shape, 0) & (GDN_CHUNK - 1)
        c = g
        d = 1
        while d < GDN_CHUNK:
            c = c + jnp.where(rowm >= d, pltpu.roll(c, d, 0), 0.0)
            d *= 2
        gc_ref[...] = c
        be_ref[...] = _sigmoid(b_ref[...])

    blk = pl.BlockSpec((S, 128), lambda i: (0, 0))
    vec = pl.BlockSpec((1, 128), lambda i: (0, 0))
    return pl.pallas_call(
        body, name="gdn_gates", grid=(1,),
        in_specs=[blk, pl.BlockSpec((S, 128), lambda i: (0, 1)), vec, vec],
        out_specs=[blk, blk],
        out_shape=[jax.ShapeDtypeStruct((S, 128), F32)] * 2,
        compiler_params=_cp(("arbitrary",)),
    )(pg, pg, alog, dtb)


def _gdn_gates_bwd(pg, alog, dtb, dgc, dbeta):
    S = pg.shape[0]

    def body(a_ref, b_ref, al_ref, dt_ref, dgc_ref, dbe_ref, dpa_ref, dpb_ref, dal_ref, ddt_ref):
        rowm = lax.broadcasted_iota(jnp.int32, (S, 128), 0) & (GDN_CHUNK - 1)
        c = dgc_ref[...]
        d = 1
        while d < GDN_CHUNK:
            c = c + jnp.where(rowm < GDN_CHUNK - d, pltpu.roll(c, S - d, 0), 0.0)
            d *= 2
        xv = a_ref[...] + dt_ref[...]
        ea = jnp.exp(al_ref[...])
        g = -ea * _softplus(xv)
        dx = c * (-ea) * _sigmoid(xv)
        dpa_ref[...] = dx
        dal_ref[...] = jnp.sum(c * g, axis=0, keepdims=True)
        ddt_ref[...] = jnp.sum(dx, axis=0, keepdims=True)
        be = _sigmoid(b_ref[...])
        dpb_ref[...] = dbe_ref[...] * be * (1.0 - be)

    blk = pl.BlockSpec((S, 128), lambda i: (0, 0))
    blk1 = pl.BlockSpec((S, 128), lambda i: (0, 1))
    vec = pl.BlockSpec((1, 128), lambda i: (0, 0))
    dpa, dpb, dal, ddt = pl.pallas_call(
        body, name="gdn_gates_bwd", grid=(1,),
        in_specs=[blk, blk1, vec, vec, blk, blk],
        out_specs=[blk, blk, vec, vec],
        out_shape=[jax.ShapeDtypeStruct((S, 128), F32)] * 2 + [jax.ShapeDtypeStruct((1, 128), F32)] * 2,
        compiler_params=_cp(("arbitrary",)),
    )(pg, pg, alog, dtb, dgc, dbeta)
    return jnp.concatenate([dpa, dpb], axis=1), dal, ddt


def _gdn_pre(qs, ks, vs, gcs, grs, betas):
    C = GDN_CHUNK
    n = len(qs)
    r = lax.broadcasted_iota(jnp.int32, (C, C), 0)
    c_ = lax.broadcasted_iota(jnp.int32, (C, C), 1)
    lower, strict = r >= c_, r > c_
    eye = jnp.where(r == c_, 1.0, 0.0)
    decs = [jnp.exp(jnp.where(lower, gcs[i] - grs[i], -jnp.inf)) for i in range(n)]
    kbs = [ks[i] * betas[i] for i in range(n)]
    vbs = [vs[i] * betas[i] for i in range(n)]
    lmats = [jnp.where(strict, _dot(kbs[i], ks[i], NT) * decs[i], 0.0) for i in range(n)]
    amats = [jnp.where(lower, _dot(qs[i], ks[i], NT) * decs[i], 0.0) for i in range(n)]
    pks = [-lm for lm in lmats]
    tinvs = [eye + pk for pk in pks]
    for _ in range(5):
        pks = [_dotf(pk, pk, NN) for pk in pks]
        tinvs = [tv + _dotf(tv, pk, NN) for tv, pk in zip(tinvs, pks)]
    es = [jnp.exp(gc) for gc in gcs]
    glasts = [gc[C - 1:C, :] for gc in gcs]
    fs = [jnp.exp(gl - gc) for gl, gc in zip(glasts, gcs)]
    gls = [jnp.exp(gl) for gl in glasts]
    us = [_dotf(tinvs[i], vbs[i], NN) for i in range(n)]
    ws = [_dotf(tinvs[i], kbs[i] * es[i], NN) for i in range(n)]
    return [dict(lower=lower, strict=strict, dec=decs[i], kb=kbs[i], vb=vbs[i], lmat=lmats[i], tinv=tinvs[i],
                 e=es[i], f=fs[i], gl=gls[i], u=us[i], w=ws[i], amat=amats[i], qd=qs[i] * es[i],
                 kd=ks[i] * fs[i]) for i in range(n)]


def _gdn_heads_per_step(H):
    return max(d for d in (1, 2, 3, 4) if H % d == 0)


def _gdn_chunk_fwd(q, k, v, gcol, grow, bcol):
    S = q.shape[0]
    H, NC = gcol.shape[0], gcol.shape[1]
    C = GDN_CHUNK
    hb = _gdn_heads_per_step(H)

    def body(q_ref, k_ref, v_ref, gc_ref, gr_ref, b_ref, o_ref, st_ref, state):
        n = pl.program_id(1)

        @pl.when(n == 0)
        def _():
            state[...] = jnp.zeros_like(state)

        cols = [slice(i * HEAD, (i + 1) * HEAD) for i in range(hb)]
        ps = _gdn_pre([q_ref[:, c] for c in cols], [k_ref[:, c] for c in cols], [v_ref[:, c] for c in cols],
                      [gc_ref[i] for i in range(hb)], [gr_ref[i] for i in range(hb)],
                      [b_ref[i] for i in range(hb)])
        s0s = [state[i] for i in range(hb)]
        vns = [ps[i]['u'] - _dot(ps[i]['w'], s0s[i], NN) for i in range(hb)]
        outs = [_dot(ps[i]['qd'], s0s[i], NN) + _dot(ps[i]['amat'], vns[i], NN) for i in range(hb)]
        news = [s0s[i] * ps[i]['gl'] + _dot(ps[i]['kd'], vns[i], TN) for i in range(hb)]
        for i in range(hb):
            st_ref[i] = s0s[i]
            o_ref[:, cols[i]] = outs[i]
            state[i] = news[i]

    tok = pl.BlockSpec((C, hb * HEAD), lambda h, n: (n, h))
    col = pl.BlockSpec((hb, None, C, 1), lambda h, n: (h, n, 0, 0))
    rowb = pl.BlockSpec((hb, None, 1, C), lambda h, n: (h, n, 0, 0))
    return pl.pallas_call(
        body, name="gdn_chunk_fwd", grid=(H // hb, NC),
        in_specs=[tok, tok, tok, col, rowb, col],
        out_specs=[tok, pl.BlockSpec((hb, None, HEAD, HEAD), lambda h, n: (h, n, 0, 0))],
        out_shape=[jax.ShapeDtypeStruct((S, H * HEAD), F32), jax.ShapeDtypeStruct((H, NC, HEAD, HEAD), F32)],
        scratch_shapes=[pltpu.VMEM((hb, HEAD, HEAD), F32)],
        compiler_params=_cp(("parallel", "arbitrary")),
    )(q, k, v, gcol, grow, bcol)


def _gdn_chunk_bwd(q, k, v, gcol, grow, bcol, st, do):
    S = q.shape[0]
    H, NC = gcol.shape[0], gcol.shape[1]
    C = GDN_CHUNK
    hb = _gdn_heads_per_step(H)

    def body(q_ref, k_ref, v_ref, gc_ref, gr_ref, b_ref, st_ref, do_ref,
             dq_ref, dk_ref, dv_ref, dgc_ref, dbe_ref, dstate):
        n = pl.program_id(1)

        @pl.when(n == 0)
        def _():
            dstate[...] = jnp.zeros_like(dstate)

        R = range(hb)
        cols = [slice(i * HEAD, (i + 1) * HEAD) for i in R]
        qs, ks, vs = [q_ref[:, c] for c in cols], [k_ref[:, c] for c in cols], [v_ref[:, c] for c in cols]
        betas = [b_ref[i] for i in R]
        ps = _gdn_pre(qs, ks, vs, [gc_ref[i] for i in R], [gr_ref[i] for i in R], betas)
        lower, strict = ps[0]['lower'], ps[0]['strict']
        s0s, dos, ds1s = [st_ref[i] for i in R], [do_ref[:, c] for c in cols], [dstate[i] for i in R]
        vns = [ps[i]['u'] - _dot(ps[i]['w'], s0s[i], NN) for i in R]
        dvns = [_dot(ps[i]['amat'], dos[i], TN) + _dot(ps[i]['kd'], ds1s[i], NN) for i in R]
        damats = [jnp.where(lower, _dot(dos[i], vns[i], NT), 0.0) for i in R]
        dqds = [_dot(dos[i], s0s[i], NT) for i in R]
        dkds = [_dot(vns[i], ds1s[i], NT) for i in R]
        dgls = [jnp.sum(s0s[i] * ds1s[i], keepdims=True) for i in R]
        ds0s = [ps[i]['gl'] * ds1s[i] + _dot(ps[i]['qd'], dos[i], TN) - _dot(ps[i]['w'], dvns[i], TN) for i in R]
        dws = [-_dot(dvns[i], s0s[i], NT) for i in R]
        dvbs = [_dotf(ps[i]['tinv'], dvns[i], TN) for i in R]
        dkgs = [_dotf(ps[i]['tinv'], dws[i], TN) for i in R]
        dls = [-jnp.where(strict, _dotf(dvbs[i], ps[i]['u'], NT) + _dotf(dkgs[i], ps[i]['w'], NT), 0.0) for i in R]
        dkks = [dls[i] * ps[i]['dec'] for i in R]
        dqks = [damats[i] * ps[i]['dec'] for i in R]
        ms = [dls[i] * ps[i]['lmat'] + damats[i] * ps[i]['amat'] for i in R]
        dkbs = [_dot(dkks[i], ks[i], NN) + dkgs[i] * ps[i]['e'] for i in R]
        dks = [_dot(dkks[i], ps[i]['kb'], TN) + _dot(dqks[i], qs[i], TN) + dkds[i] * ps[i]['f'] + dkbs[i] * betas[i]
               for i in R]
        dqs = [_dot(dqks[i], ks[i], NN) + dqds[i] * ps[i]['e'] for i in R]
        ones = jnp.ones((C, HEAD), F32)
        colsums = [_dotf(ms[i], ones, TN)[:, 0:1] for i in R]
        rowi = lax.broadcasted_iota(jnp.int32, (C, 1), 0)
        for i in R:
            p = ps[i]
            de = (jnp.sum(dkgs[i] * p['kb'], axis=-1, keepdims=True)
                  + jnp.sum(dqds[i] * qs[i], axis=-1, keepdims=True))
            df = jnp.sum(dkds[i] * ks[i], axis=-1, keepdims=True)
            dgc = jnp.sum(ms[i], axis=-1, keepdims=True) - colsums[i] + de * p['e'] - df * p['f']
            dlast = jnp.sum(df * p['f'], keepdims=True) + dgls[i] * p['gl']
            dgc_ref[i] = dgc + jnp.where(rowi == C - 1, dlast, 0.0)
            dbe_ref[i] = (jnp.sum(dkbs[i] * ks[i], axis=-1, keepdims=True)
                          + jnp.sum(dvbs[i] * vs[i], axis=-1, keepdims=True))
            dstate[i] = ds0s[i]
            dq_ref[:, cols[i]] = dqs[i]
            dk_ref[:, cols[i]] = dks[i]
            dv_ref[:, cols[i]] = dvbs[i] * betas[i]

    def rn(n):
        return NC - 1 - n

    tok = pl.BlockSpec((C, hb * HEAD), lambda h, n: (rn(n), h))
    col = pl.BlockSpec((hb, None, C, 1), lambda h, n: (h, rn(n), 0, 0))
    rowb = pl.BlockSpec((hb, None, 1, C), lambda h, n: (h, rn(n), 0, 0))
    return pl.pallas_call(
        body, name="gdn_chunk_bwd", grid=(H // hb, NC),
        in_specs=[tok, tok, tok, col, rowb, col,
                  pl.BlockSpec((hb, None, HEAD, HEAD), lambda h, n: (h, rn(n), 0, 0)), tok],
        out_specs=[tok, tok, tok, col, col],
        out_shape=[jax.ShapeDtypeStruct((S, H * HEAD), F32)] * 3
        + [jax.ShapeDtypeStruct((H, NC, C, 1), F32)] * 2,
        scratch_shapes=[pltpu.VMEM((hb, HEAD, HEAD), F32)],
        compiler_params=_cp(("parallel", "arbitrary")),
    )(q, k, v, gcol, grow, bcol, st, do)


def _gdn_onorm(o, proj, gate_blk, w, H):
    S = o.shape[0]
    tr = _tile(S, CONV_TILE, 8)

    def body(o_ref, g_ref, w_ref, out_ref):
        ov, gv = o_ref[...], g_ref[...]
        r = lax.rsqrt(jnp.mean(ov * ov, axis=-1, keepdims=True) + EPS)
        out_ref[...] = (ov * r * w_ref[...]) * (gv * _sigmoid(gv))

    return pl.pallas_call(
        body, name="gdn_onorm", grid=(S // tr, H),
        in_specs=[pl.BlockSpec((tr, HEAD), lambda i, h: (i, h)),
                  pl.BlockSpec((tr, HEAD), lambda i, h: (i, gate_blk + h)),
                  pl.BlockSpec((1, HEAD), lambda i, h: (0, 0))],
        out_specs=pl.BlockSpec((tr, HEAD), lambda i, h: (i, h)),
        out_shape=jax.ShapeDtypeStruct((S, H * HEAD), F32),
        compiler_params=_cp(("parallel", "parallel")),
    )(o, proj, w)


def _gdn_onorm_bwd(dcat, o, proj, gate_blk, w, H):
    S = o.shape[0]
    tr = _tile(S, CONV_TILE, 8)

    def body(d_ref, o_ref, g_ref, w_ref, do_ref, dg_ref, dw_ref):
        i, h = pl.program_id(0), pl.program_id(1)

        @pl.when((i == 0) & (h == 0))
        def _():
            dw_ref[...] = jnp.zeros_like(dw_ref)

        dm, ov, gv, wv = d_ref[...], o_ref[...], g_ref[...], w_ref[...]
        r = lax.rsqrt(jnp.mean(ov * ov, axis=-1, keepdims=True) + EPS)
        oh = ov * r
        sg = gv * _sigmoid(gv)
        dy = dm * sg
        t = dy * wv
        do_ref[...] = r * (t - oh * jnp.mean(t * oh, axis=-1, keepdims=True))
        dg_ref[...] = dm * (oh * wv) * _silu_grad(gv)
        dw_ref[...] += jnp.sum(dy * oh, axis=0, keepdims=True)

    tok = pl.BlockSpec((tr, HEAD), lambda i, h: (i, h))
    vec = pl.BlockSpec((1, HEAD), lambda i, h: (0, 0))
    return pl.pallas_call(
        body, name="gdn_onorm_bwd", grid=(S // tr, H),
        in_specs=[tok, tok, pl.BlockSpec((tr, HEAD), lambda i, h: (i, gate_blk + h)), vec],
        out_specs=[tok, tok, vec],
        out_shape=[jax.ShapeDtypeStruct((S, H * HEAD), F32)] * 2 + [jax.ShapeDtypeStruct((1, HEAD), F32)],
        compiler_params=_cp(("arbitrary", "arbitrary")),
    )(dcat, o, proj, w)


def _lanes_to_heads(a, H):
    return a[:, :H].T


def _heads_to_lanes(a):
    H = a.shape[0]
    return jnp.pad(a.T, ((0, 0), (0, 128 - H)))


def _take_cols(segs, a, b):
    out, off = [], 0
    for sg in segs:
        w = sg.shape[-1]
        lo, hi = max(a, off), min(b, off + w)
        if lo < hi:
            out.append(sg[..., lo - off:hi - off])
        off += w
    return out


def _pad_cols(pieces):
    m = jnp.concatenate(pieces, axis=-1)
    return jnp.pad(m, ((0, 0), (0, 128 - m.shape[-1])))


def _pad_lanes(v):
    return jnp.pad(v.reshape(1, -1), ((0, 0), (0, 128 - v.shape[-1])))


def _s5_layer_fwd(a, w, cfg):
    proj = _mm(a, w['w_in'], name="s5_in")
    wb, wc, _ = w['prep']
    pr, pi, prr, pir = w['tables']
    v, yg, hs, cin = _s5_scan_fwd(proj, wb, wc, pr, pi, w['d_skip'])
    z, mix = _mm(yg, w['w_glu'], name="s5_glu", extras=[(yg, 'ij'), (w['b_glu'], 'j')],
                 epi=lambda acc, y, b: (acc + b, y * _sigmoid(acc + b)), out_dtypes=(F32, F32))
    return proj, mix, dict(v=v, yg=yg, hs=hs, cin=cin, z=z)


def _s5_layer_bwd(a, w, proj, sv, dcat, dmemq, cfg):
    wb, wc, _ = w['prep']
    pr, pi, prr, pir = w['tables']
    dz, dyg1, db_glu = _s5_glu_bwd(dcat, sv['yg'], sv['z'])
    dw_glu = _mm(sv['yg'], dz, name="s5_dwglu", ta=True)
    dv = _mm(dz, w['w_glu'], name="s5_dyg", tb=True, extras=[(dyg1, 'ij'), (sv['v'], 'ij')],
             epi=lambda acc, d1, vv: ((acc + d1) * _gelu_grad(vv),))
    du, dwb, dwc, da, dd = _s5_scan_bwd(dv, proj, sv['hs'], sv['cin'], wb, wc, pr, pi, prr, pir, w['d_skip'])
    dproj = jnp.concatenate([du, dmemq], axis=1).astype(MXU_DTYPE)
    dw_in = _mm(a, dproj, name="s5_dwin", ta=True)
    da_in = _mm(dproj, w['w_in'], name="s5_da", tb=True)
    dlre, dlim, dldt, dbre, dbim, dcre, dcim = w['prep_vjp']((dwb, dwc, da))
    grads = dict(w_in=dw_in, w_glu=dw_glu, b_glu=db_glu[0], d_skip=dd[0], lam_re=dlre, lam_im=dlim,
                 log_dt=dldt, b_re=dbre, b_im=dbim, c_re=dcre, c_im=dcim)
    return da_in, grads


def _gdn_relayout(a, H, NC):
    t = _lanes_to_heads(a, H).reshape(H, NC, GDN_CHUNK)
    return t[..., None], t[:, :, None, :]


def _gdn_layer_fwd(a, w, cfg):
    H, MIX, S = cfg['H'], cfg['MIX'], a.shape[0]
    NC = S // GDN_CHUNK
    proj = _mm(a, w['w_main'], name="gdn_in")
    pg = _mm(a, w['w_gate'], name="gdn_in_gates")
    cw = w['conv_w']
    q = _gdn_prep(proj, 0, H, cw[:, :MIX], True, HEAD ** -0.5, "gdn_prep_q")
    k = _gdn_prep(proj, H, H, cw[:, MIX:2 * MIX], True, 1.0, "gdn_prep_k")
    v = _gdn_prep(proj, 2 * H, H, cw[:, 2 * MIX:], False, 1.0, "gdn_prep_v")
    gc, beta = _gdn_gates(pg, w['a_log'], w['dt_bias'])
    gcol, grow = _gdn_relayout(gc, H, NC)
    bcol, _ = _gdn_relayout(beta, H, NC)
    o, st = _gdn_chunk_fwd(q, k, v, gcol, grow, bcol)
    mix = _gdn_onorm(o, proj, 3 * H, w['o_norm'], H)
    return proj, mix, dict(pg=pg, q=q, k=k, v=v, gcol=gcol, grow=grow, bcol=bcol, o=o, st=st)


def _gdn_layer_bwd(a, w, proj, sv, dcat, dmemq, cfg):
    H, MIX, S = cfg['H'], cfg['MIX'], a.shape[0]
    cw = w['conv_w']
    do, dgate, donorm = _gdn_onorm_bwd(dcat, sv['o'], proj, 3 * H, w['o_norm'], H)
    dq, dk, dv, dgcol, dbcol = _gdn_chunk_bwd(sv['q'], sv['k'], sv['v'], sv['gcol'], sv['grow'], sv['bcol'],
                                              sv['st'], do)
    dgc = _heads_to_lanes(dgcol.reshape(H, S))
    dbeta = _heads_to_lanes(dbcol.reshape(H, S))
    dpg, dalog, ddtb = _gdn_gates_bwd(sv['pg'], w['a_log'], w['dt_bias'], dgc, dbeta)
    dxq, dwq = _gdn_prep_bwd(proj, 0, H, cw[:, :MIX], dq, True, HEAD ** -0.5, "gdn_prep_bwd_q")
    dxk, dwk = _gdn_prep_bwd(proj, H, H, cw[:, MIX:2 * MIX], dk, True, 1.0, "gdn_prep_bwd_k")
    dxv, dwv = _gdn_prep_bwd(proj, 2 * H, H, cw[:, 2 * MIX:], dv, False, 1.0, "gdn_prep_bwd_v")
    dproj = jnp.concatenate([dxq, dxk, dxv, dgate, dmemq], axis=1).astype(MXU_DTYPE)
    dw_main = _mm(a, dproj, name="gdn_dwmain", ta=True)
    dw_gate = _mm(a, dpg, name="gdn_dwgate", ta=True)
    da1 = _mm(dpg, w['w_gate'], name="gdn_da_gates", tb=True)
    da_in = _mm(dproj, w['w_main'], name="gdn_da", tb=True, extras=[(da1, 'ij')], epi=lambda acc, e: (acc + e,))
    grads = dict(w_main=dw_main, w_gate=dw_gate, conv_w=jnp.concatenate([dwq, dwk, dwv], axis=1),
                 a_log=dalog[0, :H], dt_bias=ddtb[0, :H], o_norm=donorm[0])
    return da_in, grads


def _fox_layer_fwd(a, w, cfg):
    H = cfg['H']
    proj = _mm(a, w['w_main'], name="fox_in")
    pg = _mm(a, w['w_gate'], name="fox_in_gates")
    cf = _fox_gates(pg, w['b_f'])
    cfh = _lanes_to_heads(cf, H)
    cfq, cfk = cfh[:, :, None], cfh[:, None, :]
    o, lse = _fox_fwd(proj, cfq, cfk, H)
    return proj, o, dict(pg=pg, cfq=cfq, cfk=cfk, lse=lse)


def _fox_layer_bwd(a, w, proj, sv, dcat, dmemq, cfg):
    H = cfg['H']
    rowdot = _fox_bwd_rowdot(proj, sv['cfq'], sv['cfk'], sv['lse'], dcat, H)
    dq, dk, dv, dck = _fox_bwd(proj, sv['cfq'], sv['cfk'], rowdot, sv['lse'], dcat, H)
    dpg, dbf = _fox_gates_bwd(sv['pg'], w['b_f'], _heads_to_lanes(dck[:, 0, :]))
    dproj = jnp.concatenate([dq, dk, dv, dmemq], axis=1).astype(MXU_DTYPE)
    dw_main = _mm(a, dproj, name="fox_dwmain", ta=True)
    dw_gate = _mm(a, dpg, name="fox_dwgate", ta=True)
    da1 = _mm(dpg, w['w_gate'], name="fox_da_gates", tb=True)
    da_in = _mm(dproj, w['w_main'], name="fox_da", tb=True, extras=[(da1, 'ij')], epi=lambda acc, e: (acc + e,))
    grads = dict(w_main=dw_main, w_gate=dw_gate, b_f=dbf[0, :H])
    return da_in, grads


_LAYER_FWD = (_s5_layer_fwd, _gdn_layer_fwd, _fox_layer_fwd)
_LAYER_BWD = (_s5_layer_bwd, _gdn_layer_bwd, _fox_layer_bwd)


def _mixer_weights(kind, j, fw, p, cfg, after):
    H, MIX, MW = cfg['H'], cfg['MIX'], cfg['MW']
    if kind == 0:
        params = tuple(p[n][j] for n in ('s5_lam_re', 's5_lam_im', 's5_log_dt', 's5_b_re', 's5_b_im',
                                         's5_c_re', 's5_c_im'))
        prep, prep_vjp = jax.vjp(_s5_prep, *params)
        prep = (prep[0].astype(MXU_DTYPE), prep[1].astype(MXU_DTYPE), prep[2])
        tables = _s5_tables(*params[:3])
        return dict(w_in=fw.get('s5_w_in', j, after), w_glu=fw.get('s5_w_glu', j, after),
                    b_glu=fw.get('s5_b_glu', j, after), d_skip=fw.get('s5_d_skip', j, after).reshape(1, MIX),
                    prep=prep, prep_vjp=prep_vjp, tables=tables)
    if kind == 1:
        segs = fw.get('gdn_w_in', j, after)
        c0 = 4 * MIX
        total = c0 + 2 * H + MW
        w_main = jnp.concatenate(_take_cols(segs, 0, c0) + _take_cols(segs, c0 + 2 * H, total), axis=1)
        w_gate = jnp.concatenate([_pad_cols(_take_cols(segs, c0, c0 + H)),
                                  _pad_cols(_take_cols(segs, c0 + H, c0 + 2 * H))], axis=1)
        return dict(w_main=w_main, w_gate=w_gate, conv_w=fw.get('gdn_conv_w', j, after),
                    a_log=_pad_lanes(p['gdn_a_log'][j]), dt_bias=_pad_lanes(p['gdn_dt_bias'][j]),
                    o_norm=p['gdn_o_norm'][j].reshape(1, HEAD))
    segs = fw.get('fox_w_in', j, after)
    c0 = 3 * MIX
    total = c0 + H + MW
    w_main = jnp.concatenate(_take_cols(segs, 0, c0) + _take_cols(segs, c0 + H, total), axis=1)
    w_gate = _pad_cols(_take_cols(segs, c0, c0 + H))
    return dict(w_main=w_main, w_gate=w_gate, b_f=_pad_lanes(p['fox_b_f'][j]))


class _Weights:
    def __init__(self, resolve):
        self._resolve, self._have = resolve, {}

    def get(self, name, layer, after):
        if (name, layer) not in self._have:
            self._have[name, layer] = self._resolve(name, layer, after)
        return self._have[name, layer]


def _local_step(p, fw, cfg, on_grad=None):
    H, MIX, MW, MH, depth = cfg['H'], cfg['MIX'], cfg['MW'], cfg['MH'], cfg['depth']
    x, mem, target = p['x'], p['mem'], p['loss_target']
    q_blk = {0: MIX // HEAD, 1: 4 * MIX // HEAD, 2: 3 * MIX // HEAD}
    zero = jnp.zeros((), F32)
    tok = [zero]

    def told(name, layer, value):
        if on_grad is not None:
            tok[0] = tok[0] + on_grad(name, layer, value)
        return value

    mem_n = _rms_fwd(mem, p['mem_norm'], MXU_DTYPE, "mem_rms")
    w_kv = fw.get('w_mem_kv', 0, mem_n)
    mkv = _mm(mem_n, w_kv, name="mem_kv")

    h = x
    saved = []
    for i in range(depth):
        kind, j = i % 3, i // 3
        a = _rms_fwd(h, p['norm1'][i], MXU_DTYPE, "rms1")
        w = _mixer_weights(kind, j, fw, p, cfg, a)
        proj, mix, sv = _LAYER_FWD[kind](a, w, cfg)
        read = _mem_fwd(proj, q_blk[kind], mkv, MH)
        cat = jnp.concatenate([mix, read], axis=1).astype(MXU_DTYPE)
        w_out, w_up = fw.get('w_out', i, proj), fw.get('w_up', i, proj)
        h1 = _mm(cat, w_out, name="out_proj", extras=[(h, 'ij')], epi=lambda acc, r: (acc + r,))
        a2 = _rms_fwd(h1, p['norm2'][i], MXU_DTYPE, "rms2")
        act = _mm(a2, w_up, name="mlp_up", epi=lambda acc: (_relu2(acc),), out_dtypes=(MXU_DTYPE,))
        w_down = fw.get('w_down', i, h1)
        h2 = _mm(act, w_down, name="mlp_down", extras=[(h1, 'ij')], epi=lambda acc, r: (acc + r,))
        saved.append(dict(w=w, h=h, a=a, proj=proj, sv=sv, cat=cat, h1=h1, a2=a2, act=act,
                          w_out=w_out, w_up=w_up, w_down=w_down))
        h = h2

    loss, dh, dnorm_f, dh16 = _loss_head(h, p['norm_f'], target)

    g = {n: None for n in WEIGHTS}
    g['norm_f'] = dnorm_f[0]
    per_layer = {n: [None] * depth for n in ('norm1', 'norm2', 'w_out', 'w_up', 'w_down')}
    mix_grads = {0: {}, 1: {}, 2: {}}
    big = {0: (('s5_w_in', 'w_in'), ('s5_w_glu', 'w_glu')), 1: (), 2: ()}
    dmkv = None
    for i in reversed(range(depth)):
        kind, j = i % 3, i // 3
        s = saved[i]
        w = s['w']
        du = _mm(dh16, s['w_down'], name="mlp_ddown", tb=True, extras=[(s['act'], 'ij')],
                 epi=lambda acc, aa: (acc * (2.0 * jnp.sqrt(aa.astype(F32))),), out_dtypes=(MXU_DTYPE,))
        per_layer['w_down'][i] = told('w_down', i, _mm(s['act'], dh16, name="mlp_dwdown", ta=True))
        per_layer['w_up'][i] = told('w_up', i, _mm(s['a2'], du, name="mlp_dwup", ta=True))
        da2 = _mm(du, s['w_up'], name="mlp_dup", tb=True)
        dh1, dn2, dh1_16 = _rms_bwd(s['h1'], p['norm2'][i] + tok[0], da2, dh, "rms2_bwd")
        per_layer['norm2'][i] = dn2[0]
        dcat = _mm(dh1_16, s['w_out'], name="out_dproj", tb=True)
        per_layer['w_out'][i] = told('w_out', i, _mm(s['cat'], dh1_16, name="out_dw", ta=True))
        dmemq, dmkv_i = _mem_bwd(s['proj'], q_blk[kind], mkv, dcat, MIX // HEAD, MH)
        dmkv = dmkv_i if dmkv is None else dmkv + dmkv_i
        da, mg = _LAYER_BWD[kind](s['a'], w, s['proj'], s['sv'], dcat, dmemq, cfg)
        mix_grads[kind][j] = mg
        for name, key in big[kind]:
            told(name, j, mg[key])
        c0 = 4 * MIX
        if kind == 1:
            mg['segs'] = told('gdn_w_in', j, [mg['w_main'][:, :c0], mg['w_gate'][:, :H],
                                               mg['w_gate'][:, 128:128 + H], mg['w_main'][:, c0:]])
        c0 = 3 * MIX
        if kind == 2:
            mg['segs'] = told('fox_w_in', j, [mg['w_main'][:, :c0], mg['w_gate'][:, :H], mg['w_main'][:, c0:]])
        dh, dn1, dh16 = _rms_bwd(s['h'], p['norm1'][i] + tok[0], da, dh1, "rms1_bwd")
        per_layer['norm1'][i] = dn1[0]
    for n in ('norm1', 'norm2'):
        g[n] = jnp.stack(per_layer[n])
    for n in ('w_out', 'w_up', 'w_down'):
        g[n] = per_layer[n]

    g['w_mem_kv'] = told('w_mem_kv', 0, _mm(mem_n, dmkv, name="mem_dwkv", ta=True))
    dmem_n = _mm(dmkv, w_kv, name="mem_dn", tb=True)
    _, dmn, _ = _rms_bwd(mem, p['mem_norm'] + tok[0], dmem_n, None, "mem_rms_bwd")
    g['mem_norm'] = dmn[0]

    def layers(kind, key):
        return [mix_grads[kind][j][key] for j in sorted(mix_grads[kind])]

    g['s5_w_in'] = layers(0, 'w_in')
    g['s5_w_glu'] = layers(0, 'w_glu')
    for n in ('b_glu', 'd_skip', 'lam_re', 'lam_im', 'log_dt', 'b_re', 'b_im', 'c_re', 'c_im'):
        g['s5_' + n] = jnp.stack(layers(0, n))
    g['gdn_w_in'] = layers(1, 'segs')
    for n in ('conv_w', 'a_log', 'dt_bias', 'o_norm'):
        g['gdn_' + n] = jnp.stack(layers(1, n))
    g['fox_w_in'] = layers(2, 'segs')
    g['fox_b_f'] = jnp.stack(layers(2, 'b_f'))
    return loss, dh, g


def _ag_order(p):
    depth, order = p['norm1'].shape[0], [('w_mem_kv', 0)]
    for i in range(depth):
        kind, j = i % 3, i // 3
        order += [[('s5_w_in', j), ('s5_w_glu', j)], [('gdn_w_in', j)], [('fox_w_in', j)]][kind]
        order += [('w_out', i), ('w_up', i), ('w_down', i)]
    return order


def _gather_begin(p, me):
    vec = _gather(_pack_small([p[n] for n in VECTOR_SHARDED], 16), me, "all_gather_vectors").reshape(4, -1)
    first = (vec[0, 0] * 0.0).astype(MXU_DTYPE)
    handles, tok = {}, jnp.zeros((), F32)
    for name, layer in _ag_order(p):
        xs = (p[name] if p[name].ndim == 2 else p[name][layer]).astype(MXU_DTYPE) + first
        handle, t = _ag_start(xs, "ag_start_%s_%d" % (name, layer))
        handles[name, layer] = (handle, xs)
        tok = tok + t
    vectors, off = {}, 0
    for n in VECTOR_SHARDED:
        sz = p[n].size
        stacked = vec[:, off:off + sz].reshape((4,) + p[n].shape)
        ax = SHARD_AXIS[n]
        t = jnp.moveaxis(stacked, 0, ax)
        shp = list(t.shape)
        vectors[n] = t.reshape(shp[:ax] + [shp[ax] * shp[ax + 1]] + shp[ax + 2:])
        off += sz
    return handles, vectors, tok


def _gather_end(name, layer, after, me_idx, handles, vectors):
    if name in vectors:
        return vectors[name][layer]
    handle, xs = handles[name, layer]
    tag = "%s_%d" % (name, layer)
    got = _ag_wait(handle, after, "ag_wait_" + tag)
    fwd = _ag_forward_start(got, "ag_forward_start_" + tag)
    got = _ag_fill_own(fwd[2], xs, me_idx, "ag_fill_own")
    got = _ag_forward_wait((fwd[0], fwd[1], got), "ag_forward_wait_" + tag)
    r, n = xs.shape
    if name == 'w_mem_kv' or SHARD_AXIS[name] == 1:
        return got.reshape(4 * r, n)
    blocks = [got[s] for s in range(4)]
    return jnp.concatenate(blocks, axis=1) if name == 'w_up' else blocks


def _shard_blocks(name, value, p):
    shp = p[name].shape
    r, n = shp[-2], shp[-1]
    if SHARD_AXIS[name] == len(shp) - 2:
        return value.reshape(4, r, n)
    segs = value if isinstance(value, list) else [value]
    return jnp.stack([jnp.concatenate(_take_cols(segs, s * n, (s + 1) * n), axis=1) for s in range(4)])


def kernel(x, mem, mem_norm, w_mem_kv, norm1, w_out, norm2, w_up, w_down, norm_f, s5_w_in, s5_lam_re, s5_lam_im, s5_log_dt, s5_b_re, s5_b_im, s5_c_re, s5_c_im, s5_d_skip, s5_w_glu, s5_b_glu, gdn_w_in, gdn_conv_w, gdn_a_log, gdn_dt_bias, gdn_o_norm, fox_w_in, fox_b_f, loss_target, m_mem_norm, m_w_mem_kv, m_norm1, m_w_out, m_norm2, m_w_up, m_w_down, m_norm_f, m_s5_w_in, m_s5_lam_re, m_s5_lam_im, m_s5_log_dt, m_s5_b_re, m_s5_b_im, m_s5_c_re, m_s5_c_im, m_s5_d_skip, m_s5_w_glu, m_s5_b_glu, m_gdn_w_in, m_gdn_conv_w, m_gdn_a_log, m_gdn_dt_bias, m_gdn_o_norm, m_fox_w_in, m_fox_b_f, v_mem_norm, v_w_mem_kv, v_norm1, v_w_out, v_norm2, v_w_up, v_w_down, v_norm_f, v_s5_w_in, v_s5_lam_re, v_s5_lam_im, v_s5_log_dt, v_s5_b_re, v_s5_b_im, v_s5_c_re, v_s5_c_im, v_s5_d_skip, v_s5_w_glu, v_s5_b_glu, v_gdn_w_in, v_gdn_conv_w, v_gdn_a_log, v_gdn_dt_bias, v_gdn_o_norm, v_fox_w_in, v_fox_b_f):
    args = locals()
    p = {n: args[n] for n in WEIGHTS}
    mom = {n: args['m_' + n] for n in WEIGHTS}
    var = {n: args['v_' + n] for n in WEIGHTS}
    S, D = x.shape[1], x.shape[2]
    MW = w_mem_kv.shape[1] // 2
    MIX = D - MW
    cfg = dict(H=MIX // HEAD, MIX=MIX, MW=MW, MH=MW // HEAD, depth=norm1.shape[0])
    p.update(x=x.reshape(S, D), mem=mem.reshape(mem.shape[1], D), loss_target=loss_target.reshape(S, D))
    c = lax.axis_index("c")
    me = 2 * lax.axis_index("x") + lax.axis_index("y")
    place = dict(c=c, c_idx=c.astype(jnp.int32).reshape(1), me_idx=me.astype(jnp.int32).reshape(1))

    handles, vectors, tok = _gather_begin(p, me)
    p['mem_norm'] = mem_norm + tok
    fw = _Weights(lambda name, layer, after: _gather_end(name, layer, after, place['me_idx'], handles, vectors))

    pending, swapping = {}, []

    def exchange(after):
        name, layer, swap = swapping.pop()
        pending[name, layer], t = _rs_begin(swap, after, place, "%s_%d" % (name, layer))
        return t

    def on_grad(name, layer, value):
        swap, t = _rs_swap_start(_shard_blocks(name, value, p), "rs_swap_start_%s_%d" % (name, layer))
        if swapping:
            t = t + exchange(value[0] if isinstance(value, list) else value)
        swapping.append((name, layer, swap))
        return t

    loss, dx, g = _local_step(p, fw, cfg, on_grad)
    p['mem_norm'] = mem_norm
    exchange(dx)

    grads = {}
    for name in MATMUL_WEIGHTS:
        shp = p[name].shape
        layers = [_rs_end(pending[name, i], dx, place, "%s_%d" % (name, i))
                  for i in range(1 if len(shp) == 2 else shp[0])]
        grads[name] = layers[0] if len(shp) == 2 else jnp.stack(layers)

    parts = []
    for name in VECTOR_SHARDED:
        ax = SHARD_AXIS[name]
        shp = list(g[name].shape)
        t = g[name].reshape(shp[:ax] + [4, shp[ax] // 4] + shp[ax + 1:])
        parts.append(jnp.moveaxis(t, ax, 0).reshape(4, -1))
    flat = jnp.concatenate(parts, axis=1)
    flat = jnp.pad(flat, ((0, 0), (0, 16 * LANES - flat.shape[1]))).reshape(4, 16, LANES)
    swap, _ = _rs_swap_start(flat, "rs_swap_start_vectors")
    handle, _ = _rs_begin(swap, dx, place, "vectors")
    red = _rs_end(handle, dx, place, "vectors").reshape(-1)
    off = 0
    for name in VECTOR_SHARDED:
        grads[name] = red[off:off + p[name].size].reshape(p[name].shape)
        off += p[name].size

    n_small = sum(p[n].size for n in REPLICATED)
    rows = -(-n_small // LANES // 8) * 8
    small = _all_reduce_small(_pack_small([g[n] for n in REPLICATED], rows), "all_reduce_small").reshape(-1)
    off = 0
    for n in REPLICATED:
        grads[n] = small[off:off + p[n].size].reshape(p[n].shape)
        off += p[n].size

    delta, new_m, new_v = {}, {}, {}
    for n in SHARD_AXIS:
        shp = p[n].shape
        two_d = (-1, shp[-1])
        d, nm, nv = _adamw(p[n].reshape(two_d), grads[n].reshape(two_d), mom[n].reshape(two_d),
                           var[n].reshape(two_d), "adamw_" + n)
        delta[n], new_m[n], new_v[n] = d.reshape(shp), nm.reshape(shp), nv.reshape(shp)
    d, nm, nv = _adamw(*[_pack_small([src[n] for n in REPLICATED], rows) for src in (p, grads, mom, var)],
                       "adamw_small")
    d, nm, nv = d.reshape(-1), nm.reshape(-1), nv.reshape(-1)
    off = 0
    for n in REPLICATED:
        sz, shp = p[n].size, p[n].shape
        delta[n], new_m[n], new_v[n] = (d[off:off + sz].reshape(shp), nm[off:off + sz].reshape(shp),
                                        nv[off:off + sz].reshape(shp))
        off += sz

    total = lax.psum(loss[0, 0], ("x", "y", "c"))
    return (total, dx.reshape(x.shape), *[grads[n] for n in WEIGHTS], *[delta[n] for n in WEIGHTS],
            *[new_m[n] for n in WEIGHTS], *[new_v[n] for n in WEIGHTS])
```

```python
import math

import jax
import jax.numpy as jnp
import numpy as np
from jax import lax
from jax.experimental import pallas as pl
from jax.experimental.pallas import tpu as pltpu

F32 = jnp.float32
MXU_DTYPE = jnp.bfloat16
EPS = 1e-6
HEAD = 128
S5_GROUP = 16
S5_STATE = 64
S5_SLAB = 256
S5_CHUNK = 128
S5_ROWS = 8
GDN_CHUNK = 64
GDN_CONV = 4
LANES = 1024
VMEM_LIMIT_BYTES = 56 * 1024 * 1024
MESH = pl.DeviceIdType.MESH
RS_PAYLOAD = jnp.bfloat16
HBM_SPEC = pl.BlockSpec(memory_space=pltpu.HBM)
SEM_SPEC = pl.BlockSpec(memory_space=pltpu.SEMAPHORE)
SPLIT_EFFECT = pltpu.SideEffectType.DATAFLOW_SIDE_EFFECTING

ADAM_LR, ADAM_B1, ADAM_B2, ADAM_EPS, ADAM_WD, ADAM_STEP = 0.001, 0.9, 0.999, 1e-08, 0.01, 10

MM_TM, MM_TN, MM_TK = 1024, 1024, 1024
ROW_TILE = 256
FOX_TILE = 512
MEM_TILE = 512
CONV_TILE = 1024

NN = (((1,), (0,)), ((), ()))
NT = (((1,), (1,)), ((), ()))
TN = (((0,), (0,)), ((), ()))

WEIGHTS = ['mem_norm', 'w_mem_kv', 'norm1', 'w_out', 'norm2', 'w_up', 'w_down', 'norm_f', 's5_w_in',
           's5_lam_re', 's5_lam_im', 's5_log_dt', 's5_b_re', 's5_b_im', 's5_c_re', 's5_c_im', 's5_d_skip',
           's5_w_glu', 's5_b_glu', 'gdn_w_in', 'gdn_conv_w', 'gdn_a_log', 'gdn_dt_bias', 'gdn_o_norm',
           'fox_w_in', 'fox_b_f']
SHARD_AXIS = {'w_mem_kv': 0, 'w_out': 1, 'w_up': 2, 'w_down': 1, 's5_w_in': 1, 's5_d_skip': 1,
              's5_w_glu': 1, 's5_b_glu': 1, 'gdn_w_in': 2, 'gdn_conv_w': 2, 'fox_w_in': 2}
MATMUL_WEIGHTS = ['w_mem_kv', 'w_out', 'w_up', 'w_down', 's5_w_in', 's5_w_glu', 'gdn_w_in', 'fox_w_in']
VECTOR_SHARDED = ['s5_d_skip', 's5_b_glu', 'gdn_conv_w']
REPLICATED = [n for n in WEIGHTS if n not in SHARD_AXIS]


def _tile(dim, target, align=128):
    if dim <= target:
        return dim
    t = (target // align) * align
    while t >= align:
        if dim % t == 0:
            return t
        t -= align
    return dim


def _cp(sem=None, **kw):
    return pltpu.CompilerParams(dimension_semantics=sem, vmem_limit_bytes=VMEM_LIMIT_BYTES, **kw)


def _dot(a, b, dims):
    return lax.dot_general(a.astype(MXU_DTYPE), b.astype(MXU_DTYPE), dims, preferred_element_type=F32)


def _dotf(a, b, dims):
    return lax.dot_general(a, b, dims, precision=lax.Precision.HIGHEST, preferred_element_type=F32)


def _sigmoid(x):
    return 1.0 / (1.0 + jnp.exp(-x))


def _softplus(x):
    return jnp.maximum(x, 0.0) + jnp.log(1.0 + jnp.exp(-jnp.abs(x)))


def _relu2(x):
    r = jnp.maximum(x, 0.0)
    return r * r


_GELU_C = math.sqrt(2.0 / math.pi)


def _gelu(x):
    return 0.5 * x * (1.0 + jnp.tanh(_GELU_C * (x + 0.044715 * x * x * x)))


def _gelu_grad(x):
    t = jnp.tanh(_GELU_C * (x + 0.044715 * x * x * x))
    return 0.5 * (1.0 + t) + 0.5 * x * (1.0 - t * t) * _GELU_C * (1.0 + 3.0 * 0.044715 * x * x)


def _silu_grad(x):
    s = _sigmoid(x)
    return s + x * s * (1.0 - s)


def _mm(a, b, *, name, ta=False, tb=False, extras=(), epi=None, out_dtypes=(F32,)):
    K, M = a.shape if ta else a.shape[::-1]
    N = b.shape[0] if tb else b.shape[1]
    assert (b.shape[1] if tb else b.shape[0]) == K, (a.shape, b.shape, ta, tb)
    tm, tn, tk = _tile(M, MM_TM), _tile(N, MM_TN), _tile(K, MM_TK)
    nk = K // tk
    n_ex, n_out = len(extras), len(out_dtypes)
    dims = TN if ta else (NT if tb else NN)

    def body(*refs):
        a_ref, b_ref = refs[0], refs[1]
        ex = refs[2:2 + n_ex]
        outs = refs[2 + n_ex:2 + n_ex + n_out]
        acc = refs[-1]
        k = pl.program_id(2)

        @pl.when(k == 0)
        def _():
            acc[...] = jnp.zeros_like(acc)

        acc[...] += _dot(a_ref[...], b_ref[...], dims)

        @pl.when(k == nk - 1)
        def _():
            res = acc[...]
            vals = epi(res, *[e[...] for e in ex]) if epi is not None else (res,)
            for o, v in zip(outs, vals):
                o[...] = v.astype(o.dtype)

    if ta:
        a_spec = pl.BlockSpec((tk, tm), lambda i, j, k: (k, i))
    else:
        a_spec = pl.BlockSpec((tm, tk), lambda i, j, k: (i, k))
    if tb:
        b_spec = pl.BlockSpec((tn, tk), lambda i, j, k: (j, k))
    else:
        b_spec = pl.BlockSpec((tk, tn), lambda i, j, k: (k, j))
    ex_specs, ex_arrays = [], []
    for arr, kind in extras:
        if kind == 'ij':
            ex_specs.append(pl.BlockSpec((tm, tn), lambda i, j, k: (i, j)))
            ex_arrays.append(arr)
        else:
            ex_specs.append(pl.BlockSpec((1, tn), lambda i, j, k: (0, j)))
            ex_arrays.append(arr.reshape(1, N))
    outs = pl.pallas_call(
        body, name=name, grid=(M // tm, N // tn, nk),
        in_specs=[a_spec, b_spec] + ex_specs,
        out_specs=[pl.BlockSpec((tm, tn), lambda i, j, k: (i, j)) for _ in out_dtypes],
        out_shape=[jax.ShapeDtypeStruct((M, N), dt) for dt in out_dtypes],
        scratch_shapes=[pltpu.VMEM((tm, tn), F32)],
        compiler_params=_cp(("parallel", "parallel", "arbitrary")),
    )(a, b, *ex_arrays)
    return outs[0] if n_out == 1 else tuple(outs)


def _rms_fwd(x, g, out_dtype, name):
    S, D = x.shape
    tr = _tile(S, ROW_TILE, 8)

    def body(x_ref, g_ref, o_ref):
        xv = x_ref[...]
        r = lax.rsqrt(jnp.mean(xv * xv, axis=-1, keepdims=True) + EPS)
        o_ref[...] = (xv * r * g_ref[...]).astype(o_ref.dtype)

    return pl.pallas_call(
        body, name=name, grid=(S // tr,),
        in_specs=[pl.BlockSpec((tr, D), lambda i: (i, 0)), pl.BlockSpec((1, D), lambda i: (0, 0))],
        out_specs=pl.BlockSpec((tr, D), lambda i: (i, 0)),
        out_shape=jax.ShapeDtypeStruct((S, D), out_dtype),
        compiler_params=_cp(("parallel",)),
    )(x, g.reshape(1, D))


def _rms_bwd(x, g, dy, res, name):
    S, D = x.shape
    tr = _tile(S, ROW_TILE, 8)
    has_res = res is not None

    def body(*refs):
        if has_res:
            x_ref, g_ref, dy_ref, res_ref, dx_ref, dg_ref, dx16_ref = refs
        else:
            x_ref, g_ref, dy_ref, dx_ref, dg_ref, dx16_ref = refs
        i = pl.program_id(0)

        @pl.when(i == 0)
        def _():
            dg_ref[...] = jnp.zeros_like(dg_ref)

        xv, d = x_ref[...], dy_ref[...].astype(F32)
        r = lax.rsqrt(jnp.mean(xv * xv, axis=-1, keepdims=True) + EPS)
        xh = xv * r
        t = d * g_ref[...]
        dx = r * (t - xh * jnp.mean(t * xh, axis=-1, keepdims=True))
        if has_res:
            dx = dx + res_ref[...]
        dx_ref[...] = dx
        dx16_ref[...] = dx.astype(dx16_ref.dtype)
        dg_ref[...] += jnp.sum(d * xh, axis=0, keepdims=True)

    row = pl.BlockSpec((tr, D), lambda i: (i, 0))
    vec = pl.BlockSpec((1, D), lambda i: (0, 0))
    ins = [x, g.reshape(1, D), dy] + ([res] if has_res else [])
    return pl.pallas_call(
        body, name=name, grid=(S // tr,),
        in_specs=[row, vec, row] + ([row] if has_res else []),
        out_specs=[row, vec, row],
        out_shape=[jax.ShapeDtypeStruct((S, D), F32), jax.ShapeDtypeStruct((1, D), F32),
                   jax.ShapeDtypeStruct((S, D), MXU_DTYPE)],
        compiler_params=_cp(("arbitrary",)),
    )(*ins)


def _loss_head(h, g, target):
    S, D = h.shape
    tr = _tile(S, ROW_TILE, 8)

    def body(h_ref, g_ref, t_ref, loss_ref, dh_ref, dg_ref, dh16_ref):
        i = pl.program_id(0)

        @pl.when(i == 0)
        def _():
            loss_ref[...] = jnp.zeros_like(loss_ref)
            dg_ref[...] = jnp.zeros_like(dg_ref)

        xv = h_ref[...]
        gv = g_ref[...]
        r = lax.rsqrt(jnp.mean(xv * xv, axis=-1, keepdims=True) + EPS)
        xh = xv * r
        err = xh * gv - t_ref[...]
        part = 0.5 * jnp.sum(jnp.mean(err * err, axis=-1, keepdims=True), axis=0, keepdims=True)
        loss_ref[...] += jnp.broadcast_to(part, loss_ref.shape)
        d = err * (1.0 / D)
        t = d * gv
        dh = r * (t - xh * jnp.mean(t * xh, axis=-1, keepdims=True))
        dh_ref[...] = dh
        dh16_ref[...] = dh.astype(dh16_ref.dtype)
        dg_ref[...] += jnp.sum(d * xh, axis=0, keepdims=True)

    row = pl.BlockSpec((tr, D), lambda i: (i, 0))
    vec = pl.BlockSpec((1, D), lambda i: (0, 0))
    return pl.pallas_call(
        body, name="loss_head", grid=(S // tr,),
        in_specs=[row, vec, row],
        out_specs=[pl.BlockSpec((8, 128), lambda i: (0, 0)), row, vec, row],
        out_shape=[jax.ShapeDtypeStruct((8, 128), F32), jax.ShapeDtypeStruct((S, D), F32),
                   jax.ShapeDtypeStruct((1, D), F32), jax.ShapeDtypeStruct((S, D), MXU_DTYPE)],
        compiler_params=_cp(("arbitrary",)),
    )(h, g.reshape(1, D), target)


def _adamw(w, g, m, v, name):
    R, C = w.shape
    tr = _tile(R, max(8, (1 << 19) // max(C, 1) // 8 * 8), 8)
    c1 = 1.0 / (1.0 - ADAM_B1 ** ADAM_STEP)
    c2 = 1.0 / (1.0 - ADAM_B2 ** ADAM_STEP)

    def body(w_ref, g_ref, m_ref, v_ref, d_ref, nm_ref, nv_ref):
        gv = g_ref[...]
        nm = ADAM_B1 * m_ref[...] + (1.0 - ADAM_B1) * gv
        nv = ADAM_B2 * v_ref[...] + (1.0 - ADAM_B2) * (gv * gv)
        d_ref[...] = -ADAM_LR * ((nm * c1) / (jnp.sqrt(nv * c2) + ADAM_EPS) + ADAM_WD * w_ref[...])
        nm_ref[...] = nm
        nv_ref[...] = nv

    blk = pl.BlockSpec((tr, C), lambda i: (i, 0))
    return pl.pallas_call(
        body, name=name, grid=(R // tr,),
        in_specs=[blk] * 4, out_specs=[blk] * 3,
        out_shape=[jax.ShapeDtypeStruct((R, C), F32)] * 3,
        compiler_params=_cp(("parallel",)),
    )(w, g, m, v)


def _place():
    x, y, c = lax.axis_index("x"), lax.axis_index("y"), lax.axis_index("c")
    chips = [(1 - x, y), (x, 1 - y), (1 - x, 1 - y)]
    return x, y, c, chips


def _in_hbm(a):
    return pltpu.with_memory_space_constraint(a, pltpu.HBM)


def _ag_start(xs, name):
    r, n = xs.shape
    half = r // 2

    def body(x_ref, land_ref, send_sems, recv_sems, x_thru, land_thru, token):
        x, y, c, chips = _place()
        rows = pl.ds(c * half, half)
        for j, (cx, cy) in enumerate(chips):
            pltpu.make_async_remote_copy(
                src_ref=x_ref.at[rows, :], dst_ref=land_ref.at[2 * x + y, rows, :], send_sem=send_sems.at[j],
                recv_sem=recv_sems.at[j], device_id=(cx, cy, c), device_id_type=MESH).start()
        token[...] = jnp.zeros_like(token)

    sems = pltpu.SemaphoreType.DMA((3,))
    out = pl.pallas_call(
        body, name=name,
        out_shape=(sems, sems, pltpu.HBM(xs.shape, xs.dtype), pltpu.HBM((4, r, n), xs.dtype),
                   jax.ShapeDtypeStruct((8, 128), F32)),
        in_specs=(HBM_SPEC, HBM_SPEC),
        out_specs=(SEM_SPEC, SEM_SPEC, HBM_SPEC, HBM_SPEC, pl.BlockSpec(memory_space=pltpu.VMEM)),
        input_output_aliases={0: 2, 1: 3},
        compiler_params=pltpu.CompilerParams(has_side_effects=SPLIT_EFFECT),
    )(_in_hbm(xs), _in_hbm(lax.empty((4, r, n), xs.dtype)))
    return out[:4], out[4][0, 0]


def _ag_wait(handle, after, name):
    send_sems, recv_sems, xs, land = handle
    r, n = xs.shape
    half = r // 2

    def body(x_ref, land_ref, send_sems, recv_sems, after_ref, x_out, land_out):
        x, y, c, chips = _place()
        rows = pl.ds(c * half, half)
        for j, (cx, cy) in enumerate(chips):
            cp = pltpu.make_async_remote_copy(
                src_ref=x_ref.at[rows, :], dst_ref=land_ref.at[2 * cx + cy, rows, :], send_sem=send_sems.at[j],
                recv_sem=recv_sems.at[j], device_id=(cx, cy, c), device_id_type=MESH)
            cp.wait_send()
            cp.wait_recv()

    return pl.pallas_call(
        body, name=name,
        out_shape=(pltpu.HBM(xs.shape, xs.dtype), pltpu.HBM(land.shape, land.dtype)),
        in_specs=(HBM_SPEC, HBM_SPEC, SEM_SPEC, SEM_SPEC, pl.BlockSpec(memory_space=pl.ANY)),
        out_specs=(HBM_SPEC, HBM_SPEC),
        input_output_aliases={0: 0, 1: 1},
        compiler_params=pltpu.CompilerParams(has_side_effects=SPLIT_EFFECT),
    )(xs, land, send_sems, recv_sems, after)[1]


def _ag_forward_start(got, name):
    _, r, n = got.shape
    half = r // 2

    def body(g_ref, send_sems, recv_sems, g_thru):
        x, y, c, chips = _place()
        for j, (cx, cy) in enumerate(chips):
            piece = g_ref.at[2 * cx + cy, pl.ds(c * half, half), :]
            pltpu.make_async_remote_copy(src_ref=piece, dst_ref=piece, send_sem=send_sems.at[j],
                                         recv_sem=recv_sems.at[j], device_id=(x, y, 1 - c),
                                         device_id_type=MESH).start()

    sems = pltpu.SemaphoreType.DMA((3,))
    return pl.pallas_call(
        body, name=name,
        out_shape=(sems, sems, pltpu.HBM(got.shape, got.dtype)),
        in_specs=(HBM_SPEC,),
        out_specs=(SEM_SPEC, SEM_SPEC, HBM_SPEC),
        input_output_aliases={0: 2},
        compiler_params=pltpu.CompilerParams(has_side_effects=SPLIT_EFFECT),
    )(got)


def _ag_forward_wait(handle, name):
    send_sems, recv_sems, got = handle
    _, r, n = got.shape
    half = r // 2

    def body(g_ref, send_sems, recv_sems, g_out):
        x, y, c, chips = _place()
        for j, (cx, cy) in enumerate(chips):
            sent = g_ref.at[2 * cx + cy, pl.ds(c * half, half), :]
            pltpu.make_async_remote_copy(src_ref=sent, dst_ref=sent, send_sem=send_sems.at[j],
                                         recv_sem=recv_sems.at[j], device_id=(x, y, 1 - c),
                                         device_id_type=MESH).wait_send()
            came = g_ref.at[2 * cx + cy, pl.ds((1 - c) * half, half), :]
            pltpu.make_async_remote_copy(src_ref=came, dst_ref=came, send_sem=send_sems.at[j],
                                         recv_sem=recv_sems.at[j], device_id=(x, y, 1 - c),
                                         device_id_type=MESH).wait_recv()

    return pl.pallas_call(
        body, name=name,
        out_shape=pltpu.HBM(got.shape, got.dtype),
        in_specs=(HBM_SPEC, SEM_SPEC, SEM_SPEC),
        out_specs=HBM_SPEC,
        input_output_aliases={0: 0},
        compiler_params=pltpu.CompilerParams(has_side_effects=SPLIT_EFFECT),
    )(got, send_sems, recv_sems)


def _ag_fill_own(got, xs, me_idx, name):
    r, n = xs.shape
    tr = _tile(r, max(16, (1 << 20) // n // 16 * 16), 16)

    def body(me_ref, x_ref, g_ref, out_ref):
        out_ref[...] = x_ref[...]

    return pl.pallas_call(
        body, name=name,
        grid_spec=pltpu.PrefetchScalarGridSpec(
            num_scalar_prefetch=1, grid=(r // tr,),
            in_specs=[pl.BlockSpec((tr, n), lambda i, me: (i, 0)), pl.BlockSpec(memory_space=pl.ANY)],
            out_specs=pl.BlockSpec((None, tr, n), lambda i, me: (me[0], i, 0))),
        out_shape=jax.ShapeDtypeStruct(got.shape, got.dtype),
        input_output_aliases={2: 0},
        compiler_params=_cp(("parallel",)),
    )(me_idx, xs, got)


def _all_gather_chips(xs, name):
    r, n = xs.shape
    half = r // 2

    def body(x_ref, out_ref, send_sems, recv_sems):
        x, y, c, chips = _place()
        me = 2 * x + y
        sibling = (x, y, 1 - c)

        def piece(chip, hc):
            return out_ref.at[chip, pl.ds(hc * half, half), :]

        def copy(k, src, dst, to):
            return pltpu.make_async_remote_copy(src_ref=src, dst_ref=dst, send_sem=send_sems.at[k],
                                                recv_sem=recv_sems.at[k], device_id=to, device_id_type=MESH)

        src = x_ref.at[pl.ds(c * half, half), :]
        first = [copy(j, src, piece(me, c), (cx, cy, c)) for j, (cx, cy) in enumerate(chips)]
        for cp in first:
            cp.start()
        passed = []
        for j, (cx, cy) in enumerate(chips):
            got = piece(2 * cx + cy, c)
            copy(j, got, got, (cx, cy, c)).wait_recv()
            fwd = copy(3 + j, got, got, sibling)
            fwd.start()
            passed.append(fwd)
        for j, (cx, cy) in enumerate(chips):
            got = piece(2 * cx + cy, 1 - c)
            copy(3 + j, got, got, sibling).wait_recv()
        for cp in first + passed:
            cp.wait_send()

    return pl.pallas_call(
        body, name=name,
        in_specs=[pl.BlockSpec(memory_space=pl.ANY)],
        out_specs=pl.BlockSpec(memory_space=pl.ANY),
        out_shape=jax.ShapeDtypeStruct((4, r, n), xs.dtype),
        scratch_shapes=[pltpu.SemaphoreType.DMA((6,)), pltpu.SemaphoreType.DMA((6,))],
    )(xs)


def _gather(xs, me, name):
    return lax.dynamic_update_slice(_all_gather_chips(xs, name), xs[None], (me, 0, 0))


def _rs_swap_start(g, name):
    _, r, n = g.shape
    half = r // 2

    def body(g_ref, land_ref, send_sem, recv_sem, g_thru, land_thru, token):
        x, y, c, _ = _place()
        pltpu.make_async_remote_copy(
            src_ref=g_ref.at[:, pl.ds((1 - c) * half, half), :], dst_ref=land_ref,
            send_sem=send_sem, recv_sem=recv_sem, device_id=(x, y, 1 - c), device_id_type=MESH).start()
        token[...] = jnp.zeros_like(token)

    sem = pltpu.SemaphoreType.DMA(())
    out = pl.pallas_call(
        body, name=name,
        out_shape=(sem, sem, pltpu.HBM(g.shape, g.dtype), pltpu.HBM((4, half, n), g.dtype),
                   jax.ShapeDtypeStruct((8, 128), F32)),
        in_specs=(HBM_SPEC, HBM_SPEC),
        out_specs=(SEM_SPEC, SEM_SPEC, HBM_SPEC, HBM_SPEC, pl.BlockSpec(memory_space=pltpu.VMEM)),
        input_output_aliases={0: 2, 1: 3},
        compiler_params=pltpu.CompilerParams(has_side_effects=SPLIT_EFFECT),
    )(_in_hbm(g), _in_hbm(lax.empty((4, half, n), g.dtype)))
    return out[:4], out[4][0, 0]


def _rs_swap_wait(handle, after, name):
    send_sem, recv_sem, g, land = handle
    half = land.shape[1]

    def body(g_ref, land_ref, send_sem, recv_sem, after_ref, g_out, land_out):
        x, y, c, _ = _place()
        cp = pltpu.make_async_remote_copy(
            src_ref=g_ref.at[:, pl.ds((1 - c) * half, half), :], dst_ref=land_ref,
            send_sem=send_sem, recv_sem=recv_sem, device_id=(x, y, 1 - c), device_id_type=MESH)
        cp.wait_send()
        cp.wait_recv()

    return pl.pallas_call(
        body, name=name,
        out_shape=(pltpu.HBM(g.shape, g.dtype), pltpu.HBM(land.shape, land.dtype)),
        in_specs=(HBM_SPEC, HBM_SPEC, SEM_SPEC, SEM_SPEC, pl.BlockSpec(memory_space=pl.ANY)),
        out_specs=(HBM_SPEC, HBM_SPEC),
        input_output_aliases={0: 0, 1: 1},
        compiler_params=pltpu.CompilerParams(has_side_effects=SPLIT_EFFECT),
    )(g, land, send_sem, recv_sem, after)


def _rs_add_halves(g, got, c_idx, name):
    _, r, n = g.shape
    half = r // 2
    tr = _tile(half, max(16, (1 << 19) // n // 16 * 16), 16)
    nb = half // tr

    def body(c_ref, g_ref, o_ref, out_ref, out16_ref):
        sm = g_ref[...] + o_ref[...]
        out_ref[...] = sm
        out16_ref[...] = sm.astype(out16_ref.dtype)

    blk = pl.BlockSpec((None, tr, n), lambda s, i, c: (s, i, 0))
    return pl.pallas_call(
        body, name=name,
        grid_spec=pltpu.PrefetchScalarGridSpec(
            num_scalar_prefetch=1, grid=(4, nb),
            in_specs=[pl.BlockSpec((None, tr, n), lambda s, i, c: (s, c[0] * nb + i, 0)), blk],
            out_specs=[blk, blk]),
        out_shape=[jax.ShapeDtypeStruct((4, half, n), F32), jax.ShapeDtypeStruct((4, half, n), RS_PAYLOAD)],
        compiler_params=_cp(("parallel", "parallel")),
    )(c_idx, g, got)


def _rs_start(p32, p16, name):
    _, h, n = p16.shape

    def body(p32_ref, p16_ref, l16_ref, l32_ref, send_sems, recv_sems, p32_t, p16_t, l16_t, l32_t, token):
        x, y, c, chips = _place()
        for j, (cx, cy) in enumerate(chips):
            for k, pc in enumerate((c, 1 - c)):
                pltpu.make_async_remote_copy(
                    src_ref=p16_ref.at[2 * cx + cy], dst_ref=l16_ref.at[c, j], send_sem=send_sems.at[3 * k + j],
                    recv_sem=recv_sems.at[3 * k + j], device_id=(cx, cy, pc), device_id_type=MESH).start()
        pltpu.make_async_remote_copy(
            src_ref=p32_ref.at[2 * x + y], dst_ref=l32_ref, send_sem=send_sems.at[6], recv_sem=recv_sems.at[6],
            device_id=(x, y, 1 - c), device_id_type=MESH).start()
        token[...] = jnp.zeros_like(token)

    sems = pltpu.SemaphoreType.DMA((7,))
    l16 = lax.empty((2, 3, h, n), p16.dtype)
    l32 = lax.empty((h, n), F32)
    out = pl.pallas_call(
        body, name=name,
        out_shape=(sems, sems, pltpu.HBM(p32.shape, F32), pltpu.HBM(p16.shape, p16.dtype),
                   pltpu.HBM(l16.shape, l16.dtype), pltpu.HBM(l32.shape, F32), jax.ShapeDtypeStruct((8, 128), F32)),
        in_specs=(HBM_SPEC,) * 4,
        out_specs=(SEM_SPEC, SEM_SPEC) + (HBM_SPEC,) * 4 + (pl.BlockSpec(memory_space=pltpu.VMEM),),
        input_output_aliases={0: 2, 1: 3, 2: 4, 3: 5},
        compiler_params=pltpu.CompilerParams(has_side_effects=SPLIT_EFFECT),
    )(_in_hbm(p32), _in_hbm(p16), _in_hbm(l16), _in_hbm(l32))
    return out[:6], out[6][0, 0]


def _rs_wait(handle, after, name):
    send_sems, recv_sems, p32, p16, l16, l32 = handle

    def body(p32_ref, p16_ref, l16_ref, l32_ref, send_sems, recv_sems, after_ref, p32_o, p16_o, l16_o, l32_o):
        x, y, c, chips = _place()
        for j, (cx, cy) in enumerate(chips):
            for k, pc in enumerate((c, 1 - c)):
                cp = pltpu.make_async_remote_copy(
                    src_ref=p16_ref.at[2 * cx + cy], dst_ref=l16_ref.at[pc, j], send_sem=send_sems.at[3 * k + j],
                    recv_sem=recv_sems.at[3 * k + j], device_id=(cx, cy, pc), device_id_type=MESH)
                cp.wait_send()
                cp.wait_recv()
        cp = pltpu.make_async_remote_copy(
            src_ref=p32_ref.at[2 * x + y], dst_ref=l32_ref, send_sem=send_sems.at[6], recv_sem=recv_sems.at[6],
            device_id=(x, y, 1 - c), device_id_type=MESH)
        cp.wait_send()
        cp.wait_recv()

    out = pl.pallas_call(
        body, name=name,
        out_shape=(pltpu.HBM(p32.shape, F32), pltpu.HBM(p16.shape, p16.dtype), pltpu.HBM(l16.shape, l16.dtype),
                   pltpu.HBM(l32.shape, F32)),
        in_specs=(HBM_SPEC,) * 4 + (SEM_SPEC, SEM_SPEC, pl.BlockSpec(memory_space=pl.ANY)),
        out_specs=(HBM_SPEC,) * 4,
        input_output_aliases={0: 0, 1: 1, 2: 2, 3: 3},
        compiler_params=pltpu.CompilerParams(has_side_effects=SPLIT_EFFECT),
    )(p32, p16, l16, l32, send_sems, recv_sems, after)
    return out[0], out[2], out[3]


def _rs_finish(p32, l16, l32, c_idx, me_idx, name):
    _, h, n = p32.shape
    tr = _tile(h, max(16, (1 << 18) // n // 16 * 16), 16)
    nb = h // tr

    def body(c_ref, me_ref, own_ref, sib_ref, a_ref, b_ref, d_ref, out_ref):
        base = jnp.where(pl.program_id(0) == c_ref[0], own_ref[...], sib_ref[...])
        out_ref[...] = ((base + a_ref[...].astype(F32)) + b_ref[...].astype(F32)) + d_ref[...].astype(F32)

    def piece(j):
        return pl.BlockSpec((None, None, tr, n), lambda hc, i, c, me: (hc, j, i, 0))

    return pl.pallas_call(
        body, name=name,
        grid_spec=pltpu.PrefetchScalarGridSpec(
            num_scalar_prefetch=2, grid=(2, nb),
            in_specs=[pl.BlockSpec((None, tr, n), lambda hc, i, c, me: (me[0], i, 0)),
                      pl.BlockSpec((tr, n), lambda hc, i, c, me: (i, 0)),
                      piece(0), piece(1), piece(2)],
            out_specs=pl.BlockSpec((tr, n), lambda hc, i, c, me: (hc * nb + i, 0))),
        out_shape=jax.ShapeDtypeStruct((2 * h, n), F32),
        compiler_params=_cp(("parallel", "parallel")),
    )(c_idx, me_idx, p32, l32, l16, l16, l16)


def _rs_begin(swap, after, place, tag):
    g, got = _rs_swap_wait(swap, after, "rs_swap_wait_" + tag)
    p32, p16 = _rs_add_halves(g, got, place['c_idx'], "rs_add_halves")
    return _rs_start(p32, p16, "rs_start_" + tag)


def _rs_end(handle, after, place, tag):
    p32, l16, l32 = _rs_wait(handle, after, "rs_wait_" + tag)
    return _rs_finish(p32, l16, l32, place['c_idx'], place['me_idx'], "rs_finish")


def _all_reduce_small(v, name):
    R, n = v.shape

    def body(v_ref, out_ref, buf, send_sems, recv_sems):
        x, y, c, _ = _place()
        me = 4 * x + 2 * y + c
        buf[me] = v_ref[...]
        copies = []
        for d in range(1, 8):
            dx, dy, dc = (d >> 2) & 1, (d >> 1) & 1, d & 1
            px = x if dx == 0 else 1 - x
            py = y if dy == 0 else 1 - y
            pc = c if dc == 0 else 1 - c
            copies.append(pltpu.make_async_remote_copy(
                src_ref=v_ref, dst_ref=buf.at[me], send_sem=send_sems.at[d - 1], recv_sem=recv_sems.at[d - 1],
                device_id=(px, py, pc), device_id_type=MESH))
        for cp in copies:
            cp.start()
        for d in range(1, 8):
            dx, dy, dc = (d >> 2) & 1, (d >> 1) & 1, d & 1
            px = x if dx == 0 else 1 - x
            py = y if dy == 0 else 1 - y
            pc = c if dc == 0 else 1 - c
            pltpu.make_async_remote_copy(
                src_ref=v_ref, dst_ref=buf.at[4 * px + 2 * py + pc], send_sem=send_sems.at[d - 1],
                recv_sem=recv_sems.at[d - 1], device_id=(px, py, pc), device_id_type=MESH).wait_recv()
        for cp in copies:
            cp.wait_send()
        acc = buf[0]
        for k in range(1, 8):
            acc = acc + buf[k]
        out_ref[...] = acc

    return pl.pallas_call(
        body, name=name,
        in_specs=[pl.BlockSpec(memory_space=pltpu.VMEM)],
        out_specs=pl.BlockSpec(memory_space=pltpu.VMEM),
        out_shape=jax.ShapeDtypeStruct((R, n), F32),
        scratch_shapes=[pltpu.VMEM((8, R, n), F32), pltpu.SemaphoreType.DMA((7,)), pltpu.SemaphoreType.DMA((7,))],
        compiler_params=pltpu.CompilerParams(vmem_limit_bytes=VMEM_LIMIT_BYTES),
    )(v)


def _pack_small(parts, rows):
    flat = jnp.concatenate([a.reshape(-1) for a in parts])
    return jnp.pad(flat, (0, rows * LANES - flat.shape[0])).reshape(rows, LANES)


def _mem_fwd(proj, q_blk, mkv, heads):
    S = proj.shape[0]
    ML = mkv.shape[0]
    t = _tile(S, MEM_TILE, 8)
    scale = HEAD ** -0.5

    def body(q_ref, k_ref, v_ref, o_ref):
        s = _dot(q_ref[...], k_ref[...], NT) * scale
        m = jnp.max(s, axis=-1, keepdims=True)
        e = jnp.exp(s - m)
        p = e / jnp.sum(e, axis=-1, keepdims=True)
        o_ref[...] = _dot(p, v_ref[...], NN)

    return pl.pallas_call(
        body, name="mem_fwd", grid=(S // t, heads),
        in_specs=[pl.BlockSpec((t, HEAD), lambda i, h: (i, q_blk + h)),
                  pl.BlockSpec((ML, HEAD), lambda i, h: (0, h)),
                  pl.BlockSpec((ML, HEAD), lambda i, h: (0, heads + h))],
        out_specs=pl.BlockSpec((t, HEAD), lambda i, h: (i, h)),
        out_shape=jax.ShapeDtypeStruct((S, heads * HEAD), F32),
        compiler_params=_cp(("parallel", "parallel")),
    )(proj, mkv, mkv)


def _mem_bwd(proj, q_blk, mkv, dcat, d_blk, heads):
    S = proj.shape[0]
    ML = mkv.shape[0]
    t = _tile(S, MEM_TILE, 8)
    scale = HEAD ** -0.5

    def body(q_ref, k_ref, v_ref, do_ref, dq_ref, dk_ref, dv_ref):
        i = pl.program_id(1)

        @pl.when(i == 0)
        def _():
            dk_ref[...] = jnp.zeros_like(dk_ref)
            dv_ref[...] = jnp.zeros_like(dv_ref)

        q, k, v, do = q_ref[...], k_ref[...], v_ref[...], do_ref[...]
        s = _dot(q, k, NT) * scale
        m = jnp.max(s, axis=-1, keepdims=True)
        e = jnp.exp(s - m)
        p = e / jnp.sum(e, axis=-1, keepdims=True)
        dp = _dot(do, v, NT)
        ds = p * (dp - jnp.sum(p * dp, axis=-1, keepdims=True))
        dq_ref[...] = _dot(ds, k, NN) * scale
        dk_ref[...] += _dot(ds, q, TN) * scale
        dv_ref[...] += _dot(p, do, TN)

    dq, dk, dv = pl.pallas_call(
        body, name="mem_bwd", grid=(heads, S // t),
        in_specs=[pl.BlockSpec((t, HEAD), lambda h, i: (i, q_blk + h)),
                  pl.BlockSpec((ML, HEAD), lambda h, i: (0, h)),
                  pl.BlockSpec((ML, HEAD), lambda h, i: (0, heads + h)),
                  pl.BlockSpec((t, HEAD), lambda h, i: (i, d_blk + h))],
        out_specs=[pl.BlockSpec((t, HEAD), lambda h, i: (i, h)),
                   pl.BlockSpec((ML, HEAD), lambda h, i: (0, h)),
                   pl.BlockSpec((ML, HEAD), lambda h, i: (0, h))],
        out_shape=[jax.ShapeDtypeStruct((S, heads * HEAD), F32),
                   jax.ShapeDtypeStruct((ML, heads * HEAD), F32),
                   jax.ShapeDtypeStruct((ML, heads * HEAD), F32)],
        compiler_params=_cp(("parallel", "arbitrary")),
    )(proj, mkv, mkv, dcat)
    return dq, jnp.concatenate([dk, dv], axis=1)


def _fox_gates(gl, bf):
    S = gl.shape[0]

    def body(g_ref, b_ref, o_ref):
        xv = g_ref[...] + b_ref[...]
        c = jnp.minimum(xv, 0.0) - jnp.log(1.0 + jnp.exp(-jnp.abs(xv)))
        row = lax.broadcasted_iota(jnp.int32, c.shape, 0)
        d = 1
        while d < S:
            c = c + jnp.where(row >= d, pltpu.roll(c, d, 0), 0.0)
            d *= 2
        o_ref[...] = c

    return pl.pallas_call(
        body, name="fox_gates", out_shape=jax.ShapeDtypeStruct((S, 128), F32),
        in_specs=[pl.BlockSpec(memory_space=pltpu.VMEM)] * 2,
        out_specs=pl.BlockSpec(memory_space=pltpu.VMEM),
        compiler_params=_cp(),
    )(gl, bf)


def _fox_gates_bwd(gl, bf, dcf):
    S = gl.shape[0]

    def body(g_ref, b_ref, d_ref, dg_ref, db_ref):
        c = d_ref[...]
        row = lax.broadcasted_iota(jnp.int32, c.shape, 0)
        d = 1
        while d < S:
            c = c + jnp.where(row < S - d, pltpu.roll(c, S - d, 0), 0.0)
            d *= 2
        dx = c * _sigmoid(-(g_ref[...] + b_ref[...]))
        dg_ref[...] = dx
        db_ref[...] = jnp.sum(dx, axis=0, keepdims=True)

    return pl.pallas_call(
        body, name="fox_gates_bwd",
        out_shape=[jax.ShapeDtypeStruct((S, 128), F32), jax.ShapeDtypeStruct((1, 128), F32)],
        in_specs=[pl.BlockSpec(memory_space=pltpu.VMEM)] * 3,
        out_specs=[pl.BlockSpec(memory_space=pltpu.VMEM)] * 2,
        compiler_params=_cp(),
    )(gl, bf, dcf)


def _fox_scores(q, k, cq, ck, t, masked):
    s = _dot(q * (HEAD ** -0.5), k, NT) + cq - ck
    if masked:
        row = lax.broadcasted_iota(jnp.int32, (t, t), 0)
        col = lax.broadcasted_iota(jnp.int32, (t, t), 1)
        s = jnp.where(row >= col, s, -jnp.inf)
    return s


def _fox_pairs(nq, by_key):
    if by_key:
        pairs = [(i, j) for j in range(nq) for i in range(j, nq)]
    else:
        pairs = [(i, j) for i in range(nq) for j in range(i + 1)]
    return (jnp.asarray(np.array([a for a, _ in pairs], np.int32)),
            jnp.asarray(np.array([b for _, b in pairs], np.int32)))


def _fox_heads_per_step(H):
    return 2 if H % 2 == 0 else 1


def _fox_fwd(proj, cfq, cfk, H):
    S = proj.shape[0]
    t = _tile(S, FOX_TILE)
    nq = S // t
    hb = _fox_heads_per_step(H)
    W, G = hb * HEAD, H // hb
    cols = [slice(i * HEAD, (i + 1) * HEAD) for i in range(hb)]
    qt, kt = _fox_pairs(nq, False)

    def body(qt_ref, kt_ref, q_ref, k_ref, v_ref, cq_ref, ck_ref, o_ref, lse_ref, m_s, l_s, acc_s):
        n = pl.program_id(1)
        qi, ki = qt_ref[n], kt_ref[n]

        @pl.when(ki == 0)
        def _():
            m_s[...] = jnp.full_like(m_s, -jnp.inf)
            l_s[...] = jnp.zeros_like(l_s)
            acc_s[...] = jnp.zeros_like(acc_s)

        def step(masked):
            R = range(hb)
            ss = [_fox_scores(q_ref[:, cols[i]], k_ref[:, cols[i]], cq_ref[i], ck_ref[i], t, masked) for i in R]
            m_new = [jnp.maximum(m_s[i], jnp.max(ss[i], axis=-1, keepdims=True)) for i in R]
            alpha = [jnp.exp(m_s[i] - m_new[i]) for i in R]
            ps = [jnp.exp(ss[i] - m_new[i]) for i in R]
            pv = [_dot(ps[i], v_ref[:, cols[i]], NN) for i in R]
            for i in R:
                l_s[i] = alpha[i] * l_s[i] + jnp.sum(ps[i], axis=-1, keepdims=True)
                acc_s[:, cols[i]] = alpha[i] * acc_s[:, cols[i]] + pv[i]
                m_s[i] = m_new[i]

        @pl.when(ki != qi)
        def _():
            step(False)

        @pl.when(ki == qi)
        def _():
            step(True)
            for i in range(hb):
                o_ref[:, cols[i]] = acc_s[:, cols[i]] / l_s[i]
                lse_ref[i] = m_s[i] + jnp.log(l_s[i])

    qcol = pl.BlockSpec((hb, t, 1), lambda h, n, qt, kt: (h, qt[n], 0))
    return pl.pallas_call(
        body, name="fox_fwd",
        grid_spec=pltpu.PrefetchScalarGridSpec(
            num_scalar_prefetch=2, grid=(G, qt.shape[0]),
            in_specs=[pl.BlockSpec((t, W), lambda h, n, qt, kt: (qt[n], h)),
                      pl.BlockSpec((t, W), lambda h, n, qt, kt: (kt[n], G + h)),
                      pl.BlockSpec((t, W), lambda h, n, qt, kt: (kt[n], 2 * G + h)),
                      qcol,
                      pl.BlockSpec((hb, 1, t), lambda h, n, qt, kt: (h, 0, kt[n]))],
            out_specs=[pl.BlockSpec((t, W), lambda h, n, qt, kt: (qt[n], h)), qcol],
            scratch_shapes=[pltpu.VMEM((hb, t, 1), F32), pltpu.VMEM((hb, t, 1), F32), pltpu.VMEM((t, W), F32)]),
        out_shape=[jax.ShapeDtypeStruct((S, H * HEAD), F32), jax.ShapeDtypeStruct((H, S, 1), F32)],
        compiler_params=_cp(("parallel", "arbitrary")),
    )(qt, kt, proj, proj, proj, cfq, cfk)


def _fox_bwd_rowdot(proj, cfq, cfk, lse, dcat, H):
    S = proj.shape[0]
    t = _tile(S, FOX_TILE)
    nq = S // t
    hb = _fox_heads_per_step(H)
    W, G = hb * HEAD, H // hb
    cols = [slice(i * HEAD, (i + 1) * HEAD) for i in range(hb)]
    qt, kt = _fox_pairs(nq, False)

    def body(qt_ref, kt_ref, q_ref, k_ref, v_ref, do_ref, lse_ref, cq_ref, ck_ref, d_ref):
        n = pl.program_id(1)
        qi, ki = qt_ref[n], kt_ref[n]

        @pl.when(ki == 0)
        def _():
            d_ref[...] = jnp.zeros_like(d_ref)

        def step(masked):
            R = range(hb)
            ss = [_fox_scores(q_ref[:, cols[i]], k_ref[:, cols[i]], cq_ref[i], ck_ref[i], t, masked) for i in R]
            dps = [_dot(do_ref[:, cols[i]], v_ref[:, cols[i]], NT) for i in R]
            ps = [jnp.exp(ss[i] - lse_ref[i]) for i in R]
            for i in R:
                d_ref[i] += jnp.sum(ps[i] * dps[i], axis=-1, keepdims=True)

        @pl.when(ki != qi)
        def _():
            step(False)

        @pl.when(ki == qi)
        def _():
            step(True)

    qtile = pl.BlockSpec((t, W), lambda h, n, qt, kt: (qt[n], h))
    qcol = pl.BlockSpec((hb, t, 1), lambda h, n, qt, kt: (h, qt[n], 0))
    return pl.pallas_call(
        body, name="fox_bwd_rowdot",
        grid_spec=pltpu.PrefetchScalarGridSpec(
            num_scalar_prefetch=2, grid=(G, qt.shape[0]),
            in_specs=[qtile,
                      pl.BlockSpec((t, W), lambda h, n, qt, kt: (kt[n], G + h)),
                      pl.BlockSpec((t, W), lambda h, n, qt, kt: (kt[n], 2 * G + h)),
                      qtile, qcol, qcol,
                      pl.BlockSpec((hb, 1, t), lambda h, n, qt, kt: (h, 0, kt[n]))],
            out_specs=qcol),
        out_shape=jax.ShapeDtypeStruct((H, S, 1), F32),
        compiler_params=_cp(("parallel", "arbitrary")),
    )(qt, kt, proj, proj, proj, dcat, lse, cfq, cfk)


def _fox_bwd(proj, cfq, cfk, rowdot, lse, dcat, H):
    S = proj.shape[0]
    t = _tile(S, FOX_TILE)
    nq = S // t
    scale = HEAD ** -0.5
    hb = _fox_heads_per_step(H)
    W, G = hb * HEAD, H // hb
    cols = [slice(i * HEAD, (i + 1) * HEAD) for i in range(hb)]
    qt, kt = _fox_pairs(nq, True)

    def body(qt_ref, kt_ref, q_ref, k_ref, v_ref, dd_ref, do_ref, lse_ref, cq_ref, ck_ref,
             dq_ref, dk_ref, dv_ref, dck_ref):
        n = pl.program_id(1)
        i_, j_ = qt_ref[n], kt_ref[n]

        @pl.when(n == 0)
        def _():
            dq_ref[...] = jnp.zeros_like(dq_ref)

        @pl.when(i_ == j_)
        def _():
            dk_ref[...] = jnp.zeros_like(dk_ref)
            dv_ref[...] = jnp.zeros_like(dv_ref)
            dck_ref[...] = jnp.zeros_like(dck_ref)

        def step(masked):
            R = range(hb)
            qs, ks = [q_ref[:, c] for c in cols], [k_ref[:, c] for c in cols]
            dos = [do_ref[:, c] for c in cols]
            ss = [_fox_scores(qs[i], ks[i], cq_ref[i], ck_ref[i], t, masked) for i in R]
            dps = [_dot(dos[i], v_ref[:, cols[i]], NT) for i in R]
            ps = [jnp.exp(ss[i] - lse_ref[i]) for i in R]
            dss = [ps[i] * (dps[i] - dd_ref[i]) for i in R]
            dvs = [_dot(ps[i], dos[i], TN) for i in R]
            dks = [_dot(dss[i], qs[i], TN) * scale for i in R]
            dqs = [_dot(dss[i], ks[i], NN) * scale for i in R]
            rows = pl.ds(pl.multiple_of(i_ * t, t), t)
            for i in R:
                dv_ref[:, cols[i]] += dvs[i]
                dk_ref[:, cols[i]] += dks[i]
                dq_ref[rows, cols[i]] += dqs[i]
                dck_ref[i] -= jnp.sum(dss[i], axis=0, keepdims=True)

        @pl.when(i_ != j_)
        def _():
            step(False)

        @pl.when(i_ == j_)
        def _():
            step(True)

    qtile = pl.BlockSpec((t, W), lambda h, n, qt, kt: (qt[n], h))
    qcol = pl.BlockSpec((hb, t, 1), lambda h, n, qt, kt: (h, qt[n], 0))
    ktile = pl.BlockSpec((t, W), lambda h, n, qt, kt: (kt[n], h))
    krow = pl.BlockSpec((hb, 1, t), lambda h, n, qt, kt: (h, 0, kt[n]))
    return pl.pallas_call(
        body, name="fox_bwd",
        grid_spec=pltpu.PrefetchScalarGridSpec(
            num_scalar_prefetch=2, grid=(G, qt.shape[0]),
            in_specs=[qtile,
                      pl.BlockSpec((t, W), lambda h, n, qt, kt: (kt[n], G + h)),
                      pl.BlockSpec((t, W), lambda h, n, qt, kt: (kt[n], 2 * G + h)),
                      qcol, qtile, qcol, qcol, krow],
            out_specs=[pl.BlockSpec((S, W), lambda h, n, qt, kt: (0, h)), ktile, ktile, krow]),
        out_shape=[jax.ShapeDtypeStruct((S, H * HEAD), F32)] * 3 + [jax.ShapeDtypeStruct((H, 1, S), F32)],
        compiler_params=_cp(("parallel", "arbitrary")),
    )(qt, kt, proj, proj, proj, rowdot, dcat, lse, cfq, cfk)


def _s5_prep(lam_re, lam_im, log_dt, b_re, b_im, c_re, c_im):
    G, P = lam_re.shape
    ns = G // 16
    dt = jnp.exp(log_dt)[:, None]
    mag = jnp.exp(lam_re * dt)
    a_re, a_im = mag * jnp.cos(lam_im * dt), mag * jnp.sin(lam_im * dt)
    den = lam_re * lam_re + lam_im * lam_im
    z_re = ((a_re - 1.0) * lam_re + a_im * lam_im) / den
    z_im = (a_im * lam_re - (a_re - 1.0) * lam_im) / den
    bb_re = z_re[..., None] * b_re - z_im[..., None] * b_im
    bb_im = z_re[..., None] * b_im + z_im[..., None] * b_re
    eye = jnp.eye(16, dtype=F32)
    bb = jnp.stack([bb_re, bb_im]).reshape(2, ns, 16, P, S5_GROUP)
    wb = jnp.einsum('asgpc,gh->sgcahp', bb, eye).reshape(ns, S5_SLAB, 2 * 16 * P)
    cc = jnp.stack([c_re, -c_im]).reshape(2, ns, 16, S5_GROUP, P)
    wc = jnp.einsum('asgcp,gh->sagphc', cc, eye).reshape(ns, 2 * 16 * P, S5_SLAB)
    a = jnp.concatenate([a_re.reshape(ns, 1, 16 * P), a_im.reshape(ns, 1, 16 * P)], axis=-1)
    return wb, wc, a


def _s5_tables(lam_re, lam_im, log_dt):
    G, P = lam_re.shape
    ns = G // 16
    dt = jnp.exp(log_dt)[:, None]
    tt = jnp.arange(1, S5_CHUNK + 1, dtype=F32)[:, None, None]
    mag = jnp.exp(lam_re * dt * tt)
    ang = lam_im * dt * tt
    pr = (mag * jnp.cos(ang)).reshape(S5_CHUNK, ns, 16 * P).transpose(1, 0, 2)
    pi = (mag * jnp.sin(ang)).reshape(S5_CHUNK, ns, 16 * P).transpose(1, 0, 2)
    return pr, pi, pr[:, ::-1], pi[:, ::-1]


def _s5_scan_fwd(proj, wb, wc, pr, pi, dskip):
    S = proj.shape[0]
    ns = wb.shape[0]
    W = wb.shape[2]
    hw = W // 2
    T = S5_CHUNK
    nc = S // T
    mix = ns * S5_SLAB

    def body(u_ref, wb_ref, wc_ref, pr_ref, pi_ref, d_ref, v_ref, yg_ref, h_ref, cin_ref, carry):
        c = pl.program_id(1)

        @pl.when(c == 0)
        def _():
            carry[...] = jnp.zeros_like(carry)

        u = u_ref[...]
        bu = _dot(u, wb_ref[...], NN)
        xr, xi = bu[:, :hw], bu[:, hw:]
        sub = lax.broadcasted_iota(jnp.int32, (T, hw), 0) & (S5_ROWS - 1)
        d = 1
        while d < S5_ROWS:
            ar, ai = pr_ref[pl.ds(d - 1, 1), :], pi_ref[pl.ds(d - 1, 1), :]
            sr = jnp.where(sub >= d, pltpu.roll(xr, d, 0), 0.0)
            si = jnp.where(sub >= d, pltpu.roll(xi, d, 0), 0.0)
            xr, xi = xr + ar * sr - ai * si, xi + ar * si + ai * sr
            d *= 2
        cin_ref[...] = carry[...]
        cr, ci = carry[:, :hw], carry[:, hw:]
        pwr, pwi = pr_ref[pl.ds(0, S5_ROWS), :], pi_ref[pl.ds(0, S5_ROWS), :]
        for g in range(T // S5_ROWS):
            rows = slice(g * S5_ROWS, (g + 1) * S5_ROWS)
            hr = xr[rows, :] + pwr * cr - pwi * ci
            hi = xi[rows, :] + pwr * ci + pwi * cr
            h_ref[rows, :hw] = hr
            h_ref[rows, hw:] = hi
            cr, ci = hr[S5_ROWS - 1:S5_ROWS, :], hi[S5_ROWS - 1:S5_ROWS, :]
        carry[:, :hw] = cr
        carry[:, hw:] = ci
        y = _dot(h_ref[...], wc_ref[...], NN)
        v = y + d_ref[...] * u
        v_ref[...] = v
        yg_ref[...] = _gelu(v)

    return pl.pallas_call(
        body, name="s5_scan_fwd", grid=(ns, nc),
        in_specs=[pl.BlockSpec((T, S5_SLAB), lambda s, c: (c, s)),
                  pl.BlockSpec((None, S5_SLAB, W), lambda s, c: (s, 0, 0)),
                  pl.BlockSpec((None, W, S5_SLAB), lambda s, c: (s, 0, 0)),
                  pl.BlockSpec((None, T, hw), lambda s, c: (s, 0, 0)),
                  pl.BlockSpec((None, T, hw), lambda s, c: (s, 0, 0)),
                  pl.BlockSpec((1, S5_SLAB), lambda s, c: (0, s))],
        out_specs=[pl.BlockSpec((T, S5_SLAB), lambda s, c: (c, s)),
                   pl.BlockSpec((T, S5_SLAB), lambda s, c: (c, s)),
                   pl.BlockSpec((T, W), lambda s, c: (c, s)),
                   pl.BlockSpec((None, 1, W), lambda s, c: (c, 0, s))],
        out_shape=[jax.ShapeDtypeStruct((S, mix), F32), jax.ShapeDtypeStruct((S, mix), F32),
                   jax.ShapeDtypeStruct((S, ns * W), F32), jax.ShapeDtypeStruct((nc, 1, ns * W), F32)],
        scratch_shapes=[pltpu.VMEM((1, W), F32)],
        compiler_params=_cp(("parallel", "arbitrary")),
    )(proj, wb, wc, pr, pi, dskip)


def _s5_scan_bwd(dv, proj, hs, cin, wb, wc, pr, pi, prr, pir, dskip):
    S = proj.shape[0]
    ns = wb.shape[0]
    W = wb.shape[2]
    hw = W // 2
    T = S5_CHUNK
    nc = S // T
    mix = ns * S5_SLAB

    def body(dv_ref, u_ref, h_ref, cin_ref, wb_ref, wc_ref, pr_ref, pi_ref, prr_ref, pir_ref, d_ref,
             du_ref, dwb_ref, dwc_ref, da_ref, dd_ref, lam_s, carry):
        c = pl.program_id(1)

        @pl.when(c == 0)
        def _():
            carry[...] = jnp.zeros_like(carry)
            dwb_ref[...] = jnp.zeros_like(dwb_ref)
            dwc_ref[...] = jnp.zeros_like(dwc_ref)
            da_ref[...] = jnp.zeros_like(da_ref)
            dd_ref[...] = jnp.zeros_like(dd_ref)

        dy, u = dv_ref[...], u_ref[...]
        dh = _dot(dy, wc_ref[...], NT)
        gr, gi = dh[:, :hw], dh[:, hw:]
        row = lax.broadcasted_iota(jnp.int32, (T, hw), 0)
        sub = row & (S5_ROWS - 1)
        d = 1
        while d < S5_ROWS:
            ar, ai = pr_ref[pl.ds(d - 1, 1), :], -pi_ref[pl.ds(d - 1, 1), :]
            sr = jnp.where(sub < S5_ROWS - d, pltpu.roll(gr, T - d, 0), 0.0)
            si = jnp.where(sub < S5_ROWS - d, pltpu.roll(gi, T - d, 0), 0.0)
            gr, gi = gr + ar * sr - ai * si, gi + ar * si + ai * sr
            d *= 2
        lr, li = carry[:, :hw], carry[:, hw:]
        pwr, pwi = prr_ref[pl.ds(T - S5_ROWS, S5_ROWS), :], -pir_ref[pl.ds(T - S5_ROWS, S5_ROWS), :]
        for g in reversed(range(T // S5_ROWS)):
            rows = slice(g * S5_ROWS, (g + 1) * S5_ROWS)
            lgr = gr[rows, :] + pwr * lr - pwi * li
            lgi = gi[rows, :] + pwr * li + pwi * lr
            lam_s[rows, :hw] = lgr
            lam_s[rows, hw:] = lgi
            lr, li = lgr[0:1, :], lgi[0:1, :]
        carry[:, :hw] = lr
        carry[:, hw:] = li
        gr, gi = lam_s[:, :hw], lam_s[:, hw:]
        hr, hi = h_ref[:, :hw], h_ref[:, hw:]
        hpr = jnp.where(row >= 1, pltpu.roll(hr, 1, 0), cin_ref[:, :hw])
        hpi = jnp.where(row >= 1, pltpu.roll(hi, 1, 0), cin_ref[:, hw:])
        da_ref[:, :hw] += jnp.sum(hpr * gr + hpi * gi, axis=0, keepdims=True)
        da_ref[:, hw:] += jnp.sum(hpr * gi - hpi * gr, axis=0, keepdims=True)
        lam = lam_s[...]
        du_ref[...] = _dot(lam, wb_ref[...], NT) + dy * d_ref[...]
        dwb_ref[...] += _dot(u, lam, TN)
        dwc_ref[...] += _dot(h_ref[...], dy, TN)
        dd_ref[...] += jnp.sum(dy * u, axis=0, keepdims=True)

    def rc(c):
        return nc - 1 - c

    return pl.pallas_call(
        body, name="s5_scan_bwd", grid=(ns, nc),
        in_specs=[pl.BlockSpec((T, S5_SLAB), lambda s, c: (rc(c), s)),
                  pl.BlockSpec((T, S5_SLAB), lambda s, c: (rc(c), s)),
                  pl.BlockSpec((T, W), lambda s, c: (rc(c), s)),
                  pl.BlockSpec((None, 1, W), lambda s, c: (rc(c), 0, s)),
                  pl.BlockSpec((None, S5_SLAB, W), lambda s, c: (s, 0, 0)),
                  pl.BlockSpec((None, W, S5_SLAB), lambda s, c: (s, 0, 0)),
                  pl.BlockSpec((None, T, hw), lambda s, c: (s, 0, 0)),
                  pl.BlockSpec((None, T, hw), lambda s, c: (s, 0, 0)),
                  pl.BlockSpec((None, T, hw), lambda s, c: (s, 0, 0)),
                  pl.BlockSpec((None, T, hw), lambda s, c: (s, 0, 0)),
                  pl.BlockSpec((1, S5_SLAB), lambda s, c: (0, s))],
        out_specs=[pl.BlockSpec((T, S5_SLAB), lambda s, c: (rc(c), s)),
                   pl.BlockSpec((None, S5_SLAB, W), lambda s, c: (s, 0, 0)),
                   pl.BlockSpec((None, W, S5_SLAB), lambda s, c: (s, 0, 0)),
                   pl.BlockSpec((None, 1, W), lambda s, c: (s, 0, 0)),
                   pl.BlockSpec((1, S5_SLAB), lambda s, c: (0, s))],
        out_shape=[jax.ShapeDtypeStruct((S, mix), F32), jax.ShapeDtypeStruct(wb.shape, F32),
                   jax.ShapeDtypeStruct(wc.shape, F32), jax.ShapeDtypeStruct((ns, 1, W), F32),
                   jax.ShapeDtypeStruct((1, mix), F32)],
        scratch_shapes=[pltpu.VMEM((T, W), F32), pltpu.VMEM((1, W), F32)],
        compiler_params=_cp(("parallel", "arbitrary")),
    )(dv, proj, hs, cin, wb, wc, pr, pi, prr, pir, dskip)


def _s5_glu_bwd(dcat, yg, z):
    S, mix = yg.shape
    tr = _tile(S, ROW_TILE, 8)

    def body(do_ref, yg_ref, z_ref, dz_ref, dy_ref, db_ref):
        i = pl.program_id(0)

        @pl.when(i == 0)
        def _():
            db_ref[...] = jnp.zeros_like(db_ref)

        do, yg_, sz = do_ref[...], yg_ref[...], _sigmoid(z_ref[...])
        dz = do * yg_ * sz * (1.0 - sz)
        dz_ref[...] = dz
        dy_ref[...] = do * sz
        db_ref[...] += jnp.sum(dz, axis=0, keepdims=True)

    blk = pl.BlockSpec((tr, mix), lambda i: (i, 0))
    return pl.pallas_call(
        body, name="s5_glu_bwd", grid=(S // tr,),
        in_specs=[blk, blk, blk], out_specs=[blk, blk, pl.BlockSpec((1, mix), lambda i: (0, 0))],
        out_shape=[jax.ShapeDtypeStruct((S, mix), F32), jax.ShapeDtypeStruct((S, mix), F32),
                   jax.ShapeDtypeStruct((1, mix), F32)],
        compiler_params=_cp(("arbitrary",)),
    )(dcat, yg, z)


def _rows_down(x, j):
    return x if j == 0 else pltpu.roll(x, j, 0)


def _conv_rows(xe, w_ref, n):
    c = None
    for j in range(GDN_CONV):
        term = w_ref[pl.ds(GDN_CONV - 1 - j, 1), :] * _rows_down(xe, j)[8:8 + n, :]
        c = term if c is None else c + term
    return c


def _gdn_prep(proj, blk0, nblk, convw, norm, scale, name):
    S = proj.shape[0]
    tr = _tile(S, CONV_TILE, 8)
    nb8 = tr // 8

    def body(x_ref, xb_ref, w_ref, o_ref):
        i = pl.program_id(1)
        xe = jnp.concatenate([jnp.where(i == 0, 0.0, xb_ref[...]), x_ref[...]], axis=0)
        c = _conv_rows(xe, w_ref, tr)
        s = c * _sigmoid(c)
        if norm:
            s = s * lax.rsqrt(jnp.sum(s * s, axis=-1, keepdims=True) + EPS) * scale
        o_ref[...] = s

    return pl.pallas_call(
        body, name=name, grid=(nblk, S // tr),
        in_specs=[pl.BlockSpec((tr, HEAD), lambda j, i: (i, blk0 + j)),
                  pl.BlockSpec((8, HEAD), lambda j, i: (jnp.maximum(i * nb8 - 1, 0), blk0 + j)),
                  pl.BlockSpec((GDN_CONV, HEAD), lambda j, i: (0, j))],
        out_specs=pl.BlockSpec((tr, HEAD), lambda j, i: (i, j)),
        out_shape=jax.ShapeDtypeStruct((S, nblk * HEAD), F32),
        compiler_params=_cp(("parallel", "parallel")),
    )(proj, proj, convw)


def _gdn_prep_bwd(proj, blk0, nblk, convw, dout, norm, scale, name):
    S = proj.shape[0]
    tr = _tile(S, CONV_TILE, 8)
    nb8 = tr // 8
    last8 = S // 8 - 1
    nrow = S // tr

    def body(x_ref, xb_ref, xa_ref, w_ref, d_ref, da_ref, dx_ref, dw_ref):
        i = pl.program_id(1)

        @pl.when(i == 0)
        def _():
            dw_ref[...] = jnp.zeros_like(dw_ref)

        xe = jnp.concatenate([jnp.where(i == 0, 0.0, xb_ref[...]), x_ref[...], xa_ref[...]], axis=0)
        de = jnp.concatenate([d_ref[...], da_ref[...]], axis=0)
        n = tr + 8
        c = _conv_rows(xe, w_ref, n)
        sg = _sigmoid(c)
        s = c * sg
        if norm:
            r = lax.rsqrt(jnp.sum(s * s, axis=-1, keepdims=True) + EPS)
            ds = scale * r * (de - s * (r * r) * jnp.sum(de * s, axis=-1, keepdims=True))
        else:
            ds = de
        dc = ds * (sg + c * sg * (1.0 - sg))
        rowi = lax.broadcasted_iota(jnp.int32, (n, HEAD), 0)
        dc = jnp.where((i == nrow - 1) & (rowi >= tr), 0.0, dc)
        dct = dc[:tr, :]
        dx = None
        for j in range(GDN_CONV):
            tap = pl.ds(GDN_CONV - 1 - j, 1)
            up = dct if j == 0 else pltpu.roll(dc, n - j, 0)[:tr, :]
            term = w_ref[tap, :] * up
            dx = term if dx is None else dx + term
            dw_ref[tap, :] += jnp.sum(dct * _rows_down(xe, j)[8:8 + tr, :], axis=0, keepdims=True)
        dx_ref[...] = dx

    return pl.pallas_call(
        body, name=name, grid=(nblk, nrow),
        in_specs=[pl.BlockSpec((tr, HEAD), lambda j, i: (i, blk0 + j)),
                  pl.BlockSpec((8, HEAD), lambda j, i: (jnp.maximum(i * nb8 - 1, 0), blk0 + j)),
                  pl.BlockSpec((8, HEAD), lambda j, i: (jnp.minimum((i + 1) * nb8, last8), blk0 + j)),
                  pl.BlockSpec((GDN_CONV, HEAD), lambda j, i: (0, j)),
                  pl.BlockSpec((tr, HEAD), lambda j, i: (i, j)),
                  pl.BlockSpec((8, HEAD), lambda j, i: (jnp.minimum((i + 1) * nb8, last8), j))],
        out_specs=[pl.BlockSpec((tr, HEAD), lambda j, i: (i, j)),
                   pl.BlockSpec((GDN_CONV, HEAD), lambda j, i: (0, j))],
        out_shape=[jax.ShapeDtypeStruct((S, nblk * HEAD), F32),
                   jax.ShapeDtypeStruct((GDN_CONV, nblk * HEAD), F32)],
        compiler_params=_cp(("parallel", "arbitrary")),
    )(proj, proj, proj, convw, dout, dout)


def _gdn_gates(pg, alog, dtb):
    S = pg.shape[0]

    def body(a_ref, b_ref, al_ref, dt_ref, gc_ref, be_ref):
        g = -jnp.exp(al_ref[...]) * _softplus(a_ref[...] + dt_ref[...])
        rowm = lax.broadcasted_iota(jnp.int32, g.shape, 0) & (GDN_CHUNK - 1)
        c = g
        d = 1
        while d < GDN_CHUNK:
            c = c + jnp.where(rowm >= d, pltpu.roll(c, d, 0), 0.0)
            d *= 2
        gc_ref[...] = c
        be_ref[...] = _sigmoid(b_ref[...])

    blk = pl.BlockSpec((S, 128), lambda i: (0, 0))
    vec = pl.BlockSpec((1, 128), lambda i: (0, 0))
    return pl.pallas_call(
        body, name="gdn_gates", grid=(1,),
        in_specs=[blk, pl.BlockSpec((S, 128), lambda i: (0, 1)), vec, vec],
        out_specs=[blk, blk],
        out_shape=[jax.ShapeDtypeStruct((S, 128), F32)] * 2,
        compiler_params=_cp(("arbitrary",)),
    )(pg, pg, alog, dtb)


def _gdn_gates_bwd(pg, alog, dtb, dgc, dbeta):
    S = pg.shape[0]

    def body(a_ref, b_ref, al_ref, dt_ref, dgc_ref, dbe_ref, dpa_ref, dpb_ref, dal_ref, ddt_ref):
        rowm = lax.broadcasted_iota(jnp.int32, (S, 128), 0) & (GDN_CHUNK - 1)
        c = dgc_ref[...]
        d = 1
        while d < GDN_CHUNK:
            c = c + jnp.where(rowm < GDN_CHUNK - d, pltpu.roll(c, S - d, 0), 0.0)
            d *= 2
        xv = a_ref[...] + dt_ref[...]
        ea = jnp.exp(al_ref[...])
        g = -ea * _softplus(xv)
        dx = c * (-ea) * _sigmoid(xv)
        dpa_ref[...] = dx
        dal_ref[...] = jnp.sum(c * g, axis=0, keepdims=True)
        ddt_ref[...] = jnp.sum(dx, axis=0, keepdims=True)
        be = _sigmoid(b_ref[...])
        dpb_ref[...] = dbe_ref[...] * be * (1.0 - be)

    blk = pl.BlockSpec((S, 128), lambda i: (0, 0))
    blk1 = pl.BlockSpec((S, 128), lambda i: (0, 1))
    vec = pl.BlockSpec((1, 128), lambda i: (0, 0))
    dpa, dpb, dal, ddt = pl.pallas_call(
        body, name="gdn_gates_bwd", grid=(1,),
        in_specs=[blk, blk1, vec, vec, blk, blk],
        out_specs=[blk, blk, vec, vec],
        out_shape=[jax.ShapeDtypeStruct((S, 128), F32)] * 2 + [jax.ShapeDtypeStruct((1, 128), F32)] * 2,
        compiler_params=_cp(("arbitrary",)),
    )(pg, pg, alog, dtb, dgc, dbeta)
    return jnp.concatenate([dpa, dpb], axis=1), dal, ddt


def _gdn_pre(qs, ks, vs, gcs, grs, betas):
    C = GDN_CHUNK
    n = len(qs)
    r = lax.broadcasted_iota(jnp.int32, (C, C), 0)
    c_ = lax.broadcasted_iota(jnp.int32, (C, C), 1)
    lower, strict = r >= c_, r > c_
    eye = jnp.where(r == c_, 1.0, 0.0)
    decs = [jnp.exp(jnp.where(lower, gcs[i] - grs[i], -jnp.inf)) for i in range(n)]
    kbs = [ks[i] * betas[i] for i in range(n)]
    vbs = [vs[i] * betas[i] for i in range(n)]
    lmats = [jnp.where(strict, _dot(kbs[i], ks[i], NT) * decs[i], 0.0) for i in range(n)]
    amats = [jnp.where(lower, _dot(qs[i], ks[i], NT) * decs[i], 0.0) for i in range(n)]
    pks = [-lm for lm in lmats]
    tinvs = [eye + pk for pk in pks]
    for _ in range(5):
        pks = [_dotf(pk, pk, NN) for pk in pks]
        tinvs = [tv + _dotf(tv, pk, NN) for tv, pk in zip(tinvs, pks)]
    es = [jnp.exp(gc) for gc in gcs]
    glasts = [gc[C - 1:C, :] for gc in gcs]
    fs = [jnp.exp(gl - gc) for gl, gc in zip(glasts, gcs)]
    gls = [jnp.exp(gl) for gl in glasts]
    us = [_dotf(tinvs[i], vbs[i], NN) for i in range(n)]
    ws = [_dotf(tinvs[i], kbs[i] * es[i], NN) for i in range(n)]
    return [dict(lower=lower, strict=strict, dec=decs[i], kb=kbs[i], vb=vbs[i], lmat=lmats[i], tinv=tinvs[i],
                 e=es[i], f=fs[i], gl=gls[i], u=us[i], w=ws[i], amat=amats[i], qd=qs[i] * es[i],
                 kd=ks[i] * fs[i]) for i in range(n)]


def _gdn_heads_per_step(H):
    return max(d for d in (1, 2, 3, 4) if H % d == 0)


def _gdn_chunk_fwd(q, k, v, gcol, grow, bcol):
    S = q.shape[0]
    H, NC = gcol.shape[0], gcol.shape[1]
    C = GDN_CHUNK
    hb = _gdn_heads_per_step(H)

    def body(q_ref, k_ref, v_ref, gc_ref, gr_ref, b_ref, o_ref, st_ref, state):
        n = pl.program_id(1)

        @pl.when(n == 0)
        def _():
            state[...] = jnp.zeros_like(state)

        cols = [slice(i * HEAD, (i + 1) * HEAD) for i in range(hb)]
        ps = _gdn_pre([q_ref[:, c] for c in cols], [k_ref[:, c] for c in cols], [v_ref[:, c] for c in cols],
                      [gc_ref[i] for i in range(hb)], [gr_ref[i] for i in range(hb)],
                      [b_ref[i] for i in range(hb)])
        s0s = [state[i] for i in range(hb)]
        vns = [ps[i]['u'] - _dot(ps[i]['w'], s0s[i], NN) for i in range(hb)]
        outs = [_dot(ps[i]['qd'], s0s[i], NN) + _dot(ps[i]['amat'], vns[i], NN) for i in range(hb)]
        news = [s0s[i] * ps[i]['gl'] + _dot(ps[i]['kd'], vns[i], TN) for i in range(hb)]
        for i in range(hb):
            st_ref[i] = s0s[i]
            o_ref[:, cols[i]] = outs[i]
            state[i] = news[i]

    tok = pl.BlockSpec((C, hb * HEAD), lambda h, n: (n, h))
    col = pl.BlockSpec((hb, None, C, 1), lambda h, n: (h, n, 0, 0))
    rowb = pl.BlockSpec((hb, None, 1, C), lambda h, n: (h, n, 0, 0))
    return pl.pallas_call(
        body, name="gdn_chunk_fwd", grid=(H // hb, NC),
        in_specs=[tok, tok, tok, col, rowb, col],
        out_specs=[tok, pl.BlockSpec((hb, None, HEAD, HEAD), lambda h, n: (h, n, 0, 0))],
        out_shape=[jax.ShapeDtypeStruct((S, H * HEAD), F32), jax.ShapeDtypeStruct((H, NC, HEAD, HEAD), F32)],
        scratch_shapes=[pltpu.VMEM((hb, HEAD, HEAD), F32)],
        compiler_params=_cp(("parallel", "arbitrary")),
    )(q, k, v, gcol, grow, bcol)


def _gdn_chunk_bwd(q, k, v, gcol, grow, bcol, st, do):
    S = q.shape[0]
    H, NC = gcol.shape[0], gcol.shape[1]
    C = GDN_CHUNK
    hb = _gdn_heads_per_step(H)

    def body(q_ref, k_ref, v_ref, gc_ref, gr_ref, b_ref, st_ref, do_ref,
             dq_ref, dk_ref, dv_ref, dgc_ref, dbe_ref, dstate):
        n = pl.program_id(1)

        @pl.when(n == 0)
        def _():
            dstate[...] = jnp.zeros_like(dstate)

        R = range(hb)
        cols = [slice(i * HEAD, (i + 1) * HEAD) for i in R]
        qs, ks, vs = [q_ref[:, c] for c in cols], [k_ref[:, c] for c in cols], [v_ref[:, c] for c in cols]
        betas = [b_ref[i] for i in R]
        ps = _gdn_pre(qs, ks, vs, [gc_ref[i] for i in R], [gr_ref[i] for i in R], betas)
        lower, strict = ps[0]['lower'], ps[0]['strict']
        s0s, dos, ds1s = [st_ref[i] for i in R], [do_ref[:, c] for c in cols], [dstate[i] for i in R]
        vns = [ps[i]['u'] - _dot(ps[i]['w'], s0s[i], NN) for i in R]
        dvns = [_dot(ps[i]['amat'], dos[i], TN) + _dot(ps[i]['kd'], ds1s[i], NN) for i in R]
        damats = [jnp.where(lower, _dot(dos[i], vns[i], NT), 0.0) for i in R]
        dqds = [_dot(dos[i], s0s[i], NT) for i in R]
        dkds = [_dot(vns[i], ds1s[i], NT) for i in R]
        dgls = [jnp.sum(s0s[i] * ds1s[i], keepdims=True) for i in R]
        ds0s = [ps[i]['gl'] * ds1s[i] + _dot(ps[i]['qd'], dos[i], TN) - _dot(ps[i]['w'], dvns[i], TN) for i in R]
        dws = [-_dot(dvns[i], s0s[i], NT) for i in R]
        dvbs = [_dotf(ps[i]['tinv'], dvns[i], TN) for i in R]
        dkgs = [_dotf(ps[i]['tinv'], dws[i], TN) for i in R]
        dls = [-jnp.where(strict, _dotf(dvbs[i], ps[i]['u'], NT) + _dotf(dkgs[i], ps[i]['w'], NT), 0.0) for i in R]
        dkks = [dls[i] * ps[i]['dec'] for i in R]
        dqks = [damats[i] * ps[i]['dec'] for i in R]
        ms = [dls[i] * ps[i]['lmat'] + damats[i] * ps[i]['amat'] for i in R]
        dkbs = [_dot(dkks[i], ks[i], NN) + dkgs[i] * ps[i]['e'] for i in R]
        dks = [_dot(dkks[i], ps[i]['kb'], TN) + _dot(dqks[i], qs[i], TN) + dkds[i] * ps[i]['f'] + dkbs[i] * betas[i]
               for i in R]
        dqs = [_dot(dqks[i], ks[i], NN) + dqds[i] * ps[i]['e'] for i in R]
        ones = jnp.ones((C, HEAD), F32)
        colsums = [_dotf(ms[i], ones, TN)[:, 0:1] for i in R]
        rowi = lax.broadcasted_iota(jnp.int32, (C, 1), 0)
        for i in R:
            p = ps[i]
            de = (jnp.sum(dkgs[i] * p['kb'], axis=-1, keepdims=True)
                  + jnp.sum(dqds[i] * qs[i], axis=-1, keepdims=True))
            df = jnp.sum(dkds[i] * ks[i], axis=-1, keepdims=True)
            dgc = jnp.sum(ms[i], axis=-1, keepdims=True) - colsums[i] + de * p['e'] - df * p['f']
            dlast = jnp.sum(df * p['f'], keepdims=True) + dgls[i] * p['gl']
            dgc_ref[i] = dgc + jnp.where(rowi == C - 1, dlast, 0.0)
            dbe_ref[i] = (jnp.sum(dkbs[i] * ks[i], axis=-1, keepdims=True)
                          + jnp.sum(dvbs[i] * vs[i], axis=-1, keepdims=True))
            dstate[i] = ds0s[i]
            dq_ref[:, cols[i]] = dqs[i]
            dk_ref[:, cols[i]] = dks[i]
            dv_ref[:, cols[i]] = dvbs[i] * betas[i]

    def rn(n):
        return NC - 1 - n

    tok = pl.BlockSpec((C, hb * HEAD), lambda h, n: (rn(n), h))
    col = pl.BlockSpec((hb, None, C, 1), lambda h, n: (h, rn(n), 0, 0))
    rowb = pl.BlockSpec((hb, None, 1, C), lambda h, n: (h, rn(n), 0, 0))
    return pl.pallas_call(
        body, name="gdn_chunk_bwd", grid=(H // hb, NC),
        in_specs=[tok, tok, tok, col, rowb, col,
                  pl.BlockSpec((hb, None, HEAD, HEAD), lambda h, n: (h, rn(n), 0, 0)), tok],
        out_specs=[tok, tok, tok, col, col],
        out_shape=[jax.ShapeDtypeStruct((S, H * HEAD), F32)] * 3
        + [jax.ShapeDtypeStruct((H, NC, C, 1), F32)] * 2,
        scratch_shapes=[pltpu.VMEM((hb, HEAD, HEAD), F32)],
        compiler_params=_cp(("parallel", "arbitrary")),
    )(q, k, v, gcol, grow, bcol, st, do)


def _gdn_onorm(o, proj, gate_blk, w, H):
    S = o.shape[0]
    tr = _tile(S, CONV_TILE, 8)

    def body(o_ref, g_ref, w_ref, out_ref):
        ov, gv = o_ref[...], g_ref[...]
        r = lax.rsqrt(jnp.mean(ov * ov, axis=-1, keepdims=True) + EPS)
        out_ref[...] = (ov * r * w_ref[...]) * (gv * _sigmoid(gv))

    return pl.pallas_call(
        body, name="gdn_onorm", grid=(S // tr, H),
        in_specs=[pl.BlockSpec((tr, HEAD), lambda i, h: (i, h)),
                  pl.BlockSpec((tr, HEAD), lambda i, h: (i, gate_blk + h)),
                  pl.BlockSpec((1, HEAD), lambda i, h: (0, 0))],
        out_specs=pl.BlockSpec((tr, HEAD), lambda i, h: (i, h)),
        out_shape=jax.ShapeDtypeStruct((S, H * HEAD), F32),
        compiler_params=_cp(("parallel", "parallel")),
    )(o, proj, w)


def _gdn_onorm_bwd(dcat, o, proj, gate_blk, w, H):
    S = o.shape[0]
    tr = _tile(S, CONV_TILE, 8)

    def body(d_ref, o_ref, g_ref, w_ref, do_ref, dg_ref, dw_ref):
        i, h = pl.program_id(0), pl.program_id(1)

        @pl.when((i == 0) & (h == 0))
        def _():
            dw_ref[...] = jnp.zeros_like(dw_ref)

        dm, ov, gv, wv = d_ref[...], o_ref[...], g_ref[...], w_ref[...]
        r = lax.rsqrt(jnp.mean(ov * ov, axis=-1, keepdims=True) + EPS)
        oh = ov * r
        sg = gv * _sigmoid(gv)
        dy = dm * sg
        t = dy * wv
        do_ref[...] = r * (t - oh * jnp.mean(t * oh, axis=-1, keepdims=True))
        dg_ref[...] = dm * (oh * wv) * _silu_grad(gv)
        dw_ref[...] += jnp.sum(dy * oh, axis=0, keepdims=True)

    tok = pl.BlockSpec((tr, HEAD), lambda i, h: (i, h))
    vec = pl.BlockSpec((1, HEAD), lambda i, h: (0, 0))
    return pl.pallas_call(
        body, name="gdn_onorm_bwd", grid=(S // tr, H),
        in_specs=[tok, tok, pl.BlockSpec((tr, HEAD), lambda i, h: (i, gate_blk + h)), vec],
        out_specs=[tok, tok, vec],
        out_shape=[jax.ShapeDtypeStruct((S, H * HEAD), F32)] * 2 + [jax.ShapeDtypeStruct((1, HEAD), F32)],
        compiler_params=_cp(("arbitrary", "arbitrary")),
    )(dcat, o, proj, w)


def _lanes_to_heads(a, H):
    return a[:, :H].T


def _heads_to_lanes(a):
    H = a.shape[0]
    return jnp.pad(a.T, ((0, 0), (0, 128 - H)))


def _take_cols(segs, a, b):
    out, off = [], 0
    for sg in segs:
        w = sg.shape[-1]
        lo, hi = max(a, off), min(b, off + w)
        if lo < hi:
            out.append(sg[..., lo - off:hi - off])
        off += w
    return out


def _pad_cols(pieces):
    m = jnp.concatenate(pieces, axis=-1)
    return jnp.pad(m, ((0, 0), (0, 128 - m.shape[-1])))


def _pad_lanes(v):
    return jnp.pad(v.reshape(1, -1), ((0, 0), (0, 128 - v.shape[-1])))


def _s5_layer_fwd(a, w, cfg):
    proj = _mm(a, w['w_in'], name="s5_in")
    wb, wc, _ = w['prep']
    pr, pi, prr, pir = w['tables']
    v, yg, hs, cin = _s5_scan_fwd(proj, wb, wc, pr, pi, w['d_skip'])
    z, mix = _mm(yg, w['w_glu'], name="s5_glu", extras=[(yg, 'ij'), (w['b_glu'], 'j')],
                 epi=lambda acc, y, b: (acc + b, y * _sigmoid(acc + b)), out_dtypes=(F32, F32))
    return proj, mix, dict(v=v, yg=yg, hs=hs, cin=cin, z=z)


def _s5_layer_bwd(a, w, proj, sv, dcat, dmemq, cfg):
    wb, wc, _ = w['prep']
    pr, pi, prr, pir = w['tables']
    dz, dyg1, db_glu = _s5_glu_bwd(dcat, sv['yg'], sv['z'])
    dw_glu = _mm(sv['yg'], dz, name="s5_dwglu", ta=True)
    dv = _mm(dz, w['w_glu'], name="s5_dyg", tb=True, extras=[(dyg1, 'ij'), (sv['v'], 'ij')],
             epi=lambda acc, d1, vv: ((acc + d1) * _gelu_grad(vv),))
    du, dwb, dwc, da, dd = _s5_scan_bwd(dv, proj, sv['hs'], sv['cin'], wb, wc, pr, pi, prr, pir, w['d_skip'])
    dproj = jnp.concatenate([du, dmemq], axis=1).astype(MXU_DTYPE)
    dw_in = _mm(a, dproj, name="s5_dwin", ta=True)
    da_in = _mm(dproj, w['w_in'], name="s5_da", tb=True)
    dlre, dlim, dldt, dbre, dbim, dcre, dcim = w['prep_vjp']((dwb, dwc, da))
    grads = dict(w_in=dw_in, w_glu=dw_glu, b_glu=db_glu[0], d_skip=dd[0], lam_re=dlre, lam_im=dlim,
                 log_dt=dldt, b_re=dbre, b_im=dbim, c_re=dcre, c_im=dcim)
    return da_in, grads


def _gdn_relayout(a, H, NC):
    t = _lanes_to_heads(a, H).reshape(H, NC, GDN_CHUNK)
    return t[..., None], t[:, :, None, :]


def _gdn_layer_fwd(a, w, cfg):
    H, MIX, S = cfg['H'], cfg['MIX'], a.shape[0]
    NC = S // GDN_CHUNK
    proj = _mm(a, w['w_main'], name="gdn_in")
    pg = _mm(a, w['w_gate'], name="gdn_in_gates")
    cw = w['conv_w']
    q = _gdn_prep(proj, 0, H, cw[:, :MIX], True, HEAD ** -0.5, "gdn_prep_q")
    k = _gdn_prep(proj, H, H, cw[:, MIX:2 * MIX], True, 1.0, "gdn_prep_k")
    v = _gdn_prep(proj, 2 * H, H, cw[:, 2 * MIX:], False, 1.0, "gdn_prep_v")
    gc, beta = _gdn_gates(pg, w['a_log'], w['dt_bias'])
    gcol, grow = _gdn_relayout(gc, H, NC)
    bcol, _ = _gdn_relayout(beta, H, NC)
    o, st = _gdn_chunk_fwd(q, k, v, gcol, grow, bcol)
    mix = _gdn_onorm(o, proj, 3 * H, w['o_norm'], H)
    return proj, mix, dict(pg=pg, q=q, k=k, v=v, gcol=gcol, grow=grow, bcol=bcol, o=o, st=st)


def _gdn_layer_bwd(a, w, proj, sv, dcat, dmemq, cfg):
    H, MIX, S = cfg['H'], cfg['MIX'], a.shape[0]
    cw = w['conv_w']
    do, dgate, donorm = _gdn_onorm_bwd(dcat, sv['o'], proj, 3 * H, w['o_norm'], H)
    dq, dk, dv, dgcol, dbcol = _gdn_chunk_bwd(sv['q'], sv['k'], sv['v'], sv['gcol'], sv['grow'], sv['bcol'],
                                              sv['st'], do)
    dgc = _heads_to_lanes(dgcol.reshape(H, S))
    dbeta = _heads_to_lanes(dbcol.reshape(H, S))
    dpg, dalog, ddtb = _gdn_gates_bwd(sv['pg'], w['a_log'], w['dt_bias'], dgc, dbeta)
    dxq, dwq = _gdn_prep_bwd(proj, 0, H, cw[:, :MIX], dq, True, HEAD ** -0.5, "gdn_prep_bwd_q")
    dxk, dwk = _gdn_prep_bwd(proj, H, H, cw[:, MIX:2 * MIX], dk, True, 1.0, "gdn_prep_bwd_k")
    dxv, dwv = _gdn_prep_bwd(proj, 2 * H, H, cw[:, 2 * MIX:], dv, False, 1.0, "gdn_prep_bwd_v")
    dproj = jnp.concatenate([dxq, dxk, dxv, dgate, dmemq], axis=1).astype(MXU_DTYPE)
    dw_main = _mm(a, dproj, name="gdn_dwmain", ta=True)
    dw_gate = _mm(a, dpg, name="gdn_dwgate", ta=True)
    da1 = _mm(dpg, w['w_gate'], name="gdn_da_gates", tb=True)
    da_in = _mm(dproj, w['w_main'], name="gdn_da", tb=True, extras=[(da1, 'ij')], epi=lambda acc, e: (acc + e,))
    grads = dict(w_main=dw_main, w_gate=dw_gate, conv_w=jnp.concatenate([dwq, dwk, dwv], axis=1),
                 a_log=dalog[0, :H], dt_bias=ddtb[0, :H], o_norm=donorm[0])
    return da_in, grads


def _fox_layer_fwd(a, w, cfg):
    H = cfg['H']
    proj = _mm(a, w['w_main'], name="fox_in")
    pg = _mm(a, w['w_gate'], name="fox_in_gates")
    cf = _fox_gates(pg, w['b_f'])
    cfh = _lanes_to_heads(cf, H)
    cfq, cfk = cfh[:, :, None], cfh[:, None, :]
    o, lse = _fox_fwd(proj, cfq, cfk, H)
    return proj, o, dict(pg=pg, cfq=cfq, cfk=cfk, lse=lse)


def _fox_layer_bwd(a, w, proj, sv, dcat, dmemq, cfg):
    H = cfg['H']
    rowdot = _fox_bwd_rowdot(proj, sv['cfq'], sv['cfk'], sv['lse'], dcat, H)
    dq, dk, dv, dck = _fox_bwd(proj, sv['cfq'], sv['cfk'], rowdot, sv['lse'], dcat, H)
    dpg, dbf = _fox_gates_bwd(sv['pg'], w['b_f'], _heads_to_lanes(dck[:, 0, :]))
    dproj = jnp.concatenate([dq, dk, dv, dmemq], axis=1).astype(MXU_DTYPE)
    dw_main = _mm(a, dproj, name="fox_dwmain", ta=True)
    dw_gate = _mm(a, dpg, name="fox_dwgate", ta=True)
    da1 = _mm(dpg, w['w_gate'], name="fox_da_gates", tb=True)
    da_in = _mm(dproj, w['w_main'], name="fox_da", tb=True, extras=[(da1, 'ij')], epi=lambda acc, e: (acc + e,))
    grads = dict(w_main=dw_main, w_gate=dw_gate, b_f=dbf[0, :H])
    return da_in, grads


_LAYER_FWD = (_s5_layer_fwd, _gdn_layer_fwd, _fox_layer_fwd)
_LAYER_BWD = (_s5_layer_bwd, _gdn_layer_bwd, _fox_layer_bwd)


def _mixer_weights(kind, j, fw, p, cfg, after):
    H, MIX, MW = cfg['H'], cfg['MIX'], cfg['MW']
    if kind == 0:
        params = tuple(p[n][j] for n in ('s5_lam_re', 's5_lam_im', 's5_log_dt', 's5_b_re', 's5_b_im',
                                         's5_c_re', 's5_c_im'))
        prep, prep_vjp = jax.vjp(_s5_prep, *params)
        prep = (prep[0].astype(MXU_DTYPE), prep[1].astype(MXU_DTYPE), prep[2])
        tables = _s5_tables(*params[:3])
        return dict(w_in=fw.get('s5_w_in', j, after), w_glu=fw.get('s5_w_glu', j, after),
                    b_glu=fw.get('s5_b_glu', j, after), d_skip=fw.get('s5_d_skip', j, after).reshape(1, MIX),
                    prep=prep, prep_vjp=prep_vjp, tables=tables)
    if kind == 1:
        segs = fw.get('gdn_w_in', j, after)
        c0 = 4 * MIX
        total = c0 + 2 * H + MW
        w_main = jnp.concatenate(_take_cols(segs, 0, c0) + _take_cols(segs, c0 + 2 * H, total), axis=1)
        w_gate = jnp.concatenate([_pad_cols(_take_cols(segs, c0, c0 + H)),
                                  _pad_cols(_take_cols(segs, c0 + H, c0 + 2 * H))], axis=1)
        return dict(w_main=w_main, w_gate=w_gate, conv_w=fw.get('gdn_conv_w', j, after),
                    a_log=_pad_lanes(p['gdn_a_log'][j]), dt_bias=_pad_lanes(p['gdn_dt_bias'][j]),
                    o_norm=p['gdn_o_norm'][j].reshape(1, HEAD))
    segs = fw.get('fox_w_in', j, after)
    c0 = 3 * MIX
    total = c0 + H + MW
    w_main = jnp.concatenate(_take_cols(segs, 0, c0) + _take_cols(segs, c0 + H, total), axis=1)
    w_gate = _pad_cols(_take_cols(segs, c0, c0 + H))
    return dict(w_main=w_main, w_gate=w_gate, b_f=_pad_lanes(p['fox_b_f'][j]))


class _Weights:
    def __init__(self, resolve):
        self._resolve, self._have = resolve, {}

    def get(self, name, layer, after):
        if (name, layer) not in self._have:
            self._have[name, layer] = self._resolve(name, layer, after)
        return self._have[name, layer]


def _local_step(p, fw, cfg, on_grad=None):
    H, MIX, MW, MH, depth = cfg['H'], cfg['MIX'], cfg['MW'], cfg['MH'], cfg['depth']
    x, mem, target = p['x'], p['mem'], p['loss_target']
    q_blk = {0: MIX // HEAD, 1: 4 * MIX // HEAD, 2: 3 * MIX // HEAD}
    zero = jnp.zeros((), F32)
    tok = [zero]

    def told(name, layer, value):
        if on_grad is not None:
            tok[0] = tok[0] + on_grad(name, layer, value)
        return value

    mem_n = _rms_fwd(mem, p['mem_norm'], MXU_DTYPE, "mem_rms")
    w_kv = fw.get('w_mem_kv', 0, mem_n)
    mkv = _mm(mem_n, w_kv, name="mem_kv")

    h = x
    saved = []
    for i in range(depth):
        kind, j = i % 3, i // 3
        a = _rms_fwd(h, p['norm1'][i], MXU_DTYPE, "rms1")
        w = _mixer_weights(kind, j, fw, p, cfg, a)
        proj, mix, sv = _LAYER_FWD[kind](a, w, cfg)
        read = _mem_fwd(proj, q_blk[kind], mkv, MH)
        cat = jnp.concatenate([mix, read], axis=1).astype(MXU_DTYPE)
        w_out, w_up = fw.get('w_out', i, proj), fw.get('w_up', i, proj)
        h1 = _mm(cat, w_out, name="out_proj", extras=[(h, 'ij')], epi=lambda acc, r: (acc + r,))
        a2 = _rms_fwd(h1, p['norm2'][i], MXU_DTYPE, "rms2")
        act = _mm(a2, w_up, name="mlp_up", epi=lambda acc: (_relu2(acc),), out_dtypes=(MXU_DTYPE,))
        w_down = fw.get('w_down', i, h1)
        h2 = _mm(act, w_down, name="mlp_down", extras=[(h1, 'ij')], epi=lambda acc, r: (acc + r,))
        saved.append(dict(w=w, h=h, a=a, proj=proj, sv=sv, cat=cat, h1=h1, a2=a2, act=act,
                          w_out=w_out, w_up=w_up, w_down=w_down))
        h = h2

    loss, dh, dnorm_f, dh16 = _loss_head(h, p['norm_f'], target)

    g = {n: None for n in WEIGHTS}
    g['norm_f'] = dnorm_f[0]
    per_layer = {n: [None] * depth for n in ('norm1', 'norm2', 'w_out', 'w_up', 'w_down')}
    mix_grads = {0: {}, 1: {}, 2: {}}
    big = {0: (('s5_w_in', 'w_in'), ('s5_w_glu', 'w_glu')), 1: (), 2: ()}
    dmkv = None
    for i in reversed(range(depth)):
        kind, j = i % 3, i // 3
        s = saved[i]
        w = s['w']
        du = _mm(dh16, s['w_down'], name="mlp_ddown", tb=True, extras=[(s['act'], 'ij')],
                 epi=lambda acc, aa: (acc * (2.0 * jnp.sqrt(aa.astype(F32))),), out_dtypes=(MXU_DTYPE,))
        per_layer['w_down'][i] = told('w_down', i, _mm(s['act'], dh16, name="mlp_dwdown", ta=True))
        per_layer['w_up'][i] = told('w_up', i, _mm(s['a2'], du, name="mlp_dwup", ta=True))
        da2 = _mm(du, s['w_up'], name="mlp_dup", tb=True)
        dh1, dn2, dh1_16 = _rms_bwd(s['h1'], p['norm2'][i] + tok[0], da2, dh, "rms2_bwd")
        per_layer['norm2'][i] = dn2[0]
        dcat = _mm(dh1_16, s['w_out'], name="out_dproj", tb=True)
        per_layer['w_out'][i] = told('w_out', i, _mm(s['cat'], dh1_16, name="out_dw", ta=True))
        dmemq, dmkv_i = _mem_bwd(s['proj'], q_blk[kind], mkv, dcat, MIX // HEAD, MH)
        dmkv = dmkv_i if dmkv is None else dmkv + dmkv_i
        da, mg = _LAYER_BWD[kind](s['a'], w, s['proj'], s['sv'], dcat, dmemq, cfg)
        mix_grads[kind][j] = mg
        for name, key in big[kind]:
            told(name, j, mg[key])
        c0 = 4 * MIX
        if kind == 1:
            mg['segs'] = told('gdn_w_in', j, [mg['w_main'][:, :c0], mg['w_gate'][:, :H],
                                               mg['w_gate'][:, 128:128 + H], mg['w_main'][:, c0:]])
        c0 = 3 * MIX
        if kind == 2:
            mg['segs'] = told('fox_w_in', j, [mg['w_main'][:, :c0], mg['w_gate'][:, :H], mg['w_main'][:, c0:]])
        dh, dn1, dh16 = _rms_bwd(s['h'], p['norm1'][i] + tok[0], da, dh1, "rms1_bwd")
        per_layer['norm1'][i] = dn1[0]
    for n in ('norm1', 'norm2'):
        g[n] = jnp.stack(per_layer[n])
    for n in ('w_out', 'w_up', 'w_down'):
        g[n] = per_layer[n]

    g['w_mem_kv'] = told('w_mem_kv', 0, _mm(mem_n, dmkv, name="mem_dwkv", ta=True))
    dmem_n = _mm(dmkv, w_kv, name="mem_dn", tb=True)
    _, dmn, _ = _rms_bwd(mem, p['mem_norm'] + tok[0], dmem_n, None, "mem_rms_bwd")
    g['mem_norm'] = dmn[0]

    def layers(kind, key):
        return [mix_grads[kind][j][key] for j in sorted(mix_grads[kind])]

    g['s5_w_in'] = layers(0, 'w_in')
    g['s5_w_glu'] = layers(0, 'w_glu')
    for n in ('b_glu', 'd_skip', 'lam_re', 'lam_im', 'log_dt', 'b_re', 'b_im', 'c_re', 'c_im'):
        g['s5_' + n] = jnp.stack(layers(0, n))
    g['gdn_w_in'] = layers(1, 'segs')
    for n in ('conv_w', 'a_log', 'dt_bias', 'o_norm'):
        g['gdn_' + n] = jnp.stack(layers(1, n))
    g['fox_w_in'] = layers(2, 'segs')
    g['fox_b_f'] = jnp.stack(layers(2, 'b_f'))
    return loss, dh, g


def _ag_order(p):
    depth, order = p['norm1'].shape[0], [('w_mem_kv', 0)]
    for i in range(depth):
        kind, j = i % 3, i // 3
        order += [[('s5_w_in', j), ('s5_w_glu', j)], [('gdn_w_in', j)], [('fox_w_in', j)]][kind]
        order += [('w_out', i), ('w_up', i), ('w_down', i)]
    return order


def _gather_begin(p, me):
    vec = _gather(_pack_small([p[n] for n in VECTOR_SHARDED], 16), me, "all_gather_vectors").reshape(4, -1)
    first = (vec[0, 0] * 0.0).astype(MXU_DTYPE)
    handles, tok = {}, jnp.zeros((), F32)
    for name, layer in _ag_order(p):
        xs = (p[name] if p[name].ndim == 2 else p[name][layer]).astype(MXU_DTYPE) + first
        handle, t = _ag_start(xs, "ag_start_%s_%d" % (name, layer))
        handles[name, layer] = (handle, xs)
        tok = tok + t
    vectors, off = {}, 0
    for n in VECTOR_SHARDED:
        sz = p[n].size
        stacked = vec[:, off:off + sz].reshape((4,) + p[n].shape)
        ax = SHARD_AXIS[n]
        t = jnp.moveaxis(stacked, 0, ax)
        shp = list(t.shape)
        vectors[n] = t.reshape(shp[:ax] + [shp[ax] * shp[ax + 1]] + shp[ax + 2:])
        off += sz
    return handles, vectors, tok


def _gather_end(name, layer, after, me_idx, handles, vectors):
    if name in vectors:
        return vectors[name][layer]
    handle, xs = handles[name, layer]
    tag = "%s_%d" % (name, layer)
    got = _ag_wait(handle, after, "ag_wait_" + tag)
    fwd = _ag_forward_start(got, "ag_forward_start_" + tag)
    got = _ag_fill_own(fwd[2], xs, me_idx, "ag_fill_own")
    got = _ag_forward_wait((fwd[0], fwd[1], got), "ag_forward_wait_" + tag)
    r, n = xs.shape
    if name == 'w_mem_kv' or SHARD_AXIS[name] == 1:
        return got.reshape(4 * r, n)
    blocks = [got[s] for s in range(4)]
    return jnp.concatenate(blocks, axis=1) if name == 'w_up' else blocks


def _shard_blocks(name, value, p):
    shp = p[name].shape
    r, n = shp[-2], shp[-1]
    if SHARD_AXIS[name] == len(shp) - 2:
        return value.reshape(4, r, n)
    segs = value if isinstance(value, list) else [value]
    return jnp.stack([jnp.concatenate(_take_cols(segs, s * n, (s + 1) * n), axis=1) for s in range(4)])


def kernel(x, mem, mem_norm, w_mem_kv, norm1, w_out, norm2, w_up, w_down, norm_f, s5_w_in, s5_lam_re, s5_lam_im, s5_log_dt, s5_b_re, s5_b_im, s5_c_re, s5_c_im, s5_d_skip, s5_w_glu, s5_b_glu, gdn_w_in, gdn_conv_w, gdn_a_log, gdn_dt_bias, gdn_o_norm, fox_w_in, fox_b_f, loss_target, m_mem_norm, m_w_mem_kv, m_norm1, m_w_out, m_norm2, m_w_up, m_w_down, m_norm_f, m_s5_w_in, m_s5_lam_re, m_s5_lam_im, m_s5_log_dt, m_s5_b_re, m_s5_b_im, m_s5_c_re, m_s5_c_im, m_s5_d_skip, m_s5_w_glu, m_s5_b_glu, m_gdn_w_in, m_gdn_conv_w, m_gdn_a_log, m_gdn_dt_bias, m_gdn_o_norm, m_fox_w_in, m_fox_b_f, v_mem_norm, v_w_mem_kv, v_norm1, v_w_out, v_norm2, v_w_up, v_w_down, v_norm_f, v_s5_w_in, v_s5_lam_re, v_s5_lam_im, v_s5_log_dt, v_s5_b_re, v_s5_b_im, v_s5_c_re, v_s5_c_im, v_s5_d_skip, v_s5_w_glu, v_s5_b_glu, v_gdn_w_in, v_gdn_conv_w, v_gdn_a_log, v_gdn_dt_bias, v_gdn_o_norm, v_fox_w_in, v_fox_b_f):
    args = locals()
    p = {n: args[n] for n in WEIGHTS}
    mom = {n: args['m_' + n] for n in WEIGHTS}
    var = {n: args['v_' + n] for n in WEIGHTS}
    S, D = x.shape[1], x.shape[2]
    MW = w_mem_kv.shape[1] // 2
    MIX = D - MW
    cfg = dict(H=MIX // HEAD, MIX=MIX, MW=MW, MH=MW // HEAD, depth=norm1.shape[0])
    p.update(x=x.reshape(S, D), mem=mem.reshape(mem.shape[1], D), loss_target=loss_target.reshape(S, D))
    c = lax.axis_index("c")
    me = 2 * lax.axis_index("x") + lax.axis_index("y")
    place = dict(c=c, c_idx=c.astype(jnp.int32).reshape(1), me_idx=me.astype(jnp.int32).reshape(1))

    handles, vectors, tok = _gather_begin(p, me)
    p['mem_norm'] = mem_norm + tok
    fw = _Weights(lambda name, layer, after: _gather_end(name, layer, after, place['me_idx'], handles, vectors))

    pending, swapping = {}, []

    def exchange(after):
        name, layer, swap = swapping.pop()
        pending[name, layer], t = _rs_begin(swap, after, place, "%s_%d" % (name, layer))
        return t

    def on_grad(name, layer, value):
        swap, t = _rs_swap_start(_shard_blocks(name, value, p), "rs_swap_start_%s_%d" % (name, layer))
        if swapping:
            t = t + exchange(value[0] if isinstance(value, list) else value)
        swapping.append((name, layer, swap))
        return t

    loss, dx, g = _local_step(p, fw, cfg, on_grad)
    p['mem_norm'] = mem_norm
    exchange(dx)

    grads = {}
    for name in MATMUL_WEIGHTS:
        shp = p[name].shape
        layers = [_rs_end(pending[name, i], dx, place, "%s_%d" % (name, i))
                  for i in range(1 if len(shp) == 2 else shp[0])]
        grads[name] = layers[0] if len(shp) == 2 else jnp.stack(layers)

    parts = []
    for name in VECTOR_SHARDED:
        ax = SHARD_AXIS[name]
        shp = list(g[name].shape)
        t = g[name].reshape(shp[:ax] + [4, shp[ax] // 4] + shp[ax + 1:])
        parts.append(jnp.moveaxis(t, ax, 0).reshape(4, -1))
    flat = jnp.concatenate(parts, axis=1)
    flat = jnp.pad(flat, ((0, 0), (0, 16 * LANES - flat.shape[1]))).reshape(4, 16, LANES)
    swap, _ = _rs_swap_start(flat, "rs_swap_start_vectors")
    handle, _ = _rs_begin(swap, dx, place, "vectors")
    red = _rs_end(handle, dx, place, "vectors").reshape(-1)
    off = 0
    for name in VECTOR_SHARDED:
        grads[name] = red[off:off + p[name].size].reshape(p[name].shape)
        off += p[name].size

    n_small = sum(p[n].size for n in REPLICATED)
    rows = -(-n_small // LANES // 8) * 8
    small = _all_reduce_small(_pack_small([g[n] for n in REPLICATED], rows), "all_reduce_small").reshape(-1)
    off = 0
    for n in REPLICATED:
        grads[n] = small[off:off + p[n].size].reshape(p[n].shape)
        off += p[n].size

    delta, new_m, new_v = {}, {}, {}
    for n in SHARD_AXIS:
        shp = p[n].shape
        two_d = (-1, shp[-1])
        d, nm, nv = _adamw(p[n].reshape(two_d), grads[n].reshape(two_d), mom[n].reshape(two_d),
                           var[n].reshape(two_d), "adamw_" + n)
        delta[n], new_m[n], new_v[n] = d.reshape(shp), nm.reshape(shp), nv.reshape(shp)
    d, nm, nv = _adamw(*[_pack_small([src[n] for n in REPLICATED], rows) for src in (p, grads, mom, var)],
                       "adamw_small")
    d, nm, nv = d.reshape(-1), nm.reshape(-1), nv.reshape(-1)
    off = 0
    for n in REPLICATED:
        sz, shp = p[n].size, p[n].shape
        delta[n], new_m[n], new_v[n] = (d[off:off + sz].reshape(shp), nm[off:off + sz].reshape(shp),
                                        nv[off:off + sz].reshape(shp))
        off += sz

    total = lax.psum(loss[0, 0], ("x", "y", "c"))
    return (total, dx.reshape(x.shape), *[grads[n] for n in WEIGHTS], *[delta[n] for n in WEIGHTS],
            *[new_m[n] for n in WEIGHTS], *[new_v[n] for n in WEIGHTS])
```
